```python
import jax, jax.numpy as jnp
from jax import lax
import numpy as np

D_MODEL = 1024
BATCH = 8
SEQ = 4096
DEPTH = 2

CHUNK = 64
CONV_CH = 512
CONV_WIDTH = 31
HG_HEADS = 4
HG_DK = 128
HG_DV = 128
HG_WIDTH = HG_HEADS * HG_DK
SB_HEADS = 8
SB_DH = 64
SB_WIDTH = SB_HEADS * SB_DH
N_BRANCH = 3
D_FF = 4 * D_MODEL
QBLK = 128
EPS = 1e-6

IN_SIZES = (CONV_CH, CONV_CH, HG_WIDTH, HG_WIDTH, HG_HEADS * HG_DV, HG_HEADS * HG_DV,
            SB_WIDTH, SB_WIDTH, SB_WIDTH, N_BRANCH * D_MODEL)
D_IN = int(sum(IN_SIZES))
SPLIT_IDX = [int(v) for v in np.cumsum(IN_SIZES)[:-1]]

kernel_name = "hybrid_conv_hgrn2_stickbreak_block"


def rms_norm(x, g):
    xf = x.astype(jnp.float32)
    y = xf * lax.rsqrt(jnp.mean(xf * xf, axis=-1, keepdims=True) + EPS)
    return (y * g.astype(jnp.float32)).astype(x.dtype)


def layer_norm(x, g, b):
    xf = x.astype(jnp.float32)
    mu = jnp.mean(xf, axis=-1, keepdims=True)
    var = jnp.mean(jnp.square(xf - mu), axis=-1, keepdims=True)
    y = (xf - mu) * lax.rsqrt(var + EPS)
    return (y * g.astype(jnp.float32) + b.astype(jnp.float32)).astype(x.dtype)


def conv_branch(a, gate, w, b, ln_g, ln_b, w_proj):
    u = a * jax.nn.sigmoid(gate)
    u = lax.conv_general_dilated(
        u, w[:, None, :], window_strides=(1,), padding=((CONV_WIDTH - 1, 0),),
        dimension_numbers=('NWC', 'WIO', 'NWC'), feature_group_count=CONV_CH) + b
    u = jax.nn.silu(layer_norm(u, ln_g, ln_b))
    return u @ w_proj


def hgrn2_branch(q, f, i, g, lb, norm_g, w_proj):
    B, S, _ = q.shape
    n_chunks = S // CHUNK
    f32 = jnp.float32

    def heads(t, d):
        return t.reshape(B, n_chunks, CHUNK, HG_HEADS, d).transpose(1, 0, 3, 2, 4)

    k = (1.0 - lb.astype(f32)) * jax.nn.sigmoid(-f.astype(f32))
    log_f = jnp.log1p(-k)
    qh = heads(jax.nn.silu(q.astype(f32)), HG_DK)
    kh = heads(k, HG_DK)
    lfh = heads(log_f, HG_DK)
    vh = heads(i.astype(f32), HG_DV)
    causal = jnp.tril(jnp.ones((CHUNK, CHUNK), dtype=bool))[:, :, None]

    def step(state, inp):
        qc, kc, lc, vc = inp
        b = jnp.cumsum(lc, axis=2)
        o_inter = jnp.einsum('bhtk,bhkv->bhtv', qc * jnp.exp(b), state)
        rel = b[:, :, :, None, :] - b[:, :, None, :, :]
        decay = jnp.exp(jnp.where(causal, rel, -jnp.inf))
        scores = jnp.einsum('bhtsk,bhsk->bhts', qc[:, :, :, None, :] * decay, kc)
        o = o_inter + jnp.einsum('bhts,bhsv->bhtv', scores, vc)
        b_last = b[:, :, -1:, :]
        new_state = (jnp.exp(b_last[:, :, 0, :])[..., None] * state
                     + jnp.einsum('bhsk,bhsv->bhkv', kc * jnp.exp(b_last - b), vc))
        return new_state, o

    s0 = jnp.zeros((B, HG_HEADS, HG_DK, HG_DV), f32)
    _, o = lax.scan(step, s0, (qh, kh, lfh, vh))
    o = o.transpose(1, 0, 3, 2, 4).reshape(B, S, HG_HEADS, HG_DV)
    o = rms_norm(o, norm_g).reshape(B, S, HG_HEADS * HG_DV)
    o = (o * jax.nn.silu(g.astype(f32))).astype(q.dtype)
    return o @ w_proj


def stick_breaking_branch(q, k, v, qn_g, kn_g, w_proj):
    B, S, _ = q.shape
    qh = rms_norm(q.reshape(B, S, SB_HEADS, SB_DH), qn_g).transpose(0, 2, 1, 3)
    kh = rms_norm(k.reshape(B, S, SB_HEADS, SB_DH), kn_g).transpose(0, 2, 1, 3)
    vh = v.reshape(B, S, SB_HEADS, SB_DH).transpose(0, 2, 1, 3)
    scale = SB_DH ** -0.5
    outs = []
    for blk in range(S // QBLK):
        start, end = blk * QBLK, (blk + 1) * QBLK
        qb = qh[:, :, start:end]
        kb = kh[:, :, :end]
        vb = vh[:, :, :end]
        z = jnp.einsum('bhtd,bhsd->bhts', qb, kb).astype(jnp.float32) * scale
        t_pos = start + jnp.arange(QBLK)
        s_pos = jnp.arange(end)
        mask = s_pos[None, :] < t_pos[:, None]
        log_keep = jnp.where(mask, jax.nn.log_sigmoid(-z), 0.0)
        between = lax.cumsum(log_keep, axis=3, reverse=True) - log_keep
        a = jnp.where(mask, jnp.exp(jax.nn.log_sigmoid(z) + between), 0.0)
        outs.append(jnp.einsum('bhts,bhsd->bhtd', a.astype(vb.dtype), vb))
    o = jnp.concatenate(outs, axis=2).transpose(0, 2, 1, 3).reshape(B, S, SB_WIDTH)
    return o @ w_proj


def _fwd_setup_inputs(seed: int = 0) -> dict:
    key = jax.random.key(seed)
    ks = jax.random.split(key, 24)

    def nrm(k, shape, scale):
        return jax.random.normal(k, shape, jnp.float32) * scale

    L, D = DEPTH, D_MODEL
    return {
        "x": nrm(ks[0], (BATCH, SEQ, D), 1.0),
        "c": nrm(ks[1], (BATCH, D), 1.0),
        "mod_w": nrm(ks[2], (L, D, 6 * D), 0.5 * D ** -0.5),
        "mod_b": nrm(ks[3], (L, 6 * D), 0.01),
        "norm1_g": 1.0 + nrm(ks[4], (L, D), 0.02),
        "w_in": nrm(ks[5], (L, D, D_IN), D ** -0.5),
        "gate_b": nrm(ks[6], (L, N_BRANCH * D), 0.01),
        "conv_w": nrm(ks[7], (L, CONV_WIDTH, CONV_CH), CONV_WIDTH ** -0.5),
        "conv_b": nrm(ks[8], (L, CONV_CH), 0.01),
        "conv_ln_g": 1.0 + nrm(ks[9], (L, CONV_CH), 0.02),
        "conv_ln_b": nrm(ks[10], (L, CONV_CH), 0.01),
        "w_conv_proj": nrm(ks[11], (L, CONV_CH, D), CONV_CH ** -0.5),
        "hgrn_lb": nrm(ks[12], (L, HG_WIDTH), 0.5),
        "hgrn_norm_g": 1.0 + nrm(ks[13], (L, HG_DV), 0.02),
        "w_hgrn_proj": nrm(ks[14], (L, HG_HEADS * HG_DV, D), (HG_HEADS * HG_DV) ** -0.5),
        "sb_qn_g": 1.0 + nrm(ks[15], (L, SB_DH), 0.02),
        "sb_kn_g": 1.0 + nrm(ks[16], (L, SB_DH), 0.02),
        "w_sb_proj": nrm(ks[17], (L, SB_WIDTH, D), SB_WIDTH ** -0.5),
        "w_out": nrm(ks[18], (L, D, D), D ** -0.5),
        "norm2_g": 1.0 + nrm(ks[19], (L, D), 0.02),
        "mlp_w1": nrm(ks[20], (L, D, D_FF), D ** -0.5),
        "mlp_w2": nrm(ks[21], (L, D_FF, D), D_FF ** -0.5),
    }


def _fwd_reference(x, c, mod_w, mod_b, norm1_g, w_in, gate_b, conv_w, conv_b, conv_ln_g,
              conv_ln_b, w_conv_proj, hgrn_lb, hgrn_norm_g, w_hgrn_proj, sb_qn_g,
              sb_kn_g, w_sb_proj, w_out, norm2_g, mlp_w1, mlp_w2):
    B, S, D = x.shape
    p = jax.nn.softmax(hgrn_lb.astype(jnp.float32), axis=0)
    lower_bounds = jnp.cumsum(p, axis=0) - p[0:1]
    c_act = jax.nn.silu(c)
    for l in range(DEPTH):
        mod = c_act @ mod_w[l] + mod_b[l]
        sh1, sc1, g1, sh2, sc2, g2 = [m[:, None, :] for m in jnp.split(mod, 6, axis=-1)]

        h = rms_norm(x, norm1_g[l]) * (1.0 + sc1) + sh1
        proj = h @ w_in[l]
        (cv_a, cv_g, hg_q, hg_f, hg_i, hg_g, sb_q, sb_k, sb_v, gl) = jnp.split(proj, SPLIT_IDX, axis=-1)
        y_conv = conv_branch(cv_a, cv_g, conv_w[l], conv_b[l], conv_ln_g[l], conv_ln_b[l], w_conv_proj[l])
        y_hgrn = hgrn2_branch(hg_q, hg_f, hg_i, hg_g, lower_bounds[l], hgrn_norm_g[l], w_hgrn_proj[l])
        y_sb = stick_breaking_branch(sb_q, sb_k, sb_v, sb_qn_g[l], sb_kn_g[l], w_sb_proj[l])
        gates = jax.nn.sigmoid(gl + gate_b[l]).reshape(B, S, N_BRANCH, D)
        merged = gates[:, :, 0] * y_conv + gates[:, :, 1] * y_hgrn + gates[:, :, 2] * y_sb
        x = x + g1 * (merged @ w_out[l])

        h2 = rms_norm(x, norm2_g[l]) * (1.0 + sc2) + sh2
        x = x + g2 * (jnp.square(jax.nn.relu(h2 @ mlp_w1[l])) @ mlp_w2[l])
    return x


import jax as _jax
import jax.numpy as _jnp

TWIN_FORMAT = 'train_step'
FWD_PARAMS = ['x', 'c', 'mod_w', 'mod_b', 'norm1_g', 'w_in', 'gate_b', 'conv_w', 'conv_b', 'conv_ln_g', 'conv_ln_b', 'w_conv_proj', 'hgrn_lb', 'hgrn_norm_g', 'w_hgrn_proj', 'sb_qn_g', 'sb_kn_g', 'w_sb_proj', 'w_out', 'norm2_g', 'mlp_w1', 'mlp_w2']
TWIN_WEIGHTS = ['mod_w', 'mod_b', 'norm1_g', 'w_in', 'gate_b', 'conv_w', 'conv_b', 'conv_ln_g', 'conv_ln_b', 'w_conv_proj', 'hgrn_lb', 'hgrn_norm_g', 'w_hgrn_proj', 'sb_qn_g', 'sb_kn_g', 'w_sb_proj', 'w_out', 'norm2_g', 'mlp_w1', 'mlp_w2']
TWIN_DIFF_INPUT = 'x'
TWIN_INPUTS = ['x', 'c', 'mod_w', 'mod_b', 'norm1_g', 'w_in', 'gate_b', 'conv_w', 'conv_b', 'conv_ln_g', 'conv_ln_b', 'w_conv_proj', 'hgrn_lb', 'hgrn_norm_g', 'w_hgrn_proj', 'sb_qn_g', 'sb_kn_g', 'w_sb_proj', 'w_out', 'norm2_g', 'mlp_w1', 'mlp_w2', 'loss_target', 'm_mod_w', 'm_mod_b', 'm_norm1_g', 'm_w_in', 'm_gate_b', 'm_conv_w', 'm_conv_b', 'm_conv_ln_g', 'm_conv_ln_b', 'm_w_conv_proj', 'm_hgrn_lb', 'm_hgrn_norm_g', 'm_w_hgrn_proj', 'm_sb_qn_g', 'm_sb_kn_g', 'm_w_sb_proj', 'm_w_out', 'm_norm2_g', 'm_mlp_w1', 'm_mlp_w2', 'v_mod_w', 'v_mod_b', 'v_norm1_g', 'v_w_in', 'v_gate_b', 'v_conv_w', 'v_conv_b', 'v_conv_ln_g', 'v_conv_ln_b', 'v_w_conv_proj', 'v_hgrn_lb', 'v_hgrn_norm_g', 'v_w_hgrn_proj', 'v_sb_qn_g', 'v_sb_kn_g', 'v_w_sb_proj', 'v_w_out', 'v_norm2_g', 'v_mlp_w1', 'v_mlp_w2']
TWIN_OUTPUTS = ['loss', 'grad_x', 'grad_mod_w', 'grad_mod_b', 'grad_norm1_g', 'grad_w_in', 'grad_gate_b', 'grad_conv_w', 'grad_conv_b', 'grad_conv_ln_g', 'grad_conv_ln_b', 'grad_w_conv_proj', 'grad_hgrn_lb', 'grad_hgrn_norm_g', 'grad_w_hgrn_proj', 'grad_sb_qn_g', 'grad_sb_kn_g', 'grad_w_sb_proj', 'grad_w_out', 'grad_norm2_g', 'grad_mlp_w1', 'grad_mlp_w2', 'delta_mod_w', 'delta_mod_b', 'delta_norm1_g', 'delta_w_in', 'delta_gate_b', 'delta_conv_w', 'delta_conv_b', 'delta_conv_ln_g', 'delta_conv_ln_b', 'delta_w_conv_proj', 'delta_hgrn_lb', 'delta_hgrn_norm_g', 'delta_w_hgrn_proj', 'delta_sb_qn_g', 'delta_sb_kn_g', 'delta_w_sb_proj', 'delta_w_out', 'delta_norm2_g', 'delta_mlp_w1', 'delta_mlp_w2', 'new_m_mod_w', 'new_m_mod_b', 'new_m_norm1_g', 'new_m_w_in', 'new_m_gate_b', 'new_m_conv_w', 'new_m_conv_b', 'new_m_conv_ln_g', 'new_m_conv_ln_b', 'new_m_w_conv_proj', 'new_m_hgrn_lb', 'new_m_hgrn_norm_g', 'new_m_w_hgrn_proj', 'new_m_sb_qn_g', 'new_m_sb_kn_g', 'new_m_w_sb_proj', 'new_m_w_out', 'new_m_norm2_g', 'new_m_mlp_w1', 'new_m_mlp_w2', 'new_v_mod_w', 'new_v_mod_b', 'new_v_norm1_g', 'new_v_w_in', 'new_v_gate_b', 'new_v_conv_w', 'new_v_conv_b', 'new_v_conv_ln_g', 'new_v_conv_ln_b', 'new_v_w_conv_proj', 'new_v_hgrn_lb', 'new_v_hgrn_norm_g', 'new_v_w_hgrn_proj', 'new_v_sb_qn_g', 'new_v_sb_kn_g', 'new_v_w_sb_proj', 'new_v_w_out', 'new_v_norm2_g', 'new_v_mlp_w1', 'new_v_mlp_w2']
TWIN_LEAF_KINDS = {'loss': 'loss', 'grad_x': 'grad_x', 'grad_mod_w': 'grad_w', 'grad_mod_b': 'grad_w', 'grad_norm1_g': 'grad_w', 'grad_w_in': 'grad_w', 'grad_gate_b': 'grad_w', 'grad_conv_w': 'grad_w', 'grad_conv_b': 'grad_w', 'grad_conv_ln_g': 'grad_w', 'grad_conv_ln_b': 'grad_w', 'grad_w_conv_proj': 'grad_w', 'grad_hgrn_lb': 'grad_w', 'grad_hgrn_norm_g': 'grad_w', 'grad_w_hgrn_proj': 'grad_w', 'grad_sb_qn_g': 'grad_w', 'grad_sb_kn_g': 'grad_w', 'grad_w_sb_proj': 'grad_w', 'grad_w_out': 'grad_w', 'grad_norm2_g': 'grad_w', 'grad_mlp_w1': 'grad_w', 'grad_mlp_w2': 'grad_w', 'delta_mod_w': 'delta_w', 'delta_mod_b': 'delta_w', 'delta_norm1_g': 'delta_w', 'delta_w_in': 'delta_w', 'delta_gate_b': 'delta_w', 'delta_conv_w': 'delta_w', 'delta_conv_b': 'delta_w', 'delta_conv_ln_g': 'delta_w', 'delta_conv_ln_b': 'delta_w', 'delta_w_conv_proj': 'delta_w', 'delta_hgrn_lb': 'delta_w', 'delta_hgrn_norm_g': 'delta_w', 'delta_w_hgrn_proj': 'delta_w', 'delta_sb_qn_g': 'delta_w', 'delta_sb_kn_g': 'delta_w', 'delta_w_sb_proj': 'delta_w', 'delta_w_out': 'delta_w', 'delta_norm2_g': 'delta_w', 'delta_mlp_w1': 'delta_w', 'delta_mlp_w2': 'delta_w', 'new_m_mod_w': 'new_m', 'new_m_mod_b': 'new_m', 'new_m_norm1_g': 'new_m', 'new_m_w_in': 'new_m', 'new_m_gate_b': 'new_m', 'new_m_conv_w': 'new_m', 'new_m_conv_b': 'new_m', 'new_m_conv_ln_g': 'new_m', 'new_m_conv_ln_b': 'new_m', 'new_m_w_conv_proj': 'new_m', 'new_m_hgrn_lb': 'new_m', 'new_m_hgrn_norm_g': 'new_m', 'new_m_w_hgrn_proj': 'new_m', 'new_m_sb_qn_g': 'new_m', 'new_m_sb_kn_g': 'new_m', 'new_m_w_sb_proj': 'new_m', 'new_m_w_out': 'new_m', 'new_m_norm2_g': 'new_m', 'new_m_mlp_w1': 'new_m', 'new_m_mlp_w2': 'new_m', 'new_v_mod_w': 'new_v', 'new_v_mod_b': 'new_v', 'new_v_norm1_g': 'new_v', 'new_v_w_in': 'new_v', 'new_v_gate_b': 'new_v', 'new_v_conv_w': 'new_v', 'new_v_conv_b': 'new_v', 'new_v_conv_ln_g': 'new_v', 'new_v_conv_ln_b': 'new_v', 'new_v_w_conv_proj': 'new_v', 'new_v_hgrn_lb': 'new_v', 'new_v_hgrn_norm_g': 'new_v', 'new_v_w_hgrn_proj': 'new_v', 'new_v_sb_qn_g': 'new_v', 'new_v_sb_kn_g': 'new_v', 'new_v_w_sb_proj': 'new_v', 'new_v_w_out': 'new_v', 'new_v_norm2_g': 'new_v', 'new_v_mlp_w1': 'new_v', 'new_v_mlp_w2': 'new_v'}


def _forward(args):
    return _fwd_reference(*[args[k] for k in FWD_PARAMS])


def _output_shape():
    out = _jax.eval_shape(lambda: _forward(_fwd_setup_inputs(0)))
    return out.shape, out.dtype

N_MICROBATCH = 1
ADAM_LR = 0.001
ADAM_B1 = 0.9
ADAM_B2 = 0.999
ADAM_EPS = 1e-08
ADAM_WD = 0.01
ADAM_STEP = 10
PER_EXAMPLE_BATCH_AXIS = {'x': 0, 'c': 0, 'loss_target': 0}
SHARED_INPUTS = []
_WEIGHT_DTYPES = {'mod_w': _jnp.float32, 'mod_b': _jnp.float32, 'norm1_g': _jnp.float32, 'w_in': _jnp.float32, 'gate_b': _jnp.float32, 'conv_w': _jnp.float32, 'conv_b': _jnp.float32, 'conv_ln_g': _jnp.float32, 'conv_ln_b': _jnp.float32, 'w_conv_proj': _jnp.float32, 'hgrn_lb': _jnp.float32, 'hgrn_norm_g': _jnp.float32, 'w_hgrn_proj': _jnp.float32, 'sb_qn_g': _jnp.float32, 'sb_kn_g': _jnp.float32, 'w_sb_proj': _jnp.float32, 'w_out': _jnp.float32, 'norm2_g': _jnp.float32, 'mlp_w1': _jnp.float32, 'mlp_w2': _jnp.float32}
MOMENT_SCALE = {'mod_w': 3.456805e+00, 'mod_b': 7.319858e+00, 'norm1_g': 1.017600e+00, 'w_in': 1.268759e-01, 'gate_b': 1.658150e-01, 'conv_w': 1.820235e-01, 'conv_b': 1.169882e+00, 'conv_ln_g': 9.935598e-01, 'conv_ln_b': 9.312674e-01, 'w_conv_proj': 2.255239e-01, 'hgrn_lb': 4.788066e-03, 'hgrn_norm_g': 3.299731e+00, 'w_hgrn_proj': 1.303975e-01, 'sb_qn_g': 9.508791e-01, 'sb_kn_g': 9.502526e-01, 'w_sb_proj': 3.018850e-01, 'w_out': 3.872237e-01, 'norm2_g': 1.227509e+01, 'mlp_w1': 4.264052e-01, 'mlp_w2': 1.574713e+00}


def _to_microbatches(a, axis):
    t = _jnp.moveaxis(a, axis, 0)
    t = t.reshape((N_MICROBATCH, t.shape[0] // N_MICROBATCH) + t.shape[1:])
    return _jnp.moveaxis(t, 1, axis + 1)


def setup_inputs(seed: int = 0) -> dict:
    inp = _fwd_setup_inputs(seed)
    key = _jax.random.fold_in(_jax.random.key(seed), 7919)
    shape, _ = _output_shape()
    out = dict(inp)
    out["loss_target"] = _jax.random.normal(_jax.random.fold_in(key, 0), shape, _jnp.float32)
    for i, name in enumerate(TWIN_WEIGHTS):
        w = inp[name].astype(_jnp.float32)
        if MOMENT_SCALE is None:
            s = _jnp.sqrt(_jnp.mean(_jnp.square(w)) + 1e-30)
        else:
            s = MOMENT_SCALE[name]
        km, kv = _jax.random.split(_jax.random.fold_in(key, i + 1))
        out[name] = w
        out["m_" + name] = s * _jax.random.normal(km, w.shape, _jnp.float32)
        out["v_" + name] = (s * s) * _jax.random.uniform(kv, w.shape, _jnp.float32, 0.5, 1.5)
    if N_MICROBATCH > 1:
        for name, axis in PER_EXAMPLE_BATCH_AXIS.items():
            out[name] = _to_microbatches(out[name], axis)
    return {'x': out['x'], 'c': out['c'], 'mod_w': out['mod_w'], 'mod_b': out['mod_b'], 'norm1_g': out['norm1_g'], 'w_in': out['w_in'], 'gate_b': out['gate_b'], 'conv_w': out['conv_w'], 'conv_b': out['conv_b'], 'conv_ln_g': out['conv_ln_g'], 'conv_ln_b': out['conv_ln_b'], 'w_conv_proj': out['w_conv_proj'], 'hgrn_lb': out['hgrn_lb'], 'hgrn_norm_g': out['hgrn_norm_g'], 'w_hgrn_proj': out['w_hgrn_proj'], 'sb_qn_g': out['sb_qn_g'], 'sb_kn_g': out['sb_kn_g'], 'w_sb_proj': out['w_sb_proj'], 'w_out': out['w_out'], 'norm2_g': out['norm2_g'], 'mlp_w1': out['mlp_w1'], 'mlp_w2': out['mlp_w2'], 'loss_target': out['loss_target'], 'm_mod_w': out['m_mod_w'], 'm_mod_b': out['m_mod_b'], 'm_norm1_g': out['m_norm1_g'], 'm_w_in': out['m_w_in'], 'm_gate_b': out['m_gate_b'], 'm_conv_w': out['m_conv_w'], 'm_conv_b': out['m_conv_b'], 'm_conv_ln_g': out['m_conv_ln_g'], 'm_conv_ln_b': out['m_conv_ln_b'], 'm_w_conv_proj': out['m_w_conv_proj'], 'm_hgrn_lb': out['m_hgrn_lb'], 'm_hgrn_norm_g': out['m_hgrn_norm_g'], 'm_w_hgrn_proj': out['m_w_hgrn_proj'], 'm_sb_qn_g': out['m_sb_qn_g'], 'm_sb_kn_g': out['m_sb_kn_g'], 'm_w_sb_proj': out['m_w_sb_proj'], 'm_w_out': out['m_w_out'], 'm_norm2_g': out['m_norm2_g'], 'm_mlp_w1': out['m_mlp_w1'], 'm_mlp_w2': out['m_mlp_w2'], 'v_mod_w': out['v_mod_w'], 'v_mod_b': out['v_mod_b'], 'v_norm1_g': out['v_norm1_g'], 'v_w_in': out['v_w_in'], 'v_gate_b': out['v_gate_b'], 'v_conv_w': out['v_conv_w'], 'v_conv_b': out['v_conv_b'], 'v_conv_ln_g': out['v_conv_ln_g'], 'v_conv_ln_b': out['v_conv_ln_b'], 'v_w_conv_proj': out['v_w_conv_proj'], 'v_hgrn_lb': out['v_hgrn_lb'], 'v_hgrn_norm_g': out['v_hgrn_norm_g'], 'v_w_hgrn_proj': out['v_w_hgrn_proj'], 'v_sb_qn_g': out['v_sb_qn_g'], 'v_sb_kn_g': out['v_sb_kn_g'], 'v_w_sb_proj': out['v_w_sb_proj'], 'v_w_out': out['v_w_out'], 'v_norm2_g': out['v_norm2_g'], 'v_mlp_w1': out['v_mlp_w1'], 'v_mlp_w2': out['v_mlp_w2']}


def _loss(weights, diff, rest, loss_target):
    with _jax.named_scope("forward"):
        args = {**rest, TWIN_DIFF_INPUT: diff, **{k: w.astype(_WEIGHT_DTYPES[k]) for k, w in weights.items()}}
        y = _forward(args)
    with _jax.named_scope("loss_head"):
        err = _jnp.square(y.astype(_jnp.float32) - loss_target)
        return 0.5 * _jnp.sum(_jnp.mean(err, axis=-1)) if err.ndim else 0.5 * err


def _adamw(w, g, m, v):
    m = ADAM_B1 * m + (1.0 - ADAM_B1) * g
    v = ADAM_B2 * v + (1.0 - ADAM_B2) * _jnp.square(g)
    m_hat = m / (1.0 - ADAM_B1 ** ADAM_STEP)
    v_hat = v / (1.0 - ADAM_B2 ** ADAM_STEP)
    delta = -ADAM_LR * (m_hat / (_jnp.sqrt(v_hat) + ADAM_EPS) + ADAM_WD * w)
    return delta, m, v


def reference(x, c, mod_w, mod_b, norm1_g, w_in, gate_b, conv_w, conv_b, conv_ln_g, conv_ln_b, w_conv_proj, hgrn_lb, hgrn_norm_g, w_hgrn_proj, sb_qn_g, sb_kn_g, w_sb_proj, w_out, norm2_g, mlp_w1, mlp_w2, loss_target, m_mod_w, m_mod_b, m_norm1_g, m_w_in, m_gate_b, m_conv_w, m_conv_b, m_conv_ln_g, m_conv_ln_b, m_w_conv_proj, m_hgrn_lb, m_hgrn_norm_g, m_w_hgrn_proj, m_sb_qn_g, m_sb_kn_g, m_w_sb_proj, m_w_out, m_norm2_g, m_mlp_w1, m_mlp_w2, v_mod_w, v_mod_b, v_norm1_g, v_w_in, v_gate_b, v_conv_w, v_conv_b, v_conv_ln_g, v_conv_ln_b, v_w_conv_proj, v_hgrn_lb, v_hgrn_norm_g, v_w_hgrn_proj, v_sb_qn_g, v_sb_kn_g, v_w_sb_proj, v_w_out, v_norm2_g, v_mlp_w1, v_mlp_w2):
    given = dict(x=x, c=c, mod_w=mod_w, mod_b=mod_b, norm1_g=norm1_g, w_in=w_in, gate_b=gate_b, conv_w=conv_w, conv_b=conv_b, conv_ln_g=conv_ln_g, conv_ln_b=conv_ln_b, w_conv_proj=w_conv_proj, hgrn_lb=hgrn_lb, hgrn_norm_g=hgrn_norm_g, w_hgrn_proj=w_hgrn_proj, sb_qn_g=sb_qn_g, sb_kn_g=sb_kn_g, w_sb_proj=w_sb_proj, w_out=w_out, norm2_g=norm2_g, mlp_w1=mlp_w1, mlp_w2=mlp_w2, loss_target=loss_target, m_mod_w=m_mod_w, m_mod_b=m_mod_b, m_norm1_g=m_norm1_g, m_w_in=m_w_in, m_gate_b=m_gate_b, m_conv_w=m_conv_w, m_conv_b=m_conv_b, m_conv_ln_g=m_conv_ln_g, m_conv_ln_b=m_conv_ln_b, m_w_conv_proj=m_w_conv_proj, m_hgrn_lb=m_hgrn_lb, m_hgrn_norm_g=m_hgrn_norm_g, m_w_hgrn_proj=m_w_hgrn_proj, m_sb_qn_g=m_sb_qn_g, m_sb_kn_g=m_sb_kn_g, m_w_sb_proj=m_w_sb_proj, m_w_out=m_w_out, m_norm2_g=m_norm2_g, m_mlp_w1=m_mlp_w1, m_mlp_w2=m_mlp_w2, v_mod_w=v_mod_w, v_mod_b=v_mod_b, v_norm1_g=v_norm1_g, v_w_in=v_w_in, v_gate_b=v_gate_b, v_conv_w=v_conv_w, v_conv_b=v_conv_b, v_conv_ln_g=v_conv_ln_g, v_conv_ln_b=v_conv_ln_b, v_w_conv_proj=v_w_conv_proj, v_hgrn_lb=v_hgrn_lb, v_hgrn_norm_g=v_hgrn_norm_g, v_w_hgrn_proj=v_w_hgrn_proj, v_sb_qn_g=v_sb_qn_g, v_sb_kn_g=v_sb_kn_g, v_w_sb_proj=v_w_sb_proj, v_w_out=v_w_out, v_norm2_g=v_norm2_g, v_mlp_w1=v_mlp_w1, v_mlp_w2=v_mlp_w2)
    weights = {n: given[n] for n in TWIN_WEIGHTS}
    shared = {n: given[n] for n in SHARED_INPUTS}
    per_example = {n: given[n] for n in ['x', 'c']}
    grad_fn = _jax.value_and_grad(_loss, argnums=(0, 1))

    def one_microbatch(ex, loss_target):
        ex = dict(ex)
        diff = ex.pop(TWIN_DIFF_INPUT)
        return grad_fn(weights, diff, {**shared, **ex}, loss_target)

    if N_MICROBATCH == 1:
        loss, (grad_w, grad_x) = one_microbatch(per_example, given["loss_target"])
    else:
        def body(carry, xs):
            loss_sum, grad_sum = carry
            l_k, (gw_k, gx_k) = one_microbatch(xs[0], xs[1])
            with _jax.named_scope("update"):
                return (loss_sum + l_k, _jax.tree.map(_jnp.add, grad_sum, gw_k)), gx_k

        init = (_jnp.zeros((), _jnp.float32), _jax.tree.map(_jnp.zeros_like, weights))
        (loss, grad_w), grad_x = _jax.lax.scan(body, init, (per_example, given["loss_target"]))
    with _jax.named_scope("update"):
        delta_w, new_m, new_v = {}, {}, {}
        for n in TWIN_WEIGHTS:
            delta_w[n], new_m[n], new_v[n] = _adamw(weights[n], grad_w[n], given["m_" + n], given["v_" + n])
    return (loss, grad_x, *[grad_w[n] for n in TWIN_WEIGHTS], *[delta_w[n] for n in TWIN_WEIGHTS],
            *[new_m[n] for n in TWIN_WEIGHTS], *[new_v[n] for n in TWIN_WEIGHTS])
```

```python
import functools
import math

import jax
import jax.numpy as jnp
from jax import lax
from jax.experimental import pallas as pl
from jax.experimental.pallas import tpu as pltpu

F32 = jnp.float32
BF16 = jnp.bfloat16
MESH = pl.DeviceIdType.MESH

EPS = 1e-6
CONV_CH = 512
CONV_WIDTH = 31
CONV_HALO = 32
HG_HEADS = 4
HG_D = 128
HG_CHUNK = 64
HG_SUB = 16
SB_HEADS = 8
SB_DH = 64
SB_BLK = 128
OFF_CONV, OFF_HG, OFF_SB, OFF_GL = 0, 1024, 3072, 4608
ADAM_LR, ADAM_B1, ADAM_B2, ADAM_EPS, ADAM_WD, ADAM_STEP = 0.001, 0.9, 0.999, 1e-08, 0.01, 10
VMEM_LIMIT_BYTES = 56 * 1024 * 1024
ROW_TILE = 256


def _cparams(sem=None, **kw):
    return pltpu.CompilerParams(dimension_semantics=sem, vmem_limit_bytes=VMEM_LIMIT_BYTES, **kw)


def _pick(n, cands):
    for c in cands:
        if n % c == 0:
            return c
    return n


def _matmul(a, b, *, ta=False, tb=False, bl=None, out_dtype=F32, name, into=None, layer=None, n_layers=None):
    M, K = (a.shape[1], a.shape[0]) if ta else a.shape
    N = b.shape[-2] if tb else b.shape[-1]
    tm = _pick(M, (512, 256, 128))
    tn = _pick(N, (512, 384, 256, 128))
    tk = _pick(K, (1024, 512, 256, 128))
    nk = K // tk
    a_spec = pl.BlockSpec((tk, tm), lambda i, j, k: (k, i)) if ta else pl.BlockSpec((tm, tk), lambda i, j, k: (i, k))
    if bl is None:
        b_spec = pl.BlockSpec((tn, tk), lambda i, j, k: (j, k)) if tb else pl.BlockSpec((tk, tn), lambda i, j, k: (k, j))
    elif tb:
        b_spec = pl.BlockSpec((None, tn, tk), lambda i, j, k: (bl, j, k))
    else:
        b_spec = pl.BlockSpec((None, tk, tn), lambda i, j, k: (bl, k, j))
    dn = (((0 if ta else 1,), (1 if tb else 0,)), ((), ()))

    def kern(a_ref, b_ref, *rest):
        o_ref = rest[-2] if nk > 1 else rest[-1]
        prod = lax.dot_general(a_ref[...].astype(BF16), b_ref[...].astype(BF16), dn, preferred_element_type=F32)
        if nk == 1:
            o_ref[...] = prod.astype(o_ref.dtype).reshape(o_ref.shape)
            return
        acc_ref = rest[-1]
        k = pl.program_id(2)

        @pl.when(k == 0)
        def _():
            acc_ref[...] = prod

        @pl.when(k > 0)
        def _():
            acc_ref[...] += prod

        @pl.when(k == nk - 1)
        def _():
            o_ref[...] = acc_ref[...].astype(o_ref.dtype).reshape(o_ref.shape)

    in_specs, args, aliases = [a_spec, b_spec], [a, b], {}
    if layer is None:
        out_shape = jax.ShapeDtypeStruct((M, N), out_dtype)
        out_spec = pl.BlockSpec((tm, tn), lambda i, j, k: (i, j))
    else:
        out_shape = jax.ShapeDtypeStruct((n_layers, M, N), out_dtype)
        out_spec = pl.BlockSpec((1, tm, tn), lambda i, j, k: (layer, i, j))
        if into is not None:
            in_specs.append(pl.BlockSpec(memory_space=pl.ANY))
            args.append(into)
            aliases = {2: 0}
    return pl.pallas_call(
        kern, grid=(M // tm, N // tn, nk), in_specs=in_specs, out_specs=out_spec, out_shape=out_shape,
        scratch_shapes=[pltpu.VMEM((tm, tn), F32)] if nk > 1 else [],
        input_output_aliases=aliases, name=name,
        compiler_params=_cparams(("parallel", "parallel", "arbitrary")))(*args)


def _col_specs(off, width, T):
    bw = math.gcd(width, off) if off else width
    return [pl.BlockSpec((T, bw), functools.partial(lambda i, c: (i, c), c=off // bw + p)) for p in range(width // bw)]


def _gather_rows(refs, counts):
    vals, pos = [], 0
    for n in counts:
        parts = [refs[pos + p][...].astype(F32) for p in range(n)]
        pos += n
        vals.append(parts[0] if n == 1 else jnp.concatenate(parts, axis=1))
    return vals, pos


def _rowop(fn, ins, params, outs, *, name):
    S = ins[0][0].shape[0]
    T = min(ROW_TILE, S)
    in_specs, counts, args = [], [], []
    for arr, off, width in ins:
        sp = _col_specs(off, width, T)
        in_specs += sp
        counts.append(len(sp))
        args += [arr] * len(sp)
    in_specs += [pl.BlockSpec(p.shape, lambda i: (0, 0)) for p in params]

    def kern(*refs):
        vals, pos = _gather_rows(refs, counts)
        pv = [refs[pos + p][...] for p in range(len(params))]
        pos += len(params)
        res = fn(*vals, *pv)
        for r, o_ref in zip(res, refs[pos:]):
            o_ref[...] = r.astype(o_ref.dtype)

    return pl.pallas_call(
        kern, grid=(S // T,), in_specs=in_specs,
        out_specs=[pl.BlockSpec((T, w), lambda i: (i, 0)) for w, _ in outs],
        out_shape=[jax.ShapeDtypeStruct((S, w), dt) for w, dt in outs],
        name=name, compiler_params=_cparams(("parallel",)))(*args, *params)


def _rowop_bwd(fn, ins, params, douts, din_dtypes, *, name, add=None):
    add = add or {}
    S = ins[0][0].shape[0]
    T = min(ROW_TILE, S)
    in_specs, counts, args = [], [], []
    for arr, off, width in ins:
        sp = _col_specs(off, width, T)
        in_specs += sp
        counts.append(len(sp))
        args += [arr] * len(sp)
    in_specs += [pl.BlockSpec(p.shape, lambda i: (0, 0)) for p in params]
    in_specs += [pl.BlockSpec((T, d.shape[1]), lambda i: (i, 0)) for d in douts]
    add_keys = sorted(add)
    in_specs += [pl.BlockSpec((T, add[k].shape[1]), lambda i: (i, 0)) for k in add_keys]
    want = [k for k, dt in enumerate(din_dtypes) if dt is not None]

    def kern(*refs):
        vals, pos = _gather_rows(refs, counts)
        pv = [refs[pos + p][...] for p in range(len(params))]
        pos += len(params)
        cts = [refs[pos + p][...].astype(F32) for p in range(len(douts))]
        pos += len(douts)
        adds = {k: refs[pos + p][...].astype(F32) for p, k in enumerate(add_keys)}
        pos += len(add_keys)
        _, vjp = jax.vjp(fn, *vals, *pv)
        grads = vjp(tuple(cts))
        for k in want:
            g = grads[k] + adds[k] if k in adds else grads[k]
            refs[pos][...] = g.astype(refs[pos].dtype)
            pos += 1
        first = pl.program_id(0) == 0
        for p in range(len(params)):
            gp, o_ref = grads[len(ins) + p], refs[pos + p]

            @pl.when(first)
            def _(gp=gp, o_ref=o_ref):
                o_ref[...] = gp

            @pl.when(jnp.logical_not(first))
            def _(gp=gp, o_ref=o_ref):
                o_ref[...] += gp

    out_specs = [pl.BlockSpec((T, ins[k][2]), lambda i: (i, 0)) for k in want]
    out_specs += [pl.BlockSpec(p.shape, lambda i: (0, 0)) for p in params]
    out_shape = [jax.ShapeDtypeStruct((S, ins[k][2]), din_dtypes[k]) for k in want]
    out_shape += [jax.ShapeDtypeStruct(p.shape, F32) for p in params]
    res = pl.pallas_call(
        kern, grid=(S // T,), in_specs=in_specs, out_specs=out_specs, out_shape=out_shape,
        name=name, compiler_params=_cparams(("arbitrary",)))(*args, *params, *douts, *[add[k] for k in add_keys])
    dins = [None] * len(ins)
    for p, k in enumerate(want):
        dins[k] = res[p]
    return dins, list(res[len(want):])


def _rms(x, g):
    return x * lax.rsqrt(jnp.mean(x * x, axis=-1, keepdims=True) + EPS) * g


def _fn_normmod(x, g, sc, sh):
    return (_rms(x, g) * (1.0 + sc) + sh,)


def _fn_lnsilu(c, g, b):
    mu = jnp.mean(c, axis=-1, keepdims=True)
    var = jnp.mean(jnp.square(c - mu), axis=-1, keepdims=True)
    y = (c - mu) * lax.rsqrt(var + EPS) * g + b
    return (y * jax.nn.sigmoid(y),)


def _fn_merge(gl, yc, yh, ys, gb):
    d = yc.shape[1]
    g = jax.nn.sigmoid(gl + gb)
    return (g[:, :d] * yc + g[:, d:2 * d] * yh + g[:, 2 * d:] * ys,)


def _fn_resid(x, y, g):
    return (x + g * y,)


def _fn_scale(y, g):
    return (g * y,)


def _fn_relu2(u):
    return (jnp.square(jnp.maximum(u, 0.0)),)


def _conv_specs(S, T):
    r = T // CONV_HALO
    cur = [pl.BlockSpec((T, CONV_CH), lambda i: (i, 0)), pl.BlockSpec((T, CONV_CH), lambda i: (i, 1))]
    prev = [pl.BlockSpec((CONV_HALO, CONV_CH), lambda i: (jnp.maximum(i * r - 1, 0), 0)),
            pl.BlockSpec((CONV_HALO, CONV_CH), lambda i: (jnp.maximum(i * r - 1, 0), 1))]
    return cur + prev


def _glu_ext(a_ref, g_ref, ah_ref, gh_ref):
    a = a_ref[...]
    sg = jax.nn.sigmoid(g_ref[...])
    uh = jnp.where(pl.program_id(0) > 0, ah_ref[...] * jax.nn.sigmoid(gh_ref[...]), 0.0)
    return a, sg, jnp.concatenate([uh, a * sg], axis=0)


def _shift_up(xe, k, T):
    return xe[:T] if k == 0 else pltpu.roll(xe, shift=xe.shape[0] - k, axis=0)[:T]


def _conv_fwd(proj, w32, b, *, name):
    S = proj.shape[0]
    T = min(ROW_TILE, S)
    lead = CONV_HALO - (CONV_WIDTH - 1)

    def kern(a_ref, g_ref, ah_ref, gh_ref, w_ref, b_ref, o_ref):
        _, _, ue = _glu_ext(a_ref, g_ref, ah_ref, gh_ref)
        acc = jnp.zeros((T, CONV_CH), F32) + b_ref[...]
        for j in range(CONV_WIDTH):
            acc = acc + w_ref[j:j + 1, :] * _shift_up(ue, lead + j, T)
        o_ref[...] = acc

    const = lambda shape: pl.BlockSpec(shape, lambda i: (0, 0))
    return pl.pallas_call(
        kern, grid=(S // T,), in_specs=_conv_specs(S, T) + [const(w32.shape), const(b.shape)],
        out_specs=pl.BlockSpec((T, CONV_CH), lambda i: (i, 0)),
        out_shape=jax.ShapeDtypeStruct((S, CONV_CH), F32), name=name,
        compiler_params=_cparams(("parallel",)))(proj, proj, proj, proj, w32, b)


def _conv_bwd(proj, dc, w32, *, name):
    S = proj.shape[0]
    T = min(ROW_TILE, S)
    nt = S // T
    r = T // CONV_HALO
    lead = CONV_HALO - (CONV_WIDTH - 1)
    last_halo = S // CONV_HALO - 1

    def kern(a_ref, g_ref, ah_ref, gh_ref, dc_ref, dcn_ref, w_ref, dag_ref, dw_ref, db_ref):
        i = pl.program_id(0)
        a, sg, ue = _glu_ext(a_ref, g_ref, ah_ref, gh_ref)
        dc_t = dc_ref[...]
        de = jnp.concatenate([dc_t, jnp.where(i < nt - 1, dcn_ref[...], 0.0)], axis=0)

        @pl.when(i == 0)
        def _():
            dw_ref[...] = jnp.zeros_like(dw_ref)
            db_ref[...] = jnp.zeros_like(db_ref)

        du = jnp.zeros((T, CONV_CH), F32)
        for j in range(CONV_WIDTH):
            du = du + w_ref[j:j + 1, :] * _shift_up(de, CONV_WIDTH - 1 - j, T)
            dw_ref[j:j + 1, :] += jnp.sum(dc_t * _shift_up(ue, lead + j, T), axis=0, keepdims=True)
        db_ref[...] += jnp.sum(dc_t, axis=0, keepdims=True)
        dag_ref[:, :CONV_CH] = (du * sg).astype(BF16)
        dag_ref[:, CONV_CH:] = (du * a * sg * (1.0 - sg)).astype(BF16)

    const = lambda shape: pl.BlockSpec(shape, lambda i: (0, 0))
    in_specs = _conv_specs(S, T) + [
        pl.BlockSpec((T, CONV_CH), lambda i: (i, 0)),
        pl.BlockSpec((CONV_HALO, CONV_CH), lambda i: (jnp.minimum((i + 1) * r, last_halo), 0)),
        const(w32.shape)]
    return pl.pallas_call(
        kern, grid=(nt,), in_specs=in_specs,
        out_specs=[pl.BlockSpec((T, 2 * CONV_CH), lambda i: (i, 0)), const(w32.shape), const((1, CONV_CH))],
        out_shape=[jax.ShapeDtypeStruct((S, 2 * CONV_CH), BF16), jax.ShapeDtypeStruct(w32.shape, F32),
                   jax.ShapeDtypeStruct((1, CONV_CH), F32)],
        name=name, compiler_params=_cparams(("arbitrary",)))(proj, proj, proj, proj, dc, dc, w32)


def _split3(x):
    h = x.astype(BF16)
    r = x - h.astype(F32)
    m = r.astype(BF16)
    return h, m, (r - m.astype(F32)).astype(BF16)


def _xdot(x, u):
    return sum(jnp.dot(p, u, preferred_element_type=F32) for p in _split3(x))


def _xdot_l(m, x):
    return sum(jnp.dot(m, p, preferred_element_type=F32) for p in _split3(x))


def _iota2(shape, dim):
    return lax.broadcasted_iota(jnp.int32, shape, dim)


def _hg_mats():
    n = HG_CHUNK
    r, c = _iota2((n, n), 0), _iota2((n, n), 1)
    low = c <= r
    same = (r // HG_SUB) == (c // HG_SUB)
    up = r <= c
    as_b = lambda m: jnp.where(m, 1.0, 0.0).astype(BF16)
    return dict(low=as_b(low), low_t=as_b(up), blk=as_b(low & same), blk_t=as_b(up & same),
                ones=jnp.ones((n, n), BF16))


@jax.custom_vjp
def _cum(m, m_t, x):
    return _xdot_l(m, x)


def _cum_bwd(res, g):
    m, m_t = res
    return jnp.zeros_like(m), jnp.zeros_like(m_t), _xdot_l(m_t, g)


_cum.defvjp(lambda m, m_t, x: (_xdot_l(m, x), (m, m_t)), _cum_bwd)


def _hg_chunk(q, f, iv, g, st, lbk, ng, mats):
    n, sub = HG_CHUNK, HG_SUB
    kk = lbk * jax.nn.sigmoid(-f)
    lf = jnp.log(1.0 - kk)
    b = _cum(mats["low"], mats["low_t"], lf)
    bs = _cum(mats["blk"], mats["blk_t"], lf)
    bt = _cum(mats["ones"], mats["ones"], lf)
    qh = q * jax.nn.sigmoid(q)
    dot_nt = lambda x, y: lax.dot_general(x.astype(BF16), y.astype(BF16), (((1,), (1,)), ((), ())),
                                          preferred_element_type=F32)
    o = dot_nt(qh * jnp.exp(b), st)
    b0 = b - bs
    qs = qh * jnp.exp(bs)
    col = _iota2((sub, n), 1)
    rows = []
    for blk in range(n // sub):
        lo = blk * sub
        sl = slice(lo, lo + sub)
        acc = o[sl]
        if blk > 0:
            ref = jnp.concatenate([b0[sl]] * (n // sub), axis=0)
            kd = kk * jnp.exp(jnp.minimum(ref - b, 0.0))
            sc = jnp.where(col < lo, dot_nt(qs[sl], kd), 0.0)
            acc = acc + jnp.dot(sc.astype(BF16), iv.astype(BF16), preferred_element_type=F32)
        bq, bk = bs[sl][None, :, :], bs[sl][:, None, :]
        s_i = lax.broadcasted_iota(jnp.int32, (sub, sub, HG_D), 0)
        t_i = lax.broadcasted_iota(jnp.int32, (sub, sub, HG_D), 1)
        keep = s_i <= t_i
        p = jnp.where(keep, qh[sl][None, :, :] * kk[sl][:, None, :] * jnp.exp(jnp.where(keep, bq - bk, 0.0)), 0.0)
        w = jnp.sum(p, axis=-1, keepdims=True)
        acc = acc + jnp.sum(w * iv[sl][:, None, :], axis=0)
        rows.append(acc)
    o = jnp.concatenate(rows, axis=0)
    kd = kk * jnp.exp(bt - b)
    st_new = jnp.exp(bt[0:1]) * st + lax.dot_general(iv.astype(BF16), kd.astype(BF16), (((0,), (0,)), ((), ())),
                                                     preferred_element_type=F32)
    out = _rms(o, ng) * (g * jax.nn.sigmoid(g))
    return out, st_new


def _hg_tile(S):
    return min(512, S)


def _hg_in_specs(rt, rev, nr):
    blk = HG_D
    base = OFF_HG // blk
    row = (lambda r: nr - 1 - r) if rev else (lambda r: r)
    return [pl.BlockSpec((rt, blk), functools.partial(lambda h, r, k: (row(r), base + HG_HEADS * k + h), k=k))
            for k in range(4)]


def _hgrn_fwd(proj, lbk, ng, *, name):
    S = proj.shape[0]
    rt = _hg_tile(S)
    nr, nc = S // rt, rt // HG_CHUNK

    def kern(q_ref, f_ref, i_ref, g_ref, lbk_ref, ng_ref, o_ref, st_out_ref, st_ref):
        @pl.when(pl.program_id(1) == 0)
        def _():
            st_ref[...] = jnp.zeros_like(st_ref)

        mats = _hg_mats()

        def body(c, carry):
            rows = pl.ds(pl.multiple_of(c * HG_CHUNK, HG_CHUNK), HG_CHUNK)
            st = st_ref[...]
            st_out_ref[0, c] = st
            out, st_new = _hg_chunk(q_ref[rows, :], f_ref[rows, :], i_ref[rows, :], g_ref[rows, :], st,
                                    lbk_ref[...], ng_ref[...], mats)
            o_ref[rows, :] = out.astype(o_ref.dtype)
            st_ref[...] = st_new
            return carry

        lax.fori_loop(0, nc, body, 0)

    in_specs = _hg_in_specs(rt, False, nr) + [pl.BlockSpec((1, HG_D), lambda h, r: (0, h)),
                                               pl.BlockSpec((1, HG_D), lambda h, r: (0, 0))]
    return pl.pallas_call(
        kern, grid=(HG_HEADS, nr), in_specs=in_specs,
        out_specs=[pl.BlockSpec((rt, HG_D), lambda h, r: (r, h)),
                   pl.BlockSpec((1, nc, HG_D, HG_D), lambda h, r: (h, r, 0, 0))],
        out_shape=[jax.ShapeDtypeStruct((S, HG_HEADS * HG_D), BF16),
                   jax.ShapeDtypeStruct((HG_HEADS, S // HG_CHUNK, HG_D, HG_D), F32)],
        scratch_shapes=[pltpu.VMEM((HG_D, HG_D), F32)], name=name,
        compiler_params=_cparams(("parallel", "arbitrary")))(proj, proj, proj, proj, lbk, ng)


def _hgrn_bwd(proj, states, dout, lbk, ng, *, name):
    S = proj.shape[0]
    rt = _hg_tile(S)
    nr, nc = S // rt, rt // HG_CHUNK
    width = HG_HEADS * HG_D

    def kern(q_ref, f_ref, i_ref, g_ref, st_in_ref, do_ref, lbk_ref, ng_ref,
             dq_ref, df_ref, di_ref, dg_ref, dlbk_ref, dng_ref, dst_ref):
        @pl.when(pl.program_id(1) == 0)
        def _():
            dst_ref[...] = jnp.zeros_like(dst_ref)
            dlbk_ref[...] = jnp.zeros_like(dlbk_ref)
            dng_ref[...] = jnp.zeros_like(dng_ref)

        mats = _hg_mats()
        fn = functools.partial(_hg_chunk, mats=mats)

        def body(k, carry):
            c = nc - 1 - k
            rows = pl.ds(pl.multiple_of(c * HG_CHUNK, HG_CHUNK), HG_CHUNK)
            _, vjp = jax.vjp(fn, q_ref[rows, :], f_ref[rows, :], i_ref[rows, :], g_ref[rows, :], st_in_ref[0, c],
                             lbk_ref[...], ng_ref[...])
            dq, df, di, dg, dst, dlbk, dng = vjp((do_ref[rows, :].astype(F32), dst_ref[...]))
            dq_ref[rows, :] = dq.astype(BF16)
            df_ref[rows, :] = df.astype(BF16)
            di_ref[rows, :] = di.astype(BF16)
            dg_ref[rows, :] = dg.astype(BF16)
            dst_ref[...] = dst
            dlbk_ref[...] += dlbk
            dng_ref[0] += dng
            return carry

        lax.fori_loop(0, nc, body, 0)

    rev = lambda r: nr - 1 - r
    tile = pl.BlockSpec((rt, HG_D), lambda h, r: (rev(r), h))
    in_specs = _hg_in_specs(rt, True, nr) + [
        pl.BlockSpec((1, nc, HG_D, HG_D), lambda h, r: (h, rev(r), 0, 0)), tile,
        pl.BlockSpec((1, HG_D), lambda h, r: (0, h)), pl.BlockSpec((1, HG_D), lambda h, r: (0, 0))]
    return pl.pallas_call(
        kern, grid=(HG_HEADS, nr), in_specs=in_specs,
        out_specs=[tile, tile, tile, tile, pl.BlockSpec((1, HG_D), lambda h, r: (0, h)),
                   pl.BlockSpec((1, 1, HG_D), lambda h, r: (h, 0, 0))],
        out_shape=[jax.ShapeDtypeStruct((S, width), BF16)] * 4 + [
            jax.ShapeDtypeStruct((1, width), F32), jax.ShapeDtypeStruct((HG_HEADS, 1, HG_D), F32)],
        scratch_shapes=[pltpu.VMEM((HG_D, HG_D), F32)], name=name,
        compiler_params=_cparams(("parallel", "arbitrary")))(proj, proj, proj, proj, states, dout, lbk, ng)


def _sb_block(qi, kj, r_run, diag, after):
    zt = lax.dot_general(kj, qi, (((1,), (1,)), ((), ())), preferred_element_type=F32)
    sp = jnp.maximum(zt, 0.0) + jnp.log(1.0 + jnp.exp(-jnp.abs(zt)))
    lk = -sp
    if diag:
        keep = _iota2(zt.shape, 0) < _iota2(zt.shape, 1)
        lk = jnp.where(keep, lk, 0.0)
    cs = _xdot_l(after, lk)
    a = jnp.exp(zt + lk + cs + r_run)
    if diag:
        a = jnp.where(keep, a, 0.0)
    return sp, a, jnp.sum(lk, axis=0, keepdims=True)


def _sb_tri(later):
    n = SB_BLK
    r, c = _iota2((n, n), 0), _iota2((n, n), 1)
    return jnp.where(c > r if later else c < r, 1.0, 0.0).astype(BF16)


def _fn_qk(x, g):
    return _rms(x, g)


def _sb_fwd(q, k, v, qg, kg, *, name):
    H, S, dh = q.shape
    nb = S // SB_BLK
    scale = dh ** -0.5

    def kern(q_ref, k_ref, v_ref, qg_ref, kg_ref, o_ref, rs_ref, qn_ref, kn_ref, vb_ref):
        qn_ref[...] = (_fn_qk(q_ref[0], qg_ref[...]) * scale).astype(BF16)
        kn_ref[...] = _fn_qk(k_ref[0], kg_ref[...]).astype(BF16)
        vb_ref[...] = v_ref[0].astype(BF16)
        after = _sb_tri(True)
        blk = lambda i: pl.ds(pl.multiple_of(i * SB_BLK, SB_BLK), SB_BLK)

        def qblock(i, carry):
            qi = qn_ref[blk(i), :]

            def step(j, diag, st):
                acc, r_run = st
                rs_ref[0, i, j] = r_run
                _, a, lk_sum = _sb_block(qi, kn_ref[blk(j), :], r_run, diag, after)
                av = lax.dot_general(a.astype(BF16), vb_ref[blk(j), :], (((0,), (0,)), ((), ())),
                                     preferred_element_type=F32)
                return acc + av, r_run + lk_sum

            st = step(i, True, (jnp.zeros((SB_BLK, dh), F32), jnp.zeros((1, SB_BLK), F32)))
            st = lax.fori_loop(0, i, lambda jj, st: step(i - 1 - jj, False, st), st)
            o_ref[0, blk(i), :] = st[0]
            return carry

        lax.fori_loop(0, nb, qblock, 0)

    head = pl.BlockSpec((1, S, dh), lambda h: (h, 0, 0))
    gain = pl.BlockSpec((1, dh), lambda h: (0, 0))
    return pl.pallas_call(
        kern, grid=(H,), in_specs=[head, head, head, gain, gain],
        out_specs=[head, pl.BlockSpec((1, nb, nb, 1, SB_BLK), lambda h: (h, 0, 0, 0, 0))],
        out_shape=[jax.ShapeDtypeStruct((H, S, dh), F32), jax.ShapeDtypeStruct((H, nb, nb, 1, SB_BLK), F32)],
        scratch_shapes=[pltpu.VMEM((S, dh), BF16)] * 3, name=name,
        compiler_params=_cparams(("parallel",)))(q, k, v, qg, kg)


def _sb_bwd(q, k, v, qg, kg, rs, do, *, name):
    H, S, dh = q.shape
    nb = S // SB_BLK
    scale = dh ** -0.5
    fn_q = lambda x, g: _fn_qk(x, g) * scale

    def kern(q_ref, k_ref, v_ref, qg_ref, kg_ref, rs_ref, do_ref, dq_ref, dk_ref, dv_ref, dqg_ref, dkg_ref,
             qn_ref, kn_ref, vb_ref, dob_ref, dqn_ref, dkn_ref, dvs_ref):
        qn_ref[...] = fn_q(q_ref[0], qg_ref[...]).astype(BF16)
        kn_ref[...] = _fn_qk(k_ref[0], kg_ref[...]).astype(BF16)
        vb_ref[...] = v_ref[0].astype(BF16)
        dob_ref[...] = do_ref[0].astype(BF16)
        dkn_ref[...] = jnp.zeros_like(dkn_ref)
        dvs_ref[...] = jnp.zeros_like(dvs_ref)
        after, before = _sb_tri(True), _sb_tri(False)
        blk = lambda i: pl.ds(pl.multiple_of(i * SB_BLK, SB_BLK), SB_BLK)

        def qblock(i, carry):
            qi = qn_ref[blk(i), :]
            doi = dob_ref[blk(i), :]

            def step(j, diag, st):
                dqa, e_run = st
                kj = kn_ref[blk(j), :]
                vj = vb_ref[blk(j), :]
                sp, a, _ = _sb_block(qi, kj, rs_ref[0, i, j], diag, after)
                dp = lax.dot_general(vj, doi, (((1,), (1,)), ((), ())), preferred_element_type=F32)
                e = dp * a
                e_left = e_run + _xdot_l(before, e)
                s_neg = jnp.exp(-sp)
                dz = e * s_neg - e_left * (1.0 - s_neg)
                if diag:
                    dz = jnp.where(_iota2(dz.shape, 0) < _iota2(dz.shape, 1), dz, 0.0)
                dzb = dz.astype(BF16)
                dkn_ref[blk(j), :] += jnp.dot(dzb, qi, preferred_element_type=F32)
                dvs_ref[blk(j), :] += jnp.dot(a.astype(BF16), doi, preferred_element_type=F32)
                dqa = dqa + lax.dot_general(dzb, kj, (((0,), (0,)), ((), ())), preferred_element_type=F32)
                return dqa, e_run + jnp.sum(e, axis=0, keepdims=True)

            st = (jnp.zeros((SB_BLK, dh), F32), jnp.zeros((1, SB_BLK), F32))
            st = lax.fori_loop(0, i, lambda j, st: step(j, False, st), st)
            st = step(i, True, st)
            dqn_ref[blk(i), :] = st[0]
            return carry

        lax.fori_loop(0, nb, qblock, 0)
        _, vjp_q = jax.vjp(fn_q, q_ref[0], qg_ref[...])
        dq, dqg = vjp_q(dqn_ref[...])
        _, vjp_k = jax.vjp(_fn_qk, k_ref[0], kg_ref[...])
        dk, dkg = vjp_k(dkn_ref[...])
        dq_ref[0] = dq
        dk_ref[0] = dk
        dv_ref[0] = dvs_ref[...]
        dqg_ref[0] = dqg
        dkg_ref[0] = dkg

    head = pl.BlockSpec((1, S, dh), lambda h: (h, 0, 0))
    gain = pl.BlockSpec((1, dh), lambda h: (0, 0))
    dgain = pl.BlockSpec((1, 1, dh), lambda h: (h, 0, 0))
    sums = pl.BlockSpec((1, nb, nb, 1, SB_BLK), lambda h: (h, 0, 0, 0, 0))
    return pl.pallas_call(
        kern, grid=(H,), in_specs=[head, head, head, gain, gain, sums, head],
        out_specs=[head, head, head, dgain, dgain],
        out_shape=[jax.ShapeDtypeStruct((H, S, dh), F32)] * 3 + [jax.ShapeDtypeStruct((H, 1, dh), F32)] * 2,
        scratch_shapes=[pltpu.VMEM((S, dh), BF16)] * 4 + [pltpu.VMEM((S, dh), F32)] * 3,
        name=name, compiler_params=_cparams(("parallel",)))(q, k, v, qg, kg, rs, do)


def _loss_head(y, target, *, name):
    S, D = y.shape
    T = min(ROW_TILE, S)

    def kern(y_ref, t_ref, dy_ref, acc_ref):
        err = y_ref[...] - t_ref[...]
        dy_ref[...] = err * (1.0 / D)
        col = jnp.sum(err * err, axis=0, keepdims=True)
        part = sum(col[:, k * 128:(k + 1) * 128] for k in range(D // 128))

        @pl.when(pl.program_id(0) == 0)
        def _():
            acc_ref[...] = part

        @pl.when(pl.program_id(0) > 0)
        def _():
            acc_ref[...] += part

    tile = pl.BlockSpec((T, D), lambda i: (i, 0))
    return pl.pallas_call(
        kern, grid=(S // T,), in_specs=[tile, tile], out_specs=[tile, pl.BlockSpec((1, 128), lambda i: (0, 0))],
        out_shape=[jax.ShapeDtypeStruct((S, D), F32), jax.ShapeDtypeStruct((1, 128), F32)],
        name=name, compiler_params=_cparams(("arbitrary",)))(y, target)


def _adamw_math(w, g, m, v):
    m = ADAM_B1 * m + (1.0 - ADAM_B1) * g
    v = ADAM_B2 * v + (1.0 - ADAM_B2) * jnp.square(g)
    m_hat = m / (1.0 - ADAM_B1 ** ADAM_STEP)
    v_hat = v / (1.0 - ADAM_B2 ** ADAM_STEP)
    return -ADAM_LR * (m_hat / (jnp.sqrt(v_hat) + ADAM_EPS) + ADAM_WD * w), m, v


def _adamw(w, g, m, v, *, name):
    R, C = w.shape
    T = _pick(R, (256, 128, 64, 32, 16, 8))

    def kern(w_ref, g_ref, m_ref, v_ref, d_ref, mo_ref, vo_ref):
        d, mn, vn = _adamw_math(w_ref[...], g_ref[...], m_ref[...], v_ref[...])
        d_ref[...] = d
        mo_ref[...] = mn
        vo_ref[...] = vn

    tile = pl.BlockSpec((T, C), lambda i: (i, 0))
    return pl.pallas_call(
        kern, grid=(R // T,), in_specs=[tile] * 4, out_specs=[tile] * 3,
        out_shape=[jax.ShapeDtypeStruct((R, C), F32)] * 3, name=name,
        compiler_params=_cparams(("parallel",)))(w, g, m, v)


def _sum8(g, *, name):
    def kern(g_ref, o_ref):
        acc = g_ref[0]
        for d in range(1, g.shape[0]):
            acc = acc + g_ref[d]
        o_ref[...] = acc

    return pl.pallas_call(kern, out_shape=jax.ShapeDtypeStruct(g.shape[1:], F32), name=name,
                          compiler_params=_cparams())(g)


def _place():
    return lax.axis_index("x"), lax.axis_index("y"), lax.axis_index("c")


def _other_chips(x, y):
    return [(1 - x, y), (x, 1 - y), (1 - x, 1 - y)]


def _remote(src, dst, send_sems, recv_sems, k, to):
    return pltpu.make_async_remote_copy(src_ref=src, dst_ref=dst, send_sem=send_sems.at[k], recv_sem=recv_sems.at[k],
                                        device_id=to, device_id_type=MESH)


def _all_gather_small(v, *, name):
    def body(x_ref, out_ref, send_sems, recv_sems, local_sem):
        x, y, c = _place()
        me = 4 * x + 2 * y + c
        mine = pltpu.make_async_copy(x_ref, out_ref.at[me], local_sem)
        mine.start()
        peers = []
        for f in range(1, 8):
            peers.append((1 - x if f & 4 else x, 1 - y if f & 2 else y, 1 - c if f & 1 else c))
        sends = [_remote(x_ref, out_ref.at[me], send_sems, recv_sems, k, p) for k, p in enumerate(peers)]
        for cp in sends:
            cp.start()
        for k, (px, py, pc) in enumerate(peers):
            _remote(x_ref, out_ref.at[4 * px + 2 * py + pc], send_sems, recv_sems, k, (px, py, pc)).wait_recv()
        for cp in sends:
            cp.wait_send()
        mine.wait()

    return pl.pallas_call(
        body, out_shape=jax.ShapeDtypeStruct((8,) + v.shape, v.dtype),
        in_specs=[pl.BlockSpec(memory_space=pltpu.VMEM)], out_specs=pl.BlockSpec(memory_space=pltpu.VMEM),
        scratch_shapes=[pltpu.SemaphoreType.DMA((7,)), pltpu.SemaphoreType.DMA((7,)), pltpu.SemaphoreType.DMA],
        name=name, compiler_params=_cparams())(v)


def _piece(ref, kind, shard_shape, qq, half):
    _, r, n = shard_shape
    h = r // 2
    lo, size = (0, r) if half is None else (half * h, h)
    if kind == "col":
        return ref.at[:, pl.ds(pl.multiple_of(lo, 16), size), pl.ds(pl.multiple_of(qq * n, 128), n)]
    return ref.at[:, pl.ds(pl.multiple_of(qq * r + lo, 16), size), :]


def _gather_weights(shards, kinds, *, name):
    nw = len(shards)
    full = [((s.shape[0], s.shape[1], 4 * s.shape[2]) if k == "col" else (s.shape[0], 4 * s.shape[1], s.shape[2]))
            for s, k in zip(shards, kinds)]

    def body(*refs):
        ins, outs = refs[:nw], refs[nw:2 * nw]
        send_sems, recv_sems, local_sems = refs[2 * nw:]
        x, y, c = _place()
        q = 2 * x + y
        sibling = (x, y, 1 - c)
        chips = _other_chips(x, y)
        local, sent = [], []
        for w in range(nw):
            shp = shards[w].shape
            h = shp[1] // 2
            cp = pltpu.make_async_copy(ins[w], _piece(outs[w], kinds[w], shp, q, None), local_sems.at[w])
            cp.start()
            local.append(cp)
            for j, (cx, cy) in enumerate(chips):
                cp = _remote(ins[w].at[:, pl.ds(pl.multiple_of(c * h, 16), h), :], _piece(outs[w], kinds[w], shp, q, c),
                             send_sems, recv_sems, 6 * w + j, (cx, cy, c))
                cp.start()
                sent.append(cp)
        for w in range(nw):
            for j, (cx, cy) in enumerate(chips):
                win = _piece(outs[w], kinds[w], shards[w].shape, 2 * cx + cy, c)
                _remote(win, win, send_sems, recv_sems, 6 * w + j, (cx, cy, c)).wait_recv()
                cp = _remote(win, win, send_sems, recv_sems, 6 * w + 3 + j, sibling)
                cp.start()
                sent.append(cp)
        for w in range(nw):
            for j, (cx, cy) in enumerate(chips):
                win = _piece(outs[w], kinds[w], shards[w].shape, 2 * cx + cy, 1 - c)
                _remote(win, win, send_sems, recv_sems, 6 * w + 3 + j, sibling).wait_recv()
        for cp in sent:
            cp.wait_send()
        for cp in local:
            cp.wait()

    hbm = pl.BlockSpec(memory_space=pltpu.HBM)
    return pl.pallas_call(
        body, out_shape=[jax.ShapeDtypeStruct(f, s.dtype) for f, s in zip(full, shards)],
        in_specs=[hbm] * nw, out_specs=[hbm] * nw,
        scratch_shapes=[pltpu.SemaphoreType.DMA((6 * nw,)), pltpu.SemaphoreType.DMA((6 * nw,)),
                        pltpu.SemaphoreType.DMA((nw,))],
        name=name, compiler_params=_cparams())(*shards)


def _half_rows(ref, half, h):
    return ref.at[:, pl.ds(pl.multiple_of(half * h, 16), h), :]


def _swap_halves(gs, *, name):
    nw = len(gs)

    def body(*refs):
        ins, lands = refs[:nw], refs[nw:2 * nw]
        send_sems, recv_sems = refs[2 * nw:]
        x, y, c = _place()
        cps = [_remote(_half_rows(ins[w], 1 - c, gs[w].shape[1] // 2), lands[w], send_sems, recv_sems, w, (x, y, 1 - c))
               for w in range(nw)]
        for cp in cps:
            cp.start()
        for cp in cps:
            cp.wait()

    hbm = pl.BlockSpec(memory_space=pltpu.HBM)
    return pl.pallas_call(
        body, out_shape=[jax.ShapeDtypeStruct((g.shape[0], g.shape[1] // 2, g.shape[2]), g.dtype) for g in gs],
        in_specs=[hbm] * nw, out_specs=[hbm] * nw,
        scratch_shapes=[pltpu.SemaphoreType.DMA((nw,)), pltpu.SemaphoreType.DMA((nw,))],
        name=name, compiler_params=_cparams())(*gs)


def _scatter_quarters(ps, kinds, *, name):
    nw = len(ps)
    part = [((p.shape[0], p.shape[1], p.shape[2] // 4) if k == "col" else (p.shape[0], p.shape[2], p.shape[3]))
            for p, k in zip(ps, kinds)]

    def body(*refs):
        ins, lands = refs[:nw], refs[nw:2 * nw]
        send_sems, recv_sems = refs[2 * nw:]
        x, y, c = _place()
        cps = []
        for w in range(nw):
            n = part[w][2]
            for j, (cx, cy) in enumerate(_other_chips(x, y)):
                qj = 2 * cx + cy
                src = ins[w].at[:, :, pl.ds(pl.multiple_of(qj * n, 128), n)] if kinds[w] == "col" else ins[w].at[:, qj]
                cps.append(_remote(src, lands[w].at[j], send_sems, recv_sems, 3 * w + j, (cx, cy, c)))
        for cp in cps:
            cp.start()
        for cp in cps:
            cp.wait()

    hbm = pl.BlockSpec(memory_space=pltpu.HBM)
    return pl.pallas_call(
        body, out_shape=[jax.ShapeDtypeStruct((3,) + pt, p.dtype) for pt, p in zip(part, ps)],
        in_specs=[hbm] * nw, out_specs=[hbm] * nw,
        scratch_shapes=[pltpu.SemaphoreType.DMA((3 * nw,)), pltpu.SemaphoreType.DMA((3 * nw,))],
        name=name, compiler_params=_cparams())(*ps)


def _share_halves(gs, *, name):
    nw = len(gs)

    def body(*refs):
        outs = refs[nw:2 * nw]
        send_sems, recv_sems = refs[2 * nw:]
        x, y, c = _place()
        cps = []
        for w in range(nw):
            win = _half_rows(outs[w], c, gs[w].shape[1] // 2)
            cps.append(_remote(win, win, send_sems, recv_sems, w, (x, y, 1 - c)))
        for cp in cps:
            cp.start()
        for w, cp in enumerate(cps):
            cp.wait_send()
            win = _half_rows(outs[w], 1 - c, gs[w].shape[1] // 2)
            _remote(win, win, send_sems, recv_sems, w, (x, y, 1 - c)).wait_recv()

    hbm = pl.BlockSpec(memory_space=pltpu.HBM)
    return pl.pallas_call(
        body, out_shape=[jax.ShapeDtypeStruct(g.shape, g.dtype) for g in gs],
        in_specs=[hbm] * nw, out_specs=[hbm] * nw, input_output_aliases={w: w for w in range(nw)},
        scratch_shapes=[pltpu.SemaphoreType.DMA((nw,)), pltpu.SemaphoreType.DMA((nw,))],
        name=name, compiler_params=_cparams())(*gs)


def _wide_tile(n):
    return _pick(n, (2048, 1920, 1024, 512, 256, 128))


def _pair_sum(g, land, place, *, name):
    B, R, N = g.shape
    h = R // 2
    tr, tc = _pick(h, (256, 128)), _wide_tile(N)

    def kern(place_ref, g_ref, l_ref, o_ref):
        o_ref[...] = (g_ref[...] + l_ref[...]).astype(o_ref.dtype)

    grid_spec = pltpu.PrefetchScalarGridSpec(
        num_scalar_prefetch=1, grid=(B, h // tr, N // tc),
        in_specs=[pl.BlockSpec((None, tr, tc), lambda b, i, j, p: (b, p[1] * (h // tr) + i, j)),
                  pl.BlockSpec((None, tr, tc), lambda b, i, j, p: (b, i, j))],
        out_specs=pl.BlockSpec((None, tr, tc), lambda b, i, j, p: (b, i, j)))
    return pl.pallas_call(kern, grid_spec=grid_spec, out_shape=jax.ShapeDtypeStruct((B, h, N), BF16), name=name,
                          compiler_params=_cparams(("parallel", "parallel", "parallel")))(place, g, land)


def _quarter_sum(p, land, kind, shard_shape, place, *, name):
    L, r, n = shard_shape
    h = r // 2
    tr, tc = _pick(h, (256, 128)), _wide_tile(n)

    def kern(place_ref, p_ref, a_ref, b_ref, c_ref, o_ref):
        o_ref[...] = ((p_ref[...].astype(F32) + a_ref[...].astype(F32)) + b_ref[...].astype(F32)) + c_ref[...].astype(F32)

    if kind == "col":
        p_spec = pl.BlockSpec((None, tr, tc), lambda l, i, j, pr: (l, i, pr[0] * (n // tc) + j))
    else:
        p_spec = pl.BlockSpec((None, None, tr, tc), lambda l, i, j, pr: (l, pr[0], i, j))
    lands = [pl.BlockSpec((None, None, tr, tc), functools.partial(lambda l, i, j, pr, s: (s, l, i, j), s=s))
             for s in range(3)]
    grid_spec = pltpu.PrefetchScalarGridSpec(
        num_scalar_prefetch=1, grid=(L, h // tr, n // tc), in_specs=[p_spec] + lands,
        out_specs=pl.BlockSpec((None, tr, tc), lambda l, i, j, pr: (l, pr[1] * (h // tr) + i, j)))
    return pl.pallas_call(kern, grid_spec=grid_spec, out_shape=jax.ShapeDtypeStruct((L, r, n), F32), name=name,
                          compiler_params=_cparams(("parallel", "parallel", "parallel")))(place, p, land, land, land)


def _reduce_scatter(grads, kinds, shard_shapes, place):
    nw = len(grads)
    g3 = [g if k == "col" else g.reshape(g.shape[0] * 4, g.shape[1] // 4, g.shape[2]) for g, k in zip(grads, kinds)]
    lands = _swap_halves(g3, name="rs_swap_halves")
    ps = [_pair_sum(g3[w], lands[w], place, name=f"rs_pair_sum_{w}") for w in range(nw)]
    ps = [p if k == "col" else p.reshape(p.shape[0] // 4, 4, p.shape[1], p.shape[2]) for p, k in zip(ps, kinds)]
    parts = _scatter_quarters(ps, kinds, name="rs_scatter_quarters")
    halves = [_quarter_sum(ps[w], parts[w], kinds[w], shard_shapes[w], place, name=f"rs_quarter_sum_{w}")
              for w in range(nw)]
    return _share_halves(halves, name="rs_share_halves")


_WEIGHTS = ["mod_w", "mod_b", "norm1_g", "w_in", "gate_b", "conv_w", "conv_b", "conv_ln_g", "conv_ln_b", "w_conv_proj",
            "hgrn_lb", "hgrn_norm_g", "w_hgrn_proj", "sb_qn_g", "sb_kn_g", "w_sb_proj", "w_out", "norm2_g", "mlp_w1",
            "mlp_w2"]
_BIG = [("w_in", "col"), ("w_conv_proj", "col"), ("w_hgrn_proj", "col"), ("w_sb_proj", "col"), ("w_out", "row"),
        ("mlp_w1", "col"), ("mlp_w2", "row")]
_REPLICATED = ["mod_b", "norm1_g", "gate_b", "conv_b", "conv_ln_g", "conv_ln_b", "hgrn_lb", "hgrn_norm_g", "sb_qn_g",
               "sb_kn_g", "norm2_g"]
LANES = 128


class _Pack:
    def __init__(self, items):
        self.shapes = {n: a.shape for n, a in items}
        self.offsets, pos = {}, 0
        for n, a in items:
            self.offsets[n] = pos
            pos += math.prod(a.shape)
        self.rows = -(-pos // (8 * LANES)) * 8
        flat = jnp.concatenate([a.reshape(-1).astype(F32) for _, a in items])
        self.array = jnp.pad(flat, (0, self.rows * LANES - pos)).reshape(self.rows, LANES)

    def get(self, packed, name):
        lead = packed.shape[:-2]
        flat = packed.reshape(lead + (self.rows * LANES,))
        n = math.prod(self.shapes[name])
        return lax.slice_in_dim(flat, self.offsets[name], self.offsets[name] + n, axis=len(lead)).reshape(
            lead + self.shapes[name])


def _lower_bounds(hgrn_lb):
    p = jax.nn.softmax(hgrn_lb.astype(F32), axis=0)
    return jnp.cumsum(p, axis=0) - p[0:1]


def _heads(t, S):
    return t.reshape(S, SB_HEADS, SB_DH).transpose(1, 0, 2)


def _unheads(t, S):
    return t.transpose(1, 0, 2).reshape(S, SB_HEADS * SB_DH)


def _layer_fwd(x, w, p, l):
    S, D = x.shape
    dff = w["mlp_w1"].shape[2]
    r = {"x": x}
    (r["h"],) = _rowop(_fn_normmod, [(x, 0, D)], [p["n1g"], p["sc1"], p["sh1"]], [(D, BF16)], name=f"normmod1_fwd_{l}")
    proj = r["proj"] = _matmul(r["h"], w["w_in"], bl=l, name=f"w_in_fwd_{l}")
    r["cpre"] = _conv_fwd(proj, p["w32"], p["conv_b"], name=f"conv_fwd_{l}")
    (r["cact"],) = _rowop(_fn_lnsilu, [(r["cpre"], 0, CONV_CH)], [p["lng"], p["lnb"]], [(CONV_CH, BF16)],
                          name=f"conv_ln_fwd_{l}")
    r["hg"], r["states"] = _hgrn_fwd(proj, p["lbk"], p["ng"], name=f"hgrn_fwd_{l}")
    r["sq"], r["sk"], r["sv"] = (_heads(proj[:, OFF_SB + k * 512:OFF_SB + (k + 1) * 512], S) for k in range(3))
    so, r["rs"] = _sb_fwd(r["sq"], r["sk"], r["sv"], p["qg"], p["kg"], name=f"sb_fwd_{l}")
    r["sb"] = _unheads(so, S).astype(BF16)
    r["y_c"] = _matmul(r["cact"], w["w_conv_proj"], bl=l, name=f"w_conv_proj_fwd_{l}")
    r["y_h"] = _matmul(r["hg"], w["w_hgrn_proj"], bl=l, name=f"w_hgrn_proj_fwd_{l}")
    r["y_s"] = _matmul(r["sb"], w["w_sb_proj"], bl=l, name=f"w_sb_proj_fwd_{l}")
    (r["merged"],) = _rowop(_fn_merge, [(proj, OFF_GL, 3 * D), (r["y_c"], 0, D), (r["y_h"], 0, D), (r["y_s"], 0, D)],
                            [p["gate_b"]], [(D, BF16)], name=f"merge_fwd_{l}")
    r["a_out"] = _matmul(r["merged"], w["w_out"], bl=l, name=f"w_out_fwd_{l}")
    (r["x1"],) = _rowop(_fn_resid, [(x, 0, D), (r["a_out"], 0, D)], [p["g1"]], [(D, F32)], name=f"resid1_fwd_{l}")
    (r["h2"],) = _rowop(_fn_normmod, [(r["x1"], 0, D)], [p["n2g"], p["sc2"], p["sh2"]], [(D, BF16)],
                        name=f"normmod2_fwd_{l}")
    r["u"] = _matmul(r["h2"], w["mlp_w1"], bl=l, name=f"mlp_w1_fwd_{l}")
    (r["act"],) = _rowop(_fn_relu2, [(r["u"], 0, dff)], [], [(dff, BF16)], name=f"relu2_fwd_{l}")
    r["m_out"] = _matmul(r["act"], w["mlp_w2"], bl=l, name=f"mlp_w2_fwd_{l}")
    (x2,) = _rowop(_fn_resid, [(r["x1"], 0, D), (r["m_out"], 0, D)], [p["g2"]], [(D, F32)], name=f"resid2_fwd_{l}")
    return x2, r


def _layer_bwd(dx2, r, w, p, l, n_layers, grads):
    S, D = dx2.shape
    dff = w["mlp_w1"].shape[2]
    small = {}

    def dweight(name, a, dy):
        grads[name] = _matmul(a, dy, ta=True, layer=l, n_layers=n_layers, into=grads.get(name),
                              name=f"{name}_dw_{l}")

    (dm_out,), (dg2,) = _rowop_bwd(_fn_scale, [(r["m_out"], 0, D)], [p["g2"]], [dx2], [BF16], name=f"resid2_bwd_{l}")
    dact = _matmul(dm_out, w["mlp_w2"], tb=True, bl=l, name=f"mlp_w2_dx_{l}")
    dweight("mlp_w2", r["act"], dm_out)
    (du,), _ = _rowop_bwd(_fn_relu2, [(r["u"], 0, dff)], [], [dact], [BF16], name=f"relu2_bwd_{l}")
    dh2 = _matmul(du, w["mlp_w1"], tb=True, bl=l, name=f"mlp_w1_dx_{l}")
    dweight("mlp_w1", r["h2"], du)
    (dx1,), (small["norm2_g"], dsc2, dsh2) = _rowop_bwd(
        _fn_normmod, [(r["x1"], 0, D)], [p["n2g"], p["sc2"], p["sh2"]], [dh2], [F32], add={0: dx2},
        name=f"normmod2_bwd_{l}")
    (da_out,), (dg1,) = _rowop_bwd(_fn_scale, [(r["a_out"], 0, D)], [p["g1"]], [dx1], [BF16], name=f"resid1_bwd_{l}")
    dmerged = _matmul(da_out, w["w_out"], tb=True, bl=l, name=f"w_out_dx_{l}")
    dweight("w_out", r["merged"], da_out)
    (dgl, dy_c, dy_h, dy_s), (small["gate_b"],) = _rowop_bwd(
        _fn_merge, [(r["proj"], OFF_GL, 3 * D), (r["y_c"], 0, D), (r["y_h"], 0, D), (r["y_s"], 0, D)], [p["gate_b"]],
        [dmerged], [BF16] * 4, name=f"merge_bwd_{l}")
    dcact = _matmul(dy_c, w["w_conv_proj"], tb=True, bl=l, name=f"w_conv_proj_dx_{l}")
    dweight("w_conv_proj", r["cact"], dy_c)
    (dcpre,), (small["conv_ln_g"], small["conv_ln_b"]) = _rowop_bwd(
        _fn_lnsilu, [(r["cpre"], 0, CONV_CH)], [p["lng"], p["lnb"]], [dcact], [F32], name=f"conv_ln_bwd_{l}")
    d_conv, dw32, small["conv_b"] = _conv_bwd(r["proj"], dcpre, p["w32"], name=f"conv_bwd_{l}")
    small["conv_w"] = dw32[:CONV_WIDTH]
    dhg = _matmul(dy_h, w["w_hgrn_proj"], tb=True, bl=l, out_dtype=BF16, name=f"w_hgrn_proj_dx_{l}")
    dweight("w_hgrn_proj", r["hg"], dy_h)
    dq, df, di, dg, dlbk, dng = _hgrn_bwd(r["proj"], r["states"], dhg, p["lbk"], p["ng"], name=f"hgrn_bwd_{l}")
    small["lower"] = -dlbk
    small["hgrn_norm_g"] = jnp.sum(dng, axis=0)
    dsb = _matmul(dy_s, w["w_sb_proj"], tb=True, bl=l, name=f"w_sb_proj_dx_{l}")
    dweight("w_sb_proj", r["sb"], dy_s)
    dsq, dsk, dsv, dqg, dkg = _sb_bwd(r["sq"], r["sk"], r["sv"], p["qg"], p["kg"], r["rs"], _heads(dsb, S),
                                      name=f"sb_bwd_{l}")
    small["sb_qn_g"], small["sb_kn_g"] = jnp.sum(dqg, axis=0), jnp.sum(dkg, axis=0)
    dproj = jnp.concatenate([d_conv, dq, df, di, dg] + [_unheads(t, S).astype(BF16) for t in (dsq, dsk, dsv)] + [dgl],
                            axis=1)
    dh = _matmul(dproj, w["w_in"], tb=True, bl=l, name=f"w_in_dx_{l}")
    dweight("w_in", r["h"], dproj)
    (dx,), (small["norm1_g"], dsc1, dsh1) = _rowop_bwd(
        _fn_normmod, [(r["x"], 0, D)], [p["n1g"], p["sc1"], p["sh1"]], [dh], [F32], add={0: dx1},
        name=f"normmod1_bwd_{l}")
    small["mod"] = jnp.concatenate([dsh1, dsc1, dg1, dsh2, dsc2, dg2], axis=1)
    return dx, small


def kernel(x, c, mod_w, mod_b, norm1_g, w_in, gate_b, conv_w, conv_b, conv_ln_g, conv_ln_b, w_conv_proj, hgrn_lb, hgrn_norm_g, w_hgrn_proj, sb_qn_g, sb_kn_g, w_sb_proj, w_out, norm2_g, mlp_w1, mlp_w2, loss_target, m_mod_w, m_mod_b, m_norm1_g, m_w_in, m_gate_b, m_conv_w, m_conv_b, m_conv_ln_g, m_conv_ln_b, m_w_conv_proj, m_hgrn_lb, m_hgrn_norm_g, m_w_hgrn_proj, m_sb_qn_g, m_sb_kn_g, m_w_sb_proj, m_w_out, m_norm2_g, m_mlp_w1, m_mlp_w2, v_mod_w, v_mod_b, v_norm1_g, v_w_in, v_gate_b, v_conv_w, v_conv_b, v_conv_ln_g, v_conv_ln_b, v_w_conv_proj, v_hgrn_lb, v_hgrn_norm_g, v_w_hgrn_proj, v_sb_qn_g, v_sb_kn_g, v_w_sb_proj, v_w_out, v_norm2_g, v_mlp_w1, v_mlp_w2):
    given = dict(locals())
    wts = {n: given[n] for n in _WEIGHTS}
    mom = {n: given["m_" + n] for n in _WEIGHTS}
    var = {n: given["v_" + n] for n in _WEIGHTS}
    n_layers, D = norm1_g.shape
    xi, yi, ci = _place()
    q = 2 * xi + yi
    me = 4 * xi + 2 * yi + ci
    place = jnp.stack([q, ci]).astype(jnp.int32)
    n_mod = mod_w.shape[2]
    cw = conv_w.shape[2]

    pk1 = _Pack([("c", c), ("conv_w", conv_w)])
    got1 = _all_gather_small(pk1.array, name="gather_cond")
    c_act = jax.nn.silu(pk1.get(got1, "c")[:, 0, :])
    conv_full = jnp.concatenate([pk1.get(got1, "conv_w")[2 * k] for k in range(4)], axis=-1)

    mod_cols = []
    for l in range(n_layers):
        mb = lax.dynamic_slice_in_dim(mod_b[l], q * n_mod, n_mod)
        mod_cols.append(_matmul(c_act, mod_w, bl=l, name=f"mod_fwd_{l}") + mb[None, :])
    got2 = _all_gather_small(jnp.concatenate(mod_cols, axis=0), name="gather_mod")
    mods = []
    for l in range(n_layers):
        row = lax.dynamic_index_in_dim(got2[0::2], l * 8 + me, axis=1, keepdims=False)
        mods.append(jnp.split(row.reshape(1, 4 * n_mod), 6, axis=1))

    lower, lower_vjp = jax.vjp(_lower_bounds, hgrn_lb)

    full = _gather_weights([wts[n].astype(BF16) for n, _ in _BIG], [k for _, k in _BIG], name="gather_weights")
    w = {n: f for (n, _), f in zip(_BIG, full)}

    def layer_params(l):
        sh1, sc1, g1, sh2, sc2, g2 = mods[l]
        return dict(sh1=sh1, sc1=sc1, g1=g1, sh2=sh2, sc2=sc2, g2=g2, n1g=norm1_g[l][None], n2g=norm2_g[l][None],
                    gate_b=gate_b[l][None], conv_b=conv_b[l][None], lng=conv_ln_g[l][None], lnb=conv_ln_b[l][None],
                    w32=jnp.pad(conv_full[l], ((0, CONV_HALO - CONV_WIDTH), (0, 0))), lbk=(1.0 - lower[l])[None],
                    ng=hgrn_norm_g[l][None], qg=sb_qn_g[l][None], kg=sb_kn_g[l][None])

    params = [layer_params(l) for l in range(n_layers)]
    act, saved = x[0], []
    for l in range(n_layers):
        act, r = _layer_fwd(act, w, params[l], l)
        saved.append(r)
    dact, loss_lanes = _loss_head(act, loss_target[0], name="loss_head")

    grads, smalls = {}, [None] * n_layers
    for l in reversed(range(n_layers)):
        dact, smalls[l] = _layer_bwd(dact, saved[l], w, params[l], l, n_layers, grads)
    grad_x = dact[None]

    stack = lambda k: jnp.stack([smalls[l][k] for l in range(n_layers)])
    (d_hgrn_lb,) = lower_vjp(stack("lower")[:, 0, :])
    items = [("loss", loss_lanes), ("mod", stack("mod")), ("hgrn_lb", d_hgrn_lb), ("conv_w", stack("conv_w"))]
    items += [(k, stack(k)) for k in ("norm1_g", "gate_b", "conv_b", "conv_ln_g", "conv_ln_b", "hgrn_norm_g", "sb_qn_g",
                                      "sb_kn_g", "norm2_g")]
    pk3 = _Pack(items)
    got3 = _all_gather_small(pk3.array, name="gather_small_grads")
    tot3 = _sum8(got3, name="sum_small_grads")
    loss = (0.5 / D) * jnp.sum(pk3.get(tot3, "loss"))
    g = {k: pk3.get(tot3, k).reshape(wts[k].shape) for k in _REPLICATED if k != "mod_b"}
    g["mod_b"] = pk3.get(tot3, "mod")[:, 0, :]
    g["conv_w"] = lax.dynamic_slice_in_dim(pk3.get(tot3, "conv_w"), q * cw, cw, axis=2)
    dmod_all = pk3.get(got3, "mod")[:, :, 0, :]
    g_mod_w = None
    for l in range(n_layers):
        cols = lax.dynamic_slice_in_dim(dmod_all[:, l, :], q * n_mod, n_mod, axis=1)
        g_mod_w = _matmul(c_act, cols, ta=True, layer=l, n_layers=n_layers, into=g_mod_w, name=f"mod_dw_{l}")
    g["mod_w"] = g_mod_w

    shard_shapes = [wts[n].shape for n, _ in _BIG]
    red = _reduce_scatter([grads[n] for n, _ in _BIG], [k for _, k in _BIG], shard_shapes, place)
    for (n, _), t in zip(_BIG, red):
        g[n] = t

    delta, new_m, new_v = {}, {}, {}
    two_d = lambda t: t.reshape(-1, t.shape[-1])
    for n in ["mod_w"] + [n for n, _ in _BIG]:
        outs = _adamw(two_d(wts[n]), two_d(g[n]), two_d(mom[n]), two_d(var[n]), name=f"adamw_{n}")
        delta[n], new_m[n], new_v[n] = (t.reshape(wts[n].shape) for t in outs)
    rest = _REPLICATED + ["conv_w"]
    packs = [_Pack([(n, src[n]) for n in rest]) for src in (wts, g, mom, var)]
    outs = _adamw(*[pk.array for pk in packs], name="adamw_small")
    for n in rest:
        delta[n], new_m[n], new_v[n] = (packs[0].get(t, n) for t in outs)

    return (loss, grad_x, *[g[n] for n in _WEIGHTS], *[delta[n] for n in _WEIGHTS], *[new_m[n] for n in _WEIGHTS],
            *[new_v[n] for n in _WEIGHTS])
```

```python
import functools
import math

import jax
import jax.numpy as jnp
from jax import lax
from jax.experimental import pallas as pl
from jax.experimental.pallas import tpu as pltpu

F32 = jnp.float32
BF16 = jnp.bfloat16
MESH = pl.DeviceIdType.MESH

EPS = 1e-6
CONV_CH = 512
CONV_WIDTH = 31
CONV_HALO = 32
HG_HEADS = 4
HG_D = 128
HG_CHUNK = 64
HG_SUB = 16
SB_HEADS = 8
SB_DH = 64
SB_BLK = 128
SB_HEADS_PER_STEP = 1
SB_SKIP = -104.0
OFF_CONV, OFF_HG, OFF_SB, OFF_GL = 0, 1024, 3072, 4608
ADAM_LR, ADAM_B1, ADAM_B2, ADAM_EPS, ADAM_WD, ADAM_STEP = 0.001, 0.9, 0.999, 1e-08, 0.01, 10
VMEM_LIMIT_BYTES = 56 * 1024 * 1024
ROW_TILE = 256


def _cparams(sem=None, **kw):
    return pltpu.CompilerParams(dimension_semantics=sem, vmem_limit_bytes=VMEM_LIMIT_BYTES, **kw)


def _pick(n, cands):
    for c in cands:
        if n % c == 0:
            return c
    return n


def _matmul(a, b, *, ta=False, tb=False, bl=None, out_dtype=F32, name, into=None, layer=None, n_layers=None):
    M, K = (a.shape[1], a.shape[0]) if ta else a.shape
    N = b.shape[-2] if tb else b.shape[-1]
    tm = _pick(M, (512, 256, 128))
    tn = _pick(N, (512, 384, 256, 128))
    tk = _pick(K, (1024, 512, 256, 128))
    nk = K // tk
    a_spec = pl.BlockSpec((tk, tm), lambda i, j, k: (k, i)) if ta else pl.BlockSpec((tm, tk), lambda i, j, k: (i, k))
    if bl is None:
        b_spec = pl.BlockSpec((tn, tk), lambda i, j, k: (j, k)) if tb else pl.BlockSpec((tk, tn), lambda i, j, k: (k, j))
    elif tb:
        b_spec = pl.BlockSpec((None, tn, tk), lambda i, j, k: (bl, j, k))
    else:
        b_spec = pl.BlockSpec((None, tk, tn), lambda i, j, k: (bl, k, j))
    dn = (((0 if ta else 1,), (1 if tb else 0,)), ((), ()))

    def kern(a_ref, b_ref, *rest):
        o_ref = rest[-2] if nk > 1 else rest[-1]
        prod = lax.dot_general(a_ref[...].astype(BF16), b_ref[...].astype(BF16), dn, preferred_element_type=F32)
        if nk == 1:
            o_ref[...] = prod.astype(o_ref.dtype).reshape(o_ref.shape)
            return
        acc_ref = rest[-1]
        k = pl.program_id(2)

        @pl.when(k == 0)
        def _():
            acc_ref[...] = prod

        @pl.when(k > 0)
        def _():
            acc_ref[...] += prod

        @pl.when(k == nk - 1)
        def _():
            o_ref[...] = acc_ref[...].astype(o_ref.dtype).reshape(o_ref.shape)

    in_specs, args, aliases = [a_spec, b_spec], [a, b], {}
    if layer is None:
        out_shape = jax.ShapeDtypeStruct((M, N), out_dtype)
        out_spec = pl.BlockSpec((tm, tn), lambda i, j, k: (i, j))
    else:
        out_shape = jax.ShapeDtypeStruct((n_layers, M, N), out_dtype)
        out_spec = pl.BlockSpec((1, tm, tn), lambda i, j, k: (layer, i, j))
        if into is not None:
            in_specs.append(pl.BlockSpec(memory_space=pl.ANY))
            args.append(into)
            aliases = {2: 0}
    return pl.pallas_call(
        kern, grid=(M // tm, N // tn, nk), in_specs=in_specs, out_specs=out_spec, out_shape=out_shape,
        scratch_shapes=[pltpu.VMEM((tm, tn), F32)] if nk > 1 else [],
        input_output_aliases=aliases, name=name,
        compiler_params=_cparams(("parallel", "parallel", "arbitrary")))(*args)


def _col_specs(off, width, T):
    bw = math.gcd(width, off) if off else width
    return [pl.BlockSpec((T, bw), functools.partial(lambda i, c: (i, c), c=off // bw + p)) for p in range(width // bw)]


def _gather_rows(refs, counts):
    vals, pos = [], 0
    for n in counts:
        parts = [refs[pos + p][...].astype(F32) for p in range(n)]
        pos += n
        vals.append(parts[0] if n == 1 else jnp.concatenate(parts, axis=1))
    return vals, pos


def _rowop(fn, ins, params, outs, *, name):
    S = ins[0][0].shape[0]
    T = min(ROW_TILE, S)
    in_specs, counts, args = [], [], []
    for arr, off, width in ins:
        sp = _col_specs(off, width, T)
        in_specs += sp
        counts.append(len(sp))
        args += [arr] * len(sp)
    in_specs += [pl.BlockSpec(p.shape, lambda i: (0, 0)) for p in params]

    def kern(*refs):
        vals, pos = _gather_rows(refs, counts)
        pv = [refs[pos + p][...] for p in range(len(params))]
        pos += len(params)
        res = fn(*vals, *pv)
        for r, o_ref in zip(res, refs[pos:]):
            o_ref[...] = r.astype(o_ref.dtype)

    return pl.pallas_call(
        kern, grid=(S // T,), in_specs=in_specs,
        out_specs=[pl.BlockSpec((T, w), lambda i: (i, 0)) for w, _ in outs],
        out_shape=[jax.ShapeDtypeStruct((S, w), dt) for w, dt in outs],
        name=name, compiler_params=_cparams(("parallel",)))(*args, *params)


def _rowop_bwd(fn, ins, params, douts, din_dtypes, *, name, add=None):
    add = add or {}
    S = ins[0][0].shape[0]
    T = min(ROW_TILE, S)
    in_specs, counts, args = [], [], []
    for arr, off, width in ins:
        sp = _col_specs(off, width, T)
        in_specs += sp
        counts.append(len(sp))
        args += [arr] * len(sp)
    in_specs += [pl.BlockSpec(p.shape, lambda i: (0, 0)) for p in params]
    in_specs += [pl.BlockSpec((T, d.shape[1]), lambda i: (i, 0)) for d in douts]
    add_keys = sorted(add)
    in_specs += [pl.BlockSpec((T, add[k].shape[1]), lambda i: (i, 0)) for k in add_keys]
    want = [k for k, dt in enumerate(din_dtypes) if dt is not None]

    def kern(*refs):
        vals, pos = _gather_rows(refs, counts)
        pv = [refs[pos + p][...] for p in range(len(params))]
        pos += len(params)
        cts = [refs[pos + p][...].astype(F32) for p in range(len(douts))]
        pos += len(douts)
        adds = {k: refs[pos + p][...].astype(F32) for p, k in enumerate(add_keys)}
        pos += len(add_keys)
        _, vjp = jax.vjp(fn, *vals, *pv)
        grads = vjp(tuple(cts))
        for k in want:
            g = grads[k] + adds[k] if k in adds else grads[k]
            refs[pos][...] = g.astype(refs[pos].dtype)
            pos += 1
        first = pl.program_id(0) == 0
        for p in range(len(params)):
            gp, o_ref = grads[len(ins) + p], refs[pos + p]

            @pl.when(first)
            def _(gp=gp, o_ref=o_ref):
                o_ref[...] = gp

            @pl.when(jnp.logical_not(first))
            def _(gp=gp, o_ref=o_ref):
                o_ref[...] += gp

    out_specs = [pl.BlockSpec((T, ins[k][2]), lambda i: (i, 0)) for k in want]
    out_specs += [pl.BlockSpec(p.shape, lambda i: (0, 0)) for p in params]
    out_shape = [jax.ShapeDtypeStruct((S, ins[k][2]), din_dtypes[k]) for k in want]
    out_shape += [jax.ShapeDtypeStruct(p.shape, F32) for p in params]
    res = pl.pallas_call(
        kern, grid=(S // T,), in_specs=in_specs, out_specs=out_specs, out_shape=out_shape,
        name=name, compiler_params=_cparams(("arbitrary",)))(*args, *params, *douts, *[add[k] for k in add_keys])
    dins = [None] * len(ins)
    for p, k in enumerate(want):
        dins[k] = res[p]
    return dins, list(res[len(want):])


def _rms(x, g):
    return x * lax.rsqrt(jnp.mean(x * x, axis=-1, keepdims=True) + EPS) * g


def _fn_normmod(x, g, sc, sh):
    return (_rms(x, g) * (1.0 + sc) + sh,)


def _fn_lnsilu(c, g, b):
    mu = jnp.mean(c, axis=-1, keepdims=True)
    var = jnp.mean(jnp.square(c - mu), axis=-1, keepdims=True)
    y = (c - mu) * lax.rsqrt(var + EPS) * g + b
    return (y * jax.nn.sigmoid(y),)


def _fn_merge(gl, yc, yh, ys, gb):
    d = yc.shape[1]
    g = jax.nn.sigmoid(gl + gb)
    return (g[:, :d] * yc + g[:, d:2 * d] * yh + g[:, 2 * d:] * ys,)


def _fn_resid(x, y, g):
    return (x + g * y,)


def _fn_scale(y, g):
    return (g * y,)


def _fn_relu2(u):
    return (jnp.square(jnp.maximum(u, 0.0)),)


def _conv_specs(S, T):
    r = T // CONV_HALO
    cur = [pl.BlockSpec((T, CONV_CH), lambda i: (i, 0)), pl.BlockSpec((T, CONV_CH), lambda i: (i, 1))]
    prev = [pl.BlockSpec((CONV_HALO, CONV_CH), lambda i: (jnp.maximum(i * r - 1, 0), 0)),
            pl.BlockSpec((CONV_HALO, CONV_CH), lambda i: (jnp.maximum(i * r - 1, 0), 1))]
    return cur + prev


def _glu_ext(a_ref, g_ref, ah_ref, gh_ref):
    a = a_ref[...]
    sg = jax.nn.sigmoid(g_ref[...])
    uh = jnp.where(pl.program_id(0) > 0, ah_ref[...] * jax.nn.sigmoid(gh_ref[...]), 0.0)
    return a, sg, jnp.concatenate([uh, a * sg], axis=0)


def _shift_up(xe, k, T):
    return xe[:T] if k == 0 else pltpu.roll(xe, shift=xe.shape[0] - k, axis=0)[:T]


def _conv_fwd(proj, w32, b, *, name):
    S = proj.shape[0]
    T = min(ROW_TILE, S)
    lead = CONV_HALO - (CONV_WIDTH - 1)

    def kern(a_ref, g_ref, ah_ref, gh_ref, w_ref, b_ref, o_ref):
        _, _, ue = _glu_ext(a_ref, g_ref, ah_ref, gh_ref)
        acc = jnp.zeros((T, CONV_CH), F32) + b_ref[...]
        for j in range(CONV_WIDTH):
            acc = acc + w_ref[j:j + 1, :] * _shift_up(ue, lead + j, T)
        o_ref[...] = acc

    const = lambda shape: pl.BlockSpec(shape, lambda i: (0, 0))
    return pl.pallas_call(
        kern, grid=(S // T,), in_specs=_conv_specs(S, T) + [const(w32.shape), const(b.shape)],
        out_specs=pl.BlockSpec((T, CONV_CH), lambda i: (i, 0)),
        out_shape=jax.ShapeDtypeStruct((S, CONV_CH), F32), name=name,
        compiler_params=_cparams(("parallel",)))(proj, proj, proj, proj, w32, b)


def _conv_bwd(proj, dc, w32, *, name):
    S = proj.shape[0]
    T = min(ROW_TILE, S)
    nt = S // T
    r = T // CONV_HALO
    lead = CONV_HALO - (CONV_WIDTH - 1)
    last_halo = S // CONV_HALO - 1

    def kern(a_ref, g_ref, ah_ref, gh_ref, dc_ref, dcn_ref, w_ref, dag_ref, dw_ref, db_ref):
        i = pl.program_id(0)
        a, sg, ue = _glu_ext(a_ref, g_ref, ah_ref, gh_ref)
        dc_t = dc_ref[...]
        de = jnp.concatenate([dc_t, jnp.where(i < nt - 1, dcn_ref[...], 0.0)], axis=0)

        @pl.when(i == 0)
        def _():
            dw_ref[...] = jnp.zeros_like(dw_ref)
            db_ref[...] = jnp.zeros_like(db_ref)

        du = jnp.zeros((T, CONV_CH), F32)
        for j in range(CONV_WIDTH):
            du = du + w_ref[j:j + 1, :] * _shift_up(de, CONV_WIDTH - 1 - j, T)
            dw_ref[j:j + 1, :] += jnp.sum(dc_t * _shift_up(ue, lead + j, T), axis=0, keepdims=True)
        db_ref[...] += jnp.sum(dc_t, axis=0, keepdims=True)
        dag_ref[:, :CONV_CH] = (du * sg).astype(BF16)
        dag_ref[:, CONV_CH:] = (du * a * sg * (1.0 - sg)).astype(BF16)

    const = lambda shape: pl.BlockSpec(shape, lambda i: (0, 0))
    in_specs = _conv_specs(S, T) + [
        pl.BlockSpec((T, CONV_CH), lambda i: (i, 0)),
        pl.BlockSpec((CONV_HALO, CONV_CH), lambda i: (jnp.minimum((i + 1) * r, last_halo), 0)),
        const(w32.shape)]
    return pl.pallas_call(
        kern, grid=(nt,), in_specs=in_specs,
        out_specs=[pl.BlockSpec((T, 2 * CONV_CH), lambda i: (i, 0)), const(w32.shape), const((1, CONV_CH))],
        out_shape=[jax.ShapeDtypeStruct((S, 2 * CONV_CH), BF16), jax.ShapeDtypeStruct(w32.shape, F32),
                   jax.ShapeDtypeStruct((1, CONV_CH), F32)],
        name=name, compiler_params=_cparams(("arbitrary",)))(proj, proj, proj, proj, dc, dc, w32)


def _split3(x):
    h = x.astype(BF16)
    r = x - h.astype(F32)
    m = r.astype(BF16)
    return h, m, (r - m.astype(F32)).astype(BF16)


def _xdot(x, u):
    return sum(jnp.dot(p, u, preferred_element_type=F32) for p in _split3(x))


def _xdot_l(m, x):
    return sum(jnp.dot(m, p, preferred_element_type=F32) for p in _split3(x))


def _iota2(shape, dim):
    return lax.broadcasted_iota(jnp.int32, shape, dim)


def _hg_mats():
    n = HG_CHUNK
    r, c = _iota2((n, n), 0), _iota2((n, n), 1)
    low = c <= r
    same = (r // HG_SUB) == (c // HG_SUB)
    up = r <= c
    as_b = lambda m: jnp.where(m, 1.0, 0.0).astype(BF16)
    return dict(low=as_b(low), low_t=as_b(up), blk=as_b(low & same), blk_t=as_b(up & same),
                ones=jnp.ones((n, n), BF16))


@jax.custom_vjp
def _cum(m, m_t, x):
    return _xdot_l(m, x)


def _cum_bwd(res, g):
    m, m_t = res
    return jnp.zeros_like(m), jnp.zeros_like(m_t), _xdot_l(m_t, g)


_cum.defvjp(lambda m, m_t, x: (_xdot_l(m, x), (m, m_t)), _cum_bwd)


def _hg_chunk(q, f, iv, g, st, lbk, ng, mats):
    n, sub = HG_CHUNK, HG_SUB
    kk = lbk * jax.nn.sigmoid(-f)
    lf = jnp.log(1.0 - kk)
    b = _cum(mats["low"], mats["low_t"], lf)
    bs = _cum(mats["blk"], mats["blk_t"], lf)
    bt = _cum(mats["ones"], mats["ones"], lf)
    qh = q * jax.nn.sigmoid(q)
    dot_nt = lambda x, y: lax.dot_general(x.astype(BF16), y.astype(BF16), (((1,), (1,)), ((), ())),
                                          preferred_element_type=F32)
    o = dot_nt(qh * jnp.exp(b), st)
    b0 = b - bs
    qs = qh * jnp.exp(bs)
    col = _iota2((sub, n), 1)
    rows = []
    for blk in range(n // sub):
        lo = blk * sub
        sl = slice(lo, lo + sub)
        acc = o[sl]
        if blk > 0:
            ref = jnp.concatenate([b0[sl]] * (n // sub), axis=0)
            kd = kk * jnp.exp(jnp.minimum(ref - b, 0.0))
            sc = jnp.where(col < lo, dot_nt(qs[sl], kd), 0.0)
            acc = acc + jnp.dot(sc.astype(BF16), iv.astype(BF16), preferred_element_type=F32)
        bq, bk = bs[sl][None, :, :], bs[sl][:, None, :]
        s_i = lax.broadcasted_iota(jnp.int32, (sub, sub, HG_D), 0)
        t_i = lax.broadcasted_iota(jnp.int32, (sub, sub, HG_D), 1)
        keep = s_i <= t_i
        p = jnp.where(keep, qh[sl][None, :, :] * kk[sl][:, None, :] * jnp.exp(jnp.where(keep, bq - bk, 0.0)), 0.0)
        w = jnp.sum(p, axis=-1, keepdims=True)
        acc = acc + jnp.sum(w * iv[sl][:, None, :], axis=0)
        rows.append(acc)
    o = jnp.concatenate(rows, axis=0)
    kd = kk * jnp.exp(bt - b)
    st_new = jnp.exp(bt[0:1]) * st + lax.dot_general(iv.astype(BF16), kd.astype(BF16), (((0,), (0,)), ((), ())),
                                                     preferred_element_type=F32)
    out = _rms(o, ng) * (g * jax.nn.sigmoid(g))
    return out, st_new


def _hg_tile(S):
    return min(512, S)


def _hg_in_specs(rt, rev, nr):
    blk = HG_D
    base = OFF_HG // blk
    row = (lambda r: nr - 1 - r) if rev else (lambda r: r)
    return [pl.BlockSpec((rt, blk), functools.partial(lambda h, r, k: (row(r), base + HG_HEADS * k + h), k=k))
            for k in range(4)]


def _hgrn_fwd(proj, lbk, ng, *, name):
    S = proj.shape[0]
    rt = _hg_tile(S)
    nr, nc = S // rt, rt // HG_CHUNK

    def kern(q_ref, f_ref, i_ref, g_ref, lbk_ref, ng_ref, o_ref, st_out_ref, st_ref):
        @pl.when(pl.program_id(1) == 0)
        def _():
            st_ref[...] = jnp.zeros_like(st_ref)

        mats = _hg_mats()

        def body(c, carry):
            rows = pl.ds(pl.multiple_of(c * HG_CHUNK, HG_CHUNK), HG_CHUNK)
            st = st_ref[...]
            st_out_ref[0, c] = st
            out, st_new = _hg_chunk(q_ref[rows, :], f_ref[rows, :], i_ref[rows, :], g_ref[rows, :], st,
                                    lbk_ref[...], ng_ref[...], mats)
            o_ref[rows, :] = out.astype(o_ref.dtype)
            st_ref[...] = st_new
            return carry

        lax.fori_loop(0, nc, body, 0)

    in_specs = _hg_in_specs(rt, False, nr) + [pl.BlockSpec((1, HG_D), lambda h, r: (0, h)),
                                               pl.BlockSpec((1, HG_D), lambda h, r: (0, 0))]
    return pl.pallas_call(
        kern, grid=(HG_HEADS, nr), in_specs=in_specs,
        out_specs=[pl.BlockSpec((rt, HG_D), lambda h, r: (r, h)),
                   pl.BlockSpec((1, nc, HG_D, HG_D), lambda h, r: (h, r, 0, 0))],
        out_shape=[jax.ShapeDtypeStruct((S, HG_HEADS * HG_D), BF16),
                   jax.ShapeDtypeStruct((HG_HEADS, S // HG_CHUNK, HG_D, HG_D), F32)],
        scratch_shapes=[pltpu.VMEM((HG_D, HG_D), F32)], name=name,
        compiler_params=_cparams(("parallel", "arbitrary")))(proj, proj, proj, proj, lbk, ng)


def _hgrn_bwd(proj, states, dout, lbk, ng, *, name):
    S = proj.shape[0]
    rt = _hg_tile(S)
    nr, nc = S // rt, rt // HG_CHUNK
    width = HG_HEADS * HG_D

    def kern(q_ref, f_ref, i_ref, g_ref, st_in_ref, do_ref, lbk_ref, ng_ref,
             dq_ref, df_ref, di_ref, dg_ref, dlbk_ref, dng_ref, dst_ref):
        @pl.when(pl.program_id(1) == 0)
        def _():
            dst_ref[...] = jnp.zeros_like(dst_ref)
            dlbk_ref[...] = jnp.zeros_like(dlbk_ref)
            dng_ref[...] = jnp.zeros_like(dng_ref)

        mats = _hg_mats()
        fn = functools.partial(_hg_chunk, mats=mats)

        def body(k, carry):
            c = nc - 1 - k
            rows = pl.ds(pl.multiple_of(c * HG_CHUNK, HG_CHUNK), HG_CHUNK)
            _, vjp = jax.vjp(fn, q_ref[rows, :], f_ref[rows, :], i_ref[rows, :], g_ref[rows, :], st_in_ref[0, c],
                             lbk_ref[...], ng_ref[...])
            dq, df, di, dg, dst, dlbk, dng = vjp((do_ref[rows, :].astype(F32), dst_ref[...]))
            dq_ref[rows, :] = dq.astype(BF16)
            df_ref[rows, :] = df.astype(BF16)
            di_ref[rows, :] = di.astype(BF16)
            dg_ref[rows, :] = dg.astype(BF16)
            dst_ref[...] = dst
            dlbk_ref[...] += dlbk
            dng_ref[0] += dng
            return carry

        lax.fori_loop(0, nc, body, 0)

    rev = lambda r: nr - 1 - r
    tile = pl.BlockSpec((rt, HG_D), lambda h, r: (rev(r), h))
    in_specs = _hg_in_specs(rt, True, nr) + [
        pl.BlockSpec((1, nc, HG_D, HG_D), lambda h, r: (h, rev(r), 0, 0)), tile,
        pl.BlockSpec((1, HG_D), lambda h, r: (0, h)), pl.BlockSpec((1, HG_D), lambda h, r: (0, 0))]
    return pl.pallas_call(
        kern, grid=(HG_HEADS, nr), in_specs=in_specs,
        out_specs=[tile, tile, tile, tile, pl.BlockSpec((1, HG_D), lambda h, r: (0, h)),
                   pl.BlockSpec((1, 1, HG_D), lambda h, r: (h, 0, 0))],
        out_shape=[jax.ShapeDtypeStruct((S, width), BF16)] * 4 + [
            jax.ShapeDtypeStruct((1, width), F32), jax.ShapeDtypeStruct((HG_HEADS, 1, HG_D), F32)],
        scratch_shapes=[pltpu.VMEM((HG_D, HG_D), F32)], name=name,
        compiler_params=_cparams(("parallel", "arbitrary")))(proj, proj, proj, proj, states, dout, lbk, ng)


def _sb_block(qi, kj, r_run, diag, after):
    zt = lax.dot_general(kj, qi, (((1,), (1,)), ((), ())), preferred_element_type=F32)
    sp = jnp.maximum(zt, 0.0) + jnp.log(1.0 + jnp.exp(-jnp.abs(zt)))
    lk = -sp
    if diag:
        keep = _iota2(zt.shape, 0) < _iota2(zt.shape, 1)
        lk = jnp.where(keep, lk, 0.0)
    cs = _xdot_l(after, lk)
    a = jnp.exp(zt + lk + cs + r_run)
    if diag:
        a = jnp.where(keep, a, 0.0)
    return sp, a, jnp.sum(lk, axis=0, keepdims=True)


def _sb_tri(later):
    n = SB_BLK
    r, c = _iota2((n, n), 0), _iota2((n, n), 1)
    return jnp.where(c > r if later else c < r, 1.0, 0.0).astype(BF16)


def _fn_qk(x, g):
    return _rms(x, g)


def _sb_fwd(q, k, v, qg, kg, *, name):
    H, S, dh = q.shape
    nb = S // SB_BLK
    scale = dh ** -0.5

    hp = SB_HEADS_PER_STEP

    def kern(q_ref, k_ref, v_ref, qg_ref, kg_ref, o_ref, rs_ref, qn_ref, kn_ref, vb_ref):
        for a in range(hp):
            qn_ref[a] = (_fn_qk(q_ref[a], qg_ref[...]) * scale).astype(BF16)
            kn_ref[a] = _fn_qk(k_ref[a], kg_ref[...]).astype(BF16)
            vb_ref[a] = v_ref[a].astype(BF16)
        after = _sb_tri(True)
        blk = lambda i: pl.ds(pl.multiple_of(i * SB_BLK, SB_BLK), SB_BLK)

        def qblock(i, carry):
            qis = [qn_ref[a, blk(i), :] for a in range(hp)]

            def step(j, diag, st):
                new = []
                for a in range(hp):
                    acc, r_run = st[a]
                    _, wgt, lk_sum = _sb_block(qis[a], kn_ref[a, blk(j), :], r_run, diag, after)
                    av = lax.dot_general(wgt.astype(BF16), vb_ref[a, blk(j), :], (((0,), (0,)), ((), ())),
                                         preferred_element_type=F32)
                    new.append((acc + av, r_run + lk_sum))
                return tuple(new)

            def note(j, st):
                for a in range(hp):
                    rs_ref[a, i, j] = st[a][1]
                return functools.reduce(jnp.maximum, [jnp.max(s[1]) for s in st]) > SB_SKIP

            st = step(i, True, tuple((jnp.zeros((SB_BLK, dh), F32), jnp.zeros((1, SB_BLK), F32)) for _ in range(hp)))
            go = lax.cond(i > 0, lambda: note(i - 1, st).astype(jnp.int32), lambda: jnp.int32(0))

            def body(c):
                jj, _, st = c
                j = i - 1 - jj
                st = step(j, False, st)
                go = lax.cond(j > 0, lambda: note(j - 1, st).astype(jnp.int32), lambda: jnp.int32(0))
                return jj + 1, go, st

            _, _, st = lax.while_loop(lambda c: c[1] > 0, body, (jnp.int32(0), go, st))
            for a in range(hp):
                o_ref[a, blk(i), :] = st[a][0]
            return carry

        lax.fori_loop(0, nb, qblock, 0)

    head = pl.BlockSpec((hp, S, dh), lambda h: (h, 0, 0))
    gain = pl.BlockSpec((1, dh), lambda h: (0, 0))
    return pl.pallas_call(
        kern, grid=(H // hp,), in_specs=[head, head, head, gain, gain],
        out_specs=[head, pl.BlockSpec((hp, nb, nb, 1, SB_BLK), lambda h: (h, 0, 0, 0, 0))],
        out_shape=[jax.ShapeDtypeStruct((H, S, dh), F32), jax.ShapeDtypeStruct((H, nb, nb, 1, SB_BLK), F32)],
        scratch_shapes=[pltpu.VMEM((hp, S, dh), BF16)] * 3, name=name,
        compiler_params=_cparams(("parallel",)))(q, k, v, qg, kg)


def _sb_bwd(q, k, v, qg, kg, rs, do, *, name):
    H, S, dh = q.shape
    nb = S // SB_BLK
    scale = dh ** -0.5
    fn_q = lambda x, g: _fn_qk(x, g) * scale

    hp = SB_HEADS_PER_STEP

    def kern(q_ref, k_ref, v_ref, qg_ref, kg_ref, rs_ref, do_ref, dq_ref, dk_ref, dv_ref, dqg_ref, dkg_ref,
             qn_ref, kn_ref, vb_ref, dob_ref, dqn_ref, dkn_ref, dvs_ref):
        for a in range(hp):
            qn_ref[a] = fn_q(q_ref[a], qg_ref[...]).astype(BF16)
            kn_ref[a] = _fn_qk(k_ref[a], kg_ref[...]).astype(BF16)
            vb_ref[a] = v_ref[a].astype(BF16)
            dob_ref[a] = do_ref[a].astype(BF16)
        dkn_ref[...] = jnp.zeros_like(dkn_ref)
        dvs_ref[...] = jnp.zeros_like(dvs_ref)
        after, before = _sb_tri(True), _sb_tri(False)
        blk = lambda i: pl.ds(pl.multiple_of(i * SB_BLK, SB_BLK), SB_BLK)

        def qblock(i, carry):
            qis = [qn_ref[a, blk(i), :] for a in range(hp)]
            dois = [dob_ref[a, blk(i), :] for a in range(hp)]

            def step(j, diag, st):
                new = []
                for a in range(hp):
                    dqa, e_run = st[a]
                    qi, doi = qis[a], dois[a]
                    kj = kn_ref[a, blk(j), :]
                    vj = vb_ref[a, blk(j), :]
                    r_run = jnp.zeros((1, SB_BLK), F32) if diag else rs_ref[a, i, j]
                    sp, wgt, _ = _sb_block(qi, kj, r_run, diag, after)
                    dp = lax.dot_general(vj, doi, (((1,), (1,)), ((), ())), preferred_element_type=F32)
                    e = dp * wgt
                    e_left = e_run + _xdot_l(before, e)
                    s_neg = jnp.exp(-sp)
                    dz = e * s_neg - e_left * (1.0 - s_neg)
                    if diag:
                        dz = jnp.where(_iota2(dz.shape, 0) < _iota2(dz.shape, 1), dz, 0.0)
                    dzb = dz.astype(BF16)
                    dkn_ref[a, blk(j), :] += jnp.dot(dzb, qi, preferred_element_type=F32)
                    dvs_ref[a, blk(j), :] += jnp.dot(wgt.astype(BF16), doi, preferred_element_type=F32)
                    dqa = dqa + lax.dot_general(dzb, kj, (((0,), (0,)), ((), ())), preferred_element_type=F32)
                    new.append((dqa, e_run + jnp.sum(e, axis=0, keepdims=True)))
                return tuple(new)

            def live(j):
                jc = jnp.maximum(j, 0)
                top = functools.reduce(jnp.maximum, [jnp.max(rs_ref[a, i, jc]) for a in range(hp)])
                return jnp.logical_and(j >= 0, top > SB_SKIP).astype(jnp.int32)

            first, _ = lax.while_loop(lambda c: c[1] > 0, lambda c: (c[0] - 1, live(c[0] - 2)), (i, live(i - 1)))
            st = tuple((jnp.zeros((SB_BLK, dh), F32), jnp.zeros((1, SB_BLK), F32)) for _ in range(hp))
            st = lax.fori_loop(first, i, lambda j, st: step(j, False, st), st)
            st = step(i, True, st)
            for a in range(hp):
                dqn_ref[a, blk(i), :] = st[a][0]
            return carry

        lax.fori_loop(0, nb, qblock, 0)
        for a in range(hp):
            _, vjp_q = jax.vjp(fn_q, q_ref[a], qg_ref[...])
            dq, dqg = vjp_q(dqn_ref[a])
            _, vjp_k = jax.vjp(_fn_qk, k_ref[a], kg_ref[...])
            dk, dkg = vjp_k(dkn_ref[a])
            dq_ref[a] = dq
            dk_ref[a] = dk
            dv_ref[a] = dvs_ref[a]
            dqg_ref[a] = dqg
            dkg_ref[a] = dkg

    head = pl.BlockSpec((hp, S, dh), lambda h: (h, 0, 0))
    gain = pl.BlockSpec((1, dh), lambda h: (0, 0))
    dgain = pl.BlockSpec((hp, 1, dh), lambda h: (h, 0, 0))
    sums = pl.BlockSpec((hp, nb, nb, 1, SB_BLK), lambda h: (h, 0, 0, 0, 0))
    return pl.pallas_call(
        kern, grid=(H // hp,), in_specs=[head, head, head, gain, gain, sums, head],
        out_specs=[head, head, head, dgain, dgain],
        out_shape=[jax.ShapeDtypeStruct((H, S, dh), F32)] * 3 + [jax.ShapeDtypeStruct((H, 1, dh), F32)] * 2,
        scratch_shapes=[pltpu.VMEM((hp, S, dh), BF16)] * 4 + [pltpu.VMEM((hp, S, dh), F32)] * 3,
        name=name, compiler_params=_cparams(("parallel",)))(q, k, v, qg, kg, rs, do)


def _loss_head(y, target, *, name):
    S, D = y.shape
    T = min(ROW_TILE, S)

    def kern(y_ref, t_ref, dy_ref, acc_ref):
        err = y_ref[...] - t_ref[...]
        dy_ref[...] = err * (1.0 / D)
        col = jnp.sum(err * err, axis=0, keepdims=True)
        part = sum(col[:, k * 128:(k + 1) * 128] for k in range(D // 128))

        @pl.when(pl.program_id(0) == 0)
        def _():
            acc_ref[...] = part

        @pl.when(pl.program_id(0) > 0)
        def _():
            acc_ref[...] += part

    tile = pl.BlockSpec((T, D), lambda i: (i, 0))
    return pl.pallas_call(
        kern, grid=(S // T,), in_specs=[tile, tile], out_specs=[tile, pl.BlockSpec((1, 128), lambda i: (0, 0))],
        out_shape=[jax.ShapeDtypeStruct((S, D), F32), jax.ShapeDtypeStruct((1, 128), F32)],
        name=name, compiler_params=_cparams(("arbitrary",)))(y, target)


def _adamw_math(w, g, m, v):
    m = ADAM_B1 * m + (1.0 - ADAM_B1) * g
    v = ADAM_B2 * v + (1.0 - ADAM_B2) * jnp.square(g)
    m_hat = m / (1.0 - ADAM_B1 ** ADAM_STEP)
    v_hat = v / (1.0 - ADAM_B2 ** ADAM_STEP)
    return -ADAM_LR * (m_hat / (jnp.sqrt(v_hat) + ADAM_EPS) + ADAM_WD * w), m, v


def _adamw(w, g, m, v, *, name):
    R, C = w.shape
    T = _pick(R, (256, 128, 64, 32, 16, 8))

    def kern(w_ref, g_ref, m_ref, v_ref, d_ref, mo_ref, vo_ref):
        d, mn, vn = _adamw_math(w_ref[...], g_ref[...], m_ref[...], v_ref[...])
        d_ref[...] = d
        mo_ref[...] = mn
        vo_ref[...] = vn

    tile = pl.BlockSpec((T, C), lambda i: (i, 0))
    return pl.pallas_call(
        kern, grid=(R // T,), in_specs=[tile] * 4, out_specs=[tile] * 3,
        out_shape=[jax.ShapeDtypeStruct((R, C), F32)] * 3, name=name,
        compiler_params=_cparams(("parallel",)))(w, g, m, v)


def _sum8(g, *, name):
    def kern(g_ref, o_ref):
        acc = g_ref[0]
        for d in range(1, g.shape[0]):
            acc = acc + g_ref[d]
        o_ref[...] = acc

    return pl.pallas_call(kern, out_shape=jax.ShapeDtypeStruct(g.shape[1:], F32), name=name,
                          compiler_params=_cparams())(g)


def _place():
    return lax.axis_index("x"), lax.axis_index("y"), lax.axis_index("c")


def _other_chips(x, y):
    return [(1 - x, y), (x, 1 - y), (1 - x, 1 - y)]


def _remote(src, dst, send_sems, recv_sems, k, to):
    return pltpu.make_async_remote_copy(src_ref=src, dst_ref=dst, send_sem=send_sems.at[k], recv_sem=recv_sems.at[k],
                                        device_id=to, device_id_type=MESH)


def _all_gather_small(v, *, name):
    def body(x_ref, out_ref, send_sems, recv_sems, local_sem):
        x, y, c = _place()
        me = 4 * x + 2 * y + c
        mine = pltpu.make_async_copy(x_ref, out_ref.at[me], local_sem)
        mine.start()
        peers = []
        for f in range(1, 8):
            peers.append((1 - x if f & 4 else x, 1 - y if f & 2 else y, 1 - c if f & 1 else c))
        sends = [_remote(x_ref, out_ref.at[me], send_sems, recv_sems, k, p) for k, p in enumerate(peers)]
        for cp in sends:
            cp.start()
        for k, (px, py, pc) in enumerate(peers):
            _remote(x_ref, out_ref.at[4 * px + 2 * py + pc], send_sems, recv_sems, k, (px, py, pc)).wait_recv()
        for cp in sends:
            cp.wait_send()
        mine.wait()

    return pl.pallas_call(
        body, out_shape=jax.ShapeDtypeStruct((8,) + v.shape, v.dtype),
        in_specs=[pl.BlockSpec(memory_space=pltpu.VMEM)], out_specs=pl.BlockSpec(memory_space=pltpu.VMEM),
        scratch_shapes=[pltpu.SemaphoreType.DMA((7,)), pltpu.SemaphoreType.DMA((7,)), pltpu.SemaphoreType.DMA],
        name=name, compiler_params=_cparams())(v)


def _piece(ref, kind, shard_shape, qq, half):
    _, r, n = shard_shape
    h = r // 2
    lo, size = (0, r) if half is None else (half * h, h)
    if kind == "col":
        return ref.at[:, pl.ds(pl.multiple_of(lo, 16), size), pl.ds(pl.multiple_of(qq * n, 128), n)]
    return ref.at[:, pl.ds(pl.multiple_of(qq * r + lo, 16), size), :]


def _gather_weights(shards, kinds, *, name):
    nw = len(shards)
    full = [((s.shape[0], s.shape[1], 4 * s.shape[2]) if k == "col" else (s.shape[0], 4 * s.shape[1], s.shape[2]))
            for s, k in zip(shards, kinds)]

    def body(*refs):
        ins, outs = refs[:nw], refs[nw:2 * nw]
        send_sems, recv_sems, local_sems = refs[2 * nw:]
        x, y, c = _place()
        q = 2 * x + y
        sibling = (x, y, 1 - c)
        chips = _other_chips(x, y)
        local, sent = [], []
        for w in range(nw):
            shp = shards[w].shape
            h = shp[1] // 2
            cp = pltpu.make_async_copy(ins[w], _piece(outs[w], kinds[w], shp, q, None), local_sems.at[w])
            cp.start()
            local.append(cp)
            for j, (cx, cy) in enumerate(chips):
                cp = _remote(ins[w].at[:, pl.ds(pl.multiple_of(c * h, 16), h), :], _piece(outs[w], kinds[w], shp, q, c),
                             send_sems, recv_sems, 6 * w + j, (cx, cy, c))
                cp.start()
                sent.append(cp)
        for w in range(nw):
            for j, (cx, cy) in enumerate(chips):
                win = _piece(outs[w], kinds[w], shards[w].shape, 2 * cx + cy, c)
                _remote(win, win, send_sems, recv_sems, 6 * w + j, (cx, cy, c)).wait_recv()
                cp = _remote(win, win, send_sems, recv_sems, 6 * w + 3 + j, sibling)
                cp.start()
                sent.append(cp)
        for w in range(nw):
            for j, (cx, cy) in enumerate(chips):
                win = _piece(outs[w], kinds[w], shards[w].shape, 2 * cx + cy, 1 - c)
                _remote(win, win, send_sems, recv_sems, 6 * w + 3 + j, sibling).wait_recv()
        for cp in sent:
            cp.wait_send()
        for cp in local:
            cp.wait()

    hbm = pl.BlockSpec(memory_space=pltpu.HBM)
    return pl.pallas_call(
        body, out_shape=[jax.ShapeDtypeStruct(f, s.dtype) for f, s in zip(full, shards)],
        in_specs=[hbm] * nw, out_specs=[hbm] * nw,
        scratch_shapes=[pltpu.SemaphoreType.DMA((6 * nw,)), pltpu.SemaphoreType.DMA((6 * nw,)),
                        pltpu.SemaphoreType.DMA((nw,))],
        name=name, compiler_params=_cparams())(*shards)


def _half_rows(ref, half, h):
    return ref.at[:, pl.ds(pl.multiple_of(half * h, 16), h), :]


def _swap_halves(gs, *, name):
    nw = len(gs)

    def body(*refs):
        ins, lands = refs[:nw], refs[nw:2 * nw]
        send_sems, recv_sems = refs[2 * nw:]
        x, y, c = _place()
        cps = [_remote(_half_rows(ins[w], 1 - c, gs[w].shape[1] // 2), lands[w], send_sems, recv_sems, w, (x, y, 1 - c))
               for w in range(nw)]
        for cp in cps:
            cp.start()
        for cp in cps:
            cp.wait()

    hbm = pl.BlockSpec(memory_space=pltpu.HBM)
    return pl.pallas_call(
        body, out_shape=[jax.ShapeDtypeStruct((g.shape[0], g.shape[1] // 2, g.shape[2]), g.dtype) for g in gs],
        in_specs=[hbm] * nw, out_specs=[hbm] * nw,
        scratch_shapes=[pltpu.SemaphoreType.DMA((nw,)), pltpu.SemaphoreType.DMA((nw,))],
        name=name, compiler_params=_cparams())(*gs)


def _scatter_quarters(ps, kinds, *, name):
    nw = len(ps)
    part = [((p.shape[0], p.shape[1], p.shape[2] // 4) if k == "col" else (p.shape[0], p.shape[2], p.shape[3]))
            for p, k in zip(ps, kinds)]

    def body(*refs):
        ins, lands = refs[:nw], refs[nw:2 * nw]
        send_sems, recv_sems = refs[2 * nw:]
        x, y, c = _place()
        cps = []
        for w in range(nw):
            n = part[w][2]
            for j, (cx, cy) in enumerate(_other_chips(x, y)):
                qj = 2 * cx + cy
                src = ins[w].at[:, :, pl.ds(pl.multiple_of(qj * n, 128), n)] if kinds[w] == "col" else ins[w].at[:, qj]
                cps.append(_remote(src, lands[w].at[j], send_sems, recv_sems, 3 * w + j, (cx, cy, c)))
        for cp in cps:
            cp.start()
        for cp in cps:
            cp.wait()

    hbm = pl.BlockSpec(memory_space=pltpu.HBM)
    return pl.pallas_call(
        body, out_shape=[jax.ShapeDtypeStruct((3,) + pt, p.dtype) for pt, p in zip(part, ps)],
        in_specs=[hbm] * nw, out_specs=[hbm] * nw,
        scratch_shapes=[pltpu.SemaphoreType.DMA((3 * nw,)), pltpu.SemaphoreType.DMA((3 * nw,))],
        name=name, compiler_params=_cparams())(*ps)


def _share_halves(gs, *, name):
    nw = len(gs)

    def body(*refs):
        outs = refs[nw:2 * nw]
        send_sems, recv_sems = refs[2 * nw:]
        x, y, c = _place()
        cps = []
        for w in range(nw):
            win = _half_rows(outs[w], c, gs[w].shape[1] // 2)
            cps.append(_remote(win, win, send_sems, recv_sems, w, (x, y, 1 - c)))
        for cp in cps:
            cp.start()
        for w, cp in enumerate(cps):
            cp.wait_send()
            win = _half_rows(outs[w], 1 - c, gs[w].shape[1] // 2)
            _remote(win, win, send_sems, recv_sems, w, (x, y, 1 - c)).wait_recv()

    hbm = pl.BlockSpec(memory_space=pltpu.HBM)
    return pl.pallas_call(
        body, out_shape=[jax.ShapeDtypeStruct(g.shape, g.dtype) for g in gs],
        in_specs=[hbm] * nw, out_specs=[hbm] * nw, input_output_aliases={w: w for w in range(nw)},
        scratch_shapes=[pltpu.SemaphoreType.DMA((nw,)), pltpu.SemaphoreType.DMA((nw,))],
        name=name, compiler_params=_cparams())(*gs)


def _wide_tile(n):
    return _pick(n, (2048, 1920, 1024, 512, 256, 128))


def _pair_sum(g, land, place, *, name):
    B, R, N = g.shape
    h = R // 2
    tr, tc = _pick(h, (256, 128)), _wide_tile(N)

    def kern(place_ref, g_ref, l_ref, o_ref):
        o_ref[...] = (g_ref[...] + l_ref[...]).astype(o_ref.dtype)

    grid_spec = pltpu.PrefetchScalarGridSpec(
        num_scalar_prefetch=1, grid=(B, h // tr, N // tc),
        in_specs=[pl.BlockSpec((None, tr, tc), lambda b, i, j, p: (b, p[1] * (h // tr) + i, j)),
                  pl.BlockSpec((None, tr, tc), lambda b, i, j, p: (b, i, j))],
        out_specs=pl.BlockSpec((None, tr, tc), lambda b, i, j, p: (b, i, j)))
    return pl.pallas_call(kern, grid_spec=grid_spec, out_shape=jax.ShapeDtypeStruct((B, h, N), BF16), name=name,
                          compiler_params=_cparams(("parallel", "parallel", "parallel")))(place, g, land)


def _quarter_sum(p, land, kind, shard_shape, place, *, name):
    L, r, n = shard_shape
    h = r // 2
    tr, tc = _pick(h, (256, 128)), _wide_tile(n)

    def kern(place_ref, p_ref, a_ref, b_ref, c_ref, o_ref):
        o_ref[...] = ((p_ref[...].astype(F32) + a_ref[...].astype(F32)) + b_ref[...].astype(F32)) + c_ref[...].astype(F32)

    if kind == "col":
        p_spec = pl.BlockSpec((None, tr, tc), lambda l, i, j, pr: (l, i, pr[0] * (n // tc) + j))
    else:
        p_spec = pl.BlockSpec((None, None, tr, tc), lambda l, i, j, pr: (l, pr[0], i, j))
    lands = [pl.BlockSpec((None, None, tr, tc), functools.partial(lambda l, i, j, pr, s: (s, l, i, j), s=s))
             for s in range(3)]
    grid_spec = pltpu.PrefetchScalarGridSpec(
        num_scalar_prefetch=1, grid=(L, h // tr, n // tc), in_specs=[p_spec] + lands,
        out_specs=pl.BlockSpec((None, tr, tc), lambda l, i, j, pr: (l, pr[1] * (h // tr) + i, j)))
    return pl.pallas_call(kern, grid_spec=grid_spec, out_shape=jax.ShapeDtypeStruct((L, r, n), F32), name=name,
                          compiler_params=_cparams(("parallel", "parallel", "parallel")))(place, p, land, land, land)


def _reduce_scatter(grads, kinds, shard_shapes, place):
    nw = len(grads)
    g3 = [g if k == "col" else g.reshape(g.shape[0] * 4, g.shape[1] // 4, g.shape[2]) for g, k in zip(grads, kinds)]
    lands = _swap_halves(g3, name="rs_swap_halves")
    ps = [_pair_sum(g3[w], lands[w], place, name=f"rs_pair_sum_{w}") for w in range(nw)]
    ps = [p if k == "col" else p.reshape(p.shape[0] // 4, 4, p.shape[1], p.shape[2]) for p, k in zip(ps, kinds)]
    parts = _scatter_quarters(ps, kinds, name="rs_scatter_quarters")
    halves = [_quarter_sum(ps[w], parts[w], kinds[w], shard_shapes[w], place, name=f"rs_quarter_sum_{w}")
              for w in range(nw)]
    return _share_halves(halves, name="rs_share_halves")


_WEIGHTS = ["mod_w", "mod_b", "norm1_g", "w_in", "gate_b", "conv_w", "conv_b", "conv_ln_g", "conv_ln_b", "w_conv_proj",
            "hgrn_lb", "hgrn_norm_g", "w_hgrn_proj", "sb_qn_g", "sb_kn_g", "w_sb_proj", "w_out", "norm2_g", "mlp_w1",
            "mlp_w2"]
_BIG = [("w_in", "col"), ("w_conv_proj", "col"), ("w_hgrn_proj", "col"), ("w_sb_proj", "col"), ("w_out", "row"),
        ("mlp_w1", "col"), ("mlp_w2", "row")]
_REPLICATED = ["mod_b", "norm1_g", "gate_b", "conv_b", "conv_ln_g", "conv_ln_b", "hgrn_lb", "hgrn_norm_g", "sb_qn_g",
               "sb_kn_g", "norm2_g"]
LANES = 128


class _Pack:
    def __init__(self, items):
        self.shapes = {n: a.shape for n, a in items}
        self.offsets, pos = {}, 0
        for n, a in items:
            self.offsets[n] = pos
            pos += math.prod(a.shape)
        self.rows = -(-pos // (8 * LANES)) * 8
        flat = jnp.concatenate([a.reshape(-1).astype(F32) for _, a in items])
        self.array = jnp.pad(flat, (0, self.rows * LANES - pos)).reshape(self.rows, LANES)

    def get(self, packed, name):
        lead = packed.shape[:-2]
        flat = packed.reshape(lead + (self.rows * LANES,))
        n = math.prod(self.shapes[name])
        return lax.slice_in_dim(flat, self.offsets[name], self.offsets[name] + n, axis=len(lead)).reshape(
            lead + self.shapes[name])


def _lower_bounds(hgrn_lb):
    p = jax.nn.softmax(hgrn_lb.astype(F32), axis=0)
    return jnp.cumsum(p, axis=0) - p[0:1]


def _heads(t, S):
    return t.reshape(S, SB_HEADS, SB_DH).transpose(1, 0, 2)


def _unheads(t, S):
    return t.transpose(1, 0, 2).reshape(S, SB_HEADS * SB_DH)


def _layer_fwd(x, w, p, l):
    S, D = x.shape
    dff = w["mlp_w1"].shape[2]
    r = {"x": x}
    (r["h"],) = _rowop(_fn_normmod, [(x, 0, D)], [p["n1g"], p["sc1"], p["sh1"]], [(D, BF16)], name=f"normmod1_fwd_{l}")
    proj = r["proj"] = _matmul(r["h"], w["w_in"], bl=l, name=f"w_in_fwd_{l}")
    r["cpre"] = _conv_fwd(proj, p["w32"], p["conv_b"], name=f"conv_fwd_{l}")
    (r["cact"],) = _rowop(_fn_lnsilu, [(r["cpre"], 0, CONV_CH)], [p["lng"], p["lnb"]], [(CONV_CH, BF16)],
                          name=f"conv_ln_fwd_{l}")
    r["hg"], r["states"] = _hgrn_fwd(proj, p["lbk"], p["ng"], name=f"hgrn_fwd_{l}")
    r["sq"], r["sk"], r["sv"] = (_heads(proj[:, OFF_SB + k * 512:OFF_SB + (k + 1) * 512], S) for k in range(3))
    so, r["rs"] = _sb_fwd(r["sq"], r["sk"], r["sv"], p["qg"], p["kg"], name=f"sb_fwd_{l}")
    r["sb"] = _unheads(so, S).astype(BF16)
    r["y_c"] = _matmul(r["cact"], w["w_conv_proj"], bl=l, name=f"w_conv_proj_fwd_{l}")
    r["y_h"] = _matmul(r["hg"], w["w_hgrn_proj"], bl=l, name=f"w_hgrn_proj_fwd_{l}")
    r["y_s"] = _matmul(r["sb"], w["w_sb_proj"], bl=l, name=f"w_sb_proj_fwd_{l}")
    (r["merged"],) = _rowop(_fn_merge, [(proj, OFF_GL, 3 * D), (r["y_c"], 0, D), (r["y_h"], 0, D), (r["y_s"], 0, D)],
                            [p["gate_b"]], [(D, BF16)], name=f"merge_fwd_{l}")
    r["a_out"] = _matmul(r["merged"], w["w_out"], bl=l, name=f"w_out_fwd_{l}")
    (r["x1"],) = _rowop(_fn_resid, [(x, 0, D), (r["a_out"], 0, D)], [p["g1"]], [(D, F32)], name=f"resid1_fwd_{l}")
    (r["h2"],) = _rowop(_fn_normmod, [(r["x1"], 0, D)], [p["n2g"], p["sc2"], p["sh2"]], [(D, BF16)],
                        name=f"normmod2_fwd_{l}")
    r["u"] = _matmul(r["h2"], w["mlp_w1"], bl=l, name=f"mlp_w1_fwd_{l}")
    (r["act"],) = _rowop(_fn_relu2, [(r["u"], 0, dff)], [], [(dff, BF16)], name=f"relu2_fwd_{l}")
    r["m_out"] = _matmul(r["act"], w["mlp_w2"], bl=l, name=f"mlp_w2_fwd_{l}")
    (x2,) = _rowop(_fn_resid, [(r["x1"], 0, D), (r["m_out"], 0, D)], [p["g2"]], [(D, F32)], name=f"resid2_fwd_{l}")
    return x2, r


def _layer_bwd(dx2, r, w, p, l, n_layers, grads):
    S, D = dx2.shape
    dff = w["mlp_w1"].shape[2]
    small = {}

    def dweight(name, a, dy):
        grads[name] = _matmul(a, dy, ta=True, layer=l, n_layers=n_layers, into=grads.get(name),
                              name=f"{name}_dw_{l}")

    (dm_out,), (dg2,) = _rowop_bwd(_fn_scale, [(r["m_out"], 0, D)], [p["g2"]], [dx2], [BF16], name=f"resid2_bwd_{l}")
    dact = _matmul(dm_out, w["mlp_w2"], tb=True, bl=l, name=f"mlp_w2_dx_{l}")
    dweight("mlp_w2", r["act"], dm_out)
    (du,), _ = _rowop_bwd(_fn_relu2, [(r["u"], 0, dff)], [], [dact], [BF16], name=f"relu2_bwd_{l}")
    dh2 = _matmul(du, w["mlp_w1"], tb=True, bl=l, name=f"mlp_w1_dx_{l}")
    dweight("mlp_w1", r["h2"], du)
    (dx1,), (small["norm2_g"], dsc2, dsh2) = _rowop_bwd(
        _fn_normmod, [(r["x1"], 0, D)], [p["n2g"], p["sc2"], p["sh2"]], [dh2], [F32], add={0: dx2},
        name=f"normmod2_bwd_{l}")
    (da_out,), (dg1,) = _rowop_bwd(_fn_scale, [(r["a_out"], 0, D)], [p["g1"]], [dx1], [BF16], name=f"resid1_bwd_{l}")
    dmerged = _matmul(da_out, w["w_out"], tb=True, bl=l, name=f"w_out_dx_{l}")
    dweight("w_out", r["merged"], da_out)
    (dgl, dy_c, dy_h, dy_s), (small["gate_b"],) = _rowop_bwd(
        _fn_merge, [(r["proj"], OFF_GL, 3 * D), (r["y_c"], 0, D), (r["y_h"], 0, D), (r["y_s"], 0, D)], [p["gate_b"]],
        [dmerged], [BF16] * 4, name=f"merge_bwd_{l}")
    dcact = _matmul(dy_c, w["w_conv_proj"], tb=True, bl=l, name=f"w_conv_proj_dx_{l}")
    dweight("w_conv_proj", r["cact"], dy_c)
    (dcpre,), (small["conv_ln_g"], small["conv_ln_b"]) = _rowop_bwd(
        _fn_lnsilu, [(r["cpre"], 0, CONV_CH)], [p["lng"], p["lnb"]], [dcact], [F32], name=f"conv_ln_bwd_{l}")
    d_conv, dw32, small["conv_b"] = _conv_bwd(r["proj"], dcpre, p["w32"], name=f"conv_bwd_{l}")
    small["conv_w"] = dw32[:CONV_WIDTH]
    dhg = _matmul(dy_h, w["w_hgrn_proj"], tb=True, bl=l, out_dtype=BF16, name=f"w_hgrn_proj_dx_{l}")
    dweight("w_hgrn_proj", r["hg"], dy_h)
    dq, df, di, dg, dlbk, dng = _hgrn_bwd(r["proj"], r["states"], dhg, p["lbk"], p["ng"], name=f"hgrn_bwd_{l}")
    small["lower"] = -dlbk
    small["hgrn_norm_g"] = jnp.sum(dng, axis=0)
    dsb = _matmul(dy_s, w["w_sb_proj"], tb=True, bl=l, name=f"w_sb_proj_dx_{l}")
    dweight("w_sb_proj", r["sb"], dy_s)
    dsq, dsk, dsv, dqg, dkg = _sb_bwd(r["sq"], r["sk"], r["sv"], p["qg"], p["kg"], r["rs"], _heads(dsb, S),
                                      name=f"sb_bwd_{l}")
    small["sb_qn_g"], small["sb_kn_g"] = jnp.sum(dqg, axis=0), jnp.sum(dkg, axis=0)
    dproj = jnp.concatenate([d_conv, dq, df, di, dg] + [_unheads(t, S).astype(BF16) for t in (dsq, dsk, dsv)] + [dgl],
                            axis=1)
    dh = _matmul(dproj, w["w_in"], tb=True, bl=l, name=f"w_in_dx_{l}")
    dweight("w_in", r["h"], dproj)
    (dx,), (small["norm1_g"], dsc1, dsh1) = _rowop_bwd(
        _fn_normmod, [(r["x"], 0, D)], [p["n1g"], p["sc1"], p["sh1"]], [dh], [F32], add={0: dx1},
        name=f"normmod1_bwd_{l}")
    small["mod"] = jnp.concatenate([dsh1, dsc1, dg1, dsh2, dsc2, dg2], axis=1)
    return dx, small


def kernel(x, c, mod_w, mod_b, norm1_g, w_in, gate_b, conv_w, conv_b, conv_ln_g, conv_ln_b, w_conv_proj, hgrn_lb, hgrn_norm_g, w_hgrn_proj, sb_qn_g, sb_kn_g, w_sb_proj, w_out, norm2_g, mlp_w1, mlp_w2, loss_target, m_mod_w, m_mod_b, m_norm1_g, m_w_in, m_gate_b, m_conv_w, m_conv_b, m_conv_ln_g, m_conv_ln_b, m_w_conv_proj, m_hgrn_lb, m_hgrn_norm_g, m_w_hgrn_proj, m_sb_qn_g, m_sb_kn_g, m_w_sb_proj, m_w_out, m_norm2_g, m_mlp_w1, m_mlp_w2, v_mod_w, v_mod_b, v_norm1_g, v_w_in, v_gate_b, v_conv_w, v_conv_b, v_conv_ln_g, v_conv_ln_b, v_w_conv_proj, v_hgrn_lb, v_hgrn_norm_g, v_w_hgrn_proj, v_sb_qn_g, v_sb_kn_g, v_w_sb_proj, v_w_out, v_norm2_g, v_mlp_w1, v_mlp_w2):
    given = dict(locals())
    wts = {n: given[n] for n in _WEIGHTS}
    mom = {n: given["m_" + n] for n in _WEIGHTS}
    var = {n: given["v_" + n] for n in _WEIGHTS}
    n_layers, D = norm1_g.shape
    xi, yi, ci = _place()
    q = 2 * xi + yi
    me = 4 * xi + 2 * yi + ci
    place = jnp.stack([q, ci]).astype(jnp.int32)
    n_mod = mod_w.shape[2]
    cw = conv_w.shape[2]

    pk1 = _Pack([("c", c), ("conv_w", conv_w)])
    got1 = _all_gather_small(pk1.array, name="gather_cond")
    c_act = jax.nn.silu(pk1.get(got1, "c")[:, 0, :])
    conv_full = jnp.concatenate([pk1.get(got1, "conv_w")[2 * k] for k in range(4)], axis=-1)

    mod_cols = []
    for l in range(n_layers):
        mb = lax.dynamic_slice_in_dim(mod_b[l], q * n_mod, n_mod)
        mod_cols.append(_matmul(c_act, mod_w, bl=l, name=f"mod_fwd_{l}") + mb[None, :])
    got2 = _all_gather_small(jnp.concatenate(mod_cols, axis=0), name="gather_mod")
    mods = []
    for l in range(n_layers):
        row = lax.dynamic_index_in_dim(got2[0::2], l * 8 + me, axis=1, keepdims=False)
        mods.append(jnp.split(row.reshape(1, 4 * n_mod), 6, axis=1))

    lower, lower_vjp = jax.vjp(_lower_bounds, hgrn_lb)

    full = _gather_weights([wts[n].astype(BF16) for n, _ in _BIG], [k for _, k in _BIG], name="gather_weights")
    w = {n: f for (n, _), f in zip(_BIG, full)}

    def layer_params(l):
        sh1, sc1, g1, sh2, sc2, g2 = mods[l]
        return dict(sh1=sh1, sc1=sc1, g1=g1, sh2=sh2, sc2=sc2, g2=g2, n1g=norm1_g[l][None], n2g=norm2_g[l][None],
                    gate_b=gate_b[l][None], conv_b=conv_b[l][None], lng=conv_ln_g[l][None], lnb=conv_ln_b[l][None],
                    w32=jnp.pad(conv_full[l], ((0, CONV_HALO - CONV_WIDTH), (0, 0))), lbk=(1.0 - lower[l])[None],
                    ng=hgrn_norm_g[l][None], qg=sb_qn_g[l][None], kg=sb_kn_g[l][None])

    params = [layer_params(l) for l in range(n_layers)]
    act, saved = x[0], []
    for l in range(n_layers):
        act, r = _layer_fwd(act, w, params[l], l)
        saved.append(r)
    dact, loss_lanes = _loss_head(act, loss_target[0], name="loss_head")

    grads, smalls = {}, [None] * n_layers
    for l in reversed(range(n_layers)):
        dact, smalls[l] = _layer_bwd(dact, saved[l], w, params[l], l, n_layers, grads)
    grad_x = dact[None]

    stack = lambda k: jnp.stack([smalls[l][k] for l in range(n_layers)])
    (d_hgrn_lb,) = lower_vjp(stack("lower")[:, 0, :])
    items = [("loss", loss_lanes), ("mod", stack("mod")), ("hgrn_lb", d_hgrn_lb), ("conv_w", stack("conv_w"))]
    items += [(k, stack(k)) for k in ("norm1_g", "gate_b", "conv_b", "conv_ln_g", "conv_ln_b", "hgrn_norm_g", "sb_qn_g",
                                      "sb_kn_g", "norm2_g")]
    pk3 = _Pack(items)
    got3 = _all_gather_small(pk3.array, name="gather_small_grads")
    tot3 = _sum8(got3, name="sum_small_grads")
    loss = (0.5 / D) * jnp.sum(pk3.get(tot3, "loss"))
    g = {k: pk3.get(tot3, k).reshape(wts[k].shape) for k in _REPLICATED if k != "mod_b"}
    g["mod_b"] = pk3.get(tot3, "mod")[:, 0, :]
    g["conv_w"] = lax.dynamic_slice_in_dim(pk3.get(tot3, "conv_w"), q * cw, cw, axis=2)
    dmod_all = pk3.get(got3, "mod")[:, :, 0, :]
    g_mod_w = None
    for l in range(n_layers):
        cols = lax.dynamic_slice_in_dim(dmod_all[:, l, :], q * n_mod, n_mod, axis=1)
        g_mod_w = _matmul(c_act, cols, ta=True, layer=l, n_layers=n_layers, into=g_mod_w, name=f"mod_dw_{l}")
    g["mod_w"] = g_mod_w

    shard_shapes = [wts[n].shape for n, _ in _BIG]
    red = _reduce_scatter([grads[n] for n, _ in _BIG], [k for _, k in _BIG], shard_shapes, place)
    for (n, _), t in zip(_BIG, red):
        g[n] = t

    delta, new_m, new_v = {}, {}, {}
    two_d = lambda t: t.reshape(-1, t.shape[-1])
    for n in ["mod_w"] + [n for n, _ in _BIG]:
        outs = _adamw(two_d(wts[n]), two_d(g[n]), two_d(mom[n]), two_d(var[n]), name=f"adamw_{n}")
        delta[n], new_m[n], new_v[n] = (t.reshape(wts[n].shape) for t in outs)
    rest = _REPLICATED + ["conv_w"]
    packs = [_Pack([(n, src[n]) for n in rest]) for src in (wts, g, mom, var)]
    outs = _adamw(*[pk.array for pk in packs], name="adamw_small")
    for n in rest:
        delta[n], new_m[n], new_v[n] = (packs[0].get(t, n) for t in outs)

    return (loss, grad_x, *[g[n] for n in _WEIGHTS], *[delta[n] for n in _WEIGHTS], *[new_m[n] for n in _WEIGHTS],
            *[new_v[n] for n in _WEIGHTS])
```

```python
import functools
import math

import jax
import jax.numpy as jnp
from jax import lax
from jax.experimental import pallas as pl
from jax.experimental.pallas import tpu as pltpu

F32 = jnp.float32
BF16 = jnp.bfloat16
MESH = pl.DeviceIdType.MESH

EPS = 1e-6
CONV_CH = 512
CONV_WIDTH = 31
CONV_HALO = 32
HG_HEADS = 4
HG_D = 128
HG_CHUNK = 64
HG_SUB = 16
SB_HEADS = 8
SB_DH = 64
SB_BLK = 128
SB_PAIR = 128
SB_SKIP = -104.0
OFF_CONV, OFF_HG, OFF_SB, OFF_GL = 0, 1024, 3072, 4608
ADAM_LR, ADAM_B1, ADAM_B2, ADAM_EPS, ADAM_WD, ADAM_STEP = 0.001, 0.9, 0.999, 1e-08, 0.01, 10
VMEM_LIMIT_BYTES = 56 * 1024 * 1024
ROW_TILE = 256


def _cparams(sem=None, **kw):
    return pltpu.CompilerParams(dimension_semantics=sem, vmem_limit_bytes=VMEM_LIMIT_BYTES, **kw)


def _pick(n, cands):
    for c in cands:
        if n % c == 0:
            return c
    return n


MATMUL_VMEM_BUDGET = 40 * 1024 * 1024


def _tile_options(n, cap):
    opts = [t for t in range(cap - cap % 128, 0, -128) if n % t == 0]
    return opts or [n]


def _matmul_tiles(M, N, K, size_a, size_b, size_o, in_acc):
    for tm in _tile_options(M, 1024):
        for tk in _tile_options(K, 2048):
            for tn in _tile_options(N, 1280):
                need = 2 * (tm * tk * size_a + tk * tn * size_b + tm * tn * size_o)
                if K > tk and not in_acc:
                    need += tm * tn * 4
                if need <= MATMUL_VMEM_BUDGET:
                    return tm, tn, tk
    raise ValueError(f"no matmul tiling fits VMEM for {(M, N, K)}")
def _matmul(a, b, *, ta=False, tb=False, bl=None, out_dtype=F32, name, into=None, layer=None, n_layers=None):
    M, K = (a.shape[1], a.shape[0]) if ta else a.shape
    N = b.shape[-2] if tb else b.shape[-1]
    in_acc = jnp.dtype(out_dtype) == jnp.dtype(F32)
    tm, tn, tk = _matmul_tiles(M, N, K, a.dtype.itemsize, b.dtype.itemsize, jnp.dtype(out_dtype).itemsize, in_acc)
    nk = K // tk
    a_spec = pl.BlockSpec((tk, tm), lambda i, j, k: (k, i)) if ta else pl.BlockSpec((tm, tk), lambda i, j, k: (i, k))
    if bl is None:
        b_spec = pl.BlockSpec((tn, tk), lambda i, j, k: (j, k)) if tb else pl.BlockSpec((tk, tn), lambda i, j, k: (k, j))
    elif tb:
        b_spec = pl.BlockSpec((None, tn, tk), lambda i, j, k: (bl, j, k))
    else:
        b_spec = pl.BlockSpec((None, tk, tn), lambda i, j, k: (bl, k, j))
    dn = (((0 if ta else 1,), (1 if tb else 0,)), ((), ()))

    use_scratch = nk > 1 and not in_acc

    def kern(a_ref, b_ref, *rest):
        o_ref = rest[-2] if use_scratch else rest[-1]
        prod = lambda: lax.dot_general(a_ref[...].astype(BF16), b_ref[...].astype(BF16), dn,
                                       preferred_element_type=F32)
        if nk == 1:
            o_ref[...] = prod().astype(o_ref.dtype).reshape(o_ref.shape)
            return
        acc_ref = rest[-1] if use_scratch else o_ref
        k = pl.program_id(2)

        @pl.when(k == 0)
        def _():
            acc_ref[...] = prod().reshape(acc_ref.shape)

        @pl.when(k > 0)
        def _():
            acc_ref[...] += prod().reshape(acc_ref.shape)

        if use_scratch:
            @pl.when(k == nk - 1)
            def _():
                o_ref[...] = acc_ref[...].astype(o_ref.dtype).reshape(o_ref.shape)

    in_specs, args, aliases = [a_spec, b_spec], [a, b], {}
    if layer is None:
        out_shape = jax.ShapeDtypeStruct((M, N), out_dtype)
        out_spec = pl.BlockSpec((tm, tn), lambda i, j, k: (i, j))
    else:
        out_shape = jax.ShapeDtypeStruct((n_layers, M, N), out_dtype)
        out_spec = pl.BlockSpec((1, tm, tn), lambda i, j, k: (layer, i, j))
        if into is not None:
            in_specs.append(pl.BlockSpec(memory_space=pl.ANY))
            args.append(into)
            aliases = {2: 0}
    return pl.pallas_call(
        kern, grid=(M // tm, N // tn, nk), in_specs=in_specs, out_specs=out_spec, out_shape=out_shape,
        scratch_shapes=[pltpu.VMEM((tm, tn), F32)] if use_scratch else [],
        input_output_aliases=aliases, name=name,
        compiler_params=_cparams(("parallel", "parallel", "arbitrary")))(*args)


def _col_specs(off, width, T):
    bw = math.gcd(width, off) if off else width
    return [pl.BlockSpec((T, bw), functools.partial(lambda i, c: (i, c), c=off // bw + p)) for p in range(width // bw)]


def _gather_rows(refs, counts):
    vals, pos = [], 0
    for n in counts:
        parts = [refs[pos + p][...].astype(F32) for p in range(n)]
        pos += n
        vals.append(parts[0] if n == 1 else jnp.concatenate(parts, axis=1))
    return vals, pos


def _rowop(fn, ins, params, outs, *, name):
    S = ins[0][0].shape[0]
    T = min(ROW_TILE, S)
    in_specs, counts, args = [], [], []
    for arr, off, width in ins:
        sp = _col_specs(off, width, T)
        in_specs += sp
        counts.append(len(sp))
        args += [arr] * len(sp)
    in_specs += [pl.BlockSpec(p.shape, lambda i: (0, 0)) for p in params]

    def kern(*refs):
        vals, pos = _gather_rows(refs, counts)
        pv = [refs[pos + p][...] for p in range(len(params))]
        pos += len(params)
        res = fn(*vals, *pv)
        for r, o_ref in zip(res, refs[pos:]):
            o_ref[...] = r.astype(o_ref.dtype)

    return pl.pallas_call(
        kern, grid=(S // T,), in_specs=in_specs,
        out_specs=[pl.BlockSpec((T, w), lambda i: (i, 0)) for w, _ in outs],
        out_shape=[jax.ShapeDtypeStruct((S, w), dt) for w, dt in outs],
        name=name, compiler_params=_cparams(("parallel",)))(*args, *params)


def _rowop_bwd(fn, ins, params, douts, din_dtypes, *, name, add=None):
    add = add or {}
    S = ins[0][0].shape[0]
    T = min(ROW_TILE, S)
    in_specs, counts, args = [], [], []
    for arr, off, width in ins:
        sp = _col_specs(off, width, T)
        in_specs += sp
        counts.append(len(sp))
        args += [arr] * len(sp)
    in_specs += [pl.BlockSpec(p.shape, lambda i: (0, 0)) for p in params]
    in_specs += [pl.BlockSpec((T, d.shape[1]), lambda i: (i, 0)) for d in douts]
    add_keys = sorted(add)
    in_specs += [pl.BlockSpec((T, add[k].shape[1]), lambda i: (i, 0)) for k in add_keys]
    want = [k for k, dt in enumerate(din_dtypes) if dt is not None]

    def kern(*refs):
        vals, pos = _gather_rows(refs, counts)
        pv = [refs[pos + p][...] for p in range(len(params))]
        pos += len(params)
        cts = [refs[pos + p][...].astype(F32) for p in range(len(douts))]
        pos += len(douts)
        adds = {k: refs[pos + p][...].astype(F32) for p, k in enumerate(add_keys)}
        pos += len(add_keys)
        _, vjp = jax.vjp(fn, *vals, *pv)
        grads = vjp(tuple(cts))
        for k in want:
            g = grads[k] + adds[k] if k in adds else grads[k]
            refs[pos][...] = g.astype(refs[pos].dtype)
            pos += 1
        first = pl.program_id(0) == 0
        for p in range(len(params)):
            gp, o_ref = grads[len(ins) + p], refs[pos + p]

            @pl.when(first)
            def _(gp=gp, o_ref=o_ref):
                o_ref[...] = gp

            @pl.when(jnp.logical_not(first))
            def _(gp=gp, o_ref=o_ref):
                o_ref[...] += gp

    out_specs = [pl.BlockSpec((T, ins[k][2]), lambda i: (i, 0)) for k in want]
    out_specs += [pl.BlockSpec(p.shape, lambda i: (0, 0)) for p in params]
    out_shape = [jax.ShapeDtypeStruct((S, ins[k][2]), din_dtypes[k]) for k in want]
    out_shape += [jax.ShapeDtypeStruct(p.shape, F32) for p in params]
    res = pl.pallas_call(
        kern, grid=(S // T,), in_specs=in_specs, out_specs=out_specs, out_shape=out_shape,
        name=name, compiler_params=_cparams(("arbitrary",)))(*args, *params, *douts, *[add[k] for k in add_keys])
    dins = [None] * len(ins)
    for p, k in enumerate(want):
        dins[k] = res[p]
    return dins, list(res[len(want):])


def _rms(x, g):
    return x * lax.rsqrt(jnp.mean(x * x, axis=-1, keepdims=True) + EPS) * g


def _fn_normmod(x, g, sc, sh):
    return (_rms(x, g) * (1.0 + sc) + sh,)


def _fn_lnsilu(c, g, b):
    mu = jnp.mean(c, axis=-1, keepdims=True)
    var = jnp.mean(jnp.square(c - mu), axis=-1, keepdims=True)
    y = (c - mu) * lax.rsqrt(var + EPS) * g + b
    return (y * jax.nn.sigmoid(y),)


def _fn_merge(gl, yc, yh, ys, gb):
    d = yc.shape[1]
    g = jax.nn.sigmoid(gl + gb)
    return (g[:, :d] * yc + g[:, d:2 * d] * yh + g[:, 2 * d:] * ys,)


def _fn_resid(x, y, g):
    return (x + g * y,)


def _fn_scale(y, g):
    return (g * y,)


def _fn_relu2(u):
    return (jnp.square(jnp.maximum(u, 0.0)),)


def _conv_specs(S, T):
    r = T // CONV_HALO
    cur = [pl.BlockSpec((T, CONV_CH), lambda i: (i, 0)), pl.BlockSpec((T, CONV_CH), lambda i: (i, 1))]
    prev = [pl.BlockSpec((CONV_HALO, CONV_CH), lambda i: (jnp.maximum(i * r - 1, 0), 0)),
            pl.BlockSpec((CONV_HALO, CONV_CH), lambda i: (jnp.maximum(i * r - 1, 0), 1))]
    return cur + prev


def _glu_ext(a_ref, g_ref, ah_ref, gh_ref):
    a = a_ref[...]
    sg = jax.nn.sigmoid(g_ref[...])
    uh = jnp.where(pl.program_id(0) > 0, ah_ref[...] * jax.nn.sigmoid(gh_ref[...]), 0.0)
    return a, sg, jnp.concatenate([uh, a * sg], axis=0)


def _shift_up(xe, k, T):
    return xe[:T] if k == 0 else pltpu.roll(xe, shift=xe.shape[0] - k, axis=0)[:T]


def _conv_fwd(proj, w32, b, *, name):
    S = proj.shape[0]
    T = min(ROW_TILE, S)
    lead = CONV_HALO - (CONV_WIDTH - 1)

    def kern(a_ref, g_ref, ah_ref, gh_ref, w_ref, b_ref, o_ref):
        _, _, ue = _glu_ext(a_ref, g_ref, ah_ref, gh_ref)
        acc = jnp.zeros((T, CONV_CH), F32) + b_ref[...]
        for j in range(CONV_WIDTH):
            acc = acc + w_ref[j:j + 1, :] * _shift_up(ue, lead + j, T)
        o_ref[...] = acc

    const = lambda shape: pl.BlockSpec(shape, lambda i: (0, 0))
    return pl.pallas_call(
        kern, grid=(S // T,), in_specs=_conv_specs(S, T) + [const(w32.shape), const(b.shape)],
        out_specs=pl.BlockSpec((T, CONV_CH), lambda i: (i, 0)),
        out_shape=jax.ShapeDtypeStruct((S, CONV_CH), F32), name=name,
        compiler_params=_cparams(("parallel",)))(proj, proj, proj, proj, w32, b)


def _conv_bwd(proj, dc, w32, *, name):
    S = proj.shape[0]
    T = min(ROW_TILE, S)
    nt = S // T
    r = T // CONV_HALO
    lead = CONV_HALO - (CONV_WIDTH - 1)
    last_halo = S // CONV_HALO - 1

    def kern(a_ref, g_ref, ah_ref, gh_ref, dc_ref, dcn_ref, w_ref, dag_ref, dw_ref, db_ref):
        i = pl.program_id(0)
        a, sg, ue = _glu_ext(a_ref, g_ref, ah_ref, gh_ref)
        dc_t = dc_ref[...]
        de = jnp.concatenate([dc_t, jnp.where(i < nt - 1, dcn_ref[...], 0.0)], axis=0)

        @pl.when(i == 0)
        def _():
            dw_ref[...] = jnp.zeros_like(dw_ref)
            db_ref[...] = jnp.zeros_like(db_ref)

        du = jnp.zeros((T, CONV_CH), F32)
        for j in range(CONV_WIDTH):
            du = du + w_ref[j:j + 1, :] * _shift_up(de, CONV_WIDTH - 1 - j, T)
            dw_ref[j:j + 1, :] += jnp.sum(dc_t * _shift_up(ue, lead + j, T), axis=0, keepdims=True)
        db_ref[...] += jnp.sum(dc_t, axis=0, keepdims=True)
        dag_ref[:, :CONV_CH] = (du * sg).astype(BF16)
        dag_ref[:, CONV_CH:] = (du * a * sg * (1.0 - sg)).astype(BF16)

    const = lambda shape: pl.BlockSpec(shape, lambda i: (0, 0))
    in_specs = _conv_specs(S, T) + [
        pl.BlockSpec((T, CONV_CH), lambda i: (i, 0)),
        pl.BlockSpec((CONV_HALO, CONV_CH), lambda i: (jnp.minimum((i + 1) * r, last_halo), 0)),
        const(w32.shape)]
    return pl.pallas_call(
        kern, grid=(nt,), in_specs=in_specs,
        out_specs=[pl.BlockSpec((T, 2 * CONV_CH), lambda i: (i, 0)), const(w32.shape), const((1, CONV_CH))],
        out_shape=[jax.ShapeDtypeStruct((S, 2 * CONV_CH), BF16), jax.ShapeDtypeStruct(w32.shape, F32),
                   jax.ShapeDtypeStruct((1, CONV_CH), F32)],
        name=name, compiler_params=_cparams(("arbitrary",)))(proj, proj, proj, proj, dc, dc, w32)


def _split3(x):
    h = x.astype(BF16)
    r = x - h.astype(F32)
    m = r.astype(BF16)
    return h, m, (r - m.astype(F32)).astype(BF16)


def _xdot_l(m, x):
    return sum(jnp.dot(m, p, preferred_element_type=F32) for p in _split3(x))


def _iota2(shape, dim):
    return lax.broadcasted_iota(jnp.int32, shape, dim)


def _hg_mats():
    n = HG_CHUNK
    r, c = _iota2((n, n), 0), _iota2((n, n), 1)
    low = c <= r
    same = (r // HG_SUB) == (c // HG_SUB)
    up = r <= c
    as_b = lambda m: jnp.where(m, 1.0, 0.0).astype(BF16)
    return dict(low=as_b(low), low_t=as_b(up), blk=as_b(low & same), blk_t=as_b(up & same),
                ones=jnp.ones((n, n), BF16))


@jax.custom_vjp
def _cum(m, m_t, x):
    return _xdot_l(m, x)


def _cum_bwd(res, g):
    m, m_t = res
    return jnp.zeros_like(m), jnp.zeros_like(m_t), _xdot_l(m_t, g)


_cum.defvjp(lambda m, m_t, x: (_xdot_l(m, x), (m, m_t)), _cum_bwd)


def _hg_chunk(q, f, iv, g, st, lbk, ng, mats):
    n, sub = HG_CHUNK, HG_SUB
    kk = lbk * jax.nn.sigmoid(-f)
    lf = jnp.log(1.0 - kk)
    b = _cum(mats["low"], mats["low_t"], lf)
    bs = _cum(mats["blk"], mats["blk_t"], lf)
    bt = _cum(mats["ones"], mats["ones"], lf)
    qh = q * jax.nn.sigmoid(q)
    dot_nt = lambda x, y: lax.dot_general(x.astype(BF16), y.astype(BF16), (((1,), (1,)), ((), ())),
                                          preferred_element_type=F32)
    o = dot_nt(qh * jnp.exp(b), st)
    b0 = b - bs
    qs = qh * jnp.exp(bs)
    col = _iota2((sub, n), 1)
    rows = []
    for blk in range(n // sub):
        lo = blk * sub
        sl = slice(lo, lo + sub)
        acc = o[sl]
        if blk > 0:
            ref = jnp.concatenate([b0[sl]] * (n // sub), axis=0)
            kd = kk * jnp.exp(jnp.minimum(ref - b, 0.0))
            sc = jnp.where(col < lo, dot_nt(qs[sl], kd), 0.0)
            acc = acc + jnp.dot(sc.astype(BF16), iv.astype(BF16), preferred_element_type=F32)
        bq, bk = bs[sl][None, :, :], bs[sl][:, None, :]
        s_i = lax.broadcasted_iota(jnp.int32, (sub, sub, HG_D), 0)
        t_i = lax.broadcasted_iota(jnp.int32, (sub, sub, HG_D), 1)
        keep = s_i <= t_i
        p = jnp.where(keep, qh[sl][None, :, :] * kk[sl][:, None, :] * jnp.exp(jnp.where(keep, bq - bk, 0.0)), 0.0)
        w = jnp.sum(p, axis=-1, keepdims=True)
        acc = acc + jnp.sum(w * iv[sl][:, None, :], axis=0)
        rows.append(acc)
    o = jnp.concatenate(rows, axis=0)
    kd = kk * jnp.exp(bt - b)
    st_new = jnp.exp(bt[0:1]) * st + lax.dot_general(iv.astype(BF16), kd.astype(BF16), (((0,), (0,)), ((), ())),
                                                     preferred_element_type=F32)
    out = _rms(o, ng) * (g * jax.nn.sigmoid(g))
    return out, st_new


def _hg_tile(S):
    return min(512, S)


def _hg_in_specs(rt, rev, nr):
    width = HG_HEADS * HG_D
    base = OFF_HG // width
    row = (lambda r: nr - 1 - r) if rev else (lambda r: r)
    return [pl.BlockSpec((rt, width), functools.partial(lambda r, k: (row(r), base + k), k=k)) for k in range(4)]


def _hg_cols(h):
    return slice(h * HG_D, (h + 1) * HG_D)


def _hgrn_fwd(proj, lbk, ng, *, name):
    S = proj.shape[0]
    rt = _hg_tile(S)
    nr, nc = S // rt, rt // HG_CHUNK

    def kern(q_ref, f_ref, i_ref, g_ref, lbk_ref, ng_ref, o_ref, st_out_ref, st_ref):
        @pl.when(pl.program_id(0) == 0)
        def _():
            st_ref[...] = jnp.zeros_like(st_ref)

        mats = _hg_mats()

        def body(c, carry):
            rows = pl.ds(pl.multiple_of(c * HG_CHUNK, HG_CHUNK), HG_CHUNK)
            for h in range(HG_HEADS):
                cols = _hg_cols(h)
                st = st_ref[h]
                st_out_ref[h, c] = st
                out, st_new = _hg_chunk(q_ref[rows, cols], f_ref[rows, cols], i_ref[rows, cols], g_ref[rows, cols], st,
                                        lbk_ref[:, cols], ng_ref[...], mats)
                o_ref[rows, cols] = out.astype(o_ref.dtype)
                st_ref[h] = st_new
            return carry

        lax.fori_loop(0, nc, body, 0)

    width = HG_HEADS * HG_D
    in_specs = _hg_in_specs(rt, False, nr) + [pl.BlockSpec((1, width), lambda r: (0, 0)),
                                               pl.BlockSpec((1, HG_D), lambda r: (0, 0))]
    return pl.pallas_call(
        kern, grid=(nr,), in_specs=in_specs,
        out_specs=[pl.BlockSpec((rt, width), lambda r: (r, 0)),
                   pl.BlockSpec((HG_HEADS, nc, HG_D, HG_D), lambda r: (0, r, 0, 0))],
        out_shape=[jax.ShapeDtypeStruct((S, width), BF16),
                   jax.ShapeDtypeStruct((HG_HEADS, S // HG_CHUNK, HG_D, HG_D), F32)],
        scratch_shapes=[pltpu.VMEM((HG_HEADS, HG_D, HG_D), F32)], name=name,
        compiler_params=_cparams(("arbitrary",)))(proj, proj, proj, proj, lbk, ng)


def _hgrn_bwd(proj, states, dout, lbk, ng, *, name):
    S = proj.shape[0]
    rt = _hg_tile(S)
    nr, nc = S // rt, rt // HG_CHUNK
    width = HG_HEADS * HG_D

    def kern(q_ref, f_ref, i_ref, g_ref, st_in_ref, do_ref, lbk_ref, ng_ref,
             dq_ref, df_ref, di_ref, dg_ref, dlbk_ref, dng_ref, dst_ref):
        @pl.when(pl.program_id(0) == 0)
        def _():
            dst_ref[...] = jnp.zeros_like(dst_ref)
            dlbk_ref[...] = jnp.zeros_like(dlbk_ref)
            dng_ref[...] = jnp.zeros_like(dng_ref)

        mats = _hg_mats()
        fn = functools.partial(_hg_chunk, mats=mats)

        def body(k, carry):
            c = nc - 1 - k
            rows = pl.ds(pl.multiple_of(c * HG_CHUNK, HG_CHUNK), HG_CHUNK)
            for h in range(HG_HEADS):
                cols = _hg_cols(h)
                _, vjp = jax.vjp(fn, q_ref[rows, cols], f_ref[rows, cols], i_ref[rows, cols], g_ref[rows, cols],
                                 st_in_ref[h, c], lbk_ref[:, cols], ng_ref[...])
                dq, df, di, dg, dst, dlbk, dng = vjp((do_ref[rows, cols].astype(F32), dst_ref[h]))
                dq_ref[rows, cols] = dq.astype(BF16)
                df_ref[rows, cols] = df.astype(BF16)
                di_ref[rows, cols] = di.astype(BF16)
                dg_ref[rows, cols] = dg.astype(BF16)
                dst_ref[h] = dst
                dlbk_ref[:, cols] += dlbk
                dng_ref[h] += dng
            return carry

        lax.fori_loop(0, nc, body, 0)

    rev = lambda r: nr - 1 - r
    tile = pl.BlockSpec((rt, width), lambda r: (rev(r), 0))
    in_specs = _hg_in_specs(rt, True, nr) + [
        pl.BlockSpec((HG_HEADS, nc, HG_D, HG_D), lambda r: (0, rev(r), 0, 0)), tile,
        pl.BlockSpec((1, width), lambda r: (0, 0)), pl.BlockSpec((1, HG_D), lambda r: (0, 0))]
    return pl.pallas_call(
        kern, grid=(nr,), in_specs=in_specs,
        out_specs=[tile, tile, tile, tile, pl.BlockSpec((1, width), lambda r: (0, 0)),
                   pl.BlockSpec((HG_HEADS, 1, HG_D), lambda r: (0, 0, 0))],
        out_shape=[jax.ShapeDtypeStruct((S, width), BF16)] * 4 + [
            jax.ShapeDtypeStruct((1, width), F32), jax.ShapeDtypeStruct((HG_HEADS, 1, HG_D), F32)],
        scratch_shapes=[pltpu.VMEM((HG_HEADS, HG_D, HG_D), F32)], name=name,
        compiler_params=_cparams(("arbitrary",)))(proj, proj, proj, proj, states, dout, lbk, ng)


def _scan_rows(x, later):
    n = x.shape[0]
    row = _iota2(x.shape, 0)
    k = 1
    while k < n:
        if later:
            x = x + jnp.where(row < n - k, pltpu.roll(x, n - k, axis=0), 0.0)
        else:
            x = x + jnp.where(row >= k, pltpu.roll(x, k, axis=0), 0.0)
        k *= 2
    return x


def _sb_block(qi, kj, r_run, diag):
    zt = lax.dot_general(kj, qi, (((1,), (1,)), ((), ())), preferred_element_type=F32)
    sp = jnp.maximum(zt, 0.0) + jnp.log(1.0 + jnp.exp(-jnp.abs(zt)))
    lk = -sp
    if diag:
        keep = _iota2(zt.shape, 0) < _iota2(zt.shape, 1)
        lk = jnp.where(keep, lk, 0.0)
    tail = _scan_rows(lk, True)
    a = jnp.exp(zt + tail + r_run)
    if diag:
        a = jnp.where(keep, a, 0.0)
    return sp, a, tail[0:1, :]


def _sb_norm_pair(x, g2, lane_lo):
    sq = x * x
    ms_lo = jnp.sum(jnp.where(lane_lo, sq, 0.0), axis=-1, keepdims=True)
    ms_hi = jnp.sum(jnp.where(lane_lo, 0.0, sq), axis=-1, keepdims=True)
    return x * lax.rsqrt(jnp.where(lane_lo, ms_lo, ms_hi) * (1.0 / SB_DH) + EPS) * g2


def _sb_specs(S, p_rows):
    base = OFF_SB // SB_PAIR
    per = SB_HEADS * SB_DH // SB_PAIR
    cols = [pl.BlockSpec((S, SB_PAIR), functools.partial(lambda p, k: (0, base + per * k + p), k=k)) for k in range(3)]
    return cols + [pl.BlockSpec((1, SB_PAIR), lambda p: (0, 0))] * 2


def _sb_fwd(proj, qg2, kg2, *, name):
    S = proj.shape[0]
    nb = S // SB_BLK
    pro = min(512, S)
    scale = SB_DH ** -0.5
    n_pairs = SB_HEADS * SB_DH // SB_PAIR

    def kern(q_ref, k_ref, v_ref, qg_ref, kg_ref, o_ref, rs_ref, qm_ref, kp_ref, vt_ref):
        lane_lo = _iota2((pro, SB_PAIR), 1) < SB_DH

        def prologue(t, carry):
            rows = pl.ds(pl.multiple_of(t * pro, pro), pro)
            qn = _sb_norm_pair(q_ref[rows, :], qg_ref[...], lane_lo) * scale
            kp_ref[rows, :] = _sb_norm_pair(k_ref[rows, :], kg_ref[...], lane_lo).astype(BF16)
            v = v_ref[rows, :]
            for a, mine in enumerate((lane_lo, jnp.logical_not(lane_lo))):
                qm_ref[a, rows, :] = jnp.where(mine, qn, 0.0).astype(BF16)
                vt_ref[a, :, rows] = jnp.where(mine, v, 0.0).T.astype(BF16)
            return carry

        lax.fori_loop(0, S // pro, prologue, 0)
        blk = lambda i: pl.ds(pl.multiple_of(i * SB_BLK, SB_BLK), SB_BLK)

        def qblock(i, carry):
            qis = [qm_ref[a, blk(i), :] for a in range(2)]

            def step(j, diag, st):
                kj = kp_ref[blk(j), :]
                new = []
                for a in range(2):
                    acc, r_run = st[a]
                    _, wgt, lk_sum = _sb_block(qis[a], kj, r_run, diag)
                    av = jnp.dot(vt_ref[a, :, blk(j)], wgt.astype(BF16), preferred_element_type=F32)
                    new.append((acc + av, r_run + lk_sum))
                return tuple(new)

            def note(j, st):
                for a in range(2):
                    rs_ref[a, i, pl.ds(j, 1), :] = st[a][1]
                return jnp.maximum(jnp.max(st[0][1]), jnp.max(st[1][1])) > SB_SKIP

            zero = (jnp.zeros((SB_BLK, SB_PAIR), F32), jnp.zeros((1, SB_BLK), F32))
            st = step(i, True, (zero, zero))
            go = lax.cond(i > 0, lambda: note(i - 1, st).astype(jnp.int32), lambda: jnp.int32(0))

            def body(c):
                jj, _, st = c
                j = i - 1 - jj
                st = step(j, False, st)
                go = lax.cond(j > 0, lambda: note(j - 1, st).astype(jnp.int32), lambda: jnp.int32(0))
                return jj + 1, go, st

            _, _, st = lax.while_loop(lambda c: c[1] > 0, body, (jnp.int32(0), go, st))
            o_ref[blk(i), :] = (st[0][0] + st[1][0]).T.astype(o_ref.dtype)
            return carry

        lax.fori_loop(0, nb, qblock, 0)

    width = SB_HEADS * SB_DH
    return pl.pallas_call(
        kern, grid=(n_pairs,), in_specs=_sb_specs(S, pro),
        out_specs=[pl.BlockSpec((S, SB_PAIR), lambda p: (0, p)),
                   pl.BlockSpec((2, nb, nb, SB_BLK), lambda p: (p, 0, 0, 0))],
        out_shape=[jax.ShapeDtypeStruct((S, width), BF16), jax.ShapeDtypeStruct((SB_HEADS, nb, nb, SB_BLK), F32)],
        scratch_shapes=[pltpu.VMEM((2, S, SB_PAIR), BF16), pltpu.VMEM((S, SB_PAIR), BF16),
                        pltpu.VMEM((2, SB_PAIR, S), BF16)],
        name=name, compiler_params=_cparams(("parallel",)))(proj, proj, proj, qg2, kg2)


def _sb_bwd(proj, qg2, kg2, rs, do, *, name):
    S = proj.shape[0]
    nb = S // SB_BLK
    pro = min(512, S)
    scale = SB_DH ** -0.5
    n_pairs = SB_HEADS * SB_DH // SB_PAIR

    def kern(q_ref, k_ref, v_ref, qg_ref, kg_ref, rs_ref, do_ref, dq_ref, dk_ref, dv_ref, dqg_ref, dkg_ref,
             qm_ref, qp_ref, kp_ref, kt_ref, vm_ref, dqn_ref, dkn_ref, dvs_ref):
        lane_lo = _iota2((pro, SB_PAIR), 1) < SB_DH
        fn_q = lambda x, g: _sb_norm_pair(x, g, lane_lo) * scale
        fn_k = lambda x, g: _sb_norm_pair(x, g, lane_lo)

        def prologue(t, carry):
            rows = pl.ds(pl.multiple_of(t * pro, pro), pro)
            qn = fn_q(q_ref[rows, :], qg_ref[...])
            qp_ref[rows, :] = qn.astype(BF16)
            kn = fn_k(k_ref[rows, :], kg_ref[...])
            kp_ref[rows, :] = kn.astype(BF16)
            kt_ref[:, rows] = kn.T.astype(BF16)
            v = v_ref[rows, :]
            for a, mine in enumerate((lane_lo, jnp.logical_not(lane_lo))):
                qm_ref[a, rows, :] = jnp.where(mine, qn, 0.0).astype(BF16)
                vm_ref[a, rows, :] = jnp.where(mine, v, 0.0).astype(BF16)
            return carry

        lax.fori_loop(0, S // pro, prologue, 0)
        dkn_ref[...] = jnp.zeros_like(dkn_ref)
        dvs_ref[...] = jnp.zeros_like(dvs_ref)
        blk = lambda i: pl.ds(pl.multiple_of(i * SB_BLK, SB_BLK), SB_BLK)
        lo_blk = _iota2((SB_PAIR, SB_BLK), 0) < SB_DH

        def qblock(i, carry):
            qis = [qm_ref[a, blk(i), :] for a in range(2)]
            qp = qp_ref[blk(i), :]
            doi = do_ref[blk(i), :]

            def step(j, diag, st):
                kj = kp_ref[blk(j), :]
                new = []
                for a in range(2):
                    dqa, e_run = st[a]
                    r_run = jnp.zeros((1, SB_BLK), F32) if diag else rs_ref[a, i, pl.ds(j, 1), :]
                    sp, wgt, _ = _sb_block(qis[a], kj, r_run, diag)
                    dp = lax.dot_general(vm_ref[a, blk(j), :], doi, (((1,), (1,)), ((), ())),
                                         preferred_element_type=F32)
                    e = dp * wgt
                    head = _scan_rows(e, False)
                    e_left = e_run + (head - e)
                    s_neg = jnp.exp(-sp)
                    dz = e * s_neg - e_left * (1.0 - s_neg)
                    if diag:
                        dz = jnp.where(_iota2(dz.shape, 0) < _iota2(dz.shape, 1), dz, 0.0)
                    dzb = dz.astype(BF16)
                    dkn_ref[a, blk(j), :] += jnp.dot(dzb, qp, preferred_element_type=F32)
                    dvs_ref[a, blk(j), :] += jnp.dot(wgt.astype(BF16), doi, preferred_element_type=F32)
                    dqa = dqa + jnp.dot(kt_ref[:, blk(j)], dzb, preferred_element_type=F32)
                    new.append((dqa, e_run + head[SB_BLK - 1:SB_BLK, :]))
                return tuple(new)

            def live(j):
                jc = jnp.maximum(j, 0)
                top = jnp.maximum(jnp.max(rs_ref[0, i, pl.ds(jc, 1), :]), jnp.max(rs_ref[1, i, pl.ds(jc, 1), :]))
                return jnp.logical_and(j >= 0, top > SB_SKIP).astype(jnp.int32)

            first, _ = lax.while_loop(lambda c: c[1] > 0, lambda c: (c[0] - 1, live(c[0] - 2)), (i, live(i - 1)))
            zero = (jnp.zeros((SB_BLK, SB_PAIR), F32), jnp.zeros((1, SB_BLK), F32))
            st = lax.fori_loop(first, i, lambda j, st: step(j, False, st), (zero, zero))
            st = step(i, True, st)
            dqn_ref[blk(i), :] = jnp.where(lo_blk, st[0][0], st[1][0]).T
            return carry

        lax.fori_loop(0, nb, qblock, 0)
        dqg_ref[...] = jnp.zeros_like(dqg_ref)
        dkg_ref[...] = jnp.zeros_like(dkg_ref)

        def epilogue(t, carry):
            rows = pl.ds(pl.multiple_of(t * pro, pro), pro)
            _, vjp_q = jax.vjp(fn_q, q_ref[rows, :], qg_ref[...])
            dq, dqg = vjp_q(dqn_ref[rows, :])
            _, vjp_k = jax.vjp(fn_k, k_ref[rows, :], kg_ref[...])
            dk, dkg = vjp_k(jnp.where(lane_lo, dkn_ref[0, rows, :], dkn_ref[1, rows, :]))
            dq_ref[rows, :] = dq.astype(BF16)
            dk_ref[rows, :] = dk.astype(BF16)
            dv_ref[rows, :] = jnp.where(lane_lo, dvs_ref[0, rows, :], dvs_ref[1, rows, :]).astype(BF16)
            dqg_ref[0] += dqg
            dkg_ref[0] += dkg
            return carry

        lax.fori_loop(0, S // pro, epilogue, 0)

    width = SB_HEADS * SB_DH
    pair = pl.BlockSpec((S, SB_PAIR), lambda p: (0, p))
    dgain = pl.BlockSpec((1, 1, SB_PAIR), lambda p: (p, 0, 0))
    in_specs = _sb_specs(S, pro) + [pl.BlockSpec((2, nb, nb, SB_BLK), lambda p: (p, 0, 0, 0)), pair]
    return pl.pallas_call(
        kern, grid=(n_pairs,), in_specs=in_specs, out_specs=[pair, pair, pair, dgain, dgain],
        out_shape=[jax.ShapeDtypeStruct((S, width), BF16)] * 3 + [jax.ShapeDtypeStruct((n_pairs, 1, SB_PAIR), F32)] * 2,
        scratch_shapes=[pltpu.VMEM((2, S, SB_PAIR), BF16), pltpu.VMEM((S, SB_PAIR), BF16), pltpu.VMEM((S, SB_PAIR), BF16),
                        pltpu.VMEM((SB_PAIR, S), BF16), pltpu.VMEM((2, S, SB_PAIR), BF16), pltpu.VMEM((S, SB_PAIR), F32),
                        pltpu.VMEM((2, S, SB_PAIR), F32), pltpu.VMEM((2, S, SB_PAIR), F32)],
        name=name, compiler_params=_cparams(("parallel",)))(proj, proj, proj, qg2, kg2, rs, do)


def _loss_head(y, target, *, name):
    S, D = y.shape
    T = min(ROW_TILE, S)

    def kern(y_ref, t_ref, dy_ref, acc_ref):
        err = y_ref[...] - t_ref[...]
        dy_ref[...] = err * (1.0 / D)
        col = jnp.sum(err * err, axis=0, keepdims=True)
        part = sum(col[:, k * 128:(k + 1) * 128] for k in range(D // 128))

        @pl.when(pl.program_id(0) == 0)
        def _():
            acc_ref[...] = part

        @pl.when(pl.program_id(0) > 0)
        def _():
            acc_ref[...] += part

    tile = pl.BlockSpec((T, D), lambda i: (i, 0))
    return pl.pallas_call(
        kern, grid=(S // T,), in_specs=[tile, tile], out_specs=[tile, pl.BlockSpec((1, 128), lambda i: (0, 0))],
        out_shape=[jax.ShapeDtypeStruct((S, D), F32), jax.ShapeDtypeStruct((1, 128), F32)],
        name=name, compiler_params=_cparams(("arbitrary",)))(y, target)


def _adamw_math(w, g, m, v):
    m = ADAM_B1 * m + (1.0 - ADAM_B1) * g
    v = ADAM_B2 * v + (1.0 - ADAM_B2) * jnp.square(g)
    m_hat = m / (1.0 - ADAM_B1 ** ADAM_STEP)
    v_hat = v / (1.0 - ADAM_B2 ** ADAM_STEP)
    return -ADAM_LR * (m_hat / (jnp.sqrt(v_hat) + ADAM_EPS) + ADAM_WD * w), m, v


def _adamw(w, g, m, v, *, name):
    R, C = w.shape
    T = _pick(R, (256, 128, 64, 32, 16, 8))

    def kern(w_ref, g_ref, m_ref, v_ref, d_ref, mo_ref, vo_ref):
        d, mn, vn = _adamw_math(w_ref[...], g_ref[...], m_ref[...], v_ref[...])
        d_ref[...] = d
        mo_ref[...] = mn
        vo_ref[...] = vn

    tile = pl.BlockSpec((T, C), lambda i: (i, 0))
    return pl.pallas_call(
        kern, grid=(R // T,), in_specs=[tile] * 4, out_specs=[tile] * 3,
        out_shape=[jax.ShapeDtypeStruct((R, C), F32)] * 3, name=name,
        compiler_params=_cparams(("parallel",)))(w, g, m, v)


def _sum8(g, *, name):
    def kern(g_ref, o_ref):
        acc = g_ref[0]
        for d in range(1, g.shape[0]):
            acc = acc + g_ref[d]
        o_ref[...] = acc

    return pl.pallas_call(kern, out_shape=jax.ShapeDtypeStruct(g.shape[1:], F32), name=name,
                          compiler_params=_cparams())(g)


def _place():
    return lax.axis_index("x"), lax.axis_index("y"), lax.axis_index("c")


def _other_chips(x, y):
    return [(1 - x, y), (x, 1 - y), (1 - x, 1 - y)]


def _remote(src, dst, send_sems, recv_sems, k, to):
    return pltpu.make_async_remote_copy(src_ref=src, dst_ref=dst, send_sem=send_sems.at[k], recv_sem=recv_sems.at[k],
                                        device_id=to, device_id_type=MESH)


def _all_gather_small(v, *, name):
    def body(x_ref, out_ref, send_sems, recv_sems, local_sem):
        x, y, c = _place()
        me = 4 * x + 2 * y + c
        mine = pltpu.make_async_copy(x_ref, out_ref.at[me], local_sem)
        mine.start()
        peers = []
        for f in range(1, 8):
            peers.append((1 - x if f & 4 else x, 1 - y if f & 2 else y, 1 - c if f & 1 else c))
        sends = [_remote(x_ref, out_ref.at[me], send_sems, recv_sems, k, p) for k, p in enumerate(peers)]
        for cp in sends:
            cp.start()
        for k, (px, py, pc) in enumerate(peers):
            _remote(x_ref, out_ref.at[4 * px + 2 * py + pc], send_sems, recv_sems, k, (px, py, pc)).wait_recv()
        for cp in sends:
            cp.wait_send()
        mine.wait()

    return pl.pallas_call(
        body, out_shape=jax.ShapeDtypeStruct((8,) + v.shape, v.dtype),
        in_specs=[pl.BlockSpec(memory_space=pltpu.VMEM)], out_specs=pl.BlockSpec(memory_space=pltpu.VMEM),
        scratch_shapes=[pltpu.SemaphoreType.DMA((7,)), pltpu.SemaphoreType.DMA((7,)), pltpu.SemaphoreType.DMA],
        name=name, compiler_params=_cparams())(v)


def _piece(ref, kind, shard_shape, qq, half):
    _, r, n = shard_shape
    h = r // 2
    lo, size = (0, r) if half is None else (half * h, h)
    if kind == "col":
        return ref.at[:, pl.ds(pl.multiple_of(lo, 16), size), pl.ds(pl.multiple_of(qq * n, 128), n)]
    return ref.at[:, pl.ds(pl.multiple_of(qq * r + lo, 16), size), :]


def _gather_weights(shards, kinds, *, name):
    nw = len(shards)
    full = [((s.shape[0], s.shape[1], 4 * s.shape[2]) if k == "col" else (s.shape[0], 4 * s.shape[1], s.shape[2]))
            for s, k in zip(shards, kinds)]

    def body(*refs):
        ins, outs = refs[:nw], refs[nw:2 * nw]
        send_sems, recv_sems, local_sems = refs[2 * nw:]
        x, y, c = _place()
        q = 2 * x + y
        sibling = (x, y, 1 - c)
        chips = _other_chips(x, y)
        local, sent = [], []
        for w in range(nw):
            shp = shards[w].shape
            h = shp[1] // 2
            cp = pltpu.make_async_copy(ins[w], _piece(outs[w], kinds[w], shp, q, None), local_sems.at[w])
            cp.start()
            local.append(cp)
            for j, (cx, cy) in enumerate(chips):
                cp = _remote(ins[w].at[:, pl.ds(pl.multiple_of(c * h, 16), h), :], _piece(outs[w], kinds[w], shp, q, c),
                             send_sems, recv_sems, 6 * w + j, (cx, cy, c))
                cp.start()
                sent.append(cp)
        for w in range(nw):
            for j, (cx, cy) in enumerate(chips):
                win = _piece(outs[w], kinds[w], shards[w].shape, 2 * cx + cy, c)
                _remote(win, win, send_sems, recv_sems, 6 * w + j, (cx, cy, c)).wait_recv()
                cp = _remote(win, win, send_sems, recv_sems, 6 * w + 3 + j, sibling)
                cp.start()
                sent.append(cp)
        for w in range(nw):
            for j, (cx, cy) in enumerate(chips):
                win = _piece(outs[w], kinds[w], shards[w].shape, 2 * cx + cy, 1 - c)
                _remote(win, win, send_sems, recv_sems, 6 * w + 3 + j, sibling).wait_recv()
        for cp in sent:
            cp.wait_send()
        for cp in local:
            cp.wait()

    hbm = pl.BlockSpec(memory_space=pltpu.HBM)
    return pl.pallas_call(
        body, out_shape=[jax.ShapeDtypeStruct(f, s.dtype) for f, s in zip(full, shards)],
        in_specs=[hbm] * nw, out_specs=[hbm] * nw,
        scratch_shapes=[pltpu.SemaphoreType.DMA((6 * nw,)), pltpu.SemaphoreType.DMA((6 * nw,)),
                        pltpu.SemaphoreType.DMA((nw,))],
        name=name, compiler_params=_cparams())(*shards)


def _half_rows(ref, half, h):
    return ref.at[:, pl.ds(pl.multiple_of(half * h, 16), h), :]


def _swap_halves(gs, *, name):
    nw = len(gs)

    def body(*refs):
        ins, lands = refs[:nw], refs[nw:2 * nw]
        send_sems, recv_sems = refs[2 * nw:]
        x, y, c = _place()
        cps = [_remote(_half_rows(ins[w], 1 - c, gs[w].shape[1] // 2), lands[w], send_sems, recv_sems, w, (x, y, 1 - c))
               for w in range(nw)]
        for cp in cps:
            cp.start()
        for cp in cps:
            cp.wait()

    hbm = pl.BlockSpec(memory_space=pltpu.HBM)
    return pl.pallas_call(
        body, out_shape=[jax.ShapeDtypeStruct((g.shape[0], g.shape[1] // 2, g.shape[2]), g.dtype) for g in gs],
        in_specs=[hbm] * nw, out_specs=[hbm] * nw,
        scratch_shapes=[pltpu.SemaphoreType.DMA((nw,)), pltpu.SemaphoreType.DMA((nw,))],
        name=name, compiler_params=_cparams())(*gs)


def _scatter_quarters(ps, kinds, *, name):
    nw = len(ps)
    part = [((p.shape[0], p.shape[1], p.shape[2] // 4) if k == "col" else (p.shape[0], p.shape[2], p.shape[3]))
            for p, k in zip(ps, kinds)]

    def body(*refs):
        ins, lands = refs[:nw], refs[nw:2 * nw]
        send_sems, recv_sems = refs[2 * nw:]
        x, y, c = _place()
        cps = []
        for w in range(nw):
            n = part[w][2]
            for j, (cx, cy) in enumerate(_other_chips(x, y)):
                qj = 2 * cx + cy
                src = ins[w].at[:, :, pl.ds(pl.multiple_of(qj * n, 128), n)] if kinds[w] == "col" else ins[w].at[:, qj]
                cps.append(_remote(src, lands[w].at[j], send_sems, recv_sems, 3 * w + j, (cx, cy, c)))
        for cp in cps:
            cp.start()
        for cp in cps:
            cp.wait()

    hbm = pl.BlockSpec(memory_space=pltpu.HBM)
    return pl.pallas_call(
        body, out_shape=[jax.ShapeDtypeStruct((3,) + pt, p.dtype) for pt, p in zip(part, ps)],
        in_specs=[hbm] * nw, out_specs=[hbm] * nw,
        scratch_shapes=[pltpu.SemaphoreType.DMA((3 * nw,)), pltpu.SemaphoreType.DMA((3 * nw,))],
        name=name, compiler_params=_cparams())(*ps)


def _share_halves(gs, *, name):
    nw = len(gs)

    def body(*refs):
        outs = refs[nw:2 * nw]
        send_sems, recv_sems = refs[2 * nw:]
        x, y, c = _place()
        cps = []
        for w in range(nw):
            win = _half_rows(outs[w], c, gs[w].shape[1] // 2)
            cps.append(_remote(win, win, send_sems, recv_sems, w, (x, y, 1 - c)))
        for cp in cps:
            cp.start()
        for w, cp in enumerate(cps):
            cp.wait_send()
            win = _half_rows(outs[w], 1 - c, gs[w].shape[1] // 2)
            _remote(win, win, send_sems, recv_sems, w, (x, y, 1 - c)).wait_recv()

    hbm = pl.BlockSpec(memory_space=pltpu.HBM)
    return pl.pallas_call(
        body, out_shape=[jax.ShapeDtypeStruct(g.shape, g.dtype) for g in gs],
        in_specs=[hbm] * nw, out_specs=[hbm] * nw, input_output_aliases={w: w for w in range(nw)},
        scratch_shapes=[pltpu.SemaphoreType.DMA((nw,)), pltpu.SemaphoreType.DMA((nw,))],
        name=name, compiler_params=_cparams())(*gs)


def _wide_tile(n):
    return _pick(n, (2048, 1920, 1024, 512, 256, 128))


def _pair_sum(g, land, place, *, name):
    B, R, N = g.shape
    h = R // 2
    tr, tc = _pick(h, (256, 128)), _wide_tile(N)

    def kern(place_ref, g_ref, l_ref, o_ref):
        o_ref[...] = (g_ref[...] + l_ref[...]).astype(o_ref.dtype)

    grid_spec = pltpu.PrefetchScalarGridSpec(
        num_scalar_prefetch=1, grid=(B, h // tr, N // tc),
        in_specs=[pl.BlockSpec((None, tr, tc), lambda b, i, j, p: (b, p[1] * (h // tr) + i, j)),
                  pl.BlockSpec((None, tr, tc), lambda b, i, j, p: (b, i, j))],
        out_specs=pl.BlockSpec((None, tr, tc), lambda b, i, j, p: (b, i, j)))
    return pl.pallas_call(kern, grid_spec=grid_spec, out_shape=jax.ShapeDtypeStruct((B, h, N), BF16), name=name,
                          compiler_params=_cparams(("parallel", "parallel", "parallel")))(place, g, land)


def _quarter_sum(p, land, kind, shard_shape, place, *, name):
    L, r, n = shard_shape
    h = r // 2
    tr, tc = _pick(h, (256, 128)), _wide_tile(n)

    def kern(place_ref, p_ref, a_ref, b_ref, c_ref, o_ref):
        o_ref[...] = ((p_ref[...].astype(F32) + a_ref[...].astype(F32)) + b_ref[...].astype(F32)) + c_ref[...].astype(F32)

    if kind == "col":
        p_spec = pl.BlockSpec((None, tr, tc), lambda l, i, j, pr: (l, i, pr[0] * (n // tc) + j))
    else:
        p_spec = pl.BlockSpec((None, None, tr, tc), lambda l, i, j, pr: (l, pr[0], i, j))
    lands = [pl.BlockSpec((None, None, tr, tc), functools.partial(lambda l, i, j, pr, s: (s, l, i, j), s=s))
             for s in range(3)]
    grid_spec = pltpu.PrefetchScalarGridSpec(
        num_scalar_prefetch=1, grid=(L, h // tr, n // tc), in_specs=[p_spec] + lands,
        out_specs=pl.BlockSpec((None, tr, tc), lambda l, i, j, pr: (l, pr[1] * (h // tr) + i, j)))
    return pl.pallas_call(kern, grid_spec=grid_spec, out_shape=jax.ShapeDtypeStruct((L, r, n), F32), name=name,
                          compiler_params=_cparams(("parallel", "parallel", "parallel")))(place, p, land, land, land)


def _reduce_scatter(grads, kinds, shard_shapes, place):
    nw = len(grads)
    g3 = [g if k == "col" else g.reshape(g.shape[0] * 4, g.shape[1] // 4, g.shape[2]) for g, k in zip(grads, kinds)]
    lands = _swap_halves(g3, name="rs_swap_halves")
    ps = [_pair_sum(g3[w], lands[w], place, name=f"rs_pair_sum_{w}") for w in range(nw)]
    ps = [p if k == "col" else p.reshape(p.shape[0] // 4, 4, p.shape[1], p.shape[2]) for p, k in zip(ps, kinds)]
    parts = _scatter_quarters(ps, kinds, name="rs_scatter_quarters")
    halves = [_quarter_sum(ps[w], parts[w], kinds[w], shard_shapes[w], place, name=f"rs_quarter_sum_{w}")
              for w in range(nw)]
    return _share_halves(halves, name="rs_share_halves")


_WEIGHTS = ["mod_w", "mod_b", "norm1_g", "w_in", "gate_b", "conv_w", "conv_b", "conv_ln_g", "conv_ln_b", "w_conv_proj",
            "hgrn_lb", "hgrn_norm_g", "w_hgrn_proj", "sb_qn_g", "sb_kn_g", "w_sb_proj", "w_out", "norm2_g", "mlp_w1",
            "mlp_w2"]
_BIG = [("w_in", "col"), ("w_conv_proj", "col"), ("w_hgrn_proj", "col"), ("w_sb_proj", "col"), ("w_out", "row"),
        ("mlp_w1", "col"), ("mlp_w2", "row")]
_REPLICATED = ["mod_b", "norm1_g", "gate_b", "conv_b", "conv_ln_g", "conv_ln_b", "hgrn_lb", "hgrn_norm_g", "sb_qn_g",
               "sb_kn_g", "norm2_g"]
LANES = 128


class _Pack:
    def __init__(self, items):
        self.shapes = {n: a.shape for n, a in items}
        self.offsets, pos = {}, 0
        for n, a in items:
            self.offsets[n] = pos
            pos += math.prod(a.shape)
        self.rows = -(-pos // (8 * LANES)) * 8
        flat = jnp.concatenate([a.reshape(-1).astype(F32) for _, a in items])
        self.array = jnp.pad(flat, (0, self.rows * LANES - pos)).reshape(self.rows, LANES)

    def get(self, packed, name):
        lead = packed.shape[:-2]
        flat = packed.reshape(lead + (self.rows * LANES,))
        n = math.prod(self.shapes[name])
        return lax.slice_in_dim(flat, self.offsets[name], self.offsets[name] + n, axis=len(lead)).reshape(
            lead + self.shapes[name])


def _lower_bounds(hgrn_lb):
    p = jax.nn.softmax(hgrn_lb.astype(F32), axis=0)
    return jnp.cumsum(p, axis=0) - p[0:1]


def _layer_fwd(x, w, p, l):
    S, D = x.shape
    dff = w["mlp_w1"].shape[2]
    r = {"x": x}
    (r["h"],) = _rowop(_fn_normmod, [(x, 0, D)], [p["n1g"], p["sc1"], p["sh1"]], [(D, BF16)], name=f"normmod1_fwd_{l}")
    proj = r["proj"] = _matmul(r["h"], w["w_in"], bl=l, name=f"w_in_fwd_{l}")
    r["cpre"] = _conv_fwd(proj, p["w32"], p["conv_b"], name=f"conv_fwd_{l}")
    (r["cact"],) = _rowop(_fn_lnsilu, [(r["cpre"], 0, CONV_CH)], [p["lng"], p["lnb"]], [(CONV_CH, BF16)],
                          name=f"conv_ln_fwd_{l}")
    r["hg"], r["states"] = _hgrn_fwd(proj, p["lbk"], p["ng"], name=f"hgrn_fwd_{l}")
    r["sb"], r["rs"] = _sb_fwd(proj, p["qg"], p["kg"], name=f"sb_fwd_{l}")
    r["y_c"] = _matmul(r["cact"], w["w_conv_proj"], bl=l, name=f"w_conv_proj_fwd_{l}")
    r["y_h"] = _matmul(r["hg"], w["w_hgrn_proj"], bl=l, name=f"w_hgrn_proj_fwd_{l}")
    r["y_s"] = _matmul(r["sb"], w["w_sb_proj"], bl=l, name=f"w_sb_proj_fwd_{l}")
    (r["merged"],) = _rowop(_fn_merge, [(proj, OFF_GL, 3 * D), (r["y_c"], 0, D), (r["y_h"], 0, D), (r["y_s"], 0, D)],
                            [p["gate_b"]], [(D, BF16)], name=f"merge_fwd_{l}")
    r["a_out"] = _matmul(r["merged"], w["w_out"], bl=l, name=f"w_out_fwd_{l}")
    (r["x1"],) = _rowop(_fn_resid, [(x, 0, D), (r["a_out"], 0, D)], [p["g1"]], [(D, F32)], name=f"resid1_fwd_{l}")
    (r["h2"],) = _rowop(_fn_normmod, [(r["x1"], 0, D)], [p["n2g"], p["sc2"], p["sh2"]], [(D, BF16)],
                        name=f"normmod2_fwd_{l}")
    r["u"] = _matmul(r["h2"], w["mlp_w1"], bl=l, name=f"mlp_w1_fwd_{l}")
    (r["act"],) = _rowop(_fn_relu2, [(r["u"], 0, dff)], [], [(dff, BF16)], name=f"relu2_fwd_{l}")
    r["m_out"] = _matmul(r["act"], w["mlp_w2"], bl=l, name=f"mlp_w2_fwd_{l}")
    (x2,) = _rowop(_fn_resid, [(r["x1"], 0, D), (r["m_out"], 0, D)], [p["g2"]], [(D, F32)], name=f"resid2_fwd_{l}")
    return x2, r


def _layer_bwd(dx2, r, w, p, l, n_layers, grads):
    S, D = dx2.shape
    dff = w["mlp_w1"].shape[2]
    small = {}

    def dweight(name, a, dy):
        grads[name] = _matmul(a, dy, ta=True, layer=l, n_layers=n_layers, into=grads.get(name),
                              name=f"{name}_dw_{l}")

    (dm_out,), (dg2,) = _rowop_bwd(_fn_scale, [(r["m_out"], 0, D)], [p["g2"]], [dx2], [BF16], name=f"resid2_bwd_{l}")
    dact = _matmul(dm_out, w["mlp_w2"], tb=True, bl=l, name=f"mlp_w2_dx_{l}")
    dweight("mlp_w2", r["act"], dm_out)
    (du,), _ = _rowop_bwd(_fn_relu2, [(r["u"], 0, dff)], [], [dact], [BF16], name=f"relu2_bwd_{l}")
    dh2 = _matmul(du, w["mlp_w1"], tb=True, bl=l, name=f"mlp_w1_dx_{l}")
    dweight("mlp_w1", r["h2"], du)
    (dx1,), (small["norm2_g"], dsc2, dsh2) = _rowop_bwd(
        _fn_normmod, [(r["x1"], 0, D)], [p["n2g"], p["sc2"], p["sh2"]], [dh2], [F32], add={0: dx2},
        name=f"normmod2_bwd_{l}")
    (da_out,), (dg1,) = _rowop_bwd(_fn_scale, [(r["a_out"], 0, D)], [p["g1"]], [dx1], [BF16], name=f"resid1_bwd_{l}")
    dmerged = _matmul(da_out, w["w_out"], tb=True, bl=l, name=f"w_out_dx_{l}")
    dweight("w_out", r["merged"], da_out)
    (dgl, dy_c, dy_h, dy_s), (small["gate_b"],) = _rowop_bwd(
        _fn_merge, [(r["proj"], OFF_GL, 3 * D), (r["y_c"], 0, D), (r["y_h"], 0, D), (r["y_s"], 0, D)], [p["gate_b"]],
        [dmerged], [BF16] * 4, name=f"merge_bwd_{l}")
    dcact = _matmul(dy_c, w["w_conv_proj"], tb=True, bl=l, name=f"w_conv_proj_dx_{l}")
    dweight("w_conv_proj", r["cact"], dy_c)
    (dcpre,), (small["conv_ln_g"], small["conv_ln_b"]) = _rowop_bwd(
        _fn_lnsilu, [(r["cpre"], 0, CONV_CH)], [p["lng"], p["lnb"]], [dcact], [F32], name=f"conv_ln_bwd_{l}")
    d_conv, dw32, small["conv_b"] = _conv_bwd(r["proj"], dcpre, p["w32"], name=f"conv_bwd_{l}")
    small["conv_w"] = dw32[:CONV_WIDTH]
    dhg = _matmul(dy_h, w["w_hgrn_proj"], tb=True, bl=l, out_dtype=BF16, name=f"w_hgrn_proj_dx_{l}")
    dweight("w_hgrn_proj", r["hg"], dy_h)
    dq, df, di, dg, dlbk, dng = _hgrn_bwd(r["proj"], r["states"], dhg, p["lbk"], p["ng"], name=f"hgrn_bwd_{l}")
    small["lower"] = -dlbk
    small["hgrn_norm_g"] = jnp.sum(dng, axis=0)
    dsb = _matmul(dy_s, w["w_sb_proj"], tb=True, bl=l, out_dtype=BF16, name=f"w_sb_proj_dx_{l}")
    dweight("w_sb_proj", r["sb"], dy_s)
    dsq, dsk, dsv, dqg, dkg = _sb_bwd(r["proj"], p["qg"], p["kg"], r["rs"], dsb, name=f"sb_bwd_{l}")
    fold = lambda t: jnp.sum(t.reshape(-1, SB_DH), axis=0, keepdims=True)
    small["sb_qn_g"], small["sb_kn_g"] = fold(dqg), fold(dkg)
    dproj = jnp.concatenate([d_conv, dq, df, di, dg, dsq, dsk, dsv, dgl], axis=1)
    dh = _matmul(dproj, w["w_in"], tb=True, bl=l, name=f"w_in_dx_{l}")
    dweight("w_in", r["h"], dproj)
    (dx,), (small["norm1_g"], dsc1, dsh1) = _rowop_bwd(
        _fn_normmod, [(r["x"], 0, D)], [p["n1g"], p["sc1"], p["sh1"]], [dh], [F32], add={0: dx1},
        name=f"normmod1_bwd_{l}")
    small["mod"] = jnp.concatenate([dsh1, dsc1, dg1, dsh2, dsc2, dg2], axis=1)
    return dx, small


def kernel(x, c, mod_w, mod_b, norm1_g, w_in, gate_b, conv_w, conv_b, conv_ln_g, conv_ln_b, w_conv_proj, hgrn_lb, hgrn_norm_g, w_hgrn_proj, sb_qn_g, sb_kn_g, w_sb_proj, w_out, norm2_g, mlp_w1, mlp_w2, loss_target, m_mod_w, m_mod_b, m_norm1_g, m_w_in, m_gate_b, m_conv_w, m_conv_b, m_conv_ln_g, m_conv_ln_b, m_w_conv_proj, m_hgrn_lb, m_hgrn_norm_g, m_w_hgrn_proj, m_sb_qn_g, m_sb_kn_g, m_w_sb_proj, m_w_out, m_norm2_g, m_mlp_w1, m_mlp_w2, v_mod_w, v_mod_b, v_norm1_g, v_w_in, v_gate_b, v_conv_w, v_conv_b, v_conv_ln_g, v_conv_ln_b, v_w_conv_proj, v_hgrn_lb, v_hgrn_norm_g, v_w_hgrn_proj, v_sb_qn_g, v_sb_kn_g, v_w_sb_proj, v_w_out, v_norm2_g, v_mlp_w1, v_mlp_w2):
    given = dict(locals())
    wts = {n: given[n] for n in _WEIGHTS}
    mom = {n: given["m_" + n] for n in _WEIGHTS}
    var = {n: given["v_" + n] for n in _WEIGHTS}
    n_layers, D = norm1_g.shape
    xi, yi, ci = _place()
    q = 2 * xi + yi
    me = 4 * xi + 2 * yi + ci
    place = jnp.stack([q, ci]).astype(jnp.int32)
    n_mod = mod_w.shape[2]
    cw = conv_w.shape[2]

    pk1 = _Pack([("c", c), ("conv_w", conv_w)])
    got1 = _all_gather_small(pk1.array, name="gather_cond")
    c_act = jax.nn.silu(pk1.get(got1, "c")[:, 0, :])
    conv_full = jnp.concatenate([pk1.get(got1, "conv_w")[2 * k] for k in range(4)], axis=-1)

    mod_cols = []
    for l in range(n_layers):
        mb = lax.dynamic_slice_in_dim(mod_b[l], q * n_mod, n_mod)
        mod_cols.append(_matmul(c_act, mod_w, bl=l, name=f"mod_fwd_{l}") + mb[None, :])
    got2 = _all_gather_small(jnp.concatenate(mod_cols, axis=0), name="gather_mod")
    mods = []
    for l in range(n_layers):
        row = lax.dynamic_index_in_dim(got2[0::2], l * 8 + me, axis=1, keepdims=False)
        mods.append(jnp.split(row.reshape(1, 4 * n_mod), 6, axis=1))

    lower, lower_vjp = jax.vjp(_lower_bounds, hgrn_lb)

    full = _gather_weights([wts[n].astype(BF16) for n, _ in _BIG], [k for _, k in _BIG], name="gather_weights")
    w = {n: f for (n, _), f in zip(_BIG, full)}

    def layer_params(l):
        sh1, sc1, g1, sh2, sc2, g2 = mods[l]
        return dict(sh1=sh1, sc1=sc1, g1=g1, sh2=sh2, sc2=sc2, g2=g2, n1g=norm1_g[l][None], n2g=norm2_g[l][None],
                    gate_b=gate_b[l][None], conv_b=conv_b[l][None], lng=conv_ln_g[l][None], lnb=conv_ln_b[l][None],
                    w32=jnp.pad(conv_full[l], ((0, CONV_HALO - CONV_WIDTH), (0, 0))), lbk=(1.0 - lower[l])[None],
                    ng=hgrn_norm_g[l][None], qg=jnp.tile(sb_qn_g[l][None], (1, SB_PAIR // SB_DH)),
                    kg=jnp.tile(sb_kn_g[l][None], (1, SB_PAIR // SB_DH)))

    params = [layer_params(l) for l in range(n_layers)]
    act, saved = x[0], []
    for l in range(n_layers):
        act, r = _layer_fwd(act, w, params[l], l)
        saved.append(r)
    dact, loss_lanes = _loss_head(act, loss_target[0], name="loss_head")

    grads, smalls = {}, [None] * n_layers
    for l in reversed(range(n_layers)):
        dact, smalls[l] = _layer_bwd(dact, saved[l], w, params[l], l, n_layers, grads)
    grad_x = dact[None]

    stack = lambda k: jnp.stack([smalls[l][k] for l in range(n_layers)])
    (d_hgrn_lb,) = lower_vjp(stack("lower")[:, 0, :])
    items = [("loss", loss_lanes), ("mod", stack("mod")), ("hgrn_lb", d_hgrn_lb), ("conv_w", stack("conv_w"))]
    items += [(k, stack(k)) for k in ("norm1_g", "gate_b", "conv_b", "conv_ln_g", "conv_ln_b", "hgrn_norm_g", "sb_qn_g",
                                      "sb_kn_g", "norm2_g")]
    pk3 = _Pack(items)
    got3 = _all_gather_small(pk3.array, name="gather_small_grads")
    tot3 = _sum8(got3, name="sum_small_grads")
    loss = (0.5 / D) * jnp.sum(pk3.get(tot3, "loss"))
    g = {k: pk3.get(tot3, k).reshape(wts[k].shape) for k in _REPLICATED if k != "mod_b"}
    g["mod_b"] = pk3.get(tot3, "mod")[:, 0, :]
    g["conv_w"] = lax.dynamic_slice_in_dim(pk3.get(tot3, "conv_w"), q * cw, cw, axis=2)
    dmod_all = pk3.get(got3, "mod")[:, :, 0, :]
    g_mod_w = None
    for l in range(n_layers):
        cols = lax.dynamic_slice_in_dim(dmod_all[:, l, :], q * n_mod, n_mod, axis=1)
        g_mod_w = _matmul(c_act, cols, ta=True, layer=l, n_layers=n_layers, into=g_mod_w, name=f"mod_dw_{l}")
    g["mod_w"] = g_mod_w

    shard_shapes = [wts[n].shape for n, _ in _BIG]
    red = _reduce_scatter([grads[n] for n, _ in _BIG], [k for _, k in _BIG], shard_shapes, place)
    for (n, _), t in zip(_BIG, red):
        g[n] = t

    delta, new_m, new_v = {}, {}, {}
    two_d = lambda t: t.reshape(-1, t.shape[-1])
    for n in ["mod_w"] + [n for n, _ in _BIG]:
        outs = _adamw(two_d(wts[n]), two_d(g[n]), two_d(mom[n]), two_d(var[n]), name=f"adamw_{n}")
        delta[n], new_m[n], new_v[n] = (t.reshape(wts[n].shape) for t in outs)
    rest = _REPLICATED + ["conv_w"]
    packs = [_Pack([(n, src[n]) for n in rest]) for src in (wts, g, mom, var)]
    outs = _adamw(*[pk.array for pk in packs], name="adamw_small")
    for n in rest:
        delta[n], new_m[n], new_v[n] = (packs[0].get(t, n) for t in outs)

    return (loss, grad_x, *[g[n] for n in _WEIGHTS], *[delta[n] for n in _WEIGHTS], *[new_m[n] for n in _WEIGHTS],
            *[new_v[n] for n in _WEIGHTS])
```

```python
import functools
import math

import jax
import jax.numpy as jnp
from jax import lax
from jax.experimental import pallas as pl
from jax.experimental.pallas import tpu as pltpu

F32 = jnp.float32
BF16 = jnp.bfloat16
MESH = pl.DeviceIdType.MESH

EPS = 1e-6
CONV_CH = 512
CONV_WIDTH = 31
CONV_HALO = 32
HG_HEADS = 4
HG_D = 128
HG_CHUNK = 64
HG_SUB = 16
SB_HEADS = 8
SB_DH = 64
SB_BLK = 128
SB_PAIR = 128
SB_SKIP = -104.0
OFF_CONV, OFF_HG, OFF_SB, OFF_GL = 0, 1024, 3072, 4608
ADAM_LR, ADAM_B1, ADAM_B2, ADAM_EPS, ADAM_WD, ADAM_STEP = 0.001, 0.9, 0.999, 1e-08, 0.01, 10
VMEM_LIMIT_BYTES = 56 * 1024 * 1024
ROW_TILE = 256


def _cparams(sem=None, **kw):
    return pltpu.CompilerParams(dimension_semantics=sem, vmem_limit_bytes=VMEM_LIMIT_BYTES, **kw)


def _pick(n, cands):
    for c in cands:
        if n % c == 0:
            return c
    return n


MATMUL_VMEM_BUDGET = 40 * 1024 * 1024


def _tile_options(n, cap):
    opts = [t for t in range(cap - cap % 128, 0, -128) if n % t == 0]
    return opts or [n]


def _matmul_tiles(M, N, K, size_a, size_b, size_o, in_acc):
    for tm in _tile_options(M, 1024):
        for tk in _tile_options(K, 2048):
            for tn in _tile_options(N, 1280):
                need = 2 * (tm * tk * size_a + tk * tn * size_b + tm * tn * size_o)
                if K > tk and not in_acc:
                    need += tm * tn * 4
                if need <= MATMUL_VMEM_BUDGET:
                    return tm, tn, tk
    raise ValueError(f"no matmul tiling fits VMEM for {(M, N, K)}")
def _matmul(a, b, *, ta=False, tb=False, bl=None, out_dtype=F32, name, into=None, layer=None, n_layers=None):
    M, K = (a.shape[1], a.shape[0]) if ta else a.shape
    N = b.shape[-2] if tb else b.shape[-1]
    in_acc = jnp.dtype(out_dtype) == jnp.dtype(F32)
    tm, tn, tk = _matmul_tiles(M, N, K, a.dtype.itemsize, b.dtype.itemsize, jnp.dtype(out_dtype).itemsize, in_acc)
    nk = K // tk
    a_spec = pl.BlockSpec((tk, tm), lambda i, j, k: (k, i)) if ta else pl.BlockSpec((tm, tk), lambda i, j, k: (i, k))
    if bl is None:
        b_spec = pl.BlockSpec((tn, tk), lambda i, j, k: (j, k)) if tb else pl.BlockSpec((tk, tn), lambda i, j, k: (k, j))
    elif tb:
        b_spec = pl.BlockSpec((None, tn, tk), lambda i, j, k: (bl, j, k))
    else:
        b_spec = pl.BlockSpec((None, tk, tn), lambda i, j, k: (bl, k, j))
    dn = (((0 if ta else 1,), (1 if tb else 0,)), ((), ()))

    use_scratch = nk > 1 and not in_acc

    def kern(a_ref, b_ref, *rest):
        o_ref = rest[-2] if use_scratch else rest[-1]
        prod = lambda: lax.dot_general(a_ref[...].astype(BF16), b_ref[...].astype(BF16), dn,
                                       preferred_element_type=F32)
        if nk == 1:
            o_ref[...] = prod().astype(o_ref.dtype).reshape(o_ref.shape)
            return
        acc_ref = rest[-1] if use_scratch else o_ref
        k = pl.program_id(2)

        @pl.when(k == 0)
        def _():
            acc_ref[...] = prod().reshape(acc_ref.shape)

        @pl.when(k > 0)
        def _():
            acc_ref[...] += prod().reshape(acc_ref.shape)

        if use_scratch:
            @pl.when(k == nk - 1)
            def _():
                o_ref[...] = acc_ref[...].astype(o_ref.dtype).reshape(o_ref.shape)

    in_specs, args, aliases = [a_spec, b_spec], [a, b], {}
    if layer is None:
        out_shape = jax.ShapeDtypeStruct((M, N), out_dtype)
        out_spec = pl.BlockSpec((tm, tn), lambda i, j, k: (i, j))
    else:
        out_shape = jax.ShapeDtypeStruct((n_layers, M, N), out_dtype)
        out_spec = pl.BlockSpec((1, tm, tn), lambda i, j, k: (layer, i, j))
        if into is not None:
            in_specs.append(pl.BlockSpec(memory_space=pl.ANY))
            args.append(into)
            aliases = {2: 0}
    return pl.pallas_call(
        kern, grid=(M // tm, N // tn, nk), in_specs=in_specs, out_specs=out_spec, out_shape=out_shape,
        scratch_shapes=[pltpu.VMEM((tm, tn), F32)] if use_scratch else [],
        input_output_aliases=aliases, name=name,
        compiler_params=_cparams(("parallel", "parallel", "arbitrary")))(*args)


def _col_specs(off, width, T):
    bw = math.gcd(width, off) if off else width
    return [pl.BlockSpec((T, bw), functools.partial(lambda i, c: (i, c), c=off // bw + p)) for p in range(width // bw)]


def _gather_rows(refs, counts):
    vals, pos = [], 0
    for n in counts:
        parts = [refs[pos + p][...].astype(F32) for p in range(n)]
        pos += n
        vals.append(parts[0] if n == 1 else jnp.concatenate(parts, axis=1))
    return vals, pos


def _rowop(fn, ins, params, outs, *, name):
    S = ins[0][0].shape[0]
    T = min(ROW_TILE, S)
    in_specs, counts, args = [], [], []
    for arr, off, width in ins:
        sp = _col_specs(off, width, T)
        in_specs += sp
        counts.append(len(sp))
        args += [arr] * len(sp)
    in_specs += [pl.BlockSpec(p.shape, lambda i: (0, 0)) for p in params]

    def kern(*refs):
        vals, pos = _gather_rows(refs, counts)
        pv = [refs[pos + p][...] for p in range(len(params))]
        pos += len(params)
        res = fn(*vals, *pv)
        for r, o_ref in zip(res, refs[pos:]):
            o_ref[...] = r.astype(o_ref.dtype)

    return pl.pallas_call(
        kern, grid=(S // T,), in_specs=in_specs,
        out_specs=[pl.BlockSpec((T, w), lambda i: (i, 0)) for w, _ in outs],
        out_shape=[jax.ShapeDtypeStruct((S, w), dt) for w, dt in outs],
        name=name, compiler_params=_cparams(("parallel",)))(*args, *params)


def _rowop_bwd(fn, ins, params, douts, din_dtypes, *, name, add=None):
    add = add or {}
    S = ins[0][0].shape[0]
    T = min(ROW_TILE, S)
    in_specs, counts, args = [], [], []
    for arr, off, width in ins:
        sp = _col_specs(off, width, T)
        in_specs += sp
        counts.append(len(sp))
        args += [arr] * len(sp)
    in_specs += [pl.BlockSpec(p.shape, lambda i: (0, 0)) for p in params]
    in_specs += [pl.BlockSpec((T, d.shape[1]), lambda i: (i, 0)) for d in douts]
    add_keys = sorted(add)
    in_specs += [pl.BlockSpec((T, add[k].shape[1]), lambda i: (i, 0)) for k in add_keys]
    want = [k for k, dt in enumerate(din_dtypes) if dt is not None]

    def kern(*refs):
        vals, pos = _gather_rows(refs, counts)
        pv = [refs[pos + p][...] for p in range(len(params))]
        pos += len(params)
        cts = [refs[pos + p][...].astype(F32) for p in range(len(douts))]
        pos += len(douts)
        adds = {k: refs[pos + p][...].astype(F32) for p, k in enumerate(add_keys)}
        pos += len(add_keys)
        _, vjp = jax.vjp(fn, *vals, *pv)
        grads = vjp(tuple(cts))
        for k in want:
            g = grads[k] + adds[k] if k in adds else grads[k]
            refs[pos][...] = g.astype(refs[pos].dtype)
            pos += 1
        first = pl.program_id(0) == 0
        for p in range(len(params)):
            gp, o_ref = grads[len(ins) + p], refs[pos + p]

            @pl.when(first)
            def _(gp=gp, o_ref=o_ref):
                o_ref[...] = gp

            @pl.when(jnp.logical_not(first))
            def _(gp=gp, o_ref=o_ref):
                o_ref[...] += gp

    out_specs = [pl.BlockSpec((T, ins[k][2]), lambda i: (i, 0)) for k in want]
    out_specs += [pl.BlockSpec(p.shape, lambda i: (0, 0)) for p in params]
    out_shape = [jax.ShapeDtypeStruct((S, ins[k][2]), din_dtypes[k]) for k in want]
    out_shape += [jax.ShapeDtypeStruct(p.shape, F32) for p in params]
    res = pl.pallas_call(
        kern, grid=(S // T,), in_specs=in_specs, out_specs=out_specs, out_shape=out_shape,
        name=name, compiler_params=_cparams(("arbitrary",)))(*args, *params, *douts, *[add[k] for k in add_keys])
    dins = [None] * len(ins)
    for p, k in enumerate(want):
        dins[k] = res[p]
    return dins, list(res[len(want):])


def _rms(x, g):
    return x * lax.rsqrt(jnp.mean(x * x, axis=-1, keepdims=True) + EPS) * g


def _fn_normmod(x, g, sc, sh):
    return (_rms(x, g) * (1.0 + sc) + sh,)


def _fn_lnsilu(c, g, b):
    mu = jnp.mean(c, axis=-1, keepdims=True)
    var = jnp.mean(jnp.square(c - mu), axis=-1, keepdims=True)
    y = (c - mu) * lax.rsqrt(var + EPS) * g + b
    return (y * jax.nn.sigmoid(y),)


def _fn_merge(gl, yc, yh, ys, gb):
    d = yc.shape[1]
    g = jax.nn.sigmoid(gl + gb)
    return (g[:, :d] * yc + g[:, d:2 * d] * yh + g[:, 2 * d:] * ys,)


def _fn_resid(x, y, g):
    return (x + g * y,)


def _fn_scale(y, g):
    return (g * y,)


def _fn_relu2(u):
    return (jnp.square(jnp.maximum(u, 0.0)),)


def _conv_specs(S, T):
    r = T // CONV_HALO
    cur = [pl.BlockSpec((T, CONV_CH), lambda i: (i, 0)), pl.BlockSpec((T, CONV_CH), lambda i: (i, 1))]
    prev = [pl.BlockSpec((CONV_HALO, CONV_CH), lambda i: (jnp.maximum(i * r - 1, 0), 0)),
            pl.BlockSpec((CONV_HALO, CONV_CH), lambda i: (jnp.maximum(i * r - 1, 0), 1))]
    return cur + prev


def _glu_ext(a_ref, g_ref, ah_ref, gh_ref):
    a = a_ref[...]
    sg = jax.nn.sigmoid(g_ref[...])
    uh = jnp.where(pl.program_id(0) > 0, ah_ref[...] * jax.nn.sigmoid(gh_ref[...]), 0.0)
    return a, sg, jnp.concatenate([uh, a * sg], axis=0)


def _shift_up(xe, k, T):
    return xe[:T] if k == 0 else pltpu.roll(xe, shift=xe.shape[0] - k, axis=0)[:T]


def _conv_fwd(proj, w32, b, *, name):
    S = proj.shape[0]
    T = min(ROW_TILE, S)
    lead = CONV_HALO - (CONV_WIDTH - 1)

    def kern(a_ref, g_ref, ah_ref, gh_ref, w_ref, b_ref, o_ref):
        _, _, ue = _glu_ext(a_ref, g_ref, ah_ref, gh_ref)
        acc = jnp.zeros((T, CONV_CH), F32) + b_ref[...]
        for j in range(CONV_WIDTH):
            acc = acc + w_ref[j:j + 1, :] * _shift_up(ue, lead + j, T)
        o_ref[...] = acc

    const = lambda shape: pl.BlockSpec(shape, lambda i: (0, 0))
    return pl.pallas_call(
        kern, grid=(S // T,), in_specs=_conv_specs(S, T) + [const(w32.shape), const(b.shape)],
        out_specs=pl.BlockSpec((T, CONV_CH), lambda i: (i, 0)),
        out_shape=jax.ShapeDtypeStruct((S, CONV_CH), F32), name=name,
        compiler_params=_cparams(("parallel",)))(proj, proj, proj, proj, w32, b)


def _conv_bwd(proj, dc, w32, *, name, comm=None):
    S = proj.shape[0]
    T = min(ROW_TILE, S)
    nt = S // T
    r = T // CONV_HALO
    lead = CONV_HALO - (CONV_WIDTH - 1)
    last_halo = S // CONV_HALO - 1

    def kern(a_ref, g_ref, ah_ref, gh_ref, dc_ref, dcn_ref, w_ref, dag_ref, dw_ref, db_ref):
        i = pl.program_id(0)
        a, sg, ue = _glu_ext(a_ref, g_ref, ah_ref, gh_ref)
        dc_t = dc_ref[...]
        de = jnp.concatenate([dc_t, jnp.where(i < nt - 1, dcn_ref[...], 0.0)], axis=0)

        @pl.when(i == 0)
        def _():
            dw_ref[...] = jnp.zeros_like(dw_ref)
            db_ref[...] = jnp.zeros_like(db_ref)

        du = jnp.zeros((T, CONV_CH), F32)
        for j in range(CONV_WIDTH):
            du = du + w_ref[j:j + 1, :] * _shift_up(de, CONV_WIDTH - 1 - j, T)
            dw_ref[j:j + 1, :] += jnp.sum(dc_t * _shift_up(ue, lead + j, T), axis=0, keepdims=True)
        db_ref[...] += jnp.sum(dc_t, axis=0, keepdims=True)
        dag_ref[:, :CONV_CH] = (du * sg).astype(BF16)
        dag_ref[:, CONV_CH:] = (du * a * sg * (1.0 - sg)).astype(BF16)

    const = lambda shape: pl.BlockSpec(shape, lambda i: (0, 0))
    in_specs = _conv_specs(S, T) + [
        pl.BlockSpec((T, CONV_CH), lambda i: (i, 0)),
        pl.BlockSpec((CONV_HALO, CONV_CH), lambda i: (jnp.minimum((i + 1) * r, last_halo), 0)),
        const(w32.shape)]
    return _hosted_call(
        kern, grid=(nt,), in_specs=in_specs,
        out_specs=[pl.BlockSpec((T, 2 * CONV_CH), lambda i: (i, 0)), const(w32.shape), const((1, CONV_CH))],
        out_shape=[jax.ShapeDtypeStruct((S, 2 * CONV_CH), BF16), jax.ShapeDtypeStruct(w32.shape, F32),
                   jax.ShapeDtypeStruct((1, CONV_CH), F32)],
        scratch_shapes=[], args=[proj, proj, proj, proj, dc, dc, w32], name=name, comm=comm, sem=("arbitrary",))


def _split3(x):
    h = x.astype(BF16)
    r = x - h.astype(F32)
    m = r.astype(BF16)
    return h, m, (r - m.astype(F32)).astype(BF16)


def _xdot_l(m, x):
    return sum(jnp.dot(m, p, preferred_element_type=F32) for p in _split3(x))


def _iota2(shape, dim):
    return lax.broadcasted_iota(jnp.int32, shape, dim)


def _hg_mats():
    n = HG_CHUNK
    r, c = _iota2((n, n), 0), _iota2((n, n), 1)
    low = c <= r
    same = (r // HG_SUB) == (c // HG_SUB)
    up = r <= c
    as_b = lambda m: jnp.where(m, 1.0, 0.0).astype(BF16)
    return dict(low=as_b(low), low_t=as_b(up), blk=as_b(low & same), blk_t=as_b(up & same),
                ones=jnp.ones((n, n), BF16))


@jax.custom_vjp
def _cum(m, m_t, x):
    return _xdot_l(m, x)


def _cum_bwd(res, g):
    m, m_t = res
    return jnp.zeros_like(m), jnp.zeros_like(m_t), _xdot_l(m_t, g)


_cum.defvjp(lambda m, m_t, x: (_xdot_l(m, x), (m, m_t)), _cum_bwd)


def _hg_chunk(q, f, iv, g, st, lbk, ng, mats):
    n, sub = HG_CHUNK, HG_SUB
    kk = lbk * jax.nn.sigmoid(-f)
    lf = jnp.log(1.0 - kk)
    b = _cum(mats["low"], mats["low_t"], lf)
    bs = _cum(mats["blk"], mats["blk_t"], lf)
    bt = _cum(mats["ones"], mats["ones"], lf)
    qh = q * jax.nn.sigmoid(q)
    dot_nt = lambda x, y: lax.dot_general(x.astype(BF16), y.astype(BF16), (((1,), (1,)), ((), ())),
                                          preferred_element_type=F32)
    o = dot_nt(qh * jnp.exp(b), st)
    b0 = b - bs
    qs = qh * jnp.exp(bs)
    col = _iota2((sub, n), 1)
    rows = []
    for blk in range(n // sub):
        lo = blk * sub
        sl = slice(lo, lo + sub)
        acc = o[sl]
        if blk > 0:
            ref = jnp.concatenate([b0[sl]] * (n // sub), axis=0)
            kd = kk * jnp.exp(jnp.minimum(ref - b, 0.0))
            sc = jnp.where(col < lo, dot_nt(qs[sl], kd), 0.0)
            acc = acc + jnp.dot(sc.astype(BF16), iv.astype(BF16), preferred_element_type=F32)
        bq, bk = bs[sl][None, :, :], bs[sl][:, None, :]
        s_i = lax.broadcasted_iota(jnp.int32, (sub, sub, HG_D), 0)
        t_i = lax.broadcasted_iota(jnp.int32, (sub, sub, HG_D), 1)
        keep = s_i <= t_i
        p = jnp.where(keep, qh[sl][None, :, :] * kk[sl][:, None, :] * jnp.exp(jnp.where(keep, bq - bk, 0.0)), 0.0)
        w = jnp.sum(p, axis=-1, keepdims=True)
        acc = acc + jnp.sum(w * iv[sl][:, None, :], axis=0)
        rows.append(acc)
    o = jnp.concatenate(rows, axis=0)
    kd = kk * jnp.exp(bt - b)
    st_new = jnp.exp(bt[0:1]) * st + lax.dot_general(iv.astype(BF16), kd.astype(BF16), (((0,), (0,)), ((), ())),
                                                     preferred_element_type=F32)
    out = _rms(o, ng) * (g * jax.nn.sigmoid(g))
    return out, st_new


def _hg_tile(S):
    return min(512, S)


def _hg_in_specs(rt, rev, nr):
    width = HG_HEADS * HG_D
    base = OFF_HG // width
    row = (lambda r: nr - 1 - r) if rev else (lambda r: r)
    return [pl.BlockSpec((rt, width), functools.partial(lambda r, k: (row(r), base + k), k=k)) for k in range(4)]


def _hg_cols(h):
    return slice(h * HG_D, (h + 1) * HG_D)


def _hgrn_fwd(proj, lbk, ng, *, name):
    S = proj.shape[0]
    rt = _hg_tile(S)
    nr, nc = S // rt, rt // HG_CHUNK

    def kern(q_ref, f_ref, i_ref, g_ref, lbk_ref, ng_ref, o_ref, st_out_ref, st_ref):
        @pl.when(pl.program_id(0) == 0)
        def _():
            st_ref[...] = jnp.zeros_like(st_ref)

        mats = _hg_mats()

        def body(c, carry):
            rows = pl.ds(pl.multiple_of(c * HG_CHUNK, HG_CHUNK), HG_CHUNK)
            for h in range(HG_HEADS):
                cols = _hg_cols(h)
                st = st_ref[h]
                st_out_ref[h, c] = st
                out, st_new = _hg_chunk(q_ref[rows, cols], f_ref[rows, cols], i_ref[rows, cols], g_ref[rows, cols], st,
                                        lbk_ref[:, cols], ng_ref[...], mats)
                o_ref[rows, cols] = out.astype(o_ref.dtype)
                st_ref[h] = st_new
            return carry

        lax.fori_loop(0, nc, body, 0)

    width = HG_HEADS * HG_D
    in_specs = _hg_in_specs(rt, False, nr) + [pl.BlockSpec((1, width), lambda r: (0, 0)),
                                               pl.BlockSpec((1, HG_D), lambda r: (0, 0))]
    return pl.pallas_call(
        kern, grid=(nr,), in_specs=in_specs,
        out_specs=[pl.BlockSpec((rt, width), lambda r: (r, 0)),
                   pl.BlockSpec((HG_HEADS, nc, HG_D, HG_D), lambda r: (0, r, 0, 0))],
        out_shape=[jax.ShapeDtypeStruct((S, width), BF16),
                   jax.ShapeDtypeStruct((HG_HEADS, S // HG_CHUNK, HG_D, HG_D), F32)],
        scratch_shapes=[pltpu.VMEM((HG_HEADS, HG_D, HG_D), F32)], name=name,
        compiler_params=_cparams(("arbitrary",)))(proj, proj, proj, proj, lbk, ng)


def _hgrn_bwd(proj, states, dout, lbk, ng, *, name, comm=None):
    S = proj.shape[0]
    rt = _hg_tile(S)
    nr, nc = S // rt, rt // HG_CHUNK
    width = HG_HEADS * HG_D

    def kern(q_ref, f_ref, i_ref, g_ref, st_in_ref, do_ref, lbk_ref, ng_ref,
             dq_ref, df_ref, di_ref, dg_ref, dlbk_ref, dng_ref, dst_ref):
        @pl.when(pl.program_id(0) == 0)
        def _():
            dst_ref[...] = jnp.zeros_like(dst_ref)
            dlbk_ref[...] = jnp.zeros_like(dlbk_ref)
            dng_ref[...] = jnp.zeros_like(dng_ref)

        mats = _hg_mats()
        fn = functools.partial(_hg_chunk, mats=mats)

        def body(k, carry):
            c = nc - 1 - k
            rows = pl.ds(pl.multiple_of(c * HG_CHUNK, HG_CHUNK), HG_CHUNK)
            for h in range(HG_HEADS):
                cols = _hg_cols(h)
                _, vjp = jax.vjp(fn, q_ref[rows, cols], f_ref[rows, cols], i_ref[rows, cols], g_ref[rows, cols],
                                 st_in_ref[h, c], lbk_ref[:, cols], ng_ref[...])
                dq, df, di, dg, dst, dlbk, dng = vjp((do_ref[rows, cols].astype(F32), dst_ref[h]))
                dq_ref[rows, cols] = dq.astype(BF16)
                df_ref[rows, cols] = df.astype(BF16)
                di_ref[rows, cols] = di.astype(BF16)
                dg_ref[rows, cols] = dg.astype(BF16)
                dst_ref[h] = dst
                dlbk_ref[:, cols] += dlbk
                dng_ref[h] += dng
            return carry

        lax.fori_loop(0, nc, body, 0)

    rev = lambda r: nr - 1 - r
    tile = pl.BlockSpec((rt, width), lambda r: (rev(r), 0))
    in_specs = _hg_in_specs(rt, True, nr) + [
        pl.BlockSpec((HG_HEADS, nc, HG_D, HG_D), lambda r: (0, rev(r), 0, 0)), tile,
        pl.BlockSpec((1, width), lambda r: (0, 0)), pl.BlockSpec((1, HG_D), lambda r: (0, 0))]
    return _hosted_call(
        kern, grid=(nr,), in_specs=in_specs,
        out_specs=[tile, tile, tile, tile, pl.BlockSpec((1, width), lambda r: (0, 0)),
                   pl.BlockSpec((HG_HEADS, 1, HG_D), lambda r: (0, 0, 0))],
        out_shape=[jax.ShapeDtypeStruct((S, width), BF16)] * 4 + [
            jax.ShapeDtypeStruct((1, width), F32), jax.ShapeDtypeStruct((HG_HEADS, 1, HG_D), F32)],
        scratch_shapes=[pltpu.VMEM((HG_HEADS, HG_D, HG_D), F32)],
        args=[proj, proj, proj, proj, states, dout, lbk, ng], name=name, comm=comm, sem=("arbitrary",))


def _scan_rows(x, later):
    n = x.shape[0]
    row = _iota2(x.shape, 0)
    k = 1
    while k < n:
        if later:
            x = x + jnp.where(row < n - k, pltpu.roll(x, n - k, axis=0), 0.0)
        else:
            x = x + jnp.where(row >= k, pltpu.roll(x, k, axis=0), 0.0)
        k *= 2
    return x


def _sb_block(qi, kj, r_run, diag):
    zt = lax.dot_general(kj, qi, (((1,), (1,)), ((), ())), preferred_element_type=F32)
    sp = jnp.maximum(zt, 0.0) + jnp.log(1.0 + jnp.exp(-jnp.abs(zt)))
    lk = -sp
    if diag:
        keep = _iota2(zt.shape, 0) < _iota2(zt.shape, 1)
        lk = jnp.where(keep, lk, 0.0)
    tail = _scan_rows(lk, True)
    a = jnp.exp(zt + tail + r_run)
    if diag:
        a = jnp.where(keep, a, 0.0)
    return sp, a, tail[0:1, :]


def _sb_norm_pair(x, g2, lane_lo):
    sq = x * x
    ms_lo = jnp.sum(jnp.where(lane_lo, sq, 0.0), axis=-1, keepdims=True)
    ms_hi = jnp.sum(jnp.where(lane_lo, 0.0, sq), axis=-1, keepdims=True)
    return x * lax.rsqrt(jnp.where(lane_lo, ms_lo, ms_hi) * (1.0 / SB_DH) + EPS) * g2


def _sb_specs(S, p_rows):
    base = OFF_SB // SB_PAIR
    per = SB_HEADS * SB_DH // SB_PAIR
    cols = [pl.BlockSpec((S, SB_PAIR), functools.partial(lambda p, k: (0, base + per * k + p), k=k)) for k in range(3)]
    return cols + [pl.BlockSpec((1, SB_PAIR), lambda p: (0, 0))] * 2


def _sb_fwd(proj, qg2, kg2, *, name, comm=None):
    S = proj.shape[0]
    nb = S // SB_BLK
    pro = min(512, S)
    scale = SB_DH ** -0.5
    n_pairs = SB_HEADS * SB_DH // SB_PAIR

    def kern(q_ref, k_ref, v_ref, qg_ref, kg_ref, o_ref, rs_ref, qm_ref, kp_ref, vt_ref):
        lane_lo = _iota2((pro, SB_PAIR), 1) < SB_DH

        def prologue(t, carry):
            rows = pl.ds(pl.multiple_of(t * pro, pro), pro)
            qn = _sb_norm_pair(q_ref[rows, :], qg_ref[...], lane_lo) * scale
            kp_ref[rows, :] = _sb_norm_pair(k_ref[rows, :], kg_ref[...], lane_lo).astype(BF16)
            v = v_ref[rows, :]
            for a, mine in enumerate((lane_lo, jnp.logical_not(lane_lo))):
                qm_ref[a, rows, :] = jnp.where(mine, qn, 0.0).astype(BF16)
                vt_ref[a, :, rows] = jnp.where(mine, v, 0.0).T.astype(BF16)
            return carry

        lax.fori_loop(0, S // pro, prologue, 0)
        blk = lambda i: pl.ds(pl.multiple_of(i * SB_BLK, SB_BLK), SB_BLK)

        def qblock(i, carry):
            qis = [qm_ref[a, blk(i), :] for a in range(2)]

            def step(j, diag, st):
                kj = kp_ref[blk(j), :]
                new = []
                for a in range(2):
                    acc, r_run = st[a]
                    _, wgt, lk_sum = _sb_block(qis[a], kj, r_run, diag)
                    av = jnp.dot(vt_ref[a, :, blk(j)], wgt.astype(BF16), preferred_element_type=F32)
                    new.append((acc + av, r_run + lk_sum))
                return tuple(new)

            def note(j, st):
                for a in range(2):
                    rs_ref[a, i, pl.ds(j, 1), :] = st[a][1]
                return jnp.maximum(jnp.max(st[0][1]), jnp.max(st[1][1])) > SB_SKIP

            zero = (jnp.zeros((SB_BLK, SB_PAIR), F32), jnp.zeros((1, SB_BLK), F32))
            st = step(i, True, (zero, zero))
            go = lax.cond(i > 0, lambda: note(i - 1, st).astype(jnp.int32), lambda: jnp.int32(0))

            def body(c):
                jj, _, st = c
                j = i - 1 - jj
                st = step(j, False, st)
                go = lax.cond(j > 0, lambda: note(j - 1, st).astype(jnp.int32), lambda: jnp.int32(0))
                return jj + 1, go, st

            _, _, st = lax.while_loop(lambda c: c[1] > 0, body, (jnp.int32(0), go, st))
            o_ref[blk(i), :] = (st[0][0] + st[1][0]).T.astype(o_ref.dtype)
            return carry

        lax.fori_loop(0, nb, qblock, 0)

    width = SB_HEADS * SB_DH
    return _hosted_call(
        kern, grid=(n_pairs,), in_specs=_sb_specs(S, pro),
        out_specs=[pl.BlockSpec((S, SB_PAIR), lambda p: (0, p)),
                   pl.BlockSpec((2, nb, nb, SB_BLK), lambda p: (p, 0, 0, 0))],
        out_shape=[jax.ShapeDtypeStruct((S, width), BF16), jax.ShapeDtypeStruct((SB_HEADS, nb, nb, SB_BLK), F32)],
        scratch_shapes=[pltpu.VMEM((2, S, SB_PAIR), BF16), pltpu.VMEM((S, SB_PAIR), BF16),
                        pltpu.VMEM((2, SB_PAIR, S), BF16)],
        args=[proj, proj, proj, qg2, kg2], name=name, comm=comm, sem=("parallel",))


def _sb_bwd(proj, qg2, kg2, rs, do, *, name, comm=None):
    S = proj.shape[0]
    nb = S // SB_BLK
    pro = min(512, S)
    scale = SB_DH ** -0.5
    n_pairs = SB_HEADS * SB_DH // SB_PAIR

    def kern(q_ref, k_ref, v_ref, qg_ref, kg_ref, rs_ref, do_ref, dq_ref, dk_ref, dv_ref, dqg_ref, dkg_ref,
             qm_ref, qp_ref, kp_ref, kt_ref, vm_ref, dqn_ref, dkn_ref, dvs_ref):
        lane_lo = _iota2((pro, SB_PAIR), 1) < SB_DH
        fn_q = lambda x, g: _sb_norm_pair(x, g, lane_lo) * scale
        fn_k = lambda x, g: _sb_norm_pair(x, g, lane_lo)

        def prologue(t, carry):
            rows = pl.ds(pl.multiple_of(t * pro, pro), pro)
            qn = fn_q(q_ref[rows, :], qg_ref[...])
            qp_ref[rows, :] = qn.astype(BF16)
            kn = fn_k(k_ref[rows, :], kg_ref[...])
            kp_ref[rows, :] = kn.astype(BF16)
            kt_ref[:, rows] = kn.T.astype(BF16)
            v = v_ref[rows, :]
            for a, mine in enumerate((lane_lo, jnp.logical_not(lane_lo))):
                qm_ref[a, rows, :] = jnp.where(mine, qn, 0.0).astype(BF16)
                vm_ref[a, rows, :] = jnp.where(mine, v, 0.0).astype(BF16)
            return carry

        lax.fori_loop(0, S // pro, prologue, 0)
        dkn_ref[...] = jnp.zeros_like(dkn_ref)
        dvs_ref[...] = jnp.zeros_like(dvs_ref)
        blk = lambda i: pl.ds(pl.multiple_of(i * SB_BLK, SB_BLK), SB_BLK)
        lo_blk = _iota2((SB_PAIR, SB_BLK), 0) < SB_DH

        def qblock(i, carry):
            qis = [qm_ref[a, blk(i), :] for a in range(2)]
            qp = qp_ref[blk(i), :]
            doi = do_ref[blk(i), :]

            def step(j, diag, st):
                kj = kp_ref[blk(j), :]
                new = []
                for a in range(2):
                    dqa, e_run = st[a]
                    r_run = jnp.zeros((1, SB_BLK), F32) if diag else rs_ref[a, i, pl.ds(j, 1), :]
                    sp, wgt, _ = _sb_block(qis[a], kj, r_run, diag)
                    dp = lax.dot_general(vm_ref[a, blk(j), :], doi, (((1,), (1,)), ((), ())),
                                         preferred_element_type=F32)
                    e = dp * wgt
                    head = _scan_rows(e, False)
                    e_left = e_run + (head - e)
                    s_neg = jnp.exp(-sp)
                    dz = e * s_neg - e_left * (1.0 - s_neg)
                    if diag:
                        dz = jnp.where(_iota2(dz.shape, 0) < _iota2(dz.shape, 1), dz, 0.0)
                    dzb = dz.astype(BF16)
                    dkn_ref[a, blk(j), :] += jnp.dot(dzb, qp, preferred_element_type=F32)
                    dvs_ref[a, blk(j), :] += jnp.dot(wgt.astype(BF16), doi, preferred_element_type=F32)
                    dqa = dqa + jnp.dot(kt_ref[:, blk(j)], dzb, preferred_element_type=F32)
                    new.append((dqa, e_run + head[SB_BLK - 1:SB_BLK, :]))
                return tuple(new)

            def live(j):
                jc = jnp.maximum(j, 0)
                top = jnp.maximum(jnp.max(rs_ref[0, i, pl.ds(jc, 1), :]), jnp.max(rs_ref[1, i, pl.ds(jc, 1), :]))
                return jnp.logical_and(j >= 0, top > SB_SKIP).astype(jnp.int32)

            first, _ = lax.while_loop(lambda c: c[1] > 0, lambda c: (c[0] - 1, live(c[0] - 2)), (i, live(i - 1)))
            zero = (jnp.zeros((SB_BLK, SB_PAIR), F32), jnp.zeros((1, SB_BLK), F32))
            st = lax.fori_loop(first, i, lambda j, st: step(j, False, st), (zero, zero))
            st = step(i, True, st)
            dqn_ref[blk(i), :] = jnp.where(lo_blk, st[0][0], st[1][0]).T
            return carry

        lax.fori_loop(0, nb, qblock, 0)
        dqg_ref[...] = jnp.zeros_like(dqg_ref)
        dkg_ref[...] = jnp.zeros_like(dkg_ref)

        def epilogue(t, carry):
            rows = pl.ds(pl.multiple_of(t * pro, pro), pro)
            _, vjp_q = jax.vjp(fn_q, q_ref[rows, :], qg_ref[...])
            dq, dqg = vjp_q(dqn_ref[rows, :])
            _, vjp_k = jax.vjp(fn_k, k_ref[rows, :], kg_ref[...])
            dk, dkg = vjp_k(jnp.where(lane_lo, dkn_ref[0, rows, :], dkn_ref[1, rows, :]))
            dq_ref[rows, :] = dq.astype(BF16)
            dk_ref[rows, :] = dk.astype(BF16)
            dv_ref[rows, :] = jnp.where(lane_lo, dvs_ref[0, rows, :], dvs_ref[1, rows, :]).astype(BF16)
            dqg_ref[0] += dqg
            dkg_ref[0] += dkg
            return carry

        lax.fori_loop(0, S // pro, epilogue, 0)

    width = SB_HEADS * SB_DH
    pair = pl.BlockSpec((S, SB_PAIR), lambda p: (0, p))
    dgain = pl.BlockSpec((1, 1, SB_PAIR), lambda p: (p, 0, 0))
    in_specs = _sb_specs(S, pro) + [pl.BlockSpec((2, nb, nb, SB_BLK), lambda p: (p, 0, 0, 0)), pair]
    return _hosted_call(
        kern, grid=(n_pairs,), in_specs=in_specs, out_specs=[pair, pair, pair, dgain, dgain],
        out_shape=[jax.ShapeDtypeStruct((S, width), BF16)] * 3 + [jax.ShapeDtypeStruct((n_pairs, 1, SB_PAIR), F32)] * 2,
        scratch_shapes=[pltpu.VMEM((2, S, SB_PAIR), BF16), pltpu.VMEM((S, SB_PAIR), BF16), pltpu.VMEM((S, SB_PAIR), BF16),
                        pltpu.VMEM((SB_PAIR, S), BF16), pltpu.VMEM((2, S, SB_PAIR), BF16), pltpu.VMEM((S, SB_PAIR), F32),
                        pltpu.VMEM((2, S, SB_PAIR), F32), pltpu.VMEM((2, S, SB_PAIR), F32)],
        args=[proj, proj, proj, qg2, kg2, rs, do], name=name, comm=comm, sem=("parallel",))


def _loss_head(y, target, *, name):
    S, D = y.shape
    T = min(ROW_TILE, S)

    def kern(y_ref, t_ref, dy_ref, acc_ref):
        err = y_ref[...] - t_ref[...]
        dy_ref[...] = err * (1.0 / D)
        col = jnp.sum(err * err, axis=0, keepdims=True)
        part = sum(col[:, k * 128:(k + 1) * 128] for k in range(D // 128))

        @pl.when(pl.program_id(0) == 0)
        def _():
            acc_ref[...] = part

        @pl.when(pl.program_id(0) > 0)
        def _():
            acc_ref[...] += part

    tile = pl.BlockSpec((T, D), lambda i: (i, 0))
    return pl.pallas_call(
        kern, grid=(S // T,), in_specs=[tile, tile], out_specs=[tile, pl.BlockSpec((1, 128), lambda i: (0, 0))],
        out_shape=[jax.ShapeDtypeStruct((S, D), F32), jax.ShapeDtypeStruct((1, 128), F32)],
        name=name, compiler_params=_cparams(("arbitrary",)))(y, target)


def _adamw_math(w, g, m, v):
    m = ADAM_B1 * m + (1.0 - ADAM_B1) * g
    v = ADAM_B2 * v + (1.0 - ADAM_B2) * jnp.square(g)
    m_hat = m / (1.0 - ADAM_B1 ** ADAM_STEP)
    v_hat = v / (1.0 - ADAM_B2 ** ADAM_STEP)
    return -ADAM_LR * (m_hat / (jnp.sqrt(v_hat) + ADAM_EPS) + ADAM_WD * w), m, v


def _adamw(w, g, m, v, *, name):
    R, C = w.shape
    T = _pick(R, (256, 128, 64, 32, 16, 8))

    def kern(w_ref, g_ref, m_ref, v_ref, d_ref, mo_ref, vo_ref):
        d, mn, vn = _adamw_math(w_ref[...], g_ref[...], m_ref[...], v_ref[...])
        d_ref[...] = d
        mo_ref[...] = mn
        vo_ref[...] = vn

    tile = pl.BlockSpec((T, C), lambda i: (i, 0))
    return pl.pallas_call(
        kern, grid=(R // T,), in_specs=[tile] * 4, out_specs=[tile] * 3,
        out_shape=[jax.ShapeDtypeStruct((R, C), F32)] * 3, name=name,
        compiler_params=_cparams(("parallel",)))(w, g, m, v)


def _adamw_layer(w, g, m, v, layer, prev, *, name):
    L, R, C = w.shape
    T = _pick(R, (256, 128, 64, 32, 16, 8))

    def kern(w_ref, g_ref, m_ref, v_ref, *rest):
        go_ref, d_ref, mo_ref, vo_ref = rest[-4:]
        grad = g_ref[...]
        d, mn, vn = _adamw_math(w_ref[...], grad, m_ref[...], v_ref[...])
        go_ref[...] = grad
        d_ref[...] = d
        mo_ref[...] = mn
        vo_ref[...] = vn

    layer_tile = pl.BlockSpec((None, T, C), lambda i: (layer, i, 0))
    in_specs = [layer_tile, pl.BlockSpec((T, C), lambda i: (i, 0)), layer_tile, layer_tile]
    args, aliases = [w, g, m, v], {}
    if prev is not None:
        in_specs += [pl.BlockSpec(memory_space=pl.ANY)] * 4
        args += list(prev)
        aliases = {4 + k: k for k in range(4)}
    return pl.pallas_call(
        kern, grid=(R // T,), in_specs=in_specs, out_specs=[layer_tile] * 4,
        out_shape=[jax.ShapeDtypeStruct((L, R, C), F32)] * 4, input_output_aliases=aliases, name=name,
        compiler_params=_cparams(("parallel",)))(*args)


def _sum8(g, *, name):
    def kern(g_ref, o_ref):
        acc = g_ref[0]
        for d in range(1, g.shape[0]):
            acc = acc + g_ref[d]
        o_ref[...] = acc

    return pl.pallas_call(kern, out_shape=jax.ShapeDtypeStruct(g.shape[1:], F32), name=name,
                          compiler_params=_cparams())(g)


def _place():
    return lax.axis_index("x"), lax.axis_index("y"), lax.axis_index("c")


def _other_chips(x, y):
    return [(1 - x, y), (x, 1 - y), (1 - x, 1 - y)]


def _remote(src, dst, send_sems, recv_sems, k, to):
    return pltpu.make_async_remote_copy(src_ref=src, dst_ref=dst, send_sem=send_sems.at[k], recv_sem=recv_sems.at[k],
                                        device_id=to, device_id_type=MESH)


def _all_gather_small(v, *, name):
    def body(x_ref, out_ref, send_sems, recv_sems, local_sem):
        x, y, c = _place()
        me = 4 * x + 2 * y + c
        mine = pltpu.make_async_copy(x_ref, out_ref.at[me], local_sem)
        mine.start()
        peers = []
        for f in range(1, 8):
            peers.append((1 - x if f & 4 else x, 1 - y if f & 2 else y, 1 - c if f & 1 else c))
        sends = [_remote(x_ref, out_ref.at[me], send_sems, recv_sems, k, p) for k, p in enumerate(peers)]
        for cp in sends:
            cp.start()
        for k, (px, py, pc) in enumerate(peers):
            _remote(x_ref, out_ref.at[4 * px + 2 * py + pc], send_sems, recv_sems, k, (px, py, pc)).wait_recv()
        for cp in sends:
            cp.wait_send()
        mine.wait()

    return pl.pallas_call(
        body, out_shape=jax.ShapeDtypeStruct((8,) + v.shape, v.dtype),
        in_specs=[pl.BlockSpec(memory_space=pltpu.VMEM)], out_specs=pl.BlockSpec(memory_space=pltpu.VMEM),
        scratch_shapes=[pltpu.SemaphoreType.DMA((7,)), pltpu.SemaphoreType.DMA((7,)), pltpu.SemaphoreType.DMA],
        name=name, compiler_params=_cparams())(v)


def _hosted_call(kern, *, grid, in_specs, out_specs, out_shape, scratch_shapes, args, name, comm=None, sem=None):
    if comm is None:
        res = pl.pallas_call(kern, grid=grid, in_specs=in_specs, out_specs=out_specs, out_shape=out_shape,
                             scratch_shapes=scratch_shapes, name=name, compiler_params=_cparams(sem))(*args)
        return list(res), []
    n_in, n_out, n_scr = len(in_specs), len(out_specs), len(scratch_shapes)
    c_in, c_out = len(comm.inputs), len(comm.out_shapes)
    steps = grid[0]

    def body(*refs):
        ins, ci = refs[:n_in], refs[n_in:n_in + c_in]
        outs = refs[n_in + c_in:n_in + c_in + n_out]
        co = refs[n_in + c_in + n_out:n_in + c_in + n_out + c_out]
        scr = refs[n_in + c_in + n_out + c_out:n_in + c_in + n_out + c_out + n_scr]
        cs = refs[n_in + c_in + n_out + c_out + n_scr:]
        step = pl.program_id(0)

        @pl.when(step == 0)
        def _():
            comm.begin(ci, co, cs)

        kern(*ins, *outs, *scr)

        @pl.when(step == steps // 2)
        def _():
            comm.middle(ci, co, cs)

        @pl.when(step == steps - 1)
        def _():
            comm.end(ci, co, cs)

    hbm = pl.BlockSpec(memory_space=pltpu.HBM)
    res = pl.pallas_call(
        body, grid=grid, in_specs=list(in_specs) + [hbm] * c_in, out_specs=list(out_specs) + [hbm] * c_out,
        out_shape=list(out_shape) + list(comm.out_shapes), scratch_shapes=list(scratch_shapes) + list(comm.scratch),
        input_output_aliases={n_in + i: n_out + o for i, o in comm.aliases.items()},
        name=name, compiler_params=_cparams(("arbitrary",)))(*args, *comm.inputs)
    return list(res[:n_out]), list(res[n_out:])


def _run_comm(comm, *, name):
    return _hosted_call(lambda: None, grid=(1,), in_specs=[], out_specs=[], out_shape=[], scratch_shapes=[], args=[],
                        name=name, comm=comm)[1]


class _Gather:
    def __init__(self, shards, kinds, items):
        used = sorted({w for w, _ in items})
        self.slot = {w: k for k, w in enumerate(used)}
        self.inputs = [shards[w] for w in used]
        self.items, self.kinds = list(items), kinds
        self.shapes = {w: shards[w].shape[1:] for w in used}
        self.out_shapes = [jax.ShapeDtypeStruct((r, 4 * n) if kinds[w] == "col" else (4 * r, n), shards[w].dtype)
                           for w, _ in items for r, n in [self.shapes[w]]]
        n_items = len(items)
        self.scratch = [pltpu.SemaphoreType.DMA((6 * n_items,)), pltpu.SemaphoreType.DMA((6 * n_items,)),
                        pltpu.SemaphoreType.DMA((n_items,))]
        self.aliases = {}

    def _piece(self, ref, w, qq, half):
        r, n = self.shapes[w]
        h = r // 2
        lo, size = (0, r) if half is None else (half * h, h)
        if self.kinds[w] == "col":
            return ref.at[pl.ds(pl.multiple_of(lo, 16), size), pl.ds(pl.multiple_of(qq * n, 128), n)]
        return ref.at[pl.ds(pl.multiple_of(qq * r + lo, 16), size), :]

    def _mine(self, ci, w, l, half):
        h = self.shapes[w][0] // 2
        return ci[self.slot[w]].at[l, pl.ds(pl.multiple_of(half * h, 16), h), :]

    def begin(self, ci, co, cs):
        send_sems, recv_sems, local_sems = cs
        x, y, c = _place()
        q = 2 * x + y
        for k, (w, l) in enumerate(self.items):
            pltpu.make_async_copy(ci[self.slot[w]].at[l], self._piece(co[k], w, q, None), local_sems.at[k]).start()
            for j, (cx, cy) in enumerate(_other_chips(x, y)):
                _remote(self._mine(ci, w, l, c), self._piece(co[k], w, q, c), send_sems, recv_sems, 6 * k + j,
                        (cx, cy, c)).start()

    def middle(self, ci, co, cs):
        send_sems, recv_sems, _ = cs
        x, y, c = _place()
        for k, (w, l) in enumerate(self.items):
            for j, (cx, cy) in enumerate(_other_chips(x, y)):
                win = self._piece(co[k], w, 2 * cx + cy, c)
                _remote(win, win, send_sems, recv_sems, 6 * k + j, (cx, cy, c)).wait_recv()
                _remote(win, win, send_sems, recv_sems, 6 * k + 3 + j, (x, y, 1 - c)).start()

    def end(self, ci, co, cs):
        send_sems, recv_sems, local_sems = cs
        x, y, c = _place()
        q = 2 * x + y
        for k, (w, l) in enumerate(self.items):
            for j, (cx, cy) in enumerate(_other_chips(x, y)):
                win = self._piece(co[k], w, 2 * cx + cy, 1 - c)
                _remote(win, win, send_sems, recv_sems, 6 * k + 3 + j, (x, y, 1 - c)).wait_recv()
        for k, (w, l) in enumerate(self.items):
            for j, (cx, cy) in enumerate(_other_chips(x, y)):
                _remote(self._mine(ci, w, l, c), self._piece(co[k], w, q, c), send_sems, recv_sems, 6 * k + j,
                        (cx, cy, c)).wait_send()
                win = self._piece(co[k], w, 2 * cx + cy, c)
                _remote(win, win, send_sems, recv_sems, 6 * k + 3 + j, (x, y, 1 - c)).wait_send()
            pltpu.make_async_copy(ci[self.slot[w]].at[l], self._piece(co[k], w, q, None), local_sems.at[k]).wait()


def _half_rows(ref, half, h):
    return ref.at[:, pl.ds(pl.multiple_of(half * h, 16), h), :]


class _Copies:
    def __init__(self, inputs, out_shapes, count, pairs, aliases=None):
        self.inputs, self.out_shapes, self.pairs = list(inputs), list(out_shapes), pairs
        self.scratch = [pltpu.SemaphoreType.DMA((count,)), pltpu.SemaphoreType.DMA((count,))]
        self.aliases = aliases or {}

    def _copies(self, ci, co, cs):
        x, y, c = _place()
        return [_remote(src, dst, cs[0], cs[1], k, to) for k, (src, dst, to) in enumerate(self.pairs(ci, co, x, y, c))]

    def begin(self, ci, co, cs):
        for cp in self._copies(ci, co, cs):
            cp.start()

    def middle(self, ci, co, cs):
        pass

    def end(self, ci, co, cs):
        for cp in self._copies(ci, co, cs):
            cp.wait()


def _swap_halves(gs):
    def pairs(ci, co, x, y, c):
        return [(_half_rows(ci[k], 1 - c, g.shape[1] // 2), co[k], (x, y, 1 - c)) for k, g in enumerate(gs)]

    return _Copies(gs, [jax.ShapeDtypeStruct((g.shape[0], g.shape[1] // 2, g.shape[2]), g.dtype) for g in gs],
                   len(gs), pairs)


def _scatter_quarters(ps, kinds):
    part = [((p.shape[1], p.shape[2] // 4) if kind == "col" else (p.shape[1], p.shape[2])) for p, kind in zip(ps, kinds)]

    def pairs(ci, co, x, y, c):
        out = []
        for k, kind in enumerate(kinds):
            n = part[k][1]
            for j, (cx, cy) in enumerate(_other_chips(x, y)):
                qj = 2 * cx + cy
                src = ci[k].at[0, :, pl.ds(pl.multiple_of(qj * n, 128), n)] if kind == "col" else ci[k].at[qj]
                out.append((src, co[k].at[j], (cx, cy, c)))
        return out

    return _Copies(ps, [jax.ShapeDtypeStruct((3,) + pt, p.dtype) for pt, p in zip(part, ps)], 3 * len(ps), pairs)


def _share_halves(gs):
    def rows(co, k, half):
        h = gs[k].shape[0] // 2
        return co[k].at[pl.ds(pl.multiple_of(half * h, 16), h), :]

    def pairs(ci, co, x, y, c):
        return [(rows(co, k, c), rows(co, k, c), (x, y, 1 - c)) for k in range(len(gs))]

    prog = _Copies(gs, [jax.ShapeDtypeStruct(g.shape, g.dtype) for g in gs], len(gs), pairs,
                   aliases={k: k for k in range(len(gs))})

    def end(ci, co, cs):
        x, y, c = _place()
        for k in range(len(gs)):
            cp = _remote(rows(co, k, c), rows(co, k, 1 - c), cs[0], cs[1], k, (x, y, 1 - c))
            cp.wait_send()
            cp.wait_recv()

    prog.end = end
    return prog


def _wide_tile(n):
    return _pick(n, (2048, 1920, 1024, 512, 256, 128))


def _pair_sum(g, land, place, *, name):
    B, R, N = g.shape
    h = R // 2
    tr, tc = _pick(h, (256, 128)), _wide_tile(N)

    def kern(place_ref, g_ref, l_ref, o_ref):
        o_ref[...] = (g_ref[...] + l_ref[...]).astype(o_ref.dtype)

    grid_spec = pltpu.PrefetchScalarGridSpec(
        num_scalar_prefetch=1, grid=(B, h // tr, N // tc),
        in_specs=[pl.BlockSpec((None, tr, tc), lambda b, i, j, p: (b, p[1] * (h // tr) + i, j)),
                  pl.BlockSpec((None, tr, tc), lambda b, i, j, p: (b, i, j))],
        out_specs=pl.BlockSpec((None, tr, tc), lambda b, i, j, p: (b, i, j)))
    return pl.pallas_call(kern, grid_spec=grid_spec, out_shape=jax.ShapeDtypeStruct((B, h, N), BF16), name=name,
                          compiler_params=_cparams(("parallel", "parallel", "parallel")))(place, g, land)


def _quarter_sum(p, land, kind, shard_shape, place, *, name):
    L, r, n = shard_shape
    h = r // 2
    tr, tc = _pick(h, (256, 128)), _wide_tile(n)

    def kern(place_ref, p_ref, a_ref, b_ref, c_ref, o_ref):
        o_ref[...] = ((p_ref[...].astype(F32) + a_ref[...].astype(F32)) + b_ref[...].astype(F32)) + c_ref[...].astype(F32)

    if kind == "col":
        p_spec = pl.BlockSpec((None, tr, tc), lambda l, i, j, pr: (l, i, pr[0] * (n // tc) + j))
    else:
        p_spec = pl.BlockSpec((None, None, tr, tc), lambda l, i, j, pr: (l, pr[0], i, j))
    lands = [pl.BlockSpec((None, None, tr, tc), functools.partial(lambda l, i, j, pr, s: (s, l, i, j), s=s))
             for s in range(3)]
    grid_spec = pltpu.PrefetchScalarGridSpec(
        num_scalar_prefetch=1, grid=(L, h // tr, n // tc), in_specs=[p_spec] + lands,
        out_specs=pl.BlockSpec((None, tr, tc), lambda l, i, j, pr: (l, pr[1] * (h // tr) + i, j)))
    return pl.pallas_call(kern, grid_spec=grid_spec, out_shape=jax.ShapeDtypeStruct((L, r, n), F32), name=name,
                          compiler_params=_cparams(("parallel", "parallel", "parallel")))(place, p, land, land, land)


class _ReduceScatter:
    def __init__(self, grads, kinds, shard_shapes, place, tag):
        self.kinds, self.shapes, self.place, self.tag = kinds, shard_shapes, place, tag
        self.g3 = [g[None] if kind == "col" else g.reshape(4, g.shape[0] // 4, g.shape[1]) for g, kind in zip(grads, kinds)]

    def swap(self):
        return _swap_halves(self.g3)

    def pair_sums(self, lands):
        self.ps = [_pair_sum(g, land, self.place, name=f"rs_pair_sum_{self.tag}_{k}")
                   for k, (g, land) in enumerate(zip(self.g3, lands))]

    def scatter(self):
        return _scatter_quarters(self.ps, self.kinds)

    def quarter_sums(self, parts):
        self.halves = []
        for k, (p, part) in enumerate(zip(self.ps, parts)):
            p4 = p if self.kinds[k] == "col" else p[None]
            out = _quarter_sum(p4, part[:, None], self.kinds[k], (1,) + tuple(self.shapes[k]), self.place,
                               name=f"rs_quarter_sum_{self.tag}_{k}")
            self.halves.append(out[0])

    def share(self):
        return _share_halves(self.halves)

    def run(self):
        self.pair_sums(_run_comm(self.swap(), name=f"rs_swap_{self.tag}"))
        self.quarter_sums(_run_comm(self.scatter(), name=f"rs_scatter_{self.tag}"))
        return _run_comm(self.share(), name=f"rs_share_{self.tag}")


_WEIGHTS = ["mod_w", "mod_b", "norm1_g", "w_in", "gate_b", "conv_w", "conv_b", "conv_ln_g", "conv_ln_b", "w_conv_proj",
            "hgrn_lb", "hgrn_norm_g", "w_hgrn_proj", "sb_qn_g", "sb_kn_g", "w_sb_proj", "w_out", "norm2_g", "mlp_w1",
            "mlp_w2"]
_BIG = [("w_in", "col"), ("w_conv_proj", "col"), ("w_hgrn_proj", "col"), ("w_sb_proj", "col"), ("w_out", "row"),
        ("mlp_w1", "col"), ("mlp_w2", "row")]
_REPLICATED = ["mod_b", "norm1_g", "gate_b", "conv_b", "conv_ln_g", "conv_ln_b", "hgrn_lb", "hgrn_norm_g", "sb_qn_g",
               "sb_kn_g", "norm2_g"]
LANES = 128


class _Pack:
    def __init__(self, items):
        self.shapes = {n: a.shape for n, a in items}
        self.offsets, pos = {}, 0
        for n, a in items:
            self.offsets[n] = pos
            pos += math.prod(a.shape)
        self.rows = -(-pos // (8 * LANES)) * 8
        flat = jnp.concatenate([a.reshape(-1).astype(F32) for _, a in items])
        self.array = jnp.pad(flat, (0, self.rows * LANES - pos)).reshape(self.rows, LANES)

    def get(self, packed, name):
        lead = packed.shape[:-2]
        flat = packed.reshape(lead + (self.rows * LANES,))
        n = math.prod(self.shapes[name])
        return lax.slice_in_dim(flat, self.offsets[name], self.offsets[name] + n, axis=len(lead)).reshape(
            lead + self.shapes[name])


def _lower_bounds(hgrn_lb):
    p = jax.nn.softmax(hgrn_lb.astype(F32), axis=0)
    return jnp.cumsum(p, axis=0) - p[0:1]


def _layer_fwd(x, w, p, l, comm=None):
    S, D = x.shape
    r = {"x": x}
    (r["h"],) = _rowop(_fn_normmod, [(x, 0, D)], [p["n1g"], p["sc1"], p["sh1"]], [(D, BF16)], name=f"normmod1_fwd_{l}")
    proj = r["proj"] = _matmul(r["h"], w["w_in", l], name=f"w_in_fwd_{l}")
    r["cpre"] = _conv_fwd(proj, p["w32"], p["conv_b"], name=f"conv_fwd_{l}")
    (r["cact"],) = _rowop(_fn_lnsilu, [(r["cpre"], 0, CONV_CH)], [p["lng"], p["lnb"]], [(CONV_CH, BF16)],
                          name=f"conv_ln_fwd_{l}")
    r["hg"], r["states"] = _hgrn_fwd(proj, p["lbk"], p["ng"], name=f"hgrn_fwd_{l}")
    (r["sb"], r["rs"]), got = _sb_fwd(proj, p["qg"], p["kg"], name=f"sb_fwd_{l}", comm=comm)
    if comm is not None:
        w.update({(_BIG[k][0], layer): arr for (k, layer), arr in zip(comm.items, got)})
    r["y_c"] = _matmul(r["cact"], w["w_conv_proj", l], name=f"w_conv_proj_fwd_{l}")
    r["y_h"] = _matmul(r["hg"], w["w_hgrn_proj", l], name=f"w_hgrn_proj_fwd_{l}")
    r["y_s"] = _matmul(r["sb"], w["w_sb_proj", l], name=f"w_sb_proj_fwd_{l}")
    (r["merged"],) = _rowop(_fn_merge, [(proj, OFF_GL, 3 * D), (r["y_c"], 0, D), (r["y_h"], 0, D), (r["y_s"], 0, D)],
                            [p["gate_b"]], [(D, BF16)], name=f"merge_fwd_{l}")
    r["a_out"] = _matmul(r["merged"], w["w_out", l], name=f"w_out_fwd_{l}")
    (r["x1"],) = _rowop(_fn_resid, [(x, 0, D), (r["a_out"], 0, D)], [p["g1"]], [(D, F32)], name=f"resid1_fwd_{l}")
    (r["h2"],) = _rowop(_fn_normmod, [(r["x1"], 0, D)], [p["n2g"], p["sc2"], p["sh2"]], [(D, BF16)],
                        name=f"normmod2_fwd_{l}")
    r["u"] = _matmul(r["h2"], w["mlp_w1", l], name=f"mlp_w1_fwd_{l}")
    dff = r["u"].shape[1]
    (r["act"],) = _rowop(_fn_relu2, [(r["u"], 0, dff)], [], [(dff, BF16)], name=f"relu2_fwd_{l}")
    r["m_out"] = _matmul(r["act"], w["mlp_w2", l], name=f"mlp_w2_fwd_{l}")
    (x2,) = _rowop(_fn_resid, [(r["x1"], 0, D), (r["m_out"], 0, D)], [p["g2"]], [(D, F32)], name=f"resid2_fwd_{l}")
    return x2, r


def _layer_bwd(dx2, r, w, p, l, grads, carry=None):
    S, D = dx2.shape
    dff = r["u"].shape[1]
    small = {}

    def dweight(name, a, dy):
        grads[name, l] = _matmul(a, dy, ta=True, name=f"{name}_dw_{l}")

    stage = (lambda k, got: carry(k, got)) if carry is not None else (lambda k, got: None)

    (dm_out,), (dg2,) = _rowop_bwd(_fn_scale, [(r["m_out"], 0, D)], [p["g2"]], [dx2], [BF16], name=f"resid2_bwd_{l}")
    dact = _matmul(dm_out, w["mlp_w2", l], tb=True, name=f"mlp_w2_dx_{l}")
    dweight("mlp_w2", r["act"], dm_out)
    (du,), _ = _rowop_bwd(_fn_relu2, [(r["u"], 0, dff)], [], [dact], [BF16], name=f"relu2_bwd_{l}")
    dh2 = _matmul(du, w["mlp_w1", l], tb=True, name=f"mlp_w1_dx_{l}")
    dweight("mlp_w1", r["h2"], du)
    (dx1,), (small["norm2_g"], dsc2, dsh2) = _rowop_bwd(
        _fn_normmod, [(r["x1"], 0, D)], [p["n2g"], p["sc2"], p["sh2"]], [dh2], [F32], add={0: dx2},
        name=f"normmod2_bwd_{l}")
    (da_out,), (dg1,) = _rowop_bwd(_fn_scale, [(r["a_out"], 0, D)], [p["g1"]], [dx1], [BF16], name=f"resid1_bwd_{l}")
    dmerged = _matmul(da_out, w["w_out", l], tb=True, name=f"w_out_dx_{l}")
    dweight("w_out", r["merged"], da_out)
    (dgl, dy_c, dy_h, dy_s), (small["gate_b"],) = _rowop_bwd(
        _fn_merge, [(r["proj"], OFF_GL, 3 * D), (r["y_c"], 0, D), (r["y_h"], 0, D), (r["y_s"], 0, D)], [p["gate_b"]],
        [dmerged], [BF16] * 4, name=f"merge_bwd_{l}")
    dcact = _matmul(dy_c, w["w_conv_proj", l], tb=True, name=f"w_conv_proj_dx_{l}")
    dweight("w_conv_proj", r["cact"], dy_c)
    (dcpre,), (small["conv_ln_g"], small["conv_ln_b"]) = _rowop_bwd(
        _fn_lnsilu, [(r["cpre"], 0, CONV_CH)], [p["lng"], p["lnb"]], [dcact], [F32], name=f"conv_ln_bwd_{l}")
    (d_conv, dw32, small["conv_b"]), got = _conv_bwd(r["proj"], dcpre, p["w32"], name=f"conv_bwd_{l}",
                                                      comm=stage(0, None))
    small["conv_w"] = dw32[:CONV_WIDTH]
    dhg = _matmul(dy_h, w["w_hgrn_proj", l], tb=True, out_dtype=BF16, name=f"w_hgrn_proj_dx_{l}")
    dweight("w_hgrn_proj", r["hg"], dy_h)
    (dq, df, di, dg, dlbk, dng), got = _hgrn_bwd(r["proj"], r["states"], dhg, p["lbk"], p["ng"], name=f"hgrn_bwd_{l}",
                                                 comm=stage(1, got))
    small["lower"] = -dlbk
    small["hgrn_norm_g"] = jnp.sum(dng, axis=0)
    dsb = _matmul(dy_s, w["w_sb_proj", l], tb=True, out_dtype=BF16, name=f"w_sb_proj_dx_{l}")
    dweight("w_sb_proj", r["sb"], dy_s)
    (dsq, dsk, dsv, dqg, dkg), got = _sb_bwd(r["proj"], p["qg"], p["kg"], r["rs"], dsb, name=f"sb_bwd_{l}",
                                             comm=stage(2, got))
    stage(3, got)
    fold = lambda t: jnp.sum(t.reshape(-1, SB_DH), axis=0, keepdims=True)
    small["sb_qn_g"], small["sb_kn_g"] = fold(dqg), fold(dkg)
    dproj = jnp.concatenate([d_conv, dq, df, di, dg, dsq, dsk, dsv, dgl], axis=1)
    dh = _matmul(dproj, w["w_in", l], tb=True, name=f"w_in_dx_{l}")
    dweight("w_in", r["h"], dproj)
    (dx,), (small["norm1_g"], dsc1, dsh1) = _rowop_bwd(
        _fn_normmod, [(r["x"], 0, D)], [p["n1g"], p["sc1"], p["sh1"]], [dh], [F32], add={0: dx1},
        name=f"normmod1_bwd_{l}")
    small["mod"] = jnp.concatenate([dsh1, dsc1, dg1, dsh2, dsc2, dg2], axis=1)
    return dx, small


def kernel(x, c, mod_w, mod_b, norm1_g, w_in, gate_b, conv_w, conv_b, conv_ln_g, conv_ln_b, w_conv_proj, hgrn_lb, hgrn_norm_g, w_hgrn_proj, sb_qn_g, sb_kn_g, w_sb_proj, w_out, norm2_g, mlp_w1, mlp_w2, loss_target, m_mod_w, m_mod_b, m_norm1_g, m_w_in, m_gate_b, m_conv_w, m_conv_b, m_conv_ln_g, m_conv_ln_b, m_w_conv_proj, m_hgrn_lb, m_hgrn_norm_g, m_w_hgrn_proj, m_sb_qn_g, m_sb_kn_g, m_w_sb_proj, m_w_out, m_norm2_g, m_mlp_w1, m_mlp_w2, v_mod_w, v_mod_b, v_norm1_g, v_w_in, v_gate_b, v_conv_w, v_conv_b, v_conv_ln_g, v_conv_ln_b, v_w_conv_proj, v_hgrn_lb, v_hgrn_norm_g, v_w_hgrn_proj, v_sb_qn_g, v_sb_kn_g, v_w_sb_proj, v_w_out, v_norm2_g, v_mlp_w1, v_mlp_w2):
    given = dict(locals())
    wts = {n: given[n] for n in _WEIGHTS}
    mom = {n: given["m_" + n] for n in _WEIGHTS}
    var = {n: given["v_" + n] for n in _WEIGHTS}
    n_layers, D = norm1_g.shape
    xi, yi, ci = _place()
    q = 2 * xi + yi
    me = 4 * xi + 2 * yi + ci
    place = jnp.stack([q, ci]).astype(jnp.int32)
    n_mod = mod_w.shape[2]
    cw = conv_w.shape[2]

    pk1 = _Pack([("c", c), ("conv_w", conv_w)])
    got1 = _all_gather_small(pk1.array, name="gather_cond")
    c_act = jax.nn.silu(pk1.get(got1, "c")[:, 0, :])
    conv_full = jnp.concatenate([pk1.get(got1, "conv_w")[2 * k] for k in range(4)], axis=-1)

    mod_cols = []
    for l in range(n_layers):
        mb = lax.dynamic_slice_in_dim(mod_b[l], q * n_mod, n_mod)
        mod_cols.append(_matmul(c_act, mod_w, bl=l, name=f"mod_fwd_{l}") + mb[None, :])
    got2 = _all_gather_small(jnp.concatenate(mod_cols, axis=0), name="gather_mod")
    mods = []
    for l in range(n_layers):
        row = lax.dynamic_index_in_dim(got2[0::2], l * 8 + me, axis=1, keepdims=False)
        mods.append(jnp.split(row.reshape(1, 4 * n_mod), 6, axis=1))

    lower, lower_vjp = jax.vjp(_lower_bounds, hgrn_lb)

    shards = [wts[n].astype(BF16) for n, _ in _BIG]
    kinds = [k for _, k in _BIG]
    index = {n: k for k, (n, _) in enumerate(_BIG)}
    first = ["w_in", "w_conv_proj", "w_hgrn_proj", "w_sb_proj"]
    later = ["w_out", "mlp_w1", "mlp_w2"]
    plan = [[(index[n], 0) for n in first]]
    for l in range(n_layers):
        nxt = [(index[n], l + 1) for n in first] if l + 1 < n_layers else []
        plan.append([(index[n], l) for n in later] + nxt)
    gathers = [_Gather(shards, kinds, items) for items in plan]
    w = {(_BIG[k][0], layer): arr
         for (k, layer), arr in zip(plan[0], _run_comm(gathers[0], name="gather_first_weights"))}

    def layer_params(l):
        sh1, sc1, g1, sh2, sc2, g2 = mods[l]
        return dict(sh1=sh1, sc1=sc1, g1=g1, sh2=sh2, sc2=sc2, g2=g2, n1g=norm1_g[l][None], n2g=norm2_g[l][None],
                    gate_b=gate_b[l][None], conv_b=conv_b[l][None], lng=conv_ln_g[l][None], lnb=conv_ln_b[l][None],
                    w32=jnp.pad(conv_full[l], ((0, CONV_HALO - CONV_WIDTH), (0, 0))), lbk=(1.0 - lower[l])[None],
                    ng=hgrn_norm_g[l][None], qg=jnp.tile(sb_qn_g[l][None], (1, SB_PAIR // SB_DH)),
                    kg=jnp.tile(sb_kn_g[l][None], (1, SB_PAIR // SB_DH)))

    params = [layer_params(l) for l in range(n_layers)]
    act, saved = x[0], []
    for l in range(n_layers):
        act, r = _layer_fwd(act, w, params[l], l, comm=gathers[l + 1])
        saved.append(r)
    dact, loss_lanes = _loss_head(act, loss_target[0], name="loss_head")

    grads, smalls, reduced = {}, [None] * n_layers, {}

    def reduce_scatter(items, tag):
        return _ReduceScatter([grads[_BIG[k][0], layer] for k, layer in items], [kinds[k] for k, _ in items],
                              [shards[k].shape[1:] for k, _ in items], place, tag)

    def carried(l):
        items = [(k, l + 1) for k in range(len(_BIG))] + [(index[n], l) for n in ("mlp_w2", "mlp_w1")]
        box = {}

        def carry(stage, got):
            if stage == 0:
                box["rs"] = reduce_scatter(items, f"l{l}")
                return box["rs"].swap()
            if stage == 1:
                box["rs"].pair_sums(got)
                return box["rs"].scatter()
            if stage == 2:
                box["rs"].quarter_sums(got)
                return box["rs"].share()
            reduced.update(zip(items, got))

        return carry

    for l in reversed(range(n_layers)):
        dact, smalls[l] = _layer_bwd(dact, saved[l], w, params[l], l, grads, carried(l) if l + 1 < n_layers else None)
    grad_x = dact[None]
    rest = [(k, l) for l in range(n_layers) for k in range(len(_BIG)) if (k, l) not in reduced]
    reduced.update(zip(rest, reduce_scatter(rest, "last").run()))

    stack = lambda k: jnp.stack([smalls[l][k] for l in range(n_layers)])
    (d_hgrn_lb,) = lower_vjp(stack("lower")[:, 0, :])
    items = [("loss", loss_lanes), ("mod", stack("mod")), ("hgrn_lb", d_hgrn_lb), ("conv_w", stack("conv_w"))]
    items += [(k, stack(k)) for k in ("norm1_g", "gate_b", "conv_b", "conv_ln_g", "conv_ln_b", "hgrn_norm_g", "sb_qn_g",
                                      "sb_kn_g", "norm2_g")]
    pk3 = _Pack(items)
    got3 = _all_gather_small(pk3.array, name="gather_small_grads")
    tot3 = _sum8(got3, name="sum_small_grads")
    loss = (0.5 / D) * jnp.sum(pk3.get(tot3, "loss"))
    g = {k: pk3.get(tot3, k).reshape(wts[k].shape) for k in _REPLICATED if k != "mod_b"}
    g["mod_b"] = pk3.get(tot3, "mod")[:, 0, :]
    g["conv_w"] = lax.dynamic_slice_in_dim(pk3.get(tot3, "conv_w"), q * cw, cw, axis=2)
    dmod_all = pk3.get(got3, "mod")[:, :, 0, :]
    g_mod_w = None
    for l in range(n_layers):
        cols = lax.dynamic_slice_in_dim(dmod_all[:, l, :], q * n_mod, n_mod, axis=1)
        g_mod_w = _matmul(c_act, cols, ta=True, layer=l, n_layers=n_layers, into=g_mod_w, name=f"mod_dw_{l}")
    g["mod_w"] = g_mod_w

    delta, new_m, new_v = {}, {}, {}
    for n, _ in _BIG:
        outs = None
        for l in reversed(range(n_layers)):
            outs = _adamw_layer(wts[n], reduced[index[n], l], mom[n], var[n], l, outs, name=f"adamw_{n}_{l}")
        g[n], delta[n], new_m[n], new_v[n] = outs
    two_d = lambda t: t.reshape(-1, t.shape[-1])
    outs = _adamw(two_d(mod_w), two_d(g["mod_w"]), two_d(m_mod_w), two_d(v_mod_w), name="adamw_mod_w")
    delta["mod_w"], new_m["mod_w"], new_v["mod_w"] = (t.reshape(mod_w.shape) for t in outs)
    rest = _REPLICATED + ["conv_w"]
    packs = [_Pack([(n, src[n]) for n in rest]) for src in (wts, g, mom, var)]
    outs = _adamw(*[pk.array for pk in packs], name="adamw_small")
    for n in rest:
        delta[n], new_m[n], new_v[n] = (packs[0].get(t, n) for t in outs)

    return (loss, grad_x, *[g[n] for n in _WEIGHTS], *[delta[n] for n in _WEIGHTS], *[new_m[n] for n in _WEIGHTS],
            *[new_v[n] for n in _WEIGHTS])
```

```python
import functools
import math

import jax
import jax.numpy as jnp
from jax import lax
from jax.experimental import pallas as pl
from jax.experimental.pallas import tpu as pltpu

F32 = jnp.float32
BF16 = jnp.bfloat16
MESH = pl.DeviceIdType.MESH

EPS = 1e-6
CONV_CH = 512
CONV_WIDTH = 31
CONV_HALO = 32
HG_HEADS = 4
HG_D = 128
HG_CHUNK = 64
HG_SUB = 16
SB_HEADS = 8
SB_DH = 64
SB_BLK = 128
SB_PAIR = 128
SB_SKIP = -104.0
OFF_CONV, OFF_HG, OFF_SB, OFF_GL = 0, 1024, 3072, 4608
ADAM_LR, ADAM_B1, ADAM_B2, ADAM_EPS, ADAM_WD, ADAM_STEP = 0.001, 0.9, 0.999, 1e-08, 0.01, 10
VMEM_LIMIT_BYTES = 56 * 1024 * 1024
ROW_TILE = 256


def _cparams(sem=None, **kw):
    return pltpu.CompilerParams(dimension_semantics=sem, vmem_limit_bytes=VMEM_LIMIT_BYTES, **kw)


def _pick(n, cands):
    for c in cands:
        if n % c == 0:
            return c
    return n


MATMUL_VMEM_BUDGET = 40 * 1024 * 1024


def _tile_options(n, cap):
    opts = [t for t in range(cap - cap % 128, 0, -128) if n % t == 0]
    return opts or [n]


def _matmul_tiles(M, N, K, size_a, size_b, size_o, in_acc):
    for tm in _tile_options(M, 1024):
        for tk in _tile_options(K, 2048):
            for tn in _tile_options(N, 1280):
                need = 2 * (tm * tk * size_a + tk * tn * size_b + tm * tn * size_o)
                if K > tk and not in_acc:
                    need += tm * tn * 4
                if need <= MATMUL_VMEM_BUDGET:
                    return tm, tn, tk
    raise ValueError(f"no matmul tiling fits VMEM for {(M, N, K)}")
def _matmul(a, b, *, ta=False, tb=False, bl=None, out_dtype=F32, name, into=None, layer=None, n_layers=None,
            post=None, extras=(), out_dtypes=None):
    M, K = (a.shape[1], a.shape[0]) if ta else a.shape
    N = b.shape[-2] if tb else b.shape[-1]
    if post is not None:
        return _matmul_post(a, b, M, N, K, ta, tb, post, extras, out_dtypes, name)
    in_acc = jnp.dtype(out_dtype) == jnp.dtype(F32)
    tm, tn, tk = _matmul_tiles(M, N, K, a.dtype.itemsize, b.dtype.itemsize, jnp.dtype(out_dtype).itemsize, in_acc)
    nk = K // tk
    a_spec = pl.BlockSpec((tk, tm), lambda i, j, k: (k, i)) if ta else pl.BlockSpec((tm, tk), lambda i, j, k: (i, k))
    if bl is None:
        b_spec = pl.BlockSpec((tn, tk), lambda i, j, k: (j, k)) if tb else pl.BlockSpec((tk, tn), lambda i, j, k: (k, j))
    elif tb:
        b_spec = pl.BlockSpec((None, tn, tk), lambda i, j, k: (bl, j, k))
    else:
        b_spec = pl.BlockSpec((None, tk, tn), lambda i, j, k: (bl, k, j))
    dn = (((0 if ta else 1,), (1 if tb else 0,)), ((), ()))

    use_scratch = nk > 1 and not in_acc

    def kern(a_ref, b_ref, *rest):
        o_ref = rest[-2] if use_scratch else rest[-1]
        prod = lambda: lax.dot_general(a_ref[...].astype(BF16), b_ref[...].astype(BF16), dn,
                                       preferred_element_type=F32)
        if nk == 1:
            o_ref[...] = prod().astype(o_ref.dtype).reshape(o_ref.shape)
            return
        acc_ref = rest[-1] if use_scratch else o_ref
        k = pl.program_id(2)

        @pl.when(k == 0)
        def _():
            acc_ref[...] = prod().reshape(acc_ref.shape)

        @pl.when(k > 0)
        def _():
            acc_ref[...] += prod().reshape(acc_ref.shape)

        if use_scratch:
            @pl.when(k == nk - 1)
            def _():
                o_ref[...] = acc_ref[...].astype(o_ref.dtype).reshape(o_ref.shape)

    in_specs, args, aliases = [a_spec, b_spec], [a, b], {}
    if layer is None:
        out_shape = jax.ShapeDtypeStruct((M, N), out_dtype)
        out_spec = pl.BlockSpec((tm, tn), lambda i, j, k: (i, j))
    else:
        out_shape = jax.ShapeDtypeStruct((n_layers, M, N), out_dtype)
        out_spec = pl.BlockSpec((1, tm, tn), lambda i, j, k: (layer, i, j))
        if into is not None:
            in_specs.append(pl.BlockSpec(memory_space=pl.ANY))
            args.append(into)
            aliases = {2: 0}
    return pl.pallas_call(
        kern, grid=(M // tm, N // tn, nk), in_specs=in_specs, out_specs=out_spec, out_shape=out_shape,
        scratch_shapes=[pltpu.VMEM((tm, tn), F32)] if use_scratch else [],
        input_output_aliases=aliases, name=name,
        compiler_params=_cparams(("parallel", "parallel", "arbitrary")))(*args)


def _matmul_post(a, b, M, N, K, ta, tb, post, extras, out_dtypes, name):
    per_elem = sum(e.dtype.itemsize for e in extras) + sum(jnp.dtype(d).itemsize for d in out_dtypes)
    tm, tn, tk = _matmul_tiles(M, N, K, a.dtype.itemsize, b.dtype.itemsize, per_elem, True)
    assert tk == K, "the epilogue form keeps the contraction in one block"
    a_spec = pl.BlockSpec((K, tm), lambda i, j: (0, i)) if ta else pl.BlockSpec((tm, K), lambda i, j: (i, 0))
    b_spec = pl.BlockSpec((tn, K), lambda i, j: (j, 0)) if tb else pl.BlockSpec((K, tn), lambda i, j: (0, j))
    tile = pl.BlockSpec((tm, tn), lambda i, j: (i, j))
    dn = (((0 if ta else 1,), (1 if tb else 0,)), ((), ()))
    n_ex = len(extras)

    def kern(a_ref, b_ref, *rest):
        prod = lax.dot_general(a_ref[...].astype(BF16), b_ref[...].astype(BF16), dn, preferred_element_type=F32)
        res = post(prod, *[r[...].astype(F32) for r in rest[:n_ex]])
        for val, o_ref in zip(res, rest[n_ex:]):
            o_ref[...] = val.astype(o_ref.dtype)

    return pl.pallas_call(
        kern, grid=(M // tm, N // tn), in_specs=[a_spec, b_spec] + [tile] * n_ex, out_specs=[tile] * len(out_dtypes),
        out_shape=[jax.ShapeDtypeStruct((M, N), d) for d in out_dtypes], name=name,
        compiler_params=_cparams(("parallel", "parallel")))(a, b, *extras)


def _col_specs(off, width, T):
    bw = math.gcd(width, off) if off else width
    return [pl.BlockSpec((T, bw), functools.partial(lambda i, c: (i, c), c=off // bw + p)) for p in range(width // bw)]


def _gather_rows(refs, counts):
    vals, pos = [], 0
    for n in counts:
        parts = [refs[pos + p][...].astype(F32) for p in range(n)]
        pos += n
        vals.append(parts[0] if n == 1 else jnp.concatenate(parts, axis=1))
    return vals, pos


def _rowop(fn, ins, params, outs, *, name):
    S = ins[0][0].shape[0]
    T = min(ROW_TILE, S)
    in_specs, counts, args = [], [], []
    for arr, off, width in ins:
        sp = _col_specs(off, width, T)
        in_specs += sp
        counts.append(len(sp))
        args += [arr] * len(sp)
    in_specs += [pl.BlockSpec(p.shape, lambda i: (0, 0)) for p in params]

    def kern(*refs):
        vals, pos = _gather_rows(refs, counts)
        pv = [refs[pos + p][...] for p in range(len(params))]
        pos += len(params)
        res = fn(*vals, *pv)
        for r, o_ref in zip(res, refs[pos:]):
            o_ref[...] = r.astype(o_ref.dtype)

    return pl.pallas_call(
        kern, grid=(S // T,), in_specs=in_specs,
        out_specs=[pl.BlockSpec((T, w), lambda i: (i, 0)) for w, _ in outs],
        out_shape=[jax.ShapeDtypeStruct((S, w), dt) for w, dt in outs],
        name=name, compiler_params=_cparams(("parallel",)))(*args, *params)


def _rowop_bwd(fn, ins, params, douts, din_dtypes, *, name, add=None):
    add = add or {}
    S = ins[0][0].shape[0]
    T = min(ROW_TILE, S)
    in_specs, counts, args = [], [], []
    for arr, off, width in ins:
        sp = _col_specs(off, width, T)
        in_specs += sp
        counts.append(len(sp))
        args += [arr] * len(sp)
    in_specs += [pl.BlockSpec(p.shape, lambda i: (0, 0)) for p in params]
    in_specs += [pl.BlockSpec((T, d.shape[1]), lambda i: (i, 0)) for d in douts]
    add_keys = sorted(add)
    in_specs += [pl.BlockSpec((T, add[k].shape[1]), lambda i: (i, 0)) for k in add_keys]
    want = [k for k, dt in enumerate(din_dtypes) if dt is not None]

    def kern(*refs):
        vals, pos = _gather_rows(refs, counts)
        pv = [refs[pos + p][...] for p in range(len(params))]
        pos += len(params)
        cts = [refs[pos + p][...].astype(F32) for p in range(len(douts))]
        pos += len(douts)
        adds = {k: refs[pos + p][...].astype(F32) for p, k in enumerate(add_keys)}
        pos += len(add_keys)
        _, vjp = jax.vjp(fn, *vals, *pv)
        grads = vjp(tuple(cts))
        for k in want:
            g = grads[k] + adds[k] if k in adds else grads[k]
            refs[pos][...] = g.astype(refs[pos].dtype)
            pos += 1
        first = pl.program_id(0) == 0
        for p in range(len(params)):
            gp, o_ref = grads[len(ins) + p], refs[pos + p]

            @pl.when(first)
            def _(gp=gp, o_ref=o_ref):
                o_ref[...] = gp

            @pl.when(jnp.logical_not(first))
            def _(gp=gp, o_ref=o_ref):
                o_ref[...] += gp

    out_specs = [pl.BlockSpec((T, ins[k][2]), lambda i: (i, 0)) for k in want]
    out_specs += [pl.BlockSpec(p.shape, lambda i: (0, 0)) for p in params]
    out_shape = [jax.ShapeDtypeStruct((S, ins[k][2]), din_dtypes[k]) for k in want]
    out_shape += [jax.ShapeDtypeStruct(p.shape, F32) for p in params]
    res = pl.pallas_call(
        kern, grid=(S // T,), in_specs=in_specs, out_specs=out_specs, out_shape=out_shape,
        name=name, compiler_params=_cparams(("arbitrary",)))(*args, *params, *douts, *[add[k] for k in add_keys])
    dins = [None] * len(ins)
    for p, k in enumerate(want):
        dins[k] = res[p]
    return dins, list(res[len(want):])


def _rms(x, g):
    return x * lax.rsqrt(jnp.mean(x * x, axis=-1, keepdims=True) + EPS) * g


def _fn_normmod(x, g, sc, sh):
    return (_rms(x, g) * (1.0 + sc) + sh,)


def _fn_lnsilu(c, g, b):
    mu = jnp.mean(c, axis=-1, keepdims=True)
    var = jnp.mean(jnp.square(c - mu), axis=-1, keepdims=True)
    y = (c - mu) * lax.rsqrt(var + EPS) * g + b
    return (y * jax.nn.sigmoid(y),)


def _fn_merge(gl, yc, yh, ys, gb):
    d = yc.shape[1]
    g = jax.nn.sigmoid(gl + gb)
    return (g[:, :d] * yc + g[:, d:2 * d] * yh + g[:, 2 * d:] * ys,)


def _fn_resid(x, y, g):
    return (x + g * y,)


def _fn_scale(y, g):
    return (g * y,)


def _fn_relu2(u):
    return (jnp.square(jnp.maximum(u, 0.0)),)


def _conv_specs(S, T):
    r = T // CONV_HALO
    cur = [pl.BlockSpec((T, CONV_CH), lambda i: (i, 0)), pl.BlockSpec((T, CONV_CH), lambda i: (i, 1))]
    prev = [pl.BlockSpec((CONV_HALO, CONV_CH), lambda i: (jnp.maximum(i * r - 1, 0), 0)),
            pl.BlockSpec((CONV_HALO, CONV_CH), lambda i: (jnp.maximum(i * r - 1, 0), 1))]
    return cur + prev


def _glu_ext(a_ref, g_ref, ah_ref, gh_ref):
    a = a_ref[...]
    sg = jax.nn.sigmoid(g_ref[...])
    uh = jnp.where(pl.program_id(0) > 0, ah_ref[...] * jax.nn.sigmoid(gh_ref[...]), 0.0)
    return a, sg, jnp.concatenate([uh, a * sg], axis=0)


def _shift_up(xe, k, T):
    return xe[:T] if k == 0 else pltpu.roll(xe, shift=xe.shape[0] - k, axis=0)[:T]


def _conv_fwd(proj, w32, b, *, name):
    S = proj.shape[0]
    T = min(ROW_TILE, S)
    lead = CONV_HALO - (CONV_WIDTH - 1)

    def kern(a_ref, g_ref, ah_ref, gh_ref, w_ref, b_ref, o_ref):
        _, _, ue = _glu_ext(a_ref, g_ref, ah_ref, gh_ref)
        acc = jnp.zeros((T, CONV_CH), F32) + b_ref[...]
        for j in range(CONV_WIDTH):
            acc = acc + w_ref[j:j + 1, :] * _shift_up(ue, lead + j, T)
        o_ref[...] = acc

    const = lambda shape: pl.BlockSpec(shape, lambda i: (0, 0))
    return pl.pallas_call(
        kern, grid=(S // T,), in_specs=_conv_specs(S, T) + [const(w32.shape), const(b.shape)],
        out_specs=pl.BlockSpec((T, CONV_CH), lambda i: (i, 0)),
        out_shape=jax.ShapeDtypeStruct((S, CONV_CH), F32), name=name,
        compiler_params=_cparams(("parallel",)))(proj, proj, proj, proj, w32, b)


def _conv_bwd(proj, dc, w32, *, name, comm=None):
    S = proj.shape[0]
    T = min(ROW_TILE, S)
    nt = S // T
    r = T // CONV_HALO
    lead = CONV_HALO - (CONV_WIDTH - 1)
    last_halo = S // CONV_HALO - 1

    def kern(a_ref, g_ref, ah_ref, gh_ref, dc_ref, dcn_ref, w_ref, dag_ref, dw_ref, db_ref):
        i = pl.program_id(0)
        a, sg, ue = _glu_ext(a_ref, g_ref, ah_ref, gh_ref)
        dc_t = dc_ref[...]
        de = jnp.concatenate([dc_t, jnp.where(i < nt - 1, dcn_ref[...], 0.0)], axis=0)

        @pl.when(i == 0)
        def _():
            dw_ref[...] = jnp.zeros_like(dw_ref)
            db_ref[...] = jnp.zeros_like(db_ref)

        du = jnp.zeros((T, CONV_CH), F32)
        for j in range(CONV_WIDTH):
            du = du + w_ref[j:j + 1, :] * _shift_up(de, CONV_WIDTH - 1 - j, T)
            dw_ref[j:j + 1, :] += jnp.sum(dc_t * _shift_up(ue, lead + j, T), axis=0, keepdims=True)
        db_ref[...] += jnp.sum(dc_t, axis=0, keepdims=True)
        dag_ref[:, :CONV_CH] = (du * sg).astype(BF16)
        dag_ref[:, CONV_CH:] = (du * a * sg * (1.0 - sg)).astype(BF16)

    const = lambda shape: pl.BlockSpec(shape, lambda i: (0, 0))
    in_specs = _conv_specs(S, T) + [
        pl.BlockSpec((T, CONV_CH), lambda i: (i, 0)),
        pl.BlockSpec((CONV_HALO, CONV_CH), lambda i: (jnp.minimum((i + 1) * r, last_halo), 0)),
        const(w32.shape)]
    return _hosted_call(
        kern, grid=(nt,), in_specs=in_specs,
        out_specs=[pl.BlockSpec((T, 2 * CONV_CH), lambda i: (i, 0)), const(w32.shape), const((1, CONV_CH))],
        out_shape=[jax.ShapeDtypeStruct((S, 2 * CONV_CH), BF16), jax.ShapeDtypeStruct(w32.shape, F32),
                   jax.ShapeDtypeStruct((1, CONV_CH), F32)],
        scratch_shapes=[], args=[proj, proj, proj, proj, dc, dc, w32], name=name, comm=comm, sem=("arbitrary",))


def _split3(x):
    h = x.astype(BF16)
    r = x - h.astype(F32)
    m = r.astype(BF16)
    return h, m, (r - m.astype(F32)).astype(BF16)


def _xdot_l(m, x):
    return sum(jnp.dot(m, p, preferred_element_type=F32) for p in _split3(x))


def _iota2(shape, dim):
    return lax.broadcasted_iota(jnp.int32, shape, dim)


def _hg_mats():
    n = HG_CHUNK
    r, c = _iota2((n, n), 0), _iota2((n, n), 1)
    low = c <= r
    same = (r // HG_SUB) == (c // HG_SUB)
    up = r <= c
    as_b = lambda m: jnp.where(m, 1.0, 0.0).astype(BF16)
    return dict(low=as_b(low), low_t=as_b(up), blk=as_b(low & same), blk_t=as_b(up & same),
                ones=jnp.ones((n, n), BF16))


@jax.custom_vjp
def _cum(m, m_t, x):
    return _xdot_l(m, x)


def _cum_bwd(res, g):
    m, m_t = res
    return jnp.zeros_like(m), jnp.zeros_like(m_t), _xdot_l(m_t, g)


_cum.defvjp(lambda m, m_t, x: (_xdot_l(m, x), (m, m_t)), _cum_bwd)


def _hg_chunk(q, f, iv, g, st, lbk, ng, mats):
    n, sub = HG_CHUNK, HG_SUB
    kk = lbk * jax.nn.sigmoid(-f)
    lf = jnp.log(1.0 - kk)
    b = _cum(mats["low"], mats["low_t"], lf)
    bs = _cum(mats["blk"], mats["blk_t"], lf)
    bt = _cum(mats["ones"], mats["ones"], lf)
    qh = q * jax.nn.sigmoid(q)
    dot_nt = lambda x, y: lax.dot_general(x.astype(BF16), y.astype(BF16), (((1,), (1,)), ((), ())),
                                          preferred_element_type=F32)
    o = dot_nt(qh * jnp.exp(b), st)
    b0 = b - bs
    qs = qh * jnp.exp(bs)
    col = _iota2((sub, n), 1)
    rows = []
    for blk in range(n // sub):
        lo = blk * sub
        sl = slice(lo, lo + sub)
        acc = o[sl]
        if blk > 0:
            ref = jnp.concatenate([b0[sl]] * (n // sub), axis=0)
            kd = kk * jnp.exp(jnp.minimum(ref - b, 0.0))
            sc = jnp.where(col < lo, dot_nt(qs[sl], kd), 0.0)
            acc = acc + jnp.dot(sc.astype(BF16), iv.astype(BF16), preferred_element_type=F32)
        bq, bk = bs[sl][None, :, :], bs[sl][:, None, :]
        s_i = lax.broadcasted_iota(jnp.int32, (sub, sub, HG_D), 0)
        t_i = lax.broadcasted_iota(jnp.int32, (sub, sub, HG_D), 1)
        keep = s_i <= t_i
        p = jnp.where(keep, qh[sl][None, :, :] * kk[sl][:, None, :] * jnp.exp(jnp.where(keep, bq - bk, 0.0)), 0.0)
        w = jnp.sum(p, axis=-1, keepdims=True)
        acc = acc + jnp.sum(w * iv[sl][:, None, :], axis=0)
        rows.append(acc)
    o = jnp.concatenate(rows, axis=0)
    kd = kk * jnp.exp(bt - b)
    st_new = jnp.exp(bt[0:1]) * st + lax.dot_general(iv.astype(BF16), kd.astype(BF16), (((0,), (0,)), ((), ())),
                                                     preferred_element_type=F32)
    out = _rms(o, ng) * (g * jax.nn.sigmoid(g))
    return out, st_new


def _hg_tile(S):
    return min(512, S)


def _hg_in_specs(rt, rev, nr):
    width = HG_HEADS * HG_D
    base = OFF_HG // width
    row = (lambda r: nr - 1 - r) if rev else (lambda r: r)
    return [pl.BlockSpec((rt, width), functools.partial(lambda r, k: (row(r), base + k), k=k)) for k in range(4)]


def _hg_cols(h):
    return slice(h * HG_D, (h + 1) * HG_D)


def _hgrn_fwd(proj, lbk, ng, *, name):
    S = proj.shape[0]
    rt = _hg_tile(S)
    nr, nc = S // rt, rt // HG_CHUNK

    def kern(q_ref, f_ref, i_ref, g_ref, lbk_ref, ng_ref, o_ref, st_out_ref, st_ref):
        @pl.when(pl.program_id(0) == 0)
        def _():
            st_ref[...] = jnp.zeros_like(st_ref)

        mats = _hg_mats()

        def body(c, carry):
            rows = pl.ds(pl.multiple_of(c * HG_CHUNK, HG_CHUNK), HG_CHUNK)
            for h in range(HG_HEADS):
                cols = _hg_cols(h)
                st = st_ref[h]
                st_out_ref[h, c] = st
                out, st_new = _hg_chunk(q_ref[rows, cols], f_ref[rows, cols], i_ref[rows, cols], g_ref[rows, cols], st,
                                        lbk_ref[:, cols], ng_ref[...], mats)
                o_ref[rows, cols] = out.astype(o_ref.dtype)
                st_ref[h] = st_new
            return carry

        lax.fori_loop(0, nc, body, 0)

    width = HG_HEADS * HG_D
    in_specs = _hg_in_specs(rt, False, nr) + [pl.BlockSpec((1, width), lambda r: (0, 0)),
                                               pl.BlockSpec((1, HG_D), lambda r: (0, 0))]
    return pl.pallas_call(
        kern, grid=(nr,), in_specs=in_specs,
        out_specs=[pl.BlockSpec((rt, width), lambda r: (r, 0)),
                   pl.BlockSpec((HG_HEADS, nc, HG_D, HG_D), lambda r: (0, r, 0, 0))],
        out_shape=[jax.ShapeDtypeStruct((S, width), BF16),
                   jax.ShapeDtypeStruct((HG_HEADS, S // HG_CHUNK, HG_D, HG_D), F32)],
        scratch_shapes=[pltpu.VMEM((HG_HEADS, HG_D, HG_D), F32)], name=name,
        compiler_params=_cparams(("arbitrary",)))(proj, proj, proj, proj, lbk, ng)


def _hgrn_bwd(proj, states, dout, lbk, ng, *, name, comm=None):
    S = proj.shape[0]
    rt = _hg_tile(S)
    nr, nc = S // rt, rt // HG_CHUNK
    width = HG_HEADS * HG_D

    def kern(q_ref, f_ref, i_ref, g_ref, st_in_ref, do_ref, lbk_ref, ng_ref,
             dq_ref, df_ref, di_ref, dg_ref, dlbk_ref, dng_ref, dst_ref):
        @pl.when(pl.program_id(0) == 0)
        def _():
            dst_ref[...] = jnp.zeros_like(dst_ref)
            dlbk_ref[...] = jnp.zeros_like(dlbk_ref)
            dng_ref[...] = jnp.zeros_like(dng_ref)

        mats = _hg_mats()
        fn = functools.partial(_hg_chunk, mats=mats)

        def body(k, carry):
            c = nc - 1 - k
            rows = pl.ds(pl.multiple_of(c * HG_CHUNK, HG_CHUNK), HG_CHUNK)
            for h in range(HG_HEADS):
                cols = _hg_cols(h)
                _, vjp = jax.vjp(fn, q_ref[rows, cols], f_ref[rows, cols], i_ref[rows, cols], g_ref[rows, cols],
                                 st_in_ref[h, c], lbk_ref[:, cols], ng_ref[...])
                dq, df, di, dg, dst, dlbk, dng = vjp((do_ref[rows, cols].astype(F32), dst_ref[h]))
                dq_ref[rows, cols] = dq.astype(BF16)
                df_ref[rows, cols] = df.astype(BF16)
                di_ref[rows, cols] = di.astype(BF16)
                dg_ref[rows, cols] = dg.astype(BF16)
                dst_ref[h] = dst
                dlbk_ref[:, cols] += dlbk
                dng_ref[h] += dng
            return carry

        lax.fori_loop(0, nc, body, 0)

    rev = lambda r: nr - 1 - r
    tile = pl.BlockSpec((rt, width), lambda r: (rev(r), 0))
    in_specs = _hg_in_specs(rt, True, nr) + [
        pl.BlockSpec((HG_HEADS, nc, HG_D, HG_D), lambda r: (0, rev(r), 0, 0)), tile,
        pl.BlockSpec((1, width), lambda r: (0, 0)), pl.BlockSpec((1, HG_D), lambda r: (0, 0))]
    return _hosted_call(
        kern, grid=(nr,), in_specs=in_specs,
        out_specs=[tile, tile, tile, tile, pl.BlockSpec((1, width), lambda r: (0, 0)),
                   pl.BlockSpec((HG_HEADS, 1, HG_D), lambda r: (0, 0, 0))],
        out_shape=[jax.ShapeDtypeStruct((S, width), BF16)] * 4 + [
            jax.ShapeDtypeStruct((1, width), F32), jax.ShapeDtypeStruct((HG_HEADS, 1, HG_D), F32)],
        scratch_shapes=[pltpu.VMEM((HG_HEADS, HG_D, HG_D), F32)],
        args=[proj, proj, proj, proj, states, dout, lbk, ng], name=name, comm=comm, sem=("arbitrary",))


def _scan_rows(x, later):
    n = x.shape[0]
    row = _iota2(x.shape, 0)
    k = 1
    while k < n:
        if later:
            x = x + jnp.where(row < n - k, pltpu.roll(x, n - k, axis=0), 0.0)
        else:
            x = x + jnp.where(row >= k, pltpu.roll(x, k, axis=0), 0.0)
        k *= 2
    return x


def _sb_block(qi, km, r_run, diag):
    n = SB_BLK
    zt = lax.dot_general(km, qi, (((1,), (1,)), ((), ())), preferred_element_type=F32)
    sp = jnp.maximum(zt, 0.0) + jnp.log(1.0 + jnp.exp(-jnp.abs(zt)))
    lk = -sp
    if diag:
        keep = (_iota2(zt.shape, 0) & (n - 1)) < _iota2(zt.shape, 1)
        lk = jnp.where(keep, lk, 0.0)
    tails = [_scan_rows(lk[a * n:(a + 1) * n], True) for a in range(2)]
    between = jnp.concatenate([tails[a] + r_run[a] for a in range(2)], axis=0)
    wgt = jnp.exp(zt + between)
    if diag:
        wgt = jnp.where(keep, wgt, 0.0)
    return sp, wgt, [t[0:1, :] for t in tails]


def _sb_norm_pair(x, g2, lane_lo):
    sq = x * x
    ms_lo = jnp.sum(jnp.where(lane_lo, sq, 0.0), axis=-1, keepdims=True)
    ms_hi = jnp.sum(jnp.where(lane_lo, 0.0, sq), axis=-1, keepdims=True)
    return x * lax.rsqrt(jnp.where(lane_lo, ms_lo, ms_hi) * (1.0 / SB_DH) + EPS) * g2


def _sb_specs(S):
    base = OFF_SB // SB_PAIR
    per = SB_HEADS * SB_DH // SB_PAIR
    cols = [pl.BlockSpec((S, SB_PAIR), functools.partial(lambda p, k: (0, base + per * k + p), k=k)) for k in range(3)]
    return cols + [pl.BlockSpec((1, SB_PAIR), lambda p: (0, 0))] * 2


def _sb_rows(i):
    return pl.ds(pl.multiple_of(i * SB_BLK, SB_BLK), SB_BLK)


def _sb_both(j, a=None):
    if a is None:
        return pl.ds(pl.multiple_of(j * 2 * SB_BLK, 2 * SB_BLK), 2 * SB_BLK)
    return pl.ds(pl.multiple_of(j * 2 * SB_BLK + a * SB_BLK, SB_BLK), SB_BLK)


def _sb_fwd(proj, qg2, kg2, *, name, comm=None):
    S = proj.shape[0]
    nb = S // SB_BLK
    scale = SB_DH ** -0.5
    n_pairs = SB_HEADS * SB_DH // SB_PAIR

    def kern(q_ref, k_ref, v_ref, qg_ref, kg_ref, o_ref, rs_ref, qp_ref, km_ref, vt_ref):
        lane_lo = _iota2((SB_BLK, SB_PAIR), 1) < SB_DH

        def prologue(j, carry):
            rows = _sb_rows(j)
            qp_ref[rows, :] = (_sb_norm_pair(q_ref[rows, :], qg_ref[...], lane_lo) * scale).astype(BF16)
            kn = _sb_norm_pair(k_ref[rows, :], kg_ref[...], lane_lo)
            v = v_ref[rows, :]
            for a, mine in enumerate((lane_lo, jnp.logical_not(lane_lo))):
                km_ref[_sb_both(j, a), :] = jnp.where(mine, kn, 0.0).astype(BF16)
                vt_ref[:, _sb_both(j, a)] = jnp.where(mine, v, 0.0).T.astype(BF16)
            return carry

        lax.fori_loop(0, nb, prologue, 0)

        def qblock(i, carry):
            qi = qp_ref[_sb_rows(i), :]

            def step(j, diag, st):
                acc, r_run = st
                _, wgt, lk_sum = _sb_block(qi, km_ref[_sb_both(j), :], r_run, diag)
                acc = acc + jnp.dot(vt_ref[:, _sb_both(j)], wgt.astype(BF16), preferred_element_type=F32)
                return acc, [r_run[a] + lk_sum[a] for a in range(2)]

            def note(j, st):
                for a in range(2):
                    rs_ref[a, i, pl.ds(j, 1), :] = st[1][a]
                return jnp.maximum(jnp.max(st[1][0]), jnp.max(st[1][1])) > SB_SKIP

            zero = jnp.zeros((1, SB_BLK), F32)
            st = step(i, True, (jnp.zeros((SB_PAIR, SB_BLK), F32), [zero, zero]))
            go = lax.cond(i > 0, lambda: note(i - 1, st).astype(jnp.int32), lambda: jnp.int32(0))

            def body(c):
                jj, _, st = c
                j = i - 1 - jj
                st = step(j, False, st)
                go = lax.cond(j > 0, lambda: note(j - 1, st).astype(jnp.int32), lambda: jnp.int32(0))
                return jj + 1, go, st

            _, _, st = lax.while_loop(lambda c: c[1] > 0, body, (jnp.int32(0), go, st))
            o_ref[_sb_rows(i), :] = st[0].T.astype(o_ref.dtype)
            return carry

        lax.fori_loop(0, nb, qblock, 0)

    width = SB_HEADS * SB_DH
    return _hosted_call(
        kern, grid=(n_pairs,), in_specs=_sb_specs(S),
        out_specs=[pl.BlockSpec((S, SB_PAIR), lambda p: (0, p)),
                   pl.BlockSpec((2, nb, nb, SB_BLK), lambda p: (p, 0, 0, 0))],
        out_shape=[jax.ShapeDtypeStruct((S, width), BF16), jax.ShapeDtypeStruct((SB_HEADS, nb, nb, SB_BLK), F32)],
        scratch_shapes=[pltpu.VMEM((S, SB_PAIR), BF16), pltpu.VMEM((2 * S, SB_PAIR), BF16),
                        pltpu.VMEM((SB_PAIR, 2 * S), BF16)],
        args=[proj, proj, proj, qg2, kg2], name=name, comm=comm, sem=("parallel",))


def _sb_bwd(proj, qg2, kg2, rs, do, *, name, comm=None):
    S = proj.shape[0]
    nb = S // SB_BLK
    scale = SB_DH ** -0.5
    n_pairs = SB_HEADS * SB_DH // SB_PAIR

    def kern(q_ref, k_ref, v_ref, qg_ref, kg_ref, rs_ref, do_ref, dq_ref, dk_ref, dv_ref, dqg_ref, dkg_ref,
             qp_ref, km_ref, kt_ref, vm_ref, dqn_ref, dkn_ref, dvs_ref):
        lane_lo = _iota2((SB_BLK, SB_PAIR), 1) < SB_DH
        heads = (lane_lo, jnp.logical_not(lane_lo))
        fn_q = lambda x, g: _sb_norm_pair(x, g, lane_lo) * scale
        fn_k = lambda x, g: _sb_norm_pair(x, g, lane_lo)

        def prologue(j, carry):
            rows = _sb_rows(j)
            qp_ref[rows, :] = fn_q(q_ref[rows, :], qg_ref[...]).astype(BF16)
            kn = fn_k(k_ref[rows, :], kg_ref[...])
            v = v_ref[rows, :]
            for a, mine in enumerate(heads):
                k_a = jnp.where(mine, kn, 0.0)
                km_ref[_sb_both(j, a), :] = k_a.astype(BF16)
                kt_ref[:, _sb_both(j, a)] = k_a.T.astype(BF16)
                vm_ref[_sb_both(j, a), :] = jnp.where(mine, v, 0.0).astype(BF16)
            return carry

        lax.fori_loop(0, nb, prologue, 0)
        dkn_ref[...] = jnp.zeros_like(dkn_ref)
        dvs_ref[...] = jnp.zeros_like(dvs_ref)

        def qblock(i, carry):
            qi = qp_ref[_sb_rows(i), :]
            doi = do_ref[_sb_rows(i), :]

            def step(j, diag, st):
                dqa, e_run = st
                zero = jnp.zeros((1, SB_BLK), F32)
                r_run = [zero, zero] if diag else [rs_ref[a, i, pl.ds(j, 1), :] for a in range(2)]
                sp, wgt, _ = _sb_block(qi, km_ref[_sb_both(j), :], r_run, diag)
                dp = lax.dot_general(vm_ref[_sb_both(j), :], doi, (((1,), (1,)), ((), ())), preferred_element_type=F32)
                e = dp * wgt
                heads_e = [_scan_rows(e[a * SB_BLK:(a + 1) * SB_BLK], False) for a in range(2)]
                e_left = jnp.concatenate([heads_e[a] + e_run[a] for a in range(2)], axis=0) - e
                s_neg = jnp.exp(-sp)
                dz = e * s_neg - e_left * (1.0 - s_neg)
                if diag:
                    dz = jnp.where((_iota2(dz.shape, 0) & (SB_BLK - 1)) < _iota2(dz.shape, 1), dz, 0.0)
                dzb = dz.astype(BF16)
                dkn_ref[_sb_both(j), :] += jnp.dot(dzb, qi, preferred_element_type=F32)
                dvs_ref[_sb_both(j), :] += jnp.dot(wgt.astype(BF16), doi, preferred_element_type=F32)
                dqa = dqa + jnp.dot(kt_ref[:, _sb_both(j)], dzb, preferred_element_type=F32)
                return dqa, [e_run[a] + heads_e[a][SB_BLK - 1:SB_BLK, :] for a in range(2)]

            def live(j):
                jc = jnp.maximum(j, 0)
                top = jnp.maximum(jnp.max(rs_ref[0, i, pl.ds(jc, 1), :]), jnp.max(rs_ref[1, i, pl.ds(jc, 1), :]))
                return jnp.logical_and(j >= 0, top > SB_SKIP).astype(jnp.int32)

            first, _ = lax.while_loop(lambda c: c[1] > 0, lambda c: (c[0] - 1, live(c[0] - 2)), (i, live(i - 1)))
            zero = jnp.zeros((1, SB_BLK), F32)
            st = lax.fori_loop(first, i, lambda j, st: step(j, False, st), (jnp.zeros((SB_PAIR, SB_BLK), F32), [zero, zero]))
            st = step(i, True, st)
            dqn_ref[_sb_rows(i), :] = st[0].T
            return carry

        lax.fori_loop(0, nb, qblock, 0)
        dqg_ref[...] = jnp.zeros_like(dqg_ref)
        dkg_ref[...] = jnp.zeros_like(dkg_ref)

        def epilogue(j, carry):
            rows = _sb_rows(j)
            _, vjp_q = jax.vjp(fn_q, q_ref[rows, :], qg_ref[...])
            dq, dqg = vjp_q(dqn_ref[rows, :])
            _, vjp_k = jax.vjp(fn_k, k_ref[rows, :], kg_ref[...])
            dk, dkg = vjp_k(jnp.where(lane_lo, dkn_ref[_sb_both(j, 0), :], dkn_ref[_sb_both(j, 1), :]))
            dq_ref[rows, :] = dq.astype(BF16)
            dk_ref[rows, :] = dk.astype(BF16)
            dv_ref[rows, :] = jnp.where(lane_lo, dvs_ref[_sb_both(j, 0), :], dvs_ref[_sb_both(j, 1), :]).astype(BF16)
            dqg_ref[0] += dqg
            dkg_ref[0] += dkg
            return carry

        lax.fori_loop(0, nb, epilogue, 0)

    width = SB_HEADS * SB_DH
    pair = pl.BlockSpec((S, SB_PAIR), lambda p: (0, p))
    dgain = pl.BlockSpec((1, 1, SB_PAIR), lambda p: (p, 0, 0))
    in_specs = _sb_specs(S) + [pl.BlockSpec((2, nb, nb, SB_BLK), lambda p: (p, 0, 0, 0)), pair]
    return _hosted_call(
        kern, grid=(n_pairs,), in_specs=in_specs, out_specs=[pair, pair, pair, dgain, dgain],
        out_shape=[jax.ShapeDtypeStruct((S, width), BF16)] * 3 + [jax.ShapeDtypeStruct((n_pairs, 1, SB_PAIR), F32)] * 2,
        scratch_shapes=[pltpu.VMEM((S, SB_PAIR), BF16), pltpu.VMEM((2 * S, SB_PAIR), BF16), pltpu.VMEM((SB_PAIR, 2 * S), BF16),
                        pltpu.VMEM((2 * S, SB_PAIR), BF16), pltpu.VMEM((S, SB_PAIR), F32),
                        pltpu.VMEM((2 * S, SB_PAIR), F32), pltpu.VMEM((2 * S, SB_PAIR), F32)],
        args=[proj, proj, proj, qg2, kg2, rs, do], name=name, comm=comm, sem=("parallel",))


def _loss_head(y, target, *, name):
    S, D = y.shape
    T = min(ROW_TILE, S)

    def kern(y_ref, t_ref, dy_ref, acc_ref):
        err = y_ref[...] - t_ref[...]
        dy_ref[...] = err * (1.0 / D)
        col = jnp.sum(err * err, axis=0, keepdims=True)
        part = sum(col[:, k * 128:(k + 1) * 128] for k in range(D // 128))

        @pl.when(pl.program_id(0) == 0)
        def _():
            acc_ref[...] = part

        @pl.when(pl.program_id(0) > 0)
        def _():
            acc_ref[...] += part

    tile = pl.BlockSpec((T, D), lambda i: (i, 0))
    return pl.pallas_call(
        kern, grid=(S // T,), in_specs=[tile, tile], out_specs=[tile, pl.BlockSpec((1, 128), lambda i: (0, 0))],
        out_shape=[jax.ShapeDtypeStruct((S, D), F32), jax.ShapeDtypeStruct((1, 128), F32)],
        name=name, compiler_params=_cparams(("arbitrary",)))(y, target)


def _adamw_math(w, g, m, v):
    m = ADAM_B1 * m + (1.0 - ADAM_B1) * g
    v = ADAM_B2 * v + (1.0 - ADAM_B2) * jnp.square(g)
    m_hat = m / (1.0 - ADAM_B1 ** ADAM_STEP)
    v_hat = v / (1.0 - ADAM_B2 ** ADAM_STEP)
    return -ADAM_LR * (m_hat / (jnp.sqrt(v_hat) + ADAM_EPS) + ADAM_WD * w), m, v


def _adamw(w, g, m, v, *, name):
    R, C = w.shape
    T = _pick(R, (256, 128, 64, 32, 16, 8))

    def kern(w_ref, g_ref, m_ref, v_ref, d_ref, mo_ref, vo_ref):
        d, mn, vn = _adamw_math(w_ref[...], g_ref[...], m_ref[...], v_ref[...])
        d_ref[...] = d
        mo_ref[...] = mn
        vo_ref[...] = vn

    tile = pl.BlockSpec((T, C), lambda i: (i, 0))
    return pl.pallas_call(
        kern, grid=(R // T,), in_specs=[tile] * 4, out_specs=[tile] * 3,
        out_shape=[jax.ShapeDtypeStruct((R, C), F32)] * 3, name=name,
        compiler_params=_cparams(("parallel",)))(w, g, m, v)


def _adamw_layer(w, g, m, v, layer, prev, *, name):
    L, R, C = w.shape
    T = _pick(R, (256, 128, 64, 32, 16, 8))

    def kern(w_ref, g_ref, m_ref, v_ref, *rest):
        go_ref, d_ref, mo_ref, vo_ref = rest[-4:]
        grad = g_ref[...]
        d, mn, vn = _adamw_math(w_ref[...], grad, m_ref[...], v_ref[...])
        go_ref[...] = grad
        d_ref[...] = d
        mo_ref[...] = mn
        vo_ref[...] = vn

    layer_tile = pl.BlockSpec((None, T, C), lambda i: (layer, i, 0))
    in_specs = [layer_tile, pl.BlockSpec((T, C), lambda i: (i, 0)), layer_tile, layer_tile]
    args, aliases = [w, g, m, v], {}
    if prev is not None:
        in_specs += [pl.BlockSpec(memory_space=pl.ANY)] * 4
        args += list(prev)
        aliases = {4 + k: k for k in range(4)}
    return pl.pallas_call(
        kern, grid=(R // T,), in_specs=in_specs, out_specs=[layer_tile] * 4,
        out_shape=[jax.ShapeDtypeStruct((L, R, C), F32)] * 4, input_output_aliases=aliases, name=name,
        compiler_params=_cparams(("parallel",)))(*args)


def _sum8(g, *, name):
    def kern(g_ref, o_ref):
        acc = g_ref[0]
        for d in range(1, g.shape[0]):
            acc = acc + g_ref[d]
        o_ref[...] = acc

    return pl.pallas_call(kern, out_shape=jax.ShapeDtypeStruct(g.shape[1:], F32), name=name,
                          compiler_params=_cparams())(g)


def _place():
    return lax.axis_index("x"), lax.axis_index("y"), lax.axis_index("c")


def _other_chips(x, y):
    return [(1 - x, y), (x, 1 - y), (1 - x, 1 - y)]


def _remote(src, dst, send_sems, recv_sems, k, to):
    return pltpu.make_async_remote_copy(src_ref=src, dst_ref=dst, send_sem=send_sems.at[k], recv_sem=recv_sems.at[k],
                                        device_id=to, device_id_type=MESH)


def _all_gather_small(v, *, name):
    def body(x_ref, out_ref, send_sems, recv_sems, local_sem):
        x, y, c = _place()
        me = 4 * x + 2 * y + c
        mine = pltpu.make_async_copy(x_ref, out_ref.at[me], local_sem)
        mine.start()
        peers = []
        for f in range(1, 8):
            peers.append((1 - x if f & 4 else x, 1 - y if f & 2 else y, 1 - c if f & 1 else c))
        sends = [_remote(x_ref, out_ref.at[me], send_sems, recv_sems, k, p) for k, p in enumerate(peers)]
        for cp in sends:
            cp.start()
        for k, (px, py, pc) in enumerate(peers):
            _remote(x_ref, out_ref.at[4 * px + 2 * py + pc], send_sems, recv_sems, k, (px, py, pc)).wait_recv()
        for cp in sends:
            cp.wait_send()
        mine.wait()

    return pl.pallas_call(
        body, out_shape=jax.ShapeDtypeStruct((8,) + v.shape, v.dtype),
        in_specs=[pl.BlockSpec(memory_space=pltpu.VMEM)], out_specs=pl.BlockSpec(memory_space=pltpu.VMEM),
        scratch_shapes=[pltpu.SemaphoreType.DMA((7,)), pltpu.SemaphoreType.DMA((7,)), pltpu.SemaphoreType.DMA],
        name=name, compiler_params=_cparams())(v)


def _hosted_call(kern, *, grid, in_specs, out_specs, out_shape, scratch_shapes, args, name, comm=None, sem=None):
    if comm is None:
        res = pl.pallas_call(kern, grid=grid, in_specs=in_specs, out_specs=out_specs, out_shape=out_shape,
                             scratch_shapes=scratch_shapes, name=name, compiler_params=_cparams(sem))(*args)
        return list(res), []
    n_in, n_out, n_scr = len(in_specs), len(out_specs), len(scratch_shapes)
    c_in, c_out = len(comm.inputs), len(comm.out_shapes)
    steps = grid[0]

    def body(*refs):
        ins, ci = refs[:n_in], refs[n_in:n_in + c_in]
        outs = refs[n_in + c_in:n_in + c_in + n_out]
        co = refs[n_in + c_in + n_out:n_in + c_in + n_out + c_out]
        scr = refs[n_in + c_in + n_out + c_out:n_in + c_in + n_out + c_out + n_scr]
        cs = refs[n_in + c_in + n_out + c_out + n_scr:]
        step = pl.program_id(0)

        @pl.when(step == 0)
        def _():
            comm.begin(ci, co, cs)

        kern(*ins, *outs, *scr)

        @pl.when(step == steps // 2)
        def _():
            comm.middle(ci, co, cs)

        @pl.when(step == steps - 1)
        def _():
            comm.end(ci, co, cs)

    hbm = pl.BlockSpec(memory_space=pltpu.HBM)
    res = pl.pallas_call(
        body, grid=grid, in_specs=list(in_specs) + [hbm] * c_in, out_specs=list(out_specs) + [hbm] * c_out,
        out_shape=list(out_shape) + list(comm.out_shapes), scratch_shapes=list(scratch_shapes) + list(comm.scratch),
        input_output_aliases={n_in + i: n_out + o for i, o in comm.aliases.items()},
        name=name, compiler_params=_cparams(("arbitrary",)))(*args, *comm.inputs)
    return list(res[:n_out]), list(res[n_out:])


def _run_comm(comm, *, name):
    return _hosted_call(lambda: None, grid=(1,), in_specs=[], out_specs=[], out_shape=[], scratch_shapes=[], args=[],
                        name=name, comm=comm)[1]


class _Gather:
    def __init__(self, shards, kinds, items):
        used = sorted({w for w, _ in items})
        self.slot = {w: k for k, w in enumerate(used)}
        self.inputs = [shards[w] for w in used]
        self.items, self.kinds = list(items), kinds
        self.shapes = {w: shards[w].shape[1:] for w in used}
        self.out_shapes = [jax.ShapeDtypeStruct((r, 4 * n) if kinds[w] == "col" else (4 * r, n), shards[w].dtype)
                           for w, _ in items for r, n in [self.shapes[w]]]
        n_items = len(items)
        self.scratch = [pltpu.SemaphoreType.DMA((6 * n_items,)), pltpu.SemaphoreType.DMA((6 * n_items,)),
                        pltpu.SemaphoreType.DMA((n_items,))]
        self.aliases = {}

    def _piece(self, ref, w, qq, half):
        r, n = self.shapes[w]
        h = r // 2
        lo, size = (0, r) if half is None else (half * h, h)
        if self.kinds[w] == "col":
            return ref.at[pl.ds(pl.multiple_of(lo, 16), size), pl.ds(pl.multiple_of(qq * n, 128), n)]
        return ref.at[pl.ds(pl.multiple_of(qq * r + lo, 16), size), :]

    def _mine(self, ci, w, l, half):
        h = self.shapes[w][0] // 2
        return ci[self.slot[w]].at[l, pl.ds(pl.multiple_of(half * h, 16), h), :]

    def begin(self, ci, co, cs):
        send_sems, recv_sems, local_sems = cs
        x, y, c = _place()
        q = 2 * x + y
        for k, (w, l) in enumerate(self.items):
            pltpu.make_async_copy(ci[self.slot[w]].at[l], self._piece(co[k], w, q, None), local_sems.at[k]).start()
            for j, (cx, cy) in enumerate(_other_chips(x, y)):
                _remote(self._mine(ci, w, l, c), self._piece(co[k], w, q, c), send_sems, recv_sems, 6 * k + j,
                        (cx, cy, c)).start()

    def middle(self, ci, co, cs):
        send_sems, recv_sems, _ = cs
        x, y, c = _place()
        for k, (w, l) in enumerate(self.items):
            for j, (cx, cy) in enumerate(_other_chips(x, y)):
                win = self._piece(co[k], w, 2 * cx + cy, c)
                _remote(win, win, send_sems, recv_sems, 6 * k + j, (cx, cy, c)).wait_recv()
                _remote(win, win, send_sems, recv_sems, 6 * k + 3 + j, (x, y, 1 - c)).start()

    def end(self, ci, co, cs):
        send_sems, recv_sems, local_sems = cs
        x, y, c = _place()
        q = 2 * x + y
        for k, (w, l) in enumerate(self.items):
            for j, (cx, cy) in enumerate(_other_chips(x, y)):
                win = self._piece(co[k], w, 2 * cx + cy, 1 - c)
                _remote(win, win, send_sems, recv_sems, 6 * k + 3 + j, (x, y, 1 - c)).wait_recv()
        for k, (w, l) in enumerate(self.items):
            for j, (cx, cy) in enumerate(_other_chips(x, y)):
                _remote(self._mine(ci, w, l, c), self._piece(co[k], w, q, c), send_sems, recv_sems, 6 * k + j,
                        (cx, cy, c)).wait_send()
                win = self._piece(co[k], w, 2 * cx + cy, c)
                _remote(win, win, send_sems, recv_sems, 6 * k + 3 + j, (x, y, 1 - c)).wait_send()
            pltpu.make_async_copy(ci[self.slot[w]].at[l], self._piece(co[k], w, q, None), local_sems.at[k]).wait()


def _half_rows(ref, half, h):
    return ref.at[:, pl.ds(pl.multiple_of(half * h, 16), h), :]


class _Copies:
    def __init__(self, inputs, out_shapes, count, pairs, aliases=None):
        self.inputs, self.out_shapes, self.pairs = list(inputs), list(out_shapes), pairs
        self.scratch = [pltpu.SemaphoreType.DMA((count,)), pltpu.SemaphoreType.DMA((count,))]
        self.aliases = aliases or {}

    def _copies(self, ci, co, cs):
        x, y, c = _place()
        return [_remote(src, dst, cs[0], cs[1], k, to) for k, (src, dst, to) in enumerate(self.pairs(ci, co, x, y, c))]

    def begin(self, ci, co, cs):
        for cp in self._copies(ci, co, cs):
            cp.start()

    def middle(self, ci, co, cs):
        pass

    def end(self, ci, co, cs):
        for cp in self._copies(ci, co, cs):
            cp.wait()


def _swap_halves(gs):
    def pairs(ci, co, x, y, c):
        return [(_half_rows(ci[k], 1 - c, g.shape[1] // 2), co[k], (x, y, 1 - c)) for k, g in enumerate(gs)]

    return _Copies(gs, [jax.ShapeDtypeStruct((g.shape[0], g.shape[1] // 2, g.shape[2]), g.dtype) for g in gs],
                   len(gs), pairs)


def _scatter_quarters(ps, kinds):
    part = [((p.shape[1], p.shape[2] // 4) if kind == "col" else (p.shape[1], p.shape[2])) for p, kind in zip(ps, kinds)]

    def pairs(ci, co, x, y, c):
        out = []
        for k, kind in enumerate(kinds):
            n = part[k][1]
            for j, (cx, cy) in enumerate(_other_chips(x, y)):
                qj = 2 * cx + cy
                src = ci[k].at[0, :, pl.ds(pl.multiple_of(qj * n, 128), n)] if kind == "col" else ci[k].at[qj]
                out.append((src, co[k].at[j], (cx, cy, c)))
        return out

    return _Copies(ps, [jax.ShapeDtypeStruct((3,) + pt, p.dtype) for pt, p in zip(part, ps)], 3 * len(ps), pairs)


def _share_halves(gs):
    def rows(co, k, half):
        h = gs[k].shape[0] // 2
        return co[k].at[pl.ds(pl.multiple_of(half * h, 16), h), :]

    def pairs(ci, co, x, y, c):
        return [(rows(co, k, c), rows(co, k, c), (x, y, 1 - c)) for k in range(len(gs))]

    prog = _Copies(gs, [jax.ShapeDtypeStruct(g.shape, g.dtype) for g in gs], len(gs), pairs,
                   aliases={k: k for k in range(len(gs))})

    def end(ci, co, cs):
        x, y, c = _place()
        for k in range(len(gs)):
            cp = _remote(rows(co, k, c), rows(co, k, 1 - c), cs[0], cs[1], k, (x, y, 1 - c))
            cp.wait_send()
            cp.wait_recv()

    prog.end = end
    return prog


def _wide_tile(n):
    return _pick(n, (2048, 1920, 1024, 512, 256, 128))


def _pair_sum(g, land, place, *, name):
    B, R, N = g.shape
    h = R // 2
    tr, tc = _pick(h, (256, 128)), _wide_tile(N)

    def kern(place_ref, g_ref, l_ref, o_ref):
        o_ref[...] = (g_ref[...] + l_ref[...]).astype(o_ref.dtype)

    grid_spec = pltpu.PrefetchScalarGridSpec(
        num_scalar_prefetch=1, grid=(B, h // tr, N // tc),
        in_specs=[pl.BlockSpec((None, tr, tc), lambda b, i, j, p: (b, p[1] * (h // tr) + i, j)),
                  pl.BlockSpec((None, tr, tc), lambda b, i, j, p: (b, i, j))],
        out_specs=pl.BlockSpec((None, tr, tc), lambda b, i, j, p: (b, i, j)))
    return pl.pallas_call(kern, grid_spec=grid_spec, out_shape=jax.ShapeDtypeStruct((B, h, N), BF16), name=name,
                          compiler_params=_cparams(("parallel", "parallel", "parallel")))(place, g, land)


def _quarter_sum(p, land, kind, shard_shape, place, *, name):
    L, r, n = shard_shape
    h = r // 2
    tr, tc = _pick(h, (256, 128)), _wide_tile(n)

    def kern(place_ref, p_ref, a_ref, b_ref, c_ref, o_ref):
        o_ref[...] = ((p_ref[...].astype(F32) + a_ref[...].astype(F32)) + b_ref[...].astype(F32)) + c_ref[...].astype(F32)

    if kind == "col":
        p_spec = pl.BlockSpec((None, tr, tc), lambda l, i, j, pr: (l, i, pr[0] * (n // tc) + j))
    else:
        p_spec = pl.BlockSpec((None, None, tr, tc), lambda l, i, j, pr: (l, pr[0], i, j))
    lands = [pl.BlockSpec((None, None, tr, tc), functools.partial(lambda l, i, j, pr, s: (s, l, i, j), s=s))
             for s in range(3)]
    grid_spec = pltpu.PrefetchScalarGridSpec(
        num_scalar_prefetch=1, grid=(L, h // tr, n // tc), in_specs=[p_spec] + lands,
        out_specs=pl.BlockSpec((None, tr, tc), lambda l, i, j, pr: (l, pr[1] * (h // tr) + i, j)))
    return pl.pallas_call(kern, grid_spec=grid_spec, out_shape=jax.ShapeDtypeStruct((L, r, n), F32), name=name,
                          compiler_params=_cparams(("parallel", "parallel", "parallel")))(place, p, land, land, land)


class _ReduceScatter:
    def __init__(self, grads, kinds, shard_shapes, place, tag):
        self.kinds, self.shapes, self.place, self.tag = kinds, shard_shapes, place, tag
        self.g3 = [g[None] if kind == "col" else g.reshape(4, g.shape[0] // 4, g.shape[1]) for g, kind in zip(grads, kinds)]

    def swap(self):
        return _swap_halves(self.g3)

    def pair_sums(self, lands):
        self.ps = [_pair_sum(g, land, self.place, name=f"rs_pair_sum_{self.tag}_{k}")
                   for k, (g, land) in enumerate(zip(self.g3, lands))]

    def scatter(self):
        return _scatter_quarters(self.ps, self.kinds)

    def quarter_sums(self, parts):
        self.halves = []
        for k, (p, part) in enumerate(zip(self.ps, parts)):
            p4 = p if self.kinds[k] == "col" else p[None]
            out = _quarter_sum(p4, part[:, None], self.kinds[k], (1,) + tuple(self.shapes[k]), self.place,
                               name=f"rs_quarter_sum_{self.tag}_{k}")
            self.halves.append(out[0])

    def share(self):
        return _share_halves(self.halves)

    def run(self):
        self.pair_sums(_run_comm(self.swap(), name=f"rs_swap_{self.tag}"))
        self.quarter_sums(_run_comm(self.scatter(), name=f"rs_scatter_{self.tag}"))
        return _run_comm(self.share(), name=f"rs_share_{self.tag}")


_WEIGHTS = ["mod_w", "mod_b", "norm1_g", "w_in", "gate_b", "conv_w", "conv_b", "conv_ln_g", "conv_ln_b", "w_conv_proj",
            "hgrn_lb", "hgrn_norm_g", "w_hgrn_proj", "sb_qn_g", "sb_kn_g", "w_sb_proj", "w_out", "norm2_g", "mlp_w1",
            "mlp_w2"]
_BIG = [("w_in", "col"), ("w_conv_proj", "col"), ("w_hgrn_proj", "col"), ("w_sb_proj", "col"), ("w_out", "row"),
        ("mlp_w1", "col"), ("mlp_w2", "row")]
_REPLICATED = ["mod_b", "norm1_g", "gate_b", "conv_b", "conv_ln_g", "conv_ln_b", "hgrn_lb", "hgrn_norm_g", "sb_qn_g",
               "sb_kn_g", "norm2_g"]
LANES = 128


class _Pack:
    def __init__(self, items):
        self.shapes = {n: a.shape for n, a in items}
        self.offsets, pos = {}, 0
        for n, a in items:
            self.offsets[n] = pos
            pos += math.prod(a.shape)
        self.rows = -(-pos // (8 * LANES)) * 8
        flat = jnp.concatenate([a.reshape(-1).astype(F32) for _, a in items])
        self.array = jnp.pad(flat, (0, self.rows * LANES - pos)).reshape(self.rows, LANES)

    def get(self, packed, name):
        lead = packed.shape[:-2]
        flat = packed.reshape(lead + (self.rows * LANES,))
        n = math.prod(self.shapes[name])
        return lax.slice_in_dim(flat, self.offsets[name], self.offsets[name] + n, axis=len(lead)).reshape(
            lead + self.shapes[name])


def _lower_bounds(hgrn_lb):
    p = jax.nn.softmax(hgrn_lb.astype(F32), axis=0)
    return jnp.cumsum(p, axis=0) - p[0:1]


def _layer_fwd(x, w, p, l, comm=None):
    S, D = x.shape
    r = {"x": x}
    (r["h"],) = _rowop(_fn_normmod, [(x, 0, D)], [p["n1g"], p["sc1"], p["sh1"]], [(D, BF16)], name=f"normmod1_fwd_{l}")
    proj = r["proj"] = _matmul(r["h"], w["w_in", l], name=f"w_in_fwd_{l}")
    r["cpre"] = _conv_fwd(proj, p["w32"], p["conv_b"], name=f"conv_fwd_{l}")
    (r["cact"],) = _rowop(_fn_lnsilu, [(r["cpre"], 0, CONV_CH)], [p["lng"], p["lnb"]], [(CONV_CH, BF16)],
                          name=f"conv_ln_fwd_{l}")
    r["hg"], r["states"] = _hgrn_fwd(proj, p["lbk"], p["ng"], name=f"hgrn_fwd_{l}")
    (r["sb"], r["rs"]), got = _sb_fwd(proj, p["qg"], p["kg"], name=f"sb_fwd_{l}", comm=comm)
    if comm is not None:
        w.update({(_BIG[k][0], layer): arr for (k, layer), arr in zip(comm.items, got)})
    r["y_c"] = _matmul(r["cact"], w["w_conv_proj", l], name=f"w_conv_proj_fwd_{l}")
    r["y_h"] = _matmul(r["hg"], w["w_hgrn_proj", l], name=f"w_hgrn_proj_fwd_{l}")
    r["y_s"] = _matmul(r["sb"], w["w_sb_proj", l], name=f"w_sb_proj_fwd_{l}")
    (r["merged"],) = _rowop(_fn_merge, [(proj, OFF_GL, 3 * D), (r["y_c"], 0, D), (r["y_h"], 0, D), (r["y_s"], 0, D)],
                            [p["gate_b"]], [(D, BF16)], name=f"merge_fwd_{l}")
    r["a_out"] = _matmul(r["merged"], w["w_out", l], name=f"w_out_fwd_{l}")
    (r["x1"],) = _rowop(_fn_resid, [(x, 0, D), (r["a_out"], 0, D)], [p["g1"]], [(D, F32)], name=f"resid1_fwd_{l}")
    (r["h2"],) = _rowop(_fn_normmod, [(r["x1"], 0, D)], [p["n2g"], p["sc2"], p["sh2"]], [(D, BF16)],
                        name=f"normmod2_fwd_{l}")
    r["u"], r["act"] = _matmul(r["h2"], w["mlp_w1", l], name=f"mlp_w1_fwd_{l}", post=lambda u: (u,) + _fn_relu2(u),
                               out_dtypes=(F32, BF16))
    r["m_out"] = _matmul(r["act"], w["mlp_w2", l], name=f"mlp_w2_fwd_{l}")
    (x2,) = _rowop(_fn_resid, [(r["x1"], 0, D), (r["m_out"], 0, D)], [p["g2"]], [(D, F32)], name=f"resid2_fwd_{l}")
    return x2, r


def _layer_bwd(dx2, r, w, p, l, grads, carry=None):
    S, D = dx2.shape
    small = {}

    def dweight(name, a, dy):
        grads[name, l] = _matmul(a, dy, ta=True, name=f"{name}_dw_{l}")

    stage = (lambda k, got: carry(k, got)) if carry is not None else (lambda k, got: None)

    (dm_out,), (dg2,) = _rowop_bwd(_fn_scale, [(r["m_out"], 0, D)], [p["g2"]], [dx2], [BF16], name=f"resid2_bwd_{l}")
    (du,) = _matmul(dm_out, w["mlp_w2", l], tb=True, name=f"mlp_w2_dx_{l}", extras=[r["u"]], out_dtypes=(BF16,),
                    post=lambda dact, u: (dact * (2.0 * jnp.maximum(u, 0.0)),))
    dweight("mlp_w2", r["act"], dm_out)
    dh2 = _matmul(du, w["mlp_w1", l], tb=True, name=f"mlp_w1_dx_{l}")
    dweight("mlp_w1", r["h2"], du)
    (dx1,), (small["norm2_g"], dsc2, dsh2) = _rowop_bwd(
        _fn_normmod, [(r["x1"], 0, D)], [p["n2g"], p["sc2"], p["sh2"]], [dh2], [F32], add={0: dx2},
        name=f"normmod2_bwd_{l}")
    (da_out,), (dg1,) = _rowop_bwd(_fn_scale, [(r["a_out"], 0, D)], [p["g1"]], [dx1], [BF16], name=f"resid1_bwd_{l}")
    dmerged = _matmul(da_out, w["w_out", l], tb=True, name=f"w_out_dx_{l}")
    dweight("w_out", r["merged"], da_out)
    (dgl, dy_c, dy_h, dy_s), (small["gate_b"],) = _rowop_bwd(
        _fn_merge, [(r["proj"], OFF_GL, 3 * D), (r["y_c"], 0, D), (r["y_h"], 0, D), (r["y_s"], 0, D)], [p["gate_b"]],
        [dmerged], [BF16] * 4, name=f"merge_bwd_{l}")
    dweight("w_conv_proj", r["cact"], dy_c)
    dweight("w_hgrn_proj", r["hg"], dy_h)
    dweight("w_sb_proj", r["sb"], dy_s)
    dcact = _matmul(dy_c, w["w_conv_proj", l], tb=True, name=f"w_conv_proj_dx_{l}")
    (dcpre,), (small["conv_ln_g"], small["conv_ln_b"]) = _rowop_bwd(
        _fn_lnsilu, [(r["cpre"], 0, CONV_CH)], [p["lng"], p["lnb"]], [dcact], [F32], name=f"conv_ln_bwd_{l}")
    (d_conv, dw32, small["conv_b"]), got = _conv_bwd(r["proj"], dcpre, p["w32"], name=f"conv_bwd_{l}",
                                                      comm=stage(0, None))
    small["conv_w"] = dw32[:CONV_WIDTH]
    dhg = _matmul(dy_h, w["w_hgrn_proj", l], tb=True, out_dtype=BF16, name=f"w_hgrn_proj_dx_{l}")
    (dq, df, di, dg, dlbk, dng), got = _hgrn_bwd(r["proj"], r["states"], dhg, p["lbk"], p["ng"], name=f"hgrn_bwd_{l}",
                                                 comm=stage(1, got))
    small["lower"] = -dlbk
    small["hgrn_norm_g"] = jnp.sum(dng, axis=0)
    dsb = _matmul(dy_s, w["w_sb_proj", l], tb=True, out_dtype=BF16, name=f"w_sb_proj_dx_{l}")
    (dsq, dsk, dsv, dqg, dkg), got = _sb_bwd(r["proj"], p["qg"], p["kg"], r["rs"], dsb, name=f"sb_bwd_{l}",
                                             comm=stage(2, got))
    stage(3, got)
    fold = lambda t: jnp.sum(t.reshape(-1, SB_DH), axis=0, keepdims=True)
    small["sb_qn_g"], small["sb_kn_g"] = fold(dqg), fold(dkg)
    dproj = jnp.concatenate([d_conv, dq, df, di, dg, dsq, dsk, dsv, dgl], axis=1)
    dh = _matmul(dproj, w["w_in", l], tb=True, name=f"w_in_dx_{l}")
    dweight("w_in", r["h"], dproj)
    (dx,), (small["norm1_g"], dsc1, dsh1) = _rowop_bwd(
        _fn_normmod, [(r["x"], 0, D)], [p["n1g"], p["sc1"], p["sh1"]], [dh], [F32], add={0: dx1},
        name=f"normmod1_bwd_{l}")
    small["mod"] = jnp.concatenate([dsh1, dsc1, dg1, dsh2, dsc2, dg2], axis=1)
    return dx, small


def kernel(x, c, mod_w, mod_b, norm1_g, w_in, gate_b, conv_w, conv_b, conv_ln_g, conv_ln_b, w_conv_proj, hgrn_lb, hgrn_norm_g, w_hgrn_proj, sb_qn_g, sb_kn_g, w_sb_proj, w_out, norm2_g, mlp_w1, mlp_w2, loss_target, m_mod_w, m_mod_b, m_norm1_g, m_w_in, m_gate_b, m_conv_w, m_conv_b, m_conv_ln_g, m_conv_ln_b, m_w_conv_proj, m_hgrn_lb, m_hgrn_norm_g, m_w_hgrn_proj, m_sb_qn_g, m_sb_kn_g, m_w_sb_proj, m_w_out, m_norm2_g, m_mlp_w1, m_mlp_w2, v_mod_w, v_mod_b, v_norm1_g, v_w_in, v_gate_b, v_conv_w, v_conv_b, v_conv_ln_g, v_conv_ln_b, v_w_conv_proj, v_hgrn_lb, v_hgrn_norm_g, v_w_hgrn_proj, v_sb_qn_g, v_sb_kn_g, v_w_sb_proj, v_w_out, v_norm2_g, v_mlp_w1, v_mlp_w2):
    given = dict(locals())
    wts = {n: given[n] for n in _WEIGHTS}
    mom = {n: given["m_" + n] for n in _WEIGHTS}
    var = {n: given["v_" + n] for n in _WEIGHTS}
    n_layers, D = norm1_g.shape
    xi, yi, ci = _place()
    q = 2 * xi + yi
    me = 4 * xi + 2 * yi + ci
    place = jnp.stack([q, ci]).astype(jnp.int32)
    n_mod = mod_w.shape[2]
    cw = conv_w.shape[2]

    pk1 = _Pack([("c", c), ("conv_w", conv_w)])
    got1 = _all_gather_small(pk1.array, name="gather_cond")
    c_act = jax.nn.silu(pk1.get(got1, "c")[:, 0, :])
    conv_full = jnp.concatenate([pk1.get(got1, "conv_w")[2 * k] for k in range(4)], axis=-1)

    mod_cols = []
    for l in range(n_layers):
        mb = lax.dynamic_slice_in_dim(mod_b[l], q * n_mod, n_mod)
        mod_cols.append(_matmul(c_act, mod_w, bl=l, name=f"mod_fwd_{l}") + mb[None, :])
    got2 = _all_gather_small(jnp.concatenate(mod_cols, axis=0), name="gather_mod")
    mods = []
    for l in range(n_layers):
        row = lax.dynamic_index_in_dim(got2[0::2], l * 8 + me, axis=1, keepdims=False)
        mods.append(jnp.split(row.reshape(1, 4 * n_mod), 6, axis=1))

    lower, lower_vjp = jax.vjp(_lower_bounds, hgrn_lb)

    shards = [wts[n].astype(BF16) for n, _ in _BIG]
    kinds = [k for _, k in _BIG]
    index = {n: k for k, (n, _) in enumerate(_BIG)}
    first = ["w_in", "w_conv_proj", "w_hgrn_proj", "w_sb_proj"]
    later = ["w_out", "mlp_w1", "mlp_w2"]
    plan = [[(index[n], 0) for n in first]]
    for l in range(n_layers):
        nxt = [(index[n], l + 1) for n in first] if l + 1 < n_layers else []
        plan.append([(index[n], l) for n in later] + nxt)
    gathers = [_Gather(shards, kinds, items) for items in plan]
    w = {(_BIG[k][0], layer): arr
         for (k, layer), arr in zip(plan[0], _run_comm(gathers[0], name="gather_first_weights"))}

    def layer_params(l):
        sh1, sc1, g1, sh2, sc2, g2 = mods[l]
        return dict(sh1=sh1, sc1=sc1, g1=g1, sh2=sh2, sc2=sc2, g2=g2, n1g=norm1_g[l][None], n2g=norm2_g[l][None],
                    gate_b=gate_b[l][None], conv_b=conv_b[l][None], lng=conv_ln_g[l][None], lnb=conv_ln_b[l][None],
                    w32=jnp.pad(conv_full[l], ((0, CONV_HALO - CONV_WIDTH), (0, 0))), lbk=(1.0 - lower[l])[None],
                    ng=hgrn_norm_g[l][None], qg=jnp.tile(sb_qn_g[l][None], (1, SB_PAIR // SB_DH)),
                    kg=jnp.tile(sb_kn_g[l][None], (1, SB_PAIR // SB_DH)))

    params = [layer_params(l) for l in range(n_layers)]
    act, saved = x[0], []
    for l in range(n_layers):
        act, r = _layer_fwd(act, w, params[l], l, comm=gathers[l + 1])
        saved.append(r)
    dact, loss_lanes = _loss_head(act, loss_target[0], name="loss_head")

    grads, smalls, reduced = {}, [None] * n_layers, {}

    def reduce_scatter(items, tag):
        return _ReduceScatter([grads[_BIG[k][0], layer] for k, layer in items], [kinds[k] for k, _ in items],
                              [shards[k].shape[1:] for k, _ in items], place, tag)

    def carried(l):
        items = [(k, l + 1) for k in range(len(_BIG))] + [(k, l) for k, (n, _) in enumerate(_BIG) if n != "w_in"]
        box = {}

        def carry(stage, got):
            if stage == 0:
                box["rs"] = reduce_scatter(items, f"l{l}")
                return box["rs"].swap()
            if stage == 1:
                box["rs"].pair_sums(got)
                return box["rs"].scatter()
            if stage == 2:
                box["rs"].quarter_sums(got)
                return box["rs"].share()
            reduced.update(zip(items, got))

        return carry

    for l in reversed(range(n_layers)):
        dact, smalls[l] = _layer_bwd(dact, saved[l], w, params[l], l, grads, carried(l) if l + 1 < n_layers else None)
    grad_x = dact[None]
    rest = [(k, l) for l in range(n_layers) for k in range(len(_BIG)) if (k, l) not in reduced]
    reduced.update(zip(rest, reduce_scatter(rest, "last").run()))

    stack = lambda k: jnp.stack([smalls[l][k] for l in range(n_layers)])
    (d_hgrn_lb,) = lower_vjp(stack("lower")[:, 0, :])
    items = [("loss", loss_lanes), ("mod", stack("mod")), ("hgrn_lb", d_hgrn_lb), ("conv_w", stack("conv_w"))]
    items += [(k, stack(k)) for k in ("norm1_g", "gate_b", "conv_b", "conv_ln_g", "conv_ln_b", "hgrn_norm_g", "sb_qn_g",
                                      "sb_kn_g", "norm2_g")]
    pk3 = _Pack(items)
    got3 = _all_gather_small(pk3.array, name="gather_small_grads")
    tot3 = _sum8(got3, name="sum_small_grads")
    loss = (0.5 / D) * jnp.sum(pk3.get(tot3, "loss"))
    g = {k: pk3.get(tot3, k).reshape(wts[k].shape) for k in _REPLICATED if k != "mod_b"}
    g["mod_b"] = pk3.get(tot3, "mod")[:, 0, :]
    g["conv_w"] = lax.dynamic_slice_in_dim(pk3.get(tot3, "conv_w"), q * cw, cw, axis=2)
    dmod_all = pk3.get(got3, "mod")[:, :, 0, :]
    g_mod_w = None
    for l in range(n_layers):
        cols = lax.dynamic_slice_in_dim(dmod_all[:, l, :], q * n_mod, n_mod, axis=1)
        g_mod_w = _matmul(c_act, cols, ta=True, layer=l, n_layers=n_layers, into=g_mod_w, name=f"mod_dw_{l}")
    g["mod_w"] = g_mod_w

    delta, new_m, new_v = {}, {}, {}
    for n, _ in _BIG:
        outs = None
        for l in reversed(range(n_layers)):
            outs = _adamw_layer(wts[n], reduced[index[n], l], mom[n], var[n], l, outs, name=f"adamw_{n}_{l}")
        g[n], delta[n], new_m[n], new_v[n] = outs
    two_d = lambda t: t.reshape(-1, t.shape[-1])
    outs = _adamw(two_d(mod_w), two_d(g["mod_w"]), two_d(m_mod_w), two_d(v_mod_w), name="adamw_mod_w")
    delta["mod_w"], new_m["mod_w"], new_v["mod_w"] = (t.reshape(mod_w.shape) for t in outs)
    rest = _REPLICATED + ["conv_w"]
    packs = [_Pack([(n, src[n]) for n in rest]) for src in (wts, g, mom, var)]
    outs = _adamw(*[pk.array for pk in packs], name="adamw_small")
    for n in rest:
        delta[n], new_m[n], new_v[n] = (packs[0].get(t, n) for t in outs)

    return (loss, grad_x, *[g[n] for n in _WEIGHTS], *[delta[n] for n in _WEIGHTS], *[new_m[n] for n in _WEIGHTS],
            *[new_v[n] for n in _WEIGHTS])
```

```python
import functools
import math

import jax
import jax.numpy as jnp
from jax import lax
from jax.experimental import pallas as pl
from jax.experimental.pallas import tpu as pltpu

F32 = jnp.float32
BF16 = jnp.bfloat16
MESH = pl.DeviceIdType.MESH

EPS = 1e-6
CONV_CH = 512
CONV_WIDTH = 31
CONV_HALO = 32
HG_HEADS = 4
HG_D = 128
HG_CHUNK = 64
HG_SUB = 16
SB_HEADS = 8
SB_DH = 64
SB_BLK = 128
SB_PAIR = 128
SB_SKIP = -104.0
OFF_CONV, OFF_HG, OFF_SB, OFF_GL = 0, 1024, 3072, 4608
ADAM_LR, ADAM_B1, ADAM_B2, ADAM_EPS, ADAM_WD, ADAM_STEP = 0.001, 0.9, 0.999, 1e-08, 0.01, 10
VMEM_LIMIT_BYTES = 56 * 1024 * 1024
ROW_TILE = 256


def _cparams(sem=None, **kw):
    return pltpu.CompilerParams(dimension_semantics=sem, vmem_limit_bytes=VMEM_LIMIT_BYTES, **kw)


def _pick(n, cands):
    for c in cands:
        if n % c == 0:
            return c
    return n


MATMUL_VMEM_BUDGET = 40 * 1024 * 1024


def _tile_options(n, cap):
    opts = [t for t in range(cap - cap % 128, 0, -128) if n % t == 0]
    return opts or [n]


def _matmul_tiles(M, N, K, size_a, size_b, size_o, in_acc):
    for tm in _tile_options(M, 1024):
        for tk in _tile_options(K, 2048):
            for tn in _tile_options(N, 1280):
                need = 2 * (tm * tk * size_a + tk * tn * size_b + tm * tn * size_o)
                if K > tk and not in_acc:
                    need += tm * tn * 4
                if need <= MATMUL_VMEM_BUDGET:
                    return tm, tn, tk
    raise ValueError(f"no matmul tiling fits VMEM for {(M, N, K)}")
def _matmul(a, b, *, ta=False, tb=False, bl=None, out_dtype=F32, name, into=None, layer=None, n_layers=None,
            post=None, extras=(), out_dtypes=None):
    M, K = (a.shape[1], a.shape[0]) if ta else a.shape
    N = b.shape[-2] if tb else b.shape[-1]
    if post is not None:
        return _matmul_post(a, b, M, N, K, ta, tb, post, extras, out_dtypes, name)
    in_acc = jnp.dtype(out_dtype) == jnp.dtype(F32)
    tm, tn, tk = _matmul_tiles(M, N, K, a.dtype.itemsize, b.dtype.itemsize, jnp.dtype(out_dtype).itemsize, in_acc)
    nk = K // tk
    a_spec = pl.BlockSpec((tk, tm), lambda i, j, k: (k, i)) if ta else pl.BlockSpec((tm, tk), lambda i, j, k: (i, k))
    if bl is None:
        b_spec = pl.BlockSpec((tn, tk), lambda i, j, k: (j, k)) if tb else pl.BlockSpec((tk, tn), lambda i, j, k: (k, j))
    elif tb:
        b_spec = pl.BlockSpec((None, tn, tk), lambda i, j, k: (bl, j, k))
    else:
        b_spec = pl.BlockSpec((None, tk, tn), lambda i, j, k: (bl, k, j))
    dn = (((0 if ta else 1,), (1 if tb else 0,)), ((), ()))

    use_scratch = nk > 1 and not in_acc

    def kern(a_ref, b_ref, *rest):
        o_ref = rest[-2] if use_scratch else rest[-1]
        prod = lambda: lax.dot_general(a_ref[...].astype(BF16), b_ref[...].astype(BF16), dn,
                                       preferred_element_type=F32)
        if nk == 1:
            o_ref[...] = prod().astype(o_ref.dtype).reshape(o_ref.shape)
            return
        acc_ref = rest[-1] if use_scratch else o_ref
        k = pl.program_id(2)

        @pl.when(k == 0)
        def _():
            acc_ref[...] = prod().reshape(acc_ref.shape)

        @pl.when(k > 0)
        def _():
            acc_ref[...] += prod().reshape(acc_ref.shape)

        if use_scratch:
            @pl.when(k == nk - 1)
            def _():
                o_ref[...] = acc_ref[...].astype(o_ref.dtype).reshape(o_ref.shape)

    in_specs, args, aliases = [a_spec, b_spec], [a, b], {}
    if layer is None:
        out_shape = jax.ShapeDtypeStruct((M, N), out_dtype)
        out_spec = pl.BlockSpec((tm, tn), lambda i, j, k: (i, j))
    else:
        out_shape = jax.ShapeDtypeStruct((n_layers, M, N), out_dtype)
        out_spec = pl.BlockSpec((1, tm, tn), lambda i, j, k: (layer, i, j))
        if into is not None:
            in_specs.append(pl.BlockSpec(memory_space=pl.ANY))
            args.append(into)
            aliases = {2: 0}
    return pl.pallas_call(
        kern, grid=(M // tm, N // tn, nk), in_specs=in_specs, out_specs=out_spec, out_shape=out_shape,
        scratch_shapes=[pltpu.VMEM((tm, tn), F32)] if use_scratch else [],
        input_output_aliases=aliases, name=name,
        compiler_params=_cparams(("parallel", "parallel", "arbitrary")))(*args)


def _matmul_post(a, b, M, N, K, ta, tb, post, extras, out_dtypes, name):
    per_elem = sum(e.dtype.itemsize for e in extras) + sum(jnp.dtype(d).itemsize for d in out_dtypes)
    tm, tn, tk = _matmul_tiles(M, N, K, a.dtype.itemsize, b.dtype.itemsize, per_elem, True)
    assert tk == K, "the epilogue form keeps the contraction in one block"
    a_spec = pl.BlockSpec((K, tm), lambda i, j: (0, i)) if ta else pl.BlockSpec((tm, K), lambda i, j: (i, 0))
    b_spec = pl.BlockSpec((tn, K), lambda i, j: (j, 0)) if tb else pl.BlockSpec((K, tn), lambda i, j: (0, j))
    tile = pl.BlockSpec((tm, tn), lambda i, j: (i, j))
    dn = (((0 if ta else 1,), (1 if tb else 0,)), ((), ()))
    n_ex = len(extras)

    def kern(a_ref, b_ref, *rest):
        prod = lax.dot_general(a_ref[...].astype(BF16), b_ref[...].astype(BF16), dn, preferred_element_type=F32)
        res = post(prod, *[r[...].astype(F32) for r in rest[:n_ex]])
        for val, o_ref in zip(res, rest[n_ex:]):
            o_ref[...] = val.astype(o_ref.dtype)

    return pl.pallas_call(
        kern, grid=(M // tm, N // tn), in_specs=[a_spec, b_spec] + [tile] * n_ex, out_specs=[tile] * len(out_dtypes),
        out_shape=[jax.ShapeDtypeStruct((M, N), d) for d in out_dtypes], name=name,
        compiler_params=_cparams(("parallel", "parallel")))(a, b, *extras)


def _col_specs(off, width, T):
    bw = math.gcd(width, off) if off else width
    return [pl.BlockSpec((T, bw), functools.partial(lambda i, c: (i, c), c=off // bw + p)) for p in range(width // bw)]


def _gather_rows(refs, counts):
    vals, pos = [], 0
    for n in counts:
        parts = [refs[pos + p][...].astype(F32) for p in range(n)]
        pos += n
        vals.append(parts[0] if n == 1 else jnp.concatenate(parts, axis=1))
    return vals, pos


def _rowop(fn, ins, params, outs, *, name):
    S = ins[0][0].shape[0]
    T = min(ROW_TILE, S)
    in_specs, counts, args = [], [], []
    for arr, off, width in ins:
        sp = _col_specs(off, width, T)
        in_specs += sp
        counts.append(len(sp))
        args += [arr] * len(sp)
    in_specs += [pl.BlockSpec(p.shape, lambda i: (0, 0)) for p in params]

    def kern(*refs):
        vals, pos = _gather_rows(refs, counts)
        pv = [refs[pos + p][...] for p in range(len(params))]
        pos += len(params)
        res = fn(*vals, *pv)
        for r, o_ref in zip(res, refs[pos:]):
            o_ref[...] = r.astype(o_ref.dtype)

    return pl.pallas_call(
        kern, grid=(S // T,), in_specs=in_specs,
        out_specs=[pl.BlockSpec((T, w), lambda i: (i, 0)) for w, _ in outs],
        out_shape=[jax.ShapeDtypeStruct((S, w), dt) for w, dt in outs],
        name=name, compiler_params=_cparams(("parallel",)))(*args, *params)


def _rowop_bwd(fn, ins, params, douts, din_dtypes, *, name, add=None):
    add = add or {}
    S = ins[0][0].shape[0]
    T = min(ROW_TILE, S)
    in_specs, counts, args = [], [], []
    for arr, off, width in ins:
        sp = _col_specs(off, width, T)
        in_specs += sp
        counts.append(len(sp))
        args += [arr] * len(sp)
    in_specs += [pl.BlockSpec(p.shape, lambda i: (0, 0)) for p in params]
    in_specs += [pl.BlockSpec((T, d.shape[1]), lambda i: (i, 0)) for d in douts]
    add_keys = sorted(add)
    in_specs += [pl.BlockSpec((T, add[k].shape[1]), lambda i: (i, 0)) for k in add_keys]
    want = [k for k, dt in enumerate(din_dtypes) if dt is not None]

    def kern(*refs):
        vals, pos = _gather_rows(refs, counts)
        pv = [refs[pos + p][...] for p in range(len(params))]
        pos += len(params)
        cts = [refs[pos + p][...].astype(F32) for p in range(len(douts))]
        pos += len(douts)
        adds = {k: refs[pos + p][...].astype(F32) for p, k in enumerate(add_keys)}
        pos += len(add_keys)
        _, vjp = jax.vjp(fn, *vals, *pv)
        grads = vjp(tuple(cts))
        for k in want:
            g = grads[k] + adds[k] if k in adds else grads[k]
            refs[pos][...] = g.astype(refs[pos].dtype)
            pos += 1
        first = pl.program_id(0) == 0
        for p in range(len(params)):
            gp, o_ref = grads[len(ins) + p], refs[pos + p]

            @pl.when(first)
            def _(gp=gp, o_ref=o_ref):
                o_ref[...] = gp

            @pl.when(jnp.logical_not(first))
            def _(gp=gp, o_ref=o_ref):
                o_ref[...] += gp

    out_specs = [pl.BlockSpec((T, ins[k][2]), lambda i: (i, 0)) for k in want]
    out_specs += [pl.BlockSpec(p.shape, lambda i: (0, 0)) for p in params]
    out_shape = [jax.ShapeDtypeStruct((S, ins[k][2]), din_dtypes[k]) for k in want]
    out_shape += [jax.ShapeDtypeStruct(p.shape, F32) for p in params]
    res = pl.pallas_call(
        kern, grid=(S // T,), in_specs=in_specs, out_specs=out_specs, out_shape=out_shape,
        name=name, compiler_params=_cparams(("arbitrary",)))(*args, *params, *douts, *[add[k] for k in add_keys])
    dins = [None] * len(ins)
    for p, k in enumerate(want):
        dins[k] = res[p]
    return dins, list(res[len(want):])


def _rms(x, g):
    return x * lax.rsqrt(jnp.mean(x * x, axis=-1, keepdims=True) + EPS) * g


def _fn_normmod(x, g, sc, sh):
    return (_rms(x, g) * (1.0 + sc) + sh,)


def _fn_lnsilu(c, g, b):
    mu = jnp.mean(c, axis=-1, keepdims=True)
    var = jnp.mean(jnp.square(c - mu), axis=-1, keepdims=True)
    y = (c - mu) * lax.rsqrt(var + EPS) * g + b
    return (y * jax.nn.sigmoid(y),)


def _fn_merge(gl, yc, yh, ys, gb):
    d = yc.shape[1]
    g = jax.nn.sigmoid(gl + gb)
    return (g[:, :d] * yc + g[:, d:2 * d] * yh + g[:, 2 * d:] * ys,)


def _fn_resid(x, y, g):
    return (x + g * y,)


def _fn_scale(y, g):
    return (g * y,)


def _fn_relu2(u):
    return (jnp.square(jnp.maximum(u, 0.0)),)


def _conv_specs(S, T):
    r = T // CONV_HALO
    cur = [pl.BlockSpec((T, CONV_CH), lambda i: (i, 0)), pl.BlockSpec((T, CONV_CH), lambda i: (i, 1))]
    prev = [pl.BlockSpec((CONV_HALO, CONV_CH), lambda i: (jnp.maximum(i * r - 1, 0), 0)),
            pl.BlockSpec((CONV_HALO, CONV_CH), lambda i: (jnp.maximum(i * r - 1, 0), 1))]
    return cur + prev


def _glu_ext(a_ref, g_ref, ah_ref, gh_ref):
    a = a_ref[...]
    sg = jax.nn.sigmoid(g_ref[...])
    uh = jnp.where(pl.program_id(0) > 0, ah_ref[...] * jax.nn.sigmoid(gh_ref[...]), 0.0)
    return a, sg, jnp.concatenate([uh, a * sg], axis=0)


def _shift_up(xe, k, T):
    return xe[:T] if k == 0 else pltpu.roll(xe, shift=xe.shape[0] - k, axis=0)[:T]


def _conv_fwd(proj, w32, b, *, name):
    S = proj.shape[0]
    T = min(ROW_TILE, S)
    lead = CONV_HALO - (CONV_WIDTH - 1)

    def kern(a_ref, g_ref, ah_ref, gh_ref, w_ref, b_ref, o_ref):
        _, _, ue = _glu_ext(a_ref, g_ref, ah_ref, gh_ref)
        acc = jnp.zeros((T, CONV_CH), F32) + b_ref[...]
        for j in range(CONV_WIDTH):
            acc = acc + w_ref[j:j + 1, :] * _shift_up(ue, lead + j, T)
        o_ref[...] = acc

    const = lambda shape: pl.BlockSpec(shape, lambda i: (0, 0))
    return pl.pallas_call(
        kern, grid=(S // T,), in_specs=_conv_specs(S, T) + [const(w32.shape), const(b.shape)],
        out_specs=pl.BlockSpec((T, CONV_CH), lambda i: (i, 0)),
        out_shape=jax.ShapeDtypeStruct((S, CONV_CH), F32), name=name,
        compiler_params=_cparams(("parallel",)))(proj, proj, proj, proj, w32, b)


def _conv_bwd(proj, dc, w32, *, name, comm=None):
    S = proj.shape[0]
    T = min(ROW_TILE, S)
    nt = S // T
    r = T // CONV_HALO
    lead = CONV_HALO - (CONV_WIDTH - 1)
    last_halo = S // CONV_HALO - 1

    def kern(a_ref, g_ref, ah_ref, gh_ref, dc_ref, dcn_ref, w_ref, dag_ref, dw_ref, db_ref):
        i = pl.program_id(0)
        a, sg, ue = _glu_ext(a_ref, g_ref, ah_ref, gh_ref)
        dc_t = dc_ref[...]
        de = jnp.concatenate([dc_t, jnp.where(i < nt - 1, dcn_ref[...], 0.0)], axis=0)

        @pl.when(i == 0)
        def _():
            dw_ref[...] = jnp.zeros_like(dw_ref)
            db_ref[...] = jnp.zeros_like(db_ref)

        du = jnp.zeros((T, CONV_CH), F32)
        for j in range(CONV_WIDTH):
            du = du + w_ref[j:j + 1, :] * _shift_up(de, CONV_WIDTH - 1 - j, T)
            dw_ref[j:j + 1, :] += jnp.sum(dc_t * _shift_up(ue, lead + j, T), axis=0, keepdims=True)
        db_ref[...] += jnp.sum(dc_t, axis=0, keepdims=True)
        dag_ref[:, :CONV_CH] = (du * sg).astype(BF16)
        dag_ref[:, CONV_CH:] = (du * a * sg * (1.0 - sg)).astype(BF16)

    const = lambda shape: pl.BlockSpec(shape, lambda i: (0, 0))
    in_specs = _conv_specs(S, T) + [
        pl.BlockSpec((T, CONV_CH), lambda i: (i, 0)),
        pl.BlockSpec((CONV_HALO, CONV_CH), lambda i: (jnp.minimum((i + 1) * r, last_halo), 0)),
        const(w32.shape)]
    return _hosted_call(
        kern, grid=(nt,), in_specs=in_specs,
        out_specs=[pl.BlockSpec((T, 2 * CONV_CH), lambda i: (i, 0)), const(w32.shape), const((1, CONV_CH))],
        out_shape=[jax.ShapeDtypeStruct((S, 2 * CONV_CH), BF16), jax.ShapeDtypeStruct(w32.shape, F32),
                   jax.ShapeDtypeStruct((1, CONV_CH), F32)],
        scratch_shapes=[], args=[proj, proj, proj, proj, dc, dc, w32], name=name, comm=comm, sem=("arbitrary",))


def _iota2(shape, dim):
    return lax.broadcasted_iota(jnp.int32, shape, dim)


def _running(x, seg, later):
    n = x.shape[0]
    pos = _iota2(x.shape, 0) & (seg - 1)
    k = 1
    while k < seg:
        if later:
            x = x + jnp.where(pos < seg - k, pltpu.roll(x, n - k, axis=0), 0.0)
        else:
            x = x + jnp.where(pos >= k, pltpu.roll(x, k, axis=0), 0.0)
        k *= 2
    return x


@functools.partial(jax.custom_vjp, nondiff_argnums=(1,))
def _prefix(x, seg):
    return _running(x, seg, False)


_prefix.defvjp(lambda x, seg: (_running(x, seg, False), None), lambda seg, _, g: (_running(g, seg, True),))


def _hg_chunk(q, f, iv, g, st, lbk, ng):
    n, sub = HG_CHUNK, HG_SUB
    kk = lbk * jax.nn.sigmoid(-f)
    lf = jnp.log(1.0 - kk)
    b = _prefix(lf, n)
    bs = _prefix(lf, sub)
    bt = jnp.sum(lf, axis=0, keepdims=True)
    qh = q * jax.nn.sigmoid(q)
    dot_nt = lambda x, y: lax.dot_general(x.astype(BF16), y.astype(BF16), (((1,), (1,)), ((), ())),
                                          preferred_element_type=F32)
    o = dot_nt(qh * jnp.exp(b), st)
    b0 = b - bs
    qs = qh * jnp.exp(bs)
    col = _iota2((sub, n), 1)
    rows = []
    for blk in range(n // sub):
        lo = blk * sub
        sl = slice(lo, lo + sub)
        acc = o[sl]
        if blk > 0:
            ref = jnp.concatenate([b0[sl]] * (n // sub), axis=0)
            kd = kk * jnp.exp(jnp.minimum(ref - b, 0.0))
            sc = jnp.where(col < lo, dot_nt(qs[sl], kd), 0.0)
            acc = acc + jnp.dot(sc.astype(BF16), iv.astype(BF16), preferred_element_type=F32)
        bq, bk = bs[sl][None, :, :], bs[sl][:, None, :]
        s_i = lax.broadcasted_iota(jnp.int32, (sub, sub, HG_D), 0)
        t_i = lax.broadcasted_iota(jnp.int32, (sub, sub, HG_D), 1)
        keep = s_i <= t_i
        p = jnp.where(keep, qh[sl][None, :, :] * kk[sl][:, None, :] * jnp.exp(jnp.where(keep, bq - bk, 0.0)), 0.0)
        w = jnp.sum(p, axis=-1, keepdims=True)
        acc = acc + jnp.sum(w * iv[sl][:, None, :], axis=0)
        rows.append(acc)
    o = jnp.concatenate(rows, axis=0)
    kd = kk * jnp.exp(bt - b)
    st_new = jnp.exp(bt) * st + lax.dot_general(iv.astype(BF16), kd.astype(BF16), (((0,), (0,)), ((), ())),
                                                     preferred_element_type=F32)
    out = _rms(o, ng) * (g * jax.nn.sigmoid(g))
    return out, st_new


def _hg_tile(S):
    return min(512, S)


def _hg_in_specs(rt, rev, nr):
    width = HG_HEADS * HG_D
    base = OFF_HG // width
    row = (lambda r: nr - 1 - r) if rev else (lambda r: r)
    return [pl.BlockSpec((rt, width), functools.partial(lambda r, k: (row(r), base + k), k=k)) for k in range(4)]


def _hg_cols(h):
    return slice(h * HG_D, (h + 1) * HG_D)


def _hgrn_fwd(proj, lbk, ng, *, name, comm=None):
    S = proj.shape[0]
    rt = _hg_tile(S)
    nr, nc = S // rt, rt // HG_CHUNK

    def kern(q_ref, f_ref, i_ref, g_ref, lbk_ref, ng_ref, o_ref, st_out_ref, st_ref):
        @pl.when(pl.program_id(0) == 0)
        def _():
            st_ref[...] = jnp.zeros_like(st_ref)

        def body(c, carry):
            rows = pl.ds(pl.multiple_of(c * HG_CHUNK, HG_CHUNK), HG_CHUNK)
            for h in range(HG_HEADS):
                cols = _hg_cols(h)
                st = st_ref[h]
                st_out_ref[h, c] = st
                out, st_new = _hg_chunk(q_ref[rows, cols], f_ref[rows, cols], i_ref[rows, cols], g_ref[rows, cols], st,
                                        lbk_ref[:, cols], ng_ref[...])
                o_ref[rows, cols] = out.astype(o_ref.dtype)
                st_ref[h] = st_new
            return carry

        lax.fori_loop(0, nc, body, 0)

    width = HG_HEADS * HG_D
    in_specs = _hg_in_specs(rt, False, nr) + [pl.BlockSpec((1, width), lambda r: (0, 0)),
                                               pl.BlockSpec((1, HG_D), lambda r: (0, 0))]
    return _hosted_call(
        kern, grid=(nr,), in_specs=in_specs,
        out_specs=[pl.BlockSpec((rt, width), lambda r: (r, 0)),
                   pl.BlockSpec((HG_HEADS, nc, HG_D, HG_D), lambda r: (0, r, 0, 0))],
        out_shape=[jax.ShapeDtypeStruct((S, width), BF16),
                   jax.ShapeDtypeStruct((HG_HEADS, S // HG_CHUNK, HG_D, HG_D), F32)],
        scratch_shapes=[pltpu.VMEM((HG_HEADS, HG_D, HG_D), F32)],
        args=[proj, proj, proj, proj, lbk, ng], name=name, comm=comm, sem=("arbitrary",))


def _hgrn_bwd(proj, states, dout, lbk, ng, *, name, comm=None):
    S = proj.shape[0]
    rt = _hg_tile(S)
    nr, nc = S // rt, rt // HG_CHUNK
    width = HG_HEADS * HG_D

    def kern(q_ref, f_ref, i_ref, g_ref, st_in_ref, do_ref, lbk_ref, ng_ref,
             dq_ref, df_ref, di_ref, dg_ref, dlbk_ref, dng_ref, dst_ref):
        @pl.when(pl.program_id(0) == 0)
        def _():
            dst_ref[...] = jnp.zeros_like(dst_ref)
            dlbk_ref[...] = jnp.zeros_like(dlbk_ref)
            dng_ref[...] = jnp.zeros_like(dng_ref)

        def body(k, carry):
            c = nc - 1 - k
            rows = pl.ds(pl.multiple_of(c * HG_CHUNK, HG_CHUNK), HG_CHUNK)
            for h in range(HG_HEADS):
                cols = _hg_cols(h)
                _, vjp = jax.vjp(_hg_chunk, q_ref[rows, cols], f_ref[rows, cols], i_ref[rows, cols], g_ref[rows, cols],
                                 st_in_ref[h, c], lbk_ref[:, cols], ng_ref[...])
                dq, df, di, dg, dst, dlbk, dng = vjp((do_ref[rows, cols].astype(F32), dst_ref[h]))
                dq_ref[rows, cols] = dq.astype(BF16)
                df_ref[rows, cols] = df.astype(BF16)
                di_ref[rows, cols] = di.astype(BF16)
                dg_ref[rows, cols] = dg.astype(BF16)
                dst_ref[h] = dst
                dlbk_ref[:, cols] += dlbk
                dng_ref[h] += dng
            return carry

        lax.fori_loop(0, nc, body, 0)

    rev = lambda r: nr - 1 - r
    tile = pl.BlockSpec((rt, width), lambda r: (rev(r), 0))
    in_specs = _hg_in_specs(rt, True, nr) + [
        pl.BlockSpec((HG_HEADS, nc, HG_D, HG_D), lambda r: (0, rev(r), 0, 0)), tile,
        pl.BlockSpec((1, width), lambda r: (0, 0)), pl.BlockSpec((1, HG_D), lambda r: (0, 0))]
    return _hosted_call(
        kern, grid=(nr,), in_specs=in_specs,
        out_specs=[tile, tile, tile, tile, pl.BlockSpec((1, width), lambda r: (0, 0)),
                   pl.BlockSpec((HG_HEADS, 1, HG_D), lambda r: (0, 0, 0))],
        out_shape=[jax.ShapeDtypeStruct((S, width), BF16)] * 4 + [
            jax.ShapeDtypeStruct((1, width), F32), jax.ShapeDtypeStruct((HG_HEADS, 1, HG_D), F32)],
        scratch_shapes=[pltpu.VMEM((HG_HEADS, HG_D, HG_D), F32)],
        args=[proj, proj, proj, proj, states, dout, lbk, ng], name=name, comm=comm, sem=("arbitrary",))


def _sb_scores(km, qi):
    return lax.dot_general(km, qi, (((1,), (1,)), ((), ())), preferred_element_type=F32)


def _sb_weights(zt, r_run, diag):
    n = SB_BLK
    sp = jnp.maximum(zt, 0.0) + jnp.log(1.0 + jnp.exp(-jnp.abs(zt)))
    lk = -sp
    if diag:
        keep = (_iota2(zt.shape, 0) & (n - 1)) < _iota2(zt.shape, 1)
        lk = jnp.where(keep, lk, 0.0)
    tails = [_running(lk[a * n:(a + 1) * n], n, True) for a in range(2)]
    between = jnp.concatenate([tails[a] + r_run[a] for a in range(2)], axis=0)
    wgt = jnp.exp(zt + between)
    if diag:
        wgt = jnp.where(keep, wgt, 0.0)
    return sp, wgt, [t[0:1, :] for t in tails]


def _sb_norm_pair(x, g2, lane_lo):
    sq = x * x
    ms_lo = jnp.sum(jnp.where(lane_lo, sq, 0.0), axis=-1, keepdims=True)
    ms_hi = jnp.sum(jnp.where(lane_lo, 0.0, sq), axis=-1, keepdims=True)
    return x * lax.rsqrt(jnp.where(lane_lo, ms_lo, ms_hi) * (1.0 / SB_DH) + EPS) * g2


def _sb_specs(S):
    base = OFF_SB // SB_PAIR
    per = SB_HEADS * SB_DH // SB_PAIR
    cols = [pl.BlockSpec((S, SB_PAIR), functools.partial(lambda p, k: (0, base + per * k + p), k=k)) for k in range(3)]
    return cols + [pl.BlockSpec((1, SB_PAIR), lambda p: (0, 0))] * 2


def _sb_rows(i):
    return pl.ds(pl.multiple_of(i * SB_BLK, SB_BLK), SB_BLK)


def _sb_both(j, a=None):
    if a is None:
        return pl.ds(pl.multiple_of(j * 2 * SB_BLK, 2 * SB_BLK), 2 * SB_BLK)
    return pl.ds(pl.multiple_of(j * 2 * SB_BLK + a * SB_BLK, SB_BLK), SB_BLK)


def _sb_fwd(proj, qg2, kg2, *, name, comm=None):
    S = proj.shape[0]
    nb = S // SB_BLK
    scale = SB_DH ** -0.5
    n_pairs = SB_HEADS * SB_DH // SB_PAIR

    def kern(q_ref, k_ref, v_ref, qg_ref, kg_ref, o_ref, rs_ref, qp_ref, km_ref, vt_ref):
        lane_lo = _iota2((SB_BLK, SB_PAIR), 1) < SB_DH

        def prologue(j, carry):
            rows = _sb_rows(j)
            qp_ref[rows, :] = (_sb_norm_pair(q_ref[rows, :], qg_ref[...], lane_lo) * scale).astype(BF16)
            kn = _sb_norm_pair(k_ref[rows, :], kg_ref[...], lane_lo)
            v = v_ref[rows, :]
            for a, mine in enumerate((lane_lo, jnp.logical_not(lane_lo))):
                km_ref[_sb_both(j, a), :] = jnp.where(mine, kn, 0.0).astype(BF16)
                vt_ref[:, _sb_both(j, a)] = jnp.where(mine, v, 0.0).T.astype(BF16)
            return carry

        lax.fori_loop(0, nb, prologue, 0)

        def qblock(i, carry):
            qi = qp_ref[_sb_rows(i), :]

            scores = lambda j: _sb_scores(km_ref[_sb_both(jnp.maximum(j, 0)), :], qi)
            output = lambda j, wgt: jnp.dot(vt_ref[:, _sb_both(j)], wgt, preferred_element_type=F32)

            def note(j, r_run):
                for a in range(2):
                    rs_ref[a, i, pl.ds(j, 1), :] = r_run[a]
                return jnp.maximum(jnp.max(r_run[0]), jnp.max(r_run[1])) > SB_SKIP

            def noted(j, r_run):
                return lax.cond(j >= 0, lambda: note(j, r_run).astype(jnp.int32), lambda: jnp.int32(0))

            zero = jnp.zeros((1, SB_BLK), F32)
            zt, z_next = scores(i), scores(i - 1)
            _, wgt, r_run = _sb_weights(zt, [zero, zero], True)
            go = noted(i - 1, r_run)

            def body(c):
                j, _, acc, r_run, zt, j_prev, w_prev = c
                z_next = scores(j - 1)
                acc = acc + output(j_prev, w_prev)
                _, wgt, lk_sum = _sb_weights(zt, r_run, False)
                r_run = [r_run[a] + lk_sum[a] for a in range(2)]
                return j - 1, noted(j - 1, r_run), acc, r_run, z_next, j, wgt.astype(BF16)

            c = (i - 1, go, jnp.zeros((SB_PAIR, SB_BLK), F32), r_run, z_next, i, wgt.astype(BF16))
            _, _, acc, _, _, j_prev, w_prev = lax.while_loop(lambda c: c[1] > 0, body, c)
            o_ref[_sb_rows(i), :] = (acc + output(j_prev, w_prev)).T.astype(o_ref.dtype)
            return carry

        lax.fori_loop(0, nb, qblock, 0)

    width = SB_HEADS * SB_DH
    return _hosted_call(
        kern, grid=(n_pairs,), in_specs=_sb_specs(S),
        out_specs=[pl.BlockSpec((S, SB_PAIR), lambda p: (0, p)),
                   pl.BlockSpec((2, nb, nb, SB_BLK), lambda p: (p, 0, 0, 0))],
        out_shape=[jax.ShapeDtypeStruct((S, width), BF16), jax.ShapeDtypeStruct((SB_HEADS, nb, nb, SB_BLK), F32)],
        scratch_shapes=[pltpu.VMEM((S, SB_PAIR), BF16), pltpu.VMEM((2 * S, SB_PAIR), BF16),
                        pltpu.VMEM((SB_PAIR, 2 * S), BF16)],
        args=[proj, proj, proj, qg2, kg2], name=name, comm=comm, sem=("parallel",))


def _sb_bwd(proj, qg2, kg2, rs, do, *, name, comm=None):
    S = proj.shape[0]
    nb = S // SB_BLK
    scale = SB_DH ** -0.5
    n_pairs = SB_HEADS * SB_DH // SB_PAIR

    def kern(q_ref, k_ref, v_ref, qg_ref, kg_ref, rs_ref, do_ref, dq_ref, dk_ref, dv_ref, dqg_ref, dkg_ref,
             qp_ref, km_ref, kt_ref, vm_ref, dqn_ref, dkn_ref, dvs_ref):
        lane_lo = _iota2((SB_BLK, SB_PAIR), 1) < SB_DH
        heads = (lane_lo, jnp.logical_not(lane_lo))
        fn_q = lambda x, g: _sb_norm_pair(x, g, lane_lo) * scale
        fn_k = lambda x, g: _sb_norm_pair(x, g, lane_lo)

        def prologue(j, carry):
            rows = _sb_rows(j)
            qp_ref[rows, :] = fn_q(q_ref[rows, :], qg_ref[...]).astype(BF16)
            kn = fn_k(k_ref[rows, :], kg_ref[...])
            v = v_ref[rows, :]
            for a, mine in enumerate(heads):
                k_a = jnp.where(mine, kn, 0.0)
                km_ref[_sb_both(j, a), :] = k_a.astype(BF16)
                kt_ref[:, _sb_both(j, a)] = k_a.T.astype(BF16)
                vm_ref[_sb_both(j, a), :] = jnp.where(mine, v, 0.0).astype(BF16)
            return carry

        lax.fori_loop(0, nb, prologue, 0)
        dkn_ref[...] = jnp.zeros_like(dkn_ref)
        dvs_ref[...] = jnp.zeros_like(dvs_ref)

        def qblock(i, carry):
            qi = qp_ref[_sb_rows(i), :]
            doi = do_ref[_sb_rows(i), :]

            def opening(j):
                jc = jnp.minimum(j, i)
                return (_sb_scores(km_ref[_sb_both(jc), :], qi),
                        lax.dot_general(vm_ref[_sb_both(jc), :], doi, (((1,), (1,)), ((), ())), preferred_element_type=F32))

            def closing(j, dzb, wgtb, dqa):
                dkn_ref[_sb_both(j), :] += jnp.dot(dzb, qi, preferred_element_type=F32)
                dvs_ref[_sb_both(j), :] += jnp.dot(wgtb, doi, preferred_element_type=F32)
                return dqa + jnp.dot(kt_ref[:, _sb_both(j)], dzb, preferred_element_type=F32)

            def middle(j, diag, zt, dp, e_run):
                zero = jnp.zeros((1, SB_BLK), F32)
                r_run = [zero, zero] if diag else [rs_ref[a, i, pl.ds(j, 1), :] for a in range(2)]
                sp, wgt, _ = _sb_weights(zt, r_run, diag)
                e = dp * wgt
                heads_e = [_running(e[a * SB_BLK:(a + 1) * SB_BLK], SB_BLK, False) for a in range(2)]
                e_left = jnp.concatenate([heads_e[a] + e_run[a] for a in range(2)], axis=0) - e
                s_neg = jnp.exp(-sp)
                dz = e * s_neg - e_left * (1.0 - s_neg)
                if diag:
                    dz = jnp.where((_iota2(dz.shape, 0) & (SB_BLK - 1)) < _iota2(dz.shape, 1), dz, 0.0)
                return dz.astype(BF16), wgt.astype(BF16), [e_run[a] + heads_e[a][SB_BLK - 1:SB_BLK, :] for a in range(2)]

            def live(j):
                jc = jnp.maximum(j, 0)
                top = jnp.maximum(jnp.max(rs_ref[0, i, pl.ds(jc, 1), :]), jnp.max(rs_ref[1, i, pl.ds(jc, 1), :]))
                return jnp.logical_and(j >= 0, top > SB_SKIP).astype(jnp.int32)

            first, _ = lax.while_loop(lambda c: c[1] > 0, lambda c: (c[0] - 1, live(c[0] - 2)), (i, live(i - 1)))

            def body(j, c):
                dqa, e_run, zt, dp, j_prev, dzb, wgtb = c
                nxt = opening(j + 1)
                dqa = closing(j_prev, dzb, wgtb, dqa)
                dzb, wgtb, e_run = middle(j, False, zt, dp, e_run)
                return (dqa, e_run) + nxt + (j, dzb, wgtb)

            zero = jnp.zeros((1, SB_BLK), F32)
            none = jnp.zeros((2 * SB_BLK, SB_BLK), BF16)
            c = (jnp.zeros((SB_PAIR, SB_BLK), F32), [zero, zero]) + opening(first) + (first, none, none)
            dqa, e_run, zt, dp, j_prev, dzb, wgtb = lax.fori_loop(first, i, body, c)
            dqa = closing(j_prev, dzb, wgtb, dqa)
            dzb, wgtb, _ = middle(i, True, zt, dp, e_run)
            dqn_ref[_sb_rows(i), :] = closing(i, dzb, wgtb, dqa).T
            return carry

        lax.fori_loop(0, nb, qblock, 0)
        dqg_ref[...] = jnp.zeros_like(dqg_ref)
        dkg_ref[...] = jnp.zeros_like(dkg_ref)

        def epilogue(j, carry):
            rows = _sb_rows(j)
            _, vjp_q = jax.vjp(fn_q, q_ref[rows, :], qg_ref[...])
            dq, dqg = vjp_q(dqn_ref[rows, :])
            _, vjp_k = jax.vjp(fn_k, k_ref[rows, :], kg_ref[...])
            dk, dkg = vjp_k(jnp.where(lane_lo, dkn_ref[_sb_both(j, 0), :], dkn_ref[_sb_both(j, 1), :]))
            dq_ref[rows, :] = dq.astype(BF16)
            dk_ref[rows, :] = dk.astype(BF16)
            dv_ref[rows, :] = jnp.where(lane_lo, dvs_ref[_sb_both(j, 0), :], dvs_ref[_sb_both(j, 1), :]).astype(BF16)
            dqg_ref[0] += dqg
            dkg_ref[0] += dkg
            return carry

        lax.fori_loop(0, nb, epilogue, 0)

    width = SB_HEADS * SB_DH
    pair = pl.BlockSpec((S, SB_PAIR), lambda p: (0, p))
    dgain = pl.BlockSpec((1, 1, SB_PAIR), lambda p: (p, 0, 0))
    in_specs = _sb_specs(S) + [pl.BlockSpec((2, nb, nb, SB_BLK), lambda p: (p, 0, 0, 0)), pair]
    return _hosted_call(
        kern, grid=(n_pairs,), in_specs=in_specs, out_specs=[pair, pair, pair, dgain, dgain],
        out_shape=[jax.ShapeDtypeStruct((S, width), BF16)] * 3 + [jax.ShapeDtypeStruct((n_pairs, 1, SB_PAIR), F32)] * 2,
        scratch_shapes=[pltpu.VMEM((S, SB_PAIR), BF16), pltpu.VMEM((2 * S, SB_PAIR), BF16), pltpu.VMEM((SB_PAIR, 2 * S), BF16),
                        pltpu.VMEM((2 * S, SB_PAIR), BF16), pltpu.VMEM((S, SB_PAIR), F32),
                        pltpu.VMEM((2 * S, SB_PAIR), F32), pltpu.VMEM((2 * S, SB_PAIR), F32)],
        args=[proj, proj, proj, qg2, kg2, rs, do], name=name, comm=comm, sem=("parallel",))


def _loss_head(y, target, *, name):
    S, D = y.shape
    T = min(ROW_TILE, S)

    def kern(y_ref, t_ref, dy_ref, acc_ref):
        err = y_ref[...] - t_ref[...]
        dy_ref[...] = err * (1.0 / D)
        col = jnp.sum(err * err, axis=0, keepdims=True)
        part = sum(col[:, k * 128:(k + 1) * 128] for k in range(D // 128))

        @pl.when(pl.program_id(0) == 0)
        def _():
            acc_ref[...] = part

        @pl.when(pl.program_id(0) > 0)
        def _():
            acc_ref[...] += part

    tile = pl.BlockSpec((T, D), lambda i: (i, 0))
    return pl.pallas_call(
        kern, grid=(S // T,), in_specs=[tile, tile], out_specs=[tile, pl.BlockSpec((1, 128), lambda i: (0, 0))],
        out_shape=[jax.ShapeDtypeStruct((S, D), F32), jax.ShapeDtypeStruct((1, 128), F32)],
        name=name, compiler_params=_cparams(("arbitrary",)))(y, target)


def _adamw_math(w, g, m, v):
    m = ADAM_B1 * m + (1.0 - ADAM_B1) * g
    v = ADAM_B2 * v + (1.0 - ADAM_B2) * jnp.square(g)
    m_hat = m / (1.0 - ADAM_B1 ** ADAM_STEP)
    v_hat = v / (1.0 - ADAM_B2 ** ADAM_STEP)
    return -ADAM_LR * (m_hat / (jnp.sqrt(v_hat) + ADAM_EPS) + ADAM_WD * w), m, v


def _adamw(w, g, m, v, *, name):
    R, C = w.shape
    T = _pick(R, (256, 128, 64, 32, 16, 8))

    def kern(w_ref, g_ref, m_ref, v_ref, d_ref, mo_ref, vo_ref):
        d, mn, vn = _adamw_math(w_ref[...], g_ref[...], m_ref[...], v_ref[...])
        d_ref[...] = d
        mo_ref[...] = mn
        vo_ref[...] = vn

    tile = pl.BlockSpec((T, C), lambda i: (i, 0))
    return pl.pallas_call(
        kern, grid=(R // T,), in_specs=[tile] * 4, out_specs=[tile] * 3,
        out_shape=[jax.ShapeDtypeStruct((R, C), F32)] * 3, name=name,
        compiler_params=_cparams(("parallel",)))(w, g, m, v)


def _adamw_layer(w, g, m, v, layer, prev, *, name):
    L, R, C = w.shape
    T = _pick(R, (256, 128, 64, 32, 16, 8))

    def kern(w_ref, g_ref, m_ref, v_ref, *rest):
        go_ref, d_ref, mo_ref, vo_ref = rest[-4:]
        grad = g_ref[...]
        d, mn, vn = _adamw_math(w_ref[...], grad, m_ref[...], v_ref[...])
        go_ref[...] = grad
        d_ref[...] = d
        mo_ref[...] = mn
        vo_ref[...] = vn

    layer_tile = pl.BlockSpec((None, T, C), lambda i: (layer, i, 0))
    in_specs = [layer_tile, pl.BlockSpec((T, C), lambda i: (i, 0)), layer_tile, layer_tile]
    args, aliases = [w, g, m, v], {}
    if prev is not None:
        in_specs += [pl.BlockSpec(memory_space=pl.ANY)] * 4
        args += list(prev)
        aliases = {4 + k: k for k in range(4)}
    return pl.pallas_call(
        kern, grid=(R // T,), in_specs=in_specs, out_specs=[layer_tile] * 4,
        out_shape=[jax.ShapeDtypeStruct((L, R, C), F32)] * 4, input_output_aliases=aliases, name=name,
        compiler_params=_cparams(("parallel",)))(*args)


def _sum8(g, *, name):
    def kern(g_ref, o_ref):
        acc = g_ref[0]
        for d in range(1, g.shape[0]):
            acc = acc + g_ref[d]
        o_ref[...] = acc

    return pl.pallas_call(kern, out_shape=jax.ShapeDtypeStruct(g.shape[1:], F32), name=name,
                          compiler_params=_cparams())(g)


def _place():
    return lax.axis_index("x"), lax.axis_index("y"), lax.axis_index("c")


def _other_chips(x, y):
    return [(1 - x, y), (x, 1 - y), (1 - x, 1 - y)]


def _remote(src, dst, send_sems, recv_sems, k, to):
    return pltpu.make_async_remote_copy(src_ref=src, dst_ref=dst, send_sem=send_sems.at[k], recv_sem=recv_sems.at[k],
                                        device_id=to, device_id_type=MESH)


def _all_gather_small(v, *, name):
    def body(x_ref, out_ref, send_sems, recv_sems, local_sem):
        x, y, c = _place()
        me = 4 * x + 2 * y + c
        mine = pltpu.make_async_copy(x_ref, out_ref.at[me], local_sem)
        mine.start()
        peers = []
        for f in range(1, 8):
            peers.append((1 - x if f & 4 else x, 1 - y if f & 2 else y, 1 - c if f & 1 else c))
        sends = [_remote(x_ref, out_ref.at[me], send_sems, recv_sems, k, p) for k, p in enumerate(peers)]
        for cp in sends:
            cp.start()
        for k, (px, py, pc) in enumerate(peers):
            _remote(x_ref, out_ref.at[4 * px + 2 * py + pc], send_sems, recv_sems, k, (px, py, pc)).wait_recv()
        for cp in sends:
            cp.wait_send()
        mine.wait()

    return pl.pallas_call(
        body, out_shape=jax.ShapeDtypeStruct((8,) + v.shape, v.dtype),
        in_specs=[pl.BlockSpec(memory_space=pltpu.VMEM)], out_specs=pl.BlockSpec(memory_space=pltpu.VMEM),
        scratch_shapes=[pltpu.SemaphoreType.DMA((7,)), pltpu.SemaphoreType.DMA((7,)), pltpu.SemaphoreType.DMA],
        name=name, compiler_params=_cparams())(v)


def _hosted_call(kern, *, grid, in_specs, out_specs, out_shape, scratch_shapes, args, name, comm=None, sem=None):
    if comm is None:
        res = pl.pallas_call(kern, grid=grid, in_specs=in_specs, out_specs=out_specs, out_shape=out_shape,
                             scratch_shapes=scratch_shapes, name=name, compiler_params=_cparams(sem))(*args)
        return list(res), []
    n_in, n_out, n_scr = len(in_specs), len(out_specs), len(scratch_shapes)
    c_in, c_out = len(comm.inputs), len(comm.out_shapes)
    steps = grid[0]

    def body(*refs):
        ins, ci = refs[:n_in], refs[n_in:n_in + c_in]
        outs = refs[n_in + c_in:n_in + c_in + n_out]
        co = refs[n_in + c_in + n_out:n_in + c_in + n_out + c_out]
        scr = refs[n_in + c_in + n_out + c_out:n_in + c_in + n_out + c_out + n_scr]
        cs = refs[n_in + c_in + n_out + c_out + n_scr:]
        step = pl.program_id(0)

        @pl.when(step == 0)
        def _():
            comm.begin(ci, co, cs)

        kern(*ins, *outs, *scr)

        @pl.when(step == steps // 2)
        def _():
            comm.middle(ci, co, cs)

        @pl.when(step == steps - 1)
        def _():
            comm.end(ci, co, cs)

    hbm = pl.BlockSpec(memory_space=pltpu.HBM)
    res = pl.pallas_call(
        body, grid=grid, in_specs=list(in_specs) + [hbm] * c_in, out_specs=list(out_specs) + [hbm] * c_out,
        out_shape=list(out_shape) + list(comm.out_shapes), scratch_shapes=list(scratch_shapes) + list(comm.scratch),
        input_output_aliases={n_in + i: n_out + o for i, o in comm.aliases.items()},
        name=name, compiler_params=_cparams(("arbitrary",)))(*args, *comm.inputs)
    return list(res[:n_out]), list(res[n_out:])


def _run_comm(comm, *, name):
    return _hosted_call(lambda: None, grid=(1,), in_specs=[], out_specs=[], out_shape=[], scratch_shapes=[], args=[],
                        name=name, comm=comm)[1]


class _Gather:
    def __init__(self, shards, kinds, items):
        used = sorted({w for w, _ in items})
        self.slot = {w: k for k, w in enumerate(used)}
        self.inputs = [shards[w] for w in used]
        self.items, self.kinds = list(items), kinds
        self.shapes = {w: shards[w].shape[1:] for w in used}
        self.out_shapes = [jax.ShapeDtypeStruct((r, 4 * n) if kinds[w] == "col" else (4 * r, n), shards[w].dtype)
                           for w, _ in items for r, n in [self.shapes[w]]]
        n_items = len(items)
        self.scratch = [pltpu.SemaphoreType.DMA((6 * n_items,)), pltpu.SemaphoreType.DMA((6 * n_items,)),
                        pltpu.SemaphoreType.DMA((n_items,))]
        self.aliases = {}

    def _piece(self, ref, w, qq, half):
        r, n = self.shapes[w]
        h = r // 2
        lo, size = (0, r) if half is None else (half * h, h)
        if self.kinds[w] == "col":
            return ref.at[pl.ds(pl.multiple_of(lo, 16), size), pl.ds(pl.multiple_of(qq * n, 128), n)]
        return ref.at[pl.ds(pl.multiple_of(qq * r + lo, 16), size), :]

    def _mine(self, ci, w, l, half):
        h = self.shapes[w][0] // 2
        return ci[self.slot[w]].at[l, pl.ds(pl.multiple_of(half * h, 16), h), :]

    def begin(self, ci, co, cs):
        send_sems, recv_sems, local_sems = cs
        x, y, c = _place()
        q = 2 * x + y
        for k, (w, l) in enumerate(self.items):
            pltpu.make_async_copy(ci[self.slot[w]].at[l], self._piece(co[k], w, q, None), local_sems.at[k]).start()
            for j, (cx, cy) in enumerate(_other_chips(x, y)):
                _remote(self._mine(ci, w, l, c), self._piece(co[k], w, q, c), send_sems, recv_sems, 6 * k + j,
                        (cx, cy, c)).start()

    def middle(self, ci, co, cs):
        send_sems, recv_sems, _ = cs
        x, y, c = _place()
        for k, (w, l) in enumerate(self.items):
            for j, (cx, cy) in enumerate(_other_chips(x, y)):
                win = self._piece(co[k], w, 2 * cx + cy, c)
                _remote(win, win, send_sems, recv_sems, 6 * k + j, (cx, cy, c)).wait_recv()
                _remote(win, win, send_sems, recv_sems, 6 * k + 3 + j, (x, y, 1 - c)).start()

    def end(self, ci, co, cs):
        send_sems, recv_sems, local_sems = cs
        x, y, c = _place()
        q = 2 * x + y
        for k, (w, l) in enumerate(self.items):
            for j, (cx, cy) in enumerate(_other_chips(x, y)):
                win = self._piece(co[k], w, 2 * cx + cy, 1 - c)
                _remote(win, win, send_sems, recv_sems, 6 * k + 3 + j, (x, y, 1 - c)).wait_recv()
        for k, (w, l) in enumerate(self.items):
            for j, (cx, cy) in enumerate(_other_chips(x, y)):
                _remote(self._mine(ci, w, l, c), self._piece(co[k], w, q, c), send_sems, recv_sems, 6 * k + j,
                        (cx, cy, c)).wait_send()
                win = self._piece(co[k], w, 2 * cx + cy, c)
                _remote(win, win, send_sems, recv_sems, 6 * k + 3 + j, (x, y, 1 - c)).wait_send()
            pltpu.make_async_copy(ci[self.slot[w]].at[l], self._piece(co[k], w, q, None), local_sems.at[k]).wait()


def _half_rows(ref, half, h):
    return ref.at[:, pl.ds(pl.multiple_of(half * h, 16), h), :]


class _Copies:
    def __init__(self, inputs, out_shapes, count, pairs, aliases=None):
        self.inputs, self.out_shapes, self.pairs = list(inputs), list(out_shapes), pairs
        self.scratch = [pltpu.SemaphoreType.DMA((count,)), pltpu.SemaphoreType.DMA((count,))]
        self.aliases = aliases or {}

    def _copies(self, ci, co, cs):
        x, y, c = _place()
        return [_remote(src, dst, cs[0], cs[1], k, to) for k, (src, dst, to) in enumerate(self.pairs(ci, co, x, y, c))]

    def begin(self, ci, co, cs):
        for cp in self._copies(ci, co, cs):
            cp.start()

    def middle(self, ci, co, cs):
        pass

    def end(self, ci, co, cs):
        for cp in self._copies(ci, co, cs):
            cp.wait()


def _swap_halves(gs):
    def pairs(ci, co, x, y, c):
        return [(_half_rows(ci[k], 1 - c, g.shape[1] // 2), co[k], (x, y, 1 - c)) for k, g in enumerate(gs)]

    return _Copies(gs, [jax.ShapeDtypeStruct((g.shape[0], g.shape[1] // 2, g.shape[2]), g.dtype) for g in gs],
                   len(gs), pairs)


def _scatter_quarters(ps, kinds):
    part = [((p.shape[1], p.shape[2] // 4) if kind == "col" else (p.shape[1], p.shape[2])) for p, kind in zip(ps, kinds)]

    def pairs(ci, co, x, y, c):
        out = []
        for k, kind in enumerate(kinds):
            n = part[k][1]
            for j, (cx, cy) in enumerate(_other_chips(x, y)):
                qj = 2 * cx + cy
                src = ci[k].at[0, :, pl.ds(pl.multiple_of(qj * n, 128), n)] if kind == "col" else ci[k].at[qj]
                out.append((src, co[k].at[j], (cx, cy, c)))
        return out

    return _Copies(ps, [jax.ShapeDtypeStruct((3,) + pt, p.dtype) for pt, p in zip(part, ps)], 3 * len(ps), pairs)


def _share_halves(gs):
    def rows(co, k, half):
        h = gs[k].shape[0] // 2
        return co[k].at[pl.ds(pl.multiple_of(half * h, 16), h), :]

    def pairs(ci, co, x, y, c):
        return [(rows(co, k, c), rows(co, k, c), (x, y, 1 - c)) for k in range(len(gs))]

    prog = _Copies(gs, [jax.ShapeDtypeStruct(g.shape, g.dtype) for g in gs], len(gs), pairs,
                   aliases={k: k for k in range(len(gs))})

    def end(ci, co, cs):
        x, y, c = _place()
        for k in range(len(gs)):
            cp = _remote(rows(co, k, c), rows(co, k, 1 - c), cs[0], cs[1], k, (x, y, 1 - c))
            cp.wait_send()
            cp.wait_recv()

    prog.end = end
    return prog


def _wide_tile(n):
    return _pick(n, (2048, 1920, 1024, 512, 256, 128))


def _pair_sum(g, land, place, *, name):
    B, R, N = g.shape
    h = R // 2
    tr, tc = _pick(h, (256, 128)), _wide_tile(N)

    def kern(place_ref, g_ref, l_ref, o_ref):
        o_ref[...] = (g_ref[...] + l_ref[...]).astype(o_ref.dtype)

    grid_spec = pltpu.PrefetchScalarGridSpec(
        num_scalar_prefetch=1, grid=(B, h // tr, N // tc),
        in_specs=[pl.BlockSpec((None, tr, tc), lambda b, i, j, p: (b, p[1] * (h // tr) + i, j)),
                  pl.BlockSpec((None, tr, tc), lambda b, i, j, p: (b, i, j))],
        out_specs=pl.BlockSpec((None, tr, tc), lambda b, i, j, p: (b, i, j)))
    return pl.pallas_call(kern, grid_spec=grid_spec, out_shape=jax.ShapeDtypeStruct((B, h, N), BF16), name=name,
                          compiler_params=_cparams(("parallel", "parallel", "parallel")))(place, g, land)


def _quarter_sum(p, land, kind, shard_shape, place, *, name):
    L, r, n = shard_shape
    h = r // 2
    tr, tc = _pick(h, (256, 128)), _wide_tile(n)

    def kern(place_ref, p_ref, a_ref, b_ref, c_ref, o_ref):
        o_ref[...] = ((p_ref[...].astype(F32) + a_ref[...].astype(F32)) + b_ref[...].astype(F32)) + c_ref[...].astype(F32)

    if kind == "col":
        p_spec = pl.BlockSpec((None, tr, tc), lambda l, i, j, pr: (l, i, pr[0] * (n // tc) + j))
    else:
        p_spec = pl.BlockSpec((None, None, tr, tc), lambda l, i, j, pr: (l, pr[0], i, j))
    lands = [pl.BlockSpec((None, None, tr, tc), functools.partial(lambda l, i, j, pr, s: (s, l, i, j), s=s))
             for s in range(3)]
    grid_spec = pltpu.PrefetchScalarGridSpec(
        num_scalar_prefetch=1, grid=(L, h // tr, n // tc), in_specs=[p_spec] + lands,
        out_specs=pl.BlockSpec((None, tr, tc), lambda l, i, j, pr: (l, pr[1] * (h // tr) + i, j)))
    return pl.pallas_call(kern, grid_spec=grid_spec, out_shape=jax.ShapeDtypeStruct((L, r, n), F32), name=name,
                          compiler_params=_cparams(("parallel", "parallel", "parallel")))(place, p, land, land, land)


class _ReduceScatter:
    def __init__(self, grads, kinds, shard_shapes, place, tag):
        self.kinds, self.shapes, self.place, self.tag = kinds, shard_shapes, place, tag
        self.g3 = [g[None] if kind == "col" else g.reshape(4, g.shape[0] // 4, g.shape[1]) for g, kind in zip(grads, kinds)]

    def swap(self):
        return _swap_halves(self.g3)

    def pair_sums(self, lands):
        self.ps = [_pair_sum(g, land, self.place, name=f"rs_pair_sum_{self.tag}_{k}")
                   for k, (g, land) in enumerate(zip(self.g3, lands))]

    def scatter(self):
        return _scatter_quarters(self.ps, self.kinds)

    def quarter_sums(self, parts):
        self.halves = []
        for k, (p, part) in enumerate(zip(self.ps, parts)):
            p4 = p if self.kinds[k] == "col" else p[None]
            out = _quarter_sum(p4, part[:, None], self.kinds[k], (1,) + tuple(self.shapes[k]), self.place,
                               name=f"rs_quarter_sum_{self.tag}_{k}")
            self.halves.append(out[0])

    def share(self):
        return _share_halves(self.halves)

    def run(self):
        self.pair_sums(_run_comm(self.swap(), name=f"rs_swap_{self.tag}"))
        self.quarter_sums(_run_comm(self.scatter(), name=f"rs_scatter_{self.tag}"))
        return _run_comm(self.share(), name=f"rs_share_{self.tag}")


_WEIGHTS = ["mod_w", "mod_b", "norm1_g", "w_in", "gate_b", "conv_w", "conv_b", "conv_ln_g", "conv_ln_b", "w_conv_proj",
            "hgrn_lb", "hgrn_norm_g", "w_hgrn_proj", "sb_qn_g", "sb_kn_g", "w_sb_proj", "w_out", "norm2_g", "mlp_w1",
            "mlp_w2"]
_BIG = [("w_in", "col"), ("w_conv_proj", "col"), ("w_hgrn_proj", "col"), ("w_sb_proj", "col"), ("w_out", "row"),
        ("mlp_w1", "col"), ("mlp_w2", "row")]
_REPLICATED = ["mod_b", "norm1_g", "gate_b", "conv_b", "conv_ln_g", "conv_ln_b", "hgrn_lb", "hgrn_norm_g", "sb_qn_g",
               "sb_kn_g", "norm2_g"]
LANES = 128


class _Pack:
    def __init__(self, items):
        self.shapes = {n: a.shape for n, a in items}
        self.offsets, pos = {}, 0
        for n, a in items:
            self.offsets[n] = pos
            pos += math.prod(a.shape)
        self.rows = -(-pos // (8 * LANES)) * 8
        flat = jnp.concatenate([a.reshape(-1).astype(F32) for _, a in items])
        self.array = jnp.pad(flat, (0, self.rows * LANES - pos)).reshape(self.rows, LANES)

    def get(self, packed, name):
        lead = packed.shape[:-2]
        flat = packed.reshape(lead + (self.rows * LANES,))
        n = math.prod(self.shapes[name])
        return lax.slice_in_dim(flat, self.offsets[name], self.offsets[name] + n, axis=len(lead)).reshape(
            lead + self.shapes[name])


def _lower_bounds(hgrn_lb):
    p = jax.nn.softmax(hgrn_lb.astype(F32), axis=0)
    return jnp.cumsum(p, axis=0) - p[0:1]


def _layer_fwd(x, w, p, l, comms=(None, None)):
    S, D = x.shape
    r = {"x": x}
    (r["h"],) = _rowop(_fn_normmod, [(x, 0, D)], [p["n1g"], p["sc1"], p["sh1"]], [(D, BF16)], name=f"normmod1_fwd_{l}")
    proj = r["proj"] = _matmul(r["h"], w["w_in", l], name=f"w_in_fwd_{l}")
    r["cpre"] = _conv_fwd(proj, p["w32"], p["conv_b"], name=f"conv_fwd_{l}")
    (r["cact"],) = _rowop(_fn_lnsilu, [(r["cpre"], 0, CONV_CH)], [p["lng"], p["lnb"]], [(CONV_CH, BF16)],
                          name=f"conv_ln_fwd_{l}")
    arrived = lambda comm, got: w.update({(_BIG[k][0], layer): arr for (k, layer), arr in zip(comm.items, got)})
    (r["hg"], r["states"]), got = _hgrn_fwd(proj, p["lbk"], p["ng"], name=f"hgrn_fwd_{l}", comm=comms[0])
    if comms[0] is not None:
        arrived(comms[0], got)
    (r["sb"], r["rs"]), got = _sb_fwd(proj, p["qg"], p["kg"], name=f"sb_fwd_{l}", comm=comms[1])
    if comms[1] is not None:
        arrived(comms[1], got)
    r["y_c"] = _matmul(r["cact"], w["w_conv_proj", l], name=f"w_conv_proj_fwd_{l}")
    r["y_h"] = _matmul(r["hg"], w["w_hgrn_proj", l], name=f"w_hgrn_proj_fwd_{l}")
    r["y_s"] = _matmul(r["sb"], w["w_sb_proj", l], name=f"w_sb_proj_fwd_{l}")
    (r["merged"],) = _rowop(_fn_merge, [(proj, OFF_GL, 3 * D), (r["y_c"], 0, D), (r["y_h"], 0, D), (r["y_s"], 0, D)],
                            [p["gate_b"]], [(D, BF16)], name=f"merge_fwd_{l}")
    r["a_out"] = _matmul(r["merged"], w["w_out", l], name=f"w_out_fwd_{l}")
    (r["x1"],) = _rowop(_fn_resid, [(x, 0, D), (r["a_out"], 0, D)], [p["g1"]], [(D, F32)], name=f"resid1_fwd_{l}")
    (r["h2"],) = _rowop(_fn_normmod, [(r["x1"], 0, D)], [p["n2g"], p["sc2"], p["sh2"]], [(D, BF16)],
                        name=f"normmod2_fwd_{l}")
    r["u"], r["act"] = _matmul(r["h2"], w["mlp_w1", l], name=f"mlp_w1_fwd_{l}", post=lambda u: (u,) + _fn_relu2(u),
                               out_dtypes=(F32, BF16))
    r["m_out"] = _matmul(r["act"], w["mlp_w2", l], name=f"mlp_w2_fwd_{l}")
    (x2,) = _rowop(_fn_resid, [(r["x1"], 0, D), (r["m_out"], 0, D)], [p["g2"]], [(D, F32)], name=f"resid2_fwd_{l}")
    return x2, r


def _layer_bwd(dx2, r, w, p, l, grads, carry=None):
    S, D = dx2.shape
    small = {}

    def dweight(name, a, dy):
        grads[name, l] = _matmul(a, dy, ta=True, name=f"{name}_dw_{l}")

    stage = (lambda k, got: carry(k, got)) if carry is not None else (lambda k, got: None)

    (dm_out,), (dg2,) = _rowop_bwd(_fn_scale, [(r["m_out"], 0, D)], [p["g2"]], [dx2], [BF16], name=f"resid2_bwd_{l}")
    (du,) = _matmul(dm_out, w["mlp_w2", l], tb=True, name=f"mlp_w2_dx_{l}", extras=[r["u"]], out_dtypes=(BF16,),
                    post=lambda dact, u: (dact * (2.0 * jnp.maximum(u, 0.0)),))
    dweight("mlp_w2", r["act"], dm_out)
    dh2 = _matmul(du, w["mlp_w1", l], tb=True, name=f"mlp_w1_dx_{l}")
    dweight("mlp_w1", r["h2"], du)
    (dx1,), (small["norm2_g"], dsc2, dsh2) = _rowop_bwd(
        _fn_normmod, [(r["x1"], 0, D)], [p["n2g"], p["sc2"], p["sh2"]], [dh2], [F32], add={0: dx2},
        name=f"normmod2_bwd_{l}")
    (da_out,), (dg1,) = _rowop_bwd(_fn_scale, [(r["a_out"], 0, D)], [p["g1"]], [dx1], [BF16], name=f"resid1_bwd_{l}")
    dmerged = _matmul(da_out, w["w_out", l], tb=True, name=f"w_out_dx_{l}")
    dweight("w_out", r["merged"], da_out)
    (dgl, dy_c, dy_h, dy_s), (small["gate_b"],) = _rowop_bwd(
        _fn_merge, [(r["proj"], OFF_GL, 3 * D), (r["y_c"], 0, D), (r["y_h"], 0, D), (r["y_s"], 0, D)], [p["gate_b"]],
        [dmerged], [BF16] * 4, name=f"merge_bwd_{l}")
    dweight("w_conv_proj", r["cact"], dy_c)
    dweight("w_hgrn_proj", r["hg"], dy_h)
    dweight("w_sb_proj", r["sb"], dy_s)
    dcact = _matmul(dy_c, w["w_conv_proj", l], tb=True, name=f"w_conv_proj_dx_{l}")
    (dcpre,), (small["conv_ln_g"], small["conv_ln_b"]) = _rowop_bwd(
        _fn_lnsilu, [(r["cpre"], 0, CONV_CH)], [p["lng"], p["lnb"]], [dcact], [F32], name=f"conv_ln_bwd_{l}")
    (d_conv, dw32, small["conv_b"]), got = _conv_bwd(r["proj"], dcpre, p["w32"], name=f"conv_bwd_{l}",
                                                      comm=stage(0, None))
    small["conv_w"] = dw32[:CONV_WIDTH]
    dhg = _matmul(dy_h, w["w_hgrn_proj", l], tb=True, out_dtype=BF16, name=f"w_hgrn_proj_dx_{l}")
    (dq, df, di, dg, dlbk, dng), got = _hgrn_bwd(r["proj"], r["states"], dhg, p["lbk"], p["ng"], name=f"hgrn_bwd_{l}",
                                                 comm=stage(1, got))
    small["lower"] = -dlbk
    small["hgrn_norm_g"] = jnp.sum(dng, axis=0)
    dsb = _matmul(dy_s, w["w_sb_proj", l], tb=True, out_dtype=BF16, name=f"w_sb_proj_dx_{l}")
    (dsq, dsk, dsv, dqg, dkg), got = _sb_bwd(r["proj"], p["qg"], p["kg"], r["rs"], dsb, name=f"sb_bwd_{l}",
                                             comm=stage(2, got))
    stage(3, got)
    fold = lambda t: jnp.sum(t.reshape(-1, SB_DH), axis=0, keepdims=True)
    small["sb_qn_g"], small["sb_kn_g"] = fold(dqg), fold(dkg)
    dproj = jnp.concatenate([d_conv, dq, df, di, dg, dsq, dsk, dsv, dgl], axis=1)
    dh = _matmul(dproj, w["w_in", l], tb=True, name=f"w_in_dx_{l}")
    dweight("w_in", r["h"], dproj)
    (dx,), (small["norm1_g"], dsc1, dsh1) = _rowop_bwd(
        _fn_normmod, [(r["x"], 0, D)], [p["n1g"], p["sc1"], p["sh1"]], [dh], [F32], add={0: dx1},
        name=f"normmod1_bwd_{l}")
    small["mod"] = jnp.concatenate([dsh1, dsc1, dg1, dsh2, dsc2, dg2], axis=1)
    return dx, small


def kernel(x, c, mod_w, mod_b, norm1_g, w_in, gate_b, conv_w, conv_b, conv_ln_g, conv_ln_b, w_conv_proj, hgrn_lb, hgrn_norm_g, w_hgrn_proj, sb_qn_g, sb_kn_g, w_sb_proj, w_out, norm2_g, mlp_w1, mlp_w2, loss_target, m_mod_w, m_mod_b, m_norm1_g, m_w_in, m_gate_b, m_conv_w, m_conv_b, m_conv_ln_g, m_conv_ln_b, m_w_conv_proj, m_hgrn_lb, m_hgrn_norm_g, m_w_hgrn_proj, m_sb_qn_g, m_sb_kn_g, m_w_sb_proj, m_w_out, m_norm2_g, m_mlp_w1, m_mlp_w2, v_mod_w, v_mod_b, v_norm1_g, v_w_in, v_gate_b, v_conv_w, v_conv_b, v_conv_ln_g, v_conv_ln_b, v_w_conv_proj, v_hgrn_lb, v_hgrn_norm_g, v_w_hgrn_proj, v_sb_qn_g, v_sb_kn_g, v_w_sb_proj, v_w_out, v_norm2_g, v_mlp_w1, v_mlp_w2):
    given = dict(locals())
    wts = {n: given[n] for n in _WEIGHTS}
    mom = {n: given["m_" + n] for n in _WEIGHTS}
    var = {n: given["v_" + n] for n in _WEIGHTS}
    n_layers, D = norm1_g.shape
    xi, yi, ci = _place()
    q = 2 * xi + yi
    me = 4 * xi + 2 * yi + ci
    place = jnp.stack([q, ci]).astype(jnp.int32)
    n_mod = mod_w.shape[2]
    cw = conv_w.shape[2]

    pk1 = _Pack([("c", c), ("conv_w", conv_w)])
    got1 = _all_gather_small(pk1.array, name="gather_cond")
    c_act = jax.nn.silu(pk1.get(got1, "c")[:, 0, :])
    conv_full = jnp.concatenate([pk1.get(got1, "conv_w")[2 * k] for k in range(4)], axis=-1)

    mod_cols = []
    for l in range(n_layers):
        mb = lax.dynamic_slice_in_dim(mod_b[l], q * n_mod, n_mod)
        mod_cols.append(_matmul(c_act, mod_w, bl=l, name=f"mod_fwd_{l}") + mb[None, :])
    got2 = _all_gather_small(jnp.concatenate(mod_cols, axis=0), name="gather_mod")
    mods = []
    for l in range(n_layers):
        row = lax.dynamic_index_in_dim(got2[0::2], l * 8 + me, axis=1, keepdims=False)
        mods.append(jnp.split(row.reshape(1, 4 * n_mod), 6, axis=1))

    lower, lower_vjp = jax.vjp(_lower_bounds, hgrn_lb)

    shards = [wts[n].astype(BF16) for n, _ in _BIG]
    kinds = [k for _, k in _BIG]
    index = {n: k for k, (n, _) in enumerate(_BIG)}
    first = ["w_in", "w_conv_proj", "w_hgrn_proj", "w_sb_proj"]
    later = ["w_out", "mlp_w1", "mlp_w2"]
    gather = lambda names, l: _Gather(shards, kinds, [(index[n], l) for n in names]) if l < n_layers else None
    start = gather(first, 0)
    w = {(_BIG[k][0], layer): arr
         for (k, layer), arr in zip(start.items, _run_comm(start, name="gather_first_weights"))}

    def layer_params(l):
        sh1, sc1, g1, sh2, sc2, g2 = mods[l]
        return dict(sh1=sh1, sc1=sc1, g1=g1, sh2=sh2, sc2=sc2, g2=g2, n1g=norm1_g[l][None], n2g=norm2_g[l][None],
                    gate_b=gate_b[l][None], conv_b=conv_b[l][None], lng=conv_ln_g[l][None], lnb=conv_ln_b[l][None],
                    w32=jnp.pad(conv_full[l], ((0, CONV_HALO - CONV_WIDTH), (0, 0))), lbk=(1.0 - lower[l])[None],
                    ng=hgrn_norm_g[l][None], qg=jnp.tile(sb_qn_g[l][None], (1, SB_PAIR // SB_DH)),
                    kg=jnp.tile(sb_kn_g[l][None], (1, SB_PAIR // SB_DH)))

    params = [layer_params(l) for l in range(n_layers)]
    act, saved = x[0], []
    for l in range(n_layers):
        act, r = _layer_fwd(act, w, params[l], l, comms=(gather(later, l), gather(first, l + 1)))
        saved.append(r)
    dact, loss_lanes = _loss_head(act, loss_target[0], name="loss_head")

    grads, smalls, reduced = {}, [None] * n_layers, {}

    def reduce_scatter(items, tag):
        return _ReduceScatter([grads[_BIG[k][0], layer] for k, layer in items], [kinds[k] for k, _ in items],
                              [shards[k].shape[1:] for k, _ in items], place, tag)

    def carried(l):
        items = [(k, l + 1) for k in range(len(_BIG))] + [(k, l) for k, (n, _) in enumerate(_BIG) if n != "w_in"]
        box = {}

        def carry(stage, got):
            if stage == 0:
                box["rs"] = reduce_scatter(items, f"l{l}")
                return box["rs"].swap()
            if stage == 1:
                box["rs"].pair_sums(got)
                return box["rs"].scatter()
            if stage == 2:
                box["rs"].quarter_sums(got)
                return box["rs"].share()
            reduced.update(zip(items, got))

        return carry

    for l in reversed(range(n_layers)):
        dact, smalls[l] = _layer_bwd(dact, saved[l], w, params[l], l, grads, carried(l) if l + 1 < n_layers else None)
    grad_x = dact[None]
    rest = [(k, l) for l in range(n_layers) for k in range(len(_BIG)) if (k, l) not in reduced]
    reduced.update(zip(rest, reduce_scatter(rest, "last").run()))

    stack = lambda k: jnp.stack([smalls[l][k] for l in range(n_layers)])
    (d_hgrn_lb,) = lower_vjp(stack("lower")[:, 0, :])
    items = [("loss", loss_lanes), ("mod", stack("mod")), ("hgrn_lb", d_hgrn_lb), ("conv_w", stack("conv_w"))]
    items += [(k, stack(k)) for k in ("norm1_g", "gate_b", "conv_b", "conv_ln_g", "conv_ln_b", "hgrn_norm_g", "sb_qn_g",
                                      "sb_kn_g", "norm2_g")]
    pk3 = _Pack(items)
    got3 = _all_gather_small(pk3.array, name="gather_small_grads")
    tot3 = _sum8(got3, name="sum_small_grads")
    loss = (0.5 / D) * jnp.sum(pk3.get(tot3, "loss"))
    g = {k: pk3.get(tot3, k).reshape(wts[k].shape) for k in _REPLICATED if k != "mod_b"}
    g["mod_b"] = pk3.get(tot3, "mod")[:, 0, :]
    g["conv_w"] = lax.dynamic_slice_in_dim(pk3.get(tot3, "conv_w"), q * cw, cw, axis=2)
    dmod_all = pk3.get(got3, "mod")[:, :, 0, :]
    g_mod_w = None
    for l in range(n_layers):
        cols = lax.dynamic_slice_in_dim(dmod_all[:, l, :], q * n_mod, n_mod, axis=1)
        g_mod_w = _matmul(c_act, cols, ta=True, layer=l, n_layers=n_layers, into=g_mod_w, name=f"mod_dw_{l}")
    g["mod_w"] = g_mod_w

    delta, new_m, new_v = {}, {}, {}
    for n, _ in _BIG:
        outs = None
        for l in reversed(range(n_layers)):
            outs = _adamw_layer(wts[n], reduced[index[n], l], mom[n], var[n], l, outs, name=f"adamw_{n}_{l}")
        g[n], delta[n], new_m[n], new_v[n] = outs
    two_d = lambda t: t.reshape(-1, t.shape[-1])
    outs = _adamw(two_d(mod_w), two_d(g["mod_w"]), two_d(m_mod_w), two_d(v_mod_w), name="adamw_mod_w")
    delta["mod_w"], new_m["mod_w"], new_v["mod_w"] = (t.reshape(mod_w.shape) for t in outs)
    rest = _REPLICATED + ["conv_w"]
    packs = [_Pack([(n, src[n]) for n in rest]) for src in (wts, g, mom, var)]
    outs = _adamw(*[pk.array for pk in packs], name="adamw_small")
    for n in rest:
        delta[n], new_m[n], new_v[n] = (packs[0].get(t, n) for t in outs)

    return (loss, grad_x, *[g[n] for n in _WEIGHTS], *[delta[n] for n in _WEIGHTS], *[new_m[n] for n in _WEIGHTS],
            *[new_v[n] for n in _WEIGHTS])
```

```python
import functools
import math

import jax
import jax.numpy as jnp
from jax import lax
from jax.experimental import pallas as pl
from jax.experimental.pallas import tpu as pltpu

F32 = jnp.float32
BF16 = jnp.bfloat16
MESH = pl.DeviceIdType.MESH

EPS = 1e-6
CONV_CH = 512
CONV_WIDTH = 31
CONV_HALO = 32
HG_HEADS = 4
HG_D = 128
HG_CHUNK = 64
HG_SUB = 16
SB_HEADS = 8
SB_DH = 64
SB_BLK = 128
SB_PAIR = 128
SB_SKIP = -104.0
OFF_CONV, OFF_HG, OFF_SB, OFF_GL = 0, 1024, 3072, 4608
ADAM_LR, ADAM_B1, ADAM_B2, ADAM_EPS, ADAM_WD, ADAM_STEP = 0.001, 0.9, 0.999, 1e-08, 0.01, 10
VMEM_LIMIT_BYTES = 56 * 1024 * 1024
ROW_TILE = 256


def _cparams(sem=None, **kw):
    return pltpu.CompilerParams(dimension_semantics=sem, vmem_limit_bytes=VMEM_LIMIT_BYTES, **kw)


def _pick(n, cands):
    for c in cands:
        if n % c == 0:
            return c
    return n


MATMUL_VMEM_BUDGET = 40 * 1024 * 1024


def _tile_options(n, cap):
    opts = [t for t in range(cap - cap % 128, 0, -128) if n % t == 0]
    return opts or [n]


def _matmul_tiles(M, N, K, size_a, size_b, size_o, in_acc):
    for tm in _tile_options(M, 1024):
        for tk in _tile_options(K, 2048):
            for tn in _tile_options(N, 1280):
                need = 2 * (tm * tk * size_a + tk * tn * size_b + tm * tn * size_o)
                if K > tk and not in_acc:
                    need += tm * tn * 4
                if need <= MATMUL_VMEM_BUDGET:
                    return tm, tn, tk
    raise ValueError(f"no matmul tiling fits VMEM for {(M, N, K)}")
def _matmul(a, b, *, ta=False, tb=False, bl=None, out_dtype=F32, name, into=None, layer=None, n_layers=None,
            post=None, extras=(), rows=(), out_dtypes=None, comm=None):
    M, K = (a.shape[1], a.shape[0]) if ta else a.shape
    N = b.shape[-2] if tb else b.shape[-1]
    if post is not None:
        return _matmul_post(a, b, M, N, K, ta, tb, post, extras, rows, out_dtypes, name)
    assert comm is None or layer is None
    in_acc = jnp.dtype(out_dtype) == jnp.dtype(F32)
    tm, tn, tk = _matmul_tiles(M, N, K, a.dtype.itemsize, b.dtype.itemsize, jnp.dtype(out_dtype).itemsize, in_acc)
    nk = K // tk
    a_spec = pl.BlockSpec((tk, tm), lambda i, j, k: (k, i)) if ta else pl.BlockSpec((tm, tk), lambda i, j, k: (i, k))
    if bl is None:
        b_spec = pl.BlockSpec((tn, tk), lambda i, j, k: (j, k)) if tb else pl.BlockSpec((tk, tn), lambda i, j, k: (k, j))
    elif tb:
        b_spec = pl.BlockSpec((None, tn, tk), lambda i, j, k: (bl, j, k))
    else:
        b_spec = pl.BlockSpec((None, tk, tn), lambda i, j, k: (bl, k, j))
    dn = (((0 if ta else 1,), (1 if tb else 0,)), ((), ()))

    use_scratch = nk > 1 and not in_acc

    def kern(a_ref, b_ref, *rest):
        o_ref = rest[-2] if use_scratch else rest[-1]
        prod = lambda: lax.dot_general(a_ref[...].astype(BF16), b_ref[...].astype(BF16), dn,
                                       preferred_element_type=F32)
        if nk == 1:
            o_ref[...] = prod().astype(o_ref.dtype).reshape(o_ref.shape)
            return
        acc_ref = rest[-1] if use_scratch else o_ref
        k = pl.program_id(2)

        @pl.when(k == 0)
        def _():
            acc_ref[...] = prod().reshape(acc_ref.shape)

        @pl.when(k > 0)
        def _():
            acc_ref[...] += prod().reshape(acc_ref.shape)

        if use_scratch:
            @pl.when(k == nk - 1)
            def _():
                o_ref[...] = acc_ref[...].astype(o_ref.dtype).reshape(o_ref.shape)

    in_specs, args, aliases = [a_spec, b_spec], [a, b], {}
    if layer is None:
        out_shape = jax.ShapeDtypeStruct((M, N), out_dtype)
        out_spec = pl.BlockSpec((tm, tn), lambda i, j, k: (i, j))
    else:
        out_shape = jax.ShapeDtypeStruct((n_layers, M, N), out_dtype)
        out_spec = pl.BlockSpec((1, tm, tn), lambda i, j, k: (layer, i, j))
        if into is not None:
            in_specs.append(pl.BlockSpec(memory_space=pl.ANY))
            args.append(into)
            aliases = {2: 0}
    if comm is not None:
        (out,), got = _hosted_call(kern, grid=(M // tm, N // tn, nk), in_specs=in_specs, out_specs=[out_spec],
                                   out_shape=[out_shape], scratch_shapes=[pltpu.VMEM((tm, tn), F32)] if use_scratch else [],
                                   args=args, name=name, comm=comm)
        return out, got
    return pl.pallas_call(
        kern, grid=(M // tm, N // tn, nk), in_specs=in_specs, out_specs=out_spec, out_shape=out_shape,
        scratch_shapes=[pltpu.VMEM((tm, tn), F32)] if use_scratch else [],
        input_output_aliases=aliases, name=name,
        compiler_params=_cparams(("parallel", "parallel", "arbitrary")))(*args)


def _matmul_post(a, b, M, N, K, ta, tb, post, extras, rows, out_dtypes, name):
    per_elem = sum(e.dtype.itemsize for e in extras) + sum(jnp.dtype(d).itemsize for d in out_dtypes)
    fits = lambda tm, tn: 2 * (tm * K * a.dtype.itemsize + K * tn * b.dtype.itemsize + tm * tn * per_elem) <= MATMUL_VMEM_BUDGET
    tm, tn = next((tm, tn) for tm in _tile_options(M, 1024) for tn in _tile_options(N, 1280) if fits(tm, tn))
    a_spec = pl.BlockSpec((K, tm), lambda i, j: (0, i)) if ta else pl.BlockSpec((tm, K), lambda i, j: (i, 0))
    b_spec = pl.BlockSpec((tn, K), lambda i, j: (j, 0)) if tb else pl.BlockSpec((K, tn), lambda i, j: (0, j))
    tile = pl.BlockSpec((tm, tn), lambda i, j: (i, j))
    row = pl.BlockSpec((1, tn), lambda i, j: (0, j))
    dn = (((0 if ta else 1,), (1 if tb else 0,)), ((), ()))
    n_ex = len(extras) + len(rows)

    def kern(a_ref, b_ref, *rest):
        prod = lax.dot_general(a_ref[...].astype(BF16), b_ref[...].astype(BF16), dn, preferred_element_type=F32)
        res = post(prod, *[r[...].astype(F32) for r in rest[:n_ex]])
        for val, o_ref in zip(res, rest[n_ex:]):
            o_ref[...] = val.astype(o_ref.dtype)

    return pl.pallas_call(
        kern, grid=(M // tm, N // tn), in_specs=[a_spec, b_spec] + [tile] * len(extras) + [row] * len(rows),
        out_specs=[tile] * len(out_dtypes), out_shape=[jax.ShapeDtypeStruct((M, N), d) for d in out_dtypes], name=name,
        compiler_params=_cparams(("parallel", "parallel")))(a, b, *extras, *rows)


def _col_specs(off, width, T):
    bw = math.gcd(width, off) if off else width
    return [pl.BlockSpec((T, bw), functools.partial(lambda i, c: (i, c), c=off // bw + p)) for p in range(width // bw)]


def _gather_rows(refs, counts):
    vals, pos = [], 0
    for n in counts:
        parts = [refs[pos + p][...].astype(F32) for p in range(n)]
        pos += n
        vals.append(parts[0] if n == 1 else jnp.concatenate(parts, axis=1))
    return vals, pos


def _rowop(fn, ins, params, outs, *, name):
    S = ins[0][0].shape[0]
    T = min(ROW_TILE, S)
    in_specs, counts, args = [], [], []
    for arr, off, width in ins:
        sp = _col_specs(off, width, T)
        in_specs += sp
        counts.append(len(sp))
        args += [arr] * len(sp)
    in_specs += [pl.BlockSpec(p.shape, lambda i: (0, 0)) for p in params]

    def kern(*refs):
        vals, pos = _gather_rows(refs, counts)
        pv = [refs[pos + p][...] for p in range(len(params))]
        pos += len(params)
        res = fn(*vals, *pv)
        for r, o_ref in zip(res, refs[pos:]):
            o_ref[...] = r.astype(o_ref.dtype)

    return pl.pallas_call(
        kern, grid=(S // T,), in_specs=in_specs,
        out_specs=[pl.BlockSpec((T, w), lambda i: (i, 0)) for w, _ in outs],
        out_shape=[jax.ShapeDtypeStruct((S, w), dt) for w, dt in outs],
        name=name, compiler_params=_cparams(("parallel",)))(*args, *params)


def _rowop_bwd(fn, ins, params, douts, din_dtypes, *, name, add=None, comm=None):
    add = add or {}
    S = ins[0][0].shape[0]
    T = min(ROW_TILE, S)
    in_specs, counts, args = [], [], []
    for arr, off, width in ins:
        sp = _col_specs(off, width, T)
        in_specs += sp
        counts.append(len(sp))
        args += [arr] * len(sp)
    in_specs += [pl.BlockSpec(p.shape, lambda i: (0, 0)) for p in params]
    in_specs += [pl.BlockSpec((T, d.shape[1]), lambda i: (i, 0)) for d in douts]
    add_keys = sorted(add)
    in_specs += [pl.BlockSpec((T, add[k].shape[1]), lambda i: (i, 0)) for k in add_keys]
    want = [k for k, dt in enumerate(din_dtypes) if dt is not None]

    def kern(*refs):
        vals, pos = _gather_rows(refs, counts)
        pv = [refs[pos + p][...] for p in range(len(params))]
        pos += len(params)
        cts = [refs[pos + p][...].astype(F32) for p in range(len(douts))]
        pos += len(douts)
        adds = {k: refs[pos + p][...].astype(F32) for p, k in enumerate(add_keys)}
        pos += len(add_keys)
        _, vjp = jax.vjp(fn, *vals, *pv)
        grads = vjp(tuple(cts))
        for k in want:
            g = grads[k] + adds[k] if k in adds else grads[k]
            refs[pos][...] = g.astype(refs[pos].dtype)
            pos += 1
        first = pl.program_id(0) == 0
        for p in range(len(params)):
            gp, o_ref = grads[len(ins) + p], refs[pos + p]

            @pl.when(first)
            def _(gp=gp, o_ref=o_ref):
                o_ref[...] = gp

            @pl.when(jnp.logical_not(first))
            def _(gp=gp, o_ref=o_ref):
                o_ref[...] += gp

    out_specs = [pl.BlockSpec((T, ins[k][2]), lambda i: (i, 0)) for k in want]
    out_specs += [pl.BlockSpec(p.shape, lambda i: (0, 0)) for p in params]
    out_shape = [jax.ShapeDtypeStruct((S, ins[k][2]), din_dtypes[k]) for k in want]
    out_shape += [jax.ShapeDtypeStruct(p.shape, F32) for p in params]
    res, got = _hosted_call(
        kern, grid=(S // T,), in_specs=in_specs, out_specs=out_specs, out_shape=out_shape, scratch_shapes=[],
        args=[*args, *params, *douts, *[add[k] for k in add_keys]], name=name, comm=comm, sem=("arbitrary",))
    dins = [None] * len(ins)
    for p, k in enumerate(want):
        dins[k] = res[p]
    return (dins, res[len(want):]) if comm is None else (dins, res[len(want):], got)


def _rms(x, g):
    return x * lax.rsqrt(jnp.mean(x * x, axis=-1, keepdims=True) + EPS) * g


def _fn_normmod(x, g, sc, sh):
    return (_rms(x, g) * (1.0 + sc) + sh,)


def _fn_lnsilu(c, g, b):
    mu = jnp.mean(c, axis=-1, keepdims=True)
    var = jnp.mean(jnp.square(c - mu), axis=-1, keepdims=True)
    y = (c - mu) * lax.rsqrt(var + EPS) * g + b
    return (y * jax.nn.sigmoid(y),)


def _fn_merge(gl, yc, yh, ys, gb):
    d = yc.shape[1]
    g = jax.nn.sigmoid(gl + gb)
    return (g[:, :d] * yc + g[:, d:2 * d] * yh + g[:, 2 * d:] * ys,)


def _fn_resid(x, y, g):
    return (x + g * y,)


def _fn_scale(y, g):
    return (g * y,)


def _fn_relu2(u):
    return (jnp.square(jnp.maximum(u, 0.0)),)


def _conv_specs(S, T):
    r = T // CONV_HALO
    cur = [pl.BlockSpec((T, CONV_CH), lambda i: (i, 0)), pl.BlockSpec((T, CONV_CH), lambda i: (i, 1))]
    prev = [pl.BlockSpec((CONV_HALO, CONV_CH), lambda i: (jnp.maximum(i * r - 1, 0), 0)),
            pl.BlockSpec((CONV_HALO, CONV_CH), lambda i: (jnp.maximum(i * r - 1, 0), 1))]
    return cur + prev


def _glu_ext(a_ref, g_ref, ah_ref, gh_ref):
    a = a_ref[...]
    sg = jax.nn.sigmoid(g_ref[...])
    uh = jnp.where(pl.program_id(0) > 0, ah_ref[...] * jax.nn.sigmoid(gh_ref[...]), 0.0)
    return a, sg, jnp.concatenate([uh, a * sg], axis=0)


def _shift_up(xe, k, T):
    return xe[:T] if k == 0 else pltpu.roll(xe, shift=xe.shape[0] - k, axis=0)[:T]


def _conv_fwd(proj, w32, b, *, name):
    S = proj.shape[0]
    T = min(ROW_TILE, S)
    lead = CONV_HALO - (CONV_WIDTH - 1)

    def kern(a_ref, g_ref, ah_ref, gh_ref, w_ref, b_ref, o_ref):
        _, _, ue = _glu_ext(a_ref, g_ref, ah_ref, gh_ref)
        acc = jnp.zeros((T, CONV_CH), F32) + b_ref[...]
        for j in range(CONV_WIDTH):
            acc = acc + w_ref[j:j + 1, :] * _shift_up(ue, lead + j, T)
        o_ref[...] = acc

    const = lambda shape: pl.BlockSpec(shape, lambda i: (0, 0))
    return pl.pallas_call(
        kern, grid=(S // T,), in_specs=_conv_specs(S, T) + [const(w32.shape), const(b.shape)],
        out_specs=pl.BlockSpec((T, CONV_CH), lambda i: (i, 0)),
        out_shape=jax.ShapeDtypeStruct((S, CONV_CH), F32), name=name,
        compiler_params=_cparams(("parallel",)))(proj, proj, proj, proj, w32, b)


def _conv_bwd(proj, dc, w32, *, name, comm=None):
    S = proj.shape[0]
    T = min(ROW_TILE, S)
    nt = S // T
    r = T // CONV_HALO
    lead = CONV_HALO - (CONV_WIDTH - 1)
    last_halo = S // CONV_HALO - 1

    def kern(a_ref, g_ref, ah_ref, gh_ref, dc_ref, dcn_ref, w_ref, dag_ref, dw_ref, db_ref):
        i = pl.program_id(0)
        a, sg, ue = _glu_ext(a_ref, g_ref, ah_ref, gh_ref)
        dc_t = dc_ref[...]
        de = jnp.concatenate([dc_t, jnp.where(i < nt - 1, dcn_ref[...], 0.0)], axis=0)

        @pl.when(i == 0)
        def _():
            dw_ref[...] = jnp.zeros_like(dw_ref)
            db_ref[...] = jnp.zeros_like(db_ref)

        du = jnp.zeros((T, CONV_CH), F32)
        for j in range(CONV_WIDTH):
            du = du + w_ref[j:j + 1, :] * _shift_up(de, CONV_WIDTH - 1 - j, T)
            dw_ref[j:j + 1, :] += jnp.sum(dc_t * _shift_up(ue, lead + j, T), axis=0, keepdims=True)
        db_ref[...] += jnp.sum(dc_t, axis=0, keepdims=True)
        dag_ref[:, :CONV_CH] = (du * sg).astype(BF16)
        dag_ref[:, CONV_CH:] = (du * a * sg * (1.0 - sg)).astype(BF16)

    const = lambda shape: pl.BlockSpec(shape, lambda i: (0, 0))
    in_specs = _conv_specs(S, T) + [
        pl.BlockSpec((T, CONV_CH), lambda i: (i, 0)),
        pl.BlockSpec((CONV_HALO, CONV_CH), lambda i: (jnp.minimum((i + 1) * r, last_halo), 0)),
        const(w32.shape)]
    return _hosted_call(
        kern, grid=(nt,), in_specs=in_specs,
        out_specs=[pl.BlockSpec((T, 2 * CONV_CH), lambda i: (i, 0)), const(w32.shape), const((1, CONV_CH))],
        out_shape=[jax.ShapeDtypeStruct((S, 2 * CONV_CH), BF16), jax.ShapeDtypeStruct(w32.shape, F32),
                   jax.ShapeDtypeStruct((1, CONV_CH), F32)],
        scratch_shapes=[], args=[proj, proj, proj, proj, dc, dc, w32], name=name, comm=comm, sem=("arbitrary",))


def _iota2(shape, dim):
    return lax.broadcasted_iota(jnp.int32, shape, dim)


def _running(x, seg, later):
    n = x.shape[0]
    pos = _iota2(x.shape, 0) & (seg - 1)
    k = 1
    while k < seg:
        if later:
            x = x + jnp.where(pos < seg - k, pltpu.roll(x, n - k, axis=0), 0.0)
        else:
            x = x + jnp.where(pos >= k, pltpu.roll(x, k, axis=0), 0.0)
        k *= 2
    return x


@functools.partial(jax.custom_vjp, nondiff_argnums=(1,))
def _prefix(x, seg):
    return _running(x, seg, False)


_prefix.defvjp(lambda x, seg: (_running(x, seg, False), None), lambda seg, _, g: (_running(g, seg, True),))


def _hg_chunk(q, f, iv, g, st, lbk, ng):
    n, sub = HG_CHUNK, HG_SUB
    kk = lbk * jax.nn.sigmoid(-f)
    lf = jnp.log(1.0 - kk)
    b = _prefix(lf, n)
    bs = _prefix(lf, sub)
    bt = jnp.sum(lf, axis=0, keepdims=True)
    qh = q * jax.nn.sigmoid(q)
    dot_nt = lambda x, y: lax.dot_general(x.astype(BF16), y.astype(BF16), (((1,), (1,)), ((), ())),
                                          preferred_element_type=F32)
    o = dot_nt(qh * jnp.exp(b), st)
    b0 = b - bs
    qs = qh * jnp.exp(bs)
    col = _iota2((sub, n), 1)
    rows = []
    for blk in range(n // sub):
        lo = blk * sub
        sl = slice(lo, lo + sub)
        acc = o[sl]
        if blk > 0:
            ref = jnp.concatenate([b0[sl]] * (n // sub), axis=0)
            kd = kk * jnp.exp(jnp.minimum(ref - b, 0.0))
            sc = jnp.where(col < lo, dot_nt(qs[sl], kd), 0.0)
            acc = acc + jnp.dot(sc.astype(BF16), iv.astype(BF16), preferred_element_type=F32)
        bq, bk = bs[sl][None, :, :], bs[sl][:, None, :]
        s_i = lax.broadcasted_iota(jnp.int32, (sub, sub, HG_D), 0)
        t_i = lax.broadcasted_iota(jnp.int32, (sub, sub, HG_D), 1)
        keep = s_i <= t_i
        p = jnp.where(keep, qh[sl][None, :, :] * kk[sl][:, None, :] * jnp.exp(jnp.where(keep, bq - bk, 0.0)), 0.0)
        w = jnp.sum(p, axis=-1, keepdims=True)
        acc = acc + jnp.sum(w * iv[sl][:, None, :], axis=0)
        rows.append(acc)
    o = jnp.concatenate(rows, axis=0)
    kd = kk * jnp.exp(bt - b)
    st_new = jnp.exp(bt) * st + lax.dot_general(iv.astype(BF16), kd.astype(BF16), (((0,), (0,)), ((), ())),
                                                     preferred_element_type=F32)
    out = _rms(o, ng) * (g * jax.nn.sigmoid(g))
    return out, st_new


def _hg_tile(S):
    return min(512, S)


def _hg_in_specs(rt, rev, nr):
    width = HG_HEADS * HG_D
    base = OFF_HG // width
    row = (lambda r: nr - 1 - r) if rev else (lambda r: r)
    return [pl.BlockSpec((rt, width), functools.partial(lambda r, k: (row(r), base + k), k=k)) for k in range(4)]


def _hg_cols(h):
    return slice(h * HG_D, (h + 1) * HG_D)


def _hgrn_fwd(proj, lbk, ng, *, name, comm=None):
    S = proj.shape[0]
    rt = _hg_tile(S)
    nr, nc = S // rt, rt // HG_CHUNK

    def kern(q_ref, f_ref, i_ref, g_ref, lbk_ref, ng_ref, o_ref, st_out_ref, st_ref):
        @pl.when(pl.program_id(0) == 0)
        def _():
            st_ref[...] = jnp.zeros_like(st_ref)

        def body(c, carry):
            rows = pl.ds(pl.multiple_of(c * HG_CHUNK, HG_CHUNK), HG_CHUNK)
            for h in range(HG_HEADS):
                cols = _hg_cols(h)
                st = st_ref[h]
                st_out_ref[h, c] = st
                out, st_new = _hg_chunk(q_ref[rows, cols], f_ref[rows, cols], i_ref[rows, cols], g_ref[rows, cols], st,
                                        lbk_ref[:, cols], ng_ref[...])
                o_ref[rows, cols] = out.astype(o_ref.dtype)
                st_ref[h] = st_new
            return carry

        lax.fori_loop(0, nc, body, 0)

    width = HG_HEADS * HG_D
    in_specs = _hg_in_specs(rt, False, nr) + [pl.BlockSpec((1, width), lambda r: (0, 0)),
                                               pl.BlockSpec((1, HG_D), lambda r: (0, 0))]
    return _hosted_call(
        kern, grid=(nr,), in_specs=in_specs,
        out_specs=[pl.BlockSpec((rt, width), lambda r: (r, 0)),
                   pl.BlockSpec((HG_HEADS, nc, HG_D, HG_D), lambda r: (0, r, 0, 0))],
        out_shape=[jax.ShapeDtypeStruct((S, width), BF16),
                   jax.ShapeDtypeStruct((HG_HEADS, S // HG_CHUNK, HG_D, HG_D), F32)],
        scratch_shapes=[pltpu.VMEM((HG_HEADS, HG_D, HG_D), F32)],
        args=[proj, proj, proj, proj, lbk, ng], name=name, comm=comm, sem=("arbitrary",))


def _hgrn_bwd(proj, states, dout, lbk, ng, *, name, comm=None):
    S = proj.shape[0]
    rt = _hg_tile(S)
    nr, nc = S // rt, rt // HG_CHUNK
    width = HG_HEADS * HG_D

    def kern(q_ref, f_ref, i_ref, g_ref, st_in_ref, do_ref, lbk_ref, ng_ref,
             dq_ref, df_ref, di_ref, dg_ref, dlbk_ref, dng_ref, dst_ref):
        @pl.when(pl.program_id(0) == 0)
        def _():
            dst_ref[...] = jnp.zeros_like(dst_ref)
            dlbk_ref[...] = jnp.zeros_like(dlbk_ref)
            dng_ref[...] = jnp.zeros_like(dng_ref)

        def body(k, carry):
            c = nc - 1 - k
            rows = pl.ds(pl.multiple_of(c * HG_CHUNK, HG_CHUNK), HG_CHUNK)
            for h in range(HG_HEADS):
                cols = _hg_cols(h)
                _, vjp = jax.vjp(_hg_chunk, q_ref[rows, cols], f_ref[rows, cols], i_ref[rows, cols], g_ref[rows, cols],
                                 st_in_ref[h, c], lbk_ref[:, cols], ng_ref[...])
                dq, df, di, dg, dst, dlbk, dng = vjp((do_ref[rows, cols].astype(F32), dst_ref[h]))
                dq_ref[rows, cols] = dq.astype(BF16)
                df_ref[rows, cols] = df.astype(BF16)
                di_ref[rows, cols] = di.astype(BF16)
                dg_ref[rows, cols] = dg.astype(BF16)
                dst_ref[h] = dst
                dlbk_ref[:, cols] += dlbk
                dng_ref[h] += dng
            return carry

        lax.fori_loop(0, nc, body, 0)

    rev = lambda r: nr - 1 - r
    tile = pl.BlockSpec((rt, width), lambda r: (rev(r), 0))
    in_specs = _hg_in_specs(rt, True, nr) + [
        pl.BlockSpec((HG_HEADS, nc, HG_D, HG_D), lambda r: (0, rev(r), 0, 0)), tile,
        pl.BlockSpec((1, width), lambda r: (0, 0)), pl.BlockSpec((1, HG_D), lambda r: (0, 0))]
    return _hosted_call(
        kern, grid=(nr,), in_specs=in_specs,
        out_specs=[tile, tile, tile, tile, pl.BlockSpec((1, width), lambda r: (0, 0)),
                   pl.BlockSpec((HG_HEADS, 1, HG_D), lambda r: (0, 0, 0))],
        out_shape=[jax.ShapeDtypeStruct((S, width), BF16)] * 4 + [
            jax.ShapeDtypeStruct((1, width), F32), jax.ShapeDtypeStruct((HG_HEADS, 1, HG_D), F32)],
        scratch_shapes=[pltpu.VMEM((HG_HEADS, HG_D, HG_D), F32)],
        args=[proj, proj, proj, proj, states, dout, lbk, ng], name=name, comm=comm, sem=("arbitrary",))


def _sb_scores(km, qi):
    return lax.dot_general(km, qi, (((1,), (1,)), ((), ())), preferred_element_type=F32)


def _sb_weights(zt, r_run, diag):
    n = SB_BLK
    sp = jnp.maximum(zt, 0.0) + jnp.log(1.0 + jnp.exp(-jnp.abs(zt)))
    lk = -sp
    if diag:
        keep = (_iota2(zt.shape, 0) & (n - 1)) < _iota2(zt.shape, 1)
        lk = jnp.where(keep, lk, 0.0)
    tails = [_running(lk[a * n:(a + 1) * n], n, True) for a in range(2)]
    between = jnp.concatenate([tails[a] + r_run[a] for a in range(2)], axis=0)
    wgt = jnp.exp(zt + between)
    if diag:
        wgt = jnp.where(keep, wgt, 0.0)
    return sp, wgt, [t[0:1, :] for t in tails]


def _sb_norm_pair(x, g2, lane_lo):
    sq = x * x
    ms_lo = jnp.sum(jnp.where(lane_lo, sq, 0.0), axis=-1, keepdims=True)
    ms_hi = jnp.sum(jnp.where(lane_lo, 0.0, sq), axis=-1, keepdims=True)
    return x * lax.rsqrt(jnp.where(lane_lo, ms_lo, ms_hi) * (1.0 / SB_DH) + EPS) * g2


def _sb_specs(S):
    base = OFF_SB // SB_PAIR
    per = SB_HEADS * SB_DH // SB_PAIR
    cols = [pl.BlockSpec((S, SB_PAIR), functools.partial(lambda p, k: (0, base + per * k + p), k=k)) for k in range(3)]
    return cols + [pl.BlockSpec((1, SB_PAIR), lambda p: (0, 0))] * 2


def _sb_rows(i):
    return pl.ds(pl.multiple_of(i * SB_BLK, SB_BLK), SB_BLK)


def _sb_both(j, a=None):
    if a is None:
        return pl.ds(pl.multiple_of(j * 2 * SB_BLK, 2 * SB_BLK), 2 * SB_BLK)
    return pl.ds(pl.multiple_of(j * 2 * SB_BLK + a * SB_BLK, SB_BLK), SB_BLK)


def _sb_fwd(proj, qg2, kg2, *, name, comm=None):
    S = proj.shape[0]
    nb = S // SB_BLK
    scale = SB_DH ** -0.5
    n_pairs = SB_HEADS * SB_DH // SB_PAIR

    def kern(q_ref, k_ref, v_ref, qg_ref, kg_ref, o_ref, rs_ref, qp_ref, km_ref, vt_ref):
        lane_lo = _iota2((SB_BLK, SB_PAIR), 1) < SB_DH

        def prologue(j, carry):
            rows = _sb_rows(j)
            qp_ref[rows, :] = (_sb_norm_pair(q_ref[rows, :], qg_ref[...], lane_lo) * scale).astype(BF16)
            kn = _sb_norm_pair(k_ref[rows, :], kg_ref[...], lane_lo)
            v = v_ref[rows, :]
            for a, mine in enumerate((lane_lo, jnp.logical_not(lane_lo))):
                km_ref[_sb_both(j, a), :] = jnp.where(mine, kn, 0.0).astype(BF16)
                vt_ref[:, _sb_both(j, a)] = jnp.where(mine, v, 0.0).T.astype(BF16)
            return carry

        lax.fori_loop(0, nb, prologue, 0)

        def qblock(i, carry):
            qi = qp_ref[_sb_rows(i), :]

            scores = lambda j: _sb_scores(km_ref[_sb_both(jnp.maximum(j, 0)), :], qi)
            output = lambda j, wgt: jnp.dot(vt_ref[:, _sb_both(j)], wgt, preferred_element_type=F32)

            def note(j, r_run):
                for a in range(2):
                    rs_ref[a, i, pl.ds(j, 1), :] = r_run[a]
                return jnp.maximum(jnp.max(r_run[0]), jnp.max(r_run[1])) > SB_SKIP

            def noted(j, r_run):
                return lax.cond(j >= 0, lambda: note(j, r_run).astype(jnp.int32), lambda: jnp.int32(0))

            zero = jnp.zeros((1, SB_BLK), F32)
            zt, z_next = scores(i), scores(i - 1)
            _, wgt, r_run = _sb_weights(zt, [zero, zero], True)
            go = noted(i - 1, r_run)

            def body(c):
                j, _, acc, r_run, zt, j_prev, w_prev = c
                z_next = scores(j - 1)
                acc = acc + output(j_prev, w_prev)
                _, wgt, lk_sum = _sb_weights(zt, r_run, False)
                r_run = [r_run[a] + lk_sum[a] for a in range(2)]
                return j - 1, noted(j - 1, r_run), acc, r_run, z_next, j, wgt.astype(BF16)

            c = (i - 1, go, jnp.zeros((SB_PAIR, SB_BLK), F32), r_run, z_next, i, wgt.astype(BF16))
            _, _, acc, _, _, j_prev, w_prev = lax.while_loop(lambda c: c[1] > 0, body, c)
            o_ref[_sb_rows(i), :] = (acc + output(j_prev, w_prev)).T.astype(o_ref.dtype)
            return carry

        lax.fori_loop(0, nb, qblock, 0)

    width = SB_HEADS * SB_DH
    return _hosted_call(
        kern, grid=(n_pairs,), in_specs=_sb_specs(S),
        out_specs=[pl.BlockSpec((S, SB_PAIR), lambda p: (0, p)),
                   pl.BlockSpec((2, nb, nb, SB_BLK), lambda p: (p, 0, 0, 0))],
        out_shape=[jax.ShapeDtypeStruct((S, width), BF16), jax.ShapeDtypeStruct((SB_HEADS, nb, nb, SB_BLK), F32)],
        scratch_shapes=[pltpu.VMEM((S, SB_PAIR), BF16), pltpu.VMEM((2 * S, SB_PAIR), BF16),
                        pltpu.VMEM((SB_PAIR, 2 * S), BF16)],
        args=[proj, proj, proj, qg2, kg2], name=name, comm=comm, sem=("parallel",))


def _sb_bwd(proj, qg2, kg2, rs, do, *, name, comm=None):
    S = proj.shape[0]
    nb = S // SB_BLK
    scale = SB_DH ** -0.5
    n_pairs = SB_HEADS * SB_DH // SB_PAIR

    def kern(q_ref, k_ref, v_ref, qg_ref, kg_ref, rs_ref, do_ref, dq_ref, dk_ref, dv_ref, dqg_ref, dkg_ref,
             qp_ref, km_ref, kt_ref, vm_ref, dqn_ref, dkn_ref, dvs_ref):
        lane_lo = _iota2((SB_BLK, SB_PAIR), 1) < SB_DH
        heads = (lane_lo, jnp.logical_not(lane_lo))
        fn_q = lambda x, g: _sb_norm_pair(x, g, lane_lo) * scale
        fn_k = lambda x, g: _sb_norm_pair(x, g, lane_lo)

        def prologue(j, carry):
            rows = _sb_rows(j)
            qp_ref[rows, :] = fn_q(q_ref[rows, :], qg_ref[...]).astype(BF16)
            kn = fn_k(k_ref[rows, :], kg_ref[...])
            v = v_ref[rows, :]
            for a, mine in enumerate(heads):
                k_a = jnp.where(mine, kn, 0.0)
                km_ref[_sb_both(j, a), :] = k_a.astype(BF16)
                kt_ref[:, _sb_both(j, a)] = k_a.T.astype(BF16)
                vm_ref[_sb_both(j, a), :] = jnp.where(mine, v, 0.0).astype(BF16)
            return carry

        lax.fori_loop(0, nb, prologue, 0)
        dkn_ref[...] = jnp.zeros_like(dkn_ref)
        dvs_ref[...] = jnp.zeros_like(dvs_ref)

        def qblock(i, carry):
            qi = qp_ref[_sb_rows(i), :]
            doi = do_ref[_sb_rows(i), :]

            def opening(j):
                jc = jnp.minimum(j, i)
                return (_sb_scores(km_ref[_sb_both(jc), :], qi),
                        lax.dot_general(vm_ref[_sb_both(jc), :], doi, (((1,), (1,)), ((), ())), preferred_element_type=F32))

            def closing(j, dzb, wgtb, dqa):
                dkn_ref[_sb_both(j), :] += jnp.dot(dzb, qi, preferred_element_type=F32)
                dvs_ref[_sb_both(j), :] += jnp.dot(wgtb, doi, preferred_element_type=F32)
                return dqa + jnp.dot(kt_ref[:, _sb_both(j)], dzb, preferred_element_type=F32)

            def middle(j, diag, zt, dp, e_run):
                zero = jnp.zeros((1, SB_BLK), F32)
                r_run = [zero, zero] if diag else [rs_ref[a, i, pl.ds(j, 1), :] for a in range(2)]
                sp, wgt, _ = _sb_weights(zt, r_run, diag)
                e = dp * wgt
                heads_e = [_running(e[a * SB_BLK:(a + 1) * SB_BLK], SB_BLK, False) for a in range(2)]
                e_left = jnp.concatenate([heads_e[a] + e_run[a] for a in range(2)], axis=0) - e
                s_neg = jnp.exp(-sp)
                dz = e * s_neg - e_left * (1.0 - s_neg)
                if diag:
                    dz = jnp.where((_iota2(dz.shape, 0) & (SB_BLK - 1)) < _iota2(dz.shape, 1), dz, 0.0)
                return dz.astype(BF16), wgt.astype(BF16), [e_run[a] + heads_e[a][SB_BLK - 1:SB_BLK, :] for a in range(2)]

            def live(j):
                jc = jnp.maximum(j, 0)
                top = jnp.maximum(jnp.max(rs_ref[0, i, pl.ds(jc, 1), :]), jnp.max(rs_ref[1, i, pl.ds(jc, 1), :]))
                return jnp.logical_and(j >= 0, top > SB_SKIP).astype(jnp.int32)

            first, _ = lax.while_loop(lambda c: c[1] > 0, lambda c: (c[0] - 1, live(c[0] - 2)), (i, live(i - 1)))

            def body(j, c):
                dqa, e_run, zt, dp, j_prev, dzb, wgtb = c
                nxt = opening(j + 1)
                dqa = closing(j_prev, dzb, wgtb, dqa)
                dzb, wgtb, e_run = middle(j, False, zt, dp, e_run)
                return (dqa, e_run) + nxt + (j, dzb, wgtb)

            zero = jnp.zeros((1, SB_BLK), F32)
            none = jnp.zeros((2 * SB_BLK, SB_BLK), BF16)
            c = (jnp.zeros((SB_PAIR, SB_BLK), F32), [zero, zero]) + opening(first) + (first, none, none)
            dqa, e_run, zt, dp, j_prev, dzb, wgtb = lax.fori_loop(first, i, body, c)
            dqa = closing(j_prev, dzb, wgtb, dqa)
            dzb, wgtb, _ = middle(i, True, zt, dp, e_run)
            dqn_ref[_sb_rows(i), :] = closing(i, dzb, wgtb, dqa).T
            return carry

        lax.fori_loop(0, nb, qblock, 0)
        dqg_ref[...] = jnp.zeros_like(dqg_ref)
        dkg_ref[...] = jnp.zeros_like(dkg_ref)

        def epilogue(j, carry):
            rows = _sb_rows(j)
            _, vjp_q = jax.vjp(fn_q, q_ref[rows, :], qg_ref[...])
            dq, dqg = vjp_q(dqn_ref[rows, :])
            _, vjp_k = jax.vjp(fn_k, k_ref[rows, :], kg_ref[...])
            dk, dkg = vjp_k(jnp.where(lane_lo, dkn_ref[_sb_both(j, 0), :], dkn_ref[_sb_both(j, 1), :]))
            dq_ref[rows, :] = dq.astype(BF16)
            dk_ref[rows, :] = dk.astype(BF16)
            dv_ref[rows, :] = jnp.where(lane_lo, dvs_ref[_sb_both(j, 0), :], dvs_ref[_sb_both(j, 1), :]).astype(BF16)
            dqg_ref[0] += dqg
            dkg_ref[0] += dkg
            return carry

        lax.fori_loop(0, nb, epilogue, 0)

    width = SB_HEADS * SB_DH
    pair = pl.BlockSpec((S, SB_PAIR), lambda p: (0, p))
    dgain = pl.BlockSpec((1, 1, SB_PAIR), lambda p: (p, 0, 0))
    in_specs = _sb_specs(S) + [pl.BlockSpec((2, nb, nb, SB_BLK), lambda p: (p, 0, 0, 0)), pair]
    return _hosted_call(
        kern, grid=(n_pairs,), in_specs=in_specs, out_specs=[pair, pair, pair, dgain, dgain],
        out_shape=[jax.ShapeDtypeStruct((S, width), BF16)] * 3 + [jax.ShapeDtypeStruct((n_pairs, 1, SB_PAIR), F32)] * 2,
        scratch_shapes=[pltpu.VMEM((S, SB_PAIR), BF16), pltpu.VMEM((2 * S, SB_PAIR), BF16), pltpu.VMEM((SB_PAIR, 2 * S), BF16),
                        pltpu.VMEM((2 * S, SB_PAIR), BF16), pltpu.VMEM((S, SB_PAIR), F32),
                        pltpu.VMEM((2 * S, SB_PAIR), F32), pltpu.VMEM((2 * S, SB_PAIR), F32)],
        args=[proj, proj, proj, qg2, kg2, rs, do], name=name, comm=comm, sem=("parallel",))


def _loss_head(y, target, *, name):
    S, D = y.shape
    T = min(ROW_TILE, S)

    def kern(y_ref, t_ref, dy_ref, acc_ref):
        err = y_ref[...] - t_ref[...]
        dy_ref[...] = err * (1.0 / D)
        col = jnp.sum(err * err, axis=0, keepdims=True)
        part = sum(col[:, k * 128:(k + 1) * 128] for k in range(D // 128))

        @pl.when(pl.program_id(0) == 0)
        def _():
            acc_ref[...] = part

        @pl.when(pl.program_id(0) > 0)
        def _():
            acc_ref[...] += part

    tile = pl.BlockSpec((T, D), lambda i: (i, 0))
    return pl.pallas_call(
        kern, grid=(S // T,), in_specs=[tile, tile], out_specs=[tile, pl.BlockSpec((1, 128), lambda i: (0, 0))],
        out_shape=[jax.ShapeDtypeStruct((S, D), F32), jax.ShapeDtypeStruct((1, 128), F32)],
        name=name, compiler_params=_cparams(("arbitrary",)))(y, target)


def _adamw_math(w, g, m, v):
    m = ADAM_B1 * m + (1.0 - ADAM_B1) * g
    v = ADAM_B2 * v + (1.0 - ADAM_B2) * jnp.square(g)
    m_hat = m / (1.0 - ADAM_B1 ** ADAM_STEP)
    v_hat = v / (1.0 - ADAM_B2 ** ADAM_STEP)
    return -ADAM_LR * (m_hat / (jnp.sqrt(v_hat) + ADAM_EPS) + ADAM_WD * w), m, v


def _adamw(w, g, m, v, *, name):
    R, C = w.shape
    T = _pick(R, (256, 128, 64, 32, 16, 8))

    def kern(w_ref, g_ref, m_ref, v_ref, d_ref, mo_ref, vo_ref):
        d, mn, vn = _adamw_math(w_ref[...], g_ref[...], m_ref[...], v_ref[...])
        d_ref[...] = d
        mo_ref[...] = mn
        vo_ref[...] = vn

    tile = pl.BlockSpec((T, C), lambda i: (i, 0))
    return pl.pallas_call(
        kern, grid=(R // T,), in_specs=[tile] * 4, out_specs=[tile] * 3,
        out_shape=[jax.ShapeDtypeStruct((R, C), F32)] * 3, name=name,
        compiler_params=_cparams(("parallel",)))(w, g, m, v)


def _adamw_layer(w, g, m, v, layer, prev, *, name):
    L, R, C = w.shape
    T = _pick(R, (256, 128, 64, 32, 16, 8))

    def kern(w_ref, g_ref, m_ref, v_ref, *rest):
        go_ref, d_ref, mo_ref, vo_ref = rest[-4:]
        grad = g_ref[...]
        d, mn, vn = _adamw_math(w_ref[...], grad, m_ref[...], v_ref[...])
        go_ref[...] = grad
        d_ref[...] = d
        mo_ref[...] = mn
        vo_ref[...] = vn

    layer_tile = pl.BlockSpec((None, T, C), lambda i: (layer, i, 0))
    in_specs = [layer_tile, pl.BlockSpec((T, C), lambda i: (i, 0)), layer_tile, layer_tile]
    args, aliases = [w, g, m, v], {}
    if prev is not None:
        in_specs += [pl.BlockSpec(memory_space=pl.ANY)] * 4
        args += list(prev)
        aliases = {4 + k: k for k in range(4)}
    return pl.pallas_call(
        kern, grid=(R // T,), in_specs=in_specs, out_specs=[layer_tile] * 4,
        out_shape=[jax.ShapeDtypeStruct((L, R, C), F32)] * 4, input_output_aliases=aliases, name=name,
        compiler_params=_cparams(("parallel",)))(*args)


def _sum8(g, *, name):
    def kern(g_ref, o_ref):
        acc = g_ref[0]
        for d in range(1, g.shape[0]):
            acc = acc + g_ref[d]
        o_ref[...] = acc

    return pl.pallas_call(kern, out_shape=jax.ShapeDtypeStruct(g.shape[1:], F32), name=name,
                          compiler_params=_cparams())(g)


def _place():
    return lax.axis_index("x"), lax.axis_index("y"), lax.axis_index("c")


def _other_chips(x, y):
    return [(1 - x, y), (x, 1 - y), (1 - x, 1 - y)]


def _remote(src, dst, send_sems, recv_sems, k, to):
    return pltpu.make_async_remote_copy(src_ref=src, dst_ref=dst, send_sem=send_sems.at[k], recv_sem=recv_sems.at[k],
                                        device_id=to, device_id_type=MESH)


def _all_gather_small(v, *, name):
    def body(x_ref, out_ref, send_sems, recv_sems, local_sem):
        x, y, c = _place()
        me = 4 * x + 2 * y + c
        mine = pltpu.make_async_copy(x_ref, out_ref.at[me], local_sem)
        mine.start()
        peers = []
        for f in range(1, 8):
            peers.append((1 - x if f & 4 else x, 1 - y if f & 2 else y, 1 - c if f & 1 else c))
        sends = [_remote(x_ref, out_ref.at[me], send_sems, recv_sems, k, p) for k, p in enumerate(peers)]
        for cp in sends:
            cp.start()
        for k, (px, py, pc) in enumerate(peers):
            _remote(x_ref, out_ref.at[4 * px + 2 * py + pc], send_sems, recv_sems, k, (px, py, pc)).wait_recv()
        for cp in sends:
            cp.wait_send()
        mine.wait()

    return pl.pallas_call(
        body, out_shape=jax.ShapeDtypeStruct((8,) + v.shape, v.dtype),
        in_specs=[pl.BlockSpec(memory_space=pltpu.VMEM)], out_specs=pl.BlockSpec(memory_space=pltpu.VMEM),
        scratch_shapes=[pltpu.SemaphoreType.DMA((7,)), pltpu.SemaphoreType.DMA((7,)), pltpu.SemaphoreType.DMA],
        name=name, compiler_params=_cparams())(v)


def _hosted_call(kern, *, grid, in_specs, out_specs, out_shape, scratch_shapes, args, name, comm=None, sem=None):
    if comm is None:
        res = pl.pallas_call(kern, grid=grid, in_specs=in_specs, out_specs=out_specs, out_shape=out_shape,
                             scratch_shapes=scratch_shapes, name=name, compiler_params=_cparams(sem))(*args)
        return list(res), []
    n_in, n_out, n_scr = len(in_specs), len(out_specs), len(scratch_shapes)
    c_in, c_out = len(comm.inputs), len(comm.out_shapes)

    def body(*refs):
        ins, ci = refs[:n_in], refs[n_in:n_in + c_in]
        outs = refs[n_in + c_in:n_in + c_in + n_out]
        co = refs[n_in + c_in + n_out:n_in + c_in + n_out + c_out]
        scr = refs[n_in + c_in + n_out + c_out:n_in + c_in + n_out + c_out + n_scr]
        cs = refs[n_in + c_in + n_out + c_out + n_scr:]
        ids = [pl.program_id(d) for d in range(len(grid))]
        inner_first = functools.reduce(jnp.logical_and, [i == 0 for i in ids[1:]], True)
        inner_last = functools.reduce(jnp.logical_and, [i == n - 1 for i, n in zip(ids[1:], grid[1:])], True)

        @pl.when(jnp.logical_and(ids[0] == 0, inner_first))
        def _():
            comm.begin(ci, co, cs)

        kern(*ins, *outs, *scr)

        @pl.when(jnp.logical_and(ids[0] == grid[0] // 2, inner_last))
        def _():
            comm.middle(ci, co, cs)

        @pl.when(jnp.logical_and(ids[0] == grid[0] - 1, inner_last))
        def _():
            comm.end(ci, co, cs)

    hbm = pl.BlockSpec(memory_space=pltpu.HBM)
    res = pl.pallas_call(
        body, grid=grid, in_specs=list(in_specs) + [hbm] * c_in, out_specs=list(out_specs) + [hbm] * c_out,
        out_shape=list(out_shape) + list(comm.out_shapes), scratch_shapes=list(scratch_shapes) + list(comm.scratch),
        input_output_aliases={n_in + i: n_out + o for i, o in comm.aliases.items()},
        name=name, compiler_params=_cparams(("arbitrary",) * len(grid)))(*args, *comm.inputs)
    return list(res[:n_out]), list(res[n_out:])


def _run_comm(comm, *, name):
    return _hosted_call(lambda: None, grid=(1,), in_specs=[], out_specs=[], out_shape=[], scratch_shapes=[], args=[],
                        name=name, comm=comm)[1]


class _Gather:
    def __init__(self, shards, kinds, items):
        used = sorted({w for w, _ in items})
        self.slot = {w: k for k, w in enumerate(used)}
        self.inputs = [shards[w] for w in used]
        self.items, self.kinds = list(items), kinds
        self.shapes = {w: shards[w].shape[1:] for w in used}
        self.out_shapes = [jax.ShapeDtypeStruct((r, 4 * n) if kinds[w] == "col" else (4 * r, n), shards[w].dtype)
                           for w, _ in items for r, n in [self.shapes[w]]]
        n_items = len(items)
        self.scratch = [pltpu.SemaphoreType.DMA((6 * n_items,)), pltpu.SemaphoreType.DMA((6 * n_items,)),
                        pltpu.SemaphoreType.DMA((n_items,))]
        self.aliases = {}

    def _piece(self, ref, w, qq, half):
        r, n = self.shapes[w]
        h = r // 2
        lo, size = (0, r) if half is None else (half * h, h)
        if self.kinds[w] == "col":
            return ref.at[pl.ds(pl.multiple_of(lo, 16), size), pl.ds(pl.multiple_of(qq * n, 128), n)]
        return ref.at[pl.ds(pl.multiple_of(qq * r + lo, 16), size), :]

    def _mine(self, ci, w, l, half):
        h = self.shapes[w][0] // 2
        return ci[self.slot[w]].at[l, pl.ds(pl.multiple_of(half * h, 16), h), :]

    def begin(self, ci, co, cs):
        send_sems, recv_sems, local_sems = cs
        x, y, c = _place()
        q = 2 * x + y
        for k, (w, l) in enumerate(self.items):
            pltpu.make_async_copy(ci[self.slot[w]].at[l], self._piece(co[k], w, q, None), local_sems.at[k]).start()
            for j, (cx, cy) in enumerate(_other_chips(x, y)):
                _remote(self._mine(ci, w, l, c), self._piece(co[k], w, q, c), send_sems, recv_sems, 6 * k + j,
                        (cx, cy, c)).start()

    def middle(self, ci, co, cs):
        send_sems, recv_sems, _ = cs
        x, y, c = _place()
        for k, (w, l) in enumerate(self.items):
            for j, (cx, cy) in enumerate(_other_chips(x, y)):
                win = self._piece(co[k], w, 2 * cx + cy, c)
                _remote(win, win, send_sems, recv_sems, 6 * k + j, (cx, cy, c)).wait_recv()
                _remote(win, win, send_sems, recv_sems, 6 * k + 3 + j, (x, y, 1 - c)).start()

    def end(self, ci, co, cs):
        send_sems, recv_sems, local_sems = cs
        x, y, c = _place()
        q = 2 * x + y
        for k, (w, l) in enumerate(self.items):
            for j, (cx, cy) in enumerate(_other_chips(x, y)):
                win = self._piece(co[k], w, 2 * cx + cy, 1 - c)
                _remote(win, win, send_sems, recv_sems, 6 * k + 3 + j, (x, y, 1 - c)).wait_recv()
        for k, (w, l) in enumerate(self.items):
            for j, (cx, cy) in enumerate(_other_chips(x, y)):
                _remote(self._mine(ci, w, l, c), self._piece(co[k], w, q, c), send_sems, recv_sems, 6 * k + j,
                        (cx, cy, c)).wait_send()
                win = self._piece(co[k], w, 2 * cx + cy, c)
                _remote(win, win, send_sems, recv_sems, 6 * k + 3 + j, (x, y, 1 - c)).wait_send()
            pltpu.make_async_copy(ci[self.slot[w]].at[l], self._piece(co[k], w, q, None), local_sems.at[k]).wait()


def _half_rows(ref, half, h):
    return ref.at[:, pl.ds(pl.multiple_of(half * h, 16), h), :]


class _Copies:
    def __init__(self, inputs, out_shapes, count, pairs, aliases=None):
        self.inputs, self.out_shapes, self.pairs = list(inputs), list(out_shapes), pairs
        self.scratch = [pltpu.SemaphoreType.DMA((count,)), pltpu.SemaphoreType.DMA((count,))]
        self.aliases = aliases or {}

    def _copies(self, ci, co, cs):
        x, y, c = _place()
        return [_remote(src, dst, cs[0], cs[1], k, to) for k, (src, dst, to) in enumerate(self.pairs(ci, co, x, y, c))]

    def begin(self, ci, co, cs):
        for cp in self._copies(ci, co, cs):
            cp.start()

    def middle(self, ci, co, cs):
        pass

    def end(self, ci, co, cs):
        for cp in self._copies(ci, co, cs):
            cp.wait()


def _swap_halves(gs):
    def pairs(ci, co, x, y, c):
        return [(_half_rows(ci[k], 1 - c, g.shape[1] // 2), co[k], (x, y, 1 - c)) for k, g in enumerate(gs)]

    return _Copies(gs, [jax.ShapeDtypeStruct((g.shape[0], g.shape[1] // 2, g.shape[2]), g.dtype) for g in gs],
                   len(gs), pairs)


def _scatter_quarters(ps, kinds):
    part = [((p.shape[1], p.shape[2] // 4) if kind == "col" else (p.shape[1], p.shape[2])) for p, kind in zip(ps, kinds)]

    def pairs(ci, co, x, y, c):
        out = []
        for k, kind in enumerate(kinds):
            n = part[k][1]
            for j, (cx, cy) in enumerate(_other_chips(x, y)):
                qj = 2 * cx + cy
                src = ci[k].at[0, :, pl.ds(pl.multiple_of(qj * n, 128), n)] if kind == "col" else ci[k].at[qj]
                out.append((src, co[k].at[j], (cx, cy, c)))
        return out

    return _Copies(ps, [jax.ShapeDtypeStruct((3,) + pt, p.dtype) for pt, p in zip(part, ps)], 3 * len(ps), pairs)


def _share_halves(gs):
    def rows(co, k, half):
        h = gs[k].shape[0] // 2
        return co[k].at[pl.ds(pl.multiple_of(half * h, 16), h), :]

    def pairs(ci, co, x, y, c):
        return [(rows(co, k, c), rows(co, k, c), (x, y, 1 - c)) for k in range(len(gs))]

    prog = _Copies(gs, [jax.ShapeDtypeStruct(g.shape, g.dtype) for g in gs], len(gs), pairs,
                   aliases={k: k for k in range(len(gs))})

    def end(ci, co, cs):
        x, y, c = _place()
        for k in range(len(gs)):
            cp = _remote(rows(co, k, c), rows(co, k, 1 - c), cs[0], cs[1], k, (x, y, 1 - c))
            cp.wait_send()
            cp.wait_recv()

    prog.end = end
    return prog


def _wide_tile(n):
    return _pick(n, (2048, 1920, 1024, 512, 256, 128))


def _pair_sum(g, land, place, *, name):
    B, R, N = g.shape
    h = R // 2
    tr, tc = _pick(h, (256, 128)), _wide_tile(N)

    def kern(place_ref, g_ref, l_ref, o_ref):
        o_ref[...] = (g_ref[...] + l_ref[...]).astype(o_ref.dtype)

    grid_spec = pltpu.PrefetchScalarGridSpec(
        num_scalar_prefetch=1, grid=(B, h // tr, N // tc),
        in_specs=[pl.BlockSpec((None, tr, tc), lambda b, i, j, p: (b, p[1] * (h // tr) + i, j)),
                  pl.BlockSpec((None, tr, tc), lambda b, i, j, p: (b, i, j))],
        out_specs=pl.BlockSpec((None, tr, tc), lambda b, i, j, p: (b, i, j)))
    return pl.pallas_call(kern, grid_spec=grid_spec, out_shape=jax.ShapeDtypeStruct((B, h, N), BF16), name=name,
                          compiler_params=_cparams(("parallel", "parallel", "parallel")))(place, g, land)


def _quarter_sum(p, land, kind, shard_shape, place, *, name):
    L, r, n = shard_shape
    h = r // 2
    tr, tc = _pick(h, (256, 128)), _wide_tile(n)

    def kern(place_ref, p_ref, a_ref, b_ref, c_ref, o_ref):
        o_ref[...] = ((p_ref[...].astype(F32) + a_ref[...].astype(F32)) + b_ref[...].astype(F32)) + c_ref[...].astype(F32)

    if kind == "col":
        p_spec = pl.BlockSpec((None, tr, tc), lambda l, i, j, pr: (l, i, pr[0] * (n // tc) + j))
    else:
        p_spec = pl.BlockSpec((None, None, tr, tc), lambda l, i, j, pr: (l, pr[0], i, j))
    lands = [pl.BlockSpec((None, None, tr, tc), functools.partial(lambda l, i, j, pr, s: (s, l, i, j), s=s))
             for s in range(3)]
    grid_spec = pltpu.PrefetchScalarGridSpec(
        num_scalar_prefetch=1, grid=(L, h // tr, n // tc), in_specs=[p_spec] + lands,
        out_specs=pl.BlockSpec((None, tr, tc), lambda l, i, j, pr: (l, pr[1] * (h // tr) + i, j)))
    return pl.pallas_call(kern, grid_spec=grid_spec, out_shape=jax.ShapeDtypeStruct((L, r, n), F32), name=name,
                          compiler_params=_cparams(("parallel", "parallel", "parallel")))(place, p, land, land, land)


class _ReduceScatter:
    def __init__(self, grads, kinds, shard_shapes, place, tag):
        self.kinds, self.shapes, self.place, self.tag = kinds, shard_shapes, place, tag
        self.g3 = [g[None] if kind == "col" else g.reshape(4, g.shape[0] // 4, g.shape[1]) for g, kind in zip(grads, kinds)]

    def swap(self):
        return _swap_halves(self.g3)

    def pair_sums(self, lands):
        self.ps = [_pair_sum(g, land, self.place, name=f"rs_pair_sum_{self.tag}_{k}")
                   for k, (g, land) in enumerate(zip(self.g3, lands))]

    def scatter(self):
        return _scatter_quarters(self.ps, self.kinds)

    def quarter_sums(self, parts):
        self.halves = []
        for k, (p, part) in enumerate(zip(self.ps, parts)):
            p4 = p if self.kinds[k] == "col" else p[None]
            out = _quarter_sum(p4, part[:, None], self.kinds[k], (1,) + tuple(self.shapes[k]), self.place,
                               name=f"rs_quarter_sum_{self.tag}_{k}")
            self.halves.append(out[0])

    def share(self):
        return _share_halves(self.halves)

    def run(self):
        self.pair_sums(_run_comm(self.swap(), name=f"rs_swap_{self.tag}"))
        self.quarter_sums(_run_comm(self.scatter(), name=f"rs_scatter_{self.tag}"))
        return _run_comm(self.share(), name=f"rs_share_{self.tag}")


_WEIGHTS = ["mod_w", "mod_b", "norm1_g", "w_in", "gate_b", "conv_w", "conv_b", "conv_ln_g", "conv_ln_b", "w_conv_proj",
            "hgrn_lb", "hgrn_norm_g", "w_hgrn_proj", "sb_qn_g", "sb_kn_g", "w_sb_proj", "w_out", "norm2_g", "mlp_w1",
            "mlp_w2"]
_BIG = [("w_in", "col"), ("w_conv_proj", "col"), ("w_hgrn_proj", "col"), ("w_sb_proj", "col"), ("w_out", "row"),
        ("mlp_w1", "col"), ("mlp_w2", "row")]
_REPLICATED = ["mod_b", "norm1_g", "gate_b", "conv_b", "conv_ln_g", "conv_ln_b", "hgrn_lb", "hgrn_norm_g", "sb_qn_g",
               "sb_kn_g", "norm2_g"]
LANES = 128


class _Pack:
    def __init__(self, items):
        self.shapes = {n: a.shape for n, a in items}
        self.offsets, pos = {}, 0
        for n, a in items:
            self.offsets[n] = pos
            pos += math.prod(a.shape)
        self.rows = -(-pos // (8 * LANES)) * 8
        flat = jnp.concatenate([a.reshape(-1).astype(F32) for _, a in items])
        self.array = jnp.pad(flat, (0, self.rows * LANES - pos)).reshape(self.rows, LANES)

    def get(self, packed, name):
        lead = packed.shape[:-2]
        flat = packed.reshape(lead + (self.rows * LANES,))
        n = math.prod(self.shapes[name])
        return lax.slice_in_dim(flat, self.offsets[name], self.offsets[name] + n, axis=len(lead)).reshape(
            lead + self.shapes[name])


def _lower_bounds(hgrn_lb):
    p = jax.nn.softmax(hgrn_lb.astype(F32), axis=0)
    return jnp.cumsum(p, axis=0) - p[0:1]


def _layer_fwd(x, w, p, l, comms=(None, None)):
    S, D = x.shape
    r = {"x": x}
    (r["h"],) = _rowop(_fn_normmod, [(x, 0, D)], [p["n1g"], p["sc1"], p["sh1"]], [(D, BF16)], name=f"normmod1_fwd_{l}")
    proj = r["proj"] = _matmul(r["h"], w["w_in", l], name=f"w_in_fwd_{l}")
    r["cpre"] = _conv_fwd(proj, p["w32"], p["conv_b"], name=f"conv_fwd_{l}")
    (r["cact"],) = _rowop(_fn_lnsilu, [(r["cpre"], 0, CONV_CH)], [p["lng"], p["lnb"]], [(CONV_CH, BF16)],
                          name=f"conv_ln_fwd_{l}")
    arrived = lambda comm, got: w.update({(_BIG[k][0], layer): arr for (k, layer), arr in zip(comm.items, got)})
    (r["hg"], r["states"]), got = _hgrn_fwd(proj, p["lbk"], p["ng"], name=f"hgrn_fwd_{l}", comm=comms[0])
    if comms[0] is not None:
        arrived(comms[0], got)
    (r["sb"], r["rs"]), got = _sb_fwd(proj, p["qg"], p["kg"], name=f"sb_fwd_{l}", comm=comms[1])
    if comms[1] is not None:
        arrived(comms[1], got)
    r["y_c"] = _matmul(r["cact"], w["w_conv_proj", l], name=f"w_conv_proj_fwd_{l}")
    r["y_h"] = _matmul(r["hg"], w["w_hgrn_proj", l], name=f"w_hgrn_proj_fwd_{l}")
    r["y_s"] = _matmul(r["sb"], w["w_sb_proj", l], name=f"w_sb_proj_fwd_{l}")
    (r["merged"],) = _rowop(_fn_merge, [(proj, OFF_GL, 3 * D), (r["y_c"], 0, D), (r["y_h"], 0, D), (r["y_s"], 0, D)],
                            [p["gate_b"]], [(D, BF16)], name=f"merge_fwd_{l}")
    resid = lambda y, x_in, gate: (y,) + _fn_resid(x_in, y, gate)
    r["a_out"], r["x1"] = _matmul(r["merged"], w["w_out", l], name=f"w_out_fwd_{l}", post=resid, extras=[x],
                                  rows=[p["g1"]], out_dtypes=(F32, F32))
    (r["h2"],) = _rowop(_fn_normmod, [(r["x1"], 0, D)], [p["n2g"], p["sc2"], p["sh2"]], [(D, BF16)],
                        name=f"normmod2_fwd_{l}")
    r["u"], r["act"] = _matmul(r["h2"], w["mlp_w1", l], name=f"mlp_w1_fwd_{l}", post=lambda u: (u,) + _fn_relu2(u),
                               out_dtypes=(F32, BF16))
    r["m_out"], x2 = _matmul(r["act"], w["mlp_w2", l], name=f"mlp_w2_fwd_{l}", post=resid, extras=[r["x1"]],
                             rows=[p["g2"]], out_dtypes=(F32, F32))
    return x2, r


def _layer_bwd(dx2, r, w, p, l, grads, carry=None, last=None):
    S, D = dx2.shape
    small = {}

    def dweight(name, a, dy):
        grads[name, l] = _matmul(a, dy, ta=True, name=f"{name}_dw_{l}")

    stage = (lambda k, got: carry(k, got)) if carry is not None else (lambda k, got: None)

    (dm_out,), (dg2,) = _rowop_bwd(_fn_scale, [(r["m_out"], 0, D)], [p["g2"]], [dx2], [BF16], name=f"resid2_bwd_{l}")
    (du,) = _matmul(dm_out, w["mlp_w2", l], tb=True, name=f"mlp_w2_dx_{l}", extras=[r["u"]], out_dtypes=(BF16,),
                    post=lambda dact, u: (dact * (2.0 * jnp.maximum(u, 0.0)),))
    dweight("mlp_w2", r["act"], dm_out)
    dh2 = _matmul(du, w["mlp_w1", l], tb=True, name=f"mlp_w1_dx_{l}")
    dweight("mlp_w1", r["h2"], du)
    (dx1,), (small["norm2_g"], dsc2, dsh2) = _rowop_bwd(
        _fn_normmod, [(r["x1"], 0, D)], [p["n2g"], p["sc2"], p["sh2"]], [dh2], [F32], add={0: dx2},
        name=f"normmod2_bwd_{l}")
    (da_out,), (dg1,) = _rowop_bwd(_fn_scale, [(r["a_out"], 0, D)], [p["g1"]], [dx1], [BF16], name=f"resid1_bwd_{l}")
    dmerged = _matmul(da_out, w["w_out", l], tb=True, name=f"w_out_dx_{l}")
    dweight("w_out", r["merged"], da_out)
    (dgl, dy_c, dy_h, dy_s), (small["gate_b"],) = _rowop_bwd(
        _fn_merge, [(r["proj"], OFF_GL, 3 * D), (r["y_c"], 0, D), (r["y_h"], 0, D), (r["y_s"], 0, D)], [p["gate_b"]],
        [dmerged], [BF16] * 4, name=f"merge_bwd_{l}")
    dweight("w_conv_proj", r["cact"], dy_c)
    dweight("w_hgrn_proj", r["hg"], dy_h)
    dweight("w_sb_proj", r["sb"], dy_s)
    dcact = _matmul(dy_c, w["w_conv_proj", l], tb=True, name=f"w_conv_proj_dx_{l}")
    (dcpre,), (small["conv_ln_g"], small["conv_ln_b"]) = _rowop_bwd(
        _fn_lnsilu, [(r["cpre"], 0, CONV_CH)], [p["lng"], p["lnb"]], [dcact], [F32], name=f"conv_ln_bwd_{l}")
    (d_conv, dw32, small["conv_b"]), got = _conv_bwd(r["proj"], dcpre, p["w32"], name=f"conv_bwd_{l}",
                                                      comm=stage(0, None))
    small["conv_w"] = dw32[:CONV_WIDTH]
    dhg = _matmul(dy_h, w["w_hgrn_proj", l], tb=True, out_dtype=BF16, name=f"w_hgrn_proj_dx_{l}")
    (dq, df, di, dg, dlbk, dng), got = _hgrn_bwd(r["proj"], r["states"], dhg, p["lbk"], p["ng"], name=f"hgrn_bwd_{l}",
                                                 comm=stage(1, got))
    small["lower"] = -dlbk
    small["hgrn_norm_g"] = jnp.sum(dng, axis=0)
    dsb = _matmul(dy_s, w["w_sb_proj", l], tb=True, out_dtype=BF16, name=f"w_sb_proj_dx_{l}")
    (dsq, dsk, dsv, dqg, dkg), got = _sb_bwd(r["proj"], p["qg"], p["kg"], r["rs"], dsb, name=f"sb_bwd_{l}",
                                             comm=stage(2, got))
    stage(3, got)
    fold = lambda t: jnp.sum(t.reshape(-1, SB_DH), axis=0, keepdims=True)
    small["sb_qn_g"], small["sb_kn_g"] = fold(dqg), fold(dkg)
    dproj = jnp.concatenate([d_conv, dq, df, di, dg, dsq, dsk, dsv, dgl], axis=1)
    dweight("w_in", r["h"], dproj)
    norm1 = functools.partial(_rowop_bwd, _fn_normmod, [(r["x"], 0, D)], [p["n1g"], p["sc1"], p["sh1"]],
                              din_dtypes=[F32], add={0: dx1}, name=f"normmod1_bwd_{l}")
    if last is None:
        dh = _matmul(dproj, w["w_in", l], tb=True, name=f"w_in_dx_{l}")
        (dx,), (small["norm1_g"], dsc1, dsh1) = norm1(douts=[dh])
    else:
        dh, got = _matmul(dproj, w["w_in", l], tb=True, name=f"w_in_dx_{l}", comm=last(0, None))
        (dx,), (small["norm1_g"], dsc1, dsh1), got = norm1(douts=[dh], comm=last(1, got))
        last(2, got)
    small["mod"] = jnp.concatenate([dsh1, dsc1, dg1, dsh2, dsc2, dg2], axis=1)
    return dx, small


def kernel(x, c, mod_w, mod_b, norm1_g, w_in, gate_b, conv_w, conv_b, conv_ln_g, conv_ln_b, w_conv_proj, hgrn_lb, hgrn_norm_g, w_hgrn_proj, sb_qn_g, sb_kn_g, w_sb_proj, w_out, norm2_g, mlp_w1, mlp_w2, loss_target, m_mod_w, m_mod_b, m_norm1_g, m_w_in, m_gate_b, m_conv_w, m_conv_b, m_conv_ln_g, m_conv_ln_b, m_w_conv_proj, m_hgrn_lb, m_hgrn_norm_g, m_w_hgrn_proj, m_sb_qn_g, m_sb_kn_g, m_w_sb_proj, m_w_out, m_norm2_g, m_mlp_w1, m_mlp_w2, v_mod_w, v_mod_b, v_norm1_g, v_w_in, v_gate_b, v_conv_w, v_conv_b, v_conv_ln_g, v_conv_ln_b, v_w_conv_proj, v_hgrn_lb, v_hgrn_norm_g, v_w_hgrn_proj, v_sb_qn_g, v_sb_kn_g, v_w_sb_proj, v_w_out, v_norm2_g, v_mlp_w1, v_mlp_w2):
    given = dict(locals())
    wts = {n: given[n] for n in _WEIGHTS}
    mom = {n: given["m_" + n] for n in _WEIGHTS}
    var = {n: given["v_" + n] for n in _WEIGHTS}
    n_layers, D = norm1_g.shape
    xi, yi, ci = _place()
    q = 2 * xi + yi
    me = 4 * xi + 2 * yi + ci
    place = jnp.stack([q, ci]).astype(jnp.int32)
    n_mod = mod_w.shape[2]
    cw = conv_w.shape[2]

    pk1 = _Pack([("c", c), ("conv_w", conv_w)])
    got1 = _all_gather_small(pk1.array, name="gather_cond")
    c_act = jax.nn.silu(pk1.get(got1, "c")[:, 0, :])
    conv_full = jnp.concatenate([pk1.get(got1, "conv_w")[2 * k] for k in range(4)], axis=-1)

    mod_cols = []
    for l in range(n_layers):
        mb = lax.dynamic_slice_in_dim(mod_b[l], q * n_mod, n_mod)
        mod_cols.append(_matmul(c_act, mod_w, bl=l, name=f"mod_fwd_{l}") + mb[None, :])
    got2 = _all_gather_small(jnp.concatenate(mod_cols, axis=0), name="gather_mod")
    mods = []
    for l in range(n_layers):
        row = lax.dynamic_index_in_dim(got2[0::2], l * 8 + me, axis=1, keepdims=False)
        mods.append(jnp.split(row.reshape(1, 4 * n_mod), 6, axis=1))

    lower, lower_vjp = jax.vjp(_lower_bounds, hgrn_lb)

    shards = [wts[n].astype(BF16) for n, _ in _BIG]
    kinds = [k for _, k in _BIG]
    index = {n: k for k, (n, _) in enumerate(_BIG)}
    first = ["w_in", "w_conv_proj", "w_hgrn_proj", "w_sb_proj"]

    def gather(*names_layers):
        items = [(index[n], l) for names, l in names_layers for n in names if l < n_layers]
        return _Gather(shards, kinds, items) if items else None

    start = gather((first, 0))
    w = {(_BIG[k][0], layer): arr
         for (k, layer), arr in zip(start.items, _run_comm(start, name="gather_first_weights"))}

    def layer_params(l):
        sh1, sc1, g1, sh2, sc2, g2 = mods[l]
        return dict(sh1=sh1, sc1=sc1, g1=g1, sh2=sh2, sc2=sc2, g2=g2, n1g=norm1_g[l][None], n2g=norm2_g[l][None],
                    gate_b=gate_b[l][None], conv_b=conv_b[l][None], lng=conv_ln_g[l][None], lnb=conv_ln_b[l][None],
                    w32=jnp.pad(conv_full[l], ((0, CONV_HALO - CONV_WIDTH), (0, 0))), lbk=(1.0 - lower[l])[None],
                    ng=hgrn_norm_g[l][None], qg=jnp.tile(sb_qn_g[l][None], (1, SB_PAIR // SB_DH)),
                    kg=jnp.tile(sb_kn_g[l][None], (1, SB_PAIR // SB_DH)))

    params = [layer_params(l) for l in range(n_layers)]
    act, saved = x[0], []
    for l in range(n_layers):
        comms = (gather((["w_out", "mlp_w1"], l)), gather((["mlp_w2"], l), (first, l + 1)))
        act, r = _layer_fwd(act, w, params[l], l, comms=comms)
        saved.append(r)
    dact, loss_lanes = _loss_head(act, loss_target[0], name="loss_head")

    grads, smalls, reduced = {}, [None] * n_layers, {}

    def reduce_scatter(items, tag):
        return _ReduceScatter([grads[_BIG[k][0], layer] for k, layer in items], [kinds[k] for k, _ in items],
                              [shards[k].shape[1:] for k, _ in items], place, tag)

    def carried(l):
        items = [(k, l + 1) for k in range(len(_BIG))] + [(k, l) for k, (n, _) in enumerate(_BIG) if n != "w_in"]
        box = {}

        def carry(stage, got):
            if stage == 0:
                box["rs"] = reduce_scatter(items, f"l{l}")
                return box["rs"].swap()
            if stage == 1:
                box["rs"].pair_sums(got)
                return box["rs"].scatter()
            if stage == 2:
                box["rs"].quarter_sums(got)
                return box["rs"].share()
            reduced.update(zip(items, got))

        return carry

    def final(l):
        items = [(index["w_in"], l)]
        box = {}

        def step(stage, got):
            if stage == 0:
                box["rs"] = reduce_scatter(items, "w_in")
                box["rs"].pair_sums(_run_comm(box["rs"].swap(), name="rs_swap_w_in"))
                return box["rs"].scatter()
            if stage == 1:
                box["rs"].quarter_sums(got)
                return box["rs"].share()
            reduced.update(zip(items, got))

        return step

    for l in reversed(range(n_layers)):
        dact, smalls[l] = _layer_bwd(dact, saved[l], w, params[l], l, grads, carried(l) if l + 1 < n_layers else None,
                                     final(l) if l == 0 else None)
    grad_x = dact[None]
    rest = [(k, l) for l in range(n_layers) for k in range(len(_BIG)) if (k, l) not in reduced]
    if rest:
        reduced.update(zip(rest, reduce_scatter(rest, "rest").run()))

    stack = lambda k: jnp.stack([smalls[l][k] for l in range(n_layers)])
    (d_hgrn_lb,) = lower_vjp(stack("lower")[:, 0, :])
    items = [("loss", loss_lanes), ("mod", stack("mod")), ("hgrn_lb", d_hgrn_lb), ("conv_w", stack("conv_w"))]
    items += [(k, stack(k)) for k in ("norm1_g", "gate_b", "conv_b", "conv_ln_g", "conv_ln_b", "hgrn_norm_g", "sb_qn_g",
                                      "sb_kn_g", "norm2_g")]
    pk3 = _Pack(items)
    got3 = _all_gather_small(pk3.array, name="gather_small_grads")
    tot3 = _sum8(got3, name="sum_small_grads")
    loss = (0.5 / D) * jnp.sum(pk3.get(tot3, "loss"))
    g = {k: pk3.get(tot3, k).reshape(wts[k].shape) for k in _REPLICATED if k != "mod_b"}
    g["mod_b"] = pk3.get(tot3, "mod")[:, 0, :]
    g["conv_w"] = lax.dynamic_slice_in_dim(pk3.get(tot3, "conv_w"), q * cw, cw, axis=2)
    dmod_all = pk3.get(got3, "mod")[:, :, 0, :]
    g_mod_w = None
    for l in range(n_layers):
        cols = lax.dynamic_slice_in_dim(dmod_all[:, l, :], q * n_mod, n_mod, axis=1)
        g_mod_w = _matmul(c_act, cols, ta=True, layer=l, n_layers=n_layers, into=g_mod_w, name=f"mod_dw_{l}")
    g["mod_w"] = g_mod_w

    delta, new_m, new_v = {}, {}, {}
    for n, _ in _BIG:
        outs = None
        for l in reversed(range(n_layers)):
            outs = _adamw_layer(wts[n], reduced[index[n], l], mom[n], var[n], l, outs, name=f"adamw_{n}_{l}")
        g[n], delta[n], new_m[n], new_v[n] = outs
    two_d = lambda t: t.reshape(-1, t.shape[-1])
    outs = _adamw(two_d(mod_w), two_d(g["mod_w"]), two_d(m_mod_w), two_d(v_mod_w), name="adamw_mod_w")
    delta["mod_w"], new_m["mod_w"], new_v["mod_w"] = (t.reshape(mod_w.shape) for t in outs)
    rest = _REPLICATED + ["conv_w"]
    packs = [_Pack([(n, src[n]) for n in rest]) for src in (wts, g, mom, var)]
    outs = _adamw(*[pk.array for pk in packs], name="adamw_small")
    for n in rest:
        delta[n], new_m[n], new_v[n] = (packs[0].get(t, n) for t in outs)

    return (loss, grad_x, *[g[n] for n in _WEIGHTS], *[delta[n] for n in _WEIGHTS], *[new_m[n] for n in _WEIGHTS],
            *[new_v[n] for n in _WEIGHTS])
```

```python
import functools
import math

import jax
import jax.numpy as jnp
from jax import lax
from jax.experimental import pallas as pl
from jax.experimental.pallas import tpu as pltpu

F32 = jnp.float32
BF16 = jnp.bfloat16
MESH = pl.DeviceIdType.MESH

EPS = 1e-6
CONV_CH = 512
CONV_WIDTH = 31
CONV_HALO = 32
HG_HEADS = 4
HG_D = 128
HG_CHUNK = 64
HG_SUB = 32
SB_HEADS = 8
SB_DH = 64
SB_BLK = 128
SB_PAIR = 128
SB_SKIP = -104.0
OFF_CONV, OFF_HG, OFF_SB, OFF_GL = 0, 1024, 3072, 4608
ADAM_LR, ADAM_B1, ADAM_B2, ADAM_EPS, ADAM_WD, ADAM_STEP = 0.001, 0.9, 0.999, 1e-08, 0.01, 10
VMEM_LIMIT_BYTES = 56 * 1024 * 1024
ROW_TILE = 256


def _cparams(sem=None, **kw):
    return pltpu.CompilerParams(dimension_semantics=sem, vmem_limit_bytes=VMEM_LIMIT_BYTES, **kw)


def _pick(n, cands):
    for c in cands:
        if n % c == 0:
            return c
    return n


MATMUL_VMEM_BUDGET = 40 * 1024 * 1024


def _tile_options(n, cap):
    opts = [t for t in range(cap - cap % 128, 0, -128) if n % t == 0]
    return opts or [n]


def _matmul_tiles(M, N, K, size_a, size_b, size_o, in_acc):
    for tm in _tile_options(M, 1024):
        for tk in _tile_options(K, 2048):
            for tn in _tile_options(N, 1280):
                need = 2 * (tm * tk * size_a + tk * tn * size_b + tm * tn * size_o)
                if K > tk and not in_acc:
                    need += tm * tn * 4
                if need <= MATMUL_VMEM_BUDGET:
                    return tm, tn, tk
    raise ValueError(f"no matmul tiling fits VMEM for {(M, N, K)}")
def _matmul(a, b, *, ta=False, tb=False, bl=None, out_dtype=F32, name, into=None, layer=None, n_layers=None,
            post=None, extras=(), rows=(), out_dtypes=None, comm=None):
    M, K = (a.shape[1], a.shape[0]) if ta else a.shape
    N = b.shape[-2] if tb else b.shape[-1]
    if post is not None:
        return _matmul_post(a, b, M, N, K, ta, tb, post, extras, rows, out_dtypes, name)
    assert comm is None or layer is None
    in_acc = jnp.dtype(out_dtype) == jnp.dtype(F32)
    tm, tn, tk = _matmul_tiles(M, N, K, a.dtype.itemsize, b.dtype.itemsize, jnp.dtype(out_dtype).itemsize, in_acc)
    nk = K // tk
    a_spec = pl.BlockSpec((tk, tm), lambda i, j, k: (k, i)) if ta else pl.BlockSpec((tm, tk), lambda i, j, k: (i, k))
    if bl is None:
        b_spec = pl.BlockSpec((tn, tk), lambda i, j, k: (j, k)) if tb else pl.BlockSpec((tk, tn), lambda i, j, k: (k, j))
    elif tb:
        b_spec = pl.BlockSpec((None, tn, tk), lambda i, j, k: (bl, j, k))
    else:
        b_spec = pl.BlockSpec((None, tk, tn), lambda i, j, k: (bl, k, j))
    dn = (((0 if ta else 1,), (1 if tb else 0,)), ((), ()))

    use_scratch = nk > 1 and not in_acc

    def kern(a_ref, b_ref, *rest):
        o_ref = rest[-2] if use_scratch else rest[-1]
        prod = lambda: lax.dot_general(a_ref[...].astype(BF16), b_ref[...].astype(BF16), dn,
                                       preferred_element_type=F32)
        if nk == 1:
            o_ref[...] = prod().astype(o_ref.dtype).reshape(o_ref.shape)
            return
        acc_ref = rest[-1] if use_scratch else o_ref
        k = pl.program_id(2)

        @pl.when(k == 0)
        def _():
            acc_ref[...] = prod().reshape(acc_ref.shape)

        @pl.when(k > 0)
        def _():
            acc_ref[...] += prod().reshape(acc_ref.shape)

        if use_scratch:
            @pl.when(k == nk - 1)
            def _():
                o_ref[...] = acc_ref[...].astype(o_ref.dtype).reshape(o_ref.shape)

    in_specs, args, aliases = [a_spec, b_spec], [a, b], {}
    if layer is None:
        out_shape = jax.ShapeDtypeStruct((M, N), out_dtype)
        out_spec = pl.BlockSpec((tm, tn), lambda i, j, k: (i, j))
    else:
        out_shape = jax.ShapeDtypeStruct((n_layers, M, N), out_dtype)
        out_spec = pl.BlockSpec((1, tm, tn), lambda i, j, k: (layer, i, j))
        if into is not None:
            in_specs.append(pl.BlockSpec(memory_space=pl.ANY))
            args.append(into)
            aliases = {2: 0}
    if comm is not None:
        (out,), got = _hosted_call(kern, grid=(M // tm, N // tn, nk), in_specs=in_specs, out_specs=[out_spec],
                                   out_shape=[out_shape], scratch_shapes=[pltpu.VMEM((tm, tn), F32)] if use_scratch else [],
                                   args=args, name=name, comm=comm)
        return out, got
    return pl.pallas_call(
        kern, grid=(M // tm, N // tn, nk), in_specs=in_specs, out_specs=out_spec, out_shape=out_shape,
        scratch_shapes=[pltpu.VMEM((tm, tn), F32)] if use_scratch else [],
        input_output_aliases=aliases, name=name,
        compiler_params=_cparams(("parallel", "parallel", "arbitrary")))(*args)


def _matmul_post(a, b, M, N, K, ta, tb, post, extras, rows, out_dtypes, name):
    per_elem = sum(e.dtype.itemsize for e in extras) + sum(jnp.dtype(d).itemsize for d in out_dtypes)
    fits = lambda tm, tn: 2 * (tm * K * a.dtype.itemsize + K * tn * b.dtype.itemsize + tm * tn * per_elem) <= MATMUL_VMEM_BUDGET
    tm, tn = next((tm, tn) for tm in _tile_options(M, 1024) for tn in _tile_options(N, 1280) if fits(tm, tn))
    a_spec = pl.BlockSpec((K, tm), lambda i, j: (0, i)) if ta else pl.BlockSpec((tm, K), lambda i, j: (i, 0))
    b_spec = pl.BlockSpec((tn, K), lambda i, j: (j, 0)) if tb else pl.BlockSpec((K, tn), lambda i, j: (0, j))
    tile = pl.BlockSpec((tm, tn), lambda i, j: (i, j))
    row = pl.BlockSpec((1, tn), lambda i, j: (0, j))
    dn = (((0 if ta else 1,), (1 if tb else 0,)), ((), ()))
    n_ex = len(extras) + len(rows)

    def kern(a_ref, b_ref, *rest):
        prod = lax.dot_general(a_ref[...].astype(BF16), b_ref[...].astype(BF16), dn, preferred_element_type=F32)
        res = post(prod, *[r[...].astype(F32) for r in rest[:n_ex]])
        for val, o_ref in zip(res, rest[n_ex:]):
            o_ref[...] = val.astype(o_ref.dtype)

    return pl.pallas_call(
        kern, grid=(M // tm, N // tn), in_specs=[a_spec, b_spec] + [tile] * len(extras) + [row] * len(rows),
        out_specs=[tile] * len(out_dtypes), out_shape=[jax.ShapeDtypeStruct((M, N), d) for d in out_dtypes], name=name,
        compiler_params=_cparams(("parallel", "parallel")))(a, b, *extras, *rows)


def _col_specs(off, width, T):
    bw = math.gcd(width, off) if off else width
    return [pl.BlockSpec((T, bw), functools.partial(lambda i, c: (i, c), c=off // bw + p)) for p in range(width // bw)]


def _gather_rows(refs, counts):
    vals, pos = [], 0
    for n in counts:
        parts = [refs[pos + p][...].astype(F32) for p in range(n)]
        pos += n
        vals.append(parts[0] if n == 1 else jnp.concatenate(parts, axis=1))
    return vals, pos


def _rowop(fn, ins, params, outs, *, name):
    S = ins[0][0].shape[0]
    T = min(ROW_TILE, S)
    in_specs, counts, args = [], [], []
    for arr, off, width in ins:
        sp = _col_specs(off, width, T)
        in_specs += sp
        counts.append(len(sp))
        args += [arr] * len(sp)
    in_specs += [pl.BlockSpec(p.shape, lambda i: (0, 0)) for p in params]

    def kern(*refs):
        vals, pos = _gather_rows(refs, counts)
        pv = [refs[pos + p][...] for p in range(len(params))]
        pos += len(params)
        res = fn(*vals, *pv)
        for r, o_ref in zip(res, refs[pos:]):
            o_ref[...] = r.astype(o_ref.dtype)

    return pl.pallas_call(
        kern, grid=(S // T,), in_specs=in_specs,
        out_specs=[pl.BlockSpec((T, w), lambda i: (i, 0)) for w, _ in outs],
        out_shape=[jax.ShapeDtypeStruct((S, w), dt) for w, dt in outs],
        name=name, compiler_params=_cparams(("parallel",)))(*args, *params)


def _rowop_bwd(fn, ins, params, douts, din_dtypes, *, name, add=None, comm=None):
    add = add or {}
    S = ins[0][0].shape[0]
    T = min(ROW_TILE, S)
    in_specs, counts, args = [], [], []
    for arr, off, width in ins:
        sp = _col_specs(off, width, T)
        in_specs += sp
        counts.append(len(sp))
        args += [arr] * len(sp)
    in_specs += [pl.BlockSpec(p.shape, lambda i: (0, 0)) for p in params]
    in_specs += [pl.BlockSpec((T, d.shape[1]), lambda i: (i, 0)) for d in douts]
    add_keys = sorted(add)
    in_specs += [pl.BlockSpec((T, add[k].shape[1]), lambda i: (i, 0)) for k in add_keys]
    want = [k for k, dt in enumerate(din_dtypes) if dt is not None]

    def kern(*refs):
        vals, pos = _gather_rows(refs, counts)
        pv = [refs[pos + p][...] for p in range(len(params))]
        pos += len(params)
        cts = [refs[pos + p][...].astype(F32) for p in range(len(douts))]
        pos += len(douts)
        adds = {k: refs[pos + p][...].astype(F32) for p, k in enumerate(add_keys)}
        pos += len(add_keys)
        _, vjp = jax.vjp(fn, *vals, *pv)
        grads = vjp(tuple(cts))
        for k in want:
            g = grads[k] + adds[k] if k in adds else grads[k]
            refs[pos][...] = g.astype(refs[pos].dtype)
            pos += 1
        first = pl.program_id(0) == 0
        for p in range(len(params)):
            gp, o_ref = grads[len(ins) + p], refs[pos + p]

            @pl.when(first)
            def _(gp=gp, o_ref=o_ref):
                o_ref[...] = gp

            @pl.when(jnp.logical_not(first))
            def _(gp=gp, o_ref=o_ref):
                o_ref[...] += gp

    out_specs = [pl.BlockSpec((T, ins[k][2]), lambda i: (i, 0)) for k in want]
    out_specs += [pl.BlockSpec(p.shape, lambda i: (0, 0)) for p in params]
    out_shape = [jax.ShapeDtypeStruct((S, ins[k][2]), din_dtypes[k]) for k in want]
    out_shape += [jax.ShapeDtypeStruct(p.shape, F32) for p in params]
    res, got = _hosted_call(
        kern, grid=(S // T,), in_specs=in_specs, out_specs=out_specs, out_shape=out_shape, scratch_shapes=[],
        args=[*args, *params, *douts, *[add[k] for k in add_keys]], name=name, comm=comm, sem=("arbitrary",))
    dins = [None] * len(ins)
    for p, k in enumerate(want):
        dins[k] = res[p]
    return (dins, res[len(want):]) if comm is None else (dins, res[len(want):], got)


def _rms(x, g):
    return x * lax.rsqrt(jnp.mean(x * x, axis=-1, keepdims=True) + EPS) * g


def _fn_normmod(x, g, sc, sh):
    return (_rms(x, g) * (1.0 + sc) + sh,)


def _fn_lnsilu(c, g, b):
    mu = jnp.mean(c, axis=-1, keepdims=True)
    var = jnp.mean(jnp.square(c - mu), axis=-1, keepdims=True)
    y = (c - mu) * lax.rsqrt(var + EPS) * g + b
    return (y * jax.nn.sigmoid(y),)


def _fn_merge(gl, yc, yh, ys, gb):
    d = yc.shape[1]
    g = jax.nn.sigmoid(gl + gb)
    return (g[:, :d] * yc + g[:, d:2 * d] * yh + g[:, 2 * d:] * ys,)


def _fn_resid(x, y, g):
    return (x + g * y,)


def _fn_resid_norm(x, y, g, n, sc, sh):
    x1 = x + g * y
    return (x1,) + _fn_normmod(x1, n, sc, sh)


def _fn_scale(y, g):
    return (g * y,)


def _fn_relu2(u):
    return (jnp.square(jnp.maximum(u, 0.0)),)


def _conv_specs(S, T):
    r = T // CONV_HALO
    cur = [pl.BlockSpec((T, CONV_CH), lambda i: (i, 0)), pl.BlockSpec((T, CONV_CH), lambda i: (i, 1))]
    prev = [pl.BlockSpec((CONV_HALO, CONV_CH), lambda i: (jnp.maximum(i * r - 1, 0), 0)),
            pl.BlockSpec((CONV_HALO, CONV_CH), lambda i: (jnp.maximum(i * r - 1, 0), 1))]
    return cur + prev


def _glu_ext(a_ref, g_ref, ah_ref, gh_ref):
    a = a_ref[...]
    sg = jax.nn.sigmoid(g_ref[...])
    uh = jnp.where(pl.program_id(0) > 0, ah_ref[...] * jax.nn.sigmoid(gh_ref[...]), 0.0)
    return a, sg, jnp.concatenate([uh, a * sg], axis=0)


def _shift_up(xe, k, T):
    return xe[:T] if k == 0 else pltpu.roll(xe, shift=xe.shape[0] - k, axis=0)[:T]


def _conv_fwd(proj, w32, b, *, name):
    S = proj.shape[0]
    T = min(ROW_TILE, S)
    lead = CONV_HALO - (CONV_WIDTH - 1)

    def kern(a_ref, g_ref, ah_ref, gh_ref, w_ref, b_ref, o_ref):
        _, _, ue = _glu_ext(a_ref, g_ref, ah_ref, gh_ref)
        acc = jnp.zeros((T, CONV_CH), F32) + b_ref[...]
        for j in range(CONV_WIDTH):
            acc = acc + w_ref[j:j + 1, :] * _shift_up(ue, lead + j, T)
        o_ref[...] = acc

    const = lambda shape: pl.BlockSpec(shape, lambda i: (0, 0))
    return pl.pallas_call(
        kern, grid=(S // T,), in_specs=_conv_specs(S, T) + [const(w32.shape), const(b.shape)],
        out_specs=pl.BlockSpec((T, CONV_CH), lambda i: (i, 0)),
        out_shape=jax.ShapeDtypeStruct((S, CONV_CH), F32), name=name,
        compiler_params=_cparams(("parallel",)))(proj, proj, proj, proj, w32, b)


def _conv_bwd(proj, dc, w32, *, name, comm=None):
    S = proj.shape[0]
    T = min(ROW_TILE, S)
    nt = S // T
    r = T // CONV_HALO
    lead = CONV_HALO - (CONV_WIDTH - 1)
    last_halo = S // CONV_HALO - 1

    def kern(a_ref, g_ref, ah_ref, gh_ref, dc_ref, dcn_ref, w_ref, dag_ref, dw_ref, db_ref):
        i = pl.program_id(0)
        a, sg, ue = _glu_ext(a_ref, g_ref, ah_ref, gh_ref)
        dc_t = dc_ref[...]
        de = jnp.concatenate([dc_t, jnp.where(i < nt - 1, dcn_ref[...], 0.0)], axis=0)

        @pl.when(i == 0)
        def _():
            dw_ref[...] = jnp.zeros_like(dw_ref)
            db_ref[...] = jnp.zeros_like(db_ref)

        du = jnp.zeros((T, CONV_CH), F32)
        for j in range(CONV_WIDTH):
            du = du + w_ref[j:j + 1, :] * _shift_up(de, CONV_WIDTH - 1 - j, T)
            dw_ref[j:j + 1, :] += jnp.sum(dc_t * _shift_up(ue, lead + j, T), axis=0, keepdims=True)
        db_ref[...] += jnp.sum(dc_t, axis=0, keepdims=True)
        dag_ref[:, :CONV_CH] = (du * sg).astype(BF16)
        dag_ref[:, CONV_CH:] = (du * a * sg * (1.0 - sg)).astype(BF16)

    const = lambda shape: pl.BlockSpec(shape, lambda i: (0, 0))
    in_specs = _conv_specs(S, T) + [
        pl.BlockSpec((T, CONV_CH), lambda i: (i, 0)),
        pl.BlockSpec((CONV_HALO, CONV_CH), lambda i: (jnp.minimum((i + 1) * r, last_halo), 0)),
        const(w32.shape)]
    return _hosted_call(
        kern, grid=(nt,), in_specs=in_specs,
        out_specs=[pl.BlockSpec((T, 2 * CONV_CH), lambda i: (i, 0)), const(w32.shape), const((1, CONV_CH))],
        out_shape=[jax.ShapeDtypeStruct((S, 2 * CONV_CH), BF16), jax.ShapeDtypeStruct(w32.shape, F32),
                   jax.ShapeDtypeStruct((1, CONV_CH), F32)],
        scratch_shapes=[], args=[proj, proj, proj, proj, dc, dc, w32], name=name, comm=comm, sem=("arbitrary",))


def _iota2(shape, dim):
    return lax.broadcasted_iota(jnp.int32, shape, dim)


def _running(x, seg, later):
    n = x.shape[0]
    pos = _iota2(x.shape, 0) & (seg - 1)
    k = 1
    while k < seg:
        if later:
            x = x + jnp.where(pos < seg - k, pltpu.roll(x, n - k, axis=0), 0.0)
        else:
            x = x + jnp.where(pos >= k, pltpu.roll(x, k, axis=0), 0.0)
        k *= 2
    return x


@functools.partial(jax.custom_vjp, nondiff_argnums=(1,))
def _prefix(x, seg):
    return _running(x, seg, False)


_prefix.defvjp(lambda x, seg: (_running(x, seg, False), None), lambda seg, _, g: (_running(g, seg, True),))


def _hg_chunk(q, f, iv, g, st, lbk, ng):
    n, sub = HG_CHUNK, HG_SUB
    kk = lbk * jax.nn.sigmoid(-f)
    lf = jnp.log(1.0 - kk)
    b = _prefix(lf, n)
    bs = _prefix(lf, sub)
    bt = jnp.sum(lf, axis=0, keepdims=True)
    qh = q * jax.nn.sigmoid(q)
    dot_nt = lambda x, y: lax.dot_general(x.astype(BF16), y.astype(BF16), (((1,), (1,)), ((), ())),
                                          preferred_element_type=F32)
    o = dot_nt(qh * jnp.exp(b), st)
    b0 = b - bs
    qs = qh * jnp.exp(bs)
    col = _iota2((sub, n), 1)
    rows = []
    for blk in range(n // sub):
        lo = blk * sub
        sl = slice(lo, lo + sub)
        acc = o[sl]
        if blk > 0:
            ref = jnp.concatenate([b0[sl]] * (n // sub), axis=0)
            kd = kk * jnp.exp(jnp.minimum(ref - b, 0.0))
            sc = jnp.where(col < lo, dot_nt(qs[sl], kd), 0.0)
            acc = acc + jnp.dot(sc.astype(BF16), iv.astype(BF16), preferred_element_type=F32)
        bq, bk = bs[sl][None, :, :], bs[sl][:, None, :]
        s_i = lax.broadcasted_iota(jnp.int32, (sub, sub, HG_D), 0)
        t_i = lax.broadcasted_iota(jnp.int32, (sub, sub, HG_D), 1)
        keep = s_i <= t_i
        p = jnp.where(keep, qh[sl][None, :, :] * kk[sl][:, None, :] * jnp.exp(jnp.where(keep, bq - bk, 0.0)), 0.0)
        w = jnp.sum(p, axis=-1, keepdims=True)
        acc = acc + jnp.sum(w * iv[sl][:, None, :], axis=0)
        rows.append(acc)
    o = jnp.concatenate(rows, axis=0)
    kd = kk * jnp.exp(bt - b)
    st_new = jnp.exp(bt) * st + lax.dot_general(iv.astype(BF16), kd.astype(BF16), (((0,), (0,)), ((), ())),
                                                     preferred_element_type=F32)
    out = _rms(o, ng) * (g * jax.nn.sigmoid(g))
    return out, st_new


def _hg_tile(S):
    return min(512, S)


def _hg_in_specs(rt, rev, nr):
    width = HG_HEADS * HG_D
    base = OFF_HG // width
    row = (lambda r: nr - 1 - r) if rev else (lambda r: r)
    return [pl.BlockSpec((rt, width), functools.partial(lambda r, k: (row(r), base + k), k=k)) for k in range(4)]


def _hg_cols(h):
    return slice(h * HG_D, (h + 1) * HG_D)


def _hgrn_fwd(proj, lbk, ng, *, name, comm=None):
    S = proj.shape[0]
    rt = _hg_tile(S)
    nr, nc = S // rt, rt // HG_CHUNK

    def kern(q_ref, f_ref, i_ref, g_ref, lbk_ref, ng_ref, o_ref, st_out_ref, st_ref):
        @pl.when(pl.program_id(0) == 0)
        def _():
            st_ref[...] = jnp.zeros_like(st_ref)

        def body(c, carry):
            rows = pl.ds(pl.multiple_of(c * HG_CHUNK, HG_CHUNK), HG_CHUNK)
            for h in range(HG_HEADS):
                cols = _hg_cols(h)
                st = st_ref[h]
                st_out_ref[h, c] = st
                out, st_new = _hg_chunk(q_ref[rows, cols], f_ref[rows, cols], i_ref[rows, cols], g_ref[rows, cols], st,
                                        lbk_ref[:, cols], ng_ref[...])
                o_ref[rows, cols] = out.astype(o_ref.dtype)
                st_ref[h] = st_new
            return carry

        lax.fori_loop(0, nc, body, 0)

    width = HG_HEADS * HG_D
    in_specs = _hg_in_specs(rt, False, nr) + [pl.BlockSpec((1, width), lambda r: (0, 0)),
                                               pl.BlockSpec((1, HG_D), lambda r: (0, 0))]
    return _hosted_call(
        kern, grid=(nr,), in_specs=in_specs,
        out_specs=[pl.BlockSpec((rt, width), lambda r: (r, 0)),
                   pl.BlockSpec((HG_HEADS, nc, HG_D, HG_D), lambda r: (0, r, 0, 0))],
        out_shape=[jax.ShapeDtypeStruct((S, width), BF16),
                   jax.ShapeDtypeStruct((HG_HEADS, S // HG_CHUNK, HG_D, HG_D), F32)],
        scratch_shapes=[pltpu.VMEM((HG_HEADS, HG_D, HG_D), F32)],
        args=[proj, proj, proj, proj, lbk, ng], name=name, comm=comm, sem=("arbitrary",))


def _hgrn_bwd(proj, states, dout, lbk, ng, *, name, comm=None):
    S = proj.shape[0]
    rt = _hg_tile(S)
    nr, nc = S // rt, rt // HG_CHUNK
    width = HG_HEADS * HG_D

    def kern(q_ref, f_ref, i_ref, g_ref, st_in_ref, do_ref, lbk_ref, ng_ref,
             dq_ref, df_ref, di_ref, dg_ref, dlbk_ref, dng_ref, dst_ref):
        @pl.when(pl.program_id(0) == 0)
        def _():
            dst_ref[...] = jnp.zeros_like(dst_ref)
            dlbk_ref[...] = jnp.zeros_like(dlbk_ref)
            dng_ref[...] = jnp.zeros_like(dng_ref)

        def body(k, carry):
            c = nc - 1 - k
            rows = pl.ds(pl.multiple_of(c * HG_CHUNK, HG_CHUNK), HG_CHUNK)
            for h in range(HG_HEADS):
                cols = _hg_cols(h)
                _, vjp = jax.vjp(_hg_chunk, q_ref[rows, cols], f_ref[rows, cols], i_ref[rows, cols], g_ref[rows, cols],
                                 st_in_ref[h, c], lbk_ref[:, cols], ng_ref[...])
                dq, df, di, dg, dst, dlbk, dng = vjp((do_ref[rows, cols].astype(F32), dst_ref[h]))
                dq_ref[rows, cols] = dq.astype(BF16)
                df_ref[rows, cols] = df.astype(BF16)
                di_ref[rows, cols] = di.astype(BF16)
                dg_ref[rows, cols] = dg.astype(BF16)
                dst_ref[h] = dst
                dlbk_ref[:, cols] += dlbk
                dng_ref[h] += dng
            return carry

        lax.fori_loop(0, nc, body, 0)

    rev = lambda r: nr - 1 - r
    tile = pl.BlockSpec((rt, width), lambda r: (rev(r), 0))
    in_specs = _hg_in_specs(rt, True, nr) + [
        pl.BlockSpec((HG_HEADS, nc, HG_D, HG_D), lambda r: (0, rev(r), 0, 0)), tile,
        pl.BlockSpec((1, width), lambda r: (0, 0)), pl.BlockSpec((1, HG_D), lambda r: (0, 0))]
    return _hosted_call(
        kern, grid=(nr,), in_specs=in_specs,
        out_specs=[tile, tile, tile, tile, pl.BlockSpec((1, width), lambda r: (0, 0)),
                   pl.BlockSpec((HG_HEADS, 1, HG_D), lambda r: (0, 0, 0))],
        out_shape=[jax.ShapeDtypeStruct((S, width), BF16)] * 4 + [
            jax.ShapeDtypeStruct((1, width), F32), jax.ShapeDtypeStruct((HG_HEADS, 1, HG_D), F32)],
        scratch_shapes=[pltpu.VMEM((HG_HEADS, HG_D, HG_D), F32)],
        args=[proj, proj, proj, proj, states, dout, lbk, ng], name=name, comm=comm, sem=("arbitrary",))


def _sb_scores(km, qi):
    return lax.dot_general(km, qi, (((1,), (1,)), ((), ())), preferred_element_type=F32)


def _sb_weights(zt, r_run, diag):
    n = SB_BLK
    sp = jnp.maximum(zt, 0.0) + jnp.log(1.0 + jnp.exp(-jnp.abs(zt)))
    lk = -sp
    if diag:
        keep = (_iota2(zt.shape, 0) & (n - 1)) < _iota2(zt.shape, 1)
        lk = jnp.where(keep, lk, 0.0)
    tails = [_running(lk[a * n:(a + 1) * n], n, True) for a in range(2)]
    between = jnp.concatenate([tails[a] + r_run[a] for a in range(2)], axis=0)
    wgt = jnp.exp(zt + between)
    if diag:
        wgt = jnp.where(keep, wgt, 0.0)
    return sp, wgt, [t[0:1, :] for t in tails]


def _sb_norm_pair(x, g2, lane_lo):
    sq = x * x
    ms_lo = jnp.sum(jnp.where(lane_lo, sq, 0.0), axis=-1, keepdims=True)
    ms_hi = jnp.sum(jnp.where(lane_lo, 0.0, sq), axis=-1, keepdims=True)
    return x * lax.rsqrt(jnp.where(lane_lo, ms_lo, ms_hi) * (1.0 / SB_DH) + EPS) * g2


def _sb_specs(S):
    base = OFF_SB // SB_PAIR
    per = SB_HEADS * SB_DH // SB_PAIR
    cols = [pl.BlockSpec((S, SB_PAIR), functools.partial(lambda p, k: (0, base + per * k + p), k=k)) for k in range(3)]
    return cols + [pl.BlockSpec((1, SB_PAIR), lambda p: (0, 0))] * 2


def _sb_rows(i):
    return pl.ds(pl.multiple_of(i * SB_BLK, SB_BLK), SB_BLK)


def _sb_both(j, a=None):
    if a is None:
        return pl.ds(pl.multiple_of(j * 2 * SB_BLK, 2 * SB_BLK), 2 * SB_BLK)
    return pl.ds(pl.multiple_of(j * 2 * SB_BLK + a * SB_BLK, SB_BLK), SB_BLK)


def _sb_fwd(proj, qg2, kg2, *, name, comm=None):
    S = proj.shape[0]
    nb = S // SB_BLK
    scale = SB_DH ** -0.5
    n_pairs = SB_HEADS * SB_DH // SB_PAIR

    def kern(q_ref, k_ref, v_ref, qg_ref, kg_ref, o_ref, rs_ref, qp_ref, km_ref, vt_ref):
        lane_lo = _iota2((SB_BLK, SB_PAIR), 1) < SB_DH

        def prologue(j, carry):
            rows = _sb_rows(j)
            qp_ref[rows, :] = (_sb_norm_pair(q_ref[rows, :], qg_ref[...], lane_lo) * scale).astype(BF16)
            kn = _sb_norm_pair(k_ref[rows, :], kg_ref[...], lane_lo)
            v = v_ref[rows, :]
            for a, mine in enumerate((lane_lo, jnp.logical_not(lane_lo))):
                km_ref[_sb_both(j, a), :] = jnp.where(mine, kn, 0.0).astype(BF16)
                vt_ref[:, _sb_both(j, a)] = jnp.where(mine, v, 0.0).T.astype(BF16)
            return carry

        lax.fori_loop(0, nb, prologue, 0)

        def qblock(i, carry):
            qi = qp_ref[_sb_rows(i), :]

            scores = lambda j: _sb_scores(km_ref[_sb_both(jnp.maximum(j, 0)), :], qi)
            output = lambda j, wgt: jnp.dot(vt_ref[:, _sb_both(j)], wgt, preferred_element_type=F32)

            def note(j, r_run):
                for a in range(2):
                    rs_ref[a, i, pl.ds(j, 1), :] = r_run[a]
                return jnp.maximum(jnp.max(r_run[0]), jnp.max(r_run[1])) > SB_SKIP

            def noted(j, r_run):
                return lax.cond(j >= 0, lambda: note(j, r_run).astype(jnp.int32), lambda: jnp.int32(0))

            zero = jnp.zeros((1, SB_BLK), F32)
            zt, z_next = scores(i), scores(i - 1)
            _, wgt, r_run = _sb_weights(zt, [zero, zero], True)
            go = noted(i - 1, r_run)

            def body(c):
                j, _, acc, r_run, zt, j_prev, w_prev = c
                z_next = scores(j - 1)
                acc = acc + output(j_prev, w_prev)
                _, wgt, lk_sum = _sb_weights(zt, r_run, False)
                r_run = [r_run[a] + lk_sum[a] for a in range(2)]
                return j - 1, noted(j - 1, r_run), acc, r_run, z_next, j, wgt.astype(BF16)

            c = (i - 1, go, jnp.zeros((SB_PAIR, SB_BLK), F32), r_run, z_next, i, wgt.astype(BF16))
            _, _, acc, _, _, j_prev, w_prev = lax.while_loop(lambda c: c[1] > 0, body, c)
            o_ref[_sb_rows(i), :] = (acc + output(j_prev, w_prev)).T.astype(o_ref.dtype)
            return carry

        lax.fori_loop(0, nb, qblock, 0)

    width = SB_HEADS * SB_DH
    return _hosted_call(
        kern, grid=(n_pairs,), in_specs=_sb_specs(S),
        out_specs=[pl.BlockSpec((S, SB_PAIR), lambda p: (0, p)),
                   pl.BlockSpec((2, nb, nb, SB_BLK), lambda p: (p, 0, 0, 0))],
        out_shape=[jax.ShapeDtypeStruct((S, width), BF16), jax.ShapeDtypeStruct((SB_HEADS, nb, nb, SB_BLK), F32)],
        scratch_shapes=[pltpu.VMEM((S, SB_PAIR), BF16), pltpu.VMEM((2 * S, SB_PAIR), BF16),
                        pltpu.VMEM((SB_PAIR, 2 * S), BF16)],
        args=[proj, proj, proj, qg2, kg2], name=name, comm=comm, sem=("parallel",))


def _sb_bwd(proj, qg2, kg2, rs, do, *, name, comm=None):
    S = proj.shape[0]
    nb = S // SB_BLK
    scale = SB_DH ** -0.5
    n_pairs = SB_HEADS * SB_DH // SB_PAIR

    def kern(q_ref, k_ref, v_ref, qg_ref, kg_ref, rs_ref, do_ref, dq_ref, dk_ref, dv_ref, dqg_ref, dkg_ref,
             qp_ref, km_ref, kt_ref, vm_ref, dqn_ref, dkn_ref, dvs_ref):
        lane_lo = _iota2((SB_BLK, SB_PAIR), 1) < SB_DH
        heads = (lane_lo, jnp.logical_not(lane_lo))
        fn_q = lambda x, g: _sb_norm_pair(x, g, lane_lo) * scale
        fn_k = lambda x, g: _sb_norm_pair(x, g, lane_lo)

        def prologue(j, carry):
            rows = _sb_rows(j)
            qp_ref[rows, :] = fn_q(q_ref[rows, :], qg_ref[...]).astype(BF16)
            kn = fn_k(k_ref[rows, :], kg_ref[...])
            v = v_ref[rows, :]
            for a, mine in enumerate(heads):
                k_a = jnp.where(mine, kn, 0.0)
                km_ref[_sb_both(j, a), :] = k_a.astype(BF16)
                kt_ref[:, _sb_both(j, a)] = k_a.T.astype(BF16)
                vm_ref[_sb_both(j, a), :] = jnp.where(mine, v, 0.0).astype(BF16)
            return carry

        lax.fori_loop(0, nb, prologue, 0)
        dkn_ref[...] = jnp.zeros_like(dkn_ref)
        dvs_ref[...] = jnp.zeros_like(dvs_ref)

        def qblock(i, carry):
            qi = qp_ref[_sb_rows(i), :]
            doi = do_ref[_sb_rows(i), :]

            def opening(j):
                jc = jnp.minimum(j, i)
                return (_sb_scores(km_ref[_sb_both(jc), :], qi),
                        lax.dot_general(vm_ref[_sb_both(jc), :], doi, (((1,), (1,)), ((), ())), preferred_element_type=F32))

            def closing(j, dzb, wgtb, dqa):
                dkn_ref[_sb_both(j), :] += jnp.dot(dzb, qi, preferred_element_type=F32)
                dvs_ref[_sb_both(j), :] += jnp.dot(wgtb, doi, preferred_element_type=F32)
                return dqa + jnp.dot(kt_ref[:, _sb_both(j)], dzb, preferred_element_type=F32)

            def middle(j, diag, zt, dp, e_run):
                zero = jnp.zeros((1, SB_BLK), F32)
                r_run = [zero, zero] if diag else [rs_ref[a, i, pl.ds(j, 1), :] for a in range(2)]
                sp, wgt, _ = _sb_weights(zt, r_run, diag)
                e = dp * wgt
                heads_e = [_running(e[a * SB_BLK:(a + 1) * SB_BLK], SB_BLK, False) for a in range(2)]
                e_left = jnp.concatenate([heads_e[a] + e_run[a] for a in range(2)], axis=0) - e
                s_neg = jnp.exp(-sp)
                dz = e * s_neg - e_left * (1.0 - s_neg)
                if diag:
                    dz = jnp.where((_iota2(dz.shape, 0) & (SB_BLK - 1)) < _iota2(dz.shape, 1), dz, 0.0)
                return dz.astype(BF16), wgt.astype(BF16), [e_run[a] + heads_e[a][SB_BLK - 1:SB_BLK, :] for a in range(2)]

            def live(j):
                jc = jnp.maximum(j, 0)
                top = jnp.maximum(jnp.max(rs_ref[0, i, pl.ds(jc, 1), :]), jnp.max(rs_ref[1, i, pl.ds(jc, 1), :]))
                return jnp.logical_and(j >= 0, top > SB_SKIP).astype(jnp.int32)

            first, _ = lax.while_loop(lambda c: c[1] > 0, lambda c: (c[0] - 1, live(c[0] - 2)), (i, live(i - 1)))

            def body(j, c):
                dqa, e_run, zt, dp, j_prev, dzb, wgtb = c
                nxt = opening(j + 1)
                dqa = closing(j_prev, dzb, wgtb, dqa)
                dzb, wgtb, e_run = middle(j, False, zt, dp, e_run)
                return (dqa, e_run) + nxt + (j, dzb, wgtb)

            zero = jnp.zeros((1, SB_BLK), F32)
            none = jnp.zeros((2 * SB_BLK, SB_BLK), BF16)
            c = (jnp.zeros((SB_PAIR, SB_BLK), F32), [zero, zero]) + opening(first) + (first, none, none)
            dqa, e_run, zt, dp, j_prev, dzb, wgtb = lax.fori_loop(first, i, body, c)
            dqa = closing(j_prev, dzb, wgtb, dqa)
            dzb, wgtb, _ = middle(i, True, zt, dp, e_run)
            dqn_ref[_sb_rows(i), :] = closing(i, dzb, wgtb, dqa).T
            return carry

        lax.fori_loop(0, nb, qblock, 0)
        dqg_ref[...] = jnp.zeros_like(dqg_ref)
        dkg_ref[...] = jnp.zeros_like(dkg_ref)

        def epilogue(j, carry):
            rows = _sb_rows(j)
            _, vjp_q = jax.vjp(fn_q, q_ref[rows, :], qg_ref[...])
            dq, dqg = vjp_q(dqn_ref[rows, :])
            _, vjp_k = jax.vjp(fn_k, k_ref[rows, :], kg_ref[...])
            dk, dkg = vjp_k(jnp.where(lane_lo, dkn_ref[_sb_both(j, 0), :], dkn_ref[_sb_both(j, 1), :]))
            dq_ref[rows, :] = dq.astype(BF16)
            dk_ref[rows, :] = dk.astype(BF16)
            dv_ref[rows, :] = jnp.where(lane_lo, dvs_ref[_sb_both(j, 0), :], dvs_ref[_sb_both(j, 1), :]).astype(BF16)
            dqg_ref[0] += dqg
            dkg_ref[0] += dkg
            return carry

        lax.fori_loop(0, nb, epilogue, 0)

    width = SB_HEADS * SB_DH
    pair = pl.BlockSpec((S, SB_PAIR), lambda p: (0, p))
    dgain = pl.BlockSpec((1, 1, SB_PAIR), lambda p: (p, 0, 0))
    in_specs = _sb_specs(S) + [pl.BlockSpec((2, nb, nb, SB_BLK), lambda p: (p, 0, 0, 0)), pair]
    return _hosted_call(
        kern, grid=(n_pairs,), in_specs=in_specs, out_specs=[pair, pair, pair, dgain, dgain],
        out_shape=[jax.ShapeDtypeStruct((S, width), BF16)] * 3 + [jax.ShapeDtypeStruct((n_pairs, 1, SB_PAIR), F32)] * 2,
        scratch_shapes=[pltpu.VMEM((S, SB_PAIR), BF16), pltpu.VMEM((2 * S, SB_PAIR), BF16), pltpu.VMEM((SB_PAIR, 2 * S), BF16),
                        pltpu.VMEM((2 * S, SB_PAIR), BF16), pltpu.VMEM((S, SB_PAIR), F32),
                        pltpu.VMEM((2 * S, SB_PAIR), F32), pltpu.VMEM((2 * S, SB_PAIR), F32)],
        args=[proj, proj, proj, qg2, kg2, rs, do], name=name, comm=comm, sem=("parallel",))


def _loss_head(y, target, *, name):
    S, D = y.shape
    T = min(ROW_TILE, S)

    def kern(y_ref, t_ref, dy_ref, acc_ref):
        err = y_ref[...] - t_ref[...]
        dy_ref[...] = err * (1.0 / D)
        col = jnp.sum(err * err, axis=0, keepdims=True)
        part = sum(col[:, k * 128:(k + 1) * 128] for k in range(D // 128))

        @pl.when(pl.program_id(0) == 0)
        def _():
            acc_ref[...] = part

        @pl.when(pl.program_id(0) > 0)
        def _():
            acc_ref[...] += part

    tile = pl.BlockSpec((T, D), lambda i: (i, 0))
    return pl.pallas_call(
        kern, grid=(S // T,), in_specs=[tile, tile], out_specs=[tile, pl.BlockSpec((1, 128), lambda i: (0, 0))],
        out_shape=[jax.ShapeDtypeStruct((S, D), F32), jax.ShapeDtypeStruct((1, 128), F32)],
        name=name, compiler_params=_cparams(("arbitrary",)))(y, target)


def _adamw_math(w, g, m, v):
    m = ADAM_B1 * m + (1.0 - ADAM_B1) * g
    v = ADAM_B2 * v + (1.0 - ADAM_B2) * jnp.square(g)
    m_hat = m / (1.0 - ADAM_B1 ** ADAM_STEP)
    v_hat = v / (1.0 - ADAM_B2 ** ADAM_STEP)
    return -ADAM_LR * (m_hat / (jnp.sqrt(v_hat) + ADAM_EPS) + ADAM_WD * w), m, v


def _adamw(w, g, m, v, *, name):
    R, C = w.shape
    T = _pick(R, (256, 128, 64, 32, 16, 8))

    def kern(w_ref, g_ref, m_ref, v_ref, d_ref, mo_ref, vo_ref):
        d, mn, vn = _adamw_math(w_ref[...], g_ref[...], m_ref[...], v_ref[...])
        d_ref[...] = d
        mo_ref[...] = mn
        vo_ref[...] = vn

    tile = pl.BlockSpec((T, C), lambda i: (i, 0))
    return pl.pallas_call(
        kern, grid=(R // T,), in_specs=[tile] * 4, out_specs=[tile] * 3,
        out_shape=[jax.ShapeDtypeStruct((R, C), F32)] * 3, name=name,
        compiler_params=_cparams(("parallel",)))(w, g, m, v)


def _adamw_layer(w, g, m, v, layer, prev, *, name, comm=None):
    L, R, C = w.shape
    T = _pick(R, (256, 128, 64, 32, 16, 8))

    def kern(w_ref, g_ref, m_ref, v_ref, *rest):
        go_ref, d_ref, mo_ref, vo_ref = rest[-4:]
        grad = g_ref[...]
        d, mn, vn = _adamw_math(w_ref[...], grad, m_ref[...], v_ref[...])
        go_ref[...] = grad
        d_ref[...] = d
        mo_ref[...] = mn
        vo_ref[...] = vn

    layer_tile = pl.BlockSpec((None, T, C), lambda i: (layer, i, 0))
    in_specs = [layer_tile, pl.BlockSpec((T, C), lambda i: (i, 0)), layer_tile, layer_tile]
    args, aliases = [w, g, m, v], {}
    if prev is not None:
        in_specs += [pl.BlockSpec(memory_space=pl.ANY)] * 4
        args += list(prev)
        aliases = {4 + k: k for k in range(4)}
    if comm is not None:
        assert prev is None
        return _hosted_call(kern, grid=(R // T,), in_specs=in_specs, out_specs=[layer_tile] * 4,
                            out_shape=[jax.ShapeDtypeStruct((L, R, C), F32)] * 4, scratch_shapes=[], args=args,
                            name=name, comm=comm)
    return pl.pallas_call(
        kern, grid=(R // T,), in_specs=in_specs, out_specs=[layer_tile] * 4,
        out_shape=[jax.ShapeDtypeStruct((L, R, C), F32)] * 4, input_output_aliases=aliases, name=name,
        compiler_params=_cparams(("parallel",)))(*args)


def _sum8(g, *, name):
    def kern(g_ref, o_ref):
        acc = g_ref[0]
        for d in range(1, g.shape[0]):
            acc = acc + g_ref[d]
        o_ref[...] = acc

    return pl.pallas_call(kern, out_shape=jax.ShapeDtypeStruct(g.shape[1:], F32), name=name,
                          compiler_params=_cparams())(g)


def _place():
    return lax.axis_index("x"), lax.axis_index("y"), lax.axis_index("c")


def _other_chips(x, y):
    return [(1 - x, y), (x, 1 - y), (1 - x, 1 - y)]


def _remote(src, dst, send_sems, recv_sems, k, to):
    return pltpu.make_async_remote_copy(src_ref=src, dst_ref=dst, send_sem=send_sems.at[k], recv_sem=recv_sems.at[k],
                                        device_id=to, device_id_type=MESH)


def _all_gather_small(v, *, name):
    def body(x_ref, out_ref, send_sems, recv_sems, local_sem):
        x, y, c = _place()
        me = 4 * x + 2 * y + c
        mine = pltpu.make_async_copy(x_ref, out_ref.at[me], local_sem)
        mine.start()
        peers = []
        for f in range(1, 8):
            peers.append((1 - x if f & 4 else x, 1 - y if f & 2 else y, 1 - c if f & 1 else c))
        sends = [_remote(x_ref, out_ref.at[me], send_sems, recv_sems, k, p) for k, p in enumerate(peers)]
        for cp in sends:
            cp.start()
        for k, (px, py, pc) in enumerate(peers):
            _remote(x_ref, out_ref.at[4 * px + 2 * py + pc], send_sems, recv_sems, k, (px, py, pc)).wait_recv()
        for cp in sends:
            cp.wait_send()
        mine.wait()

    return pl.pallas_call(
        body, out_shape=jax.ShapeDtypeStruct((8,) + v.shape, v.dtype),
        in_specs=[pl.BlockSpec(memory_space=pltpu.VMEM)], out_specs=pl.BlockSpec(memory_space=pltpu.VMEM),
        scratch_shapes=[pltpu.SemaphoreType.DMA((7,)), pltpu.SemaphoreType.DMA((7,)), pltpu.SemaphoreType.DMA],
        name=name, compiler_params=_cparams())(v)


def _hosted_call(kern, *, grid, in_specs, out_specs, out_shape, scratch_shapes, args, name, comm=None, sem=None):
    if comm is None:
        res = pl.pallas_call(kern, grid=grid, in_specs=in_specs, out_specs=out_specs, out_shape=out_shape,
                             scratch_shapes=scratch_shapes, name=name, compiler_params=_cparams(sem))(*args)
        return list(res), []
    n_in, n_out, n_scr = len(in_specs), len(out_specs), len(scratch_shapes)
    c_in, c_out = len(comm.inputs), len(comm.out_shapes)

    def body(*refs):
        ins, ci = refs[:n_in], refs[n_in:n_in + c_in]
        outs = refs[n_in + c_in:n_in + c_in + n_out]
        co = refs[n_in + c_in + n_out:n_in + c_in + n_out + c_out]
        scr = refs[n_in + c_in + n_out + c_out:n_in + c_in + n_out + c_out + n_scr]
        cs = refs[n_in + c_in + n_out + c_out + n_scr:]
        ids = [pl.program_id(d) for d in range(len(grid))]
        inner_first = functools.reduce(jnp.logical_and, [i == 0 for i in ids[1:]], True)
        inner_last = functools.reduce(jnp.logical_and, [i == n - 1 for i, n in zip(ids[1:], grid[1:])], True)

        @pl.when(jnp.logical_and(ids[0] == 0, inner_first))
        def _():
            comm.begin(ci, co, cs)

        kern(*ins, *outs, *scr)

        @pl.when(jnp.logical_and(ids[0] == grid[0] // 2, inner_last))
        def _():
            comm.middle(ci, co, cs)

        @pl.when(jnp.logical_and(ids[0] == grid[0] - 1, inner_last))
        def _():
            comm.end(ci, co, cs)

    hbm = pl.BlockSpec(memory_space=pltpu.HBM)
    res = pl.pallas_call(
        body, grid=grid, in_specs=list(in_specs) + [hbm] * c_in, out_specs=list(out_specs) + [hbm] * c_out,
        out_shape=list(out_shape) + list(comm.out_shapes), scratch_shapes=list(scratch_shapes) + list(comm.scratch),
        input_output_aliases={n_in + i: n_out + o for i, o in comm.aliases.items()},
        name=name, compiler_params=_cparams(("arbitrary",) * len(grid)))(*args, *comm.inputs)
    return list(res[:n_out]), list(res[n_out:])


def _run_comm(comm, *, name):
    return _hosted_call(lambda: None, grid=(1,), in_specs=[], out_specs=[], out_shape=[], scratch_shapes=[], args=[],
                        name=name, comm=comm)[1]


class _Gather:
    def __init__(self, shards, kinds, items):
        used = sorted({w for w, _ in items})
        self.slot = {w: k for k, w in enumerate(used)}
        self.inputs = [shards[w] for w in used]
        self.items, self.kinds = list(items), kinds
        self.shapes = {w: shards[w].shape[1:] for w in used}
        self.out_shapes = [jax.ShapeDtypeStruct((r, 4 * n) if kinds[w] == "col" else (4 * r, n), shards[w].dtype)
                           for w, _ in items for r, n in [self.shapes[w]]]
        n_items = len(items)
        self.scratch = [pltpu.SemaphoreType.DMA((6 * n_items,)), pltpu.SemaphoreType.DMA((6 * n_items,)),
                        pltpu.SemaphoreType.DMA((n_items,))]
        self.aliases = {}

    def _piece(self, ref, w, qq, half):
        r, n = self.shapes[w]
        h = r // 2
        lo, size = (0, r) if half is None else (half * h, h)
        if self.kinds[w] == "col":
            return ref.at[pl.ds(pl.multiple_of(lo, 16), size), pl.ds(pl.multiple_of(qq * n, 128), n)]
        return ref.at[pl.ds(pl.multiple_of(qq * r + lo, 16), size), :]

    def _mine(self, ci, w, l, half):
        h = self.shapes[w][0] // 2
        return ci[self.slot[w]].at[l, pl.ds(pl.multiple_of(half * h, 16), h), :]

    def begin(self, ci, co, cs):
        send_sems, recv_sems, local_sems = cs
        x, y, c = _place()
        q = 2 * x + y
        for k, (w, l) in enumerate(self.items):
            pltpu.make_async_copy(ci[self.slot[w]].at[l], self._piece(co[k], w, q, None), local_sems.at[k]).start()
            for j, (cx, cy) in enumerate(_other_chips(x, y)):
                _remote(self._mine(ci, w, l, c), self._piece(co[k], w, q, c), send_sems, recv_sems, 6 * k + j,
                        (cx, cy, c)).start()

    def middle(self, ci, co, cs):
        send_sems, recv_sems, _ = cs
        x, y, c = _place()
        for k, (w, l) in enumerate(self.items):
            for j, (cx, cy) in enumerate(_other_chips(x, y)):
                win = self._piece(co[k], w, 2 * cx + cy, c)
                _remote(win, win, send_sems, recv_sems, 6 * k + j, (cx, cy, c)).wait_recv()
                _remote(win, win, send_sems, recv_sems, 6 * k + 3 + j, (x, y, 1 - c)).start()

    def end(self, ci, co, cs):
        send_sems, recv_sems, local_sems = cs
        x, y, c = _place()
        q = 2 * x + y
        for k, (w, l) in enumerate(self.items):
            for j, (cx, cy) in enumerate(_other_chips(x, y)):
                win = self._piece(co[k], w, 2 * cx + cy, 1 - c)
                _remote(win, win, send_sems, recv_sems, 6 * k + 3 + j, (x, y, 1 - c)).wait_recv()
        for k, (w, l) in enumerate(self.items):
            for j, (cx, cy) in enumerate(_other_chips(x, y)):
                _remote(self._mine(ci, w, l, c), self._piece(co[k], w, q, c), send_sems, recv_sems, 6 * k + j,
                        (cx, cy, c)).wait_send()
                win = self._piece(co[k], w, 2 * cx + cy, c)
                _remote(win, win, send_sems, recv_sems, 6 * k + 3 + j, (x, y, 1 - c)).wait_send()
            pltpu.make_async_copy(ci[self.slot[w]].at[l], self._piece(co[k], w, q, None), local_sems.at[k]).wait()


def _half_rows(ref, half, h):
    return ref.at[:, pl.ds(pl.multiple_of(half * h, 16), h), :]


class _Copies:
    def __init__(self, inputs, out_shapes, count, pairs, aliases=None, lands=None):
        self.inputs, self.out_shapes, self.pairs, self.lands = list(inputs), list(out_shapes), pairs, lands
        self.scratch = [pltpu.SemaphoreType.DMA((count,)), pltpu.SemaphoreType.DMA((count,))]
        self.aliases = aliases or {}

    def _copies(self, ci, co, cs):
        x, y, c = _place()
        return [_remote(src, dst, cs[0], cs[1], k, to) for k, (src, dst, to) in enumerate(self.pairs(ci, co, x, y, c))]

    def begin(self, ci, co, cs):
        for cp in self._copies(ci, co, cs):
            cp.start()

    def middle(self, ci, co, cs):
        pass

    def end(self, ci, co, cs):
        x, y, c = _place()
        for k, (src, dst, to) in enumerate(self.pairs(ci, co, x, y, c)):
            _remote(src, dst, cs[0], cs[1], k, to).wait_send()
            arrival = dst if self.lands is None else self.lands(co, x, y, c)[k]
            _remote(src, arrival, cs[0], cs[1], k, to).wait_recv()


def _send_to_all(v):
    def peers(x, y, c):
        return [(1 - x if f & 4 else x, 1 - y if f & 2 else y, 1 - c if f & 1 else c) for f in range(1, 8)]

    def pairs(ci, co, x, y, c):
        return [(ci[0], co[0].at[4 * x + 2 * y + c], peer) for peer in peers(x, y, c)]

    def lands(co, x, y, c):
        return [co[0].at[4 * px + 2 * py + pc] for px, py, pc in peers(x, y, c)]

    return _Copies([v], [jax.ShapeDtypeStruct((8,) + v.shape, v.dtype)], 7, pairs, lands=lands)


def _swap_halves(gs):
    def pairs(ci, co, x, y, c):
        return [(_half_rows(ci[k], 1 - c, g.shape[1] // 2), co[k], (x, y, 1 - c)) for k, g in enumerate(gs)]

    return _Copies(gs, [jax.ShapeDtypeStruct((g.shape[0], g.shape[1] // 2, g.shape[2]), g.dtype) for g in gs],
                   len(gs), pairs)


def _scatter_quarters(ps, kinds):
    part = [((p.shape[1], p.shape[2] // 4) if kind == "col" else (p.shape[1], p.shape[2])) for p, kind in zip(ps, kinds)]

    def pairs(ci, co, x, y, c):
        out = []
        for k, kind in enumerate(kinds):
            n = part[k][1]
            for j, (cx, cy) in enumerate(_other_chips(x, y)):
                qj = 2 * cx + cy
                src = ci[k].at[0, :, pl.ds(pl.multiple_of(qj * n, 128), n)] if kind == "col" else ci[k].at[qj]
                out.append((src, co[k].at[j], (cx, cy, c)))
        return out

    return _Copies(ps, [jax.ShapeDtypeStruct((3,) + pt, p.dtype) for pt, p in zip(part, ps)], 3 * len(ps), pairs)


def _share_halves(gs):
    def rows(co, k, half):
        h = gs[k].shape[0] // 2
        return co[k].at[pl.ds(pl.multiple_of(half * h, 16), h), :]

    def pairs(ci, co, x, y, c):
        return [(rows(co, k, c), rows(co, k, c), (x, y, 1 - c)) for k in range(len(gs))]

    def lands(co, x, y, c):
        return [rows(co, k, 1 - c) for k in range(len(gs))]

    return _Copies(gs, [jax.ShapeDtypeStruct(g.shape, g.dtype) for g in gs], len(gs), pairs,
                   aliases={k: k for k in range(len(gs))}, lands=lands)


def _wide_tile(n):
    return _pick(n, (2048, 1920, 1024, 512, 256, 128))


def _pair_sum(g, land, place, *, name):
    B, R, N = g.shape
    h = R // 2
    tr, tc = _pick(h, (256, 128)), _wide_tile(N)

    def kern(place_ref, g_ref, l_ref, o_ref):
        o_ref[...] = (g_ref[...] + l_ref[...]).astype(o_ref.dtype)

    grid_spec = pltpu.PrefetchScalarGridSpec(
        num_scalar_prefetch=1, grid=(B, h // tr, N // tc),
        in_specs=[pl.BlockSpec((None, tr, tc), lambda b, i, j, p: (b, p[1] * (h // tr) + i, j)),
                  pl.BlockSpec((None, tr, tc), lambda b, i, j, p: (b, i, j))],
        out_specs=pl.BlockSpec((None, tr, tc), lambda b, i, j, p: (b, i, j)))
    return pl.pallas_call(kern, grid_spec=grid_spec, out_shape=jax.ShapeDtypeStruct((B, h, N), BF16), name=name,
                          compiler_params=_cparams(("parallel", "parallel", "parallel")))(place, g, land)


def _quarter_sum(p, land, kind, shard_shape, place, *, name):
    L, r, n = shard_shape
    h = r // 2
    tr, tc = _pick(h, (256, 128)), _wide_tile(n)

    def kern(place_ref, p_ref, a_ref, b_ref, c_ref, o_ref):
        o_ref[...] = ((p_ref[...].astype(F32) + a_ref[...].astype(F32)) + b_ref[...].astype(F32)) + c_ref[...].astype(F32)

    if kind == "col":
        p_spec = pl.BlockSpec((None, tr, tc), lambda l, i, j, pr: (l, i, pr[0] * (n // tc) + j))
    else:
        p_spec = pl.BlockSpec((None, None, tr, tc), lambda l, i, j, pr: (l, pr[0], i, j))
    lands = [pl.BlockSpec((None, None, tr, tc), functools.partial(lambda l, i, j, pr, s: (s, l, i, j), s=s))
             for s in range(3)]
    grid_spec = pltpu.PrefetchScalarGridSpec(
        num_scalar_prefetch=1, grid=(L, h // tr, n // tc), in_specs=[p_spec] + lands,
        out_specs=pl.BlockSpec((None, tr, tc), lambda l, i, j, pr: (l, pr[1] * (h // tr) + i, j)))
    return pl.pallas_call(kern, grid_spec=grid_spec, out_shape=jax.ShapeDtypeStruct((L, r, n), F32), name=name,
                          compiler_params=_cparams(("parallel", "parallel", "parallel")))(place, p, land, land, land)


class _ReduceScatter:
    def __init__(self, grads, kinds, shard_shapes, place, tag):
        self.kinds, self.shapes, self.place, self.tag = kinds, shard_shapes, place, tag
        self.g3 = [g[None] if kind == "col" else g.reshape(4, g.shape[0] // 4, g.shape[1]) for g, kind in zip(grads, kinds)]

    def swap(self):
        return _swap_halves(self.g3)

    def pair_sums(self, lands):
        self.ps = [_pair_sum(g, land, self.place, name=f"rs_pair_sum_{self.tag}_{k}")
                   for k, (g, land) in enumerate(zip(self.g3, lands))]

    def scatter(self):
        return _scatter_quarters(self.ps, self.kinds)

    def quarter_sums(self, parts):
        self.halves = []
        for k, (p, part) in enumerate(zip(self.ps, parts)):
            p4 = p if self.kinds[k] == "col" else p[None]
            out = _quarter_sum(p4, part[:, None], self.kinds[k], (1,) + tuple(self.shapes[k]), self.place,
                               name=f"rs_quarter_sum_{self.tag}_{k}")
            self.halves.append(out[0])

    def share(self):
        return _share_halves(self.halves)

    def run(self):
        self.pair_sums(_run_comm(self.swap(), name=f"rs_swap_{self.tag}"))
        self.quarter_sums(_run_comm(self.scatter(), name=f"rs_scatter_{self.tag}"))
        return _run_comm(self.share(), name=f"rs_share_{self.tag}")


_WEIGHTS = ["mod_w", "mod_b", "norm1_g", "w_in", "gate_b", "conv_w", "conv_b", "conv_ln_g", "conv_ln_b", "w_conv_proj",
            "hgrn_lb", "hgrn_norm_g", "w_hgrn_proj", "sb_qn_g", "sb_kn_g", "w_sb_proj", "w_out", "norm2_g", "mlp_w1",
            "mlp_w2"]
_BIG = [("w_in", "col"), ("w_conv_proj", "col"), ("w_hgrn_proj", "col"), ("w_sb_proj", "col"), ("w_out", "row"),
        ("mlp_w1", "col"), ("mlp_w2", "row")]
_REPLICATED = ["mod_b", "norm1_g", "gate_b", "conv_b", "conv_ln_g", "conv_ln_b", "hgrn_lb", "hgrn_norm_g", "sb_qn_g",
               "sb_kn_g", "norm2_g"]
LANES = 128


class _Pack:
    def __init__(self, items):
        self.shapes = {n: a.shape for n, a in items}
        self.offsets, pos = {}, 0
        for n, a in items:
            self.offsets[n] = pos
            pos += math.prod(a.shape)
        self.rows = -(-pos // (8 * LANES)) * 8
        flat = jnp.concatenate([a.reshape(-1).astype(F32) for _, a in items])
        self.array = jnp.pad(flat, (0, self.rows * LANES - pos)).reshape(self.rows, LANES)

    def get(self, packed, name):
        lead = packed.shape[:-2]
        flat = packed.reshape(lead + (self.rows * LANES,))
        n = math.prod(self.shapes[name])
        return lax.slice_in_dim(flat, self.offsets[name], self.offsets[name] + n, axis=len(lead)).reshape(
            lead + self.shapes[name])


def _lower_bounds(hgrn_lb):
    p = jax.nn.softmax(hgrn_lb.astype(F32), axis=0)
    return jnp.cumsum(p, axis=0) - p[0:1]


def _layer_fwd(x, w, p, l, comms=(None, None)):
    S, D = x.shape
    r = {"x": x}
    (r["h"],) = _rowop(_fn_normmod, [(x, 0, D)], [p["n1g"], p["sc1"], p["sh1"]], [(D, BF16)], name=f"normmod1_fwd_{l}")
    proj = r["proj"] = _matmul(r["h"], w["w_in", l], name=f"w_in_fwd_{l}")
    r["cpre"] = _conv_fwd(proj, p["w32"], p["conv_b"], name=f"conv_fwd_{l}")
    (r["cact"],) = _rowop(_fn_lnsilu, [(r["cpre"], 0, CONV_CH)], [p["lng"], p["lnb"]], [(CONV_CH, BF16)],
                          name=f"conv_ln_fwd_{l}")
    arrived = lambda comm, got: w.update({(_BIG[k][0], layer): arr for (k, layer), arr in zip(comm.items, got)})
    (r["hg"], r["states"]), got = _hgrn_fwd(proj, p["lbk"], p["ng"], name=f"hgrn_fwd_{l}", comm=comms[0])
    if comms[0] is not None:
        arrived(comms[0], got)
    (r["sb"], r["rs"]), got = _sb_fwd(proj, p["qg"], p["kg"], name=f"sb_fwd_{l}", comm=comms[1])
    if comms[1] is not None:
        arrived(comms[1], got)
    r["y_c"] = _matmul(r["cact"], w["w_conv_proj", l], name=f"w_conv_proj_fwd_{l}")
    r["y_h"] = _matmul(r["hg"], w["w_hgrn_proj", l], name=f"w_hgrn_proj_fwd_{l}")
    r["y_s"] = _matmul(r["sb"], w["w_sb_proj", l], name=f"w_sb_proj_fwd_{l}")
    (r["merged"],) = _rowop(_fn_merge, [(proj, OFF_GL, 3 * D), (r["y_c"], 0, D), (r["y_h"], 0, D), (r["y_s"], 0, D)],
                            [p["gate_b"]], [(D, BF16)], name=f"merge_fwd_{l}")
    resid = lambda y, x_in, gate: (y,) + _fn_resid(x_in, y, gate)
    r["a_out"], r["x1"] = _matmul(r["merged"], w["w_out", l], name=f"w_out_fwd_{l}", post=resid, extras=[x],
                                  rows=[p["g1"]], out_dtypes=(F32, F32))
    (r["h2"],) = _rowop(_fn_normmod, [(r["x1"], 0, D)], [p["n2g"], p["sc2"], p["sh2"]], [(D, BF16)],
                        name=f"normmod2_fwd_{l}")
    r["u"], r["act"] = _matmul(r["h2"], w["mlp_w1", l], name=f"mlp_w1_fwd_{l}", post=lambda u: (u,) + _fn_relu2(u),
                               out_dtypes=(F32, BF16))
    r["m_out"], x2 = _matmul(r["act"], w["mlp_w2", l], name=f"mlp_w2_fwd_{l}", post=resid, extras=[r["x1"]],
                             rows=[p["g2"]], out_dtypes=(F32, F32))
    return x2, r


def _layer_bwd(dx2, r, w, p, l, grads, carry=None, last=None):
    S, D = dx2.shape
    small = {}

    def dweight(name, a, dy):
        grads[name, l] = _matmul(a, dy, ta=True, name=f"{name}_dw_{l}")

    stage = (lambda k, got: carry(k, got)) if carry is not None else (lambda k, got: None)

    (dm_out,), (dg2,) = _rowop_bwd(_fn_scale, [(r["m_out"], 0, D)], [p["g2"]], [dx2], [BF16], name=f"resid2_bwd_{l}")
    (du,) = _matmul(dm_out, w["mlp_w2", l], tb=True, name=f"mlp_w2_dx_{l}", extras=[r["u"]], out_dtypes=(BF16,),
                    post=lambda dact, u: (dact * (2.0 * jnp.maximum(u, 0.0)),))
    dweight("mlp_w2", r["act"], dm_out)
    dh2 = _matmul(du, w["mlp_w1", l], tb=True, name=f"mlp_w1_dx_{l}")
    dweight("mlp_w1", r["h2"], du)
    (dx1, da_out), (dg1, small["norm2_g"], dsc2, dsh2) = _rowop_bwd(
        _fn_resid_norm, [(r["x"], 0, D), (r["a_out"], 0, D)], [p["g1"], p["n2g"], p["sc2"], p["sh2"]], [dx2, dh2],
        [F32, BF16], name=f"resid1_norm2_bwd_{l}")
    dmerged = _matmul(da_out, w["w_out", l], tb=True, name=f"w_out_dx_{l}")
    dweight("w_out", r["merged"], da_out)
    (dgl, dy_c, dy_h, dy_s), (small["gate_b"],) = _rowop_bwd(
        _fn_merge, [(r["proj"], OFF_GL, 3 * D), (r["y_c"], 0, D), (r["y_h"], 0, D), (r["y_s"], 0, D)], [p["gate_b"]],
        [dmerged], [BF16] * 4, name=f"merge_bwd_{l}")
    dweight("w_conv_proj", r["cact"], dy_c)
    dweight("w_hgrn_proj", r["hg"], dy_h)
    dweight("w_sb_proj", r["sb"], dy_s)
    dcact = _matmul(dy_c, w["w_conv_proj", l], tb=True, name=f"w_conv_proj_dx_{l}")
    (dcpre,), (small["conv_ln_g"], small["conv_ln_b"]) = _rowop_bwd(
        _fn_lnsilu, [(r["cpre"], 0, CONV_CH)], [p["lng"], p["lnb"]], [dcact], [F32], name=f"conv_ln_bwd_{l}")
    (d_conv, dw32, small["conv_b"]), got = _conv_bwd(r["proj"], dcpre, p["w32"], name=f"conv_bwd_{l}",
                                                      comm=stage(0, None))
    small["conv_w"] = dw32[:CONV_WIDTH]
    dhg = _matmul(dy_h, w["w_hgrn_proj", l], tb=True, out_dtype=BF16, name=f"w_hgrn_proj_dx_{l}")
    (dq, df, di, dg, dlbk, dng), got = _hgrn_bwd(r["proj"], r["states"], dhg, p["lbk"], p["ng"], name=f"hgrn_bwd_{l}",
                                                 comm=stage(1, got))
    small["lower"] = -dlbk
    small["hgrn_norm_g"] = jnp.sum(dng, axis=0)
    dsb = _matmul(dy_s, w["w_sb_proj", l], tb=True, out_dtype=BF16, name=f"w_sb_proj_dx_{l}")
    (dsq, dsk, dsv, dqg, dkg), got = _sb_bwd(r["proj"], p["qg"], p["kg"], r["rs"], dsb, name=f"sb_bwd_{l}",
                                             comm=stage(2, got))
    stage(3, got)
    fold = lambda t: jnp.sum(t.reshape(-1, SB_DH), axis=0, keepdims=True)
    small["sb_qn_g"], small["sb_kn_g"] = fold(dqg), fold(dkg)
    dproj = jnp.concatenate([d_conv, dq, df, di, dg, dsq, dsk, dsv, dgl], axis=1)
    dweight("w_in", r["h"], dproj)
    norm1 = functools.partial(_rowop_bwd, _fn_normmod, [(r["x"], 0, D)], [p["n1g"], p["sc1"], p["sh1"]],
                              din_dtypes=[F32], add={0: dx1}, name=f"normmod1_bwd_{l}")
    if last is None:
        dh = _matmul(dproj, w["w_in", l], tb=True, name=f"w_in_dx_{l}")
        (dx,), (small["norm1_g"], dsc1, dsh1) = norm1(douts=[dh])
    else:
        dh, got = _matmul(dproj, w["w_in", l], tb=True, name=f"w_in_dx_{l}", comm=last(0, None))
        (dx,), (small["norm1_g"], dsc1, dsh1), got = norm1(douts=[dh], comm=last(1, got))
        last(2, got)
    small["mod"] = jnp.concatenate([dsh1, dsc1, dg1, dsh2, dsc2, dg2], axis=1)
    return dx, small


def kernel(x, c, mod_w, mod_b, norm1_g, w_in, gate_b, conv_w, conv_b, conv_ln_g, conv_ln_b, w_conv_proj, hgrn_lb, hgrn_norm_g, w_hgrn_proj, sb_qn_g, sb_kn_g, w_sb_proj, w_out, norm2_g, mlp_w1, mlp_w2, loss_target, m_mod_w, m_mod_b, m_norm1_g, m_w_in, m_gate_b, m_conv_w, m_conv_b, m_conv_ln_g, m_conv_ln_b, m_w_conv_proj, m_hgrn_lb, m_hgrn_norm_g, m_w_hgrn_proj, m_sb_qn_g, m_sb_kn_g, m_w_sb_proj, m_w_out, m_norm2_g, m_mlp_w1, m_mlp_w2, v_mod_w, v_mod_b, v_norm1_g, v_w_in, v_gate_b, v_conv_w, v_conv_b, v_conv_ln_g, v_conv_ln_b, v_w_conv_proj, v_hgrn_lb, v_hgrn_norm_g, v_w_hgrn_proj, v_sb_qn_g, v_sb_kn_g, v_w_sb_proj, v_w_out, v_norm2_g, v_mlp_w1, v_mlp_w2):
    given = dict(locals())
    wts = {n: given[n] for n in _WEIGHTS}
    mom = {n: given["m_" + n] for n in _WEIGHTS}
    var = {n: given["v_" + n] for n in _WEIGHTS}
    n_layers, D = norm1_g.shape
    xi, yi, ci = _place()
    q = 2 * xi + yi
    me = 4 * xi + 2 * yi + ci
    place = jnp.stack([q, ci]).astype(jnp.int32)
    n_mod = mod_w.shape[2]
    cw = conv_w.shape[2]

    pk1 = _Pack([("c", c), ("conv_w", conv_w)])
    got1 = _all_gather_small(pk1.array, name="gather_cond")
    c_act = jax.nn.silu(pk1.get(got1, "c")[:, 0, :])
    conv_full = jnp.concatenate([pk1.get(got1, "conv_w")[2 * k] for k in range(4)], axis=-1)

    mod_cols = []
    for l in range(n_layers):
        mb = lax.dynamic_slice_in_dim(mod_b[l], q * n_mod, n_mod)
        mod_cols.append(_matmul(c_act, mod_w, bl=l, name=f"mod_fwd_{l}") + mb[None, :])
    got2 = _all_gather_small(jnp.concatenate(mod_cols, axis=0), name="gather_mod")
    mods = []
    for l in range(n_layers):
        row = lax.dynamic_index_in_dim(got2[0::2], l * 8 + me, axis=1, keepdims=False)
        mods.append(jnp.split(row.reshape(1, 4 * n_mod), 6, axis=1))

    lower, lower_vjp = jax.vjp(_lower_bounds, hgrn_lb)

    shards = [wts[n].astype(BF16) for n, _ in _BIG]
    kinds = [k for _, k in _BIG]
    index = {n: k for k, (n, _) in enumerate(_BIG)}
    first = ["w_in", "w_conv_proj", "w_hgrn_proj", "w_sb_proj"]

    def gather(*names_layers):
        items = [(index[n], l) for names, l in names_layers for n in names if l < n_layers]
        return _Gather(shards, kinds, items) if items else None

    start = gather((first, 0))
    w = {(_BIG[k][0], layer): arr
         for (k, layer), arr in zip(start.items, _run_comm(start, name="gather_first_weights"))}

    def layer_params(l):
        sh1, sc1, g1, sh2, sc2, g2 = mods[l]
        return dict(sh1=sh1, sc1=sc1, g1=g1, sh2=sh2, sc2=sc2, g2=g2, n1g=norm1_g[l][None], n2g=norm2_g[l][None],
                    gate_b=gate_b[l][None], conv_b=conv_b[l][None], lng=conv_ln_g[l][None], lnb=conv_ln_b[l][None],
                    w32=jnp.pad(conv_full[l], ((0, CONV_HALO - CONV_WIDTH), (0, 0))), lbk=(1.0 - lower[l])[None],
                    ng=hgrn_norm_g[l][None], qg=jnp.tile(sb_qn_g[l][None], (1, SB_PAIR // SB_DH)),
                    kg=jnp.tile(sb_kn_g[l][None], (1, SB_PAIR // SB_DH)))

    params = [layer_params(l) for l in range(n_layers)]
    act, saved = x[0], []
    for l in range(n_layers):
        comms = (gather((["w_out", "mlp_w1"], l)), gather((["mlp_w2"], l), (first, l + 1)))
        act, r = _layer_fwd(act, w, params[l], l, comms=comms)
        saved.append(r)
    dact, loss_lanes = _loss_head(act, loss_target[0], name="loss_head")

    grads, smalls, reduced = {}, [None] * n_layers, {}

    def reduce_scatter(items, tag):
        return _ReduceScatter([grads[_BIG[k][0], layer] for k, layer in items], [kinds[k] for k, _ in items],
                              [shards[k].shape[1:] for k, _ in items], place, tag)

    def carried(l):
        items = [(k, l + 1) for k in range(len(_BIG))] + [(k, l) for k, (n, _) in enumerate(_BIG) if n != "w_in"]
        box = {}

        def carry(stage, got):
            if stage == 0:
                box["rs"] = reduce_scatter(items, f"l{l}")
                return box["rs"].swap()
            if stage == 1:
                box["rs"].pair_sums(got)
                return box["rs"].scatter()
            if stage == 2:
                box["rs"].quarter_sums(got)
                return box["rs"].share()
            reduced.update(zip(items, got))

        return carry

    def final(l):
        items = [(index["w_in"], l)]
        box = {}

        def step(stage, got):
            if stage == 0:
                box["rs"] = reduce_scatter(items, "w_in")
                box["rs"].pair_sums(_run_comm(box["rs"].swap(), name="rs_swap_w_in"))
                return box["rs"].scatter()
            if stage == 1:
                box["rs"].quarter_sums(got)
                return box["rs"].share()
            reduced.update(zip(items, got))

        return step

    for l in reversed(range(n_layers)):
        dact, smalls[l] = _layer_bwd(dact, saved[l], w, params[l], l, grads, carried(l) if l + 1 < n_layers else None,
                                     final(l) if l == 0 else None)
    grad_x = dact[None]
    rest = [(k, l) for l in range(n_layers) for k in range(len(_BIG)) if (k, l) not in reduced]
    if rest:
        reduced.update(zip(rest, reduce_scatter(rest, "rest").run()))

    stack = lambda k: jnp.stack([smalls[l][k] for l in range(n_layers)])
    (d_hgrn_lb,) = lower_vjp(stack("lower")[:, 0, :])
    items = [("loss", loss_lanes), ("mod", stack("mod")), ("hgrn_lb", d_hgrn_lb), ("conv_w", stack("conv_w"))]
    items += [(k, stack(k)) for k in ("norm1_g", "gate_b", "conv_b", "conv_ln_g", "conv_ln_b", "hgrn_norm_g", "sb_qn_g",
                                      "sb_kn_g", "norm2_g")]
    pk3 = _Pack(items)

    share_small = _send_to_all(pk3.array)
    delta, new_m, new_v, big, got3 = {}, {}, {}, {}, None
    for n, _ in _BIG:
        outs = None
        for l in reversed(range(n_layers)):
            args = (wts[n], reduced[index[n], l], mom[n], var[n], l, outs)
            if got3 is None:
                outs, (got3,) = _adamw_layer(*args, name=f"adamw_{n}_{l}", comm=share_small)
            else:
                outs = _adamw_layer(*args, name=f"adamw_{n}_{l}")
        big[n] = outs
    got3 = lax.dynamic_update_slice_in_dim(got3, pk3.array[None], me, axis=0)
    tot3 = _sum8(got3, name="sum_small_grads")
    loss = (0.5 / D) * jnp.sum(pk3.get(tot3, "loss"))
    g = {k: pk3.get(tot3, k).reshape(wts[k].shape) for k in _REPLICATED if k != "mod_b"}
    g["mod_b"] = pk3.get(tot3, "mod")[:, 0, :]
    g["conv_w"] = lax.dynamic_slice_in_dim(pk3.get(tot3, "conv_w"), q * cw, cw, axis=2)
    dmod_all = pk3.get(got3, "mod")[:, :, 0, :]
    g_mod_w = None
    for l in range(n_layers):
        cols = lax.dynamic_slice_in_dim(dmod_all[:, l, :], q * n_mod, n_mod, axis=1)
        g_mod_w = _matmul(c_act, cols, ta=True, layer=l, n_layers=n_layers, into=g_mod_w, name=f"mod_dw_{l}")
    g["mod_w"] = g_mod_w

    for n, _ in _BIG:
        g[n], delta[n], new_m[n], new_v[n] = big[n]
    two_d = lambda t: t.reshape(-1, t.shape[-1])
    outs = _adamw(two_d(mod_w), two_d(g["mod_w"]), two_d(m_mod_w), two_d(v_mod_w), name="adamw_mod_w")
    delta["mod_w"], new_m["mod_w"], new_v["mod_w"] = (t.reshape(mod_w.shape) for t in outs)
    rest = _REPLICATED + ["conv_w"]
    packs = [_Pack([(n, src[n]) for n in rest]) for src in (wts, g, mom, var)]
    outs = _adamw(*[pk.array for pk in packs], name="adamw_small")
    for n in rest:
        delta[n], new_m[n], new_v[n] = (packs[0].get(t, n) for t in outs)

    return (loss, grad_x, *[g[n] for n in _WEIGHTS], *[delta[n] for n in _WEIGHTS], *[new_m[n] for n in _WEIGHTS],
            *[new_v[n] for n in _WEIGHTS])
```

```python
import functools
import math

import jax
import jax.numpy as jnp
from jax import lax
from jax.experimental import pallas as pl
from jax.experimental.pallas import tpu as pltpu

F32 = jnp.float32
BF16 = jnp.bfloat16
MESH = pl.DeviceIdType.MESH

EPS = 1e-6
CONV_CH = 512
CONV_WIDTH = 31
CONV_HALO = 32
HG_HEADS = 4
HG_D = 128
HG_CHUNK = 64
HG_SUB = 32
SB_HEADS = 8
SB_DH = 64
SB_BLK = 128
SB_PAIR = 128
SB_SKIP = -104.0
OFF_CONV, OFF_HG, OFF_SB, OFF_GL = 0, 1024, 3072, 4608
ADAM_LR, ADAM_B1, ADAM_B2, ADAM_EPS, ADAM_WD, ADAM_STEP = 0.001, 0.9, 0.999, 1e-08, 0.01, 10
VMEM_LIMIT_BYTES = 56 * 1024 * 1024
ROW_TILE = 256


def _cparams(sem=None, **kw):
    return pltpu.CompilerParams(dimension_semantics=sem, vmem_limit_bytes=VMEM_LIMIT_BYTES, **kw)


def _pick(n, cands):
    for c in cands:
        if n % c == 0:
            return c
    return n


MATMUL_VMEM_BUDGET = 40 * 1024 * 1024


def _tile_options(n, cap):
    opts = [t for t in range(cap - cap % 128, 0, -128) if n % t == 0]
    return opts or [n]


def _matmul_tiles(M, N, K, size_a, size_b, size_o, in_acc):
    for tm in _tile_options(M, 1024):
        for tk in _tile_options(K, 2048):
            for tn in _tile_options(N, 1280):
                need = 2 * (tm * tk * size_a + tk * tn * size_b + tm * tn * size_o)
                if K > tk and not in_acc:
                    need += tm * tn * 4
                if need <= MATMUL_VMEM_BUDGET:
                    return tm, tn, tk
    raise ValueError(f"no matmul tiling fits VMEM for {(M, N, K)}")
def _matmul(a, b, *, ta=False, tb=False, bl=None, out_dtype=F32, name, into=None, layer=None, n_layers=None,
            post=None, extras=(), rows=(), out_dtypes=None, comm=None):
    M, K = (a.shape[1], a.shape[0]) if ta else a.shape
    N = b.shape[-2] if tb else b.shape[-1]
    if post is not None:
        return _matmul_post(a, b, M, N, K, ta, tb, post, extras, rows, out_dtypes, name)
    assert comm is None or layer is None
    in_acc = jnp.dtype(out_dtype) == jnp.dtype(F32)
    tm, tn, tk = _matmul_tiles(M, N, K, a.dtype.itemsize, b.dtype.itemsize, jnp.dtype(out_dtype).itemsize, in_acc)
    nk = K // tk
    a_spec = pl.BlockSpec((tk, tm), lambda i, j, k: (k, i)) if ta else pl.BlockSpec((tm, tk), lambda i, j, k: (i, k))
    if bl is None:
        b_spec = pl.BlockSpec((tn, tk), lambda i, j, k: (j, k)) if tb else pl.BlockSpec((tk, tn), lambda i, j, k: (k, j))
    elif tb:
        b_spec = pl.BlockSpec((None, tn, tk), lambda i, j, k: (bl, j, k))
    else:
        b_spec = pl.BlockSpec((None, tk, tn), lambda i, j, k: (bl, k, j))
    dn = (((0 if ta else 1,), (1 if tb else 0,)), ((), ()))

    use_scratch = nk > 1 and not in_acc

    def kern(a_ref, b_ref, *rest):
        o_ref = rest[-2] if use_scratch else rest[-1]
        prod = lambda: lax.dot_general(a_ref[...].astype(BF16), b_ref[...].astype(BF16), dn,
                                       preferred_element_type=F32)
        if nk == 1:
            o_ref[...] = prod().astype(o_ref.dtype).reshape(o_ref.shape)
            return
        acc_ref = rest[-1] if use_scratch else o_ref
        k = pl.program_id(2)

        @pl.when(k == 0)
        def _():
            acc_ref[...] = prod().reshape(acc_ref.shape)

        @pl.when(k > 0)
        def _():
            acc_ref[...] += prod().reshape(acc_ref.shape)

        if use_scratch:
            @pl.when(k == nk - 1)
            def _():
                o_ref[...] = acc_ref[...].astype(o_ref.dtype).reshape(o_ref.shape)

    in_specs, args, aliases = [a_spec, b_spec], [a, b], {}
    if layer is None:
        out_shape = jax.ShapeDtypeStruct((M, N), out_dtype)
        out_spec = pl.BlockSpec((tm, tn), lambda i, j, k: (i, j))
    else:
        out_shape = jax.ShapeDtypeStruct((n_layers, M, N), out_dtype)
        out_spec = pl.BlockSpec((1, tm, tn), lambda i, j, k: (layer, i, j))
        if into is not None:
            in_specs.append(pl.BlockSpec(memory_space=pl.ANY))
            args.append(into)
            aliases = {2: 0}
    if comm is not None:
        (out,), got = _hosted_call(kern, grid=(M // tm, N // tn, nk), in_specs=in_specs, out_specs=[out_spec],
                                   out_shape=[out_shape], scratch_shapes=[pltpu.VMEM((tm, tn), F32)] if use_scratch else [],
                                   args=args, name=name, comm=comm)
        return out, got
    return pl.pallas_call(
        kern, grid=(M // tm, N // tn, nk), in_specs=in_specs, out_specs=out_spec, out_shape=out_shape,
        scratch_shapes=[pltpu.VMEM((tm, tn), F32)] if use_scratch else [],
        input_output_aliases=aliases, name=name,
        compiler_params=_cparams(("parallel", "parallel", "arbitrary")))(*args)


def _matmul_post(a, b, M, N, K, ta, tb, post, extras, rows, out_dtypes, name):
    per_elem = sum(e.dtype.itemsize for e in extras) + sum(jnp.dtype(d).itemsize for d in out_dtypes)
    fits = lambda tm, tn: 2 * (tm * K * a.dtype.itemsize + K * tn * b.dtype.itemsize + tm * tn * per_elem) <= MATMUL_VMEM_BUDGET
    tm, tn = next((tm, tn) for tm in _tile_options(M, 1024) for tn in _tile_options(N, 1280) if fits(tm, tn))
    a_spec = pl.BlockSpec((K, tm), lambda i, j: (0, i)) if ta else pl.BlockSpec((tm, K), lambda i, j: (i, 0))
    b_spec = pl.BlockSpec((tn, K), lambda i, j: (j, 0)) if tb else pl.BlockSpec((K, tn), lambda i, j: (0, j))
    tile = pl.BlockSpec((tm, tn), lambda i, j: (i, j))
    row = pl.BlockSpec((1, tn), lambda i, j: (0, j))
    dn = (((0 if ta else 1,), (1 if tb else 0,)), ((), ()))
    n_ex = len(extras) + len(rows)

    def kern(a_ref, b_ref, *rest):
        prod = lax.dot_general(a_ref[...].astype(BF16), b_ref[...].astype(BF16), dn, preferred_element_type=F32)
        res = post(prod, *[r[...].astype(F32) for r in rest[:n_ex]])
        for val, o_ref in zip(res, rest[n_ex:]):
            o_ref[...] = val.astype(o_ref.dtype)

    return pl.pallas_call(
        kern, grid=(M // tm, N // tn), in_specs=[a_spec, b_spec] + [tile] * len(extras) + [row] * len(rows),
        out_specs=[tile] * len(out_dtypes), out_shape=[jax.ShapeDtypeStruct((M, N), d) for d in out_dtypes], name=name,
        compiler_params=_cparams(("parallel", "parallel")))(a, b, *extras, *rows)


def _col_specs(off, width, T):
    bw = math.gcd(width, off) if off else width
    return [pl.BlockSpec((T, bw), functools.partial(lambda i, c: (i, c), c=off // bw + p)) for p in range(width // bw)]


def _gather_rows(refs, counts):
    vals, pos = [], 0
    for n in counts:
        parts = [refs[pos + p][...].astype(F32) for p in range(n)]
        pos += n
        vals.append(parts[0] if n == 1 else jnp.concatenate(parts, axis=1))
    return vals, pos


def _rowop(fn, ins, params, outs, *, name):
    S = ins[0][0].shape[0]
    T = min(ROW_TILE, S)
    in_specs, counts, args = [], [], []
    for arr, off, width in ins:
        sp = _col_specs(off, width, T)
        in_specs += sp
        counts.append(len(sp))
        args += [arr] * len(sp)
    in_specs += [pl.BlockSpec(p.shape, lambda i: (0, 0)) for p in params]

    def kern(*refs):
        vals, pos = _gather_rows(refs, counts)
        pv = [refs[pos + p][...] for p in range(len(params))]
        pos += len(params)
        res = fn(*vals, *pv)
        for r, o_ref in zip(res, refs[pos:]):
            o_ref[...] = r.astype(o_ref.dtype)

    return pl.pallas_call(
        kern, grid=(S // T,), in_specs=in_specs,
        out_specs=[pl.BlockSpec((T, w), lambda i: (i, 0)) for w, _ in outs],
        out_shape=[jax.ShapeDtypeStruct((S, w), dt) for w, dt in outs],
        name=name, compiler_params=_cparams(("parallel",)))(*args, *params)


def _rowop_bwd(fn, ins, params, douts, din_dtypes, *, name, add=None, comm=None):
    add = add or {}
    S = ins[0][0].shape[0]
    T = min(ROW_TILE, S)
    in_specs, counts, args = [], [], []
    for arr, off, width in ins:
        sp = _col_specs(off, width, T)
        in_specs += sp
        counts.append(len(sp))
        args += [arr] * len(sp)
    in_specs += [pl.BlockSpec(p.shape, lambda i: (0, 0)) for p in params]
    in_specs += [pl.BlockSpec((T, d.shape[1]), lambda i: (i, 0)) for d in douts]
    add_keys = sorted(add)
    in_specs += [pl.BlockSpec((T, add[k].shape[1]), lambda i: (i, 0)) for k in add_keys]
    want = [k for k, dt in enumerate(din_dtypes) if dt is not None]

    def kern(*refs):
        vals, pos = _gather_rows(refs, counts)
        pv = [refs[pos + p][...] for p in range(len(params))]
        pos += len(params)
        cts = [refs[pos + p][...].astype(F32) for p in range(len(douts))]
        pos += len(douts)
        adds = {k: refs[pos + p][...].astype(F32) for p, k in enumerate(add_keys)}
        pos += len(add_keys)
        _, vjp = jax.vjp(fn, *vals, *pv)
        grads = vjp(tuple(cts))
        for k in want:
            g = grads[k] + adds[k] if k in adds else grads[k]
            refs[pos][...] = g.astype(refs[pos].dtype)
            pos += 1
        first = pl.program_id(0) == 0
        for p in range(len(params)):
            gp, o_ref = grads[len(ins) + p], refs[pos + p]

            @pl.when(first)
            def _(gp=gp, o_ref=o_ref):
                o_ref[...] = gp

            @pl.when(jnp.logical_not(first))
            def _(gp=gp, o_ref=o_ref):
                o_ref[...] += gp

    out_specs = [pl.BlockSpec((T, ins[k][2]), lambda i: (i, 0)) for k in want]
    out_specs += [pl.BlockSpec(p.shape, lambda i: (0, 0)) for p in params]
    out_shape = [jax.ShapeDtypeStruct((S, ins[k][2]), din_dtypes[k]) for k in want]
    out_shape += [jax.ShapeDtypeStruct(p.shape, F32) for p in params]
    res, got = _hosted_call(
        kern, grid=(S // T,), in_specs=in_specs, out_specs=out_specs, out_shape=out_shape, scratch_shapes=[],
        args=[*args, *params, *douts, *[add[k] for k in add_keys]], name=name, comm=comm, sem=("arbitrary",))
    dins = [None] * len(ins)
    for p, k in enumerate(want):
        dins[k] = res[p]
    return (dins, res[len(want):]) if comm is None else (dins, res[len(want):], got)


def _rms(x, g):
    return x * lax.rsqrt(jnp.mean(x * x, axis=-1, keepdims=True) + EPS) * g


def _fn_normmod(x, g, sc, sh):
    return (_rms(x, g) * (1.0 + sc) + sh,)


def _fn_lnsilu(c, g, b):
    mu = jnp.mean(c, axis=-1, keepdims=True)
    var = jnp.mean(jnp.square(c - mu), axis=-1, keepdims=True)
    y = (c - mu) * lax.rsqrt(var + EPS) * g + b
    return (y * jax.nn.sigmoid(y),)


def _fn_merge(gl, yc, yh, ys, gb):
    d = yc.shape[1]
    g = jax.nn.sigmoid(gl + gb)
    return (g[:, :d] * yc + g[:, d:2 * d] * yh + g[:, 2 * d:] * ys,)


def _fn_resid(x, y, g):
    return (x + g * y,)


def _fn_resid_norm(x, y, g, n, sc, sh):
    x1 = x + g * y
    return (x1,) + _fn_normmod(x1, n, sc, sh)


def _fn_scale(y, g):
    return (g * y,)


def _fn_relu2(u):
    return (jnp.square(jnp.maximum(u, 0.0)),)


def _conv_specs(S, T):
    r = T // CONV_HALO
    cur = [pl.BlockSpec((T, CONV_CH), lambda i: (i, 0)), pl.BlockSpec((T, CONV_CH), lambda i: (i, 1))]
    prev = [pl.BlockSpec((CONV_HALO, CONV_CH), lambda i: (jnp.maximum(i * r - 1, 0), 0)),
            pl.BlockSpec((CONV_HALO, CONV_CH), lambda i: (jnp.maximum(i * r - 1, 0), 1))]
    return cur + prev


def _glu_ext(a_ref, g_ref, ah_ref, gh_ref):
    a = a_ref[...]
    sg = jax.nn.sigmoid(g_ref[...])
    uh = jnp.where(pl.program_id(0) > 0, ah_ref[...] * jax.nn.sigmoid(gh_ref[...]), 0.0)
    return a, sg, jnp.concatenate([uh, a * sg], axis=0)


def _shift_up(xe, k, T):
    return xe[:T] if k == 0 else pltpu.roll(xe, shift=xe.shape[0] - k, axis=0)[:T]


def _conv_fwd(proj, w32, b, *, name):
    S = proj.shape[0]
    T = min(ROW_TILE, S)
    lead = CONV_HALO - (CONV_WIDTH - 1)

    def kern(a_ref, g_ref, ah_ref, gh_ref, w_ref, b_ref, o_ref):
        _, _, ue = _glu_ext(a_ref, g_ref, ah_ref, gh_ref)
        acc = jnp.zeros((T, CONV_CH), F32) + b_ref[...]
        for j in range(CONV_WIDTH):
            acc = acc + w_ref[j:j + 1, :] * _shift_up(ue, lead + j, T)
        o_ref[...] = acc

    const = lambda shape: pl.BlockSpec(shape, lambda i: (0, 0))
    return pl.pallas_call(
        kern, grid=(S // T,), in_specs=_conv_specs(S, T) + [const(w32.shape), const(b.shape)],
        out_specs=pl.BlockSpec((T, CONV_CH), lambda i: (i, 0)),
        out_shape=jax.ShapeDtypeStruct((S, CONV_CH), F32), name=name,
        compiler_params=_cparams(("parallel",)))(proj, proj, proj, proj, w32, b)


def _conv_bwd(proj, dc, w32, *, name, comm=None):
    S = proj.shape[0]
    T = min(ROW_TILE, S)
    nt = S // T
    r = T // CONV_HALO
    lead = CONV_HALO - (CONV_WIDTH - 1)
    last_halo = S // CONV_HALO - 1

    def kern(a_ref, g_ref, ah_ref, gh_ref, dc_ref, dcn_ref, w_ref, dag_ref, dw_ref, db_ref):
        i = pl.program_id(0)
        a, sg, ue = _glu_ext(a_ref, g_ref, ah_ref, gh_ref)
        dc_t = dc_ref[...]
        de = jnp.concatenate([dc_t, jnp.where(i < nt - 1, dcn_ref[...], 0.0)], axis=0)

        @pl.when(i == 0)
        def _():
            dw_ref[...] = jnp.zeros_like(dw_ref)
            db_ref[...] = jnp.zeros_like(db_ref)

        du = jnp.zeros((T, CONV_CH), F32)
        for j in range(CONV_WIDTH):
            du = du + w_ref[j:j + 1, :] * _shift_up(de, CONV_WIDTH - 1 - j, T)
            dw_ref[j:j + 1, :] += jnp.sum(dc_t * _shift_up(ue, lead + j, T), axis=0, keepdims=True)
        db_ref[...] += jnp.sum(dc_t, axis=0, keepdims=True)
        dag_ref[:, :CONV_CH] = (du * sg).astype(BF16)
        dag_ref[:, CONV_CH:] = (du * a * sg * (1.0 - sg)).astype(BF16)

    const = lambda shape: pl.BlockSpec(shape, lambda i: (0, 0))
    in_specs = _conv_specs(S, T) + [
        pl.BlockSpec((T, CONV_CH), lambda i: (i, 0)),
        pl.BlockSpec((CONV_HALO, CONV_CH), lambda i: (jnp.minimum((i + 1) * r, last_halo), 0)),
        const(w32.shape)]
    return _hosted_call(
        kern, grid=(nt,), in_specs=in_specs,
        out_specs=[pl.BlockSpec((T, 2 * CONV_CH), lambda i: (i, 0)), const(w32.shape), const((1, CONV_CH))],
        out_shape=[jax.ShapeDtypeStruct((S, 2 * CONV_CH), BF16), jax.ShapeDtypeStruct(w32.shape, F32),
                   jax.ShapeDtypeStruct((1, CONV_CH), F32)],
        scratch_shapes=[], args=[proj, proj, proj, proj, dc, dc, w32], name=name, comm=comm, sem=("arbitrary",))


def _iota2(shape, dim):
    return lax.broadcasted_iota(jnp.int32, shape, dim)


def _running(x, seg, later):
    n = x.shape[0]
    pos = _iota2(x.shape, 0) & (seg - 1)
    k = 1
    while k < seg:
        if later:
            x = x + jnp.where(pos < seg - k, pltpu.roll(x, n - k, axis=0), 0.0)
        else:
            x = x + jnp.where(pos >= k, pltpu.roll(x, k, axis=0), 0.0)
        k *= 2
    return x


@functools.partial(jax.custom_vjp, nondiff_argnums=(1,))
def _prefix(x, seg):
    return _running(x, seg, False)


_prefix.defvjp(lambda x, seg: (_running(x, seg, False), None), lambda seg, _, g: (_running(g, seg, True),))


def _hg_chunk(q, f, iv, g, st, lbk, ng):
    n, sub = HG_CHUNK, HG_SUB
    kk = lbk * jax.nn.sigmoid(-f)
    lf = jnp.log(1.0 - kk)
    b = _prefix(lf, n)
    bs = _prefix(lf, sub)
    bt = jnp.sum(lf, axis=0, keepdims=True)
    qh = q * jax.nn.sigmoid(q)
    dot_nt = lambda x, y: lax.dot_general(x.astype(BF16), y.astype(BF16), (((1,), (1,)), ((), ())),
                                          preferred_element_type=F32)
    o = dot_nt(qh * jnp.exp(b), st)
    b0 = b - bs
    qs = qh * jnp.exp(bs)
    col = _iota2((sub, n), 1)
    rows = []
    for blk in range(n // sub):
        lo = blk * sub
        sl = slice(lo, lo + sub)
        acc = o[sl]
        if blk > 0:
            ref = jnp.concatenate([b0[sl]] * (n // sub), axis=0)
            kd = kk * jnp.exp(jnp.minimum(ref - b, 0.0))
            sc = jnp.where(col < lo, dot_nt(qs[sl], kd), 0.0)
            acc = acc + jnp.dot(sc.astype(BF16), iv.astype(BF16), preferred_element_type=F32)
        bq, bk = bs[sl][None, :, :], bs[sl][:, None, :]
        s_i = lax.broadcasted_iota(jnp.int32, (sub, sub, HG_D), 0)
        t_i = lax.broadcasted_iota(jnp.int32, (sub, sub, HG_D), 1)
        keep = s_i <= t_i
        p = jnp.where(keep, qh[sl][None, :, :] * kk[sl][:, None, :] * jnp.exp(jnp.where(keep, bq - bk, 0.0)), 0.0)
        w = jnp.sum(p, axis=-1, keepdims=True)
        acc = acc + jnp.sum(w * iv[sl][:, None, :], axis=0)
        rows.append(acc)
    o = jnp.concatenate(rows, axis=0)
    kd = kk * jnp.exp(bt - b)
    st_new = jnp.exp(bt) * st + lax.dot_general(iv.astype(BF16), kd.astype(BF16), (((0,), (0,)), ((), ())),
                                                     preferred_element_type=F32)
    out = _rms(o, ng) * (g * jax.nn.sigmoid(g))
    return out, st_new


def _hg_tile(S):
    return min(512, S)


def _hg_in_specs(rt, rev, nr):
    width = HG_HEADS * HG_D
    base = OFF_HG // width
    row = (lambda r: nr - 1 - r) if rev else (lambda r: r)
    return [pl.BlockSpec((rt, width), functools.partial(lambda r, k: (row(r), base + k), k=k)) for k in range(4)]


def _hg_cols(h):
    return slice(h * HG_D, (h + 1) * HG_D)


def _hgrn_fwd(proj, lbk, ng, *, name, comm=None):
    S = proj.shape[0]
    rt = _hg_tile(S)
    nr, nc = S // rt, rt // HG_CHUNK

    def kern(q_ref, f_ref, i_ref, g_ref, lbk_ref, ng_ref, o_ref, st_out_ref, st_ref):
        @pl.when(pl.program_id(0) == 0)
        def _():
            st_ref[...] = jnp.zeros_like(st_ref)

        def body(c, carry):
            rows = pl.ds(pl.multiple_of(c * HG_CHUNK, HG_CHUNK), HG_CHUNK)
            for h in range(HG_HEADS):
                cols = _hg_cols(h)
                st = st_ref[h]
                st_out_ref[h, c] = st
                out, st_new = _hg_chunk(q_ref[rows, cols], f_ref[rows, cols], i_ref[rows, cols], g_ref[rows, cols], st,
                                        lbk_ref[:, cols], ng_ref[...])
                o_ref[rows, cols] = out.astype(o_ref.dtype)
                st_ref[h] = st_new
            return carry

        lax.fori_loop(0, nc, body, 0)

    width = HG_HEADS * HG_D
    in_specs = _hg_in_specs(rt, False, nr) + [pl.BlockSpec((1, width), lambda r: (0, 0)),
                                               pl.BlockSpec((1, HG_D), lambda r: (0, 0))]
    return _hosted_call(
        kern, grid=(nr,), in_specs=in_specs,
        out_specs=[pl.BlockSpec((rt, width), lambda r: (r, 0)),
                   pl.BlockSpec((HG_HEADS, nc, HG_D, HG_D), lambda r: (0, r, 0, 0))],
        out_shape=[jax.ShapeDtypeStruct((S, width), BF16),
                   jax.ShapeDtypeStruct((HG_HEADS, S // HG_CHUNK, HG_D, HG_D), F32)],
        scratch_shapes=[pltpu.VMEM((HG_HEADS, HG_D, HG_D), F32)],
        args=[proj, proj, proj, proj, lbk, ng], name=name, comm=comm, sem=("arbitrary",))


def _hgrn_bwd(proj, states, dout, lbk, ng, *, name, comm=None):
    S = proj.shape[0]
    rt = _hg_tile(S)
    nr, nc = S // rt, rt // HG_CHUNK
    width = HG_HEADS * HG_D

    def kern(q_ref, f_ref, i_ref, g_ref, st_in_ref, do_ref, lbk_ref, ng_ref,
             dq_ref, df_ref, di_ref, dg_ref, dlbk_ref, dng_ref, dst_ref):
        @pl.when(pl.program_id(0) == 0)
        def _():
            dst_ref[...] = jnp.zeros_like(dst_ref)
            dlbk_ref[...] = jnp.zeros_like(dlbk_ref)
            dng_ref[...] = jnp.zeros_like(dng_ref)

        def body(k, carry):
            c = nc - 1 - k
            rows = pl.ds(pl.multiple_of(c * HG_CHUNK, HG_CHUNK), HG_CHUNK)
            for h in range(HG_HEADS):
                cols = _hg_cols(h)
                _, vjp = jax.vjp(_hg_chunk, q_ref[rows, cols], f_ref[rows, cols], i_ref[rows, cols], g_ref[rows, cols],
                                 st_in_ref[h, c], lbk_ref[:, cols], ng_ref[...])
                dq, df, di, dg, dst, dlbk, dng = vjp((do_ref[rows, cols].astype(F32), dst_ref[h]))
                dq_ref[rows, cols] = dq.astype(BF16)
                df_ref[rows, cols] = df.astype(BF16)
                di_ref[rows, cols] = di.astype(BF16)
                dg_ref[rows, cols] = dg.astype(BF16)
                dst_ref[h] = dst
                dlbk_ref[:, cols] += dlbk
                dng_ref[h] += dng
            return carry

        lax.fori_loop(0, nc, body, 0)

    rev = lambda r: nr - 1 - r
    tile = pl.BlockSpec((rt, width), lambda r: (rev(r), 0))
    in_specs = _hg_in_specs(rt, True, nr) + [
        pl.BlockSpec((HG_HEADS, nc, HG_D, HG_D), lambda r: (0, rev(r), 0, 0)), tile,
        pl.BlockSpec((1, width), lambda r: (0, 0)), pl.BlockSpec((1, HG_D), lambda r: (0, 0))]
    return _hosted_call(
        kern, grid=(nr,), in_specs=in_specs,
        out_specs=[tile, tile, tile, tile, pl.BlockSpec((1, width), lambda r: (0, 0)),
                   pl.BlockSpec((HG_HEADS, 1, HG_D), lambda r: (0, 0, 0))],
        out_shape=[jax.ShapeDtypeStruct((S, width), BF16)] * 4 + [
            jax.ShapeDtypeStruct((1, width), F32), jax.ShapeDtypeStruct((HG_HEADS, 1, HG_D), F32)],
        scratch_shapes=[pltpu.VMEM((HG_HEADS, HG_D, HG_D), F32)],
        args=[proj, proj, proj, proj, states, dout, lbk, ng], name=name, comm=comm, sem=("arbitrary",))


def _sb_scores(km, qi):
    return lax.dot_general(km, qi, (((1,), (1,)), ((), ())), preferred_element_type=F32)


def _sb_weights(zt, r_run, diag):
    n = SB_BLK
    sp = jnp.maximum(zt, 0.0) + jnp.log(1.0 + jnp.exp(-jnp.abs(zt)))
    lk = -sp
    if diag:
        keep = (_iota2(zt.shape, 0) & (n - 1)) < _iota2(zt.shape, 1)
        lk = jnp.where(keep, lk, 0.0)
    tails = [_running(lk[a * n:(a + 1) * n], n, True) for a in range(2)]
    between = jnp.concatenate([tails[a] + r_run[a] for a in range(2)], axis=0)
    wgt = jnp.exp(zt + between)
    if diag:
        wgt = jnp.where(keep, wgt, 0.0)
    return sp, wgt, [t[0:1, :] for t in tails]


def _sb_norm_pair(x, g2, lane_lo):
    sq = x * x
    ms_lo = jnp.sum(jnp.where(lane_lo, sq, 0.0), axis=-1, keepdims=True)
    ms_hi = jnp.sum(jnp.where(lane_lo, 0.0, sq), axis=-1, keepdims=True)
    return x * lax.rsqrt(jnp.where(lane_lo, ms_lo, ms_hi) * (1.0 / SB_DH) + EPS) * g2


def _sb_specs(S):
    base = OFF_SB // SB_PAIR
    per = SB_HEADS * SB_DH // SB_PAIR
    cols = [pl.BlockSpec((S, SB_PAIR), functools.partial(lambda p, k: (0, base + per * k + p), k=k)) for k in range(3)]
    return cols + [pl.BlockSpec((1, SB_PAIR), lambda p: (0, 0))] * 2


def _sb_rows(i):
    return pl.ds(pl.multiple_of(i * SB_BLK, SB_BLK), SB_BLK)


def _sb_both(j, a=None):
    if a is None:
        return pl.ds(pl.multiple_of(j * 2 * SB_BLK, 2 * SB_BLK), 2 * SB_BLK)
    return pl.ds(pl.multiple_of(j * 2 * SB_BLK + a * SB_BLK, SB_BLK), SB_BLK)


def _sb_fwd(proj, qg2, kg2, *, name, comm=None):
    S = proj.shape[0]
    nb = S // SB_BLK
    scale = SB_DH ** -0.5
    n_pairs = SB_HEADS * SB_DH // SB_PAIR

    def kern(q_ref, k_ref, v_ref, qg_ref, kg_ref, o_ref, rs_ref, qp_ref, km_ref, vt_ref):
        lane_lo = _iota2((SB_BLK, SB_PAIR), 1) < SB_DH

        def prologue(j, carry):
            rows = _sb_rows(j)
            qp_ref[rows, :] = (_sb_norm_pair(q_ref[rows, :], qg_ref[...], lane_lo) * scale).astype(BF16)
            kn = _sb_norm_pair(k_ref[rows, :], kg_ref[...], lane_lo)
            v = v_ref[rows, :]
            for a, mine in enumerate((lane_lo, jnp.logical_not(lane_lo))):
                km_ref[_sb_both(j, a), :] = jnp.where(mine, kn, 0.0).astype(BF16)
                vt_ref[:, _sb_both(j, a)] = jnp.where(mine, v, 0.0).T.astype(BF16)
            return carry

        lax.fori_loop(0, nb, prologue, 0)

        diagonal = lambda i: _sb_scores(km_ref[_sb_both(i), :], qp_ref[_sb_rows(i), :])

        def qblock(i, zt):
            qi = qp_ref[_sb_rows(i), :]

            scores = lambda j: _sb_scores(km_ref[_sb_both(jnp.maximum(j, 0)), :], qi)
            output = lambda j, wgt: jnp.dot(vt_ref[:, _sb_both(j)], wgt, preferred_element_type=F32)

            def note(j, r_run):
                for a in range(2):
                    rs_ref[a, i, pl.ds(j, 1), :] = r_run[a]
                return jnp.maximum(jnp.max(r_run[0]), jnp.max(r_run[1])) > SB_SKIP

            def noted(j, r_run):
                return lax.cond(j >= 0, lambda: note(j, r_run).astype(jnp.int32), lambda: jnp.int32(0))

            zero = jnp.zeros((1, SB_BLK), F32)
            z_next = scores(i - 1)
            _, wgt, r_run = _sb_weights(zt, [zero, zero], True)
            go = noted(i - 1, r_run)

            def body(c):
                j, _, acc, r_run, zt, j_prev, w_prev = c
                z_next = scores(j - 1)
                acc = acc + output(j_prev, w_prev)
                _, wgt, lk_sum = _sb_weights(zt, r_run, False)
                r_run = [r_run[a] + lk_sum[a] for a in range(2)]
                return j - 1, noted(j - 1, r_run), acc, r_run, z_next, j, wgt.astype(BF16)

            c = (i - 1, go, jnp.zeros((SB_PAIR, SB_BLK), F32), r_run, z_next, i, wgt.astype(BF16))
            _, _, acc, _, _, j_prev, w_prev = lax.while_loop(lambda c: c[1] > 0, body, c)
            rs_ref[0, i, pl.ds(i, 1), :] = jnp.full((1, SB_BLK), j_prev, jnp.int32).astype(F32)
            zt = diagonal(jnp.minimum(i + 1, nb - 1))
            o_ref[_sb_rows(i), :] = (acc + output(j_prev, w_prev)).T.astype(o_ref.dtype)
            return zt

        lax.fori_loop(0, nb, qblock, diagonal(0))

    width = SB_HEADS * SB_DH
    return _hosted_call(
        kern, grid=(n_pairs,), in_specs=_sb_specs(S),
        out_specs=[pl.BlockSpec((S, SB_PAIR), lambda p: (0, p)),
                   pl.BlockSpec((2, nb, nb, SB_BLK), lambda p: (p, 0, 0, 0))],
        out_shape=[jax.ShapeDtypeStruct((S, width), BF16), jax.ShapeDtypeStruct((SB_HEADS, nb, nb, SB_BLK), F32)],
        scratch_shapes=[pltpu.VMEM((S, SB_PAIR), BF16), pltpu.VMEM((2 * S, SB_PAIR), BF16),
                        pltpu.VMEM((SB_PAIR, 2 * S), BF16)],
        args=[proj, proj, proj, qg2, kg2], name=name, comm=comm, sem=("parallel",))


def _sb_bwd(proj, qg2, kg2, rs, do, *, name, comm=None):
    S = proj.shape[0]
    nb = S // SB_BLK
    scale = SB_DH ** -0.5
    n_pairs = SB_HEADS * SB_DH // SB_PAIR

    def kern(q_ref, k_ref, v_ref, qg_ref, kg_ref, rs_ref, do_ref, dq_ref, dk_ref, dv_ref, dqg_ref, dkg_ref,
             qp_ref, km_ref, kt_ref, vm_ref, dqn_ref, dkn_ref, dvs_ref):
        lane_lo = _iota2((SB_BLK, SB_PAIR), 1) < SB_DH
        heads = (lane_lo, jnp.logical_not(lane_lo))
        fn_q = lambda x, g: _sb_norm_pair(x, g, lane_lo) * scale
        fn_k = lambda x, g: _sb_norm_pair(x, g, lane_lo)

        def prologue(j, carry):
            rows = _sb_rows(j)
            qp_ref[rows, :] = fn_q(q_ref[rows, :], qg_ref[...]).astype(BF16)
            kn = fn_k(k_ref[rows, :], kg_ref[...])
            v = v_ref[rows, :]
            for a, mine in enumerate(heads):
                k_a = jnp.where(mine, kn, 0.0)
                km_ref[_sb_both(j, a), :] = k_a.astype(BF16)
                kt_ref[:, _sb_both(j, a)] = k_a.T.astype(BF16)
                vm_ref[_sb_both(j, a), :] = jnp.where(mine, v, 0.0).astype(BF16)
            return carry

        lax.fori_loop(0, nb, prologue, 0)
        dkn_ref[...] = jnp.zeros_like(dkn_ref)
        dvs_ref[...] = jnp.zeros_like(dvs_ref)

        def leftmost(i):
            return jnp.clip(jnp.max(rs_ref[0, i, pl.ds(i, 1), :]).astype(jnp.int32), 0, i)

        def opening_of(i, j):
            jc = jnp.minimum(j, i)
            return (_sb_scores(km_ref[_sb_both(jc), :], qp_ref[_sb_rows(i), :]),
                    lax.dot_general(vm_ref[_sb_both(jc), :], do_ref[_sb_rows(i), :], (((1,), (1,)), ((), ())),
                                    preferred_element_type=F32))

        def qblock(i, carry):
            first, zt, dp = carry
            qi = qp_ref[_sb_rows(i), :]
            doi = do_ref[_sb_rows(i), :]

            opening = functools.partial(opening_of, i)

            def closing(j, dzb, wgtb, dqa):
                dkn_ref[_sb_both(j), :] += jnp.dot(dzb, qi, preferred_element_type=F32)
                dvs_ref[_sb_both(j), :] += jnp.dot(wgtb, doi, preferred_element_type=F32)
                return dqa + jnp.dot(kt_ref[:, _sb_both(j)], dzb, preferred_element_type=F32)

            def middle(j, diag, zt, dp, e_run):
                zero = jnp.zeros((1, SB_BLK), F32)
                r_run = [zero, zero] if diag else [rs_ref[a, i, pl.ds(j, 1), :] for a in range(2)]
                sp, wgt, _ = _sb_weights(zt, r_run, diag)
                e = dp * wgt
                heads_e = [_running(e[a * SB_BLK:(a + 1) * SB_BLK], SB_BLK, False) for a in range(2)]
                e_left = jnp.concatenate([heads_e[a] + e_run[a] for a in range(2)], axis=0) - e
                s_neg = jnp.exp(-sp)
                dz = e * s_neg - e_left * (1.0 - s_neg)
                if diag:
                    dz = jnp.where((_iota2(dz.shape, 0) & (SB_BLK - 1)) < _iota2(dz.shape, 1), dz, 0.0)
                return dz.astype(BF16), wgt.astype(BF16), [e_run[a] + heads_e[a][SB_BLK - 1:SB_BLK, :] for a in range(2)]

            def body(j, c):
                dqa, e_run, zt, dp, j_prev, dzb, wgtb = c
                nxt = opening(j + 1)
                dqa = closing(j_prev, dzb, wgtb, dqa)
                dzb, wgtb, e_run = middle(j, False, zt, dp, e_run)
                return (dqa, e_run) + nxt + (j, dzb, wgtb)

            zero = jnp.zeros((1, SB_BLK), F32)
            none = jnp.zeros((2 * SB_BLK, SB_BLK), BF16)
            c = (jnp.zeros((SB_PAIR, SB_BLK), F32), [zero, zero], zt, dp, first, none, none)
            dqa, e_run, zt, dp, j_prev, dzb, wgtb = lax.fori_loop(first, i, body, c)
            dqa = closing(j_prev, dzb, wgtb, dqa)
            dzb, wgtb, _ = middle(i, True, zt, dp, e_run)
            i_next = jnp.minimum(i + 1, nb - 1)
            first_next = leftmost(i_next)
            nxt = opening_of(i_next, first_next)
            dqn_ref[_sb_rows(i), :] = closing(i, dzb, wgtb, dqa).T
            return (first_next,) + nxt

        lax.fori_loop(0, nb, qblock, (leftmost(0),) + opening_of(0, 0))
        dqg_ref[...] = jnp.zeros_like(dqg_ref)
        dkg_ref[...] = jnp.zeros_like(dkg_ref)

        def epilogue(j, carry):
            rows = _sb_rows(j)
            _, vjp_q = jax.vjp(fn_q, q_ref[rows, :], qg_ref[...])
            dq, dqg = vjp_q(dqn_ref[rows, :])
            _, vjp_k = jax.vjp(fn_k, k_ref[rows, :], kg_ref[...])
            dk, dkg = vjp_k(jnp.where(lane_lo, dkn_ref[_sb_both(j, 0), :], dkn_ref[_sb_both(j, 1), :]))
            dq_ref[rows, :] = dq.astype(BF16)
            dk_ref[rows, :] = dk.astype(BF16)
            dv_ref[rows, :] = jnp.where(lane_lo, dvs_ref[_sb_both(j, 0), :], dvs_ref[_sb_both(j, 1), :]).astype(BF16)
            dqg_ref[0] += dqg
            dkg_ref[0] += dkg
            return carry

        lax.fori_loop(0, nb, epilogue, 0)

    width = SB_HEADS * SB_DH
    pair = pl.BlockSpec((S, SB_PAIR), lambda p: (0, p))
    dgain = pl.BlockSpec((1, 1, SB_PAIR), lambda p: (p, 0, 0))
    in_specs = _sb_specs(S) + [pl.BlockSpec((2, nb, nb, SB_BLK), lambda p: (p, 0, 0, 0)), pair]
    return _hosted_call(
        kern, grid=(n_pairs,), in_specs=in_specs, out_specs=[pair, pair, pair, dgain, dgain],
        out_shape=[jax.ShapeDtypeStruct((S, width), BF16)] * 3 + [jax.ShapeDtypeStruct((n_pairs, 1, SB_PAIR), F32)] * 2,
        scratch_shapes=[pltpu.VMEM((S, SB_PAIR), BF16), pltpu.VMEM((2 * S, SB_PAIR), BF16), pltpu.VMEM((SB_PAIR, 2 * S), BF16),
                        pltpu.VMEM((2 * S, SB_PAIR), BF16), pltpu.VMEM((S, SB_PAIR), F32),
                        pltpu.VMEM((2 * S, SB_PAIR), F32), pltpu.VMEM((2 * S, SB_PAIR), F32)],
        args=[proj, proj, proj, qg2, kg2, rs, do], name=name, comm=comm, sem=("parallel",))


def _loss_head(y, target, *, name):
    S, D = y.shape
    T = min(ROW_TILE, S)

    def kern(y_ref, t_ref, dy_ref, acc_ref):
        err = y_ref[...] - t_ref[...]
        dy_ref[...] = err * (1.0 / D)
        col = jnp.sum(err * err, axis=0, keepdims=True)
        part = sum(col[:, k * 128:(k + 1) * 128] for k in range(D // 128))

        @pl.when(pl.program_id(0) == 0)
        def _():
            acc_ref[...] = part

        @pl.when(pl.program_id(0) > 0)
        def _():
            acc_ref[...] += part

    tile = pl.BlockSpec((T, D), lambda i: (i, 0))
    return pl.pallas_call(
        kern, grid=(S // T,), in_specs=[tile, tile], out_specs=[tile, pl.BlockSpec((1, 128), lambda i: (0, 0))],
        out_shape=[jax.ShapeDtypeStruct((S, D), F32), jax.ShapeDtypeStruct((1, 128), F32)],
        name=name, compiler_params=_cparams(("arbitrary",)))(y, target)


def _adamw_math(w, g, m, v):
    m = ADAM_B1 * m + (1.0 - ADAM_B1) * g
    v = ADAM_B2 * v + (1.0 - ADAM_B2) * jnp.square(g)
    m_hat = m / (1.0 - ADAM_B1 ** ADAM_STEP)
    v_hat = v / (1.0 - ADAM_B2 ** ADAM_STEP)
    return -ADAM_LR * (m_hat / (jnp.sqrt(v_hat) + ADAM_EPS) + ADAM_WD * w), m, v


def _adamw(w, g, m, v, *, name):
    R, C = w.shape
    T = _pick(R, (256, 128, 64, 32, 16, 8))

    def kern(w_ref, g_ref, m_ref, v_ref, d_ref, mo_ref, vo_ref):
        d, mn, vn = _adamw_math(w_ref[...], g_ref[...], m_ref[...], v_ref[...])
        d_ref[...] = d
        mo_ref[...] = mn
        vo_ref[...] = vn

    tile = pl.BlockSpec((T, C), lambda i: (i, 0))
    return pl.pallas_call(
        kern, grid=(R // T,), in_specs=[tile] * 4, out_specs=[tile] * 3,
        out_shape=[jax.ShapeDtypeStruct((R, C), F32)] * 3, name=name,
        compiler_params=_cparams(("parallel",)))(w, g, m, v)


def _adamw_layer(w, g, m, v, layer, prev, *, name, comm=None):
    L, R, C = w.shape
    T = _pick(R, (256, 128, 64, 32, 16, 8))

    def kern(w_ref, g_ref, m_ref, v_ref, *rest):
        go_ref, d_ref, mo_ref, vo_ref = rest[-4:]
        grad = g_ref[...]
        d, mn, vn = _adamw_math(w_ref[...], grad, m_ref[...], v_ref[...])
        go_ref[...] = grad
        d_ref[...] = d
        mo_ref[...] = mn
        vo_ref[...] = vn

    layer_tile = pl.BlockSpec((None, T, C), lambda i: (layer, i, 0))
    in_specs = [layer_tile, pl.BlockSpec((T, C), lambda i: (i, 0)), layer_tile, layer_tile]
    args, aliases = [w, g, m, v], {}
    if prev is not None:
        in_specs += [pl.BlockSpec(memory_space=pl.ANY)] * 4
        args += list(prev)
        aliases = {4 + k: k for k in range(4)}
    if comm is not None:
        assert prev is None
        return _hosted_call(kern, grid=(R // T,), in_specs=in_specs, out_specs=[layer_tile] * 4,
                            out_shape=[jax.ShapeDtypeStruct((L, R, C), F32)] * 4, scratch_shapes=[], args=args,
                            name=name, comm=comm)
    return pl.pallas_call(
        kern, grid=(R // T,), in_specs=in_specs, out_specs=[layer_tile] * 4,
        out_shape=[jax.ShapeDtypeStruct((L, R, C), F32)] * 4, input_output_aliases=aliases, name=name,
        compiler_params=_cparams(("parallel",)))(*args)


def _sum8(g, *, name):
    def kern(g_ref, o_ref):
        acc = g_ref[0]
        for d in range(1, g.shape[0]):
            acc = acc + g_ref[d]
        o_ref[...] = acc

    return pl.pallas_call(kern, out_shape=jax.ShapeDtypeStruct(g.shape[1:], F32), name=name,
                          compiler_params=_cparams())(g)


def _place():
    return lax.axis_index("x"), lax.axis_index("y"), lax.axis_index("c")


def _other_chips(x, y):
    return [(1 - x, y), (x, 1 - y), (1 - x, 1 - y)]


def _remote(src, dst, send_sems, recv_sems, k, to):
    return pltpu.make_async_remote_copy(src_ref=src, dst_ref=dst, send_sem=send_sems.at[k], recv_sem=recv_sems.at[k],
                                        device_id=to, device_id_type=MESH)


def _all_gather_small(v, *, name):
    def body(x_ref, out_ref, send_sems, recv_sems, local_sem):
        x, y, c = _place()
        me = 4 * x + 2 * y + c
        mine = pltpu.make_async_copy(x_ref, out_ref.at[me], local_sem)
        mine.start()
        peers = []
        for f in range(1, 8):
            peers.append((1 - x if f & 4 else x, 1 - y if f & 2 else y, 1 - c if f & 1 else c))
        sends = [_remote(x_ref, out_ref.at[me], send_sems, recv_sems, k, p) for k, p in enumerate(peers)]
        for cp in sends:
            cp.start()
        for k, (px, py, pc) in enumerate(peers):
            _remote(x_ref, out_ref.at[4 * px + 2 * py + pc], send_sems, recv_sems, k, (px, py, pc)).wait_recv()
        for cp in sends:
            cp.wait_send()
        mine.wait()

    return pl.pallas_call(
        body, out_shape=jax.ShapeDtypeStruct((8,) + v.shape, v.dtype),
        in_specs=[pl.BlockSpec(memory_space=pltpu.VMEM)], out_specs=pl.BlockSpec(memory_space=pltpu.VMEM),
        scratch_shapes=[pltpu.SemaphoreType.DMA((7,)), pltpu.SemaphoreType.DMA((7,)), pltpu.SemaphoreType.DMA],
        name=name, compiler_params=_cparams())(v)


def _hosted_call(kern, *, grid, in_specs, out_specs, out_shape, scratch_shapes, args, name, comm=None, sem=None):
    if comm is None:
        res = pl.pallas_call(kern, grid=grid, in_specs=in_specs, out_specs=out_specs, out_shape=out_shape,
                             scratch_shapes=scratch_shapes, name=name, compiler_params=_cparams(sem))(*args)
        return list(res), []
    n_in, n_out, n_scr = len(in_specs), len(out_specs), len(scratch_shapes)
    c_in, c_out = len(comm.inputs), len(comm.out_shapes)

    def body(*refs):
        ins, ci = refs[:n_in], refs[n_in:n_in + c_in]
        outs = refs[n_in + c_in:n_in + c_in + n_out]
        co = refs[n_in + c_in + n_out:n_in + c_in + n_out + c_out]
        scr = refs[n_in + c_in + n_out + c_out:n_in + c_in + n_out + c_out + n_scr]
        cs = refs[n_in + c_in + n_out + c_out + n_scr:]
        ids = [pl.program_id(d) for d in range(len(grid))]
        inner_first = functools.reduce(jnp.logical_and, [i == 0 for i in ids[1:]], True)
        inner_last = functools.reduce(jnp.logical_and, [i == n - 1 for i, n in zip(ids[1:], grid[1:])], True)

        @pl.when(jnp.logical_and(ids[0] == 0, inner_first))
        def _():
            comm.begin(ci, co, cs)

        kern(*ins, *outs, *scr)

        @pl.when(jnp.logical_and(ids[0] == grid[0] // 2, inner_last))
        def _():
            comm.middle(ci, co, cs)

        @pl.when(jnp.logical_and(ids[0] == grid[0] - 1, inner_last))
        def _():
            comm.end(ci, co, cs)

    hbm = pl.BlockSpec(memory_space=pltpu.HBM)
    res = pl.pallas_call(
        body, grid=grid, in_specs=list(in_specs) + [hbm] * c_in, out_specs=list(out_specs) + [hbm] * c_out,
        out_shape=list(out_shape) + list(comm.out_shapes), scratch_shapes=list(scratch_shapes) + list(comm.scratch),
        input_output_aliases={n_in + i: n_out + o for i, o in comm.aliases.items()},
        name=name, compiler_params=_cparams(("arbitrary",) * len(grid)))(*args, *comm.inputs)
    return list(res[:n_out]), list(res[n_out:])


def _run_comm(comm, *, name):
    return _hosted_call(lambda: None, grid=(1,), in_specs=[], out_specs=[], out_shape=[], scratch_shapes=[], args=[],
                        name=name, comm=comm)[1]


class _Gather:
    def __init__(self, shards, kinds, items):
        used = sorted({w for w, _ in items})
        self.slot = {w: k for k, w in enumerate(used)}
        self.inputs = [shards[w] for w in used]
        self.items, self.kinds = list(items), kinds
        self.shapes = {w: shards[w].shape[1:] for w in used}
        self.out_shapes = [jax.ShapeDtypeStruct((r, 4 * n) if kinds[w] == "col" else (4 * r, n), shards[w].dtype)
                           for w, _ in items for r, n in [self.shapes[w]]]
        n_items = len(items)
        self.scratch = [pltpu.SemaphoreType.DMA((6 * n_items,)), pltpu.SemaphoreType.DMA((6 * n_items,)),
                        pltpu.SemaphoreType.DMA((n_items,))]
        self.aliases = {}

    def _piece(self, ref, w, qq, half):
        r, n = self.shapes[w]
        h = r // 2
        lo, size = (0, r) if half is None else (half * h, h)
        if self.kinds[w] == "col":
            return ref.at[pl.ds(pl.multiple_of(lo, 16), size), pl.ds(pl.multiple_of(qq * n, 128), n)]
        return ref.at[pl.ds(pl.multiple_of(qq * r + lo, 16), size), :]

    def _mine(self, ci, w, l, half):
        h = self.shapes[w][0] // 2
        return ci[self.slot[w]].at[l, pl.ds(pl.multiple_of(half * h, 16), h), :]

    def begin(self, ci, co, cs):
        send_sems, recv_sems, local_sems = cs
        x, y, c = _place()
        q = 2 * x + y
        for k, (w, l) in enumerate(self.items):
            pltpu.make_async_copy(ci[self.slot[w]].at[l], self._piece(co[k], w, q, None), local_sems.at[k]).start()
            for j, (cx, cy) in enumerate(_other_chips(x, y)):
                _remote(self._mine(ci, w, l, c), self._piece(co[k], w, q, c), send_sems, recv_sems, 6 * k + j,
                        (cx, cy, c)).start()

    def middle(self, ci, co, cs):
        send_sems, recv_sems, _ = cs
        x, y, c = _place()
        for k, (w, l) in enumerate(self.items):
            for j, (cx, cy) in enumerate(_other_chips(x, y)):
                win = self._piece(co[k], w, 2 * cx + cy, c)
                _remote(win, win, send_sems, recv_sems, 6 * k + j, (cx, cy, c)).wait_recv()
                _remote(win, win, send_sems, recv_sems, 6 * k + 3 + j, (x, y, 1 - c)).start()

    def end(self, ci, co, cs):
        send_sems, recv_sems, local_sems = cs
        x, y, c = _place()
        q = 2 * x + y
        for k, (w, l) in enumerate(self.items):
            for j, (cx, cy) in enumerate(_other_chips(x, y)):
                win = self._piece(co[k], w, 2 * cx + cy, 1 - c)
                _remote(win, win, send_sems, recv_sems, 6 * k + 3 + j, (x, y, 1 - c)).wait_recv()
        for k, (w, l) in enumerate(self.items):
            for j, (cx, cy) in enumerate(_other_chips(x, y)):
                _remote(self._mine(ci, w, l, c), self._piece(co[k], w, q, c), send_sems, recv_sems, 6 * k + j,
                        (cx, cy, c)).wait_send()
                win = self._piece(co[k], w, 2 * cx + cy, c)
                _remote(win, win, send_sems, recv_sems, 6 * k + 3 + j, (x, y, 1 - c)).wait_send()
            pltpu.make_async_copy(ci[self.slot[w]].at[l], self._piece(co[k], w, q, None), local_sems.at[k]).wait()


def _half_rows(ref, half, h):
    return ref.at[:, pl.ds(pl.multiple_of(half * h, 16), h), :]


class _Copies:
    def __init__(self, inputs, out_shapes, count, pairs, aliases=None, lands=None):
        self.inputs, self.out_shapes, self.pairs, self.lands = list(inputs), list(out_shapes), pairs, lands
        self.scratch = [pltpu.SemaphoreType.DMA((count,)), pltpu.SemaphoreType.DMA((count,))]
        self.aliases = aliases or {}

    def _copies(self, ci, co, cs):
        x, y, c = _place()
        return [_remote(src, dst, cs[0], cs[1], k, to) for k, (src, dst, to) in enumerate(self.pairs(ci, co, x, y, c))]

    def begin(self, ci, co, cs):
        for cp in self._copies(ci, co, cs):
            cp.start()

    def middle(self, ci, co, cs):
        pass

    def end(self, ci, co, cs):
        x, y, c = _place()
        for k, (src, dst, to) in enumerate(self.pairs(ci, co, x, y, c)):
            _remote(src, dst, cs[0], cs[1], k, to).wait_send()
            arrival = dst if self.lands is None else self.lands(co, x, y, c)[k]
            _remote(src, arrival, cs[0], cs[1], k, to).wait_recv()


def _send_to_all(v):
    def peers(x, y, c):
        return [(1 - x if f & 4 else x, 1 - y if f & 2 else y, 1 - c if f & 1 else c) for f in range(1, 8)]

    def pairs(ci, co, x, y, c):
        return [(ci[0], co[0].at[4 * x + 2 * y + c], peer) for peer in peers(x, y, c)]

    def lands(co, x, y, c):
        return [co[0].at[4 * px + 2 * py + pc] for px, py, pc in peers(x, y, c)]

    return _Copies([v], [jax.ShapeDtypeStruct((8,) + v.shape, v.dtype)], 7, pairs, lands=lands)


def _swap_halves(gs):
    def pairs(ci, co, x, y, c):
        return [(_half_rows(ci[k], 1 - c, g.shape[1] // 2), co[k], (x, y, 1 - c)) for k, g in enumerate(gs)]

    return _Copies(gs, [jax.ShapeDtypeStruct((g.shape[0], g.shape[1] // 2, g.shape[2]), g.dtype) for g in gs],
                   len(gs), pairs)


def _scatter_quarters(ps, kinds):
    part = [((p.shape[1], p.shape[2] // 4) if kind == "col" else (p.shape[1], p.shape[2])) for p, kind in zip(ps, kinds)]

    def pairs(ci, co, x, y, c):
        out = []
        for k, kind in enumerate(kinds):
            n = part[k][1]
            for j, (cx, cy) in enumerate(_other_chips(x, y)):
                qj = 2 * cx + cy
                src = ci[k].at[0, :, pl.ds(pl.multiple_of(qj * n, 128), n)] if kind == "col" else ci[k].at[qj]
                out.append((src, co[k].at[j], (cx, cy, c)))
        return out

    return _Copies(ps, [jax.ShapeDtypeStruct((3,) + pt, p.dtype) for pt, p in zip(part, ps)], 3 * len(ps), pairs)


def _share_halves(gs):
    def rows(co, k, half):
        h = gs[k].shape[0] // 2
        return co[k].at[pl.ds(pl.multiple_of(half * h, 16), h), :]

    def pairs(ci, co, x, y, c):
        return [(rows(co, k, c), rows(co, k, c), (x, y, 1 - c)) for k in range(len(gs))]

    def lands(co, x, y, c):
        return [rows(co, k, 1 - c) for k in range(len(gs))]

    return _Copies(gs, [jax.ShapeDtypeStruct(g.shape, g.dtype) for g in gs], len(gs), pairs,
                   aliases={k: k for k in range(len(gs))}, lands=lands)


def _wide_tile(n):
    return _pick(n, (2048, 1920, 1024, 512, 256, 128))


def _pair_sum(g, land, place, *, name):
    B, R, N = g.shape
    h = R // 2
    tr, tc = _pick(h, (256, 128)), _wide_tile(N)

    def kern(place_ref, g_ref, l_ref, o_ref):
        o_ref[...] = (g_ref[...] + l_ref[...]).astype(o_ref.dtype)

    grid_spec = pltpu.PrefetchScalarGridSpec(
        num_scalar_prefetch=1, grid=(B, h // tr, N // tc),
        in_specs=[pl.BlockSpec((None, tr, tc), lambda b, i, j, p: (b, p[1] * (h // tr) + i, j)),
                  pl.BlockSpec((None, tr, tc), lambda b, i, j, p: (b, i, j))],
        out_specs=pl.BlockSpec((None, tr, tc), lambda b, i, j, p: (b, i, j)))
    return pl.pallas_call(kern, grid_spec=grid_spec, out_shape=jax.ShapeDtypeStruct((B, h, N), BF16), name=name,
                          compiler_params=_cparams(("parallel", "parallel", "parallel")))(place, g, land)


def _quarter_sum(p, land, kind, shard_shape, place, *, name):
    L, r, n = shard_shape
    h = r // 2
    tr, tc = _pick(h, (256, 128)), _wide_tile(n)

    def kern(place_ref, p_ref, a_ref, b_ref, c_ref, o_ref):
        o_ref[...] = ((p_ref[...].astype(F32) + a_ref[...].astype(F32)) + b_ref[...].astype(F32)) + c_ref[...].astype(F32)

    if kind == "col":
        p_spec = pl.BlockSpec((None, tr, tc), lambda l, i, j, pr: (l, i, pr[0] * (n // tc) + j))
    else:
        p_spec = pl.BlockSpec((None, None, tr, tc), lambda l, i, j, pr: (l, pr[0], i, j))
    lands = [pl.BlockSpec((None, None, tr, tc), functools.partial(lambda l, i, j, pr, s: (s, l, i, j), s=s))
             for s in range(3)]
    grid_spec = pltpu.PrefetchScalarGridSpec(
        num_scalar_prefetch=1, grid=(L, h // tr, n // tc), in_specs=[p_spec] + lands,
        out_specs=pl.BlockSpec((None, tr, tc), lambda l, i, j, pr: (l, pr[1] * (h // tr) + i, j)))
    return pl.pallas_call(kern, grid_spec=grid_spec, out_shape=jax.ShapeDtypeStruct((L, r, n), F32), name=name,
                          compiler_params=_cparams(("parallel", "parallel", "parallel")))(place, p, land, land, land)


class _ReduceScatter:
    def __init__(self, grads, kinds, shard_shapes, place, tag):
        self.kinds, self.shapes, self.place, self.tag = kinds, shard_shapes, place, tag
        self.g3 = [g[None] if kind == "col" else g.reshape(4, g.shape[0] // 4, g.shape[1]) for g, kind in zip(grads, kinds)]

    def swap(self):
        return _swap_halves(self.g3)

    def pair_sums(self, lands):
        self.ps = [_pair_sum(g, land, self.place, name=f"rs_pair_sum_{self.tag}_{k}")
                   for k, (g, land) in enumerate(zip(self.g3, lands))]

    def scatter(self):
        return _scatter_quarters(self.ps, self.kinds)

    def quarter_sums(self, parts):
        self.halves = []
        for k, (p, part) in enumerate(zip(self.ps, parts)):
            p4 = p if self.kinds[k] == "col" else p[None]
            out = _quarter_sum(p4, part[:, None], self.kinds[k], (1,) + tuple(self.shapes[k]), self.place,
                               name=f"rs_quarter_sum_{self.tag}_{k}")
            self.halves.append(out[0])

    def share(self):
        return _share_halves(self.halves)

    def run(self):
        self.pair_sums(_run_comm(self.swap(), name=f"rs_swap_{self.tag}"))
        self.quarter_sums(_run_comm(self.scatter(), name=f"rs_scatter_{self.tag}"))
        return _run_comm(self.share(), name=f"rs_share_{self.tag}")


_WEIGHTS = ["mod_w", "mod_b", "norm1_g", "w_in", "gate_b", "conv_w", "conv_b", "conv_ln_g", "conv_ln_b", "w_conv_proj",
            "hgrn_lb", "hgrn_norm_g", "w_hgrn_proj", "sb_qn_g", "sb_kn_g", "w_sb_proj", "w_out", "norm2_g", "mlp_w1",
            "mlp_w2"]
_BIG = [("w_in", "col"), ("w_conv_proj", "col"), ("w_hgrn_proj", "col"), ("w_sb_proj", "col"), ("w_out", "row"),
        ("mlp_w1", "col"), ("mlp_w2", "row")]
_REPLICATED = ["mod_b", "norm1_g", "gate_b", "conv_b", "conv_ln_g", "conv_ln_b", "hgrn_lb", "hgrn_norm_g", "sb_qn_g",
               "sb_kn_g", "norm2_g"]
LANES = 128


class _Pack:
    def __init__(self, items):
        self.shapes = {n: a.shape for n, a in items}
        self.offsets, pos = {}, 0
        for n, a in items:
            self.offsets[n] = pos
            pos += math.prod(a.shape)
        self.rows = -(-pos // (8 * LANES)) * 8
        flat = jnp.concatenate([a.reshape(-1).astype(F32) for _, a in items])
        self.array = jnp.pad(flat, (0, self.rows * LANES - pos)).reshape(self.rows, LANES)

    def get(self, packed, name):
        lead = packed.shape[:-2]
        flat = packed.reshape(lead + (self.rows * LANES,))
        n = math.prod(self.shapes[name])
        return lax.slice_in_dim(flat, self.offsets[name], self.offsets[name] + n, axis=len(lead)).reshape(
            lead + self.shapes[name])


def _lower_bounds(hgrn_lb):
    p = jax.nn.softmax(hgrn_lb.astype(F32), axis=0)
    return jnp.cumsum(p, axis=0) - p[0:1]


def _layer_fwd(x, w, p, l, comms=(None, None)):
    S, D = x.shape
    r = {"x": x}
    (r["h"],) = _rowop(_fn_normmod, [(x, 0, D)], [p["n1g"], p["sc1"], p["sh1"]], [(D, BF16)], name=f"normmod1_fwd_{l}")
    proj = r["proj"] = _matmul(r["h"], w["w_in", l], name=f"w_in_fwd_{l}")
    r["cpre"] = _conv_fwd(proj, p["w32"], p["conv_b"], name=f"conv_fwd_{l}")
    (r["cact"],) = _rowop(_fn_lnsilu, [(r["cpre"], 0, CONV_CH)], [p["lng"], p["lnb"]], [(CONV_CH, BF16)],
                          name=f"conv_ln_fwd_{l}")
    arrived = lambda comm, got: w.update({(_BIG[k][0], layer): arr for (k, layer), arr in zip(comm.items, got)})
    (r["hg"], r["states"]), got = _hgrn_fwd(proj, p["lbk"], p["ng"], name=f"hgrn_fwd_{l}", comm=comms[0])
    if comms[0] is not None:
        arrived(comms[0], got)
    (r["sb"], r["rs"]), got = _sb_fwd(proj, p["qg"], p["kg"], name=f"sb_fwd_{l}", comm=comms[1])
    if comms[1] is not None:
        arrived(comms[1], got)
    r["y_c"] = _matmul(r["cact"], w["w_conv_proj", l], name=f"w_conv_proj_fwd_{l}")
    r["y_h"] = _matmul(r["hg"], w["w_hgrn_proj", l], name=f"w_hgrn_proj_fwd_{l}")
    r["y_s"] = _matmul(r["sb"], w["w_sb_proj", l], name=f"w_sb_proj_fwd_{l}")
    (r["merged"],) = _rowop(_fn_merge, [(proj, OFF_GL, 3 * D), (r["y_c"], 0, D), (r["y_h"], 0, D), (r["y_s"], 0, D)],
                            [p["gate_b"]], [(D, BF16)], name=f"merge_fwd_{l}")
    resid = lambda y, x_in, gate: (y,) + _fn_resid(x_in, y, gate)
    r["a_out"], r["x1"] = _matmul(r["merged"], w["w_out", l], name=f"w_out_fwd_{l}", post=resid, extras=[x],
                                  rows=[p["g1"]], out_dtypes=(F32, F32))
    (r["h2"],) = _rowop(_fn_normmod, [(r["x1"], 0, D)], [p["n2g"], p["sc2"], p["sh2"]], [(D, BF16)],
                        name=f"normmod2_fwd_{l}")
    r["u"], r["act"] = _matmul(r["h2"], w["mlp_w1", l], name=f"mlp_w1_fwd_{l}", post=lambda u: (u,) + _fn_relu2(u),
                               out_dtypes=(F32, BF16))
    r["m_out"], x2 = _matmul(r["act"], w["mlp_w2", l], name=f"mlp_w2_fwd_{l}", post=resid, extras=[r["x1"]],
                             rows=[p["g2"]], out_dtypes=(F32, F32))
    return x2, r


def _layer_bwd(dx2, r, w, p, l, grads, carry=None, last=None):
    S, D = dx2.shape
    small = {}

    def dweight(name, a, dy):
        grads[name, l] = _matmul(a, dy, ta=True, name=f"{name}_dw_{l}")

    stage = (lambda k, got: carry(k, got)) if carry is not None else (lambda k, got: None)

    (dm_out,), (dg2,) = _rowop_bwd(_fn_scale, [(r["m_out"], 0, D)], [p["g2"]], [dx2], [BF16], name=f"resid2_bwd_{l}")
    (du,) = _matmul(dm_out, w["mlp_w2", l], tb=True, name=f"mlp_w2_dx_{l}", extras=[r["u"]], out_dtypes=(BF16,),
                    post=lambda dact, u: (dact * (2.0 * jnp.maximum(u, 0.0)),))
    dweight("mlp_w2", r["act"], dm_out)
    dh2 = _matmul(du, w["mlp_w1", l], tb=True, name=f"mlp_w1_dx_{l}")
    dweight("mlp_w1", r["h2"], du)
    (dx1, da_out), (dg1, small["norm2_g"], dsc2, dsh2) = _rowop_bwd(
        _fn_resid_norm, [(r["x"], 0, D), (r["a_out"], 0, D)], [p["g1"], p["n2g"], p["sc2"], p["sh2"]], [dx2, dh2],
        [F32, BF16], name=f"resid1_norm2_bwd_{l}")
    dmerged = _matmul(da_out, w["w_out", l], tb=True, name=f"w_out_dx_{l}")
    dweight("w_out", r["merged"], da_out)
    (dgl, dy_c, dy_h, dy_s), (small["gate_b"],) = _rowop_bwd(
        _fn_merge, [(r["proj"], OFF_GL, 3 * D), (r["y_c"], 0, D), (r["y_h"], 0, D), (r["y_s"], 0, D)], [p["gate_b"]],
        [dmerged], [BF16] * 4, name=f"merge_bwd_{l}")
    dweight("w_conv_proj", r["cact"], dy_c)
    dweight("w_hgrn_proj", r["hg"], dy_h)
    dweight("w_sb_proj", r["sb"], dy_s)
    dcact = _matmul(dy_c, w["w_conv_proj", l], tb=True, name=f"w_conv_proj_dx_{l}")
    (dcpre,), (small["conv_ln_g"], small["conv_ln_b"]) = _rowop_bwd(
        _fn_lnsilu, [(r["cpre"], 0, CONV_CH)], [p["lng"], p["lnb"]], [dcact], [F32], name=f"conv_ln_bwd_{l}")
    (d_conv, dw32, small["conv_b"]), got = _conv_bwd(r["proj"], dcpre, p["w32"], name=f"conv_bwd_{l}",
                                                      comm=stage(0, None))
    small["conv_w"] = dw32[:CONV_WIDTH]
    dhg = _matmul(dy_h, w["w_hgrn_proj", l], tb=True, out_dtype=BF16, name=f"w_hgrn_proj_dx_{l}")
    (dq, df, di, dg, dlbk, dng), got = _hgrn_bwd(r["proj"], r["states"], dhg, p["lbk"], p["ng"], name=f"hgrn_bwd_{l}",
                                                 comm=stage(1, got))
    small["lower"] = -dlbk
    small["hgrn_norm_g"] = jnp.sum(dng, axis=0)
    dsb = _matmul(dy_s, w["w_sb_proj", l], tb=True, out_dtype=BF16, name=f"w_sb_proj_dx_{l}")
    (dsq, dsk, dsv, dqg, dkg), got = _sb_bwd(r["proj"], p["qg"], p["kg"], r["rs"], dsb, name=f"sb_bwd_{l}",
                                             comm=stage(2, got))
    stage(3, got)
    fold = lambda t: jnp.sum(t.reshape(-1, SB_DH), axis=0, keepdims=True)
    small["sb_qn_g"], small["sb_kn_g"] = fold(dqg), fold(dkg)
    dproj = jnp.concatenate([d_conv, dq, df, di, dg, dsq, dsk, dsv, dgl], axis=1)
    dweight("w_in", r["h"], dproj)
    norm1 = functools.partial(_rowop_bwd, _fn_normmod, [(r["x"], 0, D)], [p["n1g"], p["sc1"], p["sh1"]],
                              din_dtypes=[F32], add={0: dx1}, name=f"normmod1_bwd_{l}")
    if last is None:
        dh = _matmul(dproj, w["w_in", l], tb=True, name=f"w_in_dx_{l}")
        (dx,), (small["norm1_g"], dsc1, dsh1) = norm1(douts=[dh])
    else:
        dh, got = _matmul(dproj, w["w_in", l], tb=True, name=f"w_in_dx_{l}", comm=last(0, None))
        (dx,), (small["norm1_g"], dsc1, dsh1), got = norm1(douts=[dh], comm=last(1, got))
        last(2, got)
    small["mod"] = jnp.concatenate([dsh1, dsc1, dg1, dsh2, dsc2, dg2], axis=1)
    return dx, small


def kernel(x, c, mod_w, mod_b, norm1_g, w_in, gate_b, conv_w, conv_b, conv_ln_g, conv_ln_b, w_conv_proj, hgrn_lb, hgrn_norm_g, w_hgrn_proj, sb_qn_g, sb_kn_g, w_sb_proj, w_out, norm2_g, mlp_w1, mlp_w2, loss_target, m_mod_w, m_mod_b, m_norm1_g, m_w_in, m_gate_b, m_conv_w, m_conv_b, m_conv_ln_g, m_conv_ln_b, m_w_conv_proj, m_hgrn_lb, m_hgrn_norm_g, m_w_hgrn_proj, m_sb_qn_g, m_sb_kn_g, m_w_sb_proj, m_w_out, m_norm2_g, m_mlp_w1, m_mlp_w2, v_mod_w, v_mod_b, v_norm1_g, v_w_in, v_gate_b, v_conv_w, v_conv_b, v_conv_ln_g, v_conv_ln_b, v_w_conv_proj, v_hgrn_lb, v_hgrn_norm_g, v_w_hgrn_proj, v_sb_qn_g, v_sb_kn_g, v_w_sb_proj, v_w_out, v_norm2_g, v_mlp_w1, v_mlp_w2):
    given = dict(locals())
    wts = {n: given[n] for n in _WEIGHTS}
    mom = {n: given["m_" + n] for n in _WEIGHTS}
    var = {n: given["v_" + n] for n in _WEIGHTS}
    n_layers, D = norm1_g.shape
    xi, yi, ci = _place()
    q = 2 * xi + yi
    me = 4 * xi + 2 * yi + ci
    place = jnp.stack([q, ci]).astype(jnp.int32)
    n_mod = mod_w.shape[2]
    cw = conv_w.shape[2]

    pk1 = _Pack([("c", c), ("conv_w", conv_w)])
    got1 = _all_gather_small(pk1.array, name="gather_cond")
    c_act = jax.nn.silu(pk1.get(got1, "c")[:, 0, :])
    conv_full = jnp.concatenate([pk1.get(got1, "conv_w")[2 * k] for k in range(4)], axis=-1)

    mod_cols = []
    for l in range(n_layers):
        mb = lax.dynamic_slice_in_dim(mod_b[l], q * n_mod, n_mod)
        mod_cols.append(_matmul(c_act, mod_w, bl=l, name=f"mod_fwd_{l}") + mb[None, :])
    got2 = _all_gather_small(jnp.concatenate(mod_cols, axis=0), name="gather_mod")
    mods = []
    for l in range(n_layers):
        row = lax.dynamic_index_in_dim(got2[0::2], l * 8 + me, axis=1, keepdims=False)
        mods.append(jnp.split(row.reshape(1, 4 * n_mod), 6, axis=1))

    lower, lower_vjp = jax.vjp(_lower_bounds, hgrn_lb)

    shards = [wts[n].astype(BF16) for n, _ in _BIG]
    kinds = [k for _, k in _BIG]
    index = {n: k for k, (n, _) in enumerate(_BIG)}
    first = ["w_in", "w_conv_proj", "w_hgrn_proj", "w_sb_proj"]

    def gather(*names_layers):
        items = [(index[n], l) for names, l in names_layers for n in names if l < n_layers]
        return _Gather(shards, kinds, items) if items else None

    start = gather((first[:1], 0))
    w = {(_BIG[k][0], layer): arr
         for (k, layer), arr in zip(start.items, _run_comm(start, name="gather_first_weights"))}

    def layer_params(l):
        sh1, sc1, g1, sh2, sc2, g2 = mods[l]
        return dict(sh1=sh1, sc1=sc1, g1=g1, sh2=sh2, sc2=sc2, g2=g2, n1g=norm1_g[l][None], n2g=norm2_g[l][None],
                    gate_b=gate_b[l][None], conv_b=conv_b[l][None], lng=conv_ln_g[l][None], lnb=conv_ln_b[l][None],
                    w32=jnp.pad(conv_full[l], ((0, CONV_HALO - CONV_WIDTH), (0, 0))), lbk=(1.0 - lower[l])[None],
                    ng=hgrn_norm_g[l][None], qg=jnp.tile(sb_qn_g[l][None], (1, SB_PAIR // SB_DH)),
                    kg=jnp.tile(sb_kn_g[l][None], (1, SB_PAIR // SB_DH)))

    params = [layer_params(l) for l in range(n_layers)]
    act, saved = x[0], []
    for l in range(n_layers):
        comms = (gather((first[1:] if l == 0 else [], l), (["w_out", "mlp_w1"], l)),
                 gather((["mlp_w2"], l), (first, l + 1)))
        act, r = _layer_fwd(act, w, params[l], l, comms=comms)
        saved.append(r)
    dact, loss_lanes = _loss_head(act, loss_target[0], name="loss_head")

    grads, smalls, reduced = {}, [None] * n_layers, {}

    def reduce_scatter(items, tag):
        return _ReduceScatter([grads[_BIG[k][0], layer] for k, layer in items], [kinds[k] for k, _ in items],
                              [shards[k].shape[1:] for k, _ in items], place, tag)

    def carried(l):
        items = [(k, l + 1) for k in range(len(_BIG))] + [(k, l) for k, (n, _) in enumerate(_BIG) if n != "w_in"]
        box = {}

        def carry(stage, got):
            if stage == 0:
                box["rs"] = reduce_scatter(items, f"l{l}")
                return box["rs"].swap()
            if stage == 1:
                box["rs"].pair_sums(got)
                return box["rs"].scatter()
            if stage == 2:
                box["rs"].quarter_sums(got)
                return box["rs"].share()
            reduced.update(zip(items, got))

        return carry

    def final(l):
        items = [(index["w_in"], l)]
        box = {}

        def step(stage, got):
            if stage == 0:
                box["rs"] = reduce_scatter(items, "w_in")
                box["rs"].pair_sums(_run_comm(box["rs"].swap(), name="rs_swap_w_in"))
                return box["rs"].scatter()
            if stage == 1:
                box["rs"].quarter_sums(got)
                return box["rs"].share()
            reduced.update(zip(items, got))

        return step

    for l in reversed(range(n_layers)):
        dact, smalls[l] = _layer_bwd(dact, saved[l], w, params[l], l, grads, carried(l) if l + 1 < n_layers else None,
                                     final(l) if l == 0 else None)
    grad_x = dact[None]
    rest = [(k, l) for l in range(n_layers) for k in range(len(_BIG)) if (k, l) not in reduced]
    if rest:
        reduced.update(zip(rest, reduce_scatter(rest, "rest").run()))

    stack = lambda k: jnp.stack([smalls[l][k] for l in range(n_layers)])
    (d_hgrn_lb,) = lower_vjp(stack("lower")[:, 0, :])
    items = [("loss", loss_lanes), ("mod", stack("mod")), ("hgrn_lb", d_hgrn_lb), ("conv_w", stack("conv_w"))]
    items += [(k, stack(k)) for k in ("norm1_g", "gate_b", "conv_b", "conv_ln_g", "conv_ln_b", "hgrn_norm_g", "sb_qn_g",
                                      "sb_kn_g", "norm2_g")]
    pk3 = _Pack(items)

    share_small = _send_to_all(pk3.array)
    delta, new_m, new_v, big, got3 = {}, {}, {}, {}, None
    for n, _ in _BIG:
        outs = None
        for l in reversed(range(n_layers)):
            args = (wts[n], reduced[index[n], l], mom[n], var[n], l, outs)
            if got3 is None:
                outs, (got3,) = _adamw_layer(*args, name=f"adamw_{n}_{l}", comm=share_small)
            else:
                outs = _adamw_layer(*args, name=f"adamw_{n}_{l}")
        big[n] = outs
    got3 = lax.dynamic_update_slice_in_dim(got3, pk3.array[None], me, axis=0)
    tot3 = _sum8(got3, name="sum_small_grads")
    loss = (0.5 / D) * jnp.sum(pk3.get(tot3, "loss"))
    g = {k: pk3.get(tot3, k).reshape(wts[k].shape) for k in _REPLICATED if k != "mod_b"}
    g["mod_b"] = pk3.get(tot3, "mod")[:, 0, :]
    g["conv_w"] = lax.dynamic_slice_in_dim(pk3.get(tot3, "conv_w"), q * cw, cw, axis=2)
    dmod_all = pk3.get(got3, "mod")[:, :, 0, :]
    g_mod_w = None
    for l in range(n_layers):
        cols = lax.dynamic_slice_in_dim(dmod_all[:, l, :], q * n_mod, n_mod, axis=1)
        g_mod_w = _matmul(c_act, cols, ta=True, layer=l, n_layers=n_layers, into=g_mod_w, name=f"mod_dw_{l}")
    g["mod_w"] = g_mod_w

    for n, _ in _BIG:
        g[n], delta[n], new_m[n], new_v[n] = big[n]
    two_d = lambda t: t.reshape(-1, t.shape[-1])
    outs = _adamw(two_d(mod_w), two_d(g["mod_w"]), two_d(m_mod_w), two_d(v_mod_w), name="adamw_mod_w")
    delta["mod_w"], new_m["mod_w"], new_v["mod_w"] = (t.reshape(mod_w.shape) for t in outs)
    rest = _REPLICATED + ["conv_w"]
    packs = [_Pack([(n, src[n]) for n in rest]) for src in (wts, g, mom, var)]
    outs = _adamw(*[pk.array for pk in packs], name="adamw_small")
    for n in rest:
        delta[n], new_m[n], new_v[n] = (packs[0].get(t, n) for t in outs)

    return (loss, grad_x, *[g[n] for n in _WEIGHTS], *[delta[n] for n in _WEIGHTS], *[new_m[n] for n in _WEIGHTS],
            *[new_v[n] for n in _WEIGHTS])
```

```python
import functools
import math

import jax
import jax.numpy as jnp
from jax import lax
from jax.experimental import pallas as pl
from jax.experimental.pallas import tpu as pltpu

F32 = jnp.float32
BF16 = jnp.bfloat16
MESH = pl.DeviceIdType.MESH

EPS = 1e-6
CONV_CH = 512
CONV_WIDTH = 31
CONV_HALO = 32
HG_HEADS = 4
HG_D = 128
HG_CHUNK = 64
HG_SUB = 32
SB_HEADS = 8
SB_DH = 64
SB_BLK = 128
SB_PAIR = 128
SB_SKIP = -104.0
OFF_CONV, OFF_HG, OFF_SB, OFF_GL = 0, 1024, 3072, 4608
ADAM_LR, ADAM_B1, ADAM_B2, ADAM_EPS, ADAM_WD, ADAM_STEP = 0.001, 0.9, 0.999, 1e-08, 0.01, 10
VMEM_LIMIT_BYTES = 56 * 1024 * 1024
ROW_TILE = 256


def _cparams(sem=None, **kw):
    return pltpu.CompilerParams(dimension_semantics=sem, vmem_limit_bytes=VMEM_LIMIT_BYTES, **kw)


def _pick(n, cands):
    for c in cands:
        if n % c == 0:
            return c
    return n


MATMUL_VMEM_BUDGET = 40 * 1024 * 1024


def _tile_options(n, cap):
    opts = [t for t in range(cap - cap % 128, 0, -128) if n % t == 0]
    return opts or [n]


def _matmul_tiles(M, N, K, size_a, size_b, size_o, in_acc):
    for tm in _tile_options(M, 1024):
        for tk in _tile_options(K, 2048):
            for tn in _tile_options(N, 1280):
                need = 2 * (tm * tk * size_a + tk * tn * size_b + tm * tn * size_o)
                if K > tk and not in_acc:
                    need += tm * tn * 4
                if need <= MATMUL_VMEM_BUDGET:
                    return tm, tn, tk
    raise ValueError(f"no matmul tiling fits VMEM for {(M, N, K)}")
def _matmul(a, b, *, ta=False, tb=False, bl=None, out_dtype=F32, name, into=None, layer=None, n_layers=None,
            post=None, extras=(), rows=(), out_dtypes=None, comm=None):
    M, K = (a.shape[1], a.shape[0]) if ta else a.shape
    N = b.shape[-2] if tb else b.shape[-1]
    if post is not None:
        return _matmul_post(a, b, M, N, K, ta, tb, post, extras, rows, out_dtypes, name)
    assert comm is None or layer is None
    in_acc = jnp.dtype(out_dtype) == jnp.dtype(F32)
    tm, tn, tk = _matmul_tiles(M, N, K, a.dtype.itemsize, b.dtype.itemsize, jnp.dtype(out_dtype).itemsize, in_acc)
    nk = K // tk
    a_spec = pl.BlockSpec((tk, tm), lambda i, j, k: (k, i)) if ta else pl.BlockSpec((tm, tk), lambda i, j, k: (i, k))
    if bl is None:
        b_spec = pl.BlockSpec((tn, tk), lambda i, j, k: (j, k)) if tb else pl.BlockSpec((tk, tn), lambda i, j, k: (k, j))
    elif tb:
        b_spec = pl.BlockSpec((None, tn, tk), lambda i, j, k: (bl, j, k))
    else:
        b_spec = pl.BlockSpec((None, tk, tn), lambda i, j, k: (bl, k, j))
    dn = (((0 if ta else 1,), (1 if tb else 0,)), ((), ()))

    use_scratch = nk > 1 and not in_acc

    def kern(a_ref, b_ref, *rest):
        o_ref = rest[-2] if use_scratch else rest[-1]
        prod = lambda: lax.dot_general(a_ref[...].astype(BF16), b_ref[...].astype(BF16), dn,
                                       preferred_element_type=F32)
        if nk == 1:
            o_ref[...] = prod().astype(o_ref.dtype).reshape(o_ref.shape)
            return
        acc_ref = rest[-1] if use_scratch else o_ref
        k = pl.program_id(2)

        @pl.when(k == 0)
        def _():
            acc_ref[...] = prod().reshape(acc_ref.shape)

        @pl.when(k > 0)
        def _():
            acc_ref[...] += prod().reshape(acc_ref.shape)

        if use_scratch:
            @pl.when(k == nk - 1)
            def _():
                o_ref[...] = acc_ref[...].astype(o_ref.dtype).reshape(o_ref.shape)

    in_specs, args, aliases = [a_spec, b_spec], [a, b], {}
    if layer is None:
        out_shape = jax.ShapeDtypeStruct((M, N), out_dtype)
        out_spec = pl.BlockSpec((tm, tn), lambda i, j, k: (i, j))
    else:
        out_shape = jax.ShapeDtypeStruct((n_layers, M, N), out_dtype)
        out_spec = pl.BlockSpec((1, tm, tn), lambda i, j, k: (layer, i, j))
        if into is not None:
            in_specs.append(pl.BlockSpec(memory_space=pl.ANY))
            args.append(into)
            aliases = {2: 0}
    if comm is not None:
        (out,), got = _hosted_call(kern, grid=(M // tm, N // tn, nk), in_specs=in_specs, out_specs=[out_spec],
                                   out_shape=[out_shape], scratch_shapes=[pltpu.VMEM((tm, tn), F32)] if use_scratch else [],
                                   args=args, name=name, comm=comm)
        return out, got
    return pl.pallas_call(
        kern, grid=(M // tm, N // tn, nk), in_specs=in_specs, out_specs=out_spec, out_shape=out_shape,
        scratch_shapes=[pltpu.VMEM((tm, tn), F32)] if use_scratch else [],
        input_output_aliases=aliases, name=name,
        compiler_params=_cparams(("parallel", "parallel", "arbitrary")))(*args)


def _matmul_post(a, b, M, N, K, ta, tb, post, extras, rows, out_dtypes, name):
    per_elem = sum(e.dtype.itemsize for e in extras) + sum(jnp.dtype(d).itemsize for d in out_dtypes)
    fits = lambda tm, tn: 2 * (tm * K * a.dtype.itemsize + K * tn * b.dtype.itemsize + tm * tn * per_elem) <= MATMUL_VMEM_BUDGET
    tm, tn = next((tm, tn) for tm in _tile_options(M, 1024) for tn in _tile_options(N, 1280) if fits(tm, tn))
    a_spec = pl.BlockSpec((K, tm), lambda i, j: (0, i)) if ta else pl.BlockSpec((tm, K), lambda i, j: (i, 0))
    b_spec = pl.BlockSpec((tn, K), lambda i, j: (j, 0)) if tb else pl.BlockSpec((K, tn), lambda i, j: (0, j))
    tile = pl.BlockSpec((tm, tn), lambda i, j: (i, j))
    row = pl.BlockSpec((1, tn), lambda i, j: (0, j))
    dn = (((0 if ta else 1,), (1 if tb else 0,)), ((), ()))
    n_ex = len(extras) + len(rows)

    def kern(a_ref, b_ref, *rest):
        prod = lax.dot_general(a_ref[...].astype(BF16), b_ref[...].astype(BF16), dn, preferred_element_type=F32)
        res = post(prod, *[r[...].astype(F32) for r in rest[:n_ex]])
        for val, o_ref in zip(res, rest[n_ex:]):
            o_ref[...] = val.astype(o_ref.dtype)

    return pl.pallas_call(
        kern, grid=(M // tm, N // tn), in_specs=[a_spec, b_spec] + [tile] * len(extras) + [row] * len(rows),
        out_specs=[tile] * len(out_dtypes), out_shape=[jax.ShapeDtypeStruct((M, N), d) for d in out_dtypes], name=name,
        compiler_params=_cparams(("parallel", "parallel")))(a, b, *extras, *rows)


def _col_specs(off, width, T):
    bw = math.gcd(width, off) if off else width
    return [pl.BlockSpec((T, bw), functools.partial(lambda i, c: (i, c), c=off // bw + p)) for p in range(width // bw)]


def _gather_rows(refs, counts):
    vals, pos = [], 0
    for n in counts:
        parts = [refs[pos + p][...].astype(F32) for p in range(n)]
        pos += n
        vals.append(parts[0] if n == 1 else jnp.concatenate(parts, axis=1))
    return vals, pos


def _rowop(fn, ins, params, outs, *, name):
    S = ins[0][0].shape[0]
    T = min(ROW_TILE, S)
    in_specs, counts, args = [], [], []
    for arr, off, width in ins:
        sp = _col_specs(off, width, T)
        in_specs += sp
        counts.append(len(sp))
        args += [arr] * len(sp)
    in_specs += [pl.BlockSpec(p.shape, lambda i: (0, 0)) for p in params]

    def kern(*refs):
        vals, pos = _gather_rows(refs, counts)
        pv = [refs[pos + p][...] for p in range(len(params))]
        pos += len(params)
        res = fn(*vals, *pv)
        for r, o_ref in zip(res, refs[pos:]):
            o_ref[...] = r.astype(o_ref.dtype)

    return pl.pallas_call(
        kern, grid=(S // T,), in_specs=in_specs,
        out_specs=[pl.BlockSpec((T, w), lambda i: (i, 0)) for w, _ in outs],
        out_shape=[jax.ShapeDtypeStruct((S, w), dt) for w, dt in outs],
        name=name, compiler_params=_cparams(("parallel",)))(*args, *params)


def _rowop_bwd(fn, ins, params, douts, din_dtypes, *, name, add=None, comm=None):
    add = add or {}
    S = ins[0][0].shape[0]
    T = min(ROW_TILE, S)
    in_specs, counts, args = [], [], []
    for arr, off, width in ins:
        sp = _col_specs(off, width, T)
        in_specs += sp
        counts.append(len(sp))
        args += [arr] * len(sp)
    in_specs += [pl.BlockSpec(p.shape, lambda i: (0, 0)) for p in params]
    in_specs += [pl.BlockSpec((T, d.shape[1]), lambda i: (i, 0)) for d in douts]
    add_keys = sorted(add)
    in_specs += [pl.BlockSpec((T, add[k].shape[1]), lambda i: (i, 0)) for k in add_keys]
    want = [k for k, dt in enumerate(din_dtypes) if dt is not None]

    def kern(*refs):
        vals, pos = _gather_rows(refs, counts)
        pv = [refs[pos + p][...] for p in range(len(params))]
        pos += len(params)
        cts = [refs[pos + p][...].astype(F32) for p in range(len(douts))]
        pos += len(douts)
        adds = {k: refs[pos + p][...].astype(F32) for p, k in enumerate(add_keys)}
        pos += len(add_keys)
        _, vjp = jax.vjp(fn, *vals, *pv)
        grads = vjp(tuple(cts))
        for k in want:
            g = grads[k] + adds[k] if k in adds else grads[k]
            refs[pos][...] = g.astype(refs[pos].dtype)
            pos += 1
        first = pl.program_id(0) == 0
        for p in range(len(params)):
            gp, o_ref = grads[len(ins) + p], refs[pos + p]

            @pl.when(first)
            def _(gp=gp, o_ref=o_ref):
                o_ref[...] = gp

            @pl.when(jnp.logical_not(first))
            def _(gp=gp, o_ref=o_ref):
                o_ref[...] += gp

    out_specs = [pl.BlockSpec((T, ins[k][2]), lambda i: (i, 0)) for k in want]
    out_specs += [pl.BlockSpec(p.shape, lambda i: (0, 0)) for p in params]
    out_shape = [jax.ShapeDtypeStruct((S, ins[k][2]), din_dtypes[k]) for k in want]
    out_shape += [jax.ShapeDtypeStruct(p.shape, F32) for p in params]
    res, got = _hosted_call(
        kern, grid=(S // T,), in_specs=in_specs, out_specs=out_specs, out_shape=out_shape, scratch_shapes=[],
        args=[*args, *params, *douts, *[add[k] for k in add_keys]], name=name, comm=comm, sem=("arbitrary",))
    dins = [None] * len(ins)
    for p, k in enumerate(want):
        dins[k] = res[p]
    return (dins, res[len(want):]) if comm is None else (dins, res[len(want):], got)


def _rms(x, g):
    return x * lax.rsqrt(jnp.mean(x * x, axis=-1, keepdims=True) + EPS) * g


def _fn_normmod(x, g, sc, sh):
    return (_rms(x, g) * (1.0 + sc) + sh,)


def _fn_lnsilu(c, g, b):
    mu = jnp.mean(c, axis=-1, keepdims=True)
    var = jnp.mean(jnp.square(c - mu), axis=-1, keepdims=True)
    y = (c - mu) * lax.rsqrt(var + EPS) * g + b
    return (y * jax.nn.sigmoid(y),)


def _fn_merge(gl, yc, yh, ys, gb):
    d = yc.shape[1]
    g = jax.nn.sigmoid(gl + gb)
    return (g[:, :d] * yc + g[:, d:2 * d] * yh + g[:, 2 * d:] * ys,)


def _fn_resid(x, y, g):
    return (x + g * y,)


def _fn_resid_norm(x, y, g, n, sc, sh):
    x1 = x + g * y
    return (x1,) + _fn_normmod(x1, n, sc, sh)


def _fn_scale(y, g):
    return (g * y,)


def _fn_relu2(u):
    return (jnp.square(jnp.maximum(u, 0.0)),)


def _conv_specs(S, T):
    r = T // CONV_HALO
    cur = [pl.BlockSpec((T, CONV_CH), lambda i: (i, 0)), pl.BlockSpec((T, CONV_CH), lambda i: (i, 1))]
    prev = [pl.BlockSpec((CONV_HALO, CONV_CH), lambda i: (jnp.maximum(i * r - 1, 0), 0)),
            pl.BlockSpec((CONV_HALO, CONV_CH), lambda i: (jnp.maximum(i * r - 1, 0), 1))]
    return cur + prev


def _glu_ext(a_ref, g_ref, ah_ref, gh_ref):
    a = a_ref[...]
    sg = jax.nn.sigmoid(g_ref[...])
    uh = jnp.where(pl.program_id(0) > 0, ah_ref[...] * jax.nn.sigmoid(gh_ref[...]), 0.0)
    return a, sg, jnp.concatenate([uh, a * sg], axis=0)


def _shift_up(xe, k, T):
    return xe[:T] if k == 0 else pltpu.roll(xe, shift=xe.shape[0] - k, axis=0)[:T]


def _conv_fwd(proj, w32, b, *, name):
    S = proj.shape[0]
    T = min(ROW_TILE, S)
    lead = CONV_HALO - (CONV_WIDTH - 1)

    def kern(a_ref, g_ref, ah_ref, gh_ref, w_ref, b_ref, o_ref):
        _, _, ue = _glu_ext(a_ref, g_ref, ah_ref, gh_ref)
        acc = jnp.zeros((T, CONV_CH), F32) + b_ref[...]
        for j in range(CONV_WIDTH):
            acc = acc + w_ref[j:j + 1, :] * _shift_up(ue, lead + j, T)
        o_ref[...] = acc

    const = lambda shape: pl.BlockSpec(shape, lambda i: (0, 0))
    return pl.pallas_call(
        kern, grid=(S // T,), in_specs=_conv_specs(S, T) + [const(w32.shape), const(b.shape)],
        out_specs=pl.BlockSpec((T, CONV_CH), lambda i: (i, 0)),
        out_shape=jax.ShapeDtypeStruct((S, CONV_CH), F32), name=name,
        compiler_params=_cparams(("parallel",)))(proj, proj, proj, proj, w32, b)


def _conv_bwd(proj, dc, w32, *, name, comm=None):
    S = proj.shape[0]
    T = min(ROW_TILE, S)
    nt = S // T
    r = T // CONV_HALO
    lead = CONV_HALO - (CONV_WIDTH - 1)
    last_halo = S // CONV_HALO - 1

    def kern(a_ref, g_ref, ah_ref, gh_ref, dc_ref, dcn_ref, w_ref, dag_ref, dw_ref, db_ref):
        i = pl.program_id(0)
        a, sg, ue = _glu_ext(a_ref, g_ref, ah_ref, gh_ref)
        dc_t = dc_ref[...]
        de = jnp.concatenate([dc_t, jnp.where(i < nt - 1, dcn_ref[...], 0.0)], axis=0)

        @pl.when(i == 0)
        def _():
            dw_ref[...] = jnp.zeros_like(dw_ref)
            db_ref[...] = jnp.zeros_like(db_ref)

        du = jnp.zeros((T, CONV_CH), F32)
        for j in range(CONV_WIDTH):
            du = du + w_ref[j:j + 1, :] * _shift_up(de, CONV_WIDTH - 1 - j, T)
            dw_ref[j:j + 1, :] += jnp.sum(dc_t * _shift_up(ue, lead + j, T), axis=0, keepdims=True)
        db_ref[...] += jnp.sum(dc_t, axis=0, keepdims=True)
        dag_ref[:, :CONV_CH] = (du * sg).astype(BF16)
        dag_ref[:, CONV_CH:] = (du * a * sg * (1.0 - sg)).astype(BF16)

    const = lambda shape: pl.BlockSpec(shape, lambda i: (0, 0))
    in_specs = _conv_specs(S, T) + [
        pl.BlockSpec((T, CONV_CH), lambda i: (i, 0)),
        pl.BlockSpec((CONV_HALO, CONV_CH), lambda i: (jnp.minimum((i + 1) * r, last_halo), 0)),
        const(w32.shape)]
    return _hosted_call(
        kern, grid=(nt,), in_specs=in_specs,
        out_specs=[pl.BlockSpec((T, 2 * CONV_CH), lambda i: (i, 0)), const(w32.shape), const((1, CONV_CH))],
        out_shape=[jax.ShapeDtypeStruct((S, 2 * CONV_CH), BF16), jax.ShapeDtypeStruct(w32.shape, F32),
                   jax.ShapeDtypeStruct((1, CONV_CH), F32)],
        scratch_shapes=[], args=[proj, proj, proj, proj, dc, dc, w32], name=name, comm=comm, sem=("arbitrary",))


def _iota2(shape, dim):
    return lax.broadcasted_iota(jnp.int32, shape, dim)


def _running(x, seg, later):
    n = x.shape[0]
    pos = _iota2(x.shape, 0) & (seg - 1)
    k = 1
    while k < seg:
        if later:
            x = x + jnp.where(pos < seg - k, pltpu.roll(x, n - k, axis=0), 0.0)
        else:
            x = x + jnp.where(pos >= k, pltpu.roll(x, k, axis=0), 0.0)
        k *= 2
    return x


@functools.partial(jax.custom_vjp, nondiff_argnums=(1,))
def _prefix(x, seg):
    return _running(x, seg, False)


_prefix.defvjp(lambda x, seg: (_running(x, seg, False), None), lambda seg, _, g: (_running(g, seg, True),))


def _hg_chunk(q, f, iv, g, st, lbk, ng):
    n, sub = HG_CHUNK, HG_SUB
    kk = lbk * jax.nn.sigmoid(-f)
    lf = jnp.log(1.0 - kk)
    b = _prefix(lf, n)
    bs = _prefix(lf, sub)
    bt = jnp.sum(lf, axis=0, keepdims=True)
    qh = q * jax.nn.sigmoid(q)
    dot_nt = lambda x, y: lax.dot_general(x.astype(BF16), y.astype(BF16), (((1,), (1,)), ((), ())),
                                          preferred_element_type=F32)
    o = dot_nt(qh * jnp.exp(b), st)
    b0 = b - bs
    qs = qh * jnp.exp(bs)
    col = _iota2((sub, n), 1)
    rows = []
    for blk in range(n // sub):
        lo = blk * sub
        sl = slice(lo, lo + sub)
        acc = o[sl]
        if blk > 0:
            ref = jnp.concatenate([b0[sl]] * (n // sub), axis=0)
            kd = kk * jnp.exp(jnp.minimum(ref - b, 0.0))
            sc = jnp.where(col < lo, dot_nt(qs[sl], kd), 0.0)
            acc = acc + jnp.dot(sc.astype(BF16), iv.astype(BF16), preferred_element_type=F32)
        bq, bk = bs[sl][None, :, :], bs[sl][:, None, :]
        s_i = lax.broadcasted_iota(jnp.int32, (sub, sub, HG_D), 0)
        t_i = lax.broadcasted_iota(jnp.int32, (sub, sub, HG_D), 1)
        keep = s_i <= t_i
        p = jnp.where(keep, qh[sl][None, :, :] * kk[sl][:, None, :] * jnp.exp(jnp.where(keep, bq - bk, 0.0)), 0.0)
        w = jnp.sum(p, axis=-1, keepdims=True)
        acc = acc + jnp.sum(w * iv[sl][:, None, :], axis=0)
        rows.append(acc)
    o = jnp.concatenate(rows, axis=0)
    kd = kk * jnp.exp(bt - b)
    st_new = jnp.exp(bt) * st + lax.dot_general(iv.astype(BF16), kd.astype(BF16), (((0,), (0,)), ((), ())),
                                                     preferred_element_type=F32)
    out = _rms(o, ng) * (g * jax.nn.sigmoid(g))
    return out, st_new


def _hg_tile(S):
    return min(512, S)


def _hg_in_specs(rt, rev, nr):
    width = HG_HEADS * HG_D
    base = OFF_HG // width
    row = (lambda r: nr - 1 - r) if rev else (lambda r: r)
    return [pl.BlockSpec((rt, width), functools.partial(lambda r, k: (row(r), base + k), k=k)) for k in range(4)]


def _hg_cols(h):
    return slice(h * HG_D, (h + 1) * HG_D)


def _hgrn_fwd(proj, lbk, ng, *, name, comm=None):
    S = proj.shape[0]
    rt = _hg_tile(S)
    nr, nc = S // rt, rt // HG_CHUNK

    def kern(q_ref, f_ref, i_ref, g_ref, lbk_ref, ng_ref, o_ref, st_out_ref, st_ref):
        @pl.when(pl.program_id(0) == 0)
        def _():
            st_ref[...] = jnp.zeros_like(st_ref)

        def body(c, carry):
            rows = pl.ds(pl.multiple_of(c * HG_CHUNK, HG_CHUNK), HG_CHUNK)
            for h in range(HG_HEADS):
                cols = _hg_cols(h)
                st = st_ref[h]
                st_out_ref[h, c] = st
                out, st_new = _hg_chunk(q_ref[rows, cols], f_ref[rows, cols], i_ref[rows, cols], g_ref[rows, cols], st,
                                        lbk_ref[:, cols], ng_ref[...])
                o_ref[rows, cols] = out.astype(o_ref.dtype)
                st_ref[h] = st_new
            return carry

        lax.fori_loop(0, nc, body, 0)

    width = HG_HEADS * HG_D
    in_specs = _hg_in_specs(rt, False, nr) + [pl.BlockSpec((1, width), lambda r: (0, 0)),
                                               pl.BlockSpec((1, HG_D), lambda r: (0, 0))]
    return _hosted_call(
        kern, grid=(nr,), in_specs=in_specs,
        out_specs=[pl.BlockSpec((rt, width), lambda r: (r, 0)),
                   pl.BlockSpec((HG_HEADS, nc, HG_D, HG_D), lambda r: (0, r, 0, 0))],
        out_shape=[jax.ShapeDtypeStruct((S, width), BF16),
                   jax.ShapeDtypeStruct((HG_HEADS, S // HG_CHUNK, HG_D, HG_D), F32)],
        scratch_shapes=[pltpu.VMEM((HG_HEADS, HG_D, HG_D), F32)],
        args=[proj, proj, proj, proj, lbk, ng], name=name, comm=comm, sem=("arbitrary",))


def _hgrn_bwd(proj, states, dout, lbk, ng, *, name, comm=None):
    S = proj.shape[0]
    rt = _hg_tile(S)
    nr, nc = S // rt, rt // HG_CHUNK
    width = HG_HEADS * HG_D

    def kern(q_ref, f_ref, i_ref, g_ref, st_in_ref, do_ref, lbk_ref, ng_ref,
             dq_ref, df_ref, di_ref, dg_ref, dlbk_ref, dng_ref, dst_ref):
        @pl.when(pl.program_id(0) == 0)
        def _():
            dst_ref[...] = jnp.zeros_like(dst_ref)
            dlbk_ref[...] = jnp.zeros_like(dlbk_ref)
            dng_ref[...] = jnp.zeros_like(dng_ref)

        def body(k, carry):
            c = nc - 1 - k
            rows = pl.ds(pl.multiple_of(c * HG_CHUNK, HG_CHUNK), HG_CHUNK)
            for h in range(HG_HEADS):
                cols = _hg_cols(h)
                _, vjp = jax.vjp(_hg_chunk, q_ref[rows, cols], f_ref[rows, cols], i_ref[rows, cols], g_ref[rows, cols],
                                 st_in_ref[h, c], lbk_ref[:, cols], ng_ref[...])
                dq, df, di, dg, dst, dlbk, dng = vjp((do_ref[rows, cols].astype(F32), dst_ref[h]))
                dq_ref[rows, cols] = dq.astype(BF16)
                df_ref[rows, cols] = df.astype(BF16)
                di_ref[rows, cols] = di.astype(BF16)
                dg_ref[rows, cols] = dg.astype(BF16)
                dst_ref[h] = dst
                dlbk_ref[:, cols] += dlbk
                dng_ref[h] += dng
            return carry

        lax.fori_loop(0, nc, body, 0)

    rev = lambda r: nr - 1 - r
    tile = pl.BlockSpec((rt, width), lambda r: (rev(r), 0))
    in_specs = _hg_in_specs(rt, True, nr) + [
        pl.BlockSpec((HG_HEADS, nc, HG_D, HG_D), lambda r: (0, rev(r), 0, 0)), tile,
        pl.BlockSpec((1, width), lambda r: (0, 0)), pl.BlockSpec((1, HG_D), lambda r: (0, 0))]
    return _hosted_call(
        kern, grid=(nr,), in_specs=in_specs,
        out_specs=[tile, tile, tile, tile, pl.BlockSpec((1, width), lambda r: (0, 0)),
                   pl.BlockSpec((HG_HEADS, 1, HG_D), lambda r: (0, 0, 0))],
        out_shape=[jax.ShapeDtypeStruct((S, width), BF16)] * 4 + [
            jax.ShapeDtypeStruct((1, width), F32), jax.ShapeDtypeStruct((HG_HEADS, 1, HG_D), F32)],
        scratch_shapes=[pltpu.VMEM((HG_HEADS, HG_D, HG_D), F32)],
        args=[proj, proj, proj, proj, states, dout, lbk, ng], name=name, comm=comm, sem=("arbitrary",))


def _sb_scores(km, qi):
    return lax.dot_general(km, qi, (((1,), (1,)), ((), ())), preferred_element_type=F32)


def _sb_weights(zt, r_run, diag):
    n = SB_BLK
    sp = jnp.maximum(zt, 0.0) + jnp.log(1.0 + jnp.exp(-jnp.abs(zt)))
    lk = -sp
    if diag:
        keep = (_iota2(zt.shape, 0) & (n - 1)) < _iota2(zt.shape, 1)
        lk = jnp.where(keep, lk, 0.0)
    tails = [_running(lk[a * n:(a + 1) * n], n, True) for a in range(2)]
    between = jnp.concatenate([tails[a] + r_run[a] for a in range(2)], axis=0)
    wgt = jnp.exp(zt + between)
    if diag:
        wgt = jnp.where(keep, wgt, 0.0)
    return sp, wgt, [t[0:1, :] for t in tails]


def _sb_norm_pair(x, g2, lane_lo):
    sq = x * x
    ms_lo = jnp.sum(jnp.where(lane_lo, sq, 0.0), axis=-1, keepdims=True)
    ms_hi = jnp.sum(jnp.where(lane_lo, 0.0, sq), axis=-1, keepdims=True)
    return x * lax.rsqrt(jnp.where(lane_lo, ms_lo, ms_hi) * (1.0 / SB_DH) + EPS) * g2


def _sb_specs(S):
    base = OFF_SB // SB_PAIR
    per = SB_HEADS * SB_DH // SB_PAIR
    cols = [pl.BlockSpec((S, SB_PAIR), functools.partial(lambda p, k: (0, base + per * k + p), k=k)) for k in range(3)]
    return cols + [pl.BlockSpec((1, SB_PAIR), lambda p: (0, 0))] * 2


def _sb_rows(i):
    return pl.ds(pl.multiple_of(i * SB_BLK, SB_BLK), SB_BLK)


def _sb_both(j, a=None):
    if a is None:
        return pl.ds(pl.multiple_of(j * 2 * SB_BLK, 2 * SB_BLK), 2 * SB_BLK)
    return pl.ds(pl.multiple_of(j * 2 * SB_BLK + a * SB_BLK, SB_BLK), SB_BLK)


def _sb_fwd(proj, qg2, kg2, *, name, comm=None):
    S = proj.shape[0]
    nb = S // SB_BLK
    scale = SB_DH ** -0.5
    n_pairs = SB_HEADS * SB_DH // SB_PAIR

    def kern(q_ref, k_ref, v_ref, qg_ref, kg_ref, o_ref, rs_ref, qp_ref, km_ref, vt_ref):
        lane_lo = _iota2((SB_BLK, SB_PAIR), 1) < SB_DH

        def prologue(j, carry):
            rows = _sb_rows(j)
            qp_ref[rows, :] = (_sb_norm_pair(q_ref[rows, :], qg_ref[...], lane_lo) * scale).astype(BF16)
            kn = _sb_norm_pair(k_ref[rows, :], kg_ref[...], lane_lo)
            v = v_ref[rows, :]
            for a, mine in enumerate((lane_lo, jnp.logical_not(lane_lo))):
                km_ref[_sb_both(j, a), :] = jnp.where(mine, kn, 0.0).astype(BF16)
                vt_ref[:, _sb_both(j, a)] = jnp.where(mine, v, 0.0).T.astype(BF16)
            return carry

        lax.fori_loop(0, nb, prologue, 0)

        diagonal = lambda i: _sb_scores(km_ref[_sb_both(i), :], qp_ref[_sb_rows(i), :])

        def qblock(i, zt):
            qi = qp_ref[_sb_rows(i), :]

            scores = lambda j: _sb_scores(km_ref[_sb_both(jnp.maximum(j, 0)), :], qi)
            output = lambda j, wgt: jnp.dot(vt_ref[:, _sb_both(j)], wgt, preferred_element_type=F32)

            def note(j, r_run):
                for a in range(2):
                    rs_ref[a, i, pl.ds(j, 1), :] = r_run[a]
                return jnp.maximum(jnp.max(r_run[0]), jnp.max(r_run[1])) > SB_SKIP

            def noted(j, r_run):
                return lax.cond(j >= 0, lambda: note(j, r_run).astype(jnp.int32), lambda: jnp.int32(0))

            zero = jnp.zeros((1, SB_BLK), F32)
            z_next = scores(i - 1)
            _, wgt, r_run = _sb_weights(zt, [zero, zero], True)
            go = noted(i - 1, r_run)

            def body(c):
                j, _, acc, r_run, zt, j_prev, w_prev = c
                z_next = scores(j - 1)
                acc = acc + output(j_prev, w_prev)
                _, wgt, lk_sum = _sb_weights(zt, r_run, False)
                r_run = [r_run[a] + lk_sum[a] for a in range(2)]
                return j - 1, noted(j - 1, r_run), acc, r_run, z_next, j, wgt.astype(BF16)

            c = (i - 1, go, jnp.zeros((SB_PAIR, SB_BLK), F32), r_run, z_next, i, wgt.astype(BF16))
            _, _, acc, _, _, j_prev, w_prev = lax.while_loop(lambda c: c[1] > 0, body, c)
            rs_ref[0, i, pl.ds(i, 1), :] = jnp.full((1, SB_BLK), j_prev, jnp.int32).astype(F32)
            zt = diagonal(jnp.minimum(i + 1, nb - 1))
            o_ref[_sb_rows(i), :] = (acc + output(j_prev, w_prev)).T.astype(o_ref.dtype)
            return zt

        lax.fori_loop(0, nb, qblock, diagonal(0))

    width = SB_HEADS * SB_DH
    return _hosted_call(
        kern, grid=(n_pairs,), in_specs=_sb_specs(S),
        out_specs=[pl.BlockSpec((S, SB_PAIR), lambda p: (0, p)),
                   pl.BlockSpec((2, nb, nb, SB_BLK), lambda p: (p, 0, 0, 0))],
        out_shape=[jax.ShapeDtypeStruct((S, width), BF16), jax.ShapeDtypeStruct((SB_HEADS, nb, nb, SB_BLK), F32)],
        scratch_shapes=[pltpu.VMEM((S, SB_PAIR), BF16), pltpu.VMEM((2 * S, SB_PAIR), BF16),
                        pltpu.VMEM((SB_PAIR, 2 * S), BF16)],
        args=[proj, proj, proj, qg2, kg2], name=name, comm=comm, sem=("parallel",))


def _sb_bwd(proj, qg2, kg2, rs, do, *, name, comm=None):
    S = proj.shape[0]
    nb = S // SB_BLK
    scale = SB_DH ** -0.5
    n_pairs = SB_HEADS * SB_DH // SB_PAIR

    def kern(q_ref, k_ref, v_ref, qg_ref, kg_ref, rs_ref, do_ref, dq_ref, dk_ref, dv_ref, dqg_ref, dkg_ref,
             qp_ref, km_ref, kt_ref, vm_ref, dqn_ref, dkn_ref, dvs_ref):
        lane_lo = _iota2((SB_BLK, SB_PAIR), 1) < SB_DH
        heads = (lane_lo, jnp.logical_not(lane_lo))
        fn_q = lambda x, g: _sb_norm_pair(x, g, lane_lo) * scale
        fn_k = lambda x, g: _sb_norm_pair(x, g, lane_lo)

        def prologue(j, carry):
            rows = _sb_rows(j)
            qp_ref[rows, :] = fn_q(q_ref[rows, :], qg_ref[...]).astype(BF16)
            kn = fn_k(k_ref[rows, :], kg_ref[...])
            v = v_ref[rows, :]
            for a, mine in enumerate(heads):
                k_a = jnp.where(mine, kn, 0.0)
                km_ref[_sb_both(j, a), :] = k_a.astype(BF16)
                kt_ref[:, _sb_both(j, a)] = k_a.T.astype(BF16)
                vm_ref[_sb_both(j, a), :] = jnp.where(mine, v, 0.0).astype(BF16)
            return carry

        lax.fori_loop(0, nb, prologue, 0)
        dkn_ref[...] = jnp.zeros_like(dkn_ref)
        dvs_ref[...] = jnp.zeros_like(dvs_ref)

        def leftmost(i):
            return jnp.clip(jnp.max(rs_ref[0, i, pl.ds(i, 1), :]).astype(jnp.int32), 0, i)

        def opening_of(i, j):
            jc = jnp.minimum(j, i)
            return (_sb_scores(km_ref[_sb_both(jc), :], qp_ref[_sb_rows(i), :]),
                    lax.dot_general(vm_ref[_sb_both(jc), :], do_ref[_sb_rows(i), :], (((1,), (1,)), ((), ())),
                                    preferred_element_type=F32))

        def qblock(i, carry):
            first, zt, dp = carry
            qi = qp_ref[_sb_rows(i), :]
            doi = do_ref[_sb_rows(i), :]

            opening = functools.partial(opening_of, i)

            def closing(j, dzb, wgtb, dqa):
                dkn_ref[_sb_both(j), :] += jnp.dot(dzb, qi, preferred_element_type=F32)
                dvs_ref[_sb_both(j), :] += jnp.dot(wgtb, doi, preferred_element_type=F32)
                return dqa + jnp.dot(kt_ref[:, _sb_both(j)], dzb, preferred_element_type=F32)

            def middle(j, diag, zt, dp, e_run):
                zero = jnp.zeros((1, SB_BLK), F32)
                r_run = [zero, zero] if diag else [rs_ref[a, i, pl.ds(j, 1), :] for a in range(2)]
                sp, wgt, _ = _sb_weights(zt, r_run, diag)
                e = dp * wgt
                heads_e = [_running(e[a * SB_BLK:(a + 1) * SB_BLK], SB_BLK, False) for a in range(2)]
                e_left = jnp.concatenate([heads_e[a] + e_run[a] for a in range(2)], axis=0) - e
                s_neg = jnp.exp(-sp)
                dz = e * s_neg - e_left * (1.0 - s_neg)
                if diag:
                    dz = jnp.where((_iota2(dz.shape, 0) & (SB_BLK - 1)) < _iota2(dz.shape, 1), dz, 0.0)
                return dz.astype(BF16), wgt.astype(BF16), [e_run[a] + heads_e[a][SB_BLK - 1:SB_BLK, :] for a in range(2)]

            def body(j, c):
                dqa, e_run, zt, dp, j_prev, dzb, wgtb = c
                nxt = opening(j + 1)
                dqa = closing(j_prev, dzb, wgtb, dqa)
                dzb, wgtb, e_run = middle(j, False, zt, dp, e_run)
                return (dqa, e_run) + nxt + (j, dzb, wgtb)

            zero = jnp.zeros((1, SB_BLK), F32)
            none = jnp.zeros((2 * SB_BLK, SB_BLK), BF16)
            c = (jnp.zeros((SB_PAIR, SB_BLK), F32), [zero, zero], zt, dp, first, none, none)
            dqa, e_run, zt, dp, j_prev, dzb, wgtb = lax.fori_loop(first, i, body, c)
            dqa = closing(j_prev, dzb, wgtb, dqa)
            dzb, wgtb, _ = middle(i, True, zt, dp, e_run)
            i_next = jnp.minimum(i + 1, nb - 1)
            first_next = leftmost(i_next)
            nxt = opening_of(i_next, first_next)
            dqn_ref[_sb_rows(i), :] = closing(i, dzb, wgtb, dqa).T
            return (first_next,) + nxt

        lax.fori_loop(0, nb, qblock, (leftmost(0),) + opening_of(0, 0))
        dqg_ref[...] = jnp.zeros_like(dqg_ref)
        dkg_ref[...] = jnp.zeros_like(dkg_ref)

        def epilogue(j, carry):
            rows = _sb_rows(j)
            _, vjp_q = jax.vjp(fn_q, q_ref[rows, :], qg_ref[...])
            dq, dqg = vjp_q(dqn_ref[rows, :])
            _, vjp_k = jax.vjp(fn_k, k_ref[rows, :], kg_ref[...])
            dk, dkg = vjp_k(jnp.where(lane_lo, dkn_ref[_sb_both(j, 0), :], dkn_ref[_sb_both(j, 1), :]))
            dq_ref[rows, :] = dq.astype(BF16)
            dk_ref[rows, :] = dk.astype(BF16)
            dv_ref[rows, :] = jnp.where(lane_lo, dvs_ref[_sb_both(j, 0), :], dvs_ref[_sb_both(j, 1), :]).astype(BF16)
            dqg_ref[0] += dqg
            dkg_ref[0] += dkg
            return carry

        lax.fori_loop(0, nb, epilogue, 0)

    width = SB_HEADS * SB_DH
    pair = pl.BlockSpec((S, SB_PAIR), lambda p: (0, p))
    dgain = pl.BlockSpec((1, 1, SB_PAIR), lambda p: (p, 0, 0))
    in_specs = _sb_specs(S) + [pl.BlockSpec((2, nb, nb, SB_BLK), lambda p: (p, 0, 0, 0)), pair]
    return _hosted_call(
        kern, grid=(n_pairs,), in_specs=in_specs, out_specs=[pair, pair, pair, dgain, dgain],
        out_shape=[jax.ShapeDtypeStruct((S, width), BF16)] * 3 + [jax.ShapeDtypeStruct((n_pairs, 1, SB_PAIR), F32)] * 2,
        scratch_shapes=[pltpu.VMEM((S, SB_PAIR), BF16), pltpu.VMEM((2 * S, SB_PAIR), BF16), pltpu.VMEM((SB_PAIR, 2 * S), BF16),
                        pltpu.VMEM((2 * S, SB_PAIR), BF16), pltpu.VMEM((S, SB_PAIR), F32),
                        pltpu.VMEM((2 * S, SB_PAIR), F32), pltpu.VMEM((2 * S, SB_PAIR), F32)],
        args=[proj, proj, proj, qg2, kg2, rs, do], name=name, comm=comm, sem=("parallel",))


def _loss_head(y, target, *, name):
    S, D = y.shape
    T = min(ROW_TILE, S)

    def kern(y_ref, t_ref, dy_ref, acc_ref):
        err = y_ref[...] - t_ref[...]
        dy_ref[...] = err * (1.0 / D)
        col = jnp.sum(err * err, axis=0, keepdims=True)
        part = sum(col[:, k * 128:(k + 1) * 128] for k in range(D // 128))

        @pl.when(pl.program_id(0) == 0)
        def _():
            acc_ref[...] = part

        @pl.when(pl.program_id(0) > 0)
        def _():
            acc_ref[...] += part

    tile = pl.BlockSpec((T, D), lambda i: (i, 0))
    return pl.pallas_call(
        kern, grid=(S // T,), in_specs=[tile, tile], out_specs=[tile, pl.BlockSpec((1, 128), lambda i: (0, 0))],
        out_shape=[jax.ShapeDtypeStruct((S, D), F32), jax.ShapeDtypeStruct((1, 128), F32)],
        name=name, compiler_params=_cparams(("arbitrary",)))(y, target)


def _adamw_math(w, g, m, v):
    m = ADAM_B1 * m + (1.0 - ADAM_B1) * g
    v = ADAM_B2 * v + (1.0 - ADAM_B2) * jnp.square(g)
    m_hat = m / (1.0 - ADAM_B1 ** ADAM_STEP)
    v_hat = v / (1.0 - ADAM_B2 ** ADAM_STEP)
    return -ADAM_LR * (m_hat / (jnp.sqrt(v_hat) + ADAM_EPS) + ADAM_WD * w), m, v


def _adamw(w, g, m, v, *, name):
    R, C = w.shape
    T = _pick(R, (256, 128, 64, 32, 16, 8))

    def kern(w_ref, g_ref, m_ref, v_ref, d_ref, mo_ref, vo_ref):
        d, mn, vn = _adamw_math(w_ref[...], g_ref[...], m_ref[...], v_ref[...])
        d_ref[...] = d
        mo_ref[...] = mn
        vo_ref[...] = vn

    tile = pl.BlockSpec((T, C), lambda i: (i, 0))
    return pl.pallas_call(
        kern, grid=(R // T,), in_specs=[tile] * 4, out_specs=[tile] * 3,
        out_shape=[jax.ShapeDtypeStruct((R, C), F32)] * 3, name=name,
        compiler_params=_cparams(("parallel",)))(w, g, m, v)


def _adamw_layer(w, g, m, v, layer, prev, *, name, comm=None):
    L, R, C = w.shape
    T = _pick(R, (256, 128, 64, 32, 16, 8))

    def kern(w_ref, g_ref, m_ref, v_ref, *rest):
        go_ref, d_ref, mo_ref, vo_ref = rest[-4:]
        grad = g_ref[...]
        d, mn, vn = _adamw_math(w_ref[...], grad, m_ref[...], v_ref[...])
        go_ref[...] = grad
        d_ref[...] = d
        mo_ref[...] = mn
        vo_ref[...] = vn

    layer_tile = pl.BlockSpec((None, T, C), lambda i: (layer, i, 0))
    in_specs = [layer_tile, pl.BlockSpec((T, C), lambda i: (i, 0)), layer_tile, layer_tile]
    args, aliases = [w, g, m, v], {}
    if prev is not None:
        in_specs += [pl.BlockSpec(memory_space=pl.ANY)] * 4
        args += list(prev)
        aliases = {4 + k: k for k in range(4)}
    if comm is not None:
        assert prev is None
        return _hosted_call(kern, grid=(R // T,), in_specs=in_specs, out_specs=[layer_tile] * 4,
                            out_shape=[jax.ShapeDtypeStruct((L, R, C), F32)] * 4, scratch_shapes=[], args=args,
                            name=name, comm=comm)
    return pl.pallas_call(
        kern, grid=(R // T,), in_specs=in_specs, out_specs=[layer_tile] * 4,
        out_shape=[jax.ShapeDtypeStruct((L, R, C), F32)] * 4, input_output_aliases=aliases, name=name,
        compiler_params=_cparams(("parallel",)))(*args)


def _sum8(g, *, name):
    def kern(g_ref, o_ref):
        acc = g_ref[0]
        for d in range(1, g.shape[0]):
            acc = acc + g_ref[d]
        o_ref[...] = acc

    return pl.pallas_call(kern, out_shape=jax.ShapeDtypeStruct(g.shape[1:], F32), name=name,
                          compiler_params=_cparams())(g)


def _place():
    return lax.axis_index("x"), lax.axis_index("y"), lax.axis_index("c")


def _other_chips(x, y):
    return [(1 - x, y), (x, 1 - y), (1 - x, 1 - y)]


def _remote(src, dst, send_sems, recv_sems, k, to):
    return pltpu.make_async_remote_copy(src_ref=src, dst_ref=dst, send_sem=send_sems.at[k], recv_sem=recv_sems.at[k],
                                        device_id=to, device_id_type=MESH)


def _all_gather_small(v, *, name):
    def body(x_ref, out_ref, send_sems, recv_sems, local_sem):
        x, y, c = _place()
        me = 4 * x + 2 * y + c
        mine = pltpu.make_async_copy(x_ref, out_ref.at[me], local_sem)
        mine.start()
        peers = []
        for f in range(1, 8):
            peers.append((1 - x if f & 4 else x, 1 - y if f & 2 else y, 1 - c if f & 1 else c))
        sends = [_remote(x_ref, out_ref.at[me], send_sems, recv_sems, k, p) for k, p in enumerate(peers)]
        for cp in sends:
            cp.start()
        for k, (px, py, pc) in enumerate(peers):
            _remote(x_ref, out_ref.at[4 * px + 2 * py + pc], send_sems, recv_sems, k, (px, py, pc)).wait_recv()
        for cp in sends:
            cp.wait_send()
        mine.wait()

    return pl.pallas_call(
        body, out_shape=jax.ShapeDtypeStruct((8,) + v.shape, v.dtype),
        in_specs=[pl.BlockSpec(memory_space=pltpu.VMEM)], out_specs=pl.BlockSpec(memory_space=pltpu.VMEM),
        scratch_shapes=[pltpu.SemaphoreType.DMA((7,)), pltpu.SemaphoreType.DMA((7,)), pltpu.SemaphoreType.DMA],
        name=name, compiler_params=_cparams())(v)


def _hosted_call(kern, *, grid, in_specs, out_specs, out_shape, scratch_shapes, args, name, comm=None, sem=None):
    if comm is None:
        res = pl.pallas_call(kern, grid=grid, in_specs=in_specs, out_specs=out_specs, out_shape=out_shape,
                             scratch_shapes=scratch_shapes, name=name, compiler_params=_cparams(sem))(*args)
        return list(res), []
    n_in, n_out, n_scr = len(in_specs), len(out_specs), len(scratch_shapes)
    c_in, c_out = len(comm.inputs), len(comm.out_shapes)

    def body(*refs):
        ins, ci = refs[:n_in], refs[n_in:n_in + c_in]
        outs = refs[n_in + c_in:n_in + c_in + n_out]
        co = refs[n_in + c_in + n_out:n_in + c_in + n_out + c_out]
        scr = refs[n_in + c_in + n_out + c_out:n_in + c_in + n_out + c_out + n_scr]
        cs = refs[n_in + c_in + n_out + c_out + n_scr:]
        ids = [pl.program_id(d) for d in range(len(grid))]
        inner_first = functools.reduce(jnp.logical_and, [i == 0 for i in ids[1:]], True)
        inner_last = functools.reduce(jnp.logical_and, [i == n - 1 for i, n in zip(ids[1:], grid[1:])], True)

        @pl.when(jnp.logical_and(ids[0] == 0, inner_first))
        def _():
            comm.begin(ci, co, cs)

        kern(*ins, *outs, *scr)

        @pl.when(jnp.logical_and(ids[0] == grid[0] // 2, inner_last))
        def _():
            comm.middle(ci, co, cs)

        @pl.when(jnp.logical_and(ids[0] == grid[0] - 1, inner_last))
        def _():
            comm.end(ci, co, cs)

    hbm = pl.BlockSpec(memory_space=pltpu.HBM)
    res = pl.pallas_call(
        body, grid=grid, in_specs=list(in_specs) + [hbm] * c_in, out_specs=list(out_specs) + [hbm] * c_out,
        out_shape=list(out_shape) + list(comm.out_shapes), scratch_shapes=list(scratch_shapes) + list(comm.scratch),
        input_output_aliases={n_in + i: n_out + o for i, o in comm.aliases.items()},
        name=name, compiler_params=_cparams(("arbitrary",) * len(grid)))(*args, *comm.inputs)
    return list(res[:n_out]), list(res[n_out:])


def _run_comm(comm, *, name):
    return _hosted_call(lambda: None, grid=(1,), in_specs=[], out_specs=[], out_shape=[], scratch_shapes=[], args=[],
                        name=name, comm=comm)[1]


class _Gather:
    def __init__(self, shards, kinds, items):
        used = sorted({w for w, _ in items})
        self.slot = {w: k for k, w in enumerate(used)}
        self.inputs = [shards[w] for w in used]
        self.items, self.kinds = list(items), kinds
        self.shapes = {w: shards[w].shape[1:] for w in used}
        self.out_shapes = [jax.ShapeDtypeStruct((r, 4 * n) if kinds[w] == "col" else (4 * r, n), shards[w].dtype)
                           for w, _ in items for r, n in [self.shapes[w]]]
        n_items = len(items)
        self.scratch = [pltpu.SemaphoreType.DMA((6 * n_items,)), pltpu.SemaphoreType.DMA((6 * n_items,)),
                        pltpu.SemaphoreType.DMA((n_items,))]
        self.aliases = {}

    def _piece(self, ref, w, qq, half):
        r, n = self.shapes[w]
        h = r // 2
        lo, size = (0, r) if half is None else (half * h, h)
        if self.kinds[w] == "col":
            return ref.at[pl.ds(pl.multiple_of(lo, 16), size), pl.ds(pl.multiple_of(qq * n, 128), n)]
        return ref.at[pl.ds(pl.multiple_of(qq * r + lo, 16), size), :]

    def _mine(self, ci, w, l, half):
        h = self.shapes[w][0] // 2
        return ci[self.slot[w]].at[l, pl.ds(pl.multiple_of(half * h, 16), h), :]

    def begin(self, ci, co, cs):
        send_sems, recv_sems, local_sems = cs
        x, y, c = _place()
        q = 2 * x + y
        for k, (w, l) in enumerate(self.items):
            pltpu.make_async_copy(ci[self.slot[w]].at[l], self._piece(co[k], w, q, None), local_sems.at[k]).start()
            for j, (cx, cy) in enumerate(_other_chips(x, y)):
                _remote(self._mine(ci, w, l, c), self._piece(co[k], w, q, c), send_sems, recv_sems, 6 * k + j,
                        (cx, cy, c)).start()

    def middle(self, ci, co, cs):
        send_sems, recv_sems, _ = cs
        x, y, c = _place()
        for k, (w, l) in enumerate(self.items):
            for j, (cx, cy) in enumerate(_other_chips(x, y)):
                win = self._piece(co[k], w, 2 * cx + cy, c)
                _remote(win, win, send_sems, recv_sems, 6 * k + j, (cx, cy, c)).wait_recv()
                _remote(win, win, send_sems, recv_sems, 6 * k + 3 + j, (x, y, 1 - c)).start()

    def end(self, ci, co, cs):
        send_sems, recv_sems, local_sems = cs
        x, y, c = _place()
        q = 2 * x + y
        for k, (w, l) in enumerate(self.items):
            for j, (cx, cy) in enumerate(_other_chips(x, y)):
                win = self._piece(co[k], w, 2 * cx + cy, 1 - c)
                _remote(win, win, send_sems, recv_sems, 6 * k + 3 + j, (x, y, 1 - c)).wait_recv()
        for k, (w, l) in enumerate(self.items):
            for j, (cx, cy) in enumerate(_other_chips(x, y)):
                _remote(self._mine(ci, w, l, c), self._piece(co[k], w, q, c), send_sems, recv_sems, 6 * k + j,
                        (cx, cy, c)).wait_send()
                win = self._piece(co[k], w, 2 * cx + cy, c)
                _remote(win, win, send_sems, recv_sems, 6 * k + 3 + j, (x, y, 1 - c)).wait_send()
            pltpu.make_async_copy(ci[self.slot[w]].at[l], self._piece(co[k], w, q, None), local_sems.at[k]).wait()


def _half_rows(ref, half, h):
    return ref.at[:, pl.ds(pl.multiple_of(half * h, 16), h), :]


class _Copies:
    def __init__(self, inputs, out_shapes, count, pairs, aliases=None, lands=None):
        self.inputs, self.out_shapes, self.pairs, self.lands = list(inputs), list(out_shapes), pairs, lands
        self.scratch = [pltpu.SemaphoreType.DMA((count,)), pltpu.SemaphoreType.DMA((count,))]
        self.aliases = aliases or {}

    def _copies(self, ci, co, cs):
        x, y, c = _place()
        return [_remote(src, dst, cs[0], cs[1], k, to) for k, (src, dst, to) in enumerate(self.pairs(ci, co, x, y, c))]

    def begin(self, ci, co, cs):
        for cp in self._copies(ci, co, cs):
            cp.start()

    def middle(self, ci, co, cs):
        pass

    def end(self, ci, co, cs):
        x, y, c = _place()
        for k, (src, dst, to) in enumerate(self.pairs(ci, co, x, y, c)):
            _remote(src, dst, cs[0], cs[1], k, to).wait_send()
            arrival = dst if self.lands is None else self.lands(co, x, y, c)[k]
            _remote(src, arrival, cs[0], cs[1], k, to).wait_recv()


def _send_to_all(v):
    def peers(x, y, c):
        return [(1 - x if f & 4 else x, 1 - y if f & 2 else y, 1 - c if f & 1 else c) for f in range(1, 8)]

    def pairs(ci, co, x, y, c):
        return [(ci[0], co[0].at[4 * x + 2 * y + c], peer) for peer in peers(x, y, c)]

    def lands(co, x, y, c):
        return [co[0].at[4 * px + 2 * py + pc] for px, py, pc in peers(x, y, c)]

    return _Copies([v], [jax.ShapeDtypeStruct((8,) + v.shape, v.dtype)], 7, pairs, lands=lands)


def _swap_halves(gs):
    def pairs(ci, co, x, y, c):
        return [(_half_rows(ci[k], 1 - c, g.shape[1] // 2), co[k], (x, y, 1 - c)) for k, g in enumerate(gs)]

    return _Copies(gs, [jax.ShapeDtypeStruct((g.shape[0], g.shape[1] // 2, g.shape[2]), g.dtype) for g in gs],
                   len(gs), pairs)


def _scatter_quarters(ps, kinds):
    part = [((p.shape[1], p.shape[2] // 4) if kind == "col" else (p.shape[1], p.shape[2])) for p, kind in zip(ps, kinds)]

    def pairs(ci, co, x, y, c):
        out = []
        for k, kind in enumerate(kinds):
            n = part[k][1]
            for j, (cx, cy) in enumerate(_other_chips(x, y)):
                qj = 2 * cx + cy
                src = ci[k].at[0, :, pl.ds(pl.multiple_of(qj * n, 128), n)] if kind == "col" else ci[k].at[qj]
                out.append((src, co[k].at[j], (cx, cy, c)))
        return out

    return _Copies(ps, [jax.ShapeDtypeStruct((3,) + pt, p.dtype) for pt, p in zip(part, ps)], 3 * len(ps), pairs)


def _share_halves(gs):
    def rows(co, k, half):
        h = gs[k].shape[0] // 2
        return co[k].at[pl.ds(pl.multiple_of(half * h, 16), h), :]

    def pairs(ci, co, x, y, c):
        return [(rows(co, k, c), rows(co, k, c), (x, y, 1 - c)) for k in range(len(gs))]

    def lands(co, x, y, c):
        return [rows(co, k, 1 - c) for k in range(len(gs))]

    return _Copies(gs, [jax.ShapeDtypeStruct(g.shape, g.dtype) for g in gs], len(gs), pairs,
                   aliases={k: k for k in range(len(gs))}, lands=lands)


def _wide_tile(n):
    return _pick(n, (2048, 1920, 1024, 512, 256, 128))


def _pair_sum(g, land, place, *, name):
    B, R, N = g.shape
    h = R // 2
    tr, tc = _pick(h, (256, 128)), _wide_tile(N)

    def kern(place_ref, g_ref, l_ref, o_ref):
        o_ref[...] = (g_ref[...] + l_ref[...]).astype(o_ref.dtype)

    grid_spec = pltpu.PrefetchScalarGridSpec(
        num_scalar_prefetch=1, grid=(B, h // tr, N // tc),
        in_specs=[pl.BlockSpec((None, tr, tc), lambda b, i, j, p: (b, p[1] * (h // tr) + i, j)),
                  pl.BlockSpec((None, tr, tc), lambda b, i, j, p: (b, i, j))],
        out_specs=pl.BlockSpec((None, tr, tc), lambda b, i, j, p: (b, i, j)))
    return pl.pallas_call(kern, grid_spec=grid_spec, out_shape=jax.ShapeDtypeStruct((B, h, N), BF16), name=name,
                          compiler_params=_cparams(("parallel", "parallel", "parallel")))(place, g, land)


def _quarter_sum(p, land, kind, shard_shape, place, *, name):
    L, r, n = shard_shape
    h = r // 2
    tr, tc = _pick(h, (256, 128)), _wide_tile(n)

    def kern(place_ref, p_ref, a_ref, b_ref, c_ref, o_ref):
        o_ref[...] = ((p_ref[...].astype(F32) + a_ref[...].astype(F32)) + b_ref[...].astype(F32)) + c_ref[...].astype(F32)

    if kind == "col":
        p_spec = pl.BlockSpec((None, tr, tc), lambda l, i, j, pr: (l, i, pr[0] * (n // tc) + j))
    else:
        p_spec = pl.BlockSpec((None, None, tr, tc), lambda l, i, j, pr: (l, pr[0], i, j))
    lands = [pl.BlockSpec((None, None, tr, tc), functools.partial(lambda l, i, j, pr, s: (s, l, i, j), s=s))
             for s in range(3)]
    grid_spec = pltpu.PrefetchScalarGridSpec(
        num_scalar_prefetch=1, grid=(L, h // tr, n // tc), in_specs=[p_spec] + lands,
        out_specs=pl.BlockSpec((None, tr, tc), lambda l, i, j, pr: (l, pr[1] * (h // tr) + i, j)))
    return pl.pallas_call(kern, grid_spec=grid_spec, out_shape=jax.ShapeDtypeStruct((L, r, n), F32), name=name,
                          compiler_params=_cparams(("parallel", "parallel", "parallel")))(place, p, land, land, land)


class _ReduceScatter:
    def __init__(self, grads, kinds, shard_shapes, place, tag):
        self.kinds, self.shapes, self.place, self.tag = kinds, shard_shapes, place, tag
        self.g3 = [g[None] if kind == "col" else g.reshape(4, g.shape[0] // 4, g.shape[1]) for g, kind in zip(grads, kinds)]

    def swap(self):
        return _swap_halves(self.g3)

    def pair_sums(self, lands):
        self.ps = [_pair_sum(g, land, self.place, name=f"rs_pair_sum_{self.tag}_{k}")
                   for k, (g, land) in enumerate(zip(self.g3, lands))]

    def scatter(self):
        return _scatter_quarters(self.ps, self.kinds)

    def quarter_sums(self, parts):
        self.halves = []
        for k, (p, part) in enumerate(zip(self.ps, parts)):
            p4 = p if self.kinds[k] == "col" else p[None]
            out = _quarter_sum(p4, part[:, None], self.kinds[k], (1,) + tuple(self.shapes[k]), self.place,
                               name=f"rs_quarter_sum_{self.tag}_{k}")
            self.halves.append(out[0])

    def share(self):
        return _share_halves(self.halves)

    def run(self):
        self.pair_sums(_run_comm(self.swap(), name=f"rs_swap_{self.tag}"))
        self.quarter_sums(_run_comm(self.scatter(), name=f"rs_scatter_{self.tag}"))
        return _run_comm(self.share(), name=f"rs_share_{self.tag}")


_WEIGHTS = ["mod_w", "mod_b", "norm1_g", "w_in", "gate_b", "conv_w", "conv_b", "conv_ln_g", "conv_ln_b", "w_conv_proj",
            "hgrn_lb", "hgrn_norm_g", "w_hgrn_proj", "sb_qn_g", "sb_kn_g", "w_sb_proj", "w_out", "norm2_g", "mlp_w1",
            "mlp_w2"]
_BIG = [("w_in", "col"), ("w_conv_proj", "col"), ("w_hgrn_proj", "col"), ("w_sb_proj", "col"), ("w_out", "row"),
        ("mlp_w1", "col"), ("mlp_w2", "row")]
_REPLICATED = ["mod_b", "norm1_g", "gate_b", "conv_b", "conv_ln_g", "conv_ln_b", "hgrn_lb", "hgrn_norm_g", "sb_qn_g",
               "sb_kn_g", "norm2_g"]
LANES = 128


class _Pack:
    def __init__(self, items):
        self.shapes = {n: a.shape for n, a in items}
        self.offsets, pos = {}, 0
        for n, a in items:
            self.offsets[n] = pos
            pos += math.prod(a.shape)
        self.rows = -(-pos // (8 * LANES)) * 8
        flat = jnp.concatenate([a.reshape(-1).astype(F32) for _, a in items])
        self.array = jnp.pad(flat, (0, self.rows * LANES - pos)).reshape(self.rows, LANES)

    def get(self, packed, name):
        lead = packed.shape[:-2]
        flat = packed.reshape(lead + (self.rows * LANES,))
        n = math.prod(self.shapes[name])
        return lax.slice_in_dim(flat, self.offsets[name], self.offsets[name] + n, axis=len(lead)).reshape(
            lead + self.shapes[name])


def _lower_bounds(hgrn_lb):
    p = jax.nn.softmax(hgrn_lb.astype(F32), axis=0)
    return jnp.cumsum(p, axis=0) - p[0:1]


def _layer_fwd(x, w, p, l, comms=(None, None)):
    S, D = x.shape
    r = {"x": x}
    (r["h"],) = _rowop(_fn_normmod, [(x, 0, D)], [p["n1g"], p["sc1"], p["sh1"]], [(D, BF16)], name=f"normmod1_fwd_{l}")
    proj = r["proj"] = _matmul(r["h"], w["w_in", l], name=f"w_in_fwd_{l}")
    r["cpre"] = _conv_fwd(proj, p["w32"], p["conv_b"], name=f"conv_fwd_{l}")
    (r["cact"],) = _rowop(_fn_lnsilu, [(r["cpre"], 0, CONV_CH)], [p["lng"], p["lnb"]], [(CONV_CH, BF16)],
                          name=f"conv_ln_fwd_{l}")
    arrived = lambda comm, got: w.update({(_BIG[k][0], layer): arr for (k, layer), arr in zip(comm.items, got)})
    (r["hg"], r["states"]), got = _hgrn_fwd(proj, p["lbk"], p["ng"], name=f"hgrn_fwd_{l}", comm=comms[0])
    if comms[0] is not None:
        arrived(comms[0], got)
    (r["sb"], r["rs"]), got = _sb_fwd(proj, p["qg"], p["kg"], name=f"sb_fwd_{l}", comm=comms[1])
    if comms[1] is not None:
        arrived(comms[1], got)
    r["y_c"] = _matmul(r["cact"], w["w_conv_proj", l], out_dtype=BF16, name=f"w_conv_proj_fwd_{l}")
    r["y_h"] = _matmul(r["hg"], w["w_hgrn_proj", l], out_dtype=BF16, name=f"w_hgrn_proj_fwd_{l}")
    r["y_s"] = _matmul(r["sb"], w["w_sb_proj", l], out_dtype=BF16, name=f"w_sb_proj_fwd_{l}")
    (r["merged"],) = _rowop(_fn_merge, [(proj, OFF_GL, 3 * D), (r["y_c"], 0, D), (r["y_h"], 0, D), (r["y_s"], 0, D)],
                            [p["gate_b"]], [(D, BF16)], name=f"merge_fwd_{l}")
    resid = lambda y, x_in, gate: (y,) + _fn_resid(x_in, y, gate)
    r["a_out"], r["x1"] = _matmul(r["merged"], w["w_out", l], name=f"w_out_fwd_{l}", post=resid, extras=[x],
                                  rows=[p["g1"]], out_dtypes=(F32, F32))
    (r["h2"],) = _rowop(_fn_normmod, [(r["x1"], 0, D)], [p["n2g"], p["sc2"], p["sh2"]], [(D, BF16)],
                        name=f"normmod2_fwd_{l}")
    r["u"], r["act"] = _matmul(r["h2"], w["mlp_w1", l], name=f"mlp_w1_fwd_{l}", post=lambda u: (u,) + _fn_relu2(u),
                               out_dtypes=(BF16, BF16))
    r["m_out"], x2 = _matmul(r["act"], w["mlp_w2", l], name=f"mlp_w2_fwd_{l}", post=resid, extras=[r["x1"]],
                             rows=[p["g2"]], out_dtypes=(F32, F32))
    return x2, r


def _layer_bwd(dx2, r, w, p, l, grads, carry=None, last=None):
    S, D = dx2.shape
    small = {}

    def dweight(name, a, dy):
        grads[name, l] = _matmul(a, dy, ta=True, name=f"{name}_dw_{l}")

    stage = (lambda k, got: carry(k, got)) if carry is not None else (lambda k, got: None)

    (dm_out,), (dg2,) = _rowop_bwd(_fn_scale, [(r["m_out"], 0, D)], [p["g2"]], [dx2], [BF16], name=f"resid2_bwd_{l}")
    (du,) = _matmul(dm_out, w["mlp_w2", l], tb=True, name=f"mlp_w2_dx_{l}", extras=[r["u"]], out_dtypes=(BF16,),
                    post=lambda dact, u: (dact * (2.0 * jnp.maximum(u, 0.0)),))
    dweight("mlp_w2", r["act"], dm_out)
    dh2 = _matmul(du, w["mlp_w1", l], tb=True, name=f"mlp_w1_dx_{l}")
    dweight("mlp_w1", r["h2"], du)
    (dx1, da_out), (dg1, small["norm2_g"], dsc2, dsh2) = _rowop_bwd(
        _fn_resid_norm, [(r["x"], 0, D), (r["a_out"], 0, D)], [p["g1"], p["n2g"], p["sc2"], p["sh2"]], [dx2, dh2],
        [F32, BF16], name=f"resid1_norm2_bwd_{l}")
    dmerged = _matmul(da_out, w["w_out", l], tb=True, name=f"w_out_dx_{l}")
    dweight("w_out", r["merged"], da_out)
    (dgl, dy_c, dy_h, dy_s), (small["gate_b"],) = _rowop_bwd(
        _fn_merge, [(r["proj"], OFF_GL, 3 * D), (r["y_c"], 0, D), (r["y_h"], 0, D), (r["y_s"], 0, D)], [p["gate_b"]],
        [dmerged], [BF16] * 4, name=f"merge_bwd_{l}")
    dweight("w_conv_proj", r["cact"], dy_c)
    dweight("w_hgrn_proj", r["hg"], dy_h)
    dweight("w_sb_proj", r["sb"], dy_s)
    dcact = _matmul(dy_c, w["w_conv_proj", l], tb=True, name=f"w_conv_proj_dx_{l}")
    (dcpre,), (small["conv_ln_g"], small["conv_ln_b"]) = _rowop_bwd(
        _fn_lnsilu, [(r["cpre"], 0, CONV_CH)], [p["lng"], p["lnb"]], [dcact], [F32], name=f"conv_ln_bwd_{l}")
    (d_conv, dw32, small["conv_b"]), got = _conv_bwd(r["proj"], dcpre, p["w32"], name=f"conv_bwd_{l}",
                                                      comm=stage(0, None))
    small["conv_w"] = dw32[:CONV_WIDTH]
    dhg = _matmul(dy_h, w["w_hgrn_proj", l], tb=True, out_dtype=BF16, name=f"w_hgrn_proj_dx_{l}")
    (dq, df, di, dg, dlbk, dng), got = _hgrn_bwd(r["proj"], r["states"], dhg, p["lbk"], p["ng"], name=f"hgrn_bwd_{l}",
                                                 comm=stage(1, got))
    small["lower"] = -dlbk
    small["hgrn_norm_g"] = jnp.sum(dng, axis=0)
    dsb = _matmul(dy_s, w["w_sb_proj", l], tb=True, out_dtype=BF16, name=f"w_sb_proj_dx_{l}")
    (dsq, dsk, dsv, dqg, dkg), got = _sb_bwd(r["proj"], p["qg"], p["kg"], r["rs"], dsb, name=f"sb_bwd_{l}",
                                             comm=stage(2, got))
    stage(3, got)
    fold = lambda t: jnp.sum(t.reshape(-1, SB_DH), axis=0, keepdims=True)
    small["sb_qn_g"], small["sb_kn_g"] = fold(dqg), fold(dkg)
    dproj = jnp.concatenate([d_conv, dq, df, di, dg, dsq, dsk, dsv, dgl], axis=1)
    dweight("w_in", r["h"], dproj)
    norm1 = functools.partial(_rowop_bwd, _fn_normmod, [(r["x"], 0, D)], [p["n1g"], p["sc1"], p["sh1"]],
                              din_dtypes=[F32], add={0: dx1}, name=f"normmod1_bwd_{l}")
    if last is None:
        dh = _matmul(dproj, w["w_in", l], tb=True, name=f"w_in_dx_{l}")
        (dx,), (small["norm1_g"], dsc1, dsh1) = norm1(douts=[dh])
    else:
        dh, got = _matmul(dproj, w["w_in", l], tb=True, name=f"w_in_dx_{l}", comm=last(0, None))
        (dx,), (small["norm1_g"], dsc1, dsh1), got = norm1(douts=[dh], comm=last(1, got))
        last(2, got)
    small["mod"] = jnp.concatenate([dsh1, dsc1, dg1, dsh2, dsc2, dg2], axis=1)
    return dx, small


def kernel(x, c, mod_w, mod_b, norm1_g, w_in, gate_b, conv_w, conv_b, conv_ln_g, conv_ln_b, w_conv_proj, hgrn_lb, hgrn_norm_g, w_hgrn_proj, sb_qn_g, sb_kn_g, w_sb_proj, w_out, norm2_g, mlp_w1, mlp_w2, loss_target, m_mod_w, m_mod_b, m_norm1_g, m_w_in, m_gate_b, m_conv_w, m_conv_b, m_conv_ln_g, m_conv_ln_b, m_w_conv_proj, m_hgrn_lb, m_hgrn_norm_g, m_w_hgrn_proj, m_sb_qn_g, m_sb_kn_g, m_w_sb_proj, m_w_out, m_norm2_g, m_mlp_w1, m_mlp_w2, v_mod_w, v_mod_b, v_norm1_g, v_w_in, v_gate_b, v_conv_w, v_conv_b, v_conv_ln_g, v_conv_ln_b, v_w_conv_proj, v_hgrn_lb, v_hgrn_norm_g, v_w_hgrn_proj, v_sb_qn_g, v_sb_kn_g, v_w_sb_proj, v_w_out, v_norm2_g, v_mlp_w1, v_mlp_w2):
    given = dict(locals())
    wts = {n: given[n] for n in _WEIGHTS}
    mom = {n: given["m_" + n] for n in _WEIGHTS}
    var = {n: given["v_" + n] for n in _WEIGHTS}
    n_layers, D = norm1_g.shape
    xi, yi, ci = _place()
    q = 2 * xi + yi
    me = 4 * xi + 2 * yi + ci
    place = jnp.stack([q, ci]).astype(jnp.int32)
    n_mod = mod_w.shape[2]
    cw = conv_w.shape[2]

    pk1 = _Pack([("c", c), ("conv_w", conv_w)])
    got1 = _all_gather_small(pk1.array, name="gather_cond")
    c_act = jax.nn.silu(pk1.get(got1, "c")[:, 0, :])
    conv_full = jnp.concatenate([pk1.get(got1, "conv_w")[2 * k] for k in range(4)], axis=-1)

    mod_cols = []
    for l in range(n_layers):
        mb = lax.dynamic_slice_in_dim(mod_b[l], q * n_mod, n_mod)
        mod_cols.append(_matmul(c_act, mod_w, bl=l, name=f"mod_fwd_{l}") + mb[None, :])
    got2 = _all_gather_small(jnp.concatenate(mod_cols, axis=0), name="gather_mod")
    mods = []
    for l in range(n_layers):
        row = lax.dynamic_index_in_dim(got2[0::2], l * 8 + me, axis=1, keepdims=False)
        mods.append(jnp.split(row.reshape(1, 4 * n_mod), 6, axis=1))

    lower, lower_vjp = jax.vjp(_lower_bounds, hgrn_lb)

    shards = [wts[n].astype(BF16) for n, _ in _BIG]
    kinds = [k for _, k in _BIG]
    index = {n: k for k, (n, _) in enumerate(_BIG)}
    first = ["w_in", "w_conv_proj", "w_hgrn_proj", "w_sb_proj"]

    def gather(*names_layers):
        items = [(index[n], l) for names, l in names_layers for n in names if l < n_layers]
        return _Gather(shards, kinds, items) if items else None

    start = gather((first[:1], 0))
    w = {(_BIG[k][0], layer): arr
         for (k, layer), arr in zip(start.items, _run_comm(start, name="gather_first_weights"))}

    def layer_params(l):
        sh1, sc1, g1, sh2, sc2, g2 = mods[l]
        return dict(sh1=sh1, sc1=sc1, g1=g1, sh2=sh2, sc2=sc2, g2=g2, n1g=norm1_g[l][None], n2g=norm2_g[l][None],
                    gate_b=gate_b[l][None], conv_b=conv_b[l][None], lng=conv_ln_g[l][None], lnb=conv_ln_b[l][None],
                    w32=jnp.pad(conv_full[l], ((0, CONV_HALO - CONV_WIDTH), (0, 0))), lbk=(1.0 - lower[l])[None],
                    ng=hgrn_norm_g[l][None], qg=jnp.tile(sb_qn_g[l][None], (1, SB_PAIR // SB_DH)),
                    kg=jnp.tile(sb_kn_g[l][None], (1, SB_PAIR // SB_DH)))

    params = [layer_params(l) for l in range(n_layers)]
    act, saved = x[0], []
    for l in range(n_layers):
        comms = (gather((first[1:] if l == 0 else [], l), (["w_out", "mlp_w1"], l)),
                 gather((["mlp_w2"], l), (first, l + 1)))
        act, r = _layer_fwd(act, w, params[l], l, comms=comms)
        saved.append(r)
    dact, loss_lanes = _loss_head(act, loss_target[0], name="loss_head")

    grads, smalls, reduced = {}, [None] * n_layers, {}

    def reduce_scatter(items, tag):
        return _ReduceScatter([grads[_BIG[k][0], layer] for k, layer in items], [kinds[k] for k, _ in items],
                              [shards[k].shape[1:] for k, _ in items], place, tag)

    def carried(l):
        items = [(k, l + 1) for k in range(len(_BIG))] + [(k, l) for k, (n, _) in enumerate(_BIG) if n != "w_in"]
        box = {}

        def carry(stage, got):
            if stage == 0:
                box["rs"] = reduce_scatter(items, f"l{l}")
                return box["rs"].swap()
            if stage == 1:
                box["rs"].pair_sums(got)
                return box["rs"].scatter()
            if stage == 2:
                box["rs"].quarter_sums(got)
                return box["rs"].share()
            reduced.update(zip(items, got))

        return carry

    def final(l):
        items = [(index["w_in"], l)]
        box = {}

        def step(stage, got):
            if stage == 0:
                box["rs"] = reduce_scatter(items, "w_in")
                box["rs"].pair_sums(_run_comm(box["rs"].swap(), name="rs_swap_w_in"))
                return box["rs"].scatter()
            if stage == 1:
                box["rs"].quarter_sums(got)
                return box["rs"].share()
            reduced.update(zip(items, got))

        return step

    for l in reversed(range(n_layers)):
        dact, smalls[l] = _layer_bwd(dact, saved[l], w, params[l], l, grads, carried(l) if l + 1 < n_layers else None,
                                     final(l) if l == 0 else None)
    grad_x = dact[None]
    rest = [(k, l) for l in range(n_layers) for k in range(len(_BIG)) if (k, l) not in reduced]
    if rest:
        reduced.update(zip(rest, reduce_scatter(rest, "rest").run()))

    stack = lambda k: jnp.stack([smalls[l][k] for l in range(n_layers)])
    (d_hgrn_lb,) = lower_vjp(stack("lower")[:, 0, :])
    items = [("loss", loss_lanes), ("mod", stack("mod")), ("hgrn_lb", d_hgrn_lb), ("conv_w", stack("conv_w"))]
    items += [(k, stack(k)) for k in ("norm1_g", "gate_b", "conv_b", "conv_ln_g", "conv_ln_b", "hgrn_norm_g", "sb_qn_g",
                                      "sb_kn_g", "norm2_g")]
    pk3 = _Pack(items)

    share_small = _send_to_all(pk3.array)
    delta, new_m, new_v, big, got3 = {}, {}, {}, {}, None
    for n, _ in _BIG:
        outs = None
        for l in reversed(range(n_layers)):
            args = (wts[n], reduced[index[n], l], mom[n], var[n], l, outs)
            if got3 is None:
                outs, (got3,) = _adamw_layer(*args, name=f"adamw_{n}_{l}", comm=share_small)
            else:
                outs = _adamw_layer(*args, name=f"adamw_{n}_{l}")
        big[n] = outs
    got3 = lax.dynamic_update_slice_in_dim(got3, pk3.array[None], me, axis=0)
    tot3 = _sum8(got3, name="sum_small_grads")
    loss = (0.5 / D) * jnp.sum(pk3.get(tot3, "loss"))
    g = {k: pk3.get(tot3, k).reshape(wts[k].shape) for k in _REPLICATED if k != "mod_b"}
    g["mod_b"] = pk3.get(tot3, "mod")[:, 0, :]
    g["conv_w"] = lax.dynamic_slice_in_dim(pk3.get(tot3, "conv_w"), q * cw, cw, axis=2)
    dmod_all = pk3.get(got3, "mod")[:, :, 0, :]
    g_mod_w = None
    for l in range(n_layers):
        cols = lax.dynamic_slice_in_dim(dmod_all[:, l, :], q * n_mod, n_mod, axis=1)
        g_mod_w = _matmul(c_act, cols, ta=True, layer=l, n_layers=n_layers, into=g_mod_w, name=f"mod_dw_{l}")
    g["mod_w"] = g_mod_w

    for n, _ in _BIG:
        g[n], delta[n], new_m[n], new_v[n] = big[n]
    two_d = lambda t: t.reshape(-1, t.shape[-1])
    outs = _adamw(two_d(mod_w), two_d(g["mod_w"]), two_d(m_mod_w), two_d(v_mod_w), name="adamw_mod_w")
    delta["mod_w"], new_m["mod_w"], new_v["mod_w"] = (t.reshape(mod_w.shape) for t in outs)
    rest = _REPLICATED + ["conv_w"]
    packs = [_Pack([(n, src[n]) for n in rest]) for src in (wts, g, mom, var)]
    outs = _adamw(*[pk.array for pk in packs], name="adamw_small")
    for n in rest:
        delta[n], new_m[n], new_v[n] = (packs[0].get(t, n) for t in outs)

    return (loss, grad_x, *[g[n] for n in _WEIGHTS], *[delta[n] for n in _WEIGHTS], *[new_m[n] for n in _WEIGHTS],
            *[new_v[n] for n in _WEIGHTS])
```

```python
import functools
import math

import jax
import jax.numpy as jnp
from jax import lax
from jax.experimental import pallas as pl
from jax.experimental.pallas import tpu as pltpu

F32 = jnp.float32
BF16 = jnp.bfloat16
MESH = pl.DeviceIdType.MESH

EPS = 1e-6
CONV_CH = 512
CONV_WIDTH = 31
CONV_HALO = 32
HG_HEADS = 4
HG_D = 128
HG_CHUNK = 64
HG_KEYS = 8
HG_SUB = 32
SB_HEADS = 8
SB_DH = 64
SB_BLK = 128
SB_PAIR = 128
SB_SKIP = -104.0
OFF_CONV, OFF_HG, OFF_SB, OFF_GL = 0, 1024, 3072, 4608
ADAM_LR, ADAM_B1, ADAM_B2, ADAM_EPS, ADAM_WD, ADAM_STEP = 0.001, 0.9, 0.999, 1e-08, 0.01, 10
VMEM_LIMIT_BYTES = 56 * 1024 * 1024
ROW_TILE = 256


def _cparams(sem=None, **kw):
    return pltpu.CompilerParams(dimension_semantics=sem, vmem_limit_bytes=VMEM_LIMIT_BYTES, **kw)


def _pick(n, cands):
    for c in cands:
        if n % c == 0:
            return c
    return n


MATMUL_VMEM_BUDGET = 40 * 1024 * 1024


def _tile_options(n, cap):
    opts = [t for t in range(cap - cap % 128, 0, -128) if n % t == 0]
    return opts or [n]


def _matmul_tiles(M, N, K, size_a, size_b, size_o, in_acc):
    for tm in _tile_options(M, 1024):
        for tk in _tile_options(K, 2048):
            for tn in _tile_options(N, 1280):
                need = 2 * (tm * tk * size_a + tk * tn * size_b + tm * tn * size_o)
                if K > tk and not in_acc:
                    need += tm * tn * 4
                if need <= MATMUL_VMEM_BUDGET:
                    return tm, tn, tk
    raise ValueError(f"no matmul tiling fits VMEM for {(M, N, K)}")
def _matmul(a, b, *, ta=False, tb=False, bl=None, out_dtype=F32, name, into=None, layer=None, n_layers=None,
            post=None, extras=(), rows=(), out_dtypes=None, comm=None):
    M, K = (a.shape[1], a.shape[0]) if ta else a.shape
    N = b.shape[-2] if tb else b.shape[-1]
    if post is not None:
        return _matmul_post(a, b, M, N, K, ta, tb, post, extras, rows, out_dtypes, name)
    assert comm is None or layer is None
    in_acc = jnp.dtype(out_dtype) == jnp.dtype(F32)
    tm, tn, tk = _matmul_tiles(M, N, K, a.dtype.itemsize, b.dtype.itemsize, jnp.dtype(out_dtype).itemsize, in_acc)
    nk = K // tk
    a_spec = pl.BlockSpec((tk, tm), lambda i, j, k: (k, i)) if ta else pl.BlockSpec((tm, tk), lambda i, j, k: (i, k))
    if bl is None:
        b_spec = pl.BlockSpec((tn, tk), lambda i, j, k: (j, k)) if tb else pl.BlockSpec((tk, tn), lambda i, j, k: (k, j))
    elif tb:
        b_spec = pl.BlockSpec((None, tn, tk), lambda i, j, k: (bl, j, k))
    else:
        b_spec = pl.BlockSpec((None, tk, tn), lambda i, j, k: (bl, k, j))
    dn = (((0 if ta else 1,), (1 if tb else 0,)), ((), ()))

    use_scratch = nk > 1 and not in_acc

    def kern(a_ref, b_ref, *rest):
        o_ref = rest[-2] if use_scratch else rest[-1]
        prod = lambda: lax.dot_general(a_ref[...].astype(BF16), b_ref[...].astype(BF16), dn,
                                       preferred_element_type=F32)
        if nk == 1:
            o_ref[...] = prod().astype(o_ref.dtype).reshape(o_ref.shape)
            return
        acc_ref = rest[-1] if use_scratch else o_ref
        k = pl.program_id(2)

        @pl.when(k == 0)
        def _():
            acc_ref[...] = prod().reshape(acc_ref.shape)

        @pl.when(k > 0)
        def _():
            acc_ref[...] += prod().reshape(acc_ref.shape)

        if use_scratch:
            @pl.when(k == nk - 1)
            def _():
                o_ref[...] = acc_ref[...].astype(o_ref.dtype).reshape(o_ref.shape)

    in_specs, args, aliases = [a_spec, b_spec], [a, b], {}
    if layer is None:
        out_shape = jax.ShapeDtypeStruct((M, N), out_dtype)
        out_spec = pl.BlockSpec((tm, tn), lambda i, j, k: (i, j))
    else:
        out_shape = jax.ShapeDtypeStruct((n_layers, M, N), out_dtype)
        out_spec = pl.BlockSpec((1, tm, tn), lambda i, j, k: (layer, i, j))
        if into is not None:
            in_specs.append(pl.BlockSpec(memory_space=pl.ANY))
            args.append(into)
            aliases = {2: 0}
    if comm is not None:
        (out,), got = _hosted_call(kern, grid=(M // tm, N // tn, nk), in_specs=in_specs, out_specs=[out_spec],
                                   out_shape=[out_shape], scratch_shapes=[pltpu.VMEM((tm, tn), F32)] if use_scratch else [],
                                   args=args, name=name, comm=comm)
        return out, got
    return pl.pallas_call(
        kern, grid=(M // tm, N // tn, nk), in_specs=in_specs, out_specs=out_spec, out_shape=out_shape,
        scratch_shapes=[pltpu.VMEM((tm, tn), F32)] if use_scratch else [],
        input_output_aliases=aliases, name=name,
        compiler_params=_cparams(("parallel", "parallel", "arbitrary")))(*args)


def _matmul_post(a, b, M, N, K, ta, tb, post, extras, rows, out_dtypes, name):
    per_elem = sum(e.dtype.itemsize for e in extras) + sum(jnp.dtype(d).itemsize for d in out_dtypes)
    fits = lambda tm, tn: 2 * (tm * K * a.dtype.itemsize + K * tn * b.dtype.itemsize + tm * tn * per_elem) <= MATMUL_VMEM_BUDGET
    tm, tn = next((tm, tn) for tm in _tile_options(M, 1024) for tn in _tile_options(N, 1280) if fits(tm, tn))
    a_spec = pl.BlockSpec((K, tm), lambda i, j: (0, i)) if ta else pl.BlockSpec((tm, K), lambda i, j: (i, 0))
    b_spec = pl.BlockSpec((tn, K), lambda i, j: (j, 0)) if tb else pl.BlockSpec((K, tn), lambda i, j: (0, j))
    tile = pl.BlockSpec((tm, tn), lambda i, j: (i, j))
    row = pl.BlockSpec((1, tn), lambda i, j: (0, j))
    dn = (((0 if ta else 1,), (1 if tb else 0,)), ((), ()))
    n_ex = len(extras) + len(rows)

    def kern(a_ref, b_ref, *rest):
        prod = lax.dot_general(a_ref[...].astype(BF16), b_ref[...].astype(BF16), dn, preferred_element_type=F32)
        res = post(prod, *[r[...].astype(F32) for r in rest[:n_ex]])
        for val, o_ref in zip(res, rest[n_ex:]):
            o_ref[...] = val.astype(o_ref.dtype)

    return pl.pallas_call(
        kern, grid=(M // tm, N // tn), in_specs=[a_spec, b_spec] + [tile] * len(extras) + [row] * len(rows),
        out_specs=[tile] * len(out_dtypes), out_shape=[jax.ShapeDtypeStruct((M, N), d) for d in out_dtypes], name=name,
        compiler_params=_cparams(("parallel", "parallel")))(a, b, *extras, *rows)


def _col_specs(off, width, T):
    bw = math.gcd(width, off) if off else width
    return [pl.BlockSpec((T, bw), functools.partial(lambda i, c: (i, c), c=off // bw + p)) for p in range(width // bw)]


def _gather_rows(refs, counts):
    vals, pos = [], 0
    for n in counts:
        parts = [refs[pos + p][...].astype(F32) for p in range(n)]
        pos += n
        vals.append(parts[0] if n == 1 else jnp.concatenate(parts, axis=1))
    return vals, pos


def _rowop(fn, ins, params, outs, *, name):
    S = ins[0][0].shape[0]
    T = min(ROW_TILE, S)
    in_specs, counts, args = [], [], []
    for arr, off, width in ins:
        sp = _col_specs(off, width, T)
        in_specs += sp
        counts.append(len(sp))
        args += [arr] * len(sp)
    in_specs += [pl.BlockSpec(p.shape, lambda i: (0, 0)) for p in params]

    def kern(*refs):
        vals, pos = _gather_rows(refs, counts)
        pv = [refs[pos + p][...] for p in range(len(params))]
        pos += len(params)
        res = fn(*vals, *pv)
        for r, o_ref in zip(res, refs[pos:]):
            o_ref[...] = r.astype(o_ref.dtype)

    return pl.pallas_call(
        kern, grid=(S // T,), in_specs=in_specs,
        out_specs=[pl.BlockSpec((T, w), lambda i: (i, 0)) for w, _ in outs],
        out_shape=[jax.ShapeDtypeStruct((S, w), dt) for w, dt in outs],
        name=name, compiler_params=_cparams(("parallel",)))(*args, *params)


def _rowop_bwd(fn, ins, params, douts, din_dtypes, *, name, add=None, comm=None):
    add = add or {}
    S = ins[0][0].shape[0]
    T = min(ROW_TILE, S)
    in_specs, counts, args = [], [], []
    for arr, off, width in ins:
        sp = _col_specs(off, width, T)
        in_specs += sp
        counts.append(len(sp))
        args += [arr] * len(sp)
    in_specs += [pl.BlockSpec(p.shape, lambda i: (0, 0)) for p in params]
    in_specs += [pl.BlockSpec((T, d.shape[1]), lambda i: (i, 0)) for d in douts]
    add_keys = sorted(add)
    in_specs += [pl.BlockSpec((T, add[k].shape[1]), lambda i: (i, 0)) for k in add_keys]
    want = [k for k, dt in enumerate(din_dtypes) if dt is not None]

    def kern(*refs):
        vals, pos = _gather_rows(refs, counts)
        pv = [refs[pos + p][...] for p in range(len(params))]
        pos += len(params)
        cts = [refs[pos + p][...].astype(F32) for p in range(len(douts))]
        pos += len(douts)
        adds = {k: refs[pos + p][...].astype(F32) for p, k in enumerate(add_keys)}
        pos += len(add_keys)
        _, vjp = jax.vjp(fn, *vals, *pv)
        grads = vjp(tuple(cts))
        for k in want:
            g = grads[k] + adds[k] if k in adds else grads[k]
            refs[pos][...] = g.astype(refs[pos].dtype)
            pos += 1
        first = pl.program_id(0) == 0
        for p in range(len(params)):
            gp, o_ref = grads[len(ins) + p], refs[pos + p]

            @pl.when(first)
            def _(gp=gp, o_ref=o_ref):
                o_ref[...] = gp

            @pl.when(jnp.logical_not(first))
            def _(gp=gp, o_ref=o_ref):
                o_ref[...] += gp

    out_specs = [pl.BlockSpec((T, ins[k][2]), lambda i: (i, 0)) for k in want]
    out_specs += [pl.BlockSpec(p.shape, lambda i: (0, 0)) for p in params]
    out_shape = [jax.ShapeDtypeStruct((S, ins[k][2]), din_dtypes[k]) for k in want]
    out_shape += [jax.ShapeDtypeStruct(p.shape, F32) for p in params]
    res, got = _hosted_call(
        kern, grid=(S // T,), in_specs=in_specs, out_specs=out_specs, out_shape=out_shape, scratch_shapes=[],
        args=[*args, *params, *douts, *[add[k] for k in add_keys]], name=name, comm=comm, sem=("arbitrary",))
    dins = [None] * len(ins)
    for p, k in enumerate(want):
        dins[k] = res[p]
    return (dins, res[len(want):]) if comm is None else (dins, res[len(want):], got)


def _rms(x, g):
    return x * lax.rsqrt(jnp.mean(x * x, axis=-1, keepdims=True) + EPS) * g


def _fn_normmod(x, g, sc, sh):
    return (_rms(x, g) * (1.0 + sc) + sh,)


def _fn_lnsilu(c, g, b):
    mu = jnp.mean(c, axis=-1, keepdims=True)
    var = jnp.mean(jnp.square(c - mu), axis=-1, keepdims=True)
    y = (c - mu) * lax.rsqrt(var + EPS) * g + b
    return (y * jax.nn.sigmoid(y),)


def _fn_merge(gl, yc, yh, ys, gb):
    d = yc.shape[1]
    g = jax.nn.sigmoid(gl + gb)
    return (g[:, :d] * yc + g[:, d:2 * d] * yh + g[:, 2 * d:] * ys,)


def _fn_resid(x, y, g):
    return (x + g * y,)


def _fn_resid_norm(x, y, g, n, sc, sh):
    x1 = x + g * y
    return (x1,) + _fn_normmod(x1, n, sc, sh)


def _fn_scale(y, g):
    return (g * y,)


def _fn_relu2(u):
    return (jnp.square(jnp.maximum(u, 0.0)),)


def _conv_specs(S, T):
    r = T // CONV_HALO
    cur = [pl.BlockSpec((T, CONV_CH), lambda i: (i, 0)), pl.BlockSpec((T, CONV_CH), lambda i: (i, 1))]
    prev = [pl.BlockSpec((CONV_HALO, CONV_CH), lambda i: (jnp.maximum(i * r - 1, 0), 0)),
            pl.BlockSpec((CONV_HALO, CONV_CH), lambda i: (jnp.maximum(i * r - 1, 0), 1))]
    return cur + prev


def _glu_ext(a_ref, g_ref, ah_ref, gh_ref):
    a = a_ref[...]
    sg = jax.nn.sigmoid(g_ref[...])
    uh = jnp.where(pl.program_id(0) > 0, ah_ref[...] * jax.nn.sigmoid(gh_ref[...]), 0.0)
    return a, sg, jnp.concatenate([uh, a * sg], axis=0)


def _shift_up(xe, k, T):
    return xe[:T] if k == 0 else pltpu.roll(xe, shift=xe.shape[0] - k, axis=0)[:T]


def _conv_fwd(proj, w32, b, *, name):
    S = proj.shape[0]
    T = min(ROW_TILE, S)
    lead = CONV_HALO - (CONV_WIDTH - 1)

    def kern(a_ref, g_ref, ah_ref, gh_ref, w_ref, b_ref, o_ref):
        _, _, ue = _glu_ext(a_ref, g_ref, ah_ref, gh_ref)
        acc = jnp.zeros((T, CONV_CH), F32) + b_ref[...]
        for j in range(CONV_WIDTH):
            acc = acc + w_ref[j:j + 1, :] * _shift_up(ue, lead + j, T)
        o_ref[...] = acc

    const = lambda shape: pl.BlockSpec(shape, lambda i: (0, 0))
    return pl.pallas_call(
        kern, grid=(S // T,), in_specs=_conv_specs(S, T) + [const(w32.shape), const(b.shape)],
        out_specs=pl.BlockSpec((T, CONV_CH), lambda i: (i, 0)),
        out_shape=jax.ShapeDtypeStruct((S, CONV_CH), F32), name=name,
        compiler_params=_cparams(("parallel",)))(proj, proj, proj, proj, w32, b)


def _conv_bwd(proj, dc, w32, *, name, comm=None):
    S = proj.shape[0]
    T = min(ROW_TILE, S)
    nt = S // T
    r = T // CONV_HALO
    lead = CONV_HALO - (CONV_WIDTH - 1)
    last_halo = S // CONV_HALO - 1

    def kern(a_ref, g_ref, ah_ref, gh_ref, dc_ref, dcn_ref, w_ref, dag_ref, dw_ref, db_ref):
        i = pl.program_id(0)
        a, sg, ue = _glu_ext(a_ref, g_ref, ah_ref, gh_ref)
        dc_t = dc_ref[...]
        de = jnp.concatenate([dc_t, jnp.where(i < nt - 1, dcn_ref[...], 0.0)], axis=0)

        @pl.when(i == 0)
        def _():
            dw_ref[...] = jnp.zeros_like(dw_ref)
            db_ref[...] = jnp.zeros_like(db_ref)

        du = jnp.zeros((T, CONV_CH), F32)
        for j in range(CONV_WIDTH):
            du = du + w_ref[j:j + 1, :] * _shift_up(de, CONV_WIDTH - 1 - j, T)
            dw_ref[j:j + 1, :] += jnp.sum(dc_t * _shift_up(ue, lead + j, T), axis=0, keepdims=True)
        db_ref[...] += jnp.sum(dc_t, axis=0, keepdims=True)
        dag_ref[:, :CONV_CH] = (du * sg).astype(BF16)
        dag_ref[:, CONV_CH:] = (du * a * sg * (1.0 - sg)).astype(BF16)

    const = lambda shape: pl.BlockSpec(shape, lambda i: (0, 0))
    in_specs = _conv_specs(S, T) + [
        pl.BlockSpec((T, CONV_CH), lambda i: (i, 0)),
        pl.BlockSpec((CONV_HALO, CONV_CH), lambda i: (jnp.minimum((i + 1) * r, last_halo), 0)),
        const(w32.shape)]
    return _hosted_call(
        kern, grid=(nt,), in_specs=in_specs,
        out_specs=[pl.BlockSpec((T, 2 * CONV_CH), lambda i: (i, 0)), const(w32.shape), const((1, CONV_CH))],
        out_shape=[jax.ShapeDtypeStruct((S, 2 * CONV_CH), BF16), jax.ShapeDtypeStruct(w32.shape, F32),
                   jax.ShapeDtypeStruct((1, CONV_CH), F32)],
        scratch_shapes=[], args=[proj, proj, proj, proj, dc, dc, w32], name=name, comm=comm, sem=("arbitrary",))


def _iota2(shape, dim):
    return lax.broadcasted_iota(jnp.int32, shape, dim)


def _running(x, seg, later):
    n = x.shape[0]
    pos = _iota2(x.shape, 0) & (seg - 1)
    k = 1
    while k < seg:
        if later:
            x = x + jnp.where(pos < seg - k, pltpu.roll(x, n - k, axis=0), 0.0)
        else:
            x = x + jnp.where(pos >= k, pltpu.roll(x, k, axis=0), 0.0)
        k *= 2
    return x


@functools.partial(jax.custom_vjp, nondiff_argnums=(1,))
def _prefix(x, seg):
    return _running(x, seg, False)


_prefix.defvjp(lambda x, seg: (_running(x, seg, False), None), lambda seg, _, g: (_running(g, seg, True),))


def _hg_chunk(q, f, iv, g, st, lbk, ng):
    n, sub = HG_CHUNK, HG_SUB
    kk = lbk * jax.nn.sigmoid(-f)
    lf = jnp.log(1.0 - kk)
    b = _prefix(lf, n)
    bs = _prefix(lf, sub)
    bt = jnp.sum(lf, axis=0, keepdims=True)
    qh = q * jax.nn.sigmoid(q)
    dot_nt = lambda x, y: lax.dot_general(x.astype(BF16), y.astype(BF16), (((1,), (1,)), ((), ())),
                                          preferred_element_type=F32)
    o = dot_nt(qh * jnp.exp(b), st)
    b0 = b - bs
    qs = qh * jnp.exp(bs)
    col = _iota2((sub, n), 1)
    rows = []
    for blk in range(n // sub):
        lo = blk * sub
        sl = slice(lo, lo + sub)
        acc = o[sl]
        if blk > 0:
            ref = jnp.concatenate([b0[sl]] * (n // sub), axis=0)
            kd = kk * jnp.exp(jnp.minimum(ref - b, 0.0))
            sc = jnp.where(col < lo, dot_nt(qs[sl], kd), 0.0)
            acc = acc + jnp.dot(sc.astype(BF16), iv.astype(BF16), preferred_element_type=F32)
        for t0 in range(0, sub, HG_KEYS):
            keys, qrys = slice(lo + t0, lo + t0 + HG_KEYS), slice(lo + t0, lo + sub)
            nt = sub - t0
            bq, bk = bs[qrys][None, :, :], bs[keys][:, None, :]
            s_i = lax.broadcasted_iota(jnp.int32, (HG_KEYS, nt, HG_D), 0) + t0
            t_i = lax.broadcasted_iota(jnp.int32, (HG_KEYS, nt, HG_D), 1) + t0
            keep = s_i <= t_i
            p = jnp.where(keep, qh[qrys][None, :, :] * kk[keys][:, None, :] * jnp.exp(jnp.where(keep, bq - bk, 0.0)), 0.0)
            w = jnp.sum(p, axis=-1, keepdims=True)
            part = jnp.sum(w * iv[keys][:, None, :], axis=0)
            acc = acc + (part if t0 == 0 else jnp.concatenate([jnp.zeros((t0, HG_D), F32), part], axis=0))
        rows.append(acc)
    o = jnp.concatenate(rows, axis=0)
    kd = kk * jnp.exp(bt - b)
    st_new = jnp.exp(bt) * st + lax.dot_general(iv.astype(BF16), kd.astype(BF16), (((0,), (0,)), ((), ())),
                                                     preferred_element_type=F32)
    out = _rms(o, ng) * (g * jax.nn.sigmoid(g))
    return out, st_new


def _hg_tile(S):
    return min(512, S)


def _hg_in_specs(rt, rev, nr):
    width = HG_HEADS * HG_D
    base = OFF_HG // width
    row = (lambda r: nr - 1 - r) if rev else (lambda r: r)
    return [pl.BlockSpec((rt, width), functools.partial(lambda r, k: (row(r), base + k), k=k)) for k in range(4)]


def _hg_cols(h):
    return slice(h * HG_D, (h + 1) * HG_D)


def _hgrn_fwd(proj, lbk, ng, *, name, comm=None):
    S = proj.shape[0]
    rt = _hg_tile(S)
    nr, nc = S // rt, rt // HG_CHUNK

    def kern(q_ref, f_ref, i_ref, g_ref, lbk_ref, ng_ref, o_ref, st_out_ref, st_ref):
        @pl.when(pl.program_id(0) == 0)
        def _():
            st_ref[...] = jnp.zeros_like(st_ref)

        def body(c, carry):
            rows = pl.ds(pl.multiple_of(c * HG_CHUNK, HG_CHUNK), HG_CHUNK)
            for h in range(HG_HEADS):
                cols = _hg_cols(h)
                st = st_ref[h]
                st_out_ref[h, c] = st
                out, st_new = _hg_chunk(q_ref[rows, cols], f_ref[rows, cols], i_ref[rows, cols], g_ref[rows, cols], st,
                                        lbk_ref[:, cols], ng_ref[...])
                o_ref[rows, cols] = out.astype(o_ref.dtype)
                st_ref[h] = st_new
            return carry

        lax.fori_loop(0, nc, body, 0)

    width = HG_HEADS * HG_D
    in_specs = _hg_in_specs(rt, False, nr) + [pl.BlockSpec((1, width), lambda r: (0, 0)),
                                               pl.BlockSpec((1, HG_D), lambda r: (0, 0))]
    return _hosted_call(
        kern, grid=(nr,), in_specs=in_specs,
        out_specs=[pl.BlockSpec((rt, width), lambda r: (r, 0)),
                   pl.BlockSpec((HG_HEADS, nc, HG_D, HG_D), lambda r: (0, r, 0, 0))],
        out_shape=[jax.ShapeDtypeStruct((S, width), BF16),
                   jax.ShapeDtypeStruct((HG_HEADS, S // HG_CHUNK, HG_D, HG_D), F32)],
        scratch_shapes=[pltpu.VMEM((HG_HEADS, HG_D, HG_D), F32)],
        args=[proj, proj, proj, proj, lbk, ng], name=name, comm=comm, sem=("arbitrary",))


def _hgrn_bwd(proj, states, dout, lbk, ng, *, name, comm=None):
    S = proj.shape[0]
    rt = _hg_tile(S)
    nr, nc = S // rt, rt // HG_CHUNK
    width = HG_HEADS * HG_D

    def kern(q_ref, f_ref, i_ref, g_ref, st_in_ref, do_ref, lbk_ref, ng_ref,
             dq_ref, df_ref, di_ref, dg_ref, dlbk_ref, dng_ref, dst_ref):
        @pl.when(pl.program_id(0) == 0)
        def _():
            dst_ref[...] = jnp.zeros_like(dst_ref)
            dlbk_ref[...] = jnp.zeros_like(dlbk_ref)
            dng_ref[...] = jnp.zeros_like(dng_ref)

        def body(k, carry):
            c = nc - 1 - k
            rows = pl.ds(pl.multiple_of(c * HG_CHUNK, HG_CHUNK), HG_CHUNK)
            for h in range(HG_HEADS):
                cols = _hg_cols(h)
                _, vjp = jax.vjp(_hg_chunk, q_ref[rows, cols], f_ref[rows, cols], i_ref[rows, cols], g_ref[rows, cols],
                                 st_in_ref[h, c], lbk_ref[:, cols], ng_ref[...])
                dq, df, di, dg, dst, dlbk, dng = vjp((do_ref[rows, cols].astype(F32), dst_ref[h]))
                dq_ref[rows, cols] = dq.astype(BF16)
                df_ref[rows, cols] = df.astype(BF16)
                di_ref[rows, cols] = di.astype(BF16)
                dg_ref[rows, cols] = dg.astype(BF16)
                dst_ref[h] = dst
                dlbk_ref[:, cols] += dlbk
                dng_ref[h] += dng
            return carry

        lax.fori_loop(0, nc, body, 0)

    rev = lambda r: nr - 1 - r
    tile = pl.BlockSpec((rt, width), lambda r: (rev(r), 0))
    in_specs = _hg_in_specs(rt, True, nr) + [
        pl.BlockSpec((HG_HEADS, nc, HG_D, HG_D), lambda r: (0, rev(r), 0, 0)), tile,
        pl.BlockSpec((1, width), lambda r: (0, 0)), pl.BlockSpec((1, HG_D), lambda r: (0, 0))]
    return _hosted_call(
        kern, grid=(nr,), in_specs=in_specs,
        out_specs=[tile, tile, tile, tile, pl.BlockSpec((1, width), lambda r: (0, 0)),
                   pl.BlockSpec((HG_HEADS, 1, HG_D), lambda r: (0, 0, 0))],
        out_shape=[jax.ShapeDtypeStruct((S, width), BF16)] * 4 + [
            jax.ShapeDtypeStruct((1, width), F32), jax.ShapeDtypeStruct((HG_HEADS, 1, HG_D), F32)],
        scratch_shapes=[pltpu.VMEM((HG_HEADS, HG_D, HG_D), F32)],
        args=[proj, proj, proj, proj, states, dout, lbk, ng], name=name, comm=comm, sem=("arbitrary",))


def _sb_scores(km, qi):
    return lax.dot_general(km, qi, (((1,), (1,)), ((), ())), preferred_element_type=F32)


def _sb_weights(zt, r_run, diag):
    n = SB_BLK
    sp = jnp.maximum(zt, 0.0) + jnp.log(1.0 + jnp.exp(-jnp.abs(zt)))
    lk = -sp
    if diag:
        keep = (_iota2(zt.shape, 0) & (n - 1)) < _iota2(zt.shape, 1)
        lk = jnp.where(keep, lk, 0.0)
    tails = [_running(lk[a * n:(a + 1) * n], n, True) for a in range(2)]
    between = jnp.concatenate([tails[a] + r_run[a] for a in range(2)], axis=0)
    wgt = jnp.exp(zt + between)
    if diag:
        wgt = jnp.where(keep, wgt, 0.0)
    return sp, wgt, [t[0:1, :] for t in tails]


def _sb_norm_pair(x, g2, lane_lo):
    sq = x * x
    ms_lo = jnp.sum(jnp.where(lane_lo, sq, 0.0), axis=-1, keepdims=True)
    ms_hi = jnp.sum(jnp.where(lane_lo, 0.0, sq), axis=-1, keepdims=True)
    return x * lax.rsqrt(jnp.where(lane_lo, ms_lo, ms_hi) * (1.0 / SB_DH) + EPS) * g2


def _sb_specs(S):
    base = OFF_SB // SB_PAIR
    per = SB_HEADS * SB_DH // SB_PAIR
    cols = [pl.BlockSpec((S, SB_PAIR), functools.partial(lambda p, k: (0, base + per * k + p), k=k)) for k in range(3)]
    return cols + [pl.BlockSpec((1, SB_PAIR), lambda p: (0, 0))] * 2


def _sb_rows(i):
    return pl.ds(pl.multiple_of(i * SB_BLK, SB_BLK), SB_BLK)


def _sb_both(j, a=None):
    if a is None:
        return pl.ds(pl.multiple_of(j * 2 * SB_BLK, 2 * SB_BLK), 2 * SB_BLK)
    return pl.ds(pl.multiple_of(j * 2 * SB_BLK + a * SB_BLK, SB_BLK), SB_BLK)


def _sb_fwd(proj, qg2, kg2, *, name, comm=None):
    S = proj.shape[0]
    nb = S // SB_BLK
    scale = SB_DH ** -0.5
    n_pairs = SB_HEADS * SB_DH // SB_PAIR

    def kern(q_ref, k_ref, v_ref, qg_ref, kg_ref, o_ref, rs_ref, qp_ref, km_ref, vt_ref):
        lane_lo = _iota2((SB_BLK, SB_PAIR), 1) < SB_DH

        def prologue(j, carry):
            rows = _sb_rows(j)
            qp_ref[rows, :] = (_sb_norm_pair(q_ref[rows, :], qg_ref[...], lane_lo) * scale).astype(BF16)
            kn = _sb_norm_pair(k_ref[rows, :], kg_ref[...], lane_lo)
            v = v_ref[rows, :]
            for a, mine in enumerate((lane_lo, jnp.logical_not(lane_lo))):
                km_ref[_sb_both(j, a), :] = jnp.where(mine, kn, 0.0).astype(BF16)
                vt_ref[:, _sb_both(j, a)] = jnp.where(mine, v, 0.0).T.astype(BF16)
            return carry

        lax.fori_loop(0, nb, prologue, 0)

        diagonal = lambda i: _sb_scores(km_ref[_sb_both(i), :], qp_ref[_sb_rows(i), :])

        def qblock(i, zt):
            qi = qp_ref[_sb_rows(i), :]

            scores = lambda j: _sb_scores(km_ref[_sb_both(jnp.maximum(j, 0)), :], qi)
            output = lambda j, wgt: jnp.dot(vt_ref[:, _sb_both(j)], wgt, preferred_element_type=F32)

            def note(j, r_run):
                for a in range(2):
                    rs_ref[a, i, pl.ds(j, 1), :] = r_run[a]
                return jnp.maximum(jnp.max(r_run[0]), jnp.max(r_run[1])) > SB_SKIP

            def noted(j, r_run):
                return lax.cond(j >= 0, lambda: note(j, r_run).astype(jnp.int32), lambda: jnp.int32(0))

            zero = jnp.zeros((1, SB_BLK), F32)
            z_next = scores(i - 1)
            _, wgt, r_run = _sb_weights(zt, [zero, zero], True)
            go = noted(i - 1, r_run)

            def body(c):
                j, _, acc, r_run, zt, j_prev, w_prev = c
                z_next = scores(j - 1)
                acc = acc + output(j_prev, w_prev)
                _, wgt, lk_sum = _sb_weights(zt, r_run, False)
                r_run = [r_run[a] + lk_sum[a] for a in range(2)]
                return j - 1, noted(j - 1, r_run), acc, r_run, z_next, j, wgt.astype(BF16)

            c = (i - 1, go, jnp.zeros((SB_PAIR, SB_BLK), F32), r_run, z_next, i, wgt.astype(BF16))
            _, _, acc, _, _, j_prev, w_prev = lax.while_loop(lambda c: c[1] > 0, body, c)
            rs_ref[0, i, pl.ds(i, 1), :] = jnp.full((1, SB_BLK), j_prev, jnp.int32).astype(F32)
            zt = diagonal(jnp.minimum(i + 1, nb - 1))
            o_ref[_sb_rows(i), :] = (acc + output(j_prev, w_prev)).T.astype(o_ref.dtype)
            return zt

        lax.fori_loop(0, nb, qblock, diagonal(0))

    width = SB_HEADS * SB_DH
    return _hosted_call(
        kern, grid=(n_pairs,), in_specs=_sb_specs(S),
        out_specs=[pl.BlockSpec((S, SB_PAIR), lambda p: (0, p)),
                   pl.BlockSpec((2, nb, nb, SB_BLK), lambda p: (p, 0, 0, 0))],
        out_shape=[jax.ShapeDtypeStruct((S, width), BF16), jax.ShapeDtypeStruct((SB_HEADS, nb, nb, SB_BLK), F32)],
        scratch_shapes=[pltpu.VMEM((S, SB_PAIR), BF16), pltpu.VMEM((2 * S, SB_PAIR), BF16),
                        pltpu.VMEM((SB_PAIR, 2 * S), BF16)],
        args=[proj, proj, proj, qg2, kg2], name=name, comm=comm, sem=("parallel",))


def _sb_bwd(proj, qg2, kg2, rs, do, *, name, comm=None):
    S = proj.shape[0]
    nb = S // SB_BLK
    scale = SB_DH ** -0.5
    n_pairs = SB_HEADS * SB_DH // SB_PAIR

    def kern(q_ref, k_ref, v_ref, qg_ref, kg_ref, rs_ref, do_ref, dq_ref, dk_ref, dv_ref, dqg_ref, dkg_ref,
             qp_ref, km_ref, kt_ref, vm_ref, dqn_ref, dkn_ref, dvs_ref):
        lane_lo = _iota2((SB_BLK, SB_PAIR), 1) < SB_DH
        heads = (lane_lo, jnp.logical_not(lane_lo))
        fn_q = lambda x, g: _sb_norm_pair(x, g, lane_lo) * scale
        fn_k = lambda x, g: _sb_norm_pair(x, g, lane_lo)

        def prologue(j, carry):
            rows = _sb_rows(j)
            qp_ref[rows, :] = fn_q(q_ref[rows, :], qg_ref[...]).astype(BF16)
            kn = fn_k(k_ref[rows, :], kg_ref[...])
            v = v_ref[rows, :]
            for a, mine in enumerate(heads):
                k_a = jnp.where(mine, kn, 0.0)
                km_ref[_sb_both(j, a), :] = k_a.astype(BF16)
                kt_ref[:, _sb_both(j, a)] = k_a.T.astype(BF16)
                vm_ref[_sb_both(j, a), :] = jnp.where(mine, v, 0.0).astype(BF16)
            return carry

        lax.fori_loop(0, nb, prologue, 0)
        dkn_ref[...] = jnp.zeros_like(dkn_ref)
        dvs_ref[...] = jnp.zeros_like(dvs_ref)

        def leftmost(i):
            return jnp.clip(jnp.max(rs_ref[0, i, pl.ds(i, 1), :]).astype(jnp.int32), 0, i)

        def opening_of(i, j):
            jc = jnp.minimum(j, i)
            return (_sb_scores(km_ref[_sb_both(jc), :], qp_ref[_sb_rows(i), :]),
                    lax.dot_general(vm_ref[_sb_both(jc), :], do_ref[_sb_rows(i), :], (((1,), (1,)), ((), ())),
                                    preferred_element_type=F32))

        def qblock(i, carry):
            first, zt, dp = carry
            qi = qp_ref[_sb_rows(i), :]
            doi = do_ref[_sb_rows(i), :]

            opening = functools.partial(opening_of, i)

            def closing(j, dzb, wgtb, dqa):
                dkn_ref[_sb_both(j), :] += jnp.dot(dzb, qi, preferred_element_type=F32)
                dvs_ref[_sb_both(j), :] += jnp.dot(wgtb, doi, preferred_element_type=F32)
                return dqa + jnp.dot(kt_ref[:, _sb_both(j)], dzb, preferred_element_type=F32)

            def middle(j, diag, zt, dp, e_run):
                zero = jnp.zeros((1, SB_BLK), F32)
                r_run = [zero, zero] if diag else [rs_ref[a, i, pl.ds(j, 1), :] for a in range(2)]
                sp, wgt, _ = _sb_weights(zt, r_run, diag)
                e = dp * wgt
                heads_e = [_running(e[a * SB_BLK:(a + 1) * SB_BLK], SB_BLK, False) for a in range(2)]
                e_left = jnp.concatenate([heads_e[a] + e_run[a] for a in range(2)], axis=0) - e
                s_neg = jnp.exp(-sp)
                dz = e * s_neg - e_left * (1.0 - s_neg)
                if diag:
                    dz = jnp.where((_iota2(dz.shape, 0) & (SB_BLK - 1)) < _iota2(dz.shape, 1), dz, 0.0)
                return dz.astype(BF16), wgt.astype(BF16), [e_run[a] + heads_e[a][SB_BLK - 1:SB_BLK, :] for a in range(2)]

            def body(j, c):
                dqa, e_run, zt, dp, j_prev, dzb, wgtb = c
                nxt = opening(j + 1)
                dqa = closing(j_prev, dzb, wgtb, dqa)
                dzb, wgtb, e_run = middle(j, False, zt, dp, e_run)
                return (dqa, e_run) + nxt + (j, dzb, wgtb)

            zero = jnp.zeros((1, SB_BLK), F32)
            none = jnp.zeros((2 * SB_BLK, SB_BLK), BF16)
            c = (jnp.zeros((SB_PAIR, SB_BLK), F32), [zero, zero], zt, dp, first, none, none)
            dqa, e_run, zt, dp, j_prev, dzb, wgtb = lax.fori_loop(first, i, body, c)
            dqa = closing(j_prev, dzb, wgtb, dqa)
            dzb, wgtb, _ = middle(i, True, zt, dp, e_run)
            i_next = jnp.minimum(i + 1, nb - 1)
            first_next = leftmost(i_next)
            nxt = opening_of(i_next, first_next)
            dqn_ref[_sb_rows(i), :] = closing(i, dzb, wgtb, dqa).T
            return (first_next,) + nxt

        lax.fori_loop(0, nb, qblock, (leftmost(0),) + opening_of(0, 0))
        dqg_ref[...] = jnp.zeros_like(dqg_ref)
        dkg_ref[...] = jnp.zeros_like(dkg_ref)

        def epilogue(j, carry):
            rows = _sb_rows(j)
            _, vjp_q = jax.vjp(fn_q, q_ref[rows, :], qg_ref[...])
            dq, dqg = vjp_q(dqn_ref[rows, :])
            _, vjp_k = jax.vjp(fn_k, k_ref[rows, :], kg_ref[...])
            dk, dkg = vjp_k(jnp.where(lane_lo, dkn_ref[_sb_both(j, 0), :], dkn_ref[_sb_both(j, 1), :]))
            dq_ref[rows, :] = dq.astype(BF16)
            dk_ref[rows, :] = dk.astype(BF16)
            dv_ref[rows, :] = jnp.where(lane_lo, dvs_ref[_sb_both(j, 0), :], dvs_ref[_sb_both(j, 1), :]).astype(BF16)
            dqg_ref[0] += dqg
            dkg_ref[0] += dkg
            return carry

        lax.fori_loop(0, nb, epilogue, 0)

    width = SB_HEADS * SB_DH
    pair = pl.BlockSpec((S, SB_PAIR), lambda p: (0, p))
    dgain = pl.BlockSpec((1, 1, SB_PAIR), lambda p: (p, 0, 0))
    in_specs = _sb_specs(S) + [pl.BlockSpec((2, nb, nb, SB_BLK), lambda p: (p, 0, 0, 0)), pair]
    return _hosted_call(
        kern, grid=(n_pairs,), in_specs=in_specs, out_specs=[pair, pair, pair, dgain, dgain],
        out_shape=[jax.ShapeDtypeStruct((S, width), BF16)] * 3 + [jax.ShapeDtypeStruct((n_pairs, 1, SB_PAIR), F32)] * 2,
        scratch_shapes=[pltpu.VMEM((S, SB_PAIR), BF16), pltpu.VMEM((2 * S, SB_PAIR), BF16), pltpu.VMEM((SB_PAIR, 2 * S), BF16),
                        pltpu.VMEM((2 * S, SB_PAIR), BF16), pltpu.VMEM((S, SB_PAIR), F32),
                        pltpu.VMEM((2 * S, SB_PAIR), F32), pltpu.VMEM((2 * S, SB_PAIR), F32)],
        args=[proj, proj, proj, qg2, kg2, rs, do], name=name, comm=comm, sem=("parallel",))


def _loss_head(y, target, *, name):
    S, D = y.shape
    T = min(ROW_TILE, S)

    def kern(y_ref, t_ref, dy_ref, acc_ref):
        err = y_ref[...] - t_ref[...]
        dy_ref[...] = err * (1.0 / D)
        col = jnp.sum(err * err, axis=0, keepdims=True)
        part = sum(col[:, k * 128:(k + 1) * 128] for k in range(D // 128))

        @pl.when(pl.program_id(0) == 0)
        def _():
            acc_ref[...] = part

        @pl.when(pl.program_id(0) > 0)
        def _():
            acc_ref[...] += part

    tile = pl.BlockSpec((T, D), lambda i: (i, 0))
    return pl.pallas_call(
        kern, grid=(S // T,), in_specs=[tile, tile], out_specs=[tile, pl.BlockSpec((1, 128), lambda i: (0, 0))],
        out_shape=[jax.ShapeDtypeStruct((S, D), F32), jax.ShapeDtypeStruct((1, 128), F32)],
        name=name, compiler_params=_cparams(("arbitrary",)))(y, target)


def _adamw_math(w, g, m, v):
    m = ADAM_B1 * m + (1.0 - ADAM_B1) * g
    v = ADAM_B2 * v + (1.0 - ADAM_B2) * jnp.square(g)
    m_hat = m / (1.0 - ADAM_B1 ** ADAM_STEP)
    v_hat = v / (1.0 - ADAM_B2 ** ADAM_STEP)
    return -ADAM_LR * (m_hat / (jnp.sqrt(v_hat) + ADAM_EPS) + ADAM_WD * w), m, v


def _adamw(w, g, m, v, *, name):
    R, C = w.shape
    T = _pick(R, (256, 128, 64, 32, 16, 8))

    def kern(w_ref, g_ref, m_ref, v_ref, d_ref, mo_ref, vo_ref):
        d, mn, vn = _adamw_math(w_ref[...], g_ref[...], m_ref[...], v_ref[...])
        d_ref[...] = d
        mo_ref[...] = mn
        vo_ref[...] = vn

    tile = pl.BlockSpec((T, C), lambda i: (i, 0))
    return pl.pallas_call(
        kern, grid=(R // T,), in_specs=[tile] * 4, out_specs=[tile] * 3,
        out_shape=[jax.ShapeDtypeStruct((R, C), F32)] * 3, name=name,
        compiler_params=_cparams(("parallel",)))(w, g, m, v)


def _adamw_layer(w, g, m, v, layer, prev, *, name, comm=None):
    L, R, C = w.shape
    T = _pick(R, (256, 128, 64, 32, 16, 8))

    def kern(w_ref, g_ref, m_ref, v_ref, *rest):
        go_ref, d_ref, mo_ref, vo_ref = rest[-4:]
        grad = g_ref[...]
        d, mn, vn = _adamw_math(w_ref[...], grad, m_ref[...], v_ref[...])
        go_ref[...] = grad
        d_ref[...] = d
        mo_ref[...] = mn
        vo_ref[...] = vn

    layer_tile = pl.BlockSpec((None, T, C), lambda i: (layer, i, 0))
    in_specs = [layer_tile, pl.BlockSpec((T, C), lambda i: (i, 0)), layer_tile, layer_tile]
    args, aliases = [w, g, m, v], {}
    if prev is not None:
        in_specs += [pl.BlockSpec(memory_space=pl.ANY)] * 4
        args += list(prev)
        aliases = {4 + k: k for k in range(4)}
    if comm is not None:
        assert prev is None
        return _hosted_call(kern, grid=(R // T,), in_specs=in_specs, out_specs=[layer_tile] * 4,
                            out_shape=[jax.ShapeDtypeStruct((L, R, C), F32)] * 4, scratch_shapes=[], args=args,
                            name=name, comm=comm)
    return pl.pallas_call(
        kern, grid=(R // T,), in_specs=in_specs, out_specs=[layer_tile] * 4,
        out_shape=[jax.ShapeDtypeStruct((L, R, C), F32)] * 4, input_output_aliases=aliases, name=name,
        compiler_params=_cparams(("parallel",)))(*args)


def _sum8(g, *, name):
    def kern(g_ref, o_ref):
        acc = g_ref[0]
        for d in range(1, g.shape[0]):
            acc = acc + g_ref[d]
        o_ref[...] = acc

    return pl.pallas_call(kern, out_shape=jax.ShapeDtypeStruct(g.shape[1:], F32), name=name,
                          compiler_params=_cparams())(g)


def _place():
    return lax.axis_index("x"), lax.axis_index("y"), lax.axis_index("c")


def _other_chips(x, y):
    return [(1 - x, y), (x, 1 - y), (1 - x, 1 - y)]


def _remote(src, dst, send_sems, recv_sems, k, to):
    return pltpu.make_async_remote_copy(src_ref=src, dst_ref=dst, send_sem=send_sems.at[k], recv_sem=recv_sems.at[k],
                                        device_id=to, device_id_type=MESH)


def _all_gather_small(v, *, name):
    def body(x_ref, out_ref, send_sems, recv_sems, local_sem):
        x, y, c = _place()
        me = 4 * x + 2 * y + c
        mine = pltpu.make_async_copy(x_ref, out_ref.at[me], local_sem)
        mine.start()
        peers = []
        for f in range(1, 8):
            peers.append((1 - x if f & 4 else x, 1 - y if f & 2 else y, 1 - c if f & 1 else c))
        sends = [_remote(x_ref, out_ref.at[me], send_sems, recv_sems, k, p) for k, p in enumerate(peers)]
        for cp in sends:
            cp.start()
        for k, (px, py, pc) in enumerate(peers):
            _remote(x_ref, out_ref.at[4 * px + 2 * py + pc], send_sems, recv_sems, k, (px, py, pc)).wait_recv()
        for cp in sends:
            cp.wait_send()
        mine.wait()

    return pl.pallas_call(
        body, out_shape=jax.ShapeDtypeStruct((8,) + v.shape, v.dtype),
        in_specs=[pl.BlockSpec(memory_space=pltpu.VMEM)], out_specs=pl.BlockSpec(memory_space=pltpu.VMEM),
        scratch_shapes=[pltpu.SemaphoreType.DMA((7,)), pltpu.SemaphoreType.DMA((7,)), pltpu.SemaphoreType.DMA],
        name=name, compiler_params=_cparams())(v)


def _hosted_call(kern, *, grid, in_specs, out_specs, out_shape, scratch_shapes, args, name, comm=None, sem=None):
    if comm is None:
        res = pl.pallas_call(kern, grid=grid, in_specs=in_specs, out_specs=out_specs, out_shape=out_shape,
                             scratch_shapes=scratch_shapes, name=name, compiler_params=_cparams(sem))(*args)
        return list(res), []
    n_in, n_out, n_scr = len(in_specs), len(out_specs), len(scratch_shapes)
    c_in, c_out = len(comm.inputs), len(comm.out_shapes)

    def body(*refs):
        ins, ci = refs[:n_in], refs[n_in:n_in + c_in]
        outs = refs[n_in + c_in:n_in + c_in + n_out]
        co = refs[n_in + c_in + n_out:n_in + c_in + n_out + c_out]
        scr = refs[n_in + c_in + n_out + c_out:n_in + c_in + n_out + c_out + n_scr]
        cs = refs[n_in + c_in + n_out + c_out + n_scr:]
        ids = [pl.program_id(d) for d in range(len(grid))]
        inner_first = functools.reduce(jnp.logical_and, [i == 0 for i in ids[1:]], True)
        inner_last = functools.reduce(jnp.logical_and, [i == n - 1 for i, n in zip(ids[1:], grid[1:])], True)

        @pl.when(jnp.logical_and(ids[0] == 0, inner_first))
        def _():
            comm.begin(ci, co, cs)

        kern(*ins, *outs, *scr)

        @pl.when(jnp.logical_and(ids[0] == grid[0] // 2, inner_last))
        def _():
            comm.middle(ci, co, cs)

        @pl.when(jnp.logical_and(ids[0] == grid[0] - 1, inner_last))
        def _():
            comm.end(ci, co, cs)

    hbm = pl.BlockSpec(memory_space=pltpu.HBM)
    res = pl.pallas_call(
        body, grid=grid, in_specs=list(in_specs) + [hbm] * c_in, out_specs=list(out_specs) + [hbm] * c_out,
        out_shape=list(out_shape) + list(comm.out_shapes), scratch_shapes=list(scratch_shapes) + list(comm.scratch),
        input_output_aliases={n_in + i: n_out + o for i, o in comm.aliases.items()},
        name=name, compiler_params=_cparams(("arbitrary",) * len(grid)))(*args, *comm.inputs)
    return list(res[:n_out]), list(res[n_out:])


def _run_comm(comm, *, name):
    return _hosted_call(lambda: None, grid=(1,), in_specs=[], out_specs=[], out_shape=[], scratch_shapes=[], args=[],
                        name=name, comm=comm)[1]


class _Gather:
    def __init__(self, shards, kinds, items):
        used = sorted({w for w, _ in items})
        self.slot = {w: k for k, w in enumerate(used)}
        self.inputs = [shards[w] for w in used]
        self.items, self.kinds = list(items), kinds
        self.shapes = {w: shards[w].shape[1:] for w in used}
        self.out_shapes = [jax.ShapeDtypeStruct((r, 4 * n) if kinds[w] == "col" else (4 * r, n), shards[w].dtype)
                           for w, _ in items for r, n in [self.shapes[w]]]
        n_items = len(items)
        self.scratch = [pltpu.SemaphoreType.DMA((6 * n_items,)), pltpu.SemaphoreType.DMA((6 * n_items,)),
                        pltpu.SemaphoreType.DMA((n_items,))]
        self.aliases = {}

    def _piece(self, ref, w, qq, half):
        r, n = self.shapes[w]
        h = r // 2
        lo, size = (0, r) if half is None else (half * h, h)
        if self.kinds[w] == "col":
            return ref.at[pl.ds(pl.multiple_of(lo, 16), size), pl.ds(pl.multiple_of(qq * n, 128), n)]
        return ref.at[pl.ds(pl.multiple_of(qq * r + lo, 16), size), :]

    def _mine(self, ci, w, l, half):
        h = self.shapes[w][0] // 2
        return ci[self.slot[w]].at[l, pl.ds(pl.multiple_of(half * h, 16), h), :]

    def begin(self, ci, co, cs):
        send_sems, recv_sems, local_sems = cs
        x, y, c = _place()
        q = 2 * x + y
        for k, (w, l) in enumerate(self.items):
            pltpu.make_async_copy(ci[self.slot[w]].at[l], self._piece(co[k], w, q, None), local_sems.at[k]).start()
            for j, (cx, cy) in enumerate(_other_chips(x, y)):
                _remote(self._mine(ci, w, l, c), self._piece(co[k], w, q, c), send_sems, recv_sems, 6 * k + j,
                        (cx, cy, c)).start()

    def middle(self, ci, co, cs):
        send_sems, recv_sems, _ = cs
        x, y, c = _place()
        for k, (w, l) in enumerate(self.items):
            for j, (cx, cy) in enumerate(_other_chips(x, y)):
                win = self._piece(co[k], w, 2 * cx + cy, c)
                _remote(win, win, send_sems, recv_sems, 6 * k + j, (cx, cy, c)).wait_recv()
                _remote(win, win, send_sems, recv_sems, 6 * k + 3 + j, (x, y, 1 - c)).start()

    def end(self, ci, co, cs):
        send_sems, recv_sems, local_sems = cs
        x, y, c = _place()
        q = 2 * x + y
        for k, (w, l) in enumerate(self.items):
            for j, (cx, cy) in enumerate(_other_chips(x, y)):
                win = self._piece(co[k], w, 2 * cx + cy, 1 - c)
                _remote(win, win, send_sems, recv_sems, 6 * k + 3 + j, (x, y, 1 - c)).wait_recv()
        for k, (w, l) in enumerate(self.items):
            for j, (cx, cy) in enumerate(_other_chips(x, y)):
                _remote(self._mine(ci, w, l, c), self._piece(co[k], w, q, c), send_sems, recv_sems, 6 * k + j,
                        (cx, cy, c)).wait_send()
                win = self._piece(co[k], w, 2 * cx + cy, c)
                _remote(win, win, send_sems, recv_sems, 6 * k + 3 + j, (x, y, 1 - c)).wait_send()
            pltpu.make_async_copy(ci[self.slot[w]].at[l], self._piece(co[k], w, q, None), local_sems.at[k]).wait()


def _half_rows(ref, half, h):
    return ref.at[:, pl.ds(pl.multiple_of(half * h, 16), h), :]


class _Copies:
    def __init__(self, inputs, out_shapes, count, pairs, aliases=None, lands=None):
        self.inputs, self.out_shapes, self.pairs, self.lands = list(inputs), list(out_shapes), pairs, lands
        self.scratch = [pltpu.SemaphoreType.DMA((count,)), pltpu.SemaphoreType.DMA((count,))]
        self.aliases = aliases or {}

    def _copies(self, ci, co, cs):
        x, y, c = _place()
        return [_remote(src, dst, cs[0], cs[1], k, to) for k, (src, dst, to) in enumerate(self.pairs(ci, co, x, y, c))]

    def begin(self, ci, co, cs):
        for cp in self._copies(ci, co, cs):
            cp.start()

    def middle(self, ci, co, cs):
        pass

    def end(self, ci, co, cs):
        x, y, c = _place()
        for k, (src, dst, to) in enumerate(self.pairs(ci, co, x, y, c)):
            _remote(src, dst, cs[0], cs[1], k, to).wait_send()
            arrival = dst if self.lands is None else self.lands(co, x, y, c)[k]
            _remote(src, arrival, cs[0], cs[1], k, to).wait_recv()


def _send_to_all(v):
    def peers(x, y, c):
        return [(1 - x if f & 4 else x, 1 - y if f & 2 else y, 1 - c if f & 1 else c) for f in range(1, 8)]

    def pairs(ci, co, x, y, c):
        return [(ci[0], co[0].at[4 * x + 2 * y + c], peer) for peer in peers(x, y, c)]

    def lands(co, x, y, c):
        return [co[0].at[4 * px + 2 * py + pc] for px, py, pc in peers(x, y, c)]

    return _Copies([v], [jax.ShapeDtypeStruct((8,) + v.shape, v.dtype)], 7, pairs, lands=lands)


def _swap_halves(gs):
    def pairs(ci, co, x, y, c):
        return [(_half_rows(ci[k], 1 - c, g.shape[1] // 2), co[k], (x, y, 1 - c)) for k, g in enumerate(gs)]

    return _Copies(gs, [jax.ShapeDtypeStruct((g.shape[0], g.shape[1] // 2, g.shape[2]), g.dtype) for g in gs],
                   len(gs), pairs)


def _scatter_quarters(ps, kinds):
    part = [((p.shape[1], p.shape[2] // 4) if kind == "col" else (p.shape[1], p.shape[2])) for p, kind in zip(ps, kinds)]

    def pairs(ci, co, x, y, c):
        out = []
        for k, kind in enumerate(kinds):
            n = part[k][1]
            for j, (cx, cy) in enumerate(_other_chips(x, y)):
                qj = 2 * cx + cy
                src = ci[k].at[0, :, pl.ds(pl.multiple_of(qj * n, 128), n)] if kind == "col" else ci[k].at[qj]
                out.append((src, co[k].at[j], (cx, cy, c)))
        return out

    return _Copies(ps, [jax.ShapeDtypeStruct((3,) + pt, p.dtype) for pt, p in zip(part, ps)], 3 * len(ps), pairs)


def _share_halves(gs):
    def rows(co, k, half):
        h = gs[k].shape[0] // 2
        return co[k].at[pl.ds(pl.multiple_of(half * h, 16), h), :]

    def pairs(ci, co, x, y, c):
        return [(rows(co, k, c), rows(co, k, c), (x, y, 1 - c)) for k in range(len(gs))]

    def lands(co, x, y, c):
        return [rows(co, k, 1 - c) for k in range(len(gs))]

    return _Copies(gs, [jax.ShapeDtypeStruct(g.shape, g.dtype) for g in gs], len(gs), pairs,
                   aliases={k: k for k in range(len(gs))}, lands=lands)


def _wide_tile(n):
    return _pick(n, (2048, 1920, 1024, 512, 256, 128))


def _pair_sum(g, land, place, *, name):
    B, R, N = g.shape
    h = R // 2
    tr, tc = _pick(h, (256, 128)), _wide_tile(N)

    def kern(place_ref, g_ref, l_ref, o_ref):
        o_ref[...] = (g_ref[...] + l_ref[...]).astype(o_ref.dtype)

    grid_spec = pltpu.PrefetchScalarGridSpec(
        num_scalar_prefetch=1, grid=(B, h // tr, N // tc),
        in_specs=[pl.BlockSpec((None, tr, tc), lambda b, i, j, p: (b, p[1] * (h // tr) + i, j)),
                  pl.BlockSpec((None, tr, tc), lambda b, i, j, p: (b, i, j))],
        out_specs=pl.BlockSpec((None, tr, tc), lambda b, i, j, p: (b, i, j)))
    return pl.pallas_call(kern, grid_spec=grid_spec, out_shape=jax.ShapeDtypeStruct((B, h, N), BF16), name=name,
                          compiler_params=_cparams(("parallel", "parallel", "parallel")))(place, g, land)


def _quarter_sum(p, land, kind, shard_shape, place, *, name):
    L, r, n = shard_shape
    h = r // 2
    tr, tc = _pick(h, (256, 128)), _wide_tile(n)

    def kern(place_ref, p_ref, a_ref, b_ref, c_ref, o_ref):
        o_ref[...] = ((p_ref[...].astype(F32) + a_ref[...].astype(F32)) + b_ref[...].astype(F32)) + c_ref[...].astype(F32)

    if kind == "col":
        p_spec = pl.BlockSpec((None, tr, tc), lambda l, i, j, pr: (l, i, pr[0] * (n // tc) + j))
    else:
        p_spec = pl.BlockSpec((None, None, tr, tc), lambda l, i, j, pr: (l, pr[0], i, j))
    lands = [pl.BlockSpec((None, None, tr, tc), functools.partial(lambda l, i, j, pr, s: (s, l, i, j), s=s))
             for s in range(3)]
    grid_spec = pltpu.PrefetchScalarGridSpec(
        num_scalar_prefetch=1, grid=(L, h // tr, n // tc), in_specs=[p_spec] + lands,
        out_specs=pl.BlockSpec((None, tr, tc), lambda l, i, j, pr: (l, pr[1] * (h // tr) + i, j)))
    return pl.pallas_call(kern, grid_spec=grid_spec, out_shape=jax.ShapeDtypeStruct((L, r, n), F32), name=name,
                          compiler_params=_cparams(("parallel", "parallel", "parallel")))(place, p, land, land, land)


class _ReduceScatter:
    def __init__(self, grads, kinds, shard_shapes, place, tag):
        self.kinds, self.shapes, self.place, self.tag = kinds, shard_shapes, place, tag
        self.g3 = [g[None] if kind == "col" else g.reshape(4, g.shape[0] // 4, g.shape[1]) for g, kind in zip(grads, kinds)]

    def swap(self):
        return _swap_halves(self.g3)

    def pair_sums(self, lands):
        self.ps = [_pair_sum(g, land, self.place, name=f"rs_pair_sum_{self.tag}_{k}")
                   for k, (g, land) in enumerate(zip(self.g3, lands))]

    def scatter(self):
        return _scatter_quarters(self.ps, self.kinds)

    def quarter_sums(self, parts):
        self.halves = []
        for k, (p, part) in enumerate(zip(self.ps, parts)):
            p4 = p if self.kinds[k] == "col" else p[None]
            out = _quarter_sum(p4, part[:, None], self.kinds[k], (1,) + tuple(self.shapes[k]), self.place,
                               name=f"rs_quarter_sum_{self.tag}_{k}")
            self.halves.append(out[0])

    def share(self):
        return _share_halves(self.halves)

    def run(self):
        self.pair_sums(_run_comm(self.swap(), name=f"rs_swap_{self.tag}"))
        self.quarter_sums(_run_comm(self.scatter(), name=f"rs_scatter_{self.tag}"))
        return _run_comm(self.share(), name=f"rs_share_{self.tag}")


_WEIGHTS = ["mod_w", "mod_b", "norm1_g", "w_in", "gate_b", "conv_w", "conv_b", "conv_ln_g", "conv_ln_b", "w_conv_proj",
            "hgrn_lb", "hgrn_norm_g", "w_hgrn_proj", "sb_qn_g", "sb_kn_g", "w_sb_proj", "w_out", "norm2_g", "mlp_w1",
            "mlp_w2"]
_BIG = [("w_in", "col"), ("w_conv_proj", "col"), ("w_hgrn_proj", "col"), ("w_sb_proj", "col"), ("w_out", "row"),
        ("mlp_w1", "col"), ("mlp_w2", "row")]
_REPLICATED = ["mod_b", "norm1_g", "gate_b", "conv_b", "conv_ln_g", "conv_ln_b", "hgrn_lb", "hgrn_norm_g", "sb_qn_g",
               "sb_kn_g", "norm2_g"]
LANES = 128


class _Pack:
    def __init__(self, items):
        self.shapes = {n: a.shape for n, a in items}
        self.offsets, pos = {}, 0
        for n, a in items:
            self.offsets[n] = pos
            pos += math.prod(a.shape)
        self.rows = -(-pos // (8 * LANES)) * 8
        flat = jnp.concatenate([a.reshape(-1).astype(F32) for _, a in items])
        self.array = jnp.pad(flat, (0, self.rows * LANES - pos)).reshape(self.rows, LANES)

    def get(self, packed, name):
        lead = packed.shape[:-2]
        flat = packed.reshape(lead + (self.rows * LANES,))
        n = math.prod(self.shapes[name])
        return lax.slice_in_dim(flat, self.offsets[name], self.offsets[name] + n, axis=len(lead)).reshape(
            lead + self.shapes[name])


def _lower_bounds(hgrn_lb):
    p = jax.nn.softmax(hgrn_lb.astype(F32), axis=0)
    return jnp.cumsum(p, axis=0) - p[0:1]


def _layer_fwd(x, w, p, l, comms=(None, None)):
    S, D = x.shape
    r = {"x": x}
    (r["h"],) = _rowop(_fn_normmod, [(x, 0, D)], [p["n1g"], p["sc1"], p["sh1"]], [(D, BF16)], name=f"normmod1_fwd_{l}")
    proj = r["proj"] = _matmul(r["h"], w["w_in", l], name=f"w_in_fwd_{l}")
    r["cpre"] = _conv_fwd(proj, p["w32"], p["conv_b"], name=f"conv_fwd_{l}")
    (r["cact"],) = _rowop(_fn_lnsilu, [(r["cpre"], 0, CONV_CH)], [p["lng"], p["lnb"]], [(CONV_CH, BF16)],
                          name=f"conv_ln_fwd_{l}")
    arrived = lambda comm, got: w.update({(_BIG[k][0], layer): arr for (k, layer), arr in zip(comm.items, got)})
    (r["hg"], r["states"]), got = _hgrn_fwd(proj, p["lbk"], p["ng"], name=f"hgrn_fwd_{l}", comm=comms[0])
    if comms[0] is not None:
        arrived(comms[0], got)
    (r["sb"], r["rs"]), got = _sb_fwd(proj, p["qg"], p["kg"], name=f"sb_fwd_{l}", comm=comms[1])
    if comms[1] is not None:
        arrived(comms[1], got)
    r["y_c"] = _matmul(r["cact"], w["w_conv_proj", l], name=f"w_conv_proj_fwd_{l}")
    r["y_h"] = _matmul(r["hg"], w["w_hgrn_proj", l], name=f"w_hgrn_proj_fwd_{l}")
    r["y_s"] = _matmul(r["sb"], w["w_sb_proj", l], name=f"w_sb_proj_fwd_{l}")
    (r["merged"],) = _rowop(_fn_merge, [(proj, OFF_GL, 3 * D), (r["y_c"], 0, D), (r["y_h"], 0, D), (r["y_s"], 0, D)],
                            [p["gate_b"]], [(D, BF16)], name=f"merge_fwd_{l}")
    resid = lambda y, x_in, gate: (y,) + _fn_resid(x_in, y, gate)
    r["a_out"], r["x1"] = _matmul(r["merged"], w["w_out", l], name=f"w_out_fwd_{l}", post=resid, extras=[x],
                                  rows=[p["g1"]], out_dtypes=(F32, F32))
    (r["h2"],) = _rowop(_fn_normmod, [(r["x1"], 0, D)], [p["n2g"], p["sc2"], p["sh2"]], [(D, BF16)],
                        name=f"normmod2_fwd_{l}")
    r["u"], r["act"] = _matmul(r["h2"], w["mlp_w1", l], name=f"mlp_w1_fwd_{l}", post=lambda u: (u,) + _fn_relu2(u),
                               out_dtypes=(F32, BF16))
    r["m_out"], x2 = _matmul(r["act"], w["mlp_w2", l], name=f"mlp_w2_fwd_{l}", post=resid, extras=[r["x1"]],
                             rows=[p["g2"]], out_dtypes=(F32, F32))
    return x2, r


def _layer_bwd(dx2, r, w, p, l, grads, carry=None, last=None):
    S, D = dx2.shape
    small = {}

    def dweight(name, a, dy):
        grads[name, l] = _matmul(a, dy, ta=True, name=f"{name}_dw_{l}")

    stage = (lambda k, got: carry(k, got)) if carry is not None else (lambda k, got: None)

    (dm_out,), (dg2,) = _rowop_bwd(_fn_scale, [(r["m_out"], 0, D)], [p["g2"]], [dx2], [BF16], name=f"resid2_bwd_{l}")
    (du,) = _matmul(dm_out, w["mlp_w2", l], tb=True, name=f"mlp_w2_dx_{l}", extras=[r["u"]], out_dtypes=(BF16,),
                    post=lambda dact, u: (dact * (2.0 * jnp.maximum(u, 0.0)),))
    dweight("mlp_w2", r["act"], dm_out)
    dh2 = _matmul(du, w["mlp_w1", l], tb=True, name=f"mlp_w1_dx_{l}")
    dweight("mlp_w1", r["h2"], du)
    (dx1, da_out), (dg1, small["norm2_g"], dsc2, dsh2) = _rowop_bwd(
        _fn_resid_norm, [(r["x"], 0, D), (r["a_out"], 0, D)], [p["g1"], p["n2g"], p["sc2"], p["sh2"]], [dx2, dh2],
        [F32, BF16], name=f"resid1_norm2_bwd_{l}")
    dmerged = _matmul(da_out, w["w_out", l], tb=True, name=f"w_out_dx_{l}")
    dweight("w_out", r["merged"], da_out)
    (dgl, dy_c, dy_h, dy_s), (small["gate_b"],) = _rowop_bwd(
        _fn_merge, [(r["proj"], OFF_GL, 3 * D), (r["y_c"], 0, D), (r["y_h"], 0, D), (r["y_s"], 0, D)], [p["gate_b"]],
        [dmerged], [BF16] * 4, name=f"merge_bwd_{l}")
    dweight("w_conv_proj", r["cact"], dy_c)
    dweight("w_hgrn_proj", r["hg"], dy_h)
    dweight("w_sb_proj", r["sb"], dy_s)
    dcact = _matmul(dy_c, w["w_conv_proj", l], tb=True, name=f"w_conv_proj_dx_{l}")
    (dcpre,), (small["conv_ln_g"], small["conv_ln_b"]) = _rowop_bwd(
        _fn_lnsilu, [(r["cpre"], 0, CONV_CH)], [p["lng"], p["lnb"]], [dcact], [F32], name=f"conv_ln_bwd_{l}")
    (d_conv, dw32, small["conv_b"]), got = _conv_bwd(r["proj"], dcpre, p["w32"], name=f"conv_bwd_{l}",
                                                      comm=stage(0, None))
    small["conv_w"] = dw32[:CONV_WIDTH]
    dhg = _matmul(dy_h, w["w_hgrn_proj", l], tb=True, out_dtype=BF16, name=f"w_hgrn_proj_dx_{l}")
    (dq, df, di, dg, dlbk, dng), got = _hgrn_bwd(r["proj"], r["states"], dhg, p["lbk"], p["ng"], name=f"hgrn_bwd_{l}",
                                                 comm=stage(1, got))
    small["lower"] = -dlbk
    small["hgrn_norm_g"] = jnp.sum(dng, axis=0)
    dsb = _matmul(dy_s, w["w_sb_proj", l], tb=True, out_dtype=BF16, name=f"w_sb_proj_dx_{l}")
    (dsq, dsk, dsv, dqg, dkg), got = _sb_bwd(r["proj"], p["qg"], p["kg"], r["rs"], dsb, name=f"sb_bwd_{l}",
                                             comm=stage(2, got))
    stage(3, got)
    fold = lambda t: jnp.sum(t.reshape(-1, SB_DH), axis=0, keepdims=True)
    small["sb_qn_g"], small["sb_kn_g"] = fold(dqg), fold(dkg)
    dproj = jnp.concatenate([d_conv, dq, df, di, dg, dsq, dsk, dsv, dgl], axis=1)
    dweight("w_in", r["h"], dproj)
    norm1 = functools.partial(_rowop_bwd, _fn_normmod, [(r["x"], 0, D)], [p["n1g"], p["sc1"], p["sh1"]],
                              din_dtypes=[F32], add={0: dx1}, name=f"normmod1_bwd_{l}")
    if last is None:
        dh = _matmul(dproj, w["w_in", l], tb=True, name=f"w_in_dx_{l}")
        (dx,), (small["norm1_g"], dsc1, dsh1) = norm1(douts=[dh])
    else:
        dh, got = _matmul(dproj, w["w_in", l], tb=True, name=f"w_in_dx_{l}", comm=last(0, None))
        (dx,), (small["norm1_g"], dsc1, dsh1), got = norm1(douts=[dh], comm=last(1, got))
        last(2, got)
    small["mod"] = jnp.concatenate([dsh1, dsc1, dg1, dsh2, dsc2, dg2], axis=1)
    return dx, small


def kernel(x, c, mod_w, mod_b, norm1_g, w_in, gate_b, conv_w, conv_b, conv_ln_g, conv_ln_b, w_conv_proj, hgrn_lb, hgrn_norm_g, w_hgrn_proj, sb_qn_g, sb_kn_g, w_sb_proj, w_out, norm2_g, mlp_w1, mlp_w2, loss_target, m_mod_w, m_mod_b, m_norm1_g, m_w_in, m_gate_b, m_conv_w, m_conv_b, m_conv_ln_g, m_conv_ln_b, m_w_conv_proj, m_hgrn_lb, m_hgrn_norm_g, m_w_hgrn_proj, m_sb_qn_g, m_sb_kn_g, m_w_sb_proj, m_w_out, m_norm2_g, m_mlp_w1, m_mlp_w2, v_mod_w, v_mod_b, v_norm1_g, v_w_in, v_gate_b, v_conv_w, v_conv_b, v_conv_ln_g, v_conv_ln_b, v_w_conv_proj, v_hgrn_lb, v_hgrn_norm_g, v_w_hgrn_proj, v_sb_qn_g, v_sb_kn_g, v_w_sb_proj, v_w_out, v_norm2_g, v_mlp_w1, v_mlp_w2):
    given = dict(locals())
    wts = {n: given[n] for n in _WEIGHTS}
    mom = {n: given["m_" + n] for n in _WEIGHTS}
    var = {n: given["v_" + n] for n in _WEIGHTS}
    n_layers, D = norm1_g.shape
    xi, yi, ci = _place()
    q = 2 * xi + yi
    me = 4 * xi + 2 * yi + ci
    place = jnp.stack([q, ci]).astype(jnp.int32)
    n_mod = mod_w.shape[2]
    cw = conv_w.shape[2]

    pk1 = _Pack([("c", c), ("conv_w", conv_w)])
    got1 = _all_gather_small(pk1.array, name="gather_cond")
    c_act = jax.nn.silu(pk1.get(got1, "c")[:, 0, :])
    conv_full = jnp.concatenate([pk1.get(got1, "conv_w")[2 * k] for k in range(4)], axis=-1)

    mod_cols = []
    for l in range(n_layers):
        mb = lax.dynamic_slice_in_dim(mod_b[l], q * n_mod, n_mod)
        mod_cols.append(_matmul(c_act, mod_w, bl=l, name=f"mod_fwd_{l}") + mb[None, :])
    got2 = _all_gather_small(jnp.concatenate(mod_cols, axis=0), name="gather_mod")
    mods = []
    for l in range(n_layers):
        row = lax.dynamic_index_in_dim(got2[0::2], l * 8 + me, axis=1, keepdims=False)
        mods.append(jnp.split(row.reshape(1, 4 * n_mod), 6, axis=1))

    lower, lower_vjp = jax.vjp(_lower_bounds, hgrn_lb)

    shards = [wts[n].astype(BF16) for n, _ in _BIG]
    kinds = [k for _, k in _BIG]
    index = {n: k for k, (n, _) in enumerate(_BIG)}
    first = ["w_in", "w_conv_proj", "w_hgrn_proj", "w_sb_proj"]

    def gather(*names_layers):
        items = [(index[n], l) for names, l in names_layers for n in names if l < n_layers]
        return _Gather(shards, kinds, items) if items else None

    start = gather((first[:1], 0))
    w = {(_BIG[k][0], layer): arr
         for (k, layer), arr in zip(start.items, _run_comm(start, name="gather_first_weights"))}

    def layer_params(l):
        sh1, sc1, g1, sh2, sc2, g2 = mods[l]
        return dict(sh1=sh1, sc1=sc1, g1=g1, sh2=sh2, sc2=sc2, g2=g2, n1g=norm1_g[l][None], n2g=norm2_g[l][None],
                    gate_b=gate_b[l][None], conv_b=conv_b[l][None], lng=conv_ln_g[l][None], lnb=conv_ln_b[l][None],
                    w32=jnp.pad(conv_full[l], ((0, CONV_HALO - CONV_WIDTH), (0, 0))), lbk=(1.0 - lower[l])[None],
                    ng=hgrn_norm_g[l][None], qg=jnp.tile(sb_qn_g[l][None], (1, SB_PAIR // SB_DH)),
                    kg=jnp.tile(sb_kn_g[l][None], (1, SB_PAIR // SB_DH)))

    params = [layer_params(l) for l in range(n_layers)]
    act, saved = x[0], []
    for l in range(n_layers):
        comms = (gather((first[1:] if l == 0 else [], l), (["w_out", "mlp_w1"], l)),
                 gather((["mlp_w2"], l), (first, l + 1)))
        act, r = _layer_fwd(act, w, params[l], l, comms=comms)
        saved.append(r)
    dact, loss_lanes = _loss_head(act, loss_target[0], name="loss_head")

    grads, smalls, reduced = {}, [None] * n_layers, {}

    def reduce_scatter(items, tag):
        return _ReduceScatter([grads[_BIG[k][0], layer] for k, layer in items], [kinds[k] for k, _ in items],
                              [shards[k].shape[1:] for k, _ in items], place, tag)

    def carried(l):
        items = [(k, l + 1) for k in range(len(_BIG))] + [(k, l) for k, (n, _) in enumerate(_BIG) if n != "w_in"]
        box = {}

        def carry(stage, got):
            if stage == 0:
                box["rs"] = reduce_scatter(items, f"l{l}")
                return box["rs"].swap()
            if stage == 1:
                box["rs"].pair_sums(got)
                return box["rs"].scatter()
            if stage == 2:
                box["rs"].quarter_sums(got)
                return box["rs"].share()
            reduced.update(zip(items, got))

        return carry

    def final(l):
        items = [(index["w_in"], l)]
        box = {}

        def step(stage, got):
            if stage == 0:
                box["rs"] = reduce_scatter(items, "w_in")
                box["rs"].pair_sums(_run_comm(box["rs"].swap(), name="rs_swap_w_in"))
                return box["rs"].scatter()
            if stage == 1:
                box["rs"].quarter_sums(got)
                return box["rs"].share()
            reduced.update(zip(items, got))

        return step

    for l in reversed(range(n_layers)):
        dact, smalls[l] = _layer_bwd(dact, saved[l], w, params[l], l, grads, carried(l) if l + 1 < n_layers else None,
                                     final(l) if l == 0 else None)
    grad_x = dact[None]
    rest = [(k, l) for l in range(n_layers) for k in range(len(_BIG)) if (k, l) not in reduced]
    if rest:
        reduced.update(zip(rest, reduce_scatter(rest, "rest").run()))

    stack = lambda k: jnp.stack([smalls[l][k] for l in range(n_layers)])
    (d_hgrn_lb,) = lower_vjp(stack("lower")[:, 0, :])
    items = [("loss", loss_lanes), ("mod", stack("mod")), ("hgrn_lb", d_hgrn_lb), ("conv_w", stack("conv_w"))]
    items += [(k, stack(k)) for k in ("norm1_g", "gate_b", "conv_b", "conv_ln_g", "conv_ln_b", "hgrn_norm_g", "sb_qn_g",
                                      "sb_kn_g", "norm2_g")]
    pk3 = _Pack(items)

    share_small = _send_to_all(pk3.array)
    delta, new_m, new_v, big, got3 = {}, {}, {}, {}, None
    for n, _ in _BIG:
        outs = None
        for l in reversed(range(n_layers)):
            args = (wts[n], reduced[index[n], l], mom[n], var[n], l, outs)
            if got3 is None:
                outs, (got3,) = _adamw_layer(*args, name=f"adamw_{n}_{l}", comm=share_small)
            else:
                outs = _adamw_layer(*args, name=f"adamw_{n}_{l}")
        big[n] = outs
    got3 = lax.dynamic_update_slice_in_dim(got3, pk3.array[None], me, axis=0)
    tot3 = _sum8(got3, name="sum_small_grads")
    loss = (0.5 / D) * jnp.sum(pk3.get(tot3, "loss"))
    g = {k: pk3.get(tot3, k).reshape(wts[k].shape) for k in _REPLICATED if k != "mod_b"}
    g["mod_b"] = pk3.get(tot3, "mod")[:, 0, :]
    g["conv_w"] = lax.dynamic_slice_in_dim(pk3.get(tot3, "conv_w"), q * cw, cw, axis=2)
    dmod_all = pk3.get(got3, "mod")[:, :, 0, :]
    g_mod_w = None
    for l in range(n_layers):
        cols = lax.dynamic_slice_in_dim(dmod_all[:, l, :], q * n_mod, n_mod, axis=1)
        g_mod_w = _matmul(c_act, cols, ta=True, layer=l, n_layers=n_layers, into=g_mod_w, name=f"mod_dw_{l}")
    g["mod_w"] = g_mod_w

    for n, _ in _BIG:
        g[n], delta[n], new_m[n], new_v[n] = big[n]
    two_d = lambda t: t.reshape(-1, t.shape[-1])
    outs = _adamw(two_d(mod_w), two_d(g["mod_w"]), two_d(m_mod_w), two_d(v_mod_w), name="adamw_mod_w")
    delta["mod_w"], new_m["mod_w"], new_v["mod_w"] = (t.reshape(mod_w.shape) for t in outs)
    rest = _REPLICATED + ["conv_w"]
    packs = [_Pack([(n, src[n]) for n in rest]) for src in (wts, g, mom, var)]
    outs = _adamw(*[pk.array for pk in packs], name="adamw_small")
    for n in rest:
        delta[n], new_m[n], new_v[n] = (packs[0].get(t, n) for t in outs)

    return (loss, grad_x, *[g[n] for n in _WEIGHTS], *[delta[n] for n in _WEIGHTS], *[new_m[n] for n in _WEIGHTS],
            *[new_v[n] for n in _WEIGHTS])
```

```python
import functools
import math

import jax
import jax.numpy as jnp
from jax import lax
from jax.experimental import pallas as pl
from jax.experimental.pallas import tpu as pltpu

F32 = jnp.float32
BF16 = jnp.bfloat16
MESH = pl.DeviceIdType.MESH

EPS = 1e-6
CONV_CH = 512
CONV_WIDTH = 31
CONV_HALO = 32
HG_HEADS = 4
HG_D = 128
HG_CHUNK = 64
HG_KEYS = 8
HG_SUB = 32
SB_HEADS = 8
SB_DH = 64
SB_BLK = 128
SB_PAIR = 128
SB_SKIP = -104.0
OFF_CONV, OFF_HG, OFF_SB, OFF_GL = 0, 1024, 3072, 4608
ADAM_LR, ADAM_B1, ADAM_B2, ADAM_EPS, ADAM_WD, ADAM_STEP = 0.001, 0.9, 0.999, 1e-08, 0.01, 10
VMEM_LIMIT_BYTES = 56 * 1024 * 1024
ROW_TILE = 256


def _cparams(sem=None, **kw):
    return pltpu.CompilerParams(dimension_semantics=sem, vmem_limit_bytes=VMEM_LIMIT_BYTES, **kw)


def _pick(n, cands):
    for c in cands:
        if n % c == 0:
            return c
    return n


MATMUL_VMEM_BUDGET = 40 * 1024 * 1024


def _tile_options(n, cap):
    opts = [t for t in range(cap - cap % 128, 0, -128) if n % t == 0]
    return opts or [n]


def _matmul_tiles(M, N, K, size_a, size_b, size_o, in_acc):
    for tm in _tile_options(M, 1024):
        for tk in _tile_options(K, 2048):
            for tn in _tile_options(N, 1280):
                need = 2 * (tm * tk * size_a + tk * tn * size_b + tm * tn * size_o)
                if K > tk and not in_acc:
                    need += tm * tn * 4
                if need <= MATMUL_VMEM_BUDGET:
                    return tm, tn, tk
    raise ValueError(f"no matmul tiling fits VMEM for {(M, N, K)}")
def _matmul(a, b, *, ta=False, tb=False, bl=None, out_dtype=F32, name, into=None, layer=None, n_layers=None,
            post=None, extras=(), rows=(), out_dtypes=None, comm=None):
    M, K = (a.shape[1], a.shape[0]) if ta else a.shape
    N = b.shape[-2] if tb else b.shape[-1]
    if post is not None:
        return _matmul_post(a, b, M, N, K, ta, tb, post, extras, rows, out_dtypes, name)
    assert comm is None or layer is None
    in_acc = jnp.dtype(out_dtype) == jnp.dtype(F32)
    tm, tn, tk = _matmul_tiles(M, N, K, a.dtype.itemsize, b.dtype.itemsize, jnp.dtype(out_dtype).itemsize, in_acc)
    nk = K // tk
    a_spec = pl.BlockSpec((tk, tm), lambda i, j, k: (k, i)) if ta else pl.BlockSpec((tm, tk), lambda i, j, k: (i, k))
    if bl is None:
        b_spec = pl.BlockSpec((tn, tk), lambda i, j, k: (j, k)) if tb else pl.BlockSpec((tk, tn), lambda i, j, k: (k, j))
    elif tb:
        b_spec = pl.BlockSpec((None, tn, tk), lambda i, j, k: (bl, j, k))
    else:
        b_spec = pl.BlockSpec((None, tk, tn), lambda i, j, k: (bl, k, j))
    dn = (((0 if ta else 1,), (1 if tb else 0,)), ((), ()))

    use_scratch = nk > 1 and not in_acc

    def kern(a_ref, b_ref, *rest):
        o_ref = rest[-2] if use_scratch else rest[-1]
        prod = lambda: lax.dot_general(a_ref[...].astype(BF16), b_ref[...].astype(BF16), dn,
                                       preferred_element_type=F32)
        if nk == 1:
            o_ref[...] = prod().astype(o_ref.dtype).reshape(o_ref.shape)
            return
        acc_ref = rest[-1] if use_scratch else o_ref
        k = pl.program_id(2)

        @pl.when(k == 0)
        def _():
            acc_ref[...] = prod().reshape(acc_ref.shape)

        @pl.when(k > 0)
        def _():
            acc_ref[...] += prod().reshape(acc_ref.shape)

        if use_scratch:
            @pl.when(k == nk - 1)
            def _():
                o_ref[...] = acc_ref[...].astype(o_ref.dtype).reshape(o_ref.shape)

    in_specs, args, aliases = [a_spec, b_spec], [a, b], {}
    if layer is None:
        out_shape = jax.ShapeDtypeStruct((M, N), out_dtype)
        out_spec = pl.BlockSpec((tm, tn), lambda i, j, k: (i, j))
    else:
        out_shape = jax.ShapeDtypeStruct((n_layers, M, N), out_dtype)
        out_spec = pl.BlockSpec((1, tm, tn), lambda i, j, k: (layer, i, j))
        if into is not None:
            in_specs.append(pl.BlockSpec(memory_space=pl.ANY))
            args.append(into)
            aliases = {2: 0}
    if comm is not None:
        (out,), got = _hosted_call(kern, grid=(M // tm, N // tn, nk), in_specs=in_specs, out_specs=[out_spec],
                                   out_shape=[out_shape], scratch_shapes=[pltpu.VMEM((tm, tn), F32)] if use_scratch else [],
                                   args=args, name=name, comm=comm)
        return out, got
    return pl.pallas_call(
        kern, grid=(M // tm, N // tn, nk), in_specs=in_specs, out_specs=out_spec, out_shape=out_shape,
        scratch_shapes=[pltpu.VMEM((tm, tn), F32)] if use_scratch else [],
        input_output_aliases=aliases, name=name,
        compiler_params=_cparams(("parallel", "parallel", "arbitrary")))(*args)


def _matmul_post(a, b, M, N, K, ta, tb, post, extras, rows, out_dtypes, name):
    per_elem = sum(e.dtype.itemsize for e in extras) + sum(jnp.dtype(d).itemsize for d in out_dtypes)
    fits = lambda tm, tn: 2 * (tm * K * a.dtype.itemsize + K * tn * b.dtype.itemsize + tm * tn * per_elem) <= MATMUL_VMEM_BUDGET
    tm, tn = next((tm, tn) for tm in _tile_options(M, 1024) for tn in _tile_options(N, 1280) if fits(tm, tn))
    a_spec = pl.BlockSpec((K, tm), lambda i, j: (0, i)) if ta else pl.BlockSpec((tm, K), lambda i, j: (i, 0))
    b_spec = pl.BlockSpec((tn, K), lambda i, j: (j, 0)) if tb else pl.BlockSpec((K, tn), lambda i, j: (0, j))
    tile = pl.BlockSpec((tm, tn), lambda i, j: (i, j))
    row = pl.BlockSpec((1, tn), lambda i, j: (0, j))
    dn = (((0 if ta else 1,), (1 if tb else 0,)), ((), ()))
    n_ex = len(extras) + len(rows)

    def kern(a_ref, b_ref, *rest):
        prod = lax.dot_general(a_ref[...].astype(BF16), b_ref[...].astype(BF16), dn, preferred_element_type=F32)
        res = post(prod, *[r[...].astype(F32) for r in rest[:n_ex]])
        for val, o_ref in zip(res, rest[n_ex:]):
            o_ref[...] = val.astype(o_ref.dtype)

    return pl.pallas_call(
        kern, grid=(M // tm, N // tn), in_specs=[a_spec, b_spec] + [tile] * len(extras) + [row] * len(rows),
        out_specs=[tile] * len(out_dtypes), out_shape=[jax.ShapeDtypeStruct((M, N), d) for d in out_dtypes], name=name,
        compiler_params=_cparams(("parallel", "parallel")))(a, b, *extras, *rows)


def _col_specs(off, width, T):
    bw = math.gcd(width, off) if off else width
    return [pl.BlockSpec((T, bw), functools.partial(lambda i, c: (i, c), c=off // bw + p)) for p in range(width // bw)]


def _gather_rows(refs, counts):
    vals, pos = [], 0
    for n in counts:
        parts = [refs[pos + p][...].astype(F32) for p in range(n)]
        pos += n
        vals.append(parts[0] if n == 1 else jnp.concatenate(parts, axis=1))
    return vals, pos


def _rowop(fn, ins, params, outs, *, name):
    S = ins[0][0].shape[0]
    T = min(ROW_TILE, S)
    in_specs, counts, args = [], [], []
    for arr, off, width in ins:
        sp = _col_specs(off, width, T)
        in_specs += sp
        counts.append(len(sp))
        args += [arr] * len(sp)
    in_specs += [pl.BlockSpec(p.shape, lambda i: (0, 0)) for p in params]

    def kern(*refs):
        vals, pos = _gather_rows(refs, counts)
        pv = [refs[pos + p][...] for p in range(len(params))]
        pos += len(params)
        res = fn(*vals, *pv)
        for r, o_ref in zip(res, refs[pos:]):
            o_ref[...] = r.astype(o_ref.dtype)

    return pl.pallas_call(
        kern, grid=(S // T,), in_specs=in_specs,
        out_specs=[pl.BlockSpec((T, w), lambda i: (i, 0)) for w, _ in outs],
        out_shape=[jax.ShapeDtypeStruct((S, w), dt) for w, dt in outs],
        name=name, compiler_params=_cparams(("parallel",)))(*args, *params)


def _rowop_bwd(fn, ins, params, douts, din_dtypes, *, name, add=None, comm=None):
    add = add or {}
    S = ins[0][0].shape[0]
    T = min(ROW_TILE, S)
    in_specs, counts, args = [], [], []
    for arr, off, width in ins:
        sp = _col_specs(off, width, T)
        in_specs += sp
        counts.append(len(sp))
        args += [arr] * len(sp)
    in_specs += [pl.BlockSpec(p.shape, lambda i: (0, 0)) for p in params]
    in_specs += [pl.BlockSpec((T, d.shape[1]), lambda i: (i, 0)) for d in douts]
    add_keys = sorted(add)
    in_specs += [pl.BlockSpec((T, add[k].shape[1]), lambda i: (i, 0)) for k in add_keys]
    want = [k for k, dt in enumerate(din_dtypes) if dt is not None]

    def kern(*refs):
        vals, pos = _gather_rows(refs, counts)
        pv = [refs[pos + p][...] for p in range(len(params))]
        pos += len(params)
        cts = [refs[pos + p][...].astype(F32) for p in range(len(douts))]
        pos += len(douts)
        adds = {k: refs[pos + p][...].astype(F32) for p, k in enumerate(add_keys)}
        pos += len(add_keys)
        _, vjp = jax.vjp(fn, *vals, *pv)
        grads = vjp(tuple(cts))
        for k in want:
            g = grads[k] + adds[k] if k in adds else grads[k]
            refs[pos][...] = g.astype(refs[pos].dtype)
            pos += 1
        first = pl.program_id(0) == 0
        for p in range(len(params)):
            gp, o_ref = grads[len(ins) + p], refs[pos + p]

            @pl.when(first)
            def _(gp=gp, o_ref=o_ref):
                o_ref[...] = gp

            @pl.when(jnp.logical_not(first))
            def _(gp=gp, o_ref=o_ref):
                o_ref[...] += gp

    out_specs = [pl.BlockSpec((T, ins[k][2]), lambda i: (i, 0)) for k in want]
    out_specs += [pl.BlockSpec(p.shape, lambda i: (0, 0)) for p in params]
    out_shape = [jax.ShapeDtypeStruct((S, ins[k][2]), din_dtypes[k]) for k in want]
    out_shape += [jax.ShapeDtypeStruct(p.shape, F32) for p in params]
    res, got = _hosted_call(
        kern, grid=(S // T,), in_specs=in_specs, out_specs=out_specs, out_shape=out_shape, scratch_shapes=[],
        args=[*args, *params, *douts, *[add[k] for k in add_keys]], name=name, comm=comm, sem=("arbitrary",))
    dins = [None] * len(ins)
    for p, k in enumerate(want):
        dins[k] = res[p]
    return (dins, res[len(want):]) if comm is None else (dins, res[len(want):], got)


def _rms(x, g):
    return x * lax.rsqrt(jnp.mean(x * x, axis=-1, keepdims=True) + EPS) * g


def _fn_normmod(x, g, sc, sh):
    return (_rms(x, g) * (1.0 + sc) + sh,)


def _fn_lnsilu(c, g, b):
    mu = jnp.mean(c, axis=-1, keepdims=True)
    var = jnp.mean(jnp.square(c - mu), axis=-1, keepdims=True)
    y = (c - mu) * lax.rsqrt(var + EPS) * g + b
    return (y * jax.nn.sigmoid(y),)


def _fn_merge(gl, yc, yh, ys, gb):
    d = yc.shape[1]
    g = jax.nn.sigmoid(gl + gb)
    return (g[:, :d] * yc + g[:, d:2 * d] * yh + g[:, 2 * d:] * ys,)


def _fn_resid(x, y, g):
    return (x + g * y,)


def _fn_resid_norm(x, y, g, n, sc, sh):
    x1 = x + g * y
    return (x1,) + _fn_normmod(x1, n, sc, sh)


def _fn_scale(y, g):
    return (g * y,)


def _fn_relu2(u):
    return (jnp.square(jnp.maximum(u, 0.0)),)


def _conv_specs(S, T):
    r = T // CONV_HALO
    cur = [pl.BlockSpec((T, CONV_CH), lambda i: (i, 0)), pl.BlockSpec((T, CONV_CH), lambda i: (i, 1))]
    prev = [pl.BlockSpec((CONV_HALO, CONV_CH), lambda i: (jnp.maximum(i * r - 1, 0), 0)),
            pl.BlockSpec((CONV_HALO, CONV_CH), lambda i: (jnp.maximum(i * r - 1, 0), 1))]
    return cur + prev


def _glu_ext(a_ref, g_ref, ah_ref, gh_ref):
    a = a_ref[...]
    sg = jax.nn.sigmoid(g_ref[...])
    uh = jnp.where(pl.program_id(0) > 0, ah_ref[...] * jax.nn.sigmoid(gh_ref[...]), 0.0)
    return a, sg, jnp.concatenate([uh, a * sg], axis=0)


def _shift_up(xe, k, T):
    return xe[:T] if k == 0 else pltpu.roll(xe, shift=xe.shape[0] - k, axis=0)[:T]


def _conv_fwd(proj, w32, b, *, name):
    S = proj.shape[0]
    T = min(ROW_TILE, S)
    lead = CONV_HALO - (CONV_WIDTH - 1)

    def kern(a_ref, g_ref, ah_ref, gh_ref, w_ref, b_ref, o_ref):
        _, _, ue = _glu_ext(a_ref, g_ref, ah_ref, gh_ref)
        acc = jnp.zeros((T, CONV_CH), F32) + b_ref[...]
        for j in range(CONV_WIDTH):
            acc = acc + w_ref[j:j + 1, :] * _shift_up(ue, lead + j, T)
        o_ref[...] = acc

    const = lambda shape: pl.BlockSpec(shape, lambda i: (0, 0))
    return pl.pallas_call(
        kern, grid=(S // T,), in_specs=_conv_specs(S, T) + [const(w32.shape), const(b.shape)],
        out_specs=pl.BlockSpec((T, CONV_CH), lambda i: (i, 0)),
        out_shape=jax.ShapeDtypeStruct((S, CONV_CH), F32), name=name,
        compiler_params=_cparams(("parallel",)))(proj, proj, proj, proj, w32, b)


def _conv_bwd(proj, dc, w32, *, name, comm=None):
    S = proj.shape[0]
    T = min(ROW_TILE, S)
    nt = S // T
    r = T // CONV_HALO
    lead = CONV_HALO - (CONV_WIDTH - 1)
    last_halo = S // CONV_HALO - 1

    def kern(a_ref, g_ref, ah_ref, gh_ref, dc_ref, dcn_ref, w_ref, dag_ref, dw_ref, db_ref):
        i = pl.program_id(0)
        a, sg, ue = _glu_ext(a_ref, g_ref, ah_ref, gh_ref)
        dc_t = dc_ref[...]
        de = jnp.concatenate([dc_t, jnp.where(i < nt - 1, dcn_ref[...], 0.0)], axis=0)

        @pl.when(i == 0)
        def _():
            dw_ref[...] = jnp.zeros_like(dw_ref)
            db_ref[...] = jnp.zeros_like(db_ref)

        du = jnp.zeros((T, CONV_CH), F32)
        for j in range(CONV_WIDTH):
            du = du + w_ref[j:j + 1, :] * _shift_up(de, CONV_WIDTH - 1 - j, T)
            dw_ref[j:j + 1, :] += jnp.sum(dc_t * _shift_up(ue, lead + j, T), axis=0, keepdims=True)
        db_ref[...] += jnp.sum(dc_t, axis=0, keepdims=True)
        dag_ref[:, :CONV_CH] = (du * sg).astype(BF16)
        dag_ref[:, CONV_CH:] = (du * a * sg * (1.0 - sg)).astype(BF16)

    const = lambda shape: pl.BlockSpec(shape, lambda i: (0, 0))
    in_specs = _conv_specs(S, T) + [
        pl.BlockSpec((T, CONV_CH), lambda i: (i, 0)),
        pl.BlockSpec((CONV_HALO, CONV_CH), lambda i: (jnp.minimum((i + 1) * r, last_halo), 0)),
        const(w32.shape)]
    return _hosted_call(
        kern, grid=(nt,), in_specs=in_specs,
        out_specs=[pl.BlockSpec((T, 2 * CONV_CH), lambda i: (i, 0)), const(w32.shape), const((1, CONV_CH))],
        out_shape=[jax.ShapeDtypeStruct((S, 2 * CONV_CH), BF16), jax.ShapeDtypeStruct(w32.shape, F32),
                   jax.ShapeDtypeStruct((1, CONV_CH), F32)],
        scratch_shapes=[], args=[proj, proj, proj, proj, dc, dc, w32], name=name, comm=comm, sem=("arbitrary",))


def _iota2(shape, dim):
    return lax.broadcasted_iota(jnp.int32, shape, dim)


def _running(x, seg, later):
    n = x.shape[0]
    pos = _iota2(x.shape, 0) & (seg - 1)
    k = 1
    while k < seg:
        if later:
            x = x + jnp.where(pos < seg - k, pltpu.roll(x, n - k, axis=0), 0.0)
        else:
            x = x + jnp.where(pos >= k, pltpu.roll(x, k, axis=0), 0.0)
        k *= 2
    return x


@functools.partial(jax.custom_vjp, nondiff_argnums=(1,))
def _prefix(x, seg):
    return _running(x, seg, False)


_prefix.defvjp(lambda x, seg: (_running(x, seg, False), None), lambda seg, _, g: (_running(g, seg, True),))


def _hg_chunk(q, f, iv, g, st, lbk, ng):
    n, sub = HG_CHUNK, HG_SUB
    kk = lbk * jax.nn.sigmoid(-f)
    lf = jnp.log(1.0 - kk)
    b = _prefix(lf, n)
    bs = _prefix(lf, sub)
    bt = jnp.sum(lf, axis=0, keepdims=True)
    qh = q * jax.nn.sigmoid(q)
    dot_nt = lambda x, y: lax.dot_general(x.astype(BF16), y.astype(BF16), (((1,), (1,)), ((), ())),
                                          preferred_element_type=F32)
    o = dot_nt(qh * jnp.exp(b), st)
    b0 = b - bs
    qs = qh * jnp.exp(bs)
    col = _iota2((sub, n), 1)
    rows = []
    for blk in range(n // sub):
        lo = blk * sub
        sl = slice(lo, lo + sub)
        acc = o[sl]
        if blk > 0:
            ref = jnp.concatenate([b0[sl]] * (n // sub), axis=0)
            kd = kk * jnp.exp(jnp.minimum(ref - b, 0.0))
            sc = jnp.where(col < lo, dot_nt(qs[sl], kd), 0.0)
            acc = acc + jnp.dot(sc.astype(BF16), iv.astype(BF16), preferred_element_type=F32)
        for t0 in range(0, sub, HG_KEYS):
            keys, qrys = slice(lo + t0, lo + t0 + HG_KEYS), slice(lo + t0, lo + sub)
            nt = sub - t0
            bq, bk = bs[qrys][None, :, :], bs[keys][:, None, :]
            s_i = lax.broadcasted_iota(jnp.int32, (HG_KEYS, nt, HG_D), 0) + t0
            t_i = lax.broadcasted_iota(jnp.int32, (HG_KEYS, nt, HG_D), 1) + t0
            keep = s_i <= t_i
            p = jnp.where(keep, qh[qrys][None, :, :] * kk[keys][:, None, :] * jnp.exp(jnp.where(keep, bq - bk, 0.0)), 0.0)
            w = jnp.sum(p, axis=-1, keepdims=True)
            part = jnp.sum(w * iv[keys][:, None, :], axis=0)
            acc = acc + (part if t0 == 0 else jnp.concatenate([jnp.zeros((t0, HG_D), F32), part], axis=0))
        rows.append(acc)
    o = jnp.concatenate(rows, axis=0)
    kd = kk * jnp.exp(bt - b)
    st_new = jnp.exp(bt) * st + lax.dot_general(iv.astype(BF16), kd.astype(BF16), (((0,), (0,)), ((), ())),
                                                     preferred_element_type=F32)
    out = _rms(o, ng) * (g * jax.nn.sigmoid(g))
    return out, st_new


def _hg_tile(S):
    return min(512, S)


def _hg_in_specs(rt, rev, nr):
    width = HG_HEADS * HG_D
    base = OFF_HG // width
    row = (lambda r: nr - 1 - r) if rev else (lambda r: r)
    return [pl.BlockSpec((rt, width), functools.partial(lambda r, k: (row(r), base + k), k=k)) for k in range(4)]


def _hg_cols(h):
    return slice(h * HG_D, (h + 1) * HG_D)


def _hgrn_fwd(proj, lbk, ng, *, name, comm=None):
    S = proj.shape[0]
    rt = _hg_tile(S)
    nr, nc = S // rt, rt // HG_CHUNK

    def kern(q_ref, f_ref, i_ref, g_ref, lbk_ref, ng_ref, o_ref, st_out_ref, st_ref):
        @pl.when(pl.program_id(0) == 0)
        def _():
            st_ref[...] = jnp.zeros_like(st_ref)

        def body(c, carry):
            rows = pl.ds(pl.multiple_of(c * HG_CHUNK, HG_CHUNK), HG_CHUNK)
            for h in range(HG_HEADS):
                cols = _hg_cols(h)
                st = st_ref[h]
                st_out_ref[h, c] = st
                out, st_new = _hg_chunk(q_ref[rows, cols], f_ref[rows, cols], i_ref[rows, cols], g_ref[rows, cols], st,
                                        lbk_ref[:, cols], ng_ref[...])
                o_ref[rows, cols] = out.astype(o_ref.dtype)
                st_ref[h] = st_new
            return carry

        lax.fori_loop(0, nc, body, 0)

    width = HG_HEADS * HG_D
    in_specs = _hg_in_specs(rt, False, nr) + [pl.BlockSpec((1, width), lambda r: (0, 0)),
                                               pl.BlockSpec((1, HG_D), lambda r: (0, 0))]
    return _hosted_call(
        kern, grid=(nr,), in_specs=in_specs,
        out_specs=[pl.BlockSpec((rt, width), lambda r: (r, 0)),
                   pl.BlockSpec((HG_HEADS, nc, HG_D, HG_D), lambda r: (0, r, 0, 0))],
        out_shape=[jax.ShapeDtypeStruct((S, width), BF16),
                   jax.ShapeDtypeStruct((HG_HEADS, S // HG_CHUNK, HG_D, HG_D), F32)],
        scratch_shapes=[pltpu.VMEM((HG_HEADS, HG_D, HG_D), F32)],
        args=[proj, proj, proj, proj, lbk, ng], name=name, comm=comm, sem=("arbitrary",))


def _hgrn_bwd(proj, states, dout, lbk, ng, *, name, comm=None):
    S = proj.shape[0]
    rt = _hg_tile(S)
    nr, nc = S // rt, rt // HG_CHUNK
    width = HG_HEADS * HG_D

    def kern(q_ref, f_ref, i_ref, g_ref, st_in_ref, do_ref, lbk_ref, ng_ref,
             dq_ref, df_ref, di_ref, dg_ref, dlbk_ref, dng_ref, dst_ref):
        @pl.when(pl.program_id(0) == 0)
        def _():
            dst_ref[...] = jnp.zeros_like(dst_ref)
            dlbk_ref[...] = jnp.zeros_like(dlbk_ref)
            dng_ref[...] = jnp.zeros_like(dng_ref)

        def body(k, carry):
            c = nc - 1 - k
            rows = pl.ds(pl.multiple_of(c * HG_CHUNK, HG_CHUNK), HG_CHUNK)
            for h in range(HG_HEADS):
                cols = _hg_cols(h)
                _, vjp = jax.vjp(_hg_chunk, q_ref[rows, cols], f_ref[rows, cols], i_ref[rows, cols], g_ref[rows, cols],
                                 st_in_ref[h, c], lbk_ref[:, cols], ng_ref[...])
                dq, df, di, dg, dst, dlbk, dng = vjp((do_ref[rows, cols].astype(F32), dst_ref[h]))
                dq_ref[rows, cols] = dq.astype(BF16)
                df_ref[rows, cols] = df.astype(BF16)
                di_ref[rows, cols] = di.astype(BF16)
                dg_ref[rows, cols] = dg.astype(BF16)
                dst_ref[h] = dst
                dlbk_ref[:, cols] += dlbk
                dng_ref[h] += dng
            return carry

        lax.fori_loop(0, nc, body, 0)

    rev = lambda r: nr - 1 - r
    tile = pl.BlockSpec((rt, width), lambda r: (rev(r), 0))
    in_specs = _hg_in_specs(rt, True, nr) + [
        pl.BlockSpec((HG_HEADS, nc, HG_D, HG_D), lambda r: (0, rev(r), 0, 0)), tile,
        pl.BlockSpec((1, width), lambda r: (0, 0)), pl.BlockSpec((1, HG_D), lambda r: (0, 0))]
    return _hosted_call(
        kern, grid=(nr,), in_specs=in_specs,
        out_specs=[tile, tile, tile, tile, pl.BlockSpec((1, width), lambda r: (0, 0)),
                   pl.BlockSpec((HG_HEADS, 1, HG_D), lambda r: (0, 0, 0))],
        out_shape=[jax.ShapeDtypeStruct((S, width), BF16)] * 4 + [
            jax.ShapeDtypeStruct((1, width), F32), jax.ShapeDtypeStruct((HG_HEADS, 1, HG_D), F32)],
        scratch_shapes=[pltpu.VMEM((HG_HEADS, HG_D, HG_D), F32)],
        args=[proj, proj, proj, proj, states, dout, lbk, ng], name=name, comm=comm, sem=("arbitrary",))


def _sb_scores(km, qi):
    return lax.dot_general(km, qi, (((1,), (1,)), ((), ())), preferred_element_type=F32)


def _sb_weights(zt, r_run, diag):
    n = SB_BLK
    sp = jnp.maximum(zt, 0.0) + jnp.log(1.0 + jnp.exp(-jnp.abs(zt)))
    lk = -sp
    if diag:
        keep = (_iota2(zt.shape, 0) & (n - 1)) < _iota2(zt.shape, 1)
        lk = jnp.where(keep, lk, 0.0)
    tails = [_running(lk[a * n:(a + 1) * n], n, True) for a in range(2)]
    between = jnp.concatenate([tails[a] + r_run[a] for a in range(2)], axis=0)
    wgt = jnp.exp(zt + between)
    if diag:
        wgt = jnp.where(keep, wgt, 0.0)
    return sp, wgt, [t[0:1, :] for t in tails]


def _sb_norm_pair(x, g2, lane_lo):
    sq = x * x
    ms_lo = jnp.sum(jnp.where(lane_lo, sq, 0.0), axis=-1, keepdims=True)
    ms_hi = jnp.sum(jnp.where(lane_lo, 0.0, sq), axis=-1, keepdims=True)
    return x * lax.rsqrt(jnp.where(lane_lo, ms_lo, ms_hi) * (1.0 / SB_DH) + EPS) * g2


def _sb_specs(S):
    base = OFF_SB // SB_PAIR
    per = SB_HEADS * SB_DH // SB_PAIR
    cols = [pl.BlockSpec((S, SB_PAIR), functools.partial(lambda p, k: (0, base + per * k + p), k=k)) for k in range(3)]
    return cols + [pl.BlockSpec((1, SB_PAIR), lambda p: (0, 0))] * 2


def _sb_rows(i):
    return pl.ds(pl.multiple_of(i * SB_BLK, SB_BLK), SB_BLK)


def _sb_both(j, a=None):
    if a is None:
        return pl.ds(pl.multiple_of(j * 2 * SB_BLK, 2 * SB_BLK), 2 * SB_BLK)
    return pl.ds(pl.multiple_of(j * 2 * SB_BLK + a * SB_BLK, SB_BLK), SB_BLK)


def _sb_fwd(proj, qg2, kg2, *, name, comm=None):
    S = proj.shape[0]
    nb = S // SB_BLK
    scale = SB_DH ** -0.5
    n_pairs = SB_HEADS * SB_DH // SB_PAIR

    def kern(q_ref, k_ref, v_ref, qg_ref, kg_ref, o_ref, rs_ref, qp_ref, km_ref, vt_ref):
        lane_lo = _iota2((SB_BLK, SB_PAIR), 1) < SB_DH

        def prologue(j, carry):
            rows = _sb_rows(j)
            qp_ref[rows, :] = (_sb_norm_pair(q_ref[rows, :], qg_ref[...], lane_lo) * scale).astype(BF16)
            kn = _sb_norm_pair(k_ref[rows, :], kg_ref[...], lane_lo)
            v = v_ref[rows, :]
            for a, mine in enumerate((lane_lo, jnp.logical_not(lane_lo))):
                km_ref[_sb_both(j, a), :] = jnp.where(mine, kn, 0.0).astype(BF16)
                vt_ref[:, _sb_both(j, a)] = jnp.where(mine, v, 0.0).T.astype(BF16)
            return carry

        lax.fori_loop(0, nb, prologue, 0)

        diagonal = lambda i: _sb_scores(km_ref[_sb_both(i), :], qp_ref[_sb_rows(i), :])

        def qblock(i, zt):
            qi = qp_ref[_sb_rows(i), :]

            scores = lambda j: _sb_scores(km_ref[_sb_both(jnp.maximum(j, 0)), :], qi)
            output = lambda j, wgt: jnp.dot(vt_ref[:, _sb_both(j)], wgt, preferred_element_type=F32)

            def note(j, r_run):
                for a in range(2):
                    rs_ref[a, i, pl.ds(j, 1), :] = r_run[a]
                return jnp.maximum(jnp.max(r_run[0]), jnp.max(r_run[1])) > SB_SKIP

            def noted(j, r_run):
                return lax.cond(j >= 0, lambda: note(j, r_run).astype(jnp.int32), lambda: jnp.int32(0))

            zero = jnp.zeros((1, SB_BLK), F32)
            z_next = scores(i - 1)
            _, wgt, r_run = _sb_weights(zt, [zero, zero], True)
            go = noted(i - 1, r_run)

            def body(c):
                j, _, acc, r_run, zt, j_prev, w_prev = c
                z_next = scores(j - 1)
                acc = acc + output(j_prev, w_prev)
                _, wgt, lk_sum = _sb_weights(zt, r_run, False)
                r_run = [r_run[a] + lk_sum[a] for a in range(2)]
                return j - 1, noted(j - 1, r_run), acc, r_run, z_next, j, wgt.astype(BF16)

            c = (i - 1, go, jnp.zeros((SB_PAIR, SB_BLK), F32), r_run, z_next, i, wgt.astype(BF16))
            _, _, acc, _, _, j_prev, w_prev = lax.while_loop(lambda c: c[1] > 0, body, c)
            rs_ref[0, i, pl.ds(i, 1), :] = jnp.full((1, SB_BLK), j_prev, jnp.int32).astype(F32)
            zt = diagonal(jnp.minimum(i + 1, nb - 1))
            o_ref[_sb_rows(i), :] = (acc + output(j_prev, w_prev)).T.astype(o_ref.dtype)
            return zt

        lax.fori_loop(0, nb, qblock, diagonal(0))

    width = SB_HEADS * SB_DH
    return _hosted_call(
        kern, grid=(n_pairs,), in_specs=_sb_specs(S),
        out_specs=[pl.BlockSpec((S, SB_PAIR), lambda p: (0, p)),
                   pl.BlockSpec((2, nb, nb, SB_BLK), lambda p: (p, 0, 0, 0))],
        out_shape=[jax.ShapeDtypeStruct((S, width), BF16), jax.ShapeDtypeStruct((SB_HEADS, nb, nb, SB_BLK), F32)],
        scratch_shapes=[pltpu.VMEM((S, SB_PAIR), BF16), pltpu.VMEM((2 * S, SB_PAIR), BF16),
                        pltpu.VMEM((SB_PAIR, 2 * S), BF16)],
        args=[proj, proj, proj, qg2, kg2], name=name, comm=comm, sem=("parallel",))


def _sb_bwd(proj, qg2, kg2, rs, do, *, name, comm=None):
    S = proj.shape[0]
    nb = S // SB_BLK
    scale = SB_DH ** -0.5
    n_pairs = SB_HEADS * SB_DH // SB_PAIR

    def kern(q_ref, k_ref, v_ref, qg_ref, kg_ref, rs_ref, do_ref, dq_ref, dk_ref, dv_ref, dqg_ref, dkg_ref,
             qp_ref, km_ref, kt_ref, vm_ref, dqn_ref, dkn_ref, dvs_ref):
        lane_lo = _iota2((SB_BLK, SB_PAIR), 1) < SB_DH
        heads = (lane_lo, jnp.logical_not(lane_lo))
        fn_q = lambda x, g: _sb_norm_pair(x, g, lane_lo) * scale
        fn_k = lambda x, g: _sb_norm_pair(x, g, lane_lo)

        def prologue(j, carry):
            rows = _sb_rows(j)
            qp_ref[rows, :] = fn_q(q_ref[rows, :], qg_ref[...]).astype(BF16)
            kn = fn_k(k_ref[rows, :], kg_ref[...])
            v = v_ref[rows, :]
            for a, mine in enumerate(heads):
                k_a = jnp.where(mine, kn, 0.0)
                km_ref[_sb_both(j, a), :] = k_a.astype(BF16)
                kt_ref[:, _sb_both(j, a)] = k_a.T.astype(BF16)
                vm_ref[_sb_both(j, a), :] = jnp.where(mine, v, 0.0).astype(BF16)
            return carry

        lax.fori_loop(0, nb, prologue, 0)
        dkn_ref[...] = jnp.zeros_like(dkn_ref)
        dvs_ref[...] = jnp.zeros_like(dvs_ref)

        def leftmost(i):
            return jnp.clip(jnp.max(rs_ref[0, i, pl.ds(i, 1), :]).astype(jnp.int32), 0, i)

        def opening_of(i, j):
            jc = jnp.minimum(j, i)
            return (_sb_scores(km_ref[_sb_both(jc), :], qp_ref[_sb_rows(i), :]),
                    lax.dot_general(vm_ref[_sb_both(jc), :], do_ref[_sb_rows(i), :], (((1,), (1,)), ((), ())),
                                    preferred_element_type=F32))

        def qblock(i, carry):
            first, zt, dp = carry
            qi = qp_ref[_sb_rows(i), :]
            doi = do_ref[_sb_rows(i), :]

            opening = functools.partial(opening_of, i)

            def closing(j, dzb, wgtb, dqa):
                dkn_ref[_sb_both(j), :] += jnp.dot(dzb, qi, preferred_element_type=F32)
                dvs_ref[_sb_both(j), :] += jnp.dot(wgtb, doi, preferred_element_type=F32)
                return dqa + jnp.dot(kt_ref[:, _sb_both(j)], dzb, preferred_element_type=F32)

            def middle(j, diag, zt, dp, e_run):
                zero = jnp.zeros((1, SB_BLK), F32)
                r_run = [zero, zero] if diag else [rs_ref[a, i, pl.ds(j, 1), :] for a in range(2)]
                sp, wgt, _ = _sb_weights(zt, r_run, diag)
                e = dp * wgt
                heads_e = [_running(e[a * SB_BLK:(a + 1) * SB_BLK], SB_BLK, False) for a in range(2)]
                e_left = jnp.concatenate([heads_e[a] + e_run[a] for a in range(2)], axis=0) - e
                s_neg = jnp.exp(-sp)
                dz = e * s_neg - e_left * (1.0 - s_neg)
                if diag:
                    dz = jnp.where((_iota2(dz.shape, 0) & (SB_BLK - 1)) < _iota2(dz.shape, 1), dz, 0.0)
                return dz.astype(BF16), wgt.astype(BF16), [e_run[a] + heads_e[a][SB_BLK - 1:SB_BLK, :] for a in range(2)]

            def body(j, c):
                dqa, e_run, zt, dp, j_prev, dzb, wgtb = c
                nxt = opening(j + 1)
                dqa = closing(j_prev, dzb, wgtb, dqa)
                dzb, wgtb, e_run = middle(j, False, zt, dp, e_run)
                return (dqa, e_run) + nxt + (j, dzb, wgtb)

            zero = jnp.zeros((1, SB_BLK), F32)
            none = jnp.zeros((2 * SB_BLK, SB_BLK), BF16)
            c = (jnp.zeros((SB_PAIR, SB_BLK), F32), [zero, zero], zt, dp, first, none, none)
            dqa, e_run, zt, dp, j_prev, dzb, wgtb = lax.fori_loop(first, i, body, c)
            dqa = closing(j_prev, dzb, wgtb, dqa)
            dzb, wgtb, _ = middle(i, True, zt, dp, e_run)
            i_next = jnp.minimum(i + 1, nb - 1)
            first_next = leftmost(i_next)
            nxt = opening_of(i_next, first_next)
            dqn_ref[_sb_rows(i), :] = closing(i, dzb, wgtb, dqa).T
            return (first_next,) + nxt

        lax.fori_loop(0, nb, qblock, (leftmost(0),) + opening_of(0, 0))
        dqg_ref[...] = jnp.zeros_like(dqg_ref)
        dkg_ref[...] = jnp.zeros_like(dkg_ref)

        def epilogue(j, carry):
            rows = _sb_rows(j)
            _, vjp_q = jax.vjp(fn_q, q_ref[rows, :], qg_ref[...])
            dq, dqg = vjp_q(dqn_ref[rows, :])
            _, vjp_k = jax.vjp(fn_k, k_ref[rows, :], kg_ref[...])
            dk, dkg = vjp_k(jnp.where(lane_lo, dkn_ref[_sb_both(j, 0), :], dkn_ref[_sb_both(j, 1), :]))
            dq_ref[rows, :] = dq.astype(BF16)
            dk_ref[rows, :] = dk.astype(BF16)
            dv_ref[rows, :] = jnp.where(lane_lo, dvs_ref[_sb_both(j, 0), :], dvs_ref[_sb_both(j, 1), :]).astype(BF16)
            dqg_ref[0] += dqg
            dkg_ref[0] += dkg
            return carry

        lax.fori_loop(0, nb, epilogue, 0)

    width = SB_HEADS * SB_DH
    pair = pl.BlockSpec((S, SB_PAIR), lambda p: (0, p))
    dgain = pl.BlockSpec((1, 1, SB_PAIR), lambda p: (p, 0, 0))
    in_specs = _sb_specs(S) + [pl.BlockSpec((2, nb, nb, SB_BLK), lambda p: (p, 0, 0, 0)), pair]
    return _hosted_call(
        kern, grid=(n_pairs,), in_specs=in_specs, out_specs=[pair, pair, pair, dgain, dgain],
        out_shape=[jax.ShapeDtypeStruct((S, width), BF16)] * 3 + [jax.ShapeDtypeStruct((n_pairs, 1, SB_PAIR), F32)] * 2,
        scratch_shapes=[pltpu.VMEM((S, SB_PAIR), BF16), pltpu.VMEM((2 * S, SB_PAIR), BF16), pltpu.VMEM((SB_PAIR, 2 * S), BF16),
                        pltpu.VMEM((2 * S, SB_PAIR), BF16), pltpu.VMEM((S, SB_PAIR), F32),
                        pltpu.VMEM((2 * S, SB_PAIR), F32), pltpu.VMEM((2 * S, SB_PAIR), F32)],
        args=[proj, proj, proj, qg2, kg2, rs, do], name=name, comm=comm, sem=("parallel",))


def _loss_head(y, target, *, name):
    S, D = y.shape
    T = min(ROW_TILE, S)

    def kern(y_ref, t_ref, dy_ref, acc_ref):
        err = y_ref[...] - t_ref[...]
        dy_ref[...] = err * (1.0 / D)
        col = jnp.sum(err * err, axis=0, keepdims=True)
        part = sum(col[:, k * 128:(k + 1) * 128] for k in range(D // 128))

        @pl.when(pl.program_id(0) == 0)
        def _():
            acc_ref[...] = part

        @pl.when(pl.program_id(0) > 0)
        def _():
            acc_ref[...] += part

    tile = pl.BlockSpec((T, D), lambda i: (i, 0))
    return pl.pallas_call(
        kern, grid=(S // T,), in_specs=[tile, tile], out_specs=[tile, pl.BlockSpec((1, 128), lambda i: (0, 0))],
        out_shape=[jax.ShapeDtypeStruct((S, D), F32), jax.ShapeDtypeStruct((1, 128), F32)],
        name=name, compiler_params=_cparams(("arbitrary",)))(y, target)


def _adamw_math(w, g, m, v):
    m = ADAM_B1 * m + (1.0 - ADAM_B1) * g
    v = ADAM_B2 * v + (1.0 - ADAM_B2) * jnp.square(g)
    m_hat = m / (1.0 - ADAM_B1 ** ADAM_STEP)
    v_hat = v / (1.0 - ADAM_B2 ** ADAM_STEP)
    return -ADAM_LR * (m_hat / (jnp.sqrt(v_hat) + ADAM_EPS) + ADAM_WD * w), m, v


def _adamw(w, g, m, v, *, name):
    R, C = w.shape
    T = _pick(R, (256, 128, 64, 32, 16, 8))

    def kern(w_ref, g_ref, m_ref, v_ref, d_ref, mo_ref, vo_ref):
        d, mn, vn = _adamw_math(w_ref[...], g_ref[...], m_ref[...], v_ref[...])
        d_ref[...] = d
        mo_ref[...] = mn
        vo_ref[...] = vn

    tile = pl.BlockSpec((T, C), lambda i: (i, 0))
    return pl.pallas_call(
        kern, grid=(R // T,), in_specs=[tile] * 4, out_specs=[tile] * 3,
        out_shape=[jax.ShapeDtypeStruct((R, C), F32)] * 3, name=name,
        compiler_params=_cparams(("parallel",)))(w, g, m, v)


def _adamw_layer(w, g, m, v, layer, prev, *, name, comm=None):
    L, R, C = w.shape
    T = _pick(R, (256, 128, 64, 32, 16, 8))

    def kern(w_ref, g_ref, m_ref, v_ref, *rest):
        go_ref, d_ref, mo_ref, vo_ref = rest[-4:]
        grad = g_ref[...]
        d, mn, vn = _adamw_math(w_ref[...], grad, m_ref[...], v_ref[...])
        go_ref[...] = grad
        d_ref[...] = d
        mo_ref[...] = mn
        vo_ref[...] = vn

    layer_tile = pl.BlockSpec((None, T, C), lambda i: (layer, i, 0))
    in_specs = [layer_tile, pl.BlockSpec((T, C), lambda i: (i, 0)), layer_tile, layer_tile]
    args, aliases = [w, g, m, v], {}
    if prev is not None:
        in_specs += [pl.BlockSpec(memory_space=pl.ANY)] * 4
        args += list(prev)
        aliases = {4 + k: k for k in range(4)}
    if comm is not None:
        assert prev is None
        return _hosted_call(kern, grid=(R // T,), in_specs=in_specs, out_specs=[layer_tile] * 4,
                            out_shape=[jax.ShapeDtypeStruct((L, R, C), F32)] * 4, scratch_shapes=[], args=args,
                            name=name, comm=comm)
    return pl.pallas_call(
        kern, grid=(R // T,), in_specs=in_specs, out_specs=[layer_tile] * 4,
        out_shape=[jax.ShapeDtypeStruct((L, R, C), F32)] * 4, input_output_aliases=aliases, name=name,
        compiler_params=_cparams(("parallel",)))(*args)


def _sum8(g, *, name):
    def kern(g_ref, o_ref):
        acc = g_ref[0]
        for d in range(1, g.shape[0]):
            acc = acc + g_ref[d]
        o_ref[...] = acc

    return pl.pallas_call(kern, out_shape=jax.ShapeDtypeStruct(g.shape[1:], F32), name=name,
                          compiler_params=_cparams())(g)


def _place():
    return lax.axis_index("x"), lax.axis_index("y"), lax.axis_index("c")


def _other_chips(x, y):
    return [(1 - x, y), (x, 1 - y), (1 - x, 1 - y)]


def _remote(src, dst, send_sems, recv_sems, k, to):
    return pltpu.make_async_remote_copy(src_ref=src, dst_ref=dst, send_sem=send_sems.at[k], recv_sem=recv_sems.at[k],
                                        device_id=to, device_id_type=MESH)


def _all_gather_small(v, *, name):
    def body(x_ref, out_ref, send_sems, recv_sems, local_sem):
        x, y, c = _place()
        me = 4 * x + 2 * y + c
        mine = pltpu.make_async_copy(x_ref, out_ref.at[me], local_sem)
        mine.start()
        peers = []
        for f in range(1, 8):
            peers.append((1 - x if f & 4 else x, 1 - y if f & 2 else y, 1 - c if f & 1 else c))
        sends = [_remote(x_ref, out_ref.at[me], send_sems, recv_sems, k, p) for k, p in enumerate(peers)]
        for cp in sends:
            cp.start()
        for k, (px, py, pc) in enumerate(peers):
            _remote(x_ref, out_ref.at[4 * px + 2 * py + pc], send_sems, recv_sems, k, (px, py, pc)).wait_recv()
        for cp in sends:
            cp.wait_send()
        mine.wait()

    return pl.pallas_call(
        body, out_shape=jax.ShapeDtypeStruct((8,) + v.shape, v.dtype),
        in_specs=[pl.BlockSpec(memory_space=pltpu.VMEM)], out_specs=pl.BlockSpec(memory_space=pltpu.VMEM),
        scratch_shapes=[pltpu.SemaphoreType.DMA((7,)), pltpu.SemaphoreType.DMA((7,)), pltpu.SemaphoreType.DMA],
        name=name, compiler_params=_cparams())(v)


def _hosted_call(kern, *, grid, in_specs, out_specs, out_shape, scratch_shapes, args, name, comm=None, sem=None):
    if comm is None:
        res = pl.pallas_call(kern, grid=grid, in_specs=in_specs, out_specs=out_specs, out_shape=out_shape,
                             scratch_shapes=scratch_shapes, name=name, compiler_params=_cparams(sem))(*args)
        return list(res), []
    n_in, n_out, n_scr = len(in_specs), len(out_specs), len(scratch_shapes)
    c_in, c_out = len(comm.inputs), len(comm.out_shapes)

    def body(*refs):
        ins, ci = refs[:n_in], refs[n_in:n_in + c_in]
        outs = refs[n_in + c_in:n_in + c_in + n_out]
        co = refs[n_in + c_in + n_out:n_in + c_in + n_out + c_out]
        scr = refs[n_in + c_in + n_out + c_out:n_in + c_in + n_out + c_out + n_scr]
        cs = refs[n_in + c_in + n_out + c_out + n_scr:]
        ids = [pl.program_id(d) for d in range(len(grid))]
        inner_first = functools.reduce(jnp.logical_and, [i == 0 for i in ids[1:]], True)
        inner_last = functools.reduce(jnp.logical_and, [i == n - 1 for i, n in zip(ids[1:], grid[1:])], True)

        @pl.when(jnp.logical_and(ids[0] == 0, inner_first))
        def _():
            comm.begin(ci, co, cs)

        kern(*ins, *outs, *scr)

        @pl.when(jnp.logical_and(ids[0] == grid[0] // 2, inner_last))
        def _():
            comm.middle(ci, co, cs)

        @pl.when(jnp.logical_and(ids[0] == grid[0] - 1, inner_last))
        def _():
            comm.end(ci, co, cs)

    hbm = pl.BlockSpec(memory_space=pltpu.HBM)
    res = pl.pallas_call(
        body, grid=grid, in_specs=list(in_specs) + [hbm] * c_in, out_specs=list(out_specs) + [hbm] * c_out,
        out_shape=list(out_shape) + list(comm.out_shapes), scratch_shapes=list(scratch_shapes) + list(comm.scratch),
        input_output_aliases={n_in + i: n_out + o for i, o in comm.aliases.items()},
        name=name, compiler_params=_cparams(("arbitrary",) * len(grid)))(*args, *comm.inputs)
    return list(res[:n_out]), list(res[n_out:])


def _run_comm(comm, *, name):
    return _hosted_call(lambda: None, grid=(1,), in_specs=[], out_specs=[], out_shape=[], scratch_shapes=[], args=[],
                        name=name, comm=comm)[1]


class _Gather:
    def __init__(self, shards, kinds, items):
        used = sorted({w for w, _ in items})
        self.slot = {w: k for k, w in enumerate(used)}
        self.inputs = [shards[w] for w in used]
        self.items, self.kinds = list(items), kinds
        self.shapes = {w: shards[w].shape[1:] for w in used}
        self.out_shapes = [jax.ShapeDtypeStruct((r, 4 * n) if kinds[w] == "col" else (4 * r, n), shards[w].dtype)
                           for w, _ in items for r, n in [self.shapes[w]]]
        n_items = len(items)
        self.scratch = [pltpu.SemaphoreType.DMA((6 * n_items,)), pltpu.SemaphoreType.DMA((6 * n_items,)),
                        pltpu.SemaphoreType.DMA((n_items,))]
        self.aliases = {}

    def _piece(self, ref, w, qq, half):
        r, n = self.shapes[w]
        h = r // 2
        lo, size = (0, r) if half is None else (half * h, h)
        if self.kinds[w] == "col":
            return ref.at[pl.ds(pl.multiple_of(lo, 16), size), pl.ds(pl.multiple_of(qq * n, 128), n)]
        return ref.at[pl.ds(pl.multiple_of(qq * r + lo, 16), size), :]

    def _mine(self, ci, w, l, half):
        h = self.shapes[w][0] // 2
        return ci[self.slot[w]].at[l, pl.ds(pl.multiple_of(half * h, 16), h), :]

    def begin(self, ci, co, cs):
        send_sems, recv_sems, local_sems = cs
        x, y, c = _place()
        q = 2 * x + y
        for k, (w, l) in enumerate(self.items):
            pltpu.make_async_copy(ci[self.slot[w]].at[l], self._piece(co[k], w, q, None), local_sems.at[k]).start()
            for j, (cx, cy) in enumerate(_other_chips(x, y)):
                _remote(self._mine(ci, w, l, c), self._piece(co[k], w, q, c), send_sems, recv_sems, 6 * k + j,
                        (cx, cy, c)).start()

    def middle(self, ci, co, cs):
        send_sems, recv_sems, _ = cs
        x, y, c = _place()
        for k, (w, l) in enumerate(self.items):
            for j, (cx, cy) in enumerate(_other_chips(x, y)):
                win = self._piece(co[k], w, 2 * cx + cy, c)
                _remote(win, win, send_sems, recv_sems, 6 * k + j, (cx, cy, c)).wait_recv()
                _remote(win, win, send_sems, recv_sems, 6 * k + 3 + j, (x, y, 1 - c)).start()

    def end(self, ci, co, cs):
        send_sems, recv_sems, local_sems = cs
        x, y, c = _place()
        q = 2 * x + y
        for k, (w, l) in enumerate(self.items):
            for j, (cx, cy) in enumerate(_other_chips(x, y)):
                win = self._piece(co[k], w, 2 * cx + cy, 1 - c)
                _remote(win, win, send_sems, recv_sems, 6 * k + 3 + j, (x, y, 1 - c)).wait_recv()
        for k, (w, l) in enumerate(self.items):
            for j, (cx, cy) in enumerate(_other_chips(x, y)):
                _remote(self._mine(ci, w, l, c), self._piece(co[k], w, q, c), send_sems, recv_sems, 6 * k + j,
                        (cx, cy, c)).wait_send()
                win = self._piece(co[k], w, 2 * cx + cy, c)
                _remote(win, win, send_sems, recv_sems, 6 * k + 3 + j, (x, y, 1 - c)).wait_send()
            pltpu.make_async_copy(ci[self.slot[w]].at[l], self._piece(co[k], w, q, None), local_sems.at[k]).wait()


def _half_rows(ref, half, h):
    return ref.at[:, pl.ds(pl.multiple_of(half * h, 16), h), :]


class _Copies:
    def __init__(self, inputs, out_shapes, count, pairs, aliases=None, lands=None):
        self.inputs, self.out_shapes, self.pairs, self.lands = list(inputs), list(out_shapes), pairs, lands
        self.scratch = [pltpu.SemaphoreType.DMA((count,)), pltpu.SemaphoreType.DMA((count,))]
        self.aliases = aliases or {}

    def _copies(self, ci, co, cs):
        x, y, c = _place()
        return [_remote(src, dst, cs[0], cs[1], k, to) for k, (src, dst, to) in enumerate(self.pairs(ci, co, x, y, c))]

    def begin(self, ci, co, cs):
        for cp in self._copies(ci, co, cs):
            cp.start()

    def middle(self, ci, co, cs):
        pass

    def end(self, ci, co, cs):
        x, y, c = _place()
        for k, (src, dst, to) in enumerate(self.pairs(ci, co, x, y, c)):
            _remote(src, dst, cs[0], cs[1], k, to).wait_send()
            arrival = dst if self.lands is None else self.lands(co, x, y, c)[k]
            _remote(src, arrival, cs[0], cs[1], k, to).wait_recv()


class _Together:
    def __init__(self, progs):
        self.progs = progs
        self.inputs = [a for p in progs for a in p.inputs]
        self.out_shapes = [o for p in progs for o in p.out_shapes]
        self.scratch = [t for p in progs for t in p.scratch]
        self.aliases, n_in, n_out = {}, 0, 0
        for p in progs:
            self.aliases.update({n_in + i: n_out + o for i, o in p.aliases.items()})
            n_in, n_out = n_in + len(p.inputs), n_out + len(p.out_shapes)

    def _each(self, ci, co, cs):
        i = o = t = 0
        for p in self.progs:
            ni, no, nt = len(p.inputs), len(p.out_shapes), len(p.scratch)
            yield p, ci[i:i + ni], co[o:o + no], cs[t:t + nt]
            i, o, t = i + ni, o + no, t + nt

    def begin(self, ci, co, cs):
        for p, a, b, c in self._each(ci, co, cs):
            p.begin(a, b, c)

    def middle(self, ci, co, cs):
        for p, a, b, c in self._each(ci, co, cs):
            p.middle(a, b, c)

    def end(self, ci, co, cs):
        for p, a, b, c in self._each(ci, co, cs):
            p.end(a, b, c)

    def split(self, results):
        out, o = [], 0
        for p in self.progs:
            out.append(results[o:o + len(p.out_shapes)])
            o += len(p.out_shapes)
        return out


def _send_to_all(v):
    def peers(x, y, c):
        return [(1 - x if f & 4 else x, 1 - y if f & 2 else y, 1 - c if f & 1 else c) for f in range(1, 8)]

    def pairs(ci, co, x, y, c):
        return [(ci[0], co[0].at[4 * x + 2 * y + c], peer) for peer in peers(x, y, c)]

    def lands(co, x, y, c):
        return [co[0].at[4 * px + 2 * py + pc] for px, py, pc in peers(x, y, c)]

    return _Copies([v], [jax.ShapeDtypeStruct((8,) + v.shape, v.dtype)], 7, pairs, lands=lands)


def _swap_halves(gs):
    def pairs(ci, co, x, y, c):
        return [(_half_rows(ci[k], 1 - c, g.shape[1] // 2), co[k], (x, y, 1 - c)) for k, g in enumerate(gs)]

    return _Copies(gs, [jax.ShapeDtypeStruct((g.shape[0], g.shape[1] // 2, g.shape[2]), g.dtype) for g in gs],
                   len(gs), pairs)


def _scatter_quarters(ps, kinds):
    part = [((p.shape[1], p.shape[2] // 4) if kind == "col" else (p.shape[1], p.shape[2])) for p, kind in zip(ps, kinds)]

    def pairs(ci, co, x, y, c):
        out = []
        for k, kind in enumerate(kinds):
            n = part[k][1]
            for j, (cx, cy) in enumerate(_other_chips(x, y)):
                qj = 2 * cx + cy
                src = ci[k].at[0, :, pl.ds(pl.multiple_of(qj * n, 128), n)] if kind == "col" else ci[k].at[qj]
                out.append((src, co[k].at[j], (cx, cy, c)))
        return out

    return _Copies(ps, [jax.ShapeDtypeStruct((3,) + pt, p.dtype) for pt, p in zip(part, ps)], 3 * len(ps), pairs)


def _share_halves(gs):
    def rows(co, k, half):
        h = gs[k].shape[0] // 2
        return co[k].at[pl.ds(pl.multiple_of(half * h, 16), h), :]

    def pairs(ci, co, x, y, c):
        return [(rows(co, k, c), rows(co, k, c), (x, y, 1 - c)) for k in range(len(gs))]

    def lands(co, x, y, c):
        return [rows(co, k, 1 - c) for k in range(len(gs))]

    return _Copies(gs, [jax.ShapeDtypeStruct(g.shape, g.dtype) for g in gs], len(gs), pairs,
                   aliases={k: k for k in range(len(gs))}, lands=lands)


def _wide_tile(n):
    return _pick(n, (2048, 1920, 1024, 512, 256, 128))


def _pair_sum(g, land, place, *, name):
    B, R, N = g.shape
    h = R // 2
    tr, tc = _pick(h, (256, 128)), _wide_tile(N)

    def kern(place_ref, g_ref, l_ref, o_ref):
        o_ref[...] = (g_ref[...] + l_ref[...]).astype(o_ref.dtype)

    grid_spec = pltpu.PrefetchScalarGridSpec(
        num_scalar_prefetch=1, grid=(B, h // tr, N // tc),
        in_specs=[pl.BlockSpec((None, tr, tc), lambda b, i, j, p: (b, p[1] * (h // tr) + i, j)),
                  pl.BlockSpec((None, tr, tc), lambda b, i, j, p: (b, i, j))],
        out_specs=pl.BlockSpec((None, tr, tc), lambda b, i, j, p: (b, i, j)))
    return pl.pallas_call(kern, grid_spec=grid_spec, out_shape=jax.ShapeDtypeStruct((B, h, N), BF16), name=name,
                          compiler_params=_cparams(("parallel", "parallel", "parallel")))(place, g, land)


def _quarter_sum(p, land, kind, shard_shape, place, *, name):
    L, r, n = shard_shape
    h = r // 2
    tr, tc = _pick(h, (256, 128)), _wide_tile(n)

    def kern(place_ref, p_ref, a_ref, b_ref, c_ref, o_ref):
        o_ref[...] = ((p_ref[...].astype(F32) + a_ref[...].astype(F32)) + b_ref[...].astype(F32)) + c_ref[...].astype(F32)

    if kind == "col":
        p_spec = pl.BlockSpec((None, tr, tc), lambda l, i, j, pr: (l, i, pr[0] * (n // tc) + j))
    else:
        p_spec = pl.BlockSpec((None, None, tr, tc), lambda l, i, j, pr: (l, pr[0], i, j))
    lands = [pl.BlockSpec((None, None, tr, tc), functools.partial(lambda l, i, j, pr, s: (s, l, i, j), s=s))
             for s in range(3)]
    grid_spec = pltpu.PrefetchScalarGridSpec(
        num_scalar_prefetch=1, grid=(L, h // tr, n // tc), in_specs=[p_spec] + lands,
        out_specs=pl.BlockSpec((None, tr, tc), lambda l, i, j, pr: (l, pr[1] * (h // tr) + i, j)))
    return pl.pallas_call(kern, grid_spec=grid_spec, out_shape=jax.ShapeDtypeStruct((L, r, n), F32), name=name,
                          compiler_params=_cparams(("parallel", "parallel", "parallel")))(place, p, land, land, land)


class _ReduceScatter:
    def __init__(self, grads, kinds, shard_shapes, place, tag):
        self.kinds, self.shapes, self.place, self.tag = kinds, shard_shapes, place, tag
        self.g3 = [g[None] if kind == "col" else g.reshape(4, g.shape[0] // 4, g.shape[1]) for g, kind in zip(grads, kinds)]

    def swap(self):
        return _swap_halves(self.g3)

    def pair_sums(self, lands):
        self.ps = [_pair_sum(g, land, self.place, name=f"rs_pair_sum_{self.tag}_{k}")
                   for k, (g, land) in enumerate(zip(self.g3, lands))]

    def scatter(self):
        return _scatter_quarters(self.ps, self.kinds)

    def quarter_sums(self, parts):
        self.halves = []
        for k, (p, part) in enumerate(zip(self.ps, parts)):
            p4 = p if self.kinds[k] == "col" else p[None]
            out = _quarter_sum(p4, part[:, None], self.kinds[k], (1,) + tuple(self.shapes[k]), self.place,
                               name=f"rs_quarter_sum_{self.tag}_{k}")
            self.halves.append(out[0])

    def share(self):
        return _share_halves(self.halves)

    def run(self):
        self.pair_sums(_run_comm(self.swap(), name=f"rs_swap_{self.tag}"))
        self.quarter_sums(_run_comm(self.scatter(), name=f"rs_scatter_{self.tag}"))
        return _run_comm(self.share(), name=f"rs_share_{self.tag}")


_WEIGHTS = ["mod_w", "mod_b", "norm1_g", "w_in", "gate_b", "conv_w", "conv_b", "conv_ln_g", "conv_ln_b", "w_conv_proj",
            "hgrn_lb", "hgrn_norm_g", "w_hgrn_proj", "sb_qn_g", "sb_kn_g", "w_sb_proj", "w_out", "norm2_g", "mlp_w1",
            "mlp_w2"]
_BIG = [("w_in", "col"), ("w_conv_proj", "col"), ("w_hgrn_proj", "col"), ("w_sb_proj", "col"), ("w_out", "row"),
        ("mlp_w1", "col"), ("mlp_w2", "row")]
_REPLICATED = ["mod_b", "norm1_g", "gate_b", "conv_b", "conv_ln_g", "conv_ln_b", "hgrn_lb", "hgrn_norm_g", "sb_qn_g",
               "sb_kn_g", "norm2_g"]
LANES = 128


class _Pack:
    def __init__(self, items):
        self.shapes = {n: a.shape for n, a in items}
        self.offsets, pos = {}, 0
        for n, a in items:
            self.offsets[n] = pos
            pos += math.prod(a.shape)
        self.rows = -(-pos // (8 * LANES)) * 8
        flat = jnp.concatenate([a.reshape(-1).astype(F32) for _, a in items])
        self.array = jnp.pad(flat, (0, self.rows * LANES - pos)).reshape(self.rows, LANES)

    def get(self, packed, name):
        lead = packed.shape[:-2]
        flat = packed.reshape(lead + (self.rows * LANES,))
        n = math.prod(self.shapes[name])
        return lax.slice_in_dim(flat, self.offsets[name], self.offsets[name] + n, axis=len(lead)).reshape(
            lead + self.shapes[name])


def _lower_bounds(hgrn_lb):
    p = jax.nn.softmax(hgrn_lb.astype(F32), axis=0)
    return jnp.cumsum(p, axis=0) - p[0:1]


def _layer_fwd(x, w, p, l, comms=(None, None)):
    S, D = x.shape
    r = {"x": x}
    (r["h"],) = _rowop(_fn_normmod, [(x, 0, D)], [p["n1g"], p["sc1"], p["sh1"]], [(D, BF16)], name=f"normmod1_fwd_{l}")
    proj = r["proj"] = _matmul(r["h"], w["w_in", l], name=f"w_in_fwd_{l}")
    r["cpre"] = _conv_fwd(proj, p["w32"], p["conv_b"], name=f"conv_fwd_{l}")
    (r["cact"],) = _rowop(_fn_lnsilu, [(r["cpre"], 0, CONV_CH)], [p["lng"], p["lnb"]], [(CONV_CH, BF16)],
                          name=f"conv_ln_fwd_{l}")
    arrived = lambda comm, got: w.update({(_BIG[k][0], layer): arr for (k, layer), arr in zip(comm.items, got)})
    (r["hg"], r["states"]), got = _hgrn_fwd(proj, p["lbk"], p["ng"], name=f"hgrn_fwd_{l}", comm=comms[0])
    if comms[0] is not None:
        arrived(comms[0], got)
    (r["sb"], r["rs"]), got = _sb_fwd(proj, p["qg"], p["kg"], name=f"sb_fwd_{l}", comm=comms[1])
    if comms[1] is not None:
        arrived(comms[1], got)
    r["y_c"] = _matmul(r["cact"], w["w_conv_proj", l], name=f"w_conv_proj_fwd_{l}")
    r["y_h"] = _matmul(r["hg"], w["w_hgrn_proj", l], name=f"w_hgrn_proj_fwd_{l}")
    r["y_s"] = _matmul(r["sb"], w["w_sb_proj", l], name=f"w_sb_proj_fwd_{l}")
    (r["merged"],) = _rowop(_fn_merge, [(proj, OFF_GL, 3 * D), (r["y_c"], 0, D), (r["y_h"], 0, D), (r["y_s"], 0, D)],
                            [p["gate_b"]], [(D, BF16)], name=f"merge_fwd_{l}")
    resid = lambda y, x_in, gate: (y,) + _fn_resid(x_in, y, gate)
    r["a_out"], r["x1"] = _matmul(r["merged"], w["w_out", l], name=f"w_out_fwd_{l}", post=resid, extras=[x],
                                  rows=[p["g1"]], out_dtypes=(F32, F32))
    (r["h2"],) = _rowop(_fn_normmod, [(r["x1"], 0, D)], [p["n2g"], p["sc2"], p["sh2"]], [(D, BF16)],
                        name=f"normmod2_fwd_{l}")
    r["u"], r["act"] = _matmul(r["h2"], w["mlp_w1", l], name=f"mlp_w1_fwd_{l}", post=lambda u: (u,) + _fn_relu2(u),
                               out_dtypes=(F32, BF16))
    r["m_out"], x2 = _matmul(r["act"], w["mlp_w2", l], name=f"mlp_w2_fwd_{l}", post=resid, extras=[r["x1"]],
                             rows=[p["g2"]], out_dtypes=(F32, F32))
    return x2, r


def _layer_bwd(dx2, r, w, p, l, grads, carry=None, last=None):
    S, D = dx2.shape
    small = {}

    def dweight(name, a, dy):
        grads[name, l] = _matmul(a, dy, ta=True, name=f"{name}_dw_{l}")

    stage = (lambda k, got: carry(k, got)) if carry is not None else (lambda k, got: None)

    (dm_out,), (dg2,) = _rowop_bwd(_fn_scale, [(r["m_out"], 0, D)], [p["g2"]], [dx2], [BF16], name=f"resid2_bwd_{l}")
    (du,) = _matmul(dm_out, w["mlp_w2", l], tb=True, name=f"mlp_w2_dx_{l}", extras=[r["u"]], out_dtypes=(BF16,),
                    post=lambda dact, u: (dact * (2.0 * jnp.maximum(u, 0.0)),))
    dweight("mlp_w2", r["act"], dm_out)
    dh2 = _matmul(du, w["mlp_w1", l], tb=True, name=f"mlp_w1_dx_{l}")
    dweight("mlp_w1", r["h2"], du)
    (dx1, da_out), (dg1, small["norm2_g"], dsc2, dsh2) = _rowop_bwd(
        _fn_resid_norm, [(r["x"], 0, D), (r["a_out"], 0, D)], [p["g1"], p["n2g"], p["sc2"], p["sh2"]], [dx2, dh2],
        [F32, BF16], name=f"resid1_norm2_bwd_{l}")
    dmerged = _matmul(da_out, w["w_out", l], tb=True, name=f"w_out_dx_{l}")
    dweight("w_out", r["merged"], da_out)
    (dgl, dy_c, dy_h, dy_s), (small["gate_b"],) = _rowop_bwd(
        _fn_merge, [(r["proj"], OFF_GL, 3 * D), (r["y_c"], 0, D), (r["y_h"], 0, D), (r["y_s"], 0, D)], [p["gate_b"]],
        [dmerged], [BF16] * 4, name=f"merge_bwd_{l}")
    dweight("w_conv_proj", r["cact"], dy_c)
    dweight("w_hgrn_proj", r["hg"], dy_h)
    dweight("w_sb_proj", r["sb"], dy_s)
    dcact = _matmul(dy_c, w["w_conv_proj", l], tb=True, name=f"w_conv_proj_dx_{l}")
    (dcpre,), (small["conv_ln_g"], small["conv_ln_b"]) = _rowop_bwd(
        _fn_lnsilu, [(r["cpre"], 0, CONV_CH)], [p["lng"], p["lnb"]], [dcact], [F32], name=f"conv_ln_bwd_{l}")
    (d_conv, dw32, small["conv_b"]), got = _conv_bwd(r["proj"], dcpre, p["w32"], name=f"conv_bwd_{l}",
                                                      comm=stage(0, None))
    small["conv_w"] = dw32[:CONV_WIDTH]
    dhg = _matmul(dy_h, w["w_hgrn_proj", l], tb=True, out_dtype=BF16, name=f"w_hgrn_proj_dx_{l}")
    (dq, df, di, dg, dlbk, dng), got = _hgrn_bwd(r["proj"], r["states"], dhg, p["lbk"], p["ng"], name=f"hgrn_bwd_{l}",
                                                 comm=stage(1, got))
    small["lower"] = -dlbk
    small["hgrn_norm_g"] = jnp.sum(dng, axis=0)
    dsb = _matmul(dy_s, w["w_sb_proj", l], tb=True, out_dtype=BF16, name=f"w_sb_proj_dx_{l}")
    (dsq, dsk, dsv, dqg, dkg), got = _sb_bwd(r["proj"], p["qg"], p["kg"], r["rs"], dsb, name=f"sb_bwd_{l}",
                                             comm=stage(2, got))
    stage(3, got)
    fold = lambda t: jnp.sum(t.reshape(-1, SB_DH), axis=0, keepdims=True)
    small["sb_qn_g"], small["sb_kn_g"] = fold(dqg), fold(dkg)
    dproj = jnp.concatenate([d_conv, dq, df, di, dg, dsq, dsk, dsv, dgl], axis=1)
    dweight("w_in", r["h"], dproj)
    norm1 = functools.partial(_rowop_bwd, _fn_normmod, [(r["x"], 0, D)], [p["n1g"], p["sc1"], p["sh1"]],
                              din_dtypes=[F32], add={0: dx1}, name=f"normmod1_bwd_{l}")
    if last is None:
        dh = _matmul(dproj, w["w_in", l], tb=True, name=f"w_in_dx_{l}")
        (dx,), (small["norm1_g"], dsc1, dsh1) = norm1(douts=[dh])
    else:
        dh, got = _matmul(dproj, w["w_in", l], tb=True, name=f"w_in_dx_{l}", comm=last(0, None))
        (dx,), (small["norm1_g"], dsc1, dsh1), got = norm1(douts=[dh], comm=last(1, got))
        last(2, got)
    small["mod"] = jnp.concatenate([dsh1, dsc1, dg1, dsh2, dsc2, dg2], axis=1)
    return dx, small


def kernel(x, c, mod_w, mod_b, norm1_g, w_in, gate_b, conv_w, conv_b, conv_ln_g, conv_ln_b, w_conv_proj, hgrn_lb, hgrn_norm_g, w_hgrn_proj, sb_qn_g, sb_kn_g, w_sb_proj, w_out, norm2_g, mlp_w1, mlp_w2, loss_target, m_mod_w, m_mod_b, m_norm1_g, m_w_in, m_gate_b, m_conv_w, m_conv_b, m_conv_ln_g, m_conv_ln_b, m_w_conv_proj, m_hgrn_lb, m_hgrn_norm_g, m_w_hgrn_proj, m_sb_qn_g, m_sb_kn_g, m_w_sb_proj, m_w_out, m_norm2_g, m_mlp_w1, m_mlp_w2, v_mod_w, v_mod_b, v_norm1_g, v_w_in, v_gate_b, v_conv_w, v_conv_b, v_conv_ln_g, v_conv_ln_b, v_w_conv_proj, v_hgrn_lb, v_hgrn_norm_g, v_w_hgrn_proj, v_sb_qn_g, v_sb_kn_g, v_w_sb_proj, v_w_out, v_norm2_g, v_mlp_w1, v_mlp_w2):
    given = dict(locals())
    wts = {n: given[n] for n in _WEIGHTS}
    mom = {n: given["m_" + n] for n in _WEIGHTS}
    var = {n: given["v_" + n] for n in _WEIGHTS}
    n_layers, D = norm1_g.shape
    xi, yi, ci = _place()
    q = 2 * xi + yi
    me = 4 * xi + 2 * yi + ci
    place = jnp.stack([q, ci]).astype(jnp.int32)
    n_mod = mod_w.shape[2]
    cw = conv_w.shape[2]

    pk1 = _Pack([("c", c), ("conv_w", conv_w)])
    got1 = _all_gather_small(pk1.array, name="gather_cond")
    c_act = jax.nn.silu(pk1.get(got1, "c")[:, 0, :])
    conv_full = jnp.concatenate([pk1.get(got1, "conv_w")[2 * k] for k in range(4)], axis=-1)

    mod_cols = []
    for l in range(n_layers):
        mb = lax.dynamic_slice_in_dim(mod_b[l], q * n_mod, n_mod)
        mod_cols.append(_matmul(c_act, mod_w, bl=l, name=f"mod_fwd_{l}") + mb[None, :])
    got2 = _all_gather_small(jnp.concatenate(mod_cols, axis=0), name="gather_mod")
    mods = []
    for l in range(n_layers):
        row = lax.dynamic_index_in_dim(got2[0::2], l * 8 + me, axis=1, keepdims=False)
        mods.append(jnp.split(row.reshape(1, 4 * n_mod), 6, axis=1))

    lower, lower_vjp = jax.vjp(_lower_bounds, hgrn_lb)

    shards = [wts[n].astype(BF16) for n, _ in _BIG]
    kinds = [k for _, k in _BIG]
    index = {n: k for k, (n, _) in enumerate(_BIG)}
    first = ["w_in", "w_conv_proj", "w_hgrn_proj", "w_sb_proj"]

    def gather(*names_layers):
        items = [(index[n], l) for names, l in names_layers for n in names if l < n_layers]
        return _Gather(shards, kinds, items) if items else None

    start = gather((first[:1], 0))
    w = {(_BIG[k][0], layer): arr
         for (k, layer), arr in zip(start.items, _run_comm(start, name="gather_first_weights"))}

    def layer_params(l):
        sh1, sc1, g1, sh2, sc2, g2 = mods[l]
        return dict(sh1=sh1, sc1=sc1, g1=g1, sh2=sh2, sc2=sc2, g2=g2, n1g=norm1_g[l][None], n2g=norm2_g[l][None],
                    gate_b=gate_b[l][None], conv_b=conv_b[l][None], lng=conv_ln_g[l][None], lnb=conv_ln_b[l][None],
                    w32=jnp.pad(conv_full[l], ((0, CONV_HALO - CONV_WIDTH), (0, 0))), lbk=(1.0 - lower[l])[None],
                    ng=hgrn_norm_g[l][None], qg=jnp.tile(sb_qn_g[l][None], (1, SB_PAIR // SB_DH)),
                    kg=jnp.tile(sb_kn_g[l][None], (1, SB_PAIR // SB_DH)))

    params = [layer_params(l) for l in range(n_layers)]
    act, saved = x[0], []
    for l in range(n_layers):
        comms = (gather((first[1:] if l == 0 else [], l), (["w_out", "mlp_w1"], l)),
                 gather((["mlp_w2"], l), (first, l + 1)))
        act, r = _layer_fwd(act, w, params[l], l, comms=comms)
        saved.append(r)
    dact, loss_lanes = _loss_head(act, loss_target[0], name="loss_head")

    grads, smalls, reduced = {}, [None] * n_layers, {}

    def reduce_scatter(items, tag):
        return _ReduceScatter([grads[_BIG[k][0], layer] for k, layer in items], [kinds[k] for k, _ in items],
                              [shards[k].shape[1:] for k, _ in items], place, tag)

    def carried(l):
        items_a = [(k, l + 1) for k in range(len(_BIG))]
        items_b = [(k, l) for k, (n, _) in enumerate(_BIG) if n != "w_in"]
        box = boxes.setdefault(l, {})

        def carry(stage, got):
            if stage == 0:
                box["a"], box["b"] = reduce_scatter(items_a, f"l{l + 1}"), reduce_scatter(items_b, f"l{l}")
                box["swaps"] = _Together([box["a"].swap(), box["b"].swap()])
                return box["swaps"]
            if stage == 1:
                lands_a, lands_b = box["swaps"].split(got)
                box["a"].pair_sums(lands_a)
                box["b"].pair_sums(lands_b)
                return box["a"].scatter()
            if stage == 2:
                box["a"].quarter_sums(got)
                box["both"] = _Together([box["a"].share(), box["b"].scatter()])
                return box["both"]
            done_a, parts_b = box["both"].split(got)
            reduced.update(zip(items_a, done_a))
            box["b"].quarter_sums(parts_b)
            box["b_items"] = items_b

        return carry

    def final(l):
        items = [(index["w_in"], l)]
        box = boxes.setdefault(l, {})

        def step(stage, got):
            if stage == 0:
                box["w"] = reduce_scatter(items, "w_in")
                box["w"].pair_sums(_run_comm(box["w"].swap(), name="rs_swap_w_in"))
                box["last"] = _Together([box["w"].scatter()] + ([box["b"].share()] if "b_items" in box else []))
                return box["last"]
            if stage == 1:
                parts = box["last"].split(got)
                if "b_items" in box:
                    reduced.update(zip(box["b_items"], parts[1]))
                box["w"].quarter_sums(parts[0])
                return box["w"].share()
            reduced.update(zip(items, got))

        return step

    boxes = {}

    for l in reversed(range(n_layers)):
        dact, smalls[l] = _layer_bwd(dact, saved[l], w, params[l], l, grads, carried(l) if l + 1 < n_layers else None,
                                     final(l) if l == 0 else None)
    grad_x = dact[None]
    rest = [(k, l) for l in range(n_layers) for k in range(len(_BIG)) if (k, l) not in reduced]
    if rest:
        reduced.update(zip(rest, reduce_scatter(rest, "rest").run()))

    stack = lambda k: jnp.stack([smalls[l][k] for l in range(n_layers)])
    (d_hgrn_lb,) = lower_vjp(stack("lower")[:, 0, :])
    items = [("loss", loss_lanes), ("mod", stack("mod")), ("hgrn_lb", d_hgrn_lb), ("conv_w", stack("conv_w"))]
    items += [(k, stack(k)) for k in ("norm1_g", "gate_b", "conv_b", "conv_ln_g", "conv_ln_b", "hgrn_norm_g", "sb_qn_g",
                                      "sb_kn_g", "norm2_g")]
    pk3 = _Pack(items)

    share_small = _send_to_all(pk3.array)
    delta, new_m, new_v, big, got3 = {}, {}, {}, {}, None
    for n, _ in _BIG:
        outs = None
        for l in reversed(range(n_layers)):
            args = (wts[n], reduced[index[n], l], mom[n], var[n], l, outs)
            if got3 is None:
                outs, (got3,) = _adamw_layer(*args, name=f"adamw_{n}_{l}", comm=share_small)
            else:
                outs = _adamw_layer(*args, name=f"adamw_{n}_{l}")
        big[n] = outs
    got3 = lax.dynamic_update_slice_in_dim(got3, pk3.array[None], me, axis=0)
    tot3 = _sum8(got3, name="sum_small_grads")
    loss = (0.5 / D) * jnp.sum(pk3.get(tot3, "loss"))
    g = {k: pk3.get(tot3, k).reshape(wts[k].shape) for k in _REPLICATED if k != "mod_b"}
    g["mod_b"] = pk3.get(tot3, "mod")[:, 0, :]
    g["conv_w"] = lax.dynamic_slice_in_dim(pk3.get(tot3, "conv_w"), q * cw, cw, axis=2)
    dmod_all = pk3.get(got3, "mod")[:, :, 0, :]
    g_mod_w = None
    for l in range(n_layers):
        cols = lax.dynamic_slice_in_dim(dmod_all[:, l, :], q * n_mod, n_mod, axis=1)
        g_mod_w = _matmul(c_act, cols, ta=True, layer=l, n_layers=n_layers, into=g_mod_w, name=f"mod_dw_{l}")
    g["mod_w"] = g_mod_w

    for n, _ in _BIG:
        g[n], delta[n], new_m[n], new_v[n] = big[n]
    two_d = lambda t: t.reshape(-1, t.shape[-1])
    outs = _adamw(two_d(mod_w), two_d(g["mod_w"]), two_d(m_mod_w), two_d(v_mod_w), name="adamw_mod_w")
    delta["mod_w"], new_m["mod_w"], new_v["mod_w"] = (t.reshape(mod_w.shape) for t in outs)
    rest = _REPLICATED + ["conv_w"]
    packs = [_Pack([(n, src[n]) for n in rest]) for src in (wts, g, mom, var)]
    outs = _adamw(*[pk.array for pk in packs], name="adamw_small")
    for n in rest:
        delta[n], new_m[n], new_v[n] = (packs[0].get(t, n) for t in outs)

    return (loss, grad_x, *[g[n] for n in _WEIGHTS], *[delta[n] for n in _WEIGHTS], *[new_m[n] for n in _WEIGHTS],
            *[new_v[n] for n in _WEIGHTS])
```

```python
import functools
import math

import jax
import jax.numpy as jnp
from jax import lax
from jax.experimental import pallas as pl
from jax.experimental.pallas import tpu as pltpu

F32 = jnp.float32
BF16 = jnp.bfloat16
MESH = pl.DeviceIdType.MESH

EPS = 1e-6
CONV_CH = 512
CONV_WIDTH = 31
CONV_HALO = 32
HG_HEADS = 4
HG_D = 128
HG_CHUNK = 64
HG_KEYS = 8
HG_SUB = 32
SB_HEADS = 8
SB_DH = 64
SB_BLK = 128
SB_PAIR = 128
SB_SKIP = -104.0
OFF_CONV, OFF_HG, OFF_SB, OFF_GL = 0, 1024, 3072, 4608
ADAM_LR, ADAM_B1, ADAM_B2, ADAM_EPS, ADAM_WD, ADAM_STEP = 0.001, 0.9, 0.999, 1e-08, 0.01, 10
VMEM_LIMIT_BYTES = 56 * 1024 * 1024
ROW_TILE = 256


def _cparams(sem=None, **kw):
    return pltpu.CompilerParams(dimension_semantics=sem, vmem_limit_bytes=VMEM_LIMIT_BYTES, **kw)


def _pick(n, cands):
    for c in cands:
        if n % c == 0:
            return c
    return n


MATMUL_VMEM_BUDGET = 40 * 1024 * 1024


def _tile_options(n, cap):
    opts = [t for t in range(cap - cap % 128, 0, -128) if n % t == 0]
    return opts or [n]


def _matmul_tiles(M, N, K, size_a, size_b, size_o, in_acc):
    for tm in _tile_options(M, 1024):
        for tk in _tile_options(K, 2048):
            for tn in _tile_options(N, 1280):
                need = 2 * (tm * tk * size_a + tk * tn * size_b + tm * tn * size_o)
                if K > tk and not in_acc:
                    need += tm * tn * 4
                if need <= MATMUL_VMEM_BUDGET:
                    return tm, tn, tk
    raise ValueError(f"no matmul tiling fits VMEM for {(M, N, K)}")
def _matmul(a, b, *, ta=False, tb=False, bl=None, out_dtype=F32, name, into=None, layer=None, n_layers=None,
            post=None, extras=(), rows=(), out_dtypes=None, comm=None):
    M, K = (a.shape[1], a.shape[0]) if ta else a.shape
    N = b.shape[-2] if tb else b.shape[-1]
    if post is not None:
        return _matmul_post(a, b, M, N, K, ta, tb, post, extras, rows, out_dtypes, name)
    assert comm is None or layer is None
    in_acc = jnp.dtype(out_dtype) == jnp.dtype(F32)
    tm, tn, tk = _matmul_tiles(M, N, K, a.dtype.itemsize, b.dtype.itemsize, jnp.dtype(out_dtype).itemsize, in_acc)
    nk = K // tk
    a_spec = pl.BlockSpec((tk, tm), lambda i, j, k: (k, i)) if ta else pl.BlockSpec((tm, tk), lambda i, j, k: (i, k))
    if bl is None:
        b_spec = pl.BlockSpec((tn, tk), lambda i, j, k: (j, k)) if tb else pl.BlockSpec((tk, tn), lambda i, j, k: (k, j))
    elif tb:
        b_spec = pl.BlockSpec((None, tn, tk), lambda i, j, k: (bl, j, k))
    else:
        b_spec = pl.BlockSpec((None, tk, tn), lambda i, j, k: (bl, k, j))
    dn = (((0 if ta else 1,), (1 if tb else 0,)), ((), ()))

    use_scratch = nk > 1 and not in_acc

    def kern(a_ref, b_ref, *rest):
        o_ref = rest[-2] if use_scratch else rest[-1]
        prod = lambda: lax.dot_general(a_ref[...].astype(BF16), b_ref[...].astype(BF16), dn,
                                       preferred_element_type=F32)
        if nk == 1:
            o_ref[...] = prod().astype(o_ref.dtype).reshape(o_ref.shape)
            return
        acc_ref = rest[-1] if use_scratch else o_ref
        k = pl.program_id(2)

        @pl.when(k == 0)
        def _():
            acc_ref[...] = prod().reshape(acc_ref.shape)

        @pl.when(k > 0)
        def _():
            acc_ref[...] += prod().reshape(acc_ref.shape)

        if use_scratch:
            @pl.when(k == nk - 1)
            def _():
                o_ref[...] = acc_ref[...].astype(o_ref.dtype).reshape(o_ref.shape)

    in_specs, args, aliases = [a_spec, b_spec], [a, b], {}
    if layer is None:
        out_shape = jax.ShapeDtypeStruct((M, N), out_dtype)
        out_spec = pl.BlockSpec((tm, tn), lambda i, j, k: (i, j))
    else:
        out_shape = jax.ShapeDtypeStruct((n_layers, M, N), out_dtype)
        out_spec = pl.BlockSpec((1, tm, tn), lambda i, j, k: (layer, i, j))
        if into is not None:
            in_specs.append(pl.BlockSpec(memory_space=pl.ANY))
            args.append(into)
            aliases = {2: 0}
    if comm is not None:
        (out,), got = _hosted_call(kern, grid=(M // tm, N // tn, nk), in_specs=in_specs, out_specs=[out_spec],
                                   out_shape=[out_shape], scratch_shapes=[pltpu.VMEM((tm, tn), F32)] if use_scratch else [],
                                   args=args, name=name, comm=comm)
        return out, got
    return pl.pallas_call(
        kern, grid=(M // tm, N // tn, nk), in_specs=in_specs, out_specs=out_spec, out_shape=out_shape,
        scratch_shapes=[pltpu.VMEM((tm, tn), F32)] if use_scratch else [],
        input_output_aliases=aliases, name=name,
        compiler_params=_cparams(("parallel", "parallel", "arbitrary")))(*args)


def _matmul_post(a, b, M, N, K, ta, tb, post, extras, rows, out_dtypes, name):
    per_elem = sum(e.dtype.itemsize for e in extras) + sum(jnp.dtype(d).itemsize for d in out_dtypes)
    fits = lambda tm, tn: 2 * (tm * K * a.dtype.itemsize + K * tn * b.dtype.itemsize + tm * tn * per_elem) <= MATMUL_VMEM_BUDGET
    tm, tn = next((tm, tn) for tm in _tile_options(M, 1024) for tn in _tile_options(N, 1280) if fits(tm, tn))
    a_spec = pl.BlockSpec((K, tm), lambda i, j: (0, i)) if ta else pl.BlockSpec((tm, K), lambda i, j: (i, 0))
    b_spec = pl.BlockSpec((tn, K), lambda i, j: (j, 0)) if tb else pl.BlockSpec((K, tn), lambda i, j: (0, j))
    tile = pl.BlockSpec((tm, tn), lambda i, j: (i, j))
    row = pl.BlockSpec((1, tn), lambda i, j: (0, j))
    dn = (((0 if ta else 1,), (1 if tb else 0,)), ((), ()))
    n_ex = len(extras) + len(rows)

    def kern(a_ref, b_ref, *rest):
        prod = lax.dot_general(a_ref[...].astype(BF16), b_ref[...].astype(BF16), dn, preferred_element_type=F32)
        res = post(prod, *[r[...].astype(F32) for r in rest[:n_ex]])
        for val, o_ref in zip(res, rest[n_ex:]):
            o_ref[...] = val.astype(o_ref.dtype)

    return pl.pallas_call(
        kern, grid=(M // tm, N // tn), in_specs=[a_spec, b_spec] + [tile] * len(extras) + [row] * len(rows),
        out_specs=[tile] * len(out_dtypes), out_shape=[jax.ShapeDtypeStruct((M, N), d) for d in out_dtypes], name=name,
        compiler_params=_cparams(("parallel", "parallel")))(a, b, *extras, *rows)


def _col_specs(off, width, T):
    bw = math.gcd(width, off) if off else width
    return [pl.BlockSpec((T, bw), functools.partial(lambda i, c: (i, c), c=off // bw + p)) for p in range(width // bw)]


def _gather_rows(refs, counts):
    vals, pos = [], 0
    for n in counts:
        parts = [refs[pos + p][...].astype(F32) for p in range(n)]
        pos += n
        vals.append(parts[0] if n == 1 else jnp.concatenate(parts, axis=1))
    return vals, pos


def _rowop(fn, ins, params, outs, *, name):
    S = ins[0][0].shape[0]
    T = min(ROW_TILE, S)
    in_specs, counts, args = [], [], []
    for arr, off, width in ins:
        sp = _col_specs(off, width, T)
        in_specs += sp
        counts.append(len(sp))
        args += [arr] * len(sp)
    in_specs += [pl.BlockSpec(p.shape, lambda i: (0, 0)) for p in params]

    def kern(*refs):
        vals, pos = _gather_rows(refs, counts)
        pv = [refs[pos + p][...] for p in range(len(params))]
        pos += len(params)
        res = fn(*vals, *pv)
        for r, o_ref in zip(res, refs[pos:]):
            o_ref[...] = r.astype(o_ref.dtype)

    return pl.pallas_call(
        kern, grid=(S // T,), in_specs=in_specs,
        out_specs=[pl.BlockSpec((T, w), lambda i: (i, 0)) for w, _ in outs],
        out_shape=[jax.ShapeDtypeStruct((S, w), dt) for w, dt in outs],
        name=name, compiler_params=_cparams(("parallel",)))(*args, *params)


def _rowop_bwd(fn, ins, params, douts, din_dtypes, *, name, add=None, comm=None):
    add = add or {}
    S = ins[0][0].shape[0]
    T = min(ROW_TILE, S)
    in_specs, counts, args = [], [], []
    for arr, off, width in ins:
        sp = _col_specs(off, width, T)
        in_specs += sp
        counts.append(len(sp))
        args += [arr] * len(sp)
    in_specs += [pl.BlockSpec(p.shape, lambda i: (0, 0)) for p in params]
    in_specs += [pl.BlockSpec((T, d.shape[1]), lambda i: (i, 0)) for d in douts]
    add_keys = sorted(add)
    in_specs += [pl.BlockSpec((T, add[k].shape[1]), lambda i: (i, 0)) for k in add_keys]
    want = [k for k, dt in enumerate(din_dtypes) if dt is not None]

    def kern(*refs):
        vals, pos = _gather_rows(refs, counts)
        pv = [refs[pos + p][...] for p in range(len(params))]
        pos += len(params)
        cts = [refs[pos + p][...].astype(F32) for p in range(len(douts))]
        pos += len(douts)
        adds = {k: refs[pos + p][...].astype(F32) for p, k in enumerate(add_keys)}
        pos += len(add_keys)
        _, vjp = jax.vjp(fn, *vals, *pv)
        grads = vjp(tuple(cts))
        for k in want:
            g = grads[k] + adds[k] if k in adds else grads[k]
            refs[pos][...] = g.astype(refs[pos].dtype)
            pos += 1
        first = pl.program_id(0) == 0
        for p in range(len(params)):
            gp, o_ref = grads[len(ins) + p], refs[pos + p]

            @pl.when(first)
            def _(gp=gp, o_ref=o_ref):
                o_ref[...] = gp

            @pl.when(jnp.logical_not(first))
            def _(gp=gp, o_ref=o_ref):
                o_ref[...] += gp

    out_specs = [pl.BlockSpec((T, ins[k][2]), lambda i: (i, 0)) for k in want]
    out_specs += [pl.BlockSpec(p.shape, lambda i: (0, 0)) for p in params]
    out_shape = [jax.ShapeDtypeStruct((S, ins[k][2]), din_dtypes[k]) for k in want]
    out_shape += [jax.ShapeDtypeStruct(p.shape, F32) for p in params]
    res, got = _hosted_call(
        kern, grid=(S // T,), in_specs=in_specs, out_specs=out_specs, out_shape=out_shape, scratch_shapes=[],
        args=[*args, *params, *douts, *[add[k] for k in add_keys]], name=name, comm=comm, sem=("arbitrary",))
    dins = [None] * len(ins)
    for p, k in enumerate(want):
        dins[k] = res[p]
    return (dins, res[len(want):]) if comm is None else (dins, res[len(want):], got)


def _rms(x, g):
    return x * lax.rsqrt(jnp.mean(x * x, axis=-1, keepdims=True) + EPS) * g


def _fn_normmod(x, g, sc, sh):
    return (_rms(x, g) * (1.0 + sc) + sh,)


def _fn_lnsilu(c, g, b):
    mu = jnp.mean(c, axis=-1, keepdims=True)
    var = jnp.mean(jnp.square(c - mu), axis=-1, keepdims=True)
    y = (c - mu) * lax.rsqrt(var + EPS) * g + b
    return (y * jax.nn.sigmoid(y),)


def _fn_merge(gl, yc, yh, ys, gb):
    d = yc.shape[1]
    g = jax.nn.sigmoid(gl + gb)
    return (g[:, :d] * yc + g[:, d:2 * d] * yh + g[:, 2 * d:] * ys,)


def _fn_resid(x, y, g):
    return (x + g * y,)


def _fn_resid_norm(x, y, g, n, sc, sh):
    x1 = x + g * y
    return (x1,) + _fn_normmod(x1, n, sc, sh)


def _fn_scale(y, g):
    return (g * y,)


def _fn_relu2(u):
    return (jnp.square(jnp.maximum(u, 0.0)),)


def _conv_specs(S, T):
    r = T // CONV_HALO
    cur = [pl.BlockSpec((T, CONV_CH), lambda i: (i, 0)), pl.BlockSpec((T, CONV_CH), lambda i: (i, 1))]
    prev = [pl.BlockSpec((CONV_HALO, CONV_CH), lambda i: (jnp.maximum(i * r - 1, 0), 0)),
            pl.BlockSpec((CONV_HALO, CONV_CH), lambda i: (jnp.maximum(i * r - 1, 0), 1))]
    return cur + prev


def _glu_ext(a_ref, g_ref, ah_ref, gh_ref):
    a = a_ref[...]
    sg = jax.nn.sigmoid(g_ref[...])
    uh = jnp.where(pl.program_id(0) > 0, ah_ref[...] * jax.nn.sigmoid(gh_ref[...]), 0.0)
    return a, sg, jnp.concatenate([uh, a * sg], axis=0)


def _shift_up(xe, k, T):
    return xe[:T] if k == 0 else pltpu.roll(xe, shift=xe.shape[0] - k, axis=0)[:T]


def _conv_fwd(proj, w32, b, *, name):
    S = proj.shape[0]
    T = min(ROW_TILE, S)
    lead = CONV_HALO - (CONV_WIDTH - 1)

    def kern(a_ref, g_ref, ah_ref, gh_ref, w_ref, b_ref, o_ref):
        _, _, ue = _glu_ext(a_ref, g_ref, ah_ref, gh_ref)
        acc = jnp.zeros((T, CONV_CH), F32) + b_ref[...]
        for j in range(CONV_WIDTH):
            acc = acc + w_ref[j:j + 1, :] * _shift_up(ue, lead + j, T)
        o_ref[...] = acc

    const = lambda shape: pl.BlockSpec(shape, lambda i: (0, 0))
    return pl.pallas_call(
        kern, grid=(S // T,), in_specs=_conv_specs(S, T) + [const(w32.shape), const(b.shape)],
        out_specs=pl.BlockSpec((T, CONV_CH), lambda i: (i, 0)),
        out_shape=jax.ShapeDtypeStruct((S, CONV_CH), F32), name=name,
        compiler_params=_cparams(("parallel",)))(proj, proj, proj, proj, w32, b)


def _conv_bwd(proj, dc, w32, *, name, comm=None):
    S = proj.shape[0]
    T = min(ROW_TILE, S)
    nt = S // T
    r = T // CONV_HALO
    lead = CONV_HALO - (CONV_WIDTH - 1)
    last_halo = S // CONV_HALO - 1

    def kern(a_ref, g_ref, ah_ref, gh_ref, dc_ref, dcn_ref, w_ref, dag_ref, dw_ref, db_ref):
        i = pl.program_id(0)
        a, sg, ue = _glu_ext(a_ref, g_ref, ah_ref, gh_ref)
        dc_t = dc_ref[...]
        de = jnp.concatenate([dc_t, jnp.where(i < nt - 1, dcn_ref[...], 0.0)], axis=0)

        @pl.when(i == 0)
        def _():
            dw_ref[...] = jnp.zeros_like(dw_ref)
            db_ref[...] = jnp.zeros_like(db_ref)

        du = jnp.zeros((T, CONV_CH), F32)
        for j in range(CONV_WIDTH):
            du = du + w_ref[j:j + 1, :] * _shift_up(de, CONV_WIDTH - 1 - j, T)
            dw_ref[j:j + 1, :] += jnp.sum(dc_t * _shift_up(ue, lead + j, T), axis=0, keepdims=True)
        db_ref[...] += jnp.sum(dc_t, axis=0, keepdims=True)
        dag_ref[:, :CONV_CH] = (du * sg).astype(BF16)
        dag_ref[:, CONV_CH:] = (du * a * sg * (1.0 - sg)).astype(BF16)

    const = lambda shape: pl.BlockSpec(shape, lambda i: (0, 0))
    in_specs = _conv_specs(S, T) + [
        pl.BlockSpec((T, CONV_CH), lambda i: (i, 0)),
        pl.BlockSpec((CONV_HALO, CONV_CH), lambda i: (jnp.minimum((i + 1) * r, last_halo), 0)),
        const(w32.shape)]
    return _hosted_call(
        kern, grid=(nt,), in_specs=in_specs,
        out_specs=[pl.BlockSpec((T, 2 * CONV_CH), lambda i: (i, 0)), const(w32.shape), const((1, CONV_CH))],
        out_shape=[jax.ShapeDtypeStruct((S, 2 * CONV_CH), BF16), jax.ShapeDtypeStruct(w32.shape, F32),
                   jax.ShapeDtypeStruct((1, CONV_CH), F32)],
        scratch_shapes=[], args=[proj, proj, proj, proj, dc, dc, w32], name=name, comm=comm, sem=("arbitrary",))


def _iota2(shape, dim):
    return lax.broadcasted_iota(jnp.int32, shape, dim)


def _running(x, seg, later):
    n = x.shape[0]
    pos = _iota2(x.shape, 0) & (seg - 1)
    k = 1
    while k < seg:
        if later:
            x = x + jnp.where(pos < seg - k, pltpu.roll(x, n - k, axis=0), 0.0)
        else:
            x = x + jnp.where(pos >= k, pltpu.roll(x, k, axis=0), 0.0)
        k *= 2
    return x


@functools.partial(jax.custom_vjp, nondiff_argnums=(1,))
def _prefix(x, seg):
    return _running(x, seg, False)


_prefix.defvjp(lambda x, seg: (_running(x, seg, False), None), lambda seg, _, g: (_running(g, seg, True),))


def _hg_chunk(q, f, iv, g, st, lbk, ng):
    n, sub = HG_CHUNK, HG_SUB
    kk = lbk * jax.nn.sigmoid(-f)
    lf = jnp.log(1.0 - kk)
    b = _prefix(lf, n)
    bs = _prefix(lf, sub)
    bt = jnp.sum(lf, axis=0, keepdims=True)
    qh = q * jax.nn.sigmoid(q)
    dot_nt = lambda x, y: lax.dot_general(x.astype(BF16), y.astype(BF16), (((1,), (1,)), ((), ())),
                                          preferred_element_type=F32)
    o = dot_nt(qh * jnp.exp(b), st)
    b0 = b - bs
    qs = qh * jnp.exp(bs)
    col = _iota2((sub, n), 1)
    rows = []
    for blk in range(n // sub):
        lo = blk * sub
        sl = slice(lo, lo + sub)
        acc = o[sl]
        if blk > 0:
            ref = jnp.concatenate([b0[sl]] * (n // sub), axis=0)
            kd = kk * jnp.exp(jnp.minimum(ref - b, 0.0))
            sc = jnp.where(col < lo, dot_nt(qs[sl], kd), 0.0)
            acc = acc + jnp.dot(sc.astype(BF16), iv.astype(BF16), preferred_element_type=F32)
        for t0 in range(0, sub, HG_KEYS):
            keys, qrys = slice(lo + t0, lo + t0 + HG_KEYS), slice(lo + t0, lo + sub)
            nt = sub - t0
            bq, bk = bs[qrys][None, :, :], bs[keys][:, None, :]
            s_i = lax.broadcasted_iota(jnp.int32, (HG_KEYS, nt, HG_D), 0) + t0
            t_i = lax.broadcasted_iota(jnp.int32, (HG_KEYS, nt, HG_D), 1) + t0
            keep = s_i <= t_i
            p = jnp.where(keep, qh[qrys][None, :, :] * kk[keys][:, None, :] * jnp.exp(jnp.where(keep, bq - bk, 0.0)), 0.0)
            w = jnp.sum(p, axis=-1, keepdims=True)
            part = jnp.sum(w * iv[keys][:, None, :], axis=0)
            acc = acc + (part if t0 == 0 else jnp.concatenate([jnp.zeros((t0, HG_D), F32), part], axis=0))
        rows.append(acc)
    o = jnp.concatenate(rows, axis=0)
    kd = kk * jnp.exp(bt - b)
    st_new = jnp.exp(bt) * st + lax.dot_general(iv.astype(BF16), kd.astype(BF16), (((0,), (0,)), ((), ())),
                                                     preferred_element_type=F32)
    out = _rms(o, ng) * (g * jax.nn.sigmoid(g))
    return out, st_new


def _hg_tile(S):
    return min(512, S)


def _hg_in_specs(rt, rev, nr):
    width = HG_HEADS * HG_D
    base = OFF_HG // width
    row = (lambda r: nr - 1 - r) if rev else (lambda r: r)
    return [pl.BlockSpec((rt, width), functools.partial(lambda r, k: (row(r), base + k), k=k)) for k in range(4)]


def _hg_cols(h):
    return slice(h * HG_D, (h + 1) * HG_D)


def _hgrn_fwd(proj, lbk, ng, *, name, comm=None):
    S = proj.shape[0]
    rt = _hg_tile(S)
    nr, nc = S // rt, rt // HG_CHUNK

    def kern(q_ref, f_ref, i_ref, g_ref, lbk_ref, ng_ref, o_ref, st_out_ref, st_ref):
        @pl.when(pl.program_id(0) == 0)
        def _():
            st_ref[...] = jnp.zeros_like(st_ref)

        def body(c, carry):
            rows = pl.ds(pl.multiple_of(c * HG_CHUNK, HG_CHUNK), HG_CHUNK)
            for h in range(HG_HEADS):
                cols = _hg_cols(h)
                st = st_ref[h]
                st_out_ref[h, c] = st
                out, st_new = _hg_chunk(q_ref[rows, cols], f_ref[rows, cols], i_ref[rows, cols], g_ref[rows, cols], st,
                                        lbk_ref[:, cols], ng_ref[...])
                o_ref[rows, cols] = out.astype(o_ref.dtype)
                st_ref[h] = st_new
            return carry

        lax.fori_loop(0, nc, body, 0)

    width = HG_HEADS * HG_D
    in_specs = _hg_in_specs(rt, False, nr) + [pl.BlockSpec((1, width), lambda r: (0, 0)),
                                               pl.BlockSpec((1, HG_D), lambda r: (0, 0))]
    return _hosted_call(
        kern, grid=(nr,), in_specs=in_specs,
        out_specs=[pl.BlockSpec((rt, width), lambda r: (r, 0)),
                   pl.BlockSpec((HG_HEADS, nc, HG_D, HG_D), lambda r: (0, r, 0, 0))],
        out_shape=[jax.ShapeDtypeStruct((S, width), BF16),
                   jax.ShapeDtypeStruct((HG_HEADS, S // HG_CHUNK, HG_D, HG_D), F32)],
        scratch_shapes=[pltpu.VMEM((HG_HEADS, HG_D, HG_D), F32)],
        args=[proj, proj, proj, proj, lbk, ng], name=name, comm=comm, sem=("arbitrary",))


def _hgrn_bwd(proj, states, dout, lbk, ng, *, name, comm=None):
    S = proj.shape[0]
    rt = _hg_tile(S)
    nr, nc = S // rt, rt // HG_CHUNK
    width = HG_HEADS * HG_D

    def kern(q_ref, f_ref, i_ref, g_ref, st_in_ref, do_ref, lbk_ref, ng_ref,
             dq_ref, df_ref, di_ref, dg_ref, dlbk_ref, dng_ref, dst_ref):
        @pl.when(pl.program_id(0) == 0)
        def _():
            dst_ref[...] = jnp.zeros_like(dst_ref)
            dlbk_ref[...] = jnp.zeros_like(dlbk_ref)
            dng_ref[...] = jnp.zeros_like(dng_ref)

        def body(k, carry):
            c = nc - 1 - k
            rows = pl.ds(pl.multiple_of(c * HG_CHUNK, HG_CHUNK), HG_CHUNK)
            for h in range(HG_HEADS):
                cols = _hg_cols(h)
                _, vjp = jax.vjp(_hg_chunk, q_ref[rows, cols], f_ref[rows, cols], i_ref[rows, cols], g_ref[rows, cols],
                                 st_in_ref[h, c], lbk_ref[:, cols], ng_ref[...])
                dq, df, di, dg, dst, dlbk, dng = vjp((do_ref[rows, cols].astype(F32), dst_ref[h]))
                dq_ref[rows, cols] = dq.astype(BF16)
                df_ref[rows, cols] = df.astype(BF16)
                di_ref[rows, cols] = di.astype(BF16)
                dg_ref[rows, cols] = dg.astype(BF16)
                dst_ref[h] = dst
                dlbk_ref[:, cols] += dlbk
                dng_ref[h] += dng
            return carry

        lax.fori_loop(0, nc, body, 0)

    rev = lambda r: nr - 1 - r
    tile = pl.BlockSpec((rt, width), lambda r: (rev(r), 0))
    in_specs = _hg_in_specs(rt, True, nr) + [
        pl.BlockSpec((HG_HEADS, nc, HG_D, HG_D), lambda r: (0, rev(r), 0, 0)), tile,
        pl.BlockSpec((1, width), lambda r: (0, 0)), pl.BlockSpec((1, HG_D), lambda r: (0, 0))]
    return _hosted_call(
        kern, grid=(nr,), in_specs=in_specs,
        out_specs=[tile, tile, tile, tile, pl.BlockSpec((1, width), lambda r: (0, 0)),
                   pl.BlockSpec((HG_HEADS, 1, HG_D), lambda r: (0, 0, 0))],
        out_shape=[jax.ShapeDtypeStruct((S, width), BF16)] * 4 + [
            jax.ShapeDtypeStruct((1, width), F32), jax.ShapeDtypeStruct((HG_HEADS, 1, HG_D), F32)],
        scratch_shapes=[pltpu.VMEM((HG_HEADS, HG_D, HG_D), F32)],
        args=[proj, proj, proj, proj, states, dout, lbk, ng], name=name, comm=comm, sem=("arbitrary",))


def _sb_scores(km, qi):
    return lax.dot_general(km, qi, (((1,), (1,)), ((), ())), preferred_element_type=F32)


def _sb_weights(zt, r_run, diag):
    n = SB_BLK
    sp = jnp.maximum(zt, 0.0) + jnp.log(1.0 + jnp.exp(-jnp.abs(zt)))
    lk = -sp
    if diag:
        keep = (_iota2(zt.shape, 0) & (n - 1)) < _iota2(zt.shape, 1)
        lk = jnp.where(keep, lk, 0.0)
    tails = [_running(lk[a * n:(a + 1) * n], n, True) for a in range(2)]
    between = jnp.concatenate([tails[a] + r_run[a] for a in range(2)], axis=0)
    wgt = jnp.exp(zt + between)
    if diag:
        wgt = jnp.where(keep, wgt, 0.0)
    return sp, wgt, [t[0:1, :] for t in tails]


def _sb_norm_pair(x, g2, lane_lo):
    sq = x * x
    ms_lo = jnp.sum(jnp.where(lane_lo, sq, 0.0), axis=-1, keepdims=True)
    ms_hi = jnp.sum(jnp.where(lane_lo, 0.0, sq), axis=-1, keepdims=True)
    return x * lax.rsqrt(jnp.where(lane_lo, ms_lo, ms_hi) * (1.0 / SB_DH) + EPS) * g2


def _sb_specs(S):
    base = OFF_SB // SB_PAIR
    per = SB_HEADS * SB_DH // SB_PAIR
    cols = [pl.BlockSpec((S, SB_PAIR), functools.partial(lambda p, k: (0, base + per * k + p), k=k)) for k in range(3)]
    return cols + [pl.BlockSpec((1, SB_PAIR), lambda p: (0, 0))] * 2


def _sb_rows(i):
    return pl.ds(pl.multiple_of(i * SB_BLK, SB_BLK), SB_BLK)


def _sb_both(j, a=None):
    if a is None:
        return pl.ds(pl.multiple_of(j * 2 * SB_BLK, 2 * SB_BLK), 2 * SB_BLK)
    return pl.ds(pl.multiple_of(j * 2 * SB_BLK + a * SB_BLK, SB_BLK), SB_BLK)


def _sb_fwd(proj, qg2, kg2, *, name, comm=None):
    S = proj.shape[0]
    nb = S // SB_BLK
    scale = SB_DH ** -0.5
    n_pairs = SB_HEADS * SB_DH // SB_PAIR

    def kern(q_ref, k_ref, v_ref, qg_ref, kg_ref, o_ref, rs_ref, qp_ref, km_ref, vt_ref):
        lane_lo = _iota2((SB_BLK, SB_PAIR), 1) < SB_DH

        def prologue(j, carry):
            rows = _sb_rows(j)
            qp_ref[rows, :] = (_sb_norm_pair(q_ref[rows, :], qg_ref[...], lane_lo) * scale).astype(BF16)
            kn = _sb_norm_pair(k_ref[rows, :], kg_ref[...], lane_lo)
            v = v_ref[rows, :]
            for a, mine in enumerate((lane_lo, jnp.logical_not(lane_lo))):
                km_ref[_sb_both(j, a), :] = jnp.where(mine, kn, 0.0).astype(BF16)
                vt_ref[:, _sb_both(j, a)] = jnp.where(mine, v, 0.0).T.astype(BF16)
            return carry

        lax.fori_loop(0, nb, prologue, 0)

        diagonal = lambda i: _sb_scores(km_ref[_sb_both(i), :], qp_ref[_sb_rows(i), :])

        def qblock(i, zt):
            qi = qp_ref[_sb_rows(i), :]

            scores = lambda j: _sb_scores(km_ref[_sb_both(jnp.maximum(j, 0)), :], qi)
            output = lambda j, wgt: jnp.dot(vt_ref[:, _sb_both(j)], wgt, preferred_element_type=F32)

            def note(j, r_run):
                for a in range(2):
                    rs_ref[a, i, pl.ds(j, 1), :] = r_run[a]
                return jnp.maximum(jnp.max(r_run[0]), jnp.max(r_run[1])) > SB_SKIP

            def noted(j, r_run):
                return lax.cond(j >= 0, lambda: note(j, r_run).astype(jnp.int32), lambda: jnp.int32(0))

            zero = jnp.zeros((1, SB_BLK), F32)
            z_next = scores(i - 1)
            _, wgt, r_run = _sb_weights(zt, [zero, zero], True)
            go = noted(i - 1, r_run)

            def body(c):
                j, _, acc, r_run, zt, j_prev, w_prev = c
                z_next = scores(j - 1)
                acc = acc + output(j_prev, w_prev)
                _, wgt, lk_sum = _sb_weights(zt, r_run, False)
                r_run = [r_run[a] + lk_sum[a] for a in range(2)]
                return j - 1, noted(j - 1, r_run), acc, r_run, z_next, j, wgt.astype(BF16)

            c = (i - 1, go, jnp.zeros((SB_PAIR, SB_BLK), F32), r_run, z_next, i, wgt.astype(BF16))
            _, _, acc, _, _, j_prev, w_prev = lax.while_loop(lambda c: c[1] > 0, body, c)
            rs_ref[0, i, pl.ds(i, 1), :] = jnp.full((1, SB_BLK), j_prev, jnp.int32).astype(F32)
            zt = diagonal(jnp.minimum(i + 1, nb - 1))
            o_ref[_sb_rows(i), :] = (acc + output(j_prev, w_prev)).T.astype(o_ref.dtype)
            return zt

        lax.fori_loop(0, nb, qblock, diagonal(0))

    width = SB_HEADS * SB_DH
    return _hosted_call(
        kern, grid=(n_pairs,), in_specs=_sb_specs(S),
        out_specs=[pl.BlockSpec((S, SB_PAIR), lambda p: (0, p)),
                   pl.BlockSpec((2, nb, nb, SB_BLK), lambda p: (p, 0, 0, 0))],
        out_shape=[jax.ShapeDtypeStruct((S, width), BF16), jax.ShapeDtypeStruct((SB_HEADS, nb, nb, SB_BLK), F32)],
        scratch_shapes=[pltpu.VMEM((S, SB_PAIR), BF16), pltpu.VMEM((2 * S, SB_PAIR), BF16),
                        pltpu.VMEM((SB_PAIR, 2 * S), BF16)],
        args=[proj, proj, proj, qg2, kg2], name=name, comm=comm, sem=("parallel",))


def _sb_bwd(proj, qg2, kg2, rs, do, *, name, comm=None):
    S = proj.shape[0]
    nb = S // SB_BLK
    scale = SB_DH ** -0.5
    n_pairs = SB_HEADS * SB_DH // SB_PAIR

    def kern(q_ref, k_ref, v_ref, qg_ref, kg_ref, rs_ref, do_ref, dq_ref, dk_ref, dv_ref, dqg_ref, dkg_ref,
             qp_ref, km_ref, kt_ref, vm_ref, dqn_ref, dkn_ref, dvs_ref):
        lane_lo = _iota2((SB_BLK, SB_PAIR), 1) < SB_DH
        heads = (lane_lo, jnp.logical_not(lane_lo))
        fn_q = lambda x, g: _sb_norm_pair(x, g, lane_lo) * scale
        fn_k = lambda x, g: _sb_norm_pair(x, g, lane_lo)

        def prologue(j, carry):
            rows = _sb_rows(j)
            qp_ref[rows, :] = fn_q(q_ref[rows, :], qg_ref[...]).astype(BF16)
            kn = fn_k(k_ref[rows, :], kg_ref[...])
            v = v_ref[rows, :]
            for a, mine in enumerate(heads):
                k_a = jnp.where(mine, kn, 0.0)
                km_ref[_sb_both(j, a), :] = k_a.astype(BF16)
                kt_ref[:, _sb_both(j, a)] = k_a.T.astype(BF16)
                vm_ref[_sb_both(j, a), :] = jnp.where(mine, v, 0.0).astype(BF16)
            return carry

        lax.fori_loop(0, nb, prologue, 0)
        dkn_ref[...] = jnp.zeros_like(dkn_ref)
        dvs_ref[...] = jnp.zeros_like(dvs_ref)

        def leftmost(i):
            return jnp.clip(jnp.max(rs_ref[0, i, pl.ds(i, 1), :]).astype(jnp.int32), 0, i)

        def opening_of(i, j):
            jc = jnp.minimum(j, i)
            return (_sb_scores(km_ref[_sb_both(jc), :], qp_ref[_sb_rows(i), :]),
                    lax.dot_general(vm_ref[_sb_both(jc), :], do_ref[_sb_rows(i), :], (((1,), (1,)), ((), ())),
                                    preferred_element_type=F32))

        def qblock(i, carry):
            first, zt, dp = carry
            qi = qp_ref[_sb_rows(i), :]
            doi = do_ref[_sb_rows(i), :]

            opening = functools.partial(opening_of, i)

            def closing(j, dzb, wgtb, dqa):
                dkn_ref[_sb_both(j), :] += jnp.dot(dzb, qi, preferred_element_type=F32)
                dvs_ref[_sb_both(j), :] += jnp.dot(wgtb, doi, preferred_element_type=F32)
                return dqa + jnp.dot(kt_ref[:, _sb_both(j)], dzb, preferred_element_type=F32)

            def middle(j, diag, zt, dp, e_run):
                zero = jnp.zeros((1, SB_BLK), F32)
                r_run = [zero, zero] if diag else [rs_ref[a, i, pl.ds(j, 1), :] for a in range(2)]
                sp, wgt, _ = _sb_weights(zt, r_run, diag)
                e = dp * wgt
                heads_e = [_running(e[a * SB_BLK:(a + 1) * SB_BLK], SB_BLK, False) for a in range(2)]
                e_left = jnp.concatenate([heads_e[a] + e_run[a] for a in range(2)], axis=0) - e
                s_neg = jnp.exp(-sp)
                dz = e * s_neg - e_left * (1.0 - s_neg)
                if diag:
                    dz = jnp.where((_iota2(dz.shape, 0) & (SB_BLK - 1)) < _iota2(dz.shape, 1), dz, 0.0)
                return dz.astype(BF16), wgt.astype(BF16), [e_run[a] + heads_e[a][SB_BLK - 1:SB_BLK, :] for a in range(2)]

            def body(j, c):
                dqa, e_run, zt, dp, j_prev, dzb, wgtb = c
                nxt = opening(j + 1)
                dqa = closing(j_prev, dzb, wgtb, dqa)
                dzb, wgtb, e_run = middle(j, False, zt, dp, e_run)
                return (dqa, e_run) + nxt + (j, dzb, wgtb)

            zero = jnp.zeros((1, SB_BLK), F32)
            none = jnp.zeros((2 * SB_BLK, SB_BLK), BF16)
            c = (jnp.zeros((SB_PAIR, SB_BLK), F32), [zero, zero], zt, dp, first, none, none)
            dqa, e_run, zt, dp, j_prev, dzb, wgtb = lax.fori_loop(first, i, body, c)
            dqa = closing(j_prev, dzb, wgtb, dqa)
            dzb, wgtb, _ = middle(i, True, zt, dp, e_run)
            i_next = jnp.minimum(i + 1, nb - 1)
            first_next = leftmost(i_next)
            nxt = opening_of(i_next, first_next)
            dqn_ref[_sb_rows(i), :] = closing(i, dzb, wgtb, dqa).T
            return (first_next,) + nxt

        lax.fori_loop(0, nb, qblock, (leftmost(0),) + opening_of(0, 0))
        dqg_ref[...] = jnp.zeros_like(dqg_ref)
        dkg_ref[...] = jnp.zeros_like(dkg_ref)

        def epilogue(j, carry):
            rows = _sb_rows(j)
            _, vjp_q = jax.vjp(fn_q, q_ref[rows, :], qg_ref[...])
            dq, dqg = vjp_q(dqn_ref[rows, :])
            _, vjp_k = jax.vjp(fn_k, k_ref[rows, :], kg_ref[...])
            dk, dkg = vjp_k(jnp.where(lane_lo, dkn_ref[_sb_both(j, 0), :], dkn_ref[_sb_both(j, 1), :]))
            dq_ref[rows, :] = dq.astype(BF16)
            dk_ref[rows, :] = dk.astype(BF16)
            dv_ref[rows, :] = jnp.where(lane_lo, dvs_ref[_sb_both(j, 0), :], dvs_ref[_sb_both(j, 1), :]).astype(BF16)
            dqg_ref[0] += dqg
            dkg_ref[0] += dkg
            return carry

        lax.fori_loop(0, nb, epilogue, 0)

    width = SB_HEADS * SB_DH
    pair = pl.BlockSpec((S, SB_PAIR), lambda p: (0, p))
    dgain = pl.BlockSpec((1, 1, SB_PAIR), lambda p: (p, 0, 0))
    in_specs = _sb_specs(S) + [pl.BlockSpec((2, nb, nb, SB_BLK), lambda p: (p, 0, 0, 0)), pair]
    return _hosted_call(
        kern, grid=(n_pairs,), in_specs=in_specs, out_specs=[pair, pair, pair, dgain, dgain],
        out_shape=[jax.ShapeDtypeStruct((S, width), BF16)] * 3 + [jax.ShapeDtypeStruct((n_pairs, 1, SB_PAIR), F32)] * 2,
        scratch_shapes=[pltpu.VMEM((S, SB_PAIR), BF16), pltpu.VMEM((2 * S, SB_PAIR), BF16), pltpu.VMEM((SB_PAIR, 2 * S), BF16),
                        pltpu.VMEM((2 * S, SB_PAIR), BF16), pltpu.VMEM((S, SB_PAIR), F32),
                        pltpu.VMEM((2 * S, SB_PAIR), F32), pltpu.VMEM((2 * S, SB_PAIR), F32)],
        args=[proj, proj, proj, qg2, kg2, rs, do], name=name, comm=comm, sem=("parallel",))


def _loss_head(y, target, *, name):
    S, D = y.shape
    T = min(ROW_TILE, S)

    def kern(y_ref, t_ref, dy_ref, acc_ref):
        err = y_ref[...] - t_ref[...]
        dy_ref[...] = err * (1.0 / D)
        col = jnp.sum(err * err, axis=0, keepdims=True)
        part = sum(col[:, k * 128:(k + 1) * 128] for k in range(D // 128))

        @pl.when(pl.program_id(0) == 0)
        def _():
            acc_ref[...] = part

        @pl.when(pl.program_id(0) > 0)
        def _():
            acc_ref[...] += part

    tile = pl.BlockSpec((T, D), lambda i: (i, 0))
    return pl.pallas_call(
        kern, grid=(S // T,), in_specs=[tile, tile], out_specs=[tile, pl.BlockSpec((1, 128), lambda i: (0, 0))],
        out_shape=[jax.ShapeDtypeStruct((S, D), F32), jax.ShapeDtypeStruct((1, 128), F32)],
        name=name, compiler_params=_cparams(("arbitrary",)))(y, target)


def _adamw_math(w, g, m, v):
    m = ADAM_B1 * m + (1.0 - ADAM_B1) * g
    v = ADAM_B2 * v + (1.0 - ADAM_B2) * jnp.square(g)
    m_hat = m / (1.0 - ADAM_B1 ** ADAM_STEP)
    v_hat = v / (1.0 - ADAM_B2 ** ADAM_STEP)
    return -ADAM_LR * (m_hat / (jnp.sqrt(v_hat) + ADAM_EPS) + ADAM_WD * w), m, v


def _adamw(w, g, m, v, *, name):
    R, C = w.shape
    T = _pick(R, (256, 128, 64, 32, 16, 8))

    def kern(w_ref, g_ref, m_ref, v_ref, d_ref, mo_ref, vo_ref):
        d, mn, vn = _adamw_math(w_ref[...], g_ref[...], m_ref[...], v_ref[...])
        d_ref[...] = d
        mo_ref[...] = mn
        vo_ref[...] = vn

    tile = pl.BlockSpec((T, C), lambda i: (i, 0))
    return pl.pallas_call(
        kern, grid=(R // T,), in_specs=[tile] * 4, out_specs=[tile] * 3,
        out_shape=[jax.ShapeDtypeStruct((R, C), F32)] * 3, name=name,
        compiler_params=_cparams(("parallel",)))(w, g, m, v)


def _adamw_layer(w, g, m, v, layer, prev, *, name, comm=None):
    L, R, C = w.shape
    T = _pick(R, (256, 128, 64, 32, 16, 8))

    def kern(w_ref, g_ref, m_ref, v_ref, *rest):
        go_ref, d_ref, mo_ref, vo_ref = rest[-4:]
        grad = g_ref[...]
        d, mn, vn = _adamw_math(w_ref[...], grad, m_ref[...], v_ref[...])
        go_ref[...] = grad
        d_ref[...] = d
        mo_ref[...] = mn
        vo_ref[...] = vn

    layer_tile = pl.BlockSpec((None, T, C), lambda i: (layer, i, 0))
    in_specs = [layer_tile, pl.BlockSpec((T, C), lambda i: (i, 0)), layer_tile, layer_tile]
    args, aliases = [w, g, m, v], {}
    if prev is not None:
        in_specs += [pl.BlockSpec(memory_space=pl.ANY)] * 4
        args += list(prev)
        aliases = {4 + k: k for k in range(4)}
    if comm is not None:
        assert prev is None
        return _hosted_call(kern, grid=(R // T,), in_specs=in_specs, out_specs=[layer_tile] * 4,
                            out_shape=[jax.ShapeDtypeStruct((L, R, C), F32)] * 4, scratch_shapes=[], args=args,
                            name=name, comm=comm)
    return pl.pallas_call(
        kern, grid=(R // T,), in_specs=in_specs, out_specs=[layer_tile] * 4,
        out_shape=[jax.ShapeDtypeStruct((L, R, C), F32)] * 4, input_output_aliases=aliases, name=name,
        compiler_params=_cparams(("parallel",)))(*args)


def _sum8(g, *, name):
    def kern(g_ref, o_ref):
        acc = g_ref[0]
        for d in range(1, g.shape[0]):
            acc = acc + g_ref[d]
        o_ref[...] = acc

    return pl.pallas_call(kern, out_shape=jax.ShapeDtypeStruct(g.shape[1:], F32), name=name,
                          compiler_params=_cparams())(g)


def _place():
    return lax.axis_index("x"), lax.axis_index("y"), lax.axis_index("c")


def _other_chips(x, y):
    return [(1 - x, y), (x, 1 - y), (1 - x, 1 - y)]


def _remote(src, dst, send_sems, recv_sems, k, to):
    return pltpu.make_async_remote_copy(src_ref=src, dst_ref=dst, send_sem=send_sems.at[k], recv_sem=recv_sems.at[k],
                                        device_id=to, device_id_type=MESH)


def _all_gather_small(v, *, name):
    def body(x_ref, out_ref, send_sems, recv_sems, local_sem):
        x, y, c = _place()
        me = 4 * x + 2 * y + c
        mine = pltpu.make_async_copy(x_ref, out_ref.at[me], local_sem)
        mine.start()
        peers = []
        for f in range(1, 8):
            peers.append((1 - x if f & 4 else x, 1 - y if f & 2 else y, 1 - c if f & 1 else c))
        sends = [_remote(x_ref, out_ref.at[me], send_sems, recv_sems, k, p) for k, p in enumerate(peers)]
        for cp in sends:
            cp.start()
        for k, (px, py, pc) in enumerate(peers):
            _remote(x_ref, out_ref.at[4 * px + 2 * py + pc], send_sems, recv_sems, k, (px, py, pc)).wait_recv()
        for cp in sends:
            cp.wait_send()
        mine.wait()

    return pl.pallas_call(
        body, out_shape=jax.ShapeDtypeStruct((8,) + v.shape, v.dtype),
        in_specs=[pl.BlockSpec(memory_space=pltpu.VMEM)], out_specs=pl.BlockSpec(memory_space=pltpu.VMEM),
        scratch_shapes=[pltpu.SemaphoreType.DMA((7,)), pltpu.SemaphoreType.DMA((7,)), pltpu.SemaphoreType.DMA],
        name=name, compiler_params=_cparams())(v)


def _hosted_call(kern, *, grid, in_specs, out_specs, out_shape, scratch_shapes, args, name, comm=None, sem=None):
    if comm is None:
        res = pl.pallas_call(kern, grid=grid, in_specs=in_specs, out_specs=out_specs, out_shape=out_shape,
                             scratch_shapes=scratch_shapes, name=name, compiler_params=_cparams(sem))(*args)
        return list(res), []
    n_in, n_out, n_scr = len(in_specs), len(out_specs), len(scratch_shapes)
    c_in, c_out = len(comm.inputs), len(comm.out_shapes)

    def body(*refs):
        ins, ci = refs[:n_in], refs[n_in:n_in + c_in]
        outs = refs[n_in + c_in:n_in + c_in + n_out]
        co = refs[n_in + c_in + n_out:n_in + c_in + n_out + c_out]
        scr = refs[n_in + c_in + n_out + c_out:n_in + c_in + n_out + c_out + n_scr]
        cs = refs[n_in + c_in + n_out + c_out + n_scr:]
        ids = [pl.program_id(d) for d in range(len(grid))]
        inner_first = functools.reduce(jnp.logical_and, [i == 0 for i in ids[1:]], True)
        inner_last = functools.reduce(jnp.logical_and, [i == n - 1 for i, n in zip(ids[1:], grid[1:])], True)

        @pl.when(jnp.logical_and(ids[0] == 0, inner_first))
        def _():
            comm.begin(ci, co, cs)

        kern(*ins, *outs, *scr)

        @pl.when(jnp.logical_and(ids[0] == grid[0] // 2, inner_last))
        def _():
            comm.middle(ci, co, cs)

        @pl.when(jnp.logical_and(ids[0] == grid[0] - 1, inner_last))
        def _():
            comm.end(ci, co, cs)

    hbm = pl.BlockSpec(memory_space=pltpu.HBM)
    res = pl.pallas_call(
        body, grid=grid, in_specs=list(in_specs) + [hbm] * c_in, out_specs=list(out_specs) + [hbm] * c_out,
        out_shape=list(out_shape) + list(comm.out_shapes), scratch_shapes=list(scratch_shapes) + list(comm.scratch),
        input_output_aliases={n_in + i: n_out + o for i, o in comm.aliases.items()},
        name=name, compiler_params=_cparams(("arbitrary",) * len(grid)))(*args, *comm.inputs)
    return list(res[:n_out]), list(res[n_out:])


def _run_comm(comm, *, name):
    return _hosted_call(lambda: None, grid=(1,), in_specs=[], out_specs=[], out_shape=[], scratch_shapes=[], args=[],
                        name=name, comm=comm)[1]


class _Gather:
    def __init__(self, shards, kinds, items):
        used = sorted({w for w, _ in items})
        self.slot = {w: k for k, w in enumerate(used)}
        self.inputs = [shards[w] for w in used]
        self.items, self.kinds = list(items), kinds
        self.shapes = {w: shards[w].shape[1:] for w in used}
        self.out_shapes = [jax.ShapeDtypeStruct((r, 4 * n) if kinds[w] == "col" else (4 * r, n), shards[w].dtype)
                           for w, _ in items for r, n in [self.shapes[w]]]
        n_items = len(items)
        self.scratch = [pltpu.SemaphoreType.DMA((6 * n_items,)), pltpu.SemaphoreType.DMA((6 * n_items,)),
                        pltpu.SemaphoreType.DMA((n_items,))]
        self.aliases = {}

    def _piece(self, ref, w, qq, half):
        r, n = self.shapes[w]
        h = r // 2
        lo, size = (0, r) if half is None else (half * h, h)
        if self.kinds[w] == "col":
            return ref.at[pl.ds(pl.multiple_of(lo, 16), size), pl.ds(pl.multiple_of(qq * n, 128), n)]
        return ref.at[pl.ds(pl.multiple_of(qq * r + lo, 16), size), :]

    def _mine(self, ci, w, l, half):
        h = self.shapes[w][0] // 2
        return ci[self.slot[w]].at[l, pl.ds(pl.multiple_of(half * h, 16), h), :]

    def begin(self, ci, co, cs):
        send_sems, recv_sems, local_sems = cs
        x, y, c = _place()
        q = 2 * x + y
        for k, (w, l) in enumerate(self.items):
            pltpu.make_async_copy(ci[self.slot[w]].at[l], self._piece(co[k], w, q, None), local_sems.at[k]).start()
            for j, (cx, cy) in enumerate(_other_chips(x, y)):
                _remote(self._mine(ci, w, l, c), self._piece(co[k], w, q, c), send_sems, recv_sems, 6 * k + j,
                        (cx, cy, c)).start()

    def middle(self, ci, co, cs):
        send_sems, recv_sems, _ = cs
        x, y, c = _place()
        for k, (w, l) in enumerate(self.items):
            for j, (cx, cy) in enumerate(_other_chips(x, y)):
                win = self._piece(co[k], w, 2 * cx + cy, c)
                _remote(win, win, send_sems, recv_sems, 6 * k + j, (cx, cy, c)).wait_recv()
                _remote(win, win, send_sems, recv_sems, 6 * k + 3 + j, (x, y, 1 - c)).start()

    def end(self, ci, co, cs):
        send_sems, recv_sems, local_sems = cs
        x, y, c = _place()
        q = 2 * x + y
        for k, (w, l) in enumerate(self.items):
            for j, (cx, cy) in enumerate(_other_chips(x, y)):
                win = self._piece(co[k], w, 2 * cx + cy, 1 - c)
                _remote(win, win, send_sems, recv_sems, 6 * k + 3 + j, (x, y, 1 - c)).wait_recv()
        for k, (w, l) in enumerate(self.items):
            for j, (cx, cy) in enumerate(_other_chips(x, y)):
                _remote(self._mine(ci, w, l, c), self._piece(co[k], w, q, c), send_sems, recv_sems, 6 * k + j,
                        (cx, cy, c)).wait_send()
                win = self._piece(co[k], w, 2 * cx + cy, c)
                _remote(win, win, send_sems, recv_sems, 6 * k + 3 + j, (x, y, 1 - c)).wait_send()
            pltpu.make_async_copy(ci[self.slot[w]].at[l], self._piece(co[k], w, q, None), local_sems.at[k]).wait()


def _half_rows(ref, half, h):
    return ref.at[:, pl.ds(pl.multiple_of(half * h, 16), h), :]


class _Copies:
    def __init__(self, inputs, out_shapes, count, pairs, aliases=None, lands=None):
        self.inputs, self.out_shapes, self.pairs, self.lands = list(inputs), list(out_shapes), pairs, lands
        self.scratch = [pltpu.SemaphoreType.DMA((count,)), pltpu.SemaphoreType.DMA((count,))]
        self.aliases = aliases or {}

    def _copies(self, ci, co, cs):
        x, y, c = _place()
        return [_remote(src, dst, cs[0], cs[1], k, to) for k, (src, dst, to) in enumerate(self.pairs(ci, co, x, y, c))]

    def begin(self, ci, co, cs):
        for cp in self._copies(ci, co, cs):
            cp.start()

    def middle(self, ci, co, cs):
        pass

    def end(self, ci, co, cs):
        x, y, c = _place()
        for k, (src, dst, to) in enumerate(self.pairs(ci, co, x, y, c)):
            _remote(src, dst, cs[0], cs[1], k, to).wait_send()
            arrival = dst if self.lands is None else self.lands(co, x, y, c)[k]
            _remote(src, arrival, cs[0], cs[1], k, to).wait_recv()


class _Together:
    def __init__(self, progs):
        self.progs = progs
        self.inputs = [a for p in progs for a in p.inputs]
        self.out_shapes = [o for p in progs for o in p.out_shapes]
        self.scratch = [t for p in progs for t in p.scratch]
        self.aliases, n_in, n_out = {}, 0, 0
        for p in progs:
            self.aliases.update({n_in + i: n_out + o for i, o in p.aliases.items()})
            n_in, n_out = n_in + len(p.inputs), n_out + len(p.out_shapes)

    def _each(self, ci, co, cs):
        i = o = t = 0
        for p in self.progs:
            ni, no, nt = len(p.inputs), len(p.out_shapes), len(p.scratch)
            yield p, ci[i:i + ni], co[o:o + no], cs[t:t + nt]
            i, o, t = i + ni, o + no, t + nt

    def begin(self, ci, co, cs):
        for p, a, b, c in self._each(ci, co, cs):
            p.begin(a, b, c)

    def middle(self, ci, co, cs):
        for p, a, b, c in self._each(ci, co, cs):
            p.middle(a, b, c)

    def end(self, ci, co, cs):
        for p, a, b, c in self._each(ci, co, cs):
            p.end(a, b, c)

    def split(self, results):
        out, o = [], 0
        for p in self.progs:
            out.append(results[o:o + len(p.out_shapes)])
            o += len(p.out_shapes)
        return out


def _send_to_all(v):
    def peers(x, y, c):
        return [(1 - x if f & 4 else x, 1 - y if f & 2 else y, 1 - c if f & 1 else c) for f in range(1, 8)]

    def pairs(ci, co, x, y, c):
        return [(ci[0], co[0].at[4 * x + 2 * y + c], peer) for peer in peers(x, y, c)]

    def lands(co, x, y, c):
        return [co[0].at[4 * px + 2 * py + pc] for px, py, pc in peers(x, y, c)]

    return _Copies([v], [jax.ShapeDtypeStruct((8,) + v.shape, v.dtype)], 7, pairs, lands=lands)


def _swap_halves(gs):
    def pairs(ci, co, x, y, c):
        return [(_half_rows(ci[k], 1 - c, g.shape[1] // 2), co[k], (x, y, 1 - c)) for k, g in enumerate(gs)]

    return _Copies(gs, [jax.ShapeDtypeStruct((g.shape[0], g.shape[1] // 2, g.shape[2]), g.dtype) for g in gs],
                   len(gs), pairs)


def _scatter_quarters(ps, kinds):
    part = [((p.shape[1], p.shape[2] // 4) if kind == "col" else (p.shape[1], p.shape[2])) for p, kind in zip(ps, kinds)]

    def pairs(ci, co, x, y, c):
        out = []
        for k, kind in enumerate(kinds):
            n = part[k][1]
            for j, (cx, cy) in enumerate(_other_chips(x, y)):
                qj = 2 * cx + cy
                src = ci[k].at[0, :, pl.ds(pl.multiple_of(qj * n, 128), n)] if kind == "col" else ci[k].at[qj]
                out.append((src, co[k].at[j], (cx, cy, c)))
        return out

    return _Copies(ps, [jax.ShapeDtypeStruct((3,) + pt, p.dtype) for pt, p in zip(part, ps)], 3 * len(ps), pairs)


def _share_halves(gs):
    def rows(co, k, half):
        h = gs[k].shape[0] // 2
        return co[k].at[pl.ds(pl.multiple_of(half * h, 16), h), :]

    def pairs(ci, co, x, y, c):
        return [(rows(co, k, c), rows(co, k, c), (x, y, 1 - c)) for k in range(len(gs))]

    def lands(co, x, y, c):
        return [rows(co, k, 1 - c) for k in range(len(gs))]

    return _Copies(gs, [jax.ShapeDtypeStruct(g.shape, g.dtype) for g in gs], len(gs), pairs,
                   aliases={k: k for k in range(len(gs))}, lands=lands)


def _wide_tile(n):
    return _pick(n, (2048, 1920, 1024, 512, 256, 128))


def _pair_sum(g, land, place, *, name):
    B, R, N = g.shape
    h = R // 2
    tr, tc = _pick(h, (256, 128)), _wide_tile(N)

    def kern(place_ref, g_ref, l_ref, o_ref):
        o_ref[...] = (g_ref[...] + l_ref[...]).astype(o_ref.dtype)

    grid_spec = pltpu.PrefetchScalarGridSpec(
        num_scalar_prefetch=1, grid=(B, h // tr, N // tc),
        in_specs=[pl.BlockSpec((None, tr, tc), lambda b, i, j, p: (b, p[1] * (h // tr) + i, j)),
                  pl.BlockSpec((None, tr, tc), lambda b, i, j, p: (b, i, j))],
        out_specs=pl.BlockSpec((None, tr, tc), lambda b, i, j, p: (b, i, j)))
    return pl.pallas_call(kern, grid_spec=grid_spec, out_shape=jax.ShapeDtypeStruct((B, h, N), BF16), name=name,
                          compiler_params=_cparams(("parallel", "parallel", "parallel")))(place, g, land)


def _quarter_sum(p, land, kind, shard_shape, place, *, name):
    L, r, n = shard_shape
    h = r // 2
    tr, tc = _pick(h, (256, 128)), _wide_tile(n)

    def kern(place_ref, p_ref, a_ref, b_ref, c_ref, o_ref):
        o_ref[...] = ((p_ref[...].astype(F32) + a_ref[...].astype(F32)) + b_ref[...].astype(F32)) + c_ref[...].astype(F32)

    if kind == "col":
        p_spec = pl.BlockSpec((None, tr, tc), lambda l, i, j, pr: (l, i, pr[0] * (n // tc) + j))
    else:
        p_spec = pl.BlockSpec((None, None, tr, tc), lambda l, i, j, pr: (l, pr[0], i, j))
    lands = [pl.BlockSpec((None, None, tr, tc), functools.partial(lambda l, i, j, pr, s: (s, l, i, j), s=s))
             for s in range(3)]
    grid_spec = pltpu.PrefetchScalarGridSpec(
        num_scalar_prefetch=1, grid=(L, h // tr, n // tc), in_specs=[p_spec] + lands,
        out_specs=pl.BlockSpec((None, tr, tc), lambda l, i, j, pr: (l, pr[1] * (h // tr) + i, j)))
    return pl.pallas_call(kern, grid_spec=grid_spec, out_shape=jax.ShapeDtypeStruct((L, r, n), F32), name=name,
                          compiler_params=_cparams(("parallel", "parallel", "parallel")))(place, p, land, land, land)


class _ReduceScatter:
    def __init__(self, grads, kinds, shard_shapes, place, tag):
        self.kinds, self.shapes, self.place, self.tag = kinds, shard_shapes, place, tag
        self.g3 = [g[None] if kind == "col" else g.reshape(4, g.shape[0] // 4, g.shape[1]) for g, kind in zip(grads, kinds)]

    def swap(self):
        return _swap_halves(self.g3)

    def pair_sums(self, lands):
        self.ps = [_pair_sum(g, land, self.place, name=f"rs_pair_sum_{self.tag}_{k}")
                   for k, (g, land) in enumerate(zip(self.g3, lands))]

    def scatter(self):
        return _scatter_quarters(self.ps, self.kinds)

    def quarter_sums(self, parts):
        self.halves = []
        for k, (p, part) in enumerate(zip(self.ps, parts)):
            p4 = p if self.kinds[k] == "col" else p[None]
            out = _quarter_sum(p4, part[:, None], self.kinds[k], (1,) + tuple(self.shapes[k]), self.place,
                               name=f"rs_quarter_sum_{self.tag}_{k}")
            self.halves.append(out[0])

    def share(self):
        return _share_halves(self.halves)

    def run(self):
        self.pair_sums(_run_comm(self.swap(), name=f"rs_swap_{self.tag}"))
        self.quarter_sums(_run_comm(self.scatter(), name=f"rs_scatter_{self.tag}"))
        return _run_comm(self.share(), name=f"rs_share_{self.tag}")


_WEIGHTS = ["mod_w", "mod_b", "norm1_g", "w_in", "gate_b", "conv_w", "conv_b", "conv_ln_g", "conv_ln_b", "w_conv_proj",
            "hgrn_lb", "hgrn_norm_g", "w_hgrn_proj", "sb_qn_g", "sb_kn_g", "w_sb_proj", "w_out", "norm2_g", "mlp_w1",
            "mlp_w2"]
_BIG = [("w_in", "col"), ("w_conv_proj", "col"), ("w_hgrn_proj", "col"), ("w_sb_proj", "col"), ("w_out", "row"),
        ("mlp_w1", "col"), ("mlp_w2", "row")]
_REPLICATED = ["mod_b", "norm1_g", "gate_b", "conv_b", "conv_ln_g", "conv_ln_b", "hgrn_lb", "hgrn_norm_g", "sb_qn_g",
               "sb_kn_g", "norm2_g"]
LANES = 128


class _Pack:
    def __init__(self, items):
        self.shapes = {n: a.shape for n, a in items}
        self.offsets, pos = {}, 0
        for n, a in items:
            self.offsets[n] = pos
            pos += math.prod(a.shape)
        self.rows = -(-pos // (8 * LANES)) * 8
        flat = jnp.concatenate([a.reshape(-1).astype(F32) for _, a in items])
        self.array = jnp.pad(flat, (0, self.rows * LANES - pos)).reshape(self.rows, LANES)

    def get(self, packed, name):
        lead = packed.shape[:-2]
        flat = packed.reshape(lead + (self.rows * LANES,))
        n = math.prod(self.shapes[name])
        return lax.slice_in_dim(flat, self.offsets[name], self.offsets[name] + n, axis=len(lead)).reshape(
            lead + self.shapes[name])


def _lower_bounds(hgrn_lb):
    p = jax.nn.softmax(hgrn_lb.astype(F32), axis=0)
    return jnp.cumsum(p, axis=0) - p[0:1]


def _layer_fwd(x, w, p, l, comms=(None, None)):
    S, D = x.shape
    r = {"x": x}
    (r["h"],) = _rowop(_fn_normmod, [(x, 0, D)], [p["n1g"], p["sc1"], p["sh1"]], [(D, BF16)], name=f"normmod1_fwd_{l}")
    proj = r["proj"] = _matmul(r["h"], w["w_in", l], name=f"w_in_fwd_{l}")
    r["cpre"] = _conv_fwd(proj, p["w32"], p["conv_b"], name=f"conv_fwd_{l}")
    (r["cact"],) = _rowop(_fn_lnsilu, [(r["cpre"], 0, CONV_CH)], [p["lng"], p["lnb"]], [(CONV_CH, BF16)],
                          name=f"conv_ln_fwd_{l}")
    arrived = lambda comm, got: w.update({(_BIG[k][0], layer): arr for (k, layer), arr in zip(comm.items, got)})
    (r["hg"], r["states"]), got = _hgrn_fwd(proj, p["lbk"], p["ng"], name=f"hgrn_fwd_{l}", comm=comms[0])
    if comms[0] is not None:
        arrived(comms[0], got)
    (r["sb"], r["rs"]), got = _sb_fwd(proj, p["qg"], p["kg"], name=f"sb_fwd_{l}", comm=comms[1])
    if comms[1] is not None:
        arrived(comms[1], got)
    r["y_c"] = _matmul(r["cact"], w["w_conv_proj", l], name=f"w_conv_proj_fwd_{l}")
    r["y_h"] = _matmul(r["hg"], w["w_hgrn_proj", l], name=f"w_hgrn_proj_fwd_{l}")
    r["y_s"] = _matmul(r["sb"], w["w_sb_proj", l], name=f"w_sb_proj_fwd_{l}")
    (r["merged"],) = _rowop(_fn_merge, [(proj, OFF_GL, 3 * D), (r["y_c"], 0, D), (r["y_h"], 0, D), (r["y_s"], 0, D)],
                            [p["gate_b"]], [(D, BF16)], name=f"merge_fwd_{l}")
    resid = lambda y, x_in, gate: (y,) + _fn_resid(x_in, y, gate)
    r["a_out"], r["x1"] = _matmul(r["merged"], w["w_out", l], name=f"w_out_fwd_{l}", post=resid, extras=[x],
                                  rows=[p["g1"]], out_dtypes=(F32, F32))
    (r["h2"],) = _rowop(_fn_normmod, [(r["x1"], 0, D)], [p["n2g"], p["sc2"], p["sh2"]], [(D, BF16)],
                        name=f"normmod2_fwd_{l}")
    r["u"], r["act"] = _matmul(r["h2"], w["mlp_w1", l], name=f"mlp_w1_fwd_{l}", post=lambda u: (u,) + _fn_relu2(u),
                               out_dtypes=(F32, BF16))
    r["m_out"], x2 = _matmul(r["act"], w["mlp_w2", l], name=f"mlp_w2_fwd_{l}", post=resid, extras=[r["x1"]],
                             rows=[p["g2"]], out_dtypes=(F32, F32))
    return x2, r


def _layer_bwd(dx2, r, w, p, l, grads, carry=None, last=None):
    S, D = dx2.shape
    small = {}

    def dweight(name, a, dy):
        grads[name, l] = _matmul(a, dy, ta=True, name=f"{name}_dw_{l}")

    stage = (lambda k, got: carry(k, got)) if carry is not None else (lambda k, got: None)

    (dm_out,), (dg2,) = _rowop_bwd(_fn_scale, [(r["m_out"], 0, D)], [p["g2"]], [dx2], [BF16], name=f"resid2_bwd_{l}")
    (du,) = _matmul(dm_out, w["mlp_w2", l], tb=True, name=f"mlp_w2_dx_{l}", extras=[r["u"]], out_dtypes=(BF16,),
                    post=lambda dact, u: (dact * (2.0 * jnp.maximum(u, 0.0)),))
    dweight("mlp_w2", r["act"], dm_out)
    dh2 = _matmul(du, w["mlp_w1", l], tb=True, name=f"mlp_w1_dx_{l}")
    dweight("mlp_w1", r["h2"], du)
    (dx1, da_out), (dg1, small["norm2_g"], dsc2, dsh2) = _rowop_bwd(
        _fn_resid_norm, [(r["x"], 0, D), (r["a_out"], 0, D)], [p["g1"], p["n2g"], p["sc2"], p["sh2"]], [dx2, dh2],
        [F32, BF16], name=f"resid1_norm2_bwd_{l}")
    dmerged = _matmul(da_out, w["w_out", l], tb=True, name=f"w_out_dx_{l}")
    dweight("w_out", r["merged"], da_out)
    (dgl, dy_c, dy_h, dy_s), (small["gate_b"],) = _rowop_bwd(
        _fn_merge, [(r["proj"], OFF_GL, 3 * D), (r["y_c"], 0, D), (r["y_h"], 0, D), (r["y_s"], 0, D)], [p["gate_b"]],
        [dmerged], [BF16] * 4, name=f"merge_bwd_{l}")
    dweight("w_conv_proj", r["cact"], dy_c)
    dweight("w_hgrn_proj", r["hg"], dy_h)
    dweight("w_sb_proj", r["sb"], dy_s)
    dcact = _matmul(dy_c, w["w_conv_proj", l], tb=True, name=f"w_conv_proj_dx_{l}")
    (dcpre,), (small["conv_ln_g"], small["conv_ln_b"]) = _rowop_bwd(
        _fn_lnsilu, [(r["cpre"], 0, CONV_CH)], [p["lng"], p["lnb"]], [dcact], [F32], name=f"conv_ln_bwd_{l}")
    (d_conv, dw32, small["conv_b"]), got = _conv_bwd(r["proj"], dcpre, p["w32"], name=f"conv_bwd_{l}",
                                                      comm=stage(0, None))
    small["conv_w"] = dw32[:CONV_WIDTH]
    dhg = _matmul(dy_h, w["w_hgrn_proj", l], tb=True, out_dtype=BF16, name=f"w_hgrn_proj_dx_{l}")
    (dq, df, di, dg, dlbk, dng), got = _hgrn_bwd(r["proj"], r["states"], dhg, p["lbk"], p["ng"], name=f"hgrn_bwd_{l}",
                                                 comm=stage(1, got))
    small["lower"] = -dlbk
    small["hgrn_norm_g"] = jnp.sum(dng, axis=0)
    dsb = _matmul(dy_s, w["w_sb_proj", l], tb=True, out_dtype=BF16, name=f"w_sb_proj_dx_{l}")
    (dsq, dsk, dsv, dqg, dkg), got = _sb_bwd(r["proj"], p["qg"], p["kg"], r["rs"], dsb, name=f"sb_bwd_{l}",
                                             comm=stage(2, got))
    stage(3, got)
    fold = lambda t: jnp.sum(t.reshape(-1, SB_DH), axis=0, keepdims=True)
    small["sb_qn_g"], small["sb_kn_g"] = fold(dqg), fold(dkg)
    dproj = jnp.concatenate([d_conv, dq, df, di, dg, dsq, dsk, dsv, dgl], axis=1)
    dweight("w_in", r["h"], dproj)
    norm1 = functools.partial(_rowop_bwd, _fn_normmod, [(r["x"], 0, D)], [p["n1g"], p["sc1"], p["sh1"]],
                              din_dtypes=[F32], add={0: dx1}, name=f"normmod1_bwd_{l}")
    if last is None:
        dh = _matmul(dproj, w["w_in", l], tb=True, name=f"w_in_dx_{l}")
        (dx,), (small["norm1_g"], dsc1, dsh1) = norm1(douts=[dh])
    else:
        dh, got = _matmul(dproj, w["w_in", l], tb=True, name=f"w_in_dx_{l}", comm=last(0, None))
        (dx,), (small["norm1_g"], dsc1, dsh1), got = norm1(douts=[dh], comm=last(1, got))
        last(2, got)
    small["mod"] = jnp.concatenate([dsh1, dsc1, dg1, dsh2, dsc2, dg2], axis=1)
    return dx, small


def kernel(x, c, mod_w, mod_b, norm1_g, w_in, gate_b, conv_w, conv_b, conv_ln_g, conv_ln_b, w_conv_proj, hgrn_lb, hgrn_norm_g, w_hgrn_proj, sb_qn_g, sb_kn_g, w_sb_proj, w_out, norm2_g, mlp_w1, mlp_w2, loss_target, m_mod_w, m_mod_b, m_norm1_g, m_w_in, m_gate_b, m_conv_w, m_conv_b, m_conv_ln_g, m_conv_ln_b, m_w_conv_proj, m_hgrn_lb, m_hgrn_norm_g, m_w_hgrn_proj, m_sb_qn_g, m_sb_kn_g, m_w_sb_proj, m_w_out, m_norm2_g, m_mlp_w1, m_mlp_w2, v_mod_w, v_mod_b, v_norm1_g, v_w_in, v_gate_b, v_conv_w, v_conv_b, v_conv_ln_g, v_conv_ln_b, v_w_conv_proj, v_hgrn_lb, v_hgrn_norm_g, v_w_hgrn_proj, v_sb_qn_g, v_sb_kn_g, v_w_sb_proj, v_w_out, v_norm2_g, v_mlp_w1, v_mlp_w2):
    given = dict(locals())
    wts = {n: given[n] for n in _WEIGHTS}
    mom = {n: given["m_" + n] for n in _WEIGHTS}
    var = {n: given["v_" + n] for n in _WEIGHTS}
    n_layers, D = norm1_g.shape
    xi, yi, ci = _place()
    q = 2 * xi + yi
    me = 4 * xi + 2 * yi + ci
    place = jnp.stack([q, ci]).astype(jnp.int32)
    n_mod = mod_w.shape[2]
    cw = conv_w.shape[2]

    pk1 = _Pack([("c", c), ("conv_w", conv_w)])
    got1 = _all_gather_small(pk1.array, name="gather_cond")
    c_act = jax.nn.silu(pk1.get(got1, "c")[:, 0, :])
    conv_full = jnp.concatenate([pk1.get(got1, "conv_w")[2 * k] for k in range(4)], axis=-1)

    mod_cols = []
    for l in range(n_layers):
        mb = lax.dynamic_slice_in_dim(mod_b[l], q * n_mod, n_mod)
        mod_cols.append(_matmul(c_act, mod_w, bl=l, name=f"mod_fwd_{l}") + mb[None, :])
    got2 = _all_gather_small(jnp.concatenate(mod_cols, axis=0), name="gather_mod")
    mods = []
    for l in range(n_layers):
        row = lax.dynamic_index_in_dim(got2[0::2], l * 8 + me, axis=1, keepdims=False)
        mods.append(jnp.split(row.reshape(1, 4 * n_mod), 6, axis=1))

    lower, lower_vjp = jax.vjp(_lower_bounds, hgrn_lb)

    shards = [wts[n].astype(BF16) for n, _ in _BIG]
    kinds = [k for _, k in _BIG]
    index = {n: k for k, (n, _) in enumerate(_BIG)}
    first = ["w_in", "w_conv_proj", "w_hgrn_proj", "w_sb_proj"]

    def gather(*names_layers):
        items = [(index[n], l) for names, l in names_layers for n in names if l < n_layers]
        return _Gather(shards, kinds, items) if items else None

    start = gather((first[:1], 0))
    w = {(_BIG[k][0], layer): arr
         for (k, layer), arr in zip(start.items, _run_comm(start, name="gather_first_weights"))}

    def layer_params(l):
        sh1, sc1, g1, sh2, sc2, g2 = mods[l]
        return dict(sh1=sh1, sc1=sc1, g1=g1, sh2=sh2, sc2=sc2, g2=g2, n1g=norm1_g[l][None], n2g=norm2_g[l][None],
                    gate_b=gate_b[l][None], conv_b=conv_b[l][None], lng=conv_ln_g[l][None], lnb=conv_ln_b[l][None],
                    w32=jnp.pad(conv_full[l], ((0, CONV_HALO - CONV_WIDTH), (0, 0))), lbk=(1.0 - lower[l])[None],
                    ng=hgrn_norm_g[l][None], qg=jnp.tile(sb_qn_g[l][None], (1, SB_PAIR // SB_DH)),
                    kg=jnp.tile(sb_kn_g[l][None], (1, SB_PAIR // SB_DH)))

    params = [layer_params(l) for l in range(n_layers)]
    act, saved = x[0], []
    for l in range(n_layers):
        early = first[1:] + ["w_out"] if l == 0 else ["w_out", "mlp_w1"]
        comms = (gather((early, l)), gather(([n for n in ("mlp_w1", "mlp_w2") if n not in early], l), (first, l + 1)))
        act, r = _layer_fwd(act, w, params[l], l, comms=comms)
        saved.append(r)
    dact, loss_lanes = _loss_head(act, loss_target[0], name="loss_head")

    grads, smalls, reduced = {}, [None] * n_layers, {}

    def reduce_scatter(items, tag):
        return _ReduceScatter([grads[_BIG[k][0], layer] for k, layer in items], [kinds[k] for k, _ in items],
                              [shards[k].shape[1:] for k, _ in items], place, tag)

    def carried(l):
        items_a = [(k, l + 1) for k in range(len(_BIG))]
        items_b = [(k, l) for k, (n, _) in enumerate(_BIG) if n != "w_in"]
        box = boxes.setdefault(l, {})

        def carry(stage, got):
            if stage == 0:
                box["a"], box["b"] = reduce_scatter(items_a, f"l{l + 1}"), reduce_scatter(items_b, f"l{l}")
                box["swaps"] = _Together([box["a"].swap(), box["b"].swap()])
                return box["swaps"]
            if stage == 1:
                lands_a, lands_b = box["swaps"].split(got)
                box["a"].pair_sums(lands_a)
                box["b"].pair_sums(lands_b)
                return box["a"].scatter()
            if stage == 2:
                box["a"].quarter_sums(got)
                box["both"] = _Together([box["a"].share(), box["b"].scatter()])
                return box["both"]
            done_a, parts_b = box["both"].split(got)
            reduced.update(zip(items_a, done_a))
            box["b"].quarter_sums(parts_b)
            box["b_items"] = items_b

        return carry

    def final(l):
        items = [(index["w_in"], l)]
        box = boxes.setdefault(l, {})

        def step(stage, got):
            if stage == 0:
                box["w"] = reduce_scatter(items, "w_in")
                box["w"].pair_sums(_run_comm(box["w"].swap(), name="rs_swap_w_in"))
                box["last"] = _Together([box["w"].scatter()] + ([box["b"].share()] if "b_items" in box else []))
                return box["last"]
            if stage == 1:
                parts = box["last"].split(got)
                if "b_items" in box:
                    reduced.update(zip(box["b_items"], parts[1]))
                box["w"].quarter_sums(parts[0])
                return box["w"].share()
            reduced.update(zip(items, got))

        return step

    boxes = {}

    for l in reversed(range(n_layers)):
        dact, smalls[l] = _layer_bwd(dact, saved[l], w, params[l], l, grads, carried(l) if l + 1 < n_layers else None,
                                     final(l) if l == 0 else None)
    grad_x = dact[None]
    rest = [(k, l) for l in range(n_layers) for k in range(len(_BIG)) if (k, l) not in reduced]
    if rest:
        reduced.update(zip(rest, reduce_scatter(rest, "rest").run()))

    stack = lambda k: jnp.stack([smalls[l][k] for l in range(n_layers)])
    (d_hgrn_lb,) = lower_vjp(stack("lower")[:, 0, :])
    items = [("loss", loss_lanes), ("mod", stack("mod")), ("hgrn_lb", d_hgrn_lb), ("conv_w", stack("conv_w"))]
    items += [(k, stack(k)) for k in ("norm1_g", "gate_b", "conv_b", "conv_ln_g", "conv_ln_b", "hgrn_norm_g", "sb_qn_g",
                                      "sb_kn_g", "norm2_g")]
    pk3 = _Pack(items)

    share_small = _send_to_all(pk3.array)
    delta, new_m, new_v, big, got3 = {}, {}, {}, {}, None
    for n, _ in _BIG:
        outs = None
        for l in reversed(range(n_layers)):
            args = (wts[n], reduced[index[n], l], mom[n], var[n], l, outs)
            if got3 is None:
                outs, (got3,) = _adamw_layer(*args, name=f"adamw_{n}_{l}", comm=share_small)
            else:
                outs = _adamw_layer(*args, name=f"adamw_{n}_{l}")
        big[n] = outs
    got3 = lax.dynamic_update_slice_in_dim(got3, pk3.array[None], me, axis=0)
    tot3 = _sum8(got3, name="sum_small_grads")
    loss = (0.5 / D) * jnp.sum(pk3.get(tot3, "loss"))
    g = {k: pk3.get(tot3, k).reshape(wts[k].shape) for k in _REPLICATED if k != "mod_b"}
    g["mod_b"] = pk3.get(tot3, "mod")[:, 0, :]
    g["conv_w"] = lax.dynamic_slice_in_dim(pk3.get(tot3, "conv_w"), q * cw, cw, axis=2)
    dmod_all = pk3.get(got3, "mod")[:, :, 0, :]
    g_mod_w = None
    for l in range(n_layers):
        cols = lax.dynamic_slice_in_dim(dmod_all[:, l, :], q * n_mod, n_mod, axis=1)
        g_mod_w = _matmul(c_act, cols, ta=True, layer=l, n_layers=n_layers, into=g_mod_w, name=f"mod_dw_{l}")
    g["mod_w"] = g_mod_w

    for n, _ in _BIG:
        g[n], delta[n], new_m[n], new_v[n] = big[n]
    two_d = lambda t: t.reshape(-1, t.shape[-1])
    outs = _adamw(two_d(mod_w), two_d(g["mod_w"]), two_d(m_mod_w), two_d(v_mod_w), name="adamw_mod_w")
    delta["mod_w"], new_m["mod_w"], new_v["mod_w"] = (t.reshape(mod_w.shape) for t in outs)
    rest = _REPLICATED + ["conv_w"]
    packs = [_Pack([(n, src[n]) for n in rest]) for src in (wts, g, mom, var)]
    outs = _adamw(*[pk.array for pk in packs], name="adamw_small")
    for n in rest:
        delta[n], new_m[n], new_v[n] = (packs[0].get(t, n) for t in outs)

    return (loss, grad_x, *[g[n] for n in _WEIGHTS], *[delta[n] for n in _WEIGHTS], *[new_m[n] for n in _WEIGHTS],
            *[new_v[n] for n in _WEIGHTS])
```

```python
import functools
import math

import jax
import jax.numpy as jnp
from jax import lax
from jax.experimental import pallas as pl
from jax.experimental.pallas import tpu as pltpu

F32 = jnp.float32
BF16 = jnp.bfloat16
MESH = pl.DeviceIdType.MESH

EPS = 1e-6
CONV_CH = 512
CONV_WIDTH = 31
CONV_HALO = 32
HG_HEADS = 4
HG_D = 128
HG_CHUNK = 64
HG_KEYS = 8
HG_SUB = 32
SB_HEADS = 8
SB_DH = 64
SB_BLK = 128
SB_PAIR = 128
SB_SKIP = -104.0
OFF_CONV, OFF_HG, OFF_SB, OFF_GL = 0, 1024, 3072, 4608
ADAM_LR, ADAM_B1, ADAM_B2, ADAM_EPS, ADAM_WD, ADAM_STEP = 0.001, 0.9, 0.999, 1e-08, 0.01, 10
VMEM_LIMIT_BYTES = 56 * 1024 * 1024
ROW_TILE = 512


def _cparams(sem=None, **kw):
    return pltpu.CompilerParams(dimension_semantics=sem, vmem_limit_bytes=VMEM_LIMIT_BYTES, **kw)


def _pick(n, cands):
    for c in cands:
        if n % c == 0:
            return c
    return n


MATMUL_VMEM_BUDGET = 40 * 1024 * 1024


def _tile_options(n, cap):
    opts = [t for t in range(cap - cap % 128, 0, -128) if n % t == 0]
    return opts or [n]


def _matmul_tiles(M, N, K, size_a, size_b, size_o, in_acc):
    for tm in _tile_options(M, 1024):
        for tk in _tile_options(K, 2048):
            for tn in _tile_options(N, 1280):
                need = 2 * (tm * tk * size_a + tk * tn * size_b + tm * tn * size_o)
                if K > tk and not in_acc:
                    need += tm * tn * 4
                if need <= MATMUL_VMEM_BUDGET:
                    return tm, tn, tk
    raise ValueError(f"no matmul tiling fits VMEM for {(M, N, K)}")
def _matmul(a, b, *, ta=False, tb=False, bl=None, out_dtype=F32, name, into=None, layer=None, n_layers=None,
            post=None, extras=(), rows=(), out_dtypes=None, comm=None):
    M, K = (a.shape[1], a.shape[0]) if ta else a.shape
    N = b.shape[-2] if tb else b.shape[-1]
    if post is not None:
        return _matmul_post(a, b, M, N, K, ta, tb, post, extras, rows, out_dtypes, name)
    assert comm is None or layer is None
    in_acc = jnp.dtype(out_dtype) == jnp.dtype(F32)
    tm, tn, tk = _matmul_tiles(M, N, K, a.dtype.itemsize, b.dtype.itemsize, jnp.dtype(out_dtype).itemsize, in_acc)
    nk = K // tk
    a_spec = pl.BlockSpec((tk, tm), lambda i, j, k: (k, i)) if ta else pl.BlockSpec((tm, tk), lambda i, j, k: (i, k))
    if bl is None:
        b_spec = pl.BlockSpec((tn, tk), lambda i, j, k: (j, k)) if tb else pl.BlockSpec((tk, tn), lambda i, j, k: (k, j))
    elif tb:
        b_spec = pl.BlockSpec((None, tn, tk), lambda i, j, k: (bl, j, k))
    else:
        b_spec = pl.BlockSpec((None, tk, tn), lambda i, j, k: (bl, k, j))
    dn = (((0 if ta else 1,), (1 if tb else 0,)), ((), ()))

    use_scratch = nk > 1 and not in_acc

    def kern(a_ref, b_ref, *rest):
        o_ref = rest[-2] if use_scratch else rest[-1]
        prod = lambda: lax.dot_general(a_ref[...].astype(BF16), b_ref[...].astype(BF16), dn,
                                       preferred_element_type=F32)
        if nk == 1:
            o_ref[...] = prod().astype(o_ref.dtype).reshape(o_ref.shape)
            return
        acc_ref = rest[-1] if use_scratch else o_ref
        k = pl.program_id(2)

        @pl.when(k == 0)
        def _():
            acc_ref[...] = prod().reshape(acc_ref.shape)

        @pl.when(k > 0)
        def _():
            acc_ref[...] += prod().reshape(acc_ref.shape)

        if use_scratch:
            @pl.when(k == nk - 1)
            def _():
                o_ref[...] = acc_ref[...].astype(o_ref.dtype).reshape(o_ref.shape)

    in_specs, args, aliases = [a_spec, b_spec], [a, b], {}
    if layer is None:
        out_shape = jax.ShapeDtypeStruct((M, N), out_dtype)
        out_spec = pl.BlockSpec((tm, tn), lambda i, j, k: (i, j))
    else:
        out_shape = jax.ShapeDtypeStruct((n_layers, M, N), out_dtype)
        out_spec = pl.BlockSpec((1, tm, tn), lambda i, j, k: (layer, i, j))
        if into is not None:
            in_specs.append(pl.BlockSpec(memory_space=pl.ANY))
            args.append(into)
            aliases = {2: 0}
    if comm is not None:
        (out,), got = _hosted_call(kern, grid=(M // tm, N // tn, nk), in_specs=in_specs, out_specs=[out_spec],
                                   out_shape=[out_shape], scratch_shapes=[pltpu.VMEM((tm, tn), F32)] if use_scratch else [],
                                   args=args, name=name, comm=comm)
        return out, got
    return pl.pallas_call(
        kern, grid=(M // tm, N // tn, nk), in_specs=in_specs, out_specs=out_spec, out_shape=out_shape,
        scratch_shapes=[pltpu.VMEM((tm, tn), F32)] if use_scratch else [],
        input_output_aliases=aliases, name=name,
        compiler_params=_cparams(("parallel", "parallel", "arbitrary")))(*args)


def _matmul_post(a, b, M, N, K, ta, tb, post, extras, rows, out_dtypes, name):
    per_elem = sum(e.dtype.itemsize for e in extras) + sum(jnp.dtype(d).itemsize for d in out_dtypes)
    fits = lambda tm, tn: 2 * (tm * K * a.dtype.itemsize + K * tn * b.dtype.itemsize + tm * tn * per_elem) <= MATMUL_VMEM_BUDGET
    tm, tn = next((tm, tn) for tm in _tile_options(M, 1024) for tn in _tile_options(N, 1280) if fits(tm, tn))
    a_spec = pl.BlockSpec((K, tm), lambda i, j: (0, i)) if ta else pl.BlockSpec((tm, K), lambda i, j: (i, 0))
    b_spec = pl.BlockSpec((tn, K), lambda i, j: (j, 0)) if tb else pl.BlockSpec((K, tn), lambda i, j: (0, j))
    tile = pl.BlockSpec((tm, tn), lambda i, j: (i, j))
    row = pl.BlockSpec((1, tn), lambda i, j: (0, j))
    dn = (((0 if ta else 1,), (1 if tb else 0,)), ((), ()))
    n_ex = len(extras) + len(rows)

    def kern(a_ref, b_ref, *rest):
        prod = lax.dot_general(a_ref[...].astype(BF16), b_ref[...].astype(BF16), dn, preferred_element_type=F32)
        res = post(prod, *[r[...].astype(F32) for r in rest[:n_ex]])
        for val, o_ref in zip(res, rest[n_ex:]):
            o_ref[...] = val.astype(o_ref.dtype)

    return pl.pallas_call(
        kern, grid=(M // tm, N // tn), in_specs=[a_spec, b_spec] + [tile] * len(extras) + [row] * len(rows),
        out_specs=[tile] * len(out_dtypes), out_shape=[jax.ShapeDtypeStruct((M, N), d) for d in out_dtypes], name=name,
        compiler_params=_cparams(("parallel", "parallel")))(a, b, *extras, *rows)


def _col_specs(off, width, T):
    bw = math.gcd(width, off) if off else width
    return [pl.BlockSpec((T, bw), functools.partial(lambda i, c: (i, c), c=off // bw + p)) for p in range(width // bw)]


def _gather_rows(refs, counts):
    vals, pos = [], 0
    for n in counts:
        parts = [refs[pos + p][...].astype(F32) for p in range(n)]
        pos += n
        vals.append(parts[0] if n == 1 else jnp.concatenate(parts, axis=1))
    return vals, pos


def _rowop(fn, ins, params, outs, *, name):
    S = ins[0][0].shape[0]
    T = min(ROW_TILE, S)
    in_specs, counts, args = [], [], []
    for arr, off, width in ins:
        sp = _col_specs(off, width, T)
        in_specs += sp
        counts.append(len(sp))
        args += [arr] * len(sp)
    in_specs += [pl.BlockSpec(p.shape, lambda i: (0, 0)) for p in params]

    def kern(*refs):
        vals, pos = _gather_rows(refs, counts)
        pv = [refs[pos + p][...] for p in range(len(params))]
        pos += len(params)
        res = fn(*vals, *pv)
        for r, o_ref in zip(res, refs[pos:]):
            o_ref[...] = r.astype(o_ref.dtype)

    return pl.pallas_call(
        kern, grid=(S // T,), in_specs=in_specs,
        out_specs=[pl.BlockSpec((T, w), lambda i: (i, 0)) for w, _ in outs],
        out_shape=[jax.ShapeDtypeStruct((S, w), dt) for w, dt in outs],
        name=name, compiler_params=_cparams(("parallel",)))(*args, *params)


def _rowop_bwd(fn, ins, params, douts, din_dtypes, *, name, add=None, comm=None):
    add = add or {}
    S = ins[0][0].shape[0]
    T = min(ROW_TILE, S)
    in_specs, counts, args = [], [], []
    for arr, off, width in ins:
        sp = _col_specs(off, width, T)
        in_specs += sp
        counts.append(len(sp))
        args += [arr] * len(sp)
    in_specs += [pl.BlockSpec(p.shape, lambda i: (0, 0)) for p in params]
    in_specs += [pl.BlockSpec((T, d.shape[1]), lambda i: (i, 0)) for d in douts]
    add_keys = sorted(add)
    in_specs += [pl.BlockSpec((T, add[k].shape[1]), lambda i: (i, 0)) for k in add_keys]
    want = [k for k, dt in enumerate(din_dtypes) if dt is not None]

    def kern(*refs):
        vals, pos = _gather_rows(refs, counts)
        pv = [refs[pos + p][...] for p in range(len(params))]
        pos += len(params)
        cts = [refs[pos + p][...].astype(F32) for p in range(len(douts))]
        pos += len(douts)
        adds = {k: refs[pos + p][...].astype(F32) for p, k in enumerate(add_keys)}
        pos += len(add_keys)
        _, vjp = jax.vjp(fn, *vals, *pv)
        grads = vjp(tuple(cts))
        for k in want:
            g = grads[k] + adds[k] if k in adds else grads[k]
            refs[pos][...] = g.astype(refs[pos].dtype)
            pos += 1
        first = pl.program_id(0) == 0
        for p in range(len(params)):
            gp, o_ref = grads[len(ins) + p], refs[pos + p]

            @pl.when(first)
            def _(gp=gp, o_ref=o_ref):
                o_ref[...] = gp

            @pl.when(jnp.logical_not(first))
            def _(gp=gp, o_ref=o_ref):
                o_ref[...] += gp

    out_specs = [pl.BlockSpec((T, ins[k][2]), lambda i: (i, 0)) for k in want]
    out_specs += [pl.BlockSpec(p.shape, lambda i: (0, 0)) for p in params]
    out_shape = [jax.ShapeDtypeStruct((S, ins[k][2]), din_dtypes[k]) for k in want]
    out_shape += [jax.ShapeDtypeStruct(p.shape, F32) for p in params]
    res, got = _hosted_call(
        kern, grid=(S // T,), in_specs=in_specs, out_specs=out_specs, out_shape=out_shape, scratch_shapes=[],
        args=[*args, *params, *douts, *[add[k] for k in add_keys]], name=name, comm=comm, sem=("arbitrary",))
    dins = [None] * len(ins)
    for p, k in enumerate(want):
        dins[k] = res[p]
    return (dins, res[len(want):]) if comm is None else (dins, res[len(want):], got)


def _rms(x, g):
    return x * lax.rsqrt(jnp.mean(x * x, axis=-1, keepdims=True) + EPS) * g


def _fn_normmod(x, g, sc, sh):
    return (_rms(x, g) * (1.0 + sc) + sh,)


def _fn_lnsilu(c, g, b):
    mu = jnp.mean(c, axis=-1, keepdims=True)
    var = jnp.mean(jnp.square(c - mu), axis=-1, keepdims=True)
    y = (c - mu) * lax.rsqrt(var + EPS) * g + b
    return (y * jax.nn.sigmoid(y),)


def _fn_merge(gl, yc, yh, ys, gb):
    d = yc.shape[1]
    g = jax.nn.sigmoid(gl + gb)
    return (g[:, :d] * yc + g[:, d:2 * d] * yh + g[:, 2 * d:] * ys,)


def _fn_resid(x, y, g):
    return (x + g * y,)


def _fn_resid_norm(x, y, g, n, sc, sh):
    x1 = x + g * y
    return (x1,) + _fn_normmod(x1, n, sc, sh)


def _fn_scale(y, g):
    return (g * y,)


def _fn_relu2(u):
    return (jnp.square(jnp.maximum(u, 0.0)),)


def _conv_specs(S, T):
    r = T // CONV_HALO
    cur = [pl.BlockSpec((T, CONV_CH), lambda i: (i, 0)), pl.BlockSpec((T, CONV_CH), lambda i: (i, 1))]
    prev = [pl.BlockSpec((CONV_HALO, CONV_CH), lambda i: (jnp.maximum(i * r - 1, 0), 0)),
            pl.BlockSpec((CONV_HALO, CONV_CH), lambda i: (jnp.maximum(i * r - 1, 0), 1))]
    return cur + prev


def _glu_ext(a_ref, g_ref, ah_ref, gh_ref):
    a = a_ref[...]
    sg = jax.nn.sigmoid(g_ref[...])
    uh = jnp.where(pl.program_id(0) > 0, ah_ref[...] * jax.nn.sigmoid(gh_ref[...]), 0.0)
    return a, sg, jnp.concatenate([uh, a * sg], axis=0)


def _shift_up(xe, k, T):
    return xe[:T] if k == 0 else pltpu.roll(xe, shift=xe.shape[0] - k, axis=0)[:T]


def _conv_fwd(proj, w32, b, *, name):
    S = proj.shape[0]
    T = min(ROW_TILE, S)
    lead = CONV_HALO - (CONV_WIDTH - 1)

    def kern(a_ref, g_ref, ah_ref, gh_ref, w_ref, b_ref, o_ref):
        _, _, ue = _glu_ext(a_ref, g_ref, ah_ref, gh_ref)
        acc = jnp.zeros((T, CONV_CH), F32) + b_ref[...]
        for j in range(CONV_WIDTH):
            acc = acc + w_ref[j:j + 1, :] * _shift_up(ue, lead + j, T)
        o_ref[...] = acc

    const = lambda shape: pl.BlockSpec(shape, lambda i: (0, 0))
    return pl.pallas_call(
        kern, grid=(S // T,), in_specs=_conv_specs(S, T) + [const(w32.shape), const(b.shape)],
        out_specs=pl.BlockSpec((T, CONV_CH), lambda i: (i, 0)),
        out_shape=jax.ShapeDtypeStruct((S, CONV_CH), F32), name=name,
        compiler_params=_cparams(("parallel",)))(proj, proj, proj, proj, w32, b)


def _conv_bwd(proj, dc, w32, *, name, comm=None):
    S = proj.shape[0]
    T = min(ROW_TILE, S)
    nt = S // T
    r = T // CONV_HALO
    lead = CONV_HALO - (CONV_WIDTH - 1)
    last_halo = S // CONV_HALO - 1

    def kern(a_ref, g_ref, ah_ref, gh_ref, dc_ref, dcn_ref, w_ref, dag_ref, dw_ref, db_ref):
        i = pl.program_id(0)
        a, sg, ue = _glu_ext(a_ref, g_ref, ah_ref, gh_ref)
        dc_t = dc_ref[...]
        de = jnp.concatenate([dc_t, jnp.where(i < nt - 1, dcn_ref[...], 0.0)], axis=0)

        @pl.when(i == 0)
        def _():
            dw_ref[...] = jnp.zeros_like(dw_ref)
            db_ref[...] = jnp.zeros_like(db_ref)

        du = jnp.zeros((T, CONV_CH), F32)
        for j in range(CONV_WIDTH):
            du = du + w_ref[j:j + 1, :] * _shift_up(de, CONV_WIDTH - 1 - j, T)
            dw_ref[j:j + 1, :] += jnp.sum(dc_t * _shift_up(ue, lead + j, T), axis=0, keepdims=True)
        db_ref[...] += jnp.sum(dc_t, axis=0, keepdims=True)
        dag_ref[:, :CONV_CH] = (du * sg).astype(BF16)
        dag_ref[:, CONV_CH:] = (du * a * sg * (1.0 - sg)).astype(BF16)

    const = lambda shape: pl.BlockSpec(shape, lambda i: (0, 0))
    in_specs = _conv_specs(S, T) + [
        pl.BlockSpec((T, CONV_CH), lambda i: (i, 0)),
        pl.BlockSpec((CONV_HALO, CONV_CH), lambda i: (jnp.minimum((i + 1) * r, last_halo), 0)),
        const(w32.shape)]
    return _hosted_call(
        kern, grid=(nt,), in_specs=in_specs,
        out_specs=[pl.BlockSpec((T, 2 * CONV_CH), lambda i: (i, 0)), const(w32.shape), const((1, CONV_CH))],
        out_shape=[jax.ShapeDtypeStruct((S, 2 * CONV_CH), BF16), jax.ShapeDtypeStruct(w32.shape, F32),
                   jax.ShapeDtypeStruct((1, CONV_CH), F32)],
        scratch_shapes=[], args=[proj, proj, proj, proj, dc, dc, w32], name=name, comm=comm, sem=("arbitrary",))


def _iota2(shape, dim):
    return lax.broadcasted_iota(jnp.int32, shape, dim)


def _running(x, seg, later):
    n = x.shape[0]
    pos = _iota2(x.shape, 0) & (seg - 1)
    k = 1
    while k < seg:
        if later:
            x = x + jnp.where(pos < seg - k, pltpu.roll(x, n - k, axis=0), 0.0)
        else:
            x = x + jnp.where(pos >= k, pltpu.roll(x, k, axis=0), 0.0)
        k *= 2
    return x


@functools.partial(jax.custom_vjp, nondiff_argnums=(1,))
def _prefix(x, seg):
    return _running(x, seg, False)


_prefix.defvjp(lambda x, seg: (_running(x, seg, False), None), lambda seg, _, g: (_running(g, seg, True),))


def _hg_chunk(q, f, iv, g, st, lbk, ng):
    n, sub = HG_CHUNK, HG_SUB
    kk = lbk * jax.nn.sigmoid(-f)
    lf = jnp.log(1.0 - kk)
    b = _prefix(lf, n)
    bs = _prefix(lf, sub)
    bt = jnp.sum(lf, axis=0, keepdims=True)
    qh = q * jax.nn.sigmoid(q)
    dot_nt = lambda x, y: lax.dot_general(x.astype(BF16), y.astype(BF16), (((1,), (1,)), ((), ())),
                                          preferred_element_type=F32)
    o = dot_nt(qh * jnp.exp(b), st)
    b0 = b - bs
    qs = qh * jnp.exp(bs)
    col = _iota2((sub, n), 1)
    rows = []
    for blk in range(n // sub):
        lo = blk * sub
        sl = slice(lo, lo + sub)
        acc = o[sl]
        if blk > 0:
            ref = jnp.concatenate([b0[sl]] * (n // sub), axis=0)
            kd = kk * jnp.exp(jnp.minimum(ref - b, 0.0))
            sc = jnp.where(col < lo, dot_nt(qs[sl], kd), 0.0)
            acc = acc + jnp.dot(sc.astype(BF16), iv.astype(BF16), preferred_element_type=F32)
        for t0 in range(0, sub, HG_KEYS):
            keys, qrys = slice(lo + t0, lo + t0 + HG_KEYS), slice(lo + t0, lo + sub)
            nt = sub - t0
            bq, bk = bs[qrys][None, :, :], bs[keys][:, None, :]
            s_i = lax.broadcasted_iota(jnp.int32, (HG_KEYS, nt, HG_D), 0) + t0
            t_i = lax.broadcasted_iota(jnp.int32, (HG_KEYS, nt, HG_D), 1) + t0
            keep = s_i <= t_i
            p = jnp.where(keep, qh[qrys][None, :, :] * kk[keys][:, None, :] * jnp.exp(jnp.where(keep, bq - bk, 0.0)), 0.0)
            w = jnp.sum(p, axis=-1, keepdims=True)
            part = jnp.sum(w * iv[keys][:, None, :], axis=0)
            acc = acc + (part if t0 == 0 else jnp.concatenate([jnp.zeros((t0, HG_D), F32), part], axis=0))
        rows.append(acc)
    o = jnp.concatenate(rows, axis=0)
    kd = kk * jnp.exp(bt - b)
    st_new = jnp.exp(bt) * st + lax.dot_general(iv.astype(BF16), kd.astype(BF16), (((0,), (0,)), ((), ())),
                                                     preferred_element_type=F32)
    out = _rms(o, ng) * (g * jax.nn.sigmoid(g))
    return out, st_new


def _hg_tile(S):
    return min(512, S)


def _hg_in_specs(rt, rev, nr):
    width = HG_HEADS * HG_D
    base = OFF_HG // width
    row = (lambda r: nr - 1 - r) if rev else (lambda r: r)
    return [pl.BlockSpec((rt, width), functools.partial(lambda r, k: (row(r), base + k), k=k)) for k in range(4)]


def _hg_cols(h):
    return slice(h * HG_D, (h + 1) * HG_D)


def _hgrn_fwd(proj, lbk, ng, *, name, comm=None):
    S = proj.shape[0]
    rt = _hg_tile(S)
    nr, nc = S // rt, rt // HG_CHUNK

    def kern(q_ref, f_ref, i_ref, g_ref, lbk_ref, ng_ref, o_ref, st_out_ref, st_ref):
        @pl.when(pl.program_id(0) == 0)
        def _():
            st_ref[...] = jnp.zeros_like(st_ref)

        def body(c, carry):
            rows = pl.ds(pl.multiple_of(c * HG_CHUNK, HG_CHUNK), HG_CHUNK)
            for h in range(HG_HEADS):
                cols = _hg_cols(h)
                st = st_ref[h]
                st_out_ref[h, c] = st
                out, st_new = _hg_chunk(q_ref[rows, cols], f_ref[rows, cols], i_ref[rows, cols], g_ref[rows, cols], st,
                                        lbk_ref[:, cols], ng_ref[...])
                o_ref[rows, cols] = out.astype(o_ref.dtype)
                st_ref[h] = st_new
            return carry

        lax.fori_loop(0, nc, body, 0)

    width = HG_HEADS * HG_D
    in_specs = _hg_in_specs(rt, False, nr) + [pl.BlockSpec((1, width), lambda r: (0, 0)),
                                               pl.BlockSpec((1, HG_D), lambda r: (0, 0))]
    return _hosted_call(
        kern, grid=(nr,), in_specs=in_specs,
        out_specs=[pl.BlockSpec((rt, width), lambda r: (r, 0)),
                   pl.BlockSpec((HG_HEADS, nc, HG_D, HG_D), lambda r: (0, r, 0, 0))],
        out_shape=[jax.ShapeDtypeStruct((S, width), BF16),
                   jax.ShapeDtypeStruct((HG_HEADS, S // HG_CHUNK, HG_D, HG_D), F32)],
        scratch_shapes=[pltpu.VMEM((HG_HEADS, HG_D, HG_D), F32)],
        args=[proj, proj, proj, proj, lbk, ng], name=name, comm=comm, sem=("arbitrary",))


def _hgrn_bwd(proj, states, dout, lbk, ng, *, name, comm=None):
    S = proj.shape[0]
    rt = _hg_tile(S)
    nr, nc = S // rt, rt // HG_CHUNK
    width = HG_HEADS * HG_D

    def kern(q_ref, f_ref, i_ref, g_ref, st_in_ref, do_ref, lbk_ref, ng_ref,
             dq_ref, df_ref, di_ref, dg_ref, dlbk_ref, dng_ref, dst_ref):
        @pl.when(pl.program_id(0) == 0)
        def _():
            dst_ref[...] = jnp.zeros_like(dst_ref)
            dlbk_ref[...] = jnp.zeros_like(dlbk_ref)
            dng_ref[...] = jnp.zeros_like(dng_ref)

        def body(k, carry):
            c = nc - 1 - k
            rows = pl.ds(pl.multiple_of(c * HG_CHUNK, HG_CHUNK), HG_CHUNK)
            for h in range(HG_HEADS):
                cols = _hg_cols(h)
                _, vjp = jax.vjp(_hg_chunk, q_ref[rows, cols], f_ref[rows, cols], i_ref[rows, cols], g_ref[rows, cols],
                                 st_in_ref[h, c], lbk_ref[:, cols], ng_ref[...])
                dq, df, di, dg, dst, dlbk, dng = vjp((do_ref[rows, cols].astype(F32), dst_ref[h]))
                dq_ref[rows, cols] = dq.astype(BF16)
                df_ref[rows, cols] = df.astype(BF16)
                di_ref[rows, cols] = di.astype(BF16)
                dg_ref[rows, cols] = dg.astype(BF16)
                dst_ref[h] = dst
                dlbk_ref[:, cols] += dlbk
                dng_ref[h] += dng
            return carry

        lax.fori_loop(0, nc, body, 0)

    rev = lambda r: nr - 1 - r
    tile = pl.BlockSpec((rt, width), lambda r: (rev(r), 0))
    in_specs = _hg_in_specs(rt, True, nr) + [
        pl.BlockSpec((HG_HEADS, nc, HG_D, HG_D), lambda r: (0, rev(r), 0, 0)), tile,
        pl.BlockSpec((1, width), lambda r: (0, 0)), pl.BlockSpec((1, HG_D), lambda r: (0, 0))]
    return _hosted_call(
        kern, grid=(nr,), in_specs=in_specs,
        out_specs=[tile, tile, tile, tile, pl.BlockSpec((1, width), lambda r: (0, 0)),
                   pl.BlockSpec((HG_HEADS, 1, HG_D), lambda r: (0, 0, 0))],
        out_shape=[jax.ShapeDtypeStruct((S, width), BF16)] * 4 + [
            jax.ShapeDtypeStruct((1, width), F32), jax.ShapeDtypeStruct((HG_HEADS, 1, HG_D), F32)],
        scratch_shapes=[pltpu.VMEM((HG_HEADS, HG_D, HG_D), F32)],
        args=[proj, proj, proj, proj, states, dout, lbk, ng], name=name, comm=comm, sem=("arbitrary",))


def _sb_scores(km, qi):
    return lax.dot_general(km, qi, (((1,), (1,)), ((), ())), preferred_element_type=F32)


def _sb_weights(zt, r_run, diag):
    n = SB_BLK
    sp = jnp.maximum(zt, 0.0) + jnp.log(1.0 + jnp.exp(-jnp.abs(zt)))
    lk = -sp
    if diag:
        keep = (_iota2(zt.shape, 0) & (n - 1)) < _iota2(zt.shape, 1)
        lk = jnp.where(keep, lk, 0.0)
    tails = [_running(lk[a * n:(a + 1) * n], n, True) for a in range(2)]
    between = jnp.concatenate([tails[a] + r_run[a] for a in range(2)], axis=0)
    wgt = jnp.exp(zt + between)
    if diag:
        wgt = jnp.where(keep, wgt, 0.0)
    return sp, wgt, [t[0:1, :] for t in tails]


def _sb_norm_pair(x, g2, lane_lo):
    sq = x * x
    ms_lo = jnp.sum(jnp.where(lane_lo, sq, 0.0), axis=-1, keepdims=True)
    ms_hi = jnp.sum(jnp.where(lane_lo, 0.0, sq), axis=-1, keepdims=True)
    return x * lax.rsqrt(jnp.where(lane_lo, ms_lo, ms_hi) * (1.0 / SB_DH) + EPS) * g2


def _sb_specs(S):
    base = OFF_SB // SB_PAIR
    per = SB_HEADS * SB_DH // SB_PAIR
    cols = [pl.BlockSpec((S, SB_PAIR), functools.partial(lambda p, k: (0, base + per * k + p), k=k)) for k in range(3)]
    return cols + [pl.BlockSpec((1, SB_PAIR), lambda p: (0, 0))] * 2


def _sb_rows(i):
    return pl.ds(pl.multiple_of(i * SB_BLK, SB_BLK), SB_BLK)


def _sb_both(j, a=None):
    if a is None:
        return pl.ds(pl.multiple_of(j * 2 * SB_BLK, 2 * SB_BLK), 2 * SB_BLK)
    return pl.ds(pl.multiple_of(j * 2 * SB_BLK + a * SB_BLK, SB_BLK), SB_BLK)


def _sb_fwd(proj, qg2, kg2, *, name, comm=None):
    S = proj.shape[0]
    nb = S // SB_BLK
    scale = SB_DH ** -0.5
    n_pairs = SB_HEADS * SB_DH // SB_PAIR

    def kern(q_ref, k_ref, v_ref, qg_ref, kg_ref, o_ref, rs_ref, qp_ref, km_ref, vt_ref):
        lane_lo = _iota2((SB_BLK, SB_PAIR), 1) < SB_DH

        def prologue(j, carry):
            rows = _sb_rows(j)
            qp_ref[rows, :] = (_sb_norm_pair(q_ref[rows, :], qg_ref[...], lane_lo) * scale).astype(BF16)
            kn = _sb_norm_pair(k_ref[rows, :], kg_ref[...], lane_lo)
            v = v_ref[rows, :]
            for a, mine in enumerate((lane_lo, jnp.logical_not(lane_lo))):
                km_ref[_sb_both(j, a), :] = jnp.where(mine, kn, 0.0).astype(BF16)
                vt_ref[:, _sb_both(j, a)] = jnp.where(mine, v, 0.0).T.astype(BF16)
            return carry

        lax.fori_loop(0, nb, prologue, 0)

        diagonal = lambda i: _sb_scores(km_ref[_sb_both(i), :], qp_ref[_sb_rows(i), :])

        def qblock(i, zt):
            qi = qp_ref[_sb_rows(i), :]

            scores = lambda j: _sb_scores(km_ref[_sb_both(jnp.maximum(j, 0)), :], qi)
            output = lambda j, wgt: jnp.dot(vt_ref[:, _sb_both(j)], wgt, preferred_element_type=F32)

            def note(j, r_run):
                for a in range(2):
                    rs_ref[a, i, pl.ds(j, 1), :] = r_run[a]
                return jnp.maximum(jnp.max(r_run[0]), jnp.max(r_run[1])) > SB_SKIP

            def noted(j, r_run):
                return lax.cond(j >= 0, lambda: note(j, r_run).astype(jnp.int32), lambda: jnp.int32(0))

            zero = jnp.zeros((1, SB_BLK), F32)
            z_next = scores(i - 1)
            _, wgt, r_run = _sb_weights(zt, [zero, zero], True)
            go = noted(i - 1, r_run)

            def body(c):
                j, _, acc, r_run, zt, j_prev, w_prev = c
                z_next = scores(j - 1)
                acc = acc + output(j_prev, w_prev)
                _, wgt, lk_sum = _sb_weights(zt, r_run, False)
                r_run = [r_run[a] + lk_sum[a] for a in range(2)]
                return j - 1, noted(j - 1, r_run), acc, r_run, z_next, j, wgt.astype(BF16)

            c = (i - 1, go, jnp.zeros((SB_PAIR, SB_BLK), F32), r_run, z_next, i, wgt.astype(BF16))
            _, _, acc, _, _, j_prev, w_prev = lax.while_loop(lambda c: c[1] > 0, body, c)
            rs_ref[0, i, pl.ds(i, 1), :] = jnp.full((1, SB_BLK), j_prev, jnp.int32).astype(F32)
            zt = diagonal(jnp.minimum(i + 1, nb - 1))
            o_ref[_sb_rows(i), :] = (acc + output(j_prev, w_prev)).T.astype(o_ref.dtype)
            return zt

        lax.fori_loop(0, nb, qblock, diagonal(0))

    width = SB_HEADS * SB_DH
    return _hosted_call(
        kern, grid=(n_pairs,), in_specs=_sb_specs(S),
        out_specs=[pl.BlockSpec((S, SB_PAIR), lambda p: (0, p)),
                   pl.BlockSpec((2, nb, nb, SB_BLK), lambda p: (p, 0, 0, 0))],
        out_shape=[jax.ShapeDtypeStruct((S, width), BF16), jax.ShapeDtypeStruct((SB_HEADS, nb, nb, SB_BLK), F32)],
        scratch_shapes=[pltpu.VMEM((S, SB_PAIR), BF16), pltpu.VMEM((2 * S, SB_PAIR), BF16),
                        pltpu.VMEM((SB_PAIR, 2 * S), BF16)],
        args=[proj, proj, proj, qg2, kg2], name=name, comm=comm, sem=("parallel",))


def _sb_bwd(proj, qg2, kg2, rs, do, *, name, comm=None):
    S = proj.shape[0]
    nb = S // SB_BLK
    scale = SB_DH ** -0.5
    n_pairs = SB_HEADS * SB_DH // SB_PAIR

    def kern(q_ref, k_ref, v_ref, qg_ref, kg_ref, rs_ref, do_ref, dq_ref, dk_ref, dv_ref, dqg_ref, dkg_ref,
             qp_ref, km_ref, kt_ref, vm_ref, dqn_ref, dkn_ref, dvs_ref):
        lane_lo = _iota2((SB_BLK, SB_PAIR), 1) < SB_DH
        heads = (lane_lo, jnp.logical_not(lane_lo))
        fn_q = lambda x, g: _sb_norm_pair(x, g, lane_lo) * scale
        fn_k = lambda x, g: _sb_norm_pair(x, g, lane_lo)

        def prologue(j, carry):
            rows = _sb_rows(j)
            qp_ref[rows, :] = fn_q(q_ref[rows, :], qg_ref[...]).astype(BF16)
            kn = fn_k(k_ref[rows, :], kg_ref[...])
            v = v_ref[rows, :]
            for a, mine in enumerate(heads):
                k_a = jnp.where(mine, kn, 0.0)
                km_ref[_sb_both(j, a), :] = k_a.astype(BF16)
                kt_ref[:, _sb_both(j, a)] = k_a.T.astype(BF16)
                vm_ref[_sb_both(j, a), :] = jnp.where(mine, v, 0.0).astype(BF16)
            return carry

        lax.fori_loop(0, nb, prologue, 0)
        dkn_ref[...] = jnp.zeros_like(dkn_ref)
        dvs_ref[...] = jnp.zeros_like(dvs_ref)

        def leftmost(i):
            return jnp.clip(jnp.max(rs_ref[0, i, pl.ds(i, 1), :]).astype(jnp.int32), 0, i)

        def opening_of(i, j):
            jc = jnp.minimum(j, i)
            return (_sb_scores(km_ref[_sb_both(jc), :], qp_ref[_sb_rows(i), :]),
                    lax.dot_general(vm_ref[_sb_both(jc), :], do_ref[_sb_rows(i), :], (((1,), (1,)), ((), ())),
                                    preferred_element_type=F32))

        def qblock(i, carry):
            first, zt, dp = carry
            qi = qp_ref[_sb_rows(i), :]
            doi = do_ref[_sb_rows(i), :]

            opening = functools.partial(opening_of, i)

            def closing(j, dzb, wgtb, dqa):
                dkn_ref[_sb_both(j), :] += jnp.dot(dzb, qi, preferred_element_type=F32)
                dvs_ref[_sb_both(j), :] += jnp.dot(wgtb, doi, preferred_element_type=F32)
                return dqa + jnp.dot(kt_ref[:, _sb_both(j)], dzb, preferred_element_type=F32)

            def middle(j, diag, zt, dp, e_run):
                zero = jnp.zeros((1, SB_BLK), F32)
                r_run = [zero, zero] if diag else [rs_ref[a, i, pl.ds(j, 1), :] for a in range(2)]
                sp, wgt, _ = _sb_weights(zt, r_run, diag)
                e = dp * wgt
                heads_e = [_running(e[a * SB_BLK:(a + 1) * SB_BLK], SB_BLK, False) for a in range(2)]
                e_left = jnp.concatenate([heads_e[a] + e_run[a] for a in range(2)], axis=0) - e
                s_neg = jnp.exp(-sp)
                dz = e * s_neg - e_left * (1.0 - s_neg)
                if diag:
                    dz = jnp.where((_iota2(dz.shape, 0) & (SB_BLK - 1)) < _iota2(dz.shape, 1), dz, 0.0)
                return dz.astype(BF16), wgt.astype(BF16), [e_run[a] + heads_e[a][SB_BLK - 1:SB_BLK, :] for a in range(2)]

            def body(j, c):
                dqa, e_run, zt, dp, j_prev, dzb, wgtb = c
                nxt = opening(j + 1)
                dqa = closing(j_prev, dzb, wgtb, dqa)
                dzb, wgtb, e_run = middle(j, False, zt, dp, e_run)
                return (dqa, e_run) + nxt + (j, dzb, wgtb)

            zero = jnp.zeros((1, SB_BLK), F32)
            none = jnp.zeros((2 * SB_BLK, SB_BLK), BF16)
            c = (jnp.zeros((SB_PAIR, SB_BLK), F32), [zero, zero], zt, dp, first, none, none)
            dqa, e_run, zt, dp, j_prev, dzb, wgtb = lax.fori_loop(first, i, body, c)
            dqa = closing(j_prev, dzb, wgtb, dqa)
            dzb, wgtb, _ = middle(i, True, zt, dp, e_run)
            i_next = jnp.minimum(i + 1, nb - 1)
            first_next = leftmost(i_next)
            nxt = opening_of(i_next, first_next)
            dqn_ref[_sb_rows(i), :] = closing(i, dzb, wgtb, dqa).T
            return (first_next,) + nxt

        lax.fori_loop(0, nb, qblock, (leftmost(0),) + opening_of(0, 0))
        dqg_ref[...] = jnp.zeros_like(dqg_ref)
        dkg_ref[...] = jnp.zeros_like(dkg_ref)

        def epilogue(j, carry):
            rows = _sb_rows(j)
            _, vjp_q = jax.vjp(fn_q, q_ref[rows, :], qg_ref[...])
            dq, dqg = vjp_q(dqn_ref[rows, :])
            _, vjp_k = jax.vjp(fn_k, k_ref[rows, :], kg_ref[...])
            dk, dkg = vjp_k(jnp.where(lane_lo, dkn_ref[_sb_both(j, 0), :], dkn_ref[_sb_both(j, 1), :]))
            dq_ref[rows, :] = dq.astype(BF16)
            dk_ref[rows, :] = dk.astype(BF16)
            dv_ref[rows, :] = jnp.where(lane_lo, dvs_ref[_sb_both(j, 0), :], dvs_ref[_sb_both(j, 1), :]).astype(BF16)
            dqg_ref[0] += dqg
            dkg_ref[0] += dkg
            return carry

        lax.fori_loop(0, nb, epilogue, 0)

    width = SB_HEADS * SB_DH
    pair = pl.BlockSpec((S, SB_PAIR), lambda p: (0, p))
    dgain = pl.BlockSpec((1, 1, SB_PAIR), lambda p: (p, 0, 0))
    in_specs = _sb_specs(S) + [pl.BlockSpec((2, nb, nb, SB_BLK), lambda p: (p, 0, 0, 0)), pair]
    return _hosted_call(
        kern, grid=(n_pairs,), in_specs=in_specs, out_specs=[pair, pair, pair, dgain, dgain],
        out_shape=[jax.ShapeDtypeStruct((S, width), BF16)] * 3 + [jax.ShapeDtypeStruct((n_pairs, 1, SB_PAIR), F32)] * 2,
        scratch_shapes=[pltpu.VMEM((S, SB_PAIR), BF16), pltpu.VMEM((2 * S, SB_PAIR), BF16), pltpu.VMEM((SB_PAIR, 2 * S), BF16),
                        pltpu.VMEM((2 * S, SB_PAIR), BF16), pltpu.VMEM((S, SB_PAIR), F32),
                        pltpu.VMEM((2 * S, SB_PAIR), F32), pltpu.VMEM((2 * S, SB_PAIR), F32)],
        args=[proj, proj, proj, qg2, kg2, rs, do], name=name, comm=comm, sem=("parallel",))


def _loss_head(y, target, *, name):
    S, D = y.shape
    T = min(ROW_TILE, S)

    def kern(y_ref, t_ref, dy_ref, acc_ref):
        err = y_ref[...] - t_ref[...]
        dy_ref[...] = err * (1.0 / D)
        col = jnp.sum(err * err, axis=0, keepdims=True)
        part = sum(col[:, k * 128:(k + 1) * 128] for k in range(D // 128))

        @pl.when(pl.program_id(0) == 0)
        def _():
            acc_ref[...] = part

        @pl.when(pl.program_id(0) > 0)
        def _():
            acc_ref[...] += part

    tile = pl.BlockSpec((T, D), lambda i: (i, 0))
    return pl.pallas_call(
        kern, grid=(S // T,), in_specs=[tile, tile], out_specs=[tile, pl.BlockSpec((1, 128), lambda i: (0, 0))],
        out_shape=[jax.ShapeDtypeStruct((S, D), F32), jax.ShapeDtypeStruct((1, 128), F32)],
        name=name, compiler_params=_cparams(("arbitrary",)))(y, target)


def _adamw_math(w, g, m, v):
    m = ADAM_B1 * m + (1.0 - ADAM_B1) * g
    v = ADAM_B2 * v + (1.0 - ADAM_B2) * jnp.square(g)
    m_hat = m / (1.0 - ADAM_B1 ** ADAM_STEP)
    v_hat = v / (1.0 - ADAM_B2 ** ADAM_STEP)
    return -ADAM_LR * (m_hat / (jnp.sqrt(v_hat) + ADAM_EPS) + ADAM_WD * w), m, v


def _adamw(w, g, m, v, *, name):
    R, C = w.shape
    T = _pick(R, (256, 128, 64, 32, 16, 8))

    def kern(w_ref, g_ref, m_ref, v_ref, d_ref, mo_ref, vo_ref):
        d, mn, vn = _adamw_math(w_ref[...], g_ref[...], m_ref[...], v_ref[...])
        d_ref[...] = d
        mo_ref[...] = mn
        vo_ref[...] = vn

    tile = pl.BlockSpec((T, C), lambda i: (i, 0))
    return pl.pallas_call(
        kern, grid=(R // T,), in_specs=[tile] * 4, out_specs=[tile] * 3,
        out_shape=[jax.ShapeDtypeStruct((R, C), F32)] * 3, name=name,
        compiler_params=_cparams(("parallel",)))(w, g, m, v)


def _adamw_layer(w, g, m, v, layer, prev, *, name, comm=None):
    L, R, C = w.shape
    T = _pick(R, (256, 128, 64, 32, 16, 8))

    def kern(w_ref, g_ref, m_ref, v_ref, *rest):
        go_ref, d_ref, mo_ref, vo_ref = rest[-4:]
        grad = g_ref[...]
        d, mn, vn = _adamw_math(w_ref[...], grad, m_ref[...], v_ref[...])
        go_ref[...] = grad
        d_ref[...] = d
        mo_ref[...] = mn
        vo_ref[...] = vn

    layer_tile = pl.BlockSpec((None, T, C), lambda i: (layer, i, 0))
    in_specs = [layer_tile, pl.BlockSpec((T, C), lambda i: (i, 0)), layer_tile, layer_tile]
    args, aliases = [w, g, m, v], {}
    if prev is not None:
        in_specs += [pl.BlockSpec(memory_space=pl.ANY)] * 4
        args += list(prev)
        aliases = {4 + k: k for k in range(4)}
    if comm is not None:
        assert prev is None
        return _hosted_call(kern, grid=(R // T,), in_specs=in_specs, out_specs=[layer_tile] * 4,
                            out_shape=[jax.ShapeDtypeStruct((L, R, C), F32)] * 4, scratch_shapes=[], args=args,
                            name=name, comm=comm)
    return pl.pallas_call(
        kern, grid=(R // T,), in_specs=in_specs, out_specs=[layer_tile] * 4,
        out_shape=[jax.ShapeDtypeStruct((L, R, C), F32)] * 4, input_output_aliases=aliases, name=name,
        compiler_params=_cparams(("parallel",)))(*args)


def _sum8(g, *, name):
    def kern(g_ref, o_ref):
        acc = g_ref[0]
        for d in range(1, g.shape[0]):
            acc = acc + g_ref[d]
        o_ref[...] = acc

    return pl.pallas_call(kern, out_shape=jax.ShapeDtypeStruct(g.shape[1:], F32), name=name,
                          compiler_params=_cparams())(g)


def _place():
    return lax.axis_index("x"), lax.axis_index("y"), lax.axis_index("c")


def _other_chips(x, y):
    return [(1 - x, y), (x, 1 - y), (1 - x, 1 - y)]


def _remote(src, dst, send_sems, recv_sems, k, to):
    return pltpu.make_async_remote_copy(src_ref=src, dst_ref=dst, send_sem=send_sems.at[k], recv_sem=recv_sems.at[k],
                                        device_id=to, device_id_type=MESH)


def _all_gather_small(v, *, name):
    def body(x_ref, out_ref, send_sems, recv_sems, local_sem):
        x, y, c = _place()
        me = 4 * x + 2 * y + c
        mine = pltpu.make_async_copy(x_ref, out_ref.at[me], local_sem)
        mine.start()
        peers = []
        for f in range(1, 8):
            peers.append((1 - x if f & 4 else x, 1 - y if f & 2 else y, 1 - c if f & 1 else c))
        sends = [_remote(x_ref, out_ref.at[me], send_sems, recv_sems, k, p) for k, p in enumerate(peers)]
        for cp in sends:
            cp.start()
        for k, (px, py, pc) in enumerate(peers):
            _remote(x_ref, out_ref.at[4 * px + 2 * py + pc], send_sems, recv_sems, k, (px, py, pc)).wait_recv()
        for cp in sends:
            cp.wait_send()
        mine.wait()

    return pl.pallas_call(
        body, out_shape=jax.ShapeDtypeStruct((8,) + v.shape, v.dtype),
        in_specs=[pl.BlockSpec(memory_space=pltpu.VMEM)], out_specs=pl.BlockSpec(memory_space=pltpu.VMEM),
        scratch_shapes=[pltpu.SemaphoreType.DMA((7,)), pltpu.SemaphoreType.DMA((7,)), pltpu.SemaphoreType.DMA],
        name=name, compiler_params=_cparams())(v)


def _hosted_call(kern, *, grid, in_specs, out_specs, out_shape, scratch_shapes, args, name, comm=None, sem=None):
    if comm is None:
        res = pl.pallas_call(kern, grid=grid, in_specs=in_specs, out_specs=out_specs, out_shape=out_shape,
                             scratch_shapes=scratch_shapes, name=name, compiler_params=_cparams(sem))(*args)
        return list(res), []
    n_in, n_out, n_scr = len(in_specs), len(out_specs), len(scratch_shapes)
    c_in, c_out = len(comm.inputs), len(comm.out_shapes)

    def body(*refs):
        ins, ci = refs[:n_in], refs[n_in:n_in + c_in]
        outs = refs[n_in + c_in:n_in + c_in + n_out]
        co = refs[n_in + c_in + n_out:n_in + c_in + n_out + c_out]
        scr = refs[n_in + c_in + n_out + c_out:n_in + c_in + n_out + c_out + n_scr]
        cs = refs[n_in + c_in + n_out + c_out + n_scr:]
        ids = [pl.program_id(d) for d in range(len(grid))]
        inner_first = functools.reduce(jnp.logical_and, [i == 0 for i in ids[1:]], True)
        inner_last = functools.reduce(jnp.logical_and, [i == n - 1 for i, n in zip(ids[1:], grid[1:])], True)

        @pl.when(jnp.logical_and(ids[0] == 0, inner_first))
        def _():
            comm.begin(ci, co, cs)

        kern(*ins, *outs, *scr)

        @pl.when(jnp.logical_and(ids[0] == grid[0] // 2, inner_last))
        def _():
            comm.middle(ci, co, cs)

        @pl.when(jnp.logical_and(ids[0] == grid[0] - 1, inner_last))
        def _():
            comm.end(ci, co, cs)

    hbm = pl.BlockSpec(memory_space=pltpu.HBM)
    res = pl.pallas_call(
        body, grid=grid, in_specs=list(in_specs) + [hbm] * c_in, out_specs=list(out_specs) + [hbm] * c_out,
        out_shape=list(out_shape) + list(comm.out_shapes), scratch_shapes=list(scratch_shapes) + list(comm.scratch),
        input_output_aliases={n_in + i: n_out + o for i, o in comm.aliases.items()},
        name=name, compiler_params=_cparams(("arbitrary",) * len(grid)))(*args, *comm.inputs)
    return list(res[:n_out]), list(res[n_out:])


def _run_comm(comm, *, name):
    return _hosted_call(lambda: None, grid=(1,), in_specs=[], out_specs=[], out_shape=[], scratch_shapes=[], args=[],
                        name=name, comm=comm)[1]


class _Gather:
    def __init__(self, shards, kinds, items):
        used = sorted({w for w, _ in items})
        self.slot = {w: k for k, w in enumerate(used)}
        self.inputs = [shards[w] for w in used]
        self.items, self.kinds = list(items), kinds
        self.shapes = {w: shards[w].shape[1:] for w in used}
        self.out_shapes = [jax.ShapeDtypeStruct((r, 4 * n) if kinds[w] == "col" else (4 * r, n), shards[w].dtype)
                           for w, _ in items for r, n in [self.shapes[w]]]
        n_items = len(items)
        self.scratch = [pltpu.SemaphoreType.DMA((6 * n_items,)), pltpu.SemaphoreType.DMA((6 * n_items,)),
                        pltpu.SemaphoreType.DMA((n_items,))]
        self.aliases = {}

    def _piece(self, ref, w, qq, half):
        r, n = self.shapes[w]
        h = r // 2
        lo, size = (0, r) if half is None else (half * h, h)
        if self.kinds[w] == "col":
            return ref.at[pl.ds(pl.multiple_of(lo, 16), size), pl.ds(pl.multiple_of(qq * n, 128), n)]
        return ref.at[pl.ds(pl.multiple_of(qq * r + lo, 16), size), :]

    def _mine(self, ci, w, l, half):
        h = self.shapes[w][0] // 2
        return ci[self.slot[w]].at[l, pl.ds(pl.multiple_of(half * h, 16), h), :]

    def begin(self, ci, co, cs):
        send_sems, recv_sems, local_sems = cs
        x, y, c = _place()
        q = 2 * x + y
        for k, (w, l) in enumerate(self.items):
            pltpu.make_async_copy(ci[self.slot[w]].at[l], self._piece(co[k], w, q, None), local_sems.at[k]).start()
            for j, (cx, cy) in enumerate(_other_chips(x, y)):
                _remote(self._mine(ci, w, l, c), self._piece(co[k], w, q, c), send_sems, recv_sems, 6 * k + j,
                        (cx, cy, c)).start()

    def middle(self, ci, co, cs):
        send_sems, recv_sems, _ = cs
        x, y, c = _place()
        for k, (w, l) in enumerate(self.items):
            for j, (cx, cy) in enumerate(_other_chips(x, y)):
                win = self._piece(co[k], w, 2 * cx + cy, c)
                _remote(win, win, send_sems, recv_sems, 6 * k + j, (cx, cy, c)).wait_recv()
                _remote(win, win, send_sems, recv_sems, 6 * k + 3 + j, (x, y, 1 - c)).start()

    def end(self, ci, co, cs):
        send_sems, recv_sems, local_sems = cs
        x, y, c = _place()
        q = 2 * x + y
        for k, (w, l) in enumerate(self.items):
            for j, (cx, cy) in enumerate(_other_chips(x, y)):
                win = self._piece(co[k], w, 2 * cx + cy, 1 - c)
                _remote(win, win, send_sems, recv_sems, 6 * k + 3 + j, (x, y, 1 - c)).wait_recv()
        for k, (w, l) in enumerate(self.items):
            for j, (cx, cy) in enumerate(_other_chips(x, y)):
                _remote(self._mine(ci, w, l, c), self._piece(co[k], w, q, c), send_sems, recv_sems, 6 * k + j,
                        (cx, cy, c)).wait_send()
                win = self._piece(co[k], w, 2 * cx + cy, c)
                _remote(win, win, send_sems, recv_sems, 6 * k + 3 + j, (x, y, 1 - c)).wait_send()
            pltpu.make_async_copy(ci[self.slot[w]].at[l], self._piece(co[k], w, q, None), local_sems.at[k]).wait()


def _half_rows(ref, half, h):
    return ref.at[:, pl.ds(pl.multiple_of(half * h, 16), h), :]


class _Copies:
    def __init__(self, inputs, out_shapes, count, pairs, aliases=None, lands=None):
        self.inputs, self.out_shapes, self.pairs, self.lands = list(inputs), list(out_shapes), pairs, lands
        self.scratch = [pltpu.SemaphoreType.DMA((count,)), pltpu.SemaphoreType.DMA((count,))]
        self.aliases = aliases or {}

    def _copies(self, ci, co, cs):
        x, y, c = _place()
        return [_remote(src, dst, cs[0], cs[1], k, to) for k, (src, dst, to) in enumerate(self.pairs(ci, co, x, y, c))]

    def begin(self, ci, co, cs):
        for cp in self._copies(ci, co, cs):
            cp.start()

    def middle(self, ci, co, cs):
        pass

    def end(self, ci, co, cs):
        x, y, c = _place()
        for k, (src, dst, to) in enumerate(self.pairs(ci, co, x, y, c)):
            _remote(src, dst, cs[0], cs[1], k, to).wait_send()
            arrival = dst if self.lands is None else self.lands(co, x, y, c)[k]
            _remote(src, arrival, cs[0], cs[1], k, to).wait_recv()


class _Together:
    def __init__(self, progs):
        self.progs = progs
        self.inputs = [a for p in progs for a in p.inputs]
        self.out_shapes = [o for p in progs for o in p.out_shapes]
        self.scratch = [t for p in progs for t in p.scratch]
        self.aliases, n_in, n_out = {}, 0, 0
        for p in progs:
            self.aliases.update({n_in + i: n_out + o for i, o in p.aliases.items()})
            n_in, n_out = n_in + len(p.inputs), n_out + len(p.out_shapes)

    def _each(self, ci, co, cs):
        i = o = t = 0
        for p in self.progs:
            ni, no, nt = len(p.inputs), len(p.out_shapes), len(p.scratch)
            yield p, ci[i:i + ni], co[o:o + no], cs[t:t + nt]
            i, o, t = i + ni, o + no, t + nt

    def begin(self, ci, co, cs):
        for p, a, b, c in self._each(ci, co, cs):
            p.begin(a, b, c)

    def middle(self, ci, co, cs):
        for p, a, b, c in self._each(ci, co, cs):
            p.middle(a, b, c)

    def end(self, ci, co, cs):
        for p, a, b, c in self._each(ci, co, cs):
            p.end(a, b, c)

    def split(self, results):
        out, o = [], 0
        for p in self.progs:
            out.append(results[o:o + len(p.out_shapes)])
            o += len(p.out_shapes)
        return out


def _send_to_all(v):
    def peers(x, y, c):
        return [(1 - x if f & 4 else x, 1 - y if f & 2 else y, 1 - c if f & 1 else c) for f in range(1, 8)]

    def pairs(ci, co, x, y, c):
        return [(ci[0], co[0].at[4 * x + 2 * y + c], peer) for peer in peers(x, y, c)]

    def lands(co, x, y, c):
        return [co[0].at[4 * px + 2 * py + pc] for px, py, pc in peers(x, y, c)]

    return _Copies([v], [jax.ShapeDtypeStruct((8,) + v.shape, v.dtype)], 7, pairs, lands=lands)


def _swap_halves(gs):
    def pairs(ci, co, x, y, c):
        return [(_half_rows(ci[k], 1 - c, g.shape[1] // 2), co[k], (x, y, 1 - c)) for k, g in enumerate(gs)]

    return _Copies(gs, [jax.ShapeDtypeStruct((g.shape[0], g.shape[1] // 2, g.shape[2]), g.dtype) for g in gs],
                   len(gs), pairs)


def _scatter_quarters(ps, kinds):
    part = [((p.shape[1], p.shape[2] // 4) if kind == "col" else (p.shape[1], p.shape[2])) for p, kind in zip(ps, kinds)]

    def pairs(ci, co, x, y, c):
        out = []
        for k, kind in enumerate(kinds):
            n = part[k][1]
            for j, (cx, cy) in enumerate(_other_chips(x, y)):
                qj = 2 * cx + cy
                src = ci[k].at[0, :, pl.ds(pl.multiple_of(qj * n, 128), n)] if kind == "col" else ci[k].at[qj]
                out.append((src, co[k].at[j], (cx, cy, c)))
        return out

    return _Copies(ps, [jax.ShapeDtypeStruct((3,) + pt, p.dtype) for pt, p in zip(part, ps)], 3 * len(ps), pairs)


def _share_halves(gs):
    def rows(co, k, half):
        h = gs[k].shape[0] // 2
        return co[k].at[pl.ds(pl.multiple_of(half * h, 16), h), :]

    def pairs(ci, co, x, y, c):
        return [(rows(co, k, c), rows(co, k, c), (x, y, 1 - c)) for k in range(len(gs))]

    def lands(co, x, y, c):
        return [rows(co, k, 1 - c) for k in range(len(gs))]

    return _Copies(gs, [jax.ShapeDtypeStruct(g.shape, g.dtype) for g in gs], len(gs), pairs,
                   aliases={k: k for k in range(len(gs))}, lands=lands)


def _wide_tile(n):
    return _pick(n, (2048, 1920, 1024, 512, 256, 128))


def _pair_sum(g, land, place, *, name):
    B, R, N = g.shape
    h = R // 2
    tr, tc = _pick(h, (256, 128)), _wide_tile(N)

    def kern(place_ref, g_ref, l_ref, o_ref):
        o_ref[...] = (g_ref[...] + l_ref[...]).astype(o_ref.dtype)

    grid_spec = pltpu.PrefetchScalarGridSpec(
        num_scalar_prefetch=1, grid=(B, h // tr, N // tc),
        in_specs=[pl.BlockSpec((None, tr, tc), lambda b, i, j, p: (b, p[1] * (h // tr) + i, j)),
                  pl.BlockSpec((None, tr, tc), lambda b, i, j, p: (b, i, j))],
        out_specs=pl.BlockSpec((None, tr, tc), lambda b, i, j, p: (b, i, j)))
    return pl.pallas_call(kern, grid_spec=grid_spec, out_shape=jax.ShapeDtypeStruct((B, h, N), BF16), name=name,
                          compiler_params=_cparams(("parallel", "parallel", "parallel")))(place, g, land)


def _quarter_sum(p, land, kind, shard_shape, place, *, name):
    L, r, n = shard_shape
    h = r // 2
    tr, tc = _pick(h, (256, 128)), _wide_tile(n)

    def kern(place_ref, p_ref, a_ref, b_ref, c_ref, o_ref):
        o_ref[...] = ((p_ref[...].astype(F32) + a_ref[...].astype(F32)) + b_ref[...].astype(F32)) + c_ref[...].astype(F32)

    if kind == "col":
        p_spec = pl.BlockSpec((None, tr, tc), lambda l, i, j, pr: (l, i, pr[0] * (n // tc) + j))
    else:
        p_spec = pl.BlockSpec((None, None, tr, tc), lambda l, i, j, pr: (l, pr[0], i, j))
    lands = [pl.BlockSpec((None, None, tr, tc), functools.partial(lambda l, i, j, pr, s: (s, l, i, j), s=s))
             for s in range(3)]
    grid_spec = pltpu.PrefetchScalarGridSpec(
        num_scalar_prefetch=1, grid=(L, h // tr, n // tc), in_specs=[p_spec] + lands,
        out_specs=pl.BlockSpec((None, tr, tc), lambda l, i, j, pr: (l, pr[1] * (h // tr) + i, j)))
    return pl.pallas_call(kern, grid_spec=grid_spec, out_shape=jax.ShapeDtypeStruct((L, r, n), F32), name=name,
                          compiler_params=_cparams(("parallel", "parallel", "parallel")))(place, p, land, land, land)


class _ReduceScatter:
    def __init__(self, grads, kinds, shard_shapes, place, tag):
        self.kinds, self.shapes, self.place, self.tag = kinds, shard_shapes, place, tag
        self.g3 = [g[None] if kind == "col" else g.reshape(4, g.shape[0] // 4, g.shape[1]) for g, kind in zip(grads, kinds)]

    def swap(self):
        return _swap_halves(self.g3)

    def pair_sums(self, lands):
        self.ps = [_pair_sum(g, land, self.place, name=f"rs_pair_sum_{self.tag}_{k}")
                   for k, (g, land) in enumerate(zip(self.g3, lands))]

    def scatter(self):
        return _scatter_quarters(self.ps, self.kinds)

    def quarter_sums(self, parts):
        self.halves = []
        for k, (p, part) in enumerate(zip(self.ps, parts)):
            p4 = p if self.kinds[k] == "col" else p[None]
            out = _quarter_sum(p4, part[:, None], self.kinds[k], (1,) + tuple(self.shapes[k]), self.place,
                               name=f"rs_quarter_sum_{self.tag}_{k}")
            self.halves.append(out[0])

    def share(self):
        return _share_halves(self.halves)

    def run(self):
        self.pair_sums(_run_comm(self.swap(), name=f"rs_swap_{self.tag}"))
        self.quarter_sums(_run_comm(self.scatter(), name=f"rs_scatter_{self.tag}"))
        return _run_comm(self.share(), name=f"rs_share_{self.tag}")


_WEIGHTS = ["mod_w", "mod_b", "norm1_g", "w_in", "gate_b", "conv_w", "conv_b", "conv_ln_g", "conv_ln_b", "w_conv_proj",
            "hgrn_lb", "hgrn_norm_g", "w_hgrn_proj", "sb_qn_g", "sb_kn_g", "w_sb_proj", "w_out", "norm2_g", "mlp_w1",
            "mlp_w2"]
_BIG = [("w_in", "col"), ("w_conv_proj", "col"), ("w_hgrn_proj", "col"), ("w_sb_proj", "col"), ("w_out", "row"),
        ("mlp_w1", "col"), ("mlp_w2", "row")]
_REPLICATED = ["mod_b", "norm1_g", "gate_b", "conv_b", "conv_ln_g", "conv_ln_b", "hgrn_lb", "hgrn_norm_g", "sb_qn_g",
               "sb_kn_g", "norm2_g"]
LANES = 128


class _Pack:
    def __init__(self, items):
        self.shapes = {n: a.shape for n, a in items}
        self.offsets, pos = {}, 0
        for n, a in items:
            self.offsets[n] = pos
            pos += math.prod(a.shape)
        self.rows = -(-pos // (8 * LANES)) * 8
        flat = jnp.concatenate([a.reshape(-1).astype(F32) for _, a in items])
        self.array = jnp.pad(flat, (0, self.rows * LANES - pos)).reshape(self.rows, LANES)

    def get(self, packed, name):
        lead = packed.shape[:-2]
        flat = packed.reshape(lead + (self.rows * LANES,))
        n = math.prod(self.shapes[name])
        return lax.slice_in_dim(flat, self.offsets[name], self.offsets[name] + n, axis=len(lead)).reshape(
            lead + self.shapes[name])


def _lower_bounds(hgrn_lb):
    p = jax.nn.softmax(hgrn_lb.astype(F32), axis=0)
    return jnp.cumsum(p, axis=0) - p[0:1]


def _layer_fwd(x, w, p, l, comms=(None, None)):
    S, D = x.shape
    r = {"x": x}
    (r["h"],) = _rowop(_fn_normmod, [(x, 0, D)], [p["n1g"], p["sc1"], p["sh1"]], [(D, BF16)], name=f"normmod1_fwd_{l}")
    proj = r["proj"] = _matmul(r["h"], w["w_in", l], name=f"w_in_fwd_{l}")
    r["cpre"] = _conv_fwd(proj, p["w32"], p["conv_b"], name=f"conv_fwd_{l}")
    (r["cact"],) = _rowop(_fn_lnsilu, [(r["cpre"], 0, CONV_CH)], [p["lng"], p["lnb"]], [(CONV_CH, BF16)],
                          name=f"conv_ln_fwd_{l}")
    arrived = lambda comm, got: w.update({(_BIG[k][0], layer): arr for (k, layer), arr in zip(comm.items, got)})
    (r["hg"], r["states"]), got = _hgrn_fwd(proj, p["lbk"], p["ng"], name=f"hgrn_fwd_{l}", comm=comms[0])
    if comms[0] is not None:
        arrived(comms[0], got)
    (r["sb"], r["rs"]), got = _sb_fwd(proj, p["qg"], p["kg"], name=f"sb_fwd_{l}", comm=comms[1])
    if comms[1] is not None:
        arrived(comms[1], got)
    r["y_c"] = _matmul(r["cact"], w["w_conv_proj", l], name=f"w_conv_proj_fwd_{l}")
    r["y_h"] = _matmul(r["hg"], w["w_hgrn_proj", l], name=f"w_hgrn_proj_fwd_{l}")
    r["y_s"] = _matmul(r["sb"], w["w_sb_proj", l], name=f"w_sb_proj_fwd_{l}")
    (r["merged"],) = _rowop(_fn_merge, [(proj, OFF_GL, 3 * D), (r["y_c"], 0, D), (r["y_h"], 0, D), (r["y_s"], 0, D)],
                            [p["gate_b"]], [(D, BF16)], name=f"merge_fwd_{l}")
    resid = lambda y, x_in, gate: (y,) + _fn_resid(x_in, y, gate)
    r["a_out"], r["x1"] = _matmul(r["merged"], w["w_out", l], name=f"w_out_fwd_{l}", post=resid, extras=[x],
                                  rows=[p["g1"]], out_dtypes=(F32, F32))
    (r["h2"],) = _rowop(_fn_normmod, [(r["x1"], 0, D)], [p["n2g"], p["sc2"], p["sh2"]], [(D, BF16)],
                        name=f"normmod2_fwd_{l}")
    r["u"], r["act"] = _matmul(r["h2"], w["mlp_w1", l], name=f"mlp_w1_fwd_{l}", post=lambda u: (u,) + _fn_relu2(u),
                               out_dtypes=(F32, BF16))
    r["m_out"], x2 = _matmul(r["act"], w["mlp_w2", l], name=f"mlp_w2_fwd_{l}", post=resid, extras=[r["x1"]],
                             rows=[p["g2"]], out_dtypes=(F32, F32))
    return x2, r


def _layer_bwd(dx2, r, w, p, l, grads, carry=None, last=None):
    S, D = dx2.shape
    small = {}

    def dweight(name, a, dy):
        grads[name, l] = _matmul(a, dy, ta=True, name=f"{name}_dw_{l}")

    stage = (lambda k, got: carry(k, got)) if carry is not None else (lambda k, got: None)

    (dm_out,), (dg2,) = _rowop_bwd(_fn_scale, [(r["m_out"], 0, D)], [p["g2"]], [dx2], [BF16], name=f"resid2_bwd_{l}")
    (du,) = _matmul(dm_out, w["mlp_w2", l], tb=True, name=f"mlp_w2_dx_{l}", extras=[r["u"]], out_dtypes=(BF16,),
                    post=lambda dact, u: (dact * (2.0 * jnp.maximum(u, 0.0)),))
    dweight("mlp_w2", r["act"], dm_out)
    dh2 = _matmul(du, w["mlp_w1", l], tb=True, name=f"mlp_w1_dx_{l}")
    dweight("mlp_w1", r["h2"], du)
    (dx1, da_out), (dg1, small["norm2_g"], dsc2, dsh2) = _rowop_bwd(
        _fn_resid_norm, [(r["x"], 0, D), (r["a_out"], 0, D)], [p["g1"], p["n2g"], p["sc2"], p["sh2"]], [dx2, dh2],
        [F32, BF16], name=f"resid1_norm2_bwd_{l}")
    dmerged = _matmul(da_out, w["w_out", l], tb=True, name=f"w_out_dx_{l}")
    dweight("w_out", r["merged"], da_out)
    (dgl, dy_c, dy_h, dy_s), (small["gate_b"],) = _rowop_bwd(
        _fn_merge, [(r["proj"], OFF_GL, 3 * D), (r["y_c"], 0, D), (r["y_h"], 0, D), (r["y_s"], 0, D)], [p["gate_b"]],
        [dmerged], [BF16] * 4, name=f"merge_bwd_{l}")
    dweight("w_conv_proj", r["cact"], dy_c)
    dweight("w_hgrn_proj", r["hg"], dy_h)
    dweight("w_sb_proj", r["sb"], dy_s)
    dcact = _matmul(dy_c, w["w_conv_proj", l], tb=True, name=f"w_conv_proj_dx_{l}")
    (dcpre,), (small["conv_ln_g"], small["conv_ln_b"]) = _rowop_bwd(
        _fn_lnsilu, [(r["cpre"], 0, CONV_CH)], [p["lng"], p["lnb"]], [dcact], [F32], name=f"conv_ln_bwd_{l}")
    (d_conv, dw32, small["conv_b"]), got = _conv_bwd(r["proj"], dcpre, p["w32"], name=f"conv_bwd_{l}",
                                                      comm=stage(0, None))
    small["conv_w"] = dw32[:CONV_WIDTH]
    dhg = _matmul(dy_h, w["w_hgrn_proj", l], tb=True, out_dtype=BF16, name=f"w_hgrn_proj_dx_{l}")
    (dq, df, di, dg, dlbk, dng), got = _hgrn_bwd(r["proj"], r["states"], dhg, p["lbk"], p["ng"], name=f"hgrn_bwd_{l}",
                                                 comm=stage(1, got))
    small["lower"] = -dlbk
    small["hgrn_norm_g"] = jnp.sum(dng, axis=0)
    dsb = _matmul(dy_s, w["w_sb_proj", l], tb=True, out_dtype=BF16, name=f"w_sb_proj_dx_{l}")
    (dsq, dsk, dsv, dqg, dkg), got = _sb_bwd(r["proj"], p["qg"], p["kg"], r["rs"], dsb, name=f"sb_bwd_{l}",
                                             comm=stage(2, got))
    stage(3, got)
    fold = lambda t: jnp.sum(t.reshape(-1, SB_DH), axis=0, keepdims=True)
    small["sb_qn_g"], small["sb_kn_g"] = fold(dqg), fold(dkg)
    dproj = jnp.concatenate([d_conv, dq, df, di, dg, dsq, dsk, dsv, dgl], axis=1)
    dweight("w_in", r["h"], dproj)
    norm1 = functools.partial(_rowop_bwd, _fn_normmod, [(r["x"], 0, D)], [p["n1g"], p["sc1"], p["sh1"]],
                              din_dtypes=[F32], add={0: dx1}, name=f"normmod1_bwd_{l}")
    if last is None:
        dh = _matmul(dproj, w["w_in", l], tb=True, name=f"w_in_dx_{l}")
        (dx,), (small["norm1_g"], dsc1, dsh1) = norm1(douts=[dh])
    else:
        dh, got = _matmul(dproj, w["w_in", l], tb=True, name=f"w_in_dx_{l}", comm=last(0, None))
        (dx,), (small["norm1_g"], dsc1, dsh1), got = norm1(douts=[dh], comm=last(1, got))
        last(2, got)
    small["mod"] = jnp.concatenate([dsh1, dsc1, dg1, dsh2, dsc2, dg2], axis=1)
    return dx, small


def kernel(x, c, mod_w, mod_b, norm1_g, w_in, gate_b, conv_w, conv_b, conv_ln_g, conv_ln_b, w_conv_proj, hgrn_lb, hgrn_norm_g, w_hgrn_proj, sb_qn_g, sb_kn_g, w_sb_proj, w_out, norm2_g, mlp_w1, mlp_w2, loss_target, m_mod_w, m_mod_b, m_norm1_g, m_w_in, m_gate_b, m_conv_w, m_conv_b, m_conv_ln_g, m_conv_ln_b, m_w_conv_proj, m_hgrn_lb, m_hgrn_norm_g, m_w_hgrn_proj, m_sb_qn_g, m_sb_kn_g, m_w_sb_proj, m_w_out, m_norm2_g, m_mlp_w1, m_mlp_w2, v_mod_w, v_mod_b, v_norm1_g, v_w_in, v_gate_b, v_conv_w, v_conv_b, v_conv_ln_g, v_conv_ln_b, v_w_conv_proj, v_hgrn_lb, v_hgrn_norm_g, v_w_hgrn_proj, v_sb_qn_g, v_sb_kn_g, v_w_sb_proj, v_w_out, v_norm2_g, v_mlp_w1, v_mlp_w2):
    given = dict(locals())
    wts = {n: given[n] for n in _WEIGHTS}
    mom = {n: given["m_" + n] for n in _WEIGHTS}
    var = {n: given["v_" + n] for n in _WEIGHTS}
    n_layers, D = norm1_g.shape
    xi, yi, ci = _place()
    q = 2 * xi + yi
    me = 4 * xi + 2 * yi + ci
    place = jnp.stack([q, ci]).astype(jnp.int32)
    n_mod = mod_w.shape[2]
    cw = conv_w.shape[2]

    pk1 = _Pack([("c", c), ("conv_w", conv_w)])
    got1 = _all_gather_small(pk1.array, name="gather_cond")
    c_act = jax.nn.silu(pk1.get(got1, "c")[:, 0, :])
    conv_full = jnp.concatenate([pk1.get(got1, "conv_w")[2 * k] for k in range(4)], axis=-1)

    mod_cols = []
    for l in range(n_layers):
        mb = lax.dynamic_slice_in_dim(mod_b[l], q * n_mod, n_mod)
        mod_cols.append(_matmul(c_act, mod_w, bl=l, name=f"mod_fwd_{l}") + mb[None, :])
    got2 = _all_gather_small(jnp.concatenate(mod_cols, axis=0), name="gather_mod")
    mods = []
    for l in range(n_layers):
        row = lax.dynamic_index_in_dim(got2[0::2], l * 8 + me, axis=1, keepdims=False)
        mods.append(jnp.split(row.reshape(1, 4 * n_mod), 6, axis=1))

    lower, lower_vjp = jax.vjp(_lower_bounds, hgrn_lb)

    shards = [wts[n].astype(BF16) for n, _ in _BIG]
    kinds = [k for _, k in _BIG]
    index = {n: k for k, (n, _) in enumerate(_BIG)}
    first = ["w_in", "w_conv_proj", "w_hgrn_proj", "w_sb_proj"]

    def gather(*names_layers):
        items = [(index[n], l) for names, l in names_layers for n in names if l < n_layers]
        return _Gather(shards, kinds, items) if items else None

    start = gather((first[:1], 0))
    w = {(_BIG[k][0], layer): arr
         for (k, layer), arr in zip(start.items, _run_comm(start, name="gather_first_weights"))}

    def layer_params(l):
        sh1, sc1, g1, sh2, sc2, g2 = mods[l]
        return dict(sh1=sh1, sc1=sc1, g1=g1, sh2=sh2, sc2=sc2, g2=g2, n1g=norm1_g[l][None], n2g=norm2_g[l][None],
                    gate_b=gate_b[l][None], conv_b=conv_b[l][None], lng=conv_ln_g[l][None], lnb=conv_ln_b[l][None],
                    w32=jnp.pad(conv_full[l], ((0, CONV_HALO - CONV_WIDTH), (0, 0))), lbk=(1.0 - lower[l])[None],
                    ng=hgrn_norm_g[l][None], qg=jnp.tile(sb_qn_g[l][None], (1, SB_PAIR // SB_DH)),
                    kg=jnp.tile(sb_kn_g[l][None], (1, SB_PAIR // SB_DH)))

    params = [layer_params(l) for l in range(n_layers)]
    act, saved = x[0], []
    for l in range(n_layers):
        comms = (gather((first[1:] if l == 0 else [], l), (["w_out", "mlp_w1"], l)),
                 gather((["mlp_w2"], l), (first, l + 1)))
        act, r = _layer_fwd(act, w, params[l], l, comms=comms)
        saved.append(r)
    dact, loss_lanes = _loss_head(act, loss_target[0], name="loss_head")

    grads, smalls, reduced = {}, [None] * n_layers, {}

    def reduce_scatter(items, tag):
        return _ReduceScatter([grads[_BIG[k][0], layer] for k, layer in items], [kinds[k] for k, _ in items],
                              [shards[k].shape[1:] for k, _ in items], place, tag)

    def carried(l):
        items_a = [(k, l + 1) for k in range(len(_BIG))]
        items_b = [(k, l) for k, (n, _) in enumerate(_BIG) if n != "w_in"]
        box = boxes.setdefault(l, {})

        def carry(stage, got):
            if stage == 0:
                box["a"], box["b"] = reduce_scatter(items_a, f"l{l + 1}"), reduce_scatter(items_b, f"l{l}")
                box["swaps"] = _Together([box["a"].swap(), box["b"].swap()])
                return box["swaps"]
            if stage == 1:
                lands_a, lands_b = box["swaps"].split(got)
                box["a"].pair_sums(lands_a)
                box["b"].pair_sums(lands_b)
                return box["a"].scatter()
            if stage == 2:
                box["a"].quarter_sums(got)
                box["both"] = _Together([box["a"].share(), box["b"].scatter()])
                return box["both"]
            done_a, parts_b = box["both"].split(got)
            reduced.update(zip(items_a, done_a))
            box["b"].quarter_sums(parts_b)
            box["b_items"] = items_b

        return carry

    def final(l):
        items = [(index["w_in"], l)]
        box = boxes.setdefault(l, {})

        def step(stage, got):
            if stage == 0:
                box["w"] = reduce_scatter(items, "w_in")
                box["w"].pair_sums(_run_comm(box["w"].swap(), name="rs_swap_w_in"))
                box["last"] = _Together([box["w"].scatter()] + ([box["b"].share()] if "b_items" in box else []))
                return box["last"]
            if stage == 1:
                parts = box["last"].split(got)
                if "b_items" in box:
                    reduced.update(zip(box["b_items"], parts[1]))
                box["w"].quarter_sums(parts[0])
                return box["w"].share()
            reduced.update(zip(items, got))

        return step

    boxes = {}

    for l in reversed(range(n_layers)):
        dact, smalls[l] = _layer_bwd(dact, saved[l], w, params[l], l, grads, carried(l) if l + 1 < n_layers else None,
                                     final(l) if l == 0 else None)
    grad_x = dact[None]
    rest = [(k, l) for l in range(n_layers) for k in range(len(_BIG)) if (k, l) not in reduced]
    if rest:
        reduced.update(zip(rest, reduce_scatter(rest, "rest").run()))

    stack = lambda k: jnp.stack([smalls[l][k] for l in range(n_layers)])
    (d_hgrn_lb,) = lower_vjp(stack("lower")[:, 0, :])
    items = [("loss", loss_lanes), ("mod", stack("mod")), ("hgrn_lb", d_hgrn_lb), ("conv_w", stack("conv_w"))]
    items += [(k, stack(k)) for k in ("norm1_g", "gate_b", "conv_b", "conv_ln_g", "conv_ln_b", "hgrn_norm_g", "sb_qn_g",
                                      "sb_kn_g", "norm2_g")]
    pk3 = _Pack(items)

    share_small = _send_to_all(pk3.array)
    delta, new_m, new_v, big, got3 = {}, {}, {}, {}, None
    for n, _ in _BIG:
        outs = None
        for l in reversed(range(n_layers)):
            args = (wts[n], reduced[index[n], l], mom[n], var[n], l, outs)
            if got3 is None:
                outs, (got3,) = _adamw_layer(*args, name=f"adamw_{n}_{l}", comm=share_small)
            else:
                outs = _adamw_layer(*args, name=f"adamw_{n}_{l}")
        big[n] = outs
    got3 = lax.dynamic_update_slice_in_dim(got3, pk3.array[None], me, axis=0)
    tot3 = _sum8(got3, name="sum_small_grads")
    loss = (0.5 / D) * jnp.sum(pk3.get(tot3, "loss"))
    g = {k: pk3.get(tot3, k).reshape(wts[k].shape) for k in _REPLICATED if k != "mod_b"}
    g["mod_b"] = pk3.get(tot3, "mod")[:, 0, :]
    g["conv_w"] = lax.dynamic_slice_in_dim(pk3.get(tot3, "conv_w"), q * cw, cw, axis=2)
    dmod_all = pk3.get(got3, "mod")[:, :, 0, :]
    g_mod_w = None
    for l in range(n_layers):
        cols = lax.dynamic_slice_in_dim(dmod_all[:, l, :], q * n_mod, n_mod, axis=1)
        g_mod_w = _matmul(c_act, cols, ta=True, layer=l, n_layers=n_layers, into=g_mod_w, name=f"mod_dw_{l}")
    g["mod_w"] = g_mod_w

    for n, _ in _BIG:
        g[n], delta[n], new_m[n], new_v[n] = big[n]
    two_d = lambda t: t.reshape(-1, t.shape[-1])
    outs = _adamw(two_d(mod_w), two_d(g["mod_w"]), two_d(m_mod_w), two_d(v_mod_w), name="adamw_mod_w")
    delta["mod_w"], new_m["mod_w"], new_v["mod_w"] = (t.reshape(mod_w.shape) for t in outs)
    rest = _REPLICATED + ["conv_w"]
    packs = [_Pack([(n, src[n]) for n in rest]) for src in (wts, g, mom, var)]
    outs = _adamw(*[pk.array for pk in packs], name="adamw_small")
    for n in rest:
        delta[n], new_m[n], new_v[n] = (packs[0].get(t, n) for t in outs)

    return (loss, grad_x, *[g[n] for n in _WEIGHTS], *[delta[n] for n in _WEIGHTS], *[new_m[n] for n in _WEIGHTS],
            *[new_v[n] for n in _WEIGHTS])
```
